```python
import jax, jax.numpy as jnp
from jax import lax
import numpy as np

D_MODEL = 2048
BATCH = 8
SEQ = 2048
DEPTH = 4

D_MIX = D_MODEL
HEAD_DIM = 128
ATTN_WIDTH = D_MIX // 2
N_ATTN_HEADS = ATTN_WIDTH // HEAD_DIM
GMLP_WIDTH = D_MIX // 4
N_GMLP_HEADS = 4
GMLP_HEAD_DIM = GMLP_WIDTH // N_GMLP_HEADS
POOL_WIDTH = D_MIX - ATTN_WIDTH - GMLP_WIDTH
N_POOL_GROUPS = 4
POOL_GROUP_DIM = POOL_WIDTH // N_POOL_GROUPS
POOL_WINDOWS = (2, 4, 8, 16)
CHUNK = 128
Q_BLOCK = 128
D_PLE = 256
D_FF = -(-8 * D_MODEL // (3 * 256)) * 256
EPS = 1e-6

PROJ_SIZES = (ATTN_WIDTH, ATTN_WIDTH, ATTN_WIDTH, N_ATTN_HEADS, GMLP_WIDTH, GMLP_WIDTH, POOL_WIDTH)
D_PROJ = 3 * ATTN_WIDTH + N_ATTN_HEADS + 2 * GMLP_WIDTH + POOL_WIDTH
SPLIT_POINTS = (
    ATTN_WIDTH,
    2 * ATTN_WIDTH,
    3 * ATTN_WIDTH,
    3 * ATTN_WIDTH + N_ATTN_HEADS,
    3 * ATTN_WIDTH + N_ATTN_HEADS + GMLP_WIDTH,
    3 * ATTN_WIDTH + N_ATTN_HEADS + 2 * GMLP_WIDTH,
)

kernel_name = "hybrid_parallel_fox_gmlp_pool_block"


def rms_norm(x, gain):
    xf = x.astype(jnp.float32)
    y = xf * lax.rsqrt(jnp.mean(xf * xf, axis=-1, keepdims=True) + EPS)
    return (y * gain.astype(jnp.float32)).astype(x.dtype)


def fox_attention(q, k, v, log_f):
    B, S, H, Dh = q.shape
    nb = S // Q_BLOCK
    c = jnp.cumsum(log_f, axis=1)
    q_blocks = q.reshape(B, nb, Q_BLOCK, H, Dh).transpose(1, 0, 3, 2, 4)
    cq_blocks = c.reshape(B, nb, Q_BLOCK, H).transpose(1, 0, 3, 2)
    pos_blocks = jnp.arange(S, dtype=jnp.int32).reshape(nb, Q_BLOCK)
    kh = k.transpose(0, 2, 1, 3)
    vh = v.transpose(0, 2, 1, 3)
    ck = c.transpose(0, 2, 1)
    kpos = jnp.arange(S, dtype=jnp.int32)
    scale = Dh ** -0.5

    def one_block(args):
        q_blk, cq_blk, qpos = args
        s = jnp.einsum('bhqd,bhkd->bhqk', q_blk, kh, preferred_element_type=jnp.float32) * scale
        s = s + (cq_blk[..., :, None] - ck[:, :, None, :])
        s = jnp.where(qpos[:, None] >= kpos[None, :], s, -jnp.inf)
        w = jax.nn.softmax(s, axis=-1)
        return jnp.einsum('bhqk,bhkd->bhqd', w.astype(vh.dtype), vh)

    out = lax.map(one_block, (q_blocks, cq_blocks, pos_blocks))
    return out.transpose(1, 0, 3, 2, 4).reshape(B, S, H * Dh)


def gmlp_mixer(u, v, v_gain, w_s, b_s):
    B, S, _ = u.shape
    nc = S // CHUNK
    u = jax.nn.gelu(u)
    v = jax.nn.gelu(v).reshape(B, S, N_GMLP_HEADS, GMLP_HEAD_DIM)
    v = rms_norm(v, v_gain).reshape(B, nc, CHUNK, N_GMLP_HEADS, GMLP_HEAD_DIM)
    w = w_s * jnp.tril(jnp.ones((CHUNK, CHUNK), w_s.dtype))[None]
    mixed = jnp.einsum('gts,bnsgc->bntgc', w, v) + b_s.T[None, None, :, :, None]
    return u * mixed.reshape(B, S, GMLP_WIDTH)


def pool_mixer(xp, w_pool, scale):
    B, S, _ = xp.shape
    x4 = xp.reshape(B, S, N_POOL_GROUPS, POOL_GROUP_DIM)
    cs = jnp.cumsum(x4.astype(jnp.float32), axis=1)
    cpad = jnp.concatenate([jnp.zeros((B, 1, N_POOL_GROUPS, POOL_GROUP_DIM), jnp.float32), cs], axis=1)
    t = jnp.arange(S, dtype=jnp.int32)[:, None]
    win = jnp.asarray(POOL_WINDOWS, dtype=jnp.int32)[None, :]
    lo = jnp.maximum(t + 1 - win, 0)
    cnt = (t + 1 - lo).astype(jnp.float32)
    g_idx = jnp.arange(N_POOL_GROUPS, dtype=jnp.int32)[None, :]
    window_sum = cs - cpad[:, lo, g_idx, :]
    d = (window_sum / cnt[None, :, :, None] - x4.astype(jnp.float32)).astype(xp.dtype)
    y = jnp.einsum('bsgc,gcd->bsgd', d, w_pool)
    return y.reshape(B, S, POOL_WIDTH) * scale


def _fwd_setup_inputs(seed: int = 0) -> dict:
    key = jax.random.key(seed)
    ks = jax.random.split(key, 20)
    f32 = jnp.float32

    def nrm(k, shape, s):
        return jax.random.normal(k, shape, f32) * s

    def gain(k, shape):
        return 1.0 + 0.05 * jax.random.normal(k, shape, f32)

    return {
        "x": jax.random.normal(ks[0], (BATCH, SEQ, D_MODEL), f32),
        "p": jax.random.normal(ks[1], (DEPTH, BATCH, SEQ, D_PLE), f32),
        "norm_mix": gain(ks[2], (DEPTH, D_MODEL)),
        "w_in": nrm(ks[3], (DEPTH, D_MODEL, D_PROJ), D_MODEL ** -0.5),
        "q_norm": gain(ks[4], (DEPTH, HEAD_DIM)),
        "k_norm": gain(ks[5], (DEPTH, HEAD_DIM)),
        "forget_bias": 3.0 + 0.5 * jax.random.normal(ks[6], (DEPTH, N_ATTN_HEADS), f32),
        "gmlp_v_norm": gain(ks[7], (DEPTH, N_GMLP_HEADS, GMLP_HEAD_DIM)),
        "gmlp_w_s": nrm(ks[8], (DEPTH, N_GMLP_HEADS, CHUNK, CHUNK), CHUNK ** -0.5),
        "gmlp_b_s": 1.0 + 0.1 * jax.random.normal(ks[9], (DEPTH, N_GMLP_HEADS, CHUNK), f32),
        "pool_w": nrm(ks[10], (DEPTH, N_POOL_GROUPS, POOL_GROUP_DIM, POOL_GROUP_DIM), POOL_GROUP_DIM ** -0.5),
        "pool_scale": 1.0 + 0.1 * jax.random.normal(ks[11], (DEPTH, POOL_WIDTH), f32),
        "w_out": nrm(ks[12], (DEPTH, D_MIX, D_MODEL), D_MIX ** -0.5),
        "norm_ffn": gain(ks[13], (DEPTH, D_MODEL)),
        "w_ffn_gate": nrm(ks[14], (DEPTH, D_MODEL, D_FF), D_MODEL ** -0.5),
        "w_ffn_up": nrm(ks[15], (DEPTH, D_MODEL, D_FF), D_MODEL ** -0.5),
        "w_ffn_down": nrm(ks[16], (DEPTH, D_FF, D_MODEL), D_FF ** -0.5),
        "norm_ple": gain(ks[17], (DEPTH, D_MODEL)),
        "w_ple_gate": nrm(ks[18], (DEPTH, D_MODEL, D_MODEL), D_MODEL ** -0.5),
        "w_ple_proj": nrm(ks[19], (DEPTH, D_PLE, D_MODEL), D_PLE ** -0.5),
    }


def _fwd_reference(x, p, norm_mix, w_in, q_norm, k_norm, forget_bias, gmlp_v_norm, gmlp_w_s, gmlp_b_s,
              pool_w, pool_scale, w_out, norm_ffn, w_ffn_gate, w_ffn_up, w_ffn_down,
              norm_ple, w_ple_gate, w_ple_proj):
    B, S, _ = x.shape
    h = x
    for i in range(DEPTH):
        xn = rms_norm(h, norm_mix[i])
        proj = xn @ w_in[i]
        q, k, v, f_logit, gu, gv, xp = jnp.split(proj, SPLIT_POINTS, axis=-1)
        q = rms_norm(q.reshape(B, S, N_ATTN_HEADS, HEAD_DIM), q_norm[i])
        k = rms_norm(k.reshape(B, S, N_ATTN_HEADS, HEAD_DIM), k_norm[i])
        v = v.reshape(B, S, N_ATTN_HEADS, HEAD_DIM)
        log_f = jax.nn.log_sigmoid((f_logit + forget_bias[i]).astype(jnp.float32))
        y_attn = fox_attention(q, k, v, log_f)
        y_gmlp = gmlp_mixer(gu, gv, gmlp_v_norm[i], gmlp_w_s[i], gmlp_b_s[i])
        y_pool = pool_mixer(xp, pool_w[i], pool_scale[i])
        mix = jnp.concatenate([y_attn, y_gmlp, y_pool], axis=-1)
        h = h + mix @ w_out[i]
        xn = rms_norm(h, norm_ffn[i])
        h = h + (jax.nn.silu(xn @ w_ffn_gate[i]) * (xn @ w_ffn_up[i])) @ w_ffn_down[i]
        gate = jax.nn.sigmoid(rms_norm(h, norm_ple[i]) @ w_ple_gate[i])
        h = h + (p[i] @ w_ple_proj[i]) * gate
    return h


import jax as _jax
import jax.numpy as _jnp

TWIN_FORMAT = 'train_step'
FWD_PARAMS = ['x', 'p', 'norm_mix', 'w_in', 'q_norm', 'k_norm', 'forget_bias', 'gmlp_v_norm', 'gmlp_w_s', 'gmlp_b_s', 'pool_w', 'pool_scale', 'w_out', 'norm_ffn', 'w_ffn_gate', 'w_ffn_up', 'w_ffn_down', 'norm_ple', 'w_ple_gate', 'w_ple_proj']
TWIN_WEIGHTS = ['norm_mix', 'w_in', 'q_norm', 'k_norm', 'forget_bias', 'gmlp_v_norm', 'gmlp_w_s', 'gmlp_b_s', 'pool_w', 'pool_scale', 'w_out', 'norm_ffn', 'w_ffn_gate', 'w_ffn_up', 'w_ffn_down', 'norm_ple', 'w_ple_gate', 'w_ple_proj']
TWIN_DIFF_INPUT = 'x'
TWIN_INPUTS = ['x', 'p', 'norm_mix', 'w_in', 'q_norm', 'k_norm', 'forget_bias', 'gmlp_v_norm', 'gmlp_w_s', 'gmlp_b_s', 'pool_w', 'pool_scale', 'w_out', 'norm_ffn', 'w_ffn_gate', 'w_ffn_up', 'w_ffn_down', 'norm_ple', 'w_ple_gate', 'w_ple_proj', 'loss_target', 'm_norm_mix', 'm_w_in', 'm_q_norm', 'm_k_norm', 'm_forget_bias', 'm_gmlp_v_norm', 'm_gmlp_w_s', 'm_gmlp_b_s', 'm_pool_w', 'm_pool_scale', 'm_w_out', 'm_norm_ffn', 'm_w_ffn_gate', 'm_w_ffn_up', 'm_w_ffn_down', 'm_norm_ple', 'm_w_ple_gate', 'm_w_ple_proj', 'v_norm_mix', 'v_w_in', 'v_q_norm', 'v_k_norm', 'v_forget_bias', 'v_gmlp_v_norm', 'v_gmlp_w_s', 'v_gmlp_b_s', 'v_pool_w', 'v_pool_scale', 'v_w_out', 'v_norm_ffn', 'v_w_ffn_gate', 'v_w_ffn_up', 'v_w_ffn_down', 'v_norm_ple', 'v_w_ple_gate', 'v_w_ple_proj']
TWIN_OUTPUTS = ['loss', 'grad_x', 'grad_norm_mix', 'grad_w_in', 'grad_q_norm', 'grad_k_norm', 'grad_forget_bias', 'grad_gmlp_v_norm', 'grad_gmlp_w_s', 'grad_gmlp_b_s', 'grad_pool_w', 'grad_pool_scale', 'grad_w_out', 'grad_norm_ffn', 'grad_w_ffn_gate', 'grad_w_ffn_up', 'grad_w_ffn_down', 'grad_norm_ple', 'grad_w_ple_gate', 'grad_w_ple_proj', 'delta_norm_mix', 'delta_w_in', 'delta_q_norm', 'delta_k_norm', 'delta_forget_bias', 'delta_gmlp_v_norm', 'delta_gmlp_w_s', 'delta_gmlp_b_s', 'delta_pool_w', 'delta_pool_scale', 'delta_w_out', 'delta_norm_ffn', 'delta_w_ffn_gate', 'delta_w_ffn_up', 'delta_w_ffn_down', 'delta_norm_ple', 'delta_w_ple_gate', 'delta_w_ple_proj', 'new_m_norm_mix', 'new_m_w_in', 'new_m_q_norm', 'new_m_k_norm', 'new_m_forget_bias', 'new_m_gmlp_v_norm', 'new_m_gmlp_w_s', 'new_m_gmlp_b_s', 'new_m_pool_w', 'new_m_pool_scale', 'new_m_w_out', 'new_m_norm_ffn', 'new_m_w_ffn_gate', 'new_m_w_ffn_up', 'new_m_w_ffn_down', 'new_m_norm_ple', 'new_m_w_ple_gate', 'new_m_w_ple_proj', 'new_v_norm_mix', 'new_v_w_in', 'new_v_q_norm', 'new_v_k_norm', 'new_v_forget_bias', 'new_v_gmlp_v_norm', 'new_v_gmlp_w_s', 'new_v_gmlp_b_s', 'new_v_pool_w', 'new_v_pool_scale', 'new_v_w_out', 'new_v_norm_ffn', 'new_v_w_ffn_gate', 'new_v_w_ffn_up', 'new_v_w_ffn_down', 'new_v_norm_ple', 'new_v_w_ple_gate', 'new_v_w_ple_proj']
TWIN_LEAF_KINDS = {'loss': 'loss', 'grad_x': 'grad_x', 'grad_norm_mix': 'grad_w', 'grad_w_in': 'grad_w', 'grad_q_norm': 'grad_w', 'grad_k_norm': 'grad_w', 'grad_forget_bias': 'grad_w', 'grad_gmlp_v_norm': 'grad_w', 'grad_gmlp_w_s': 'grad_w', 'grad_gmlp_b_s': 'grad_w', 'grad_pool_w': 'grad_w', 'grad_pool_scale': 'grad_w', 'grad_w_out': 'grad_w', 'grad_norm_ffn': 'grad_w', 'grad_w_ffn_gate': 'grad_w', 'grad_w_ffn_up': 'grad_w', 'grad_w_ffn_down': 'grad_w', 'grad_norm_ple': 'grad_w', 'grad_w_ple_gate': 'grad_w', 'grad_w_ple_proj': 'grad_w', 'delta_norm_mix': 'delta_w', 'delta_w_in': 'delta_w', 'delta_q_norm': 'delta_w', 'delta_k_norm': 'delta_w', 'delta_forget_bias': 'delta_w', 'delta_gmlp_v_norm': 'delta_w', 'delta_gmlp_w_s': 'delta_w', 'delta_gmlp_b_s': 'delta_w', 'delta_pool_w': 'delta_w', 'delta_pool_scale': 'delta_w', 'delta_w_out': 'delta_w', 'delta_norm_ffn': 'delta_w', 'delta_w_ffn_gate': 'delta_w', 'delta_w_ffn_up': 'delta_w', 'delta_w_ffn_down': 'delta_w', 'delta_norm_ple': 'delta_w', 'delta_w_ple_gate': 'delta_w', 'delta_w_ple_proj': 'delta_w', 'new_m_norm_mix': 'new_m', 'new_m_w_in': 'new_m', 'new_m_q_norm': 'new_m', 'new_m_k_norm': 'new_m', 'new_m_forget_bias': 'new_m', 'new_m_gmlp_v_norm': 'new_m', 'new_m_gmlp_w_s': 'new_m', 'new_m_gmlp_b_s': 'new_m', 'new_m_pool_w': 'new_m', 'new_m_pool_scale': 'new_m', 'new_m_w_out': 'new_m', 'new_m_norm_ffn': 'new_m', 'new_m_w_ffn_gate': 'new_m', 'new_m_w_ffn_up': 'new_m', 'new_m_w_ffn_down': 'new_m', 'new_m_norm_ple': 'new_m', 'new_m_w_ple_gate': 'new_m', 'new_m_w_ple_proj': 'new_m', 'new_v_norm_mix': 'new_v', 'new_v_w_in': 'new_v', 'new_v_q_norm': 'new_v', 'new_v_k_norm': 'new_v', 'new_v_forget_bias': 'new_v', 'new_v_gmlp_v_norm': 'new_v', 'new_v_gmlp_w_s': 'new_v', 'new_v_gmlp_b_s': 'new_v', 'new_v_pool_w': 'new_v', 'new_v_pool_scale': 'new_v', 'new_v_w_out': 'new_v', 'new_v_norm_ffn': 'new_v', 'new_v_w_ffn_gate': 'new_v', 'new_v_w_ffn_up': 'new_v', 'new_v_w_ffn_down': 'new_v', 'new_v_norm_ple': 'new_v', 'new_v_w_ple_gate': 'new_v', 'new_v_w_ple_proj': 'new_v'}


def _forward(args):
    return _fwd_reference(*[args[k] for k in FWD_PARAMS])


def _output_shape():
    out = _jax.eval_shape(lambda: _forward(_fwd_setup_inputs(0)))
    return out.shape, out.dtype

N_MICROBATCH = 1
ADAM_LR = 0.001
ADAM_B1 = 0.9
ADAM_B2 = 0.999
ADAM_EPS = 1e-08
ADAM_WD = 0.01
ADAM_STEP = 10
PER_EXAMPLE_BATCH_AXIS = {'x': 0, 'p': 1, 'loss_target': 0}
SHARED_INPUTS = []
_WEIGHT_DTYPES = {'norm_mix': _jnp.float32, 'w_in': _jnp.float32, 'q_norm': _jnp.float32, 'k_norm': _jnp.float32, 'forget_bias': _jnp.float32, 'gmlp_v_norm': _jnp.float32, 'gmlp_w_s': _jnp.float32, 'gmlp_b_s': _jnp.float32, 'pool_w': _jnp.float32, 'pool_scale': _jnp.float32, 'w_out': _jnp.float32, 'norm_ffn': _jnp.float32, 'w_ffn_gate': _jnp.float32, 'w_ffn_up': _jnp.float32, 'w_ffn_down': _jnp.float32, 'norm_ple': _jnp.float32, 'w_ple_gate': _jnp.float32, 'w_ple_proj': _jnp.float32}
MOMENT_SCALE = {'norm_mix': 3.464326e+00, 'w_in': 2.812338e-01, 'q_norm': 2.694442e+00, 'k_norm': 2.696570e+00, 'forget_bias': 3.069090e+01, 'gmlp_v_norm': 1.752243e+00, 'gmlp_w_s': 1.090916e+00, 'gmlp_b_s': 3.553917e+00, 'pool_w': 5.988911e-01, 'pool_scale': 6.274855e+00, 'w_out': 5.959270e-01, 'norm_ffn': 6.172596e+00, 'w_ffn_gate': 1.271044e-01, 'w_ffn_up': 1.116862e-01, 'w_ffn_down': 1.788762e-01, 'norm_ple': 2.490047e-01, 'w_ple_gate': 9.625786e-02, 'w_ple_proj': 1.706124e-01}


def _to_microbatches(a, axis):
    t = _jnp.moveaxis(a, axis, 0)
    t = t.reshape((N_MICROBATCH, t.shape[0] // N_MICROBATCH) + t.shape[1:])
    return _jnp.moveaxis(t, 1, axis + 1)


def setup_inputs(seed: int = 0) -> dict:
    inp = _fwd_setup_inputs(seed)
    key = _jax.random.fold_in(_jax.random.key(seed), 7919)
    shape, _ = _output_shape()
    out = dict(inp)
    out["loss_target"] = _jax.random.normal(_jax.random.fold_in(key, 0), shape, _jnp.float32)
    for i, name in enumerate(TWIN_WEIGHTS):
        w = inp[name].astype(_jnp.float32)
        if MOMENT_SCALE is None:
            s = _jnp.sqrt(_jnp.mean(_jnp.square(w)) + 1e-30)
        else:
            s = MOMENT_SCALE[name]
        km, kv = _jax.random.split(_jax.random.fold_in(key, i + 1))
        out[name] = w
        out["m_" + name] = s * _jax.random.normal(km, w.shape, _jnp.float32)
        out["v_" + name] = (s * s) * _jax.random.uniform(kv, w.shape, _jnp.float32, 0.5, 1.5)
    if N_MICROBATCH > 1:
        for name, axis in PER_EXAMPLE_BATCH_AXIS.items():
            out[name] = _to_microbatches(out[name], axis)
    return {'x': out['x'], 'p': out['p'], 'norm_mix': out['norm_mix'], 'w_in': out['w_in'], 'q_norm': out['q_norm'], 'k_norm': out['k_norm'], 'forget_bias': out['forget_bias'], 'gmlp_v_norm': out['gmlp_v_norm'], 'gmlp_w_s': out['gmlp_w_s'], 'gmlp_b_s': out['gmlp_b_s'], 'pool_w': out['pool_w'], 'pool_scale': out['pool_scale'], 'w_out': out['w_out'], 'norm_ffn': out['norm_ffn'], 'w_ffn_gate': out['w_ffn_gate'], 'w_ffn_up': out['w_ffn_up'], 'w_ffn_down': out['w_ffn_down'], 'norm_ple': out['norm_ple'], 'w_ple_gate': out['w_ple_gate'], 'w_ple_proj': out['w_ple_proj'], 'loss_target': out['loss_target'], 'm_norm_mix': out['m_norm_mix'], 'm_w_in': out['m_w_in'], 'm_q_norm': out['m_q_norm'], 'm_k_norm': out['m_k_norm'], 'm_forget_bias': out['m_forget_bias'], 'm_gmlp_v_norm': out['m_gmlp_v_norm'], 'm_gmlp_w_s': out['m_gmlp_w_s'], 'm_gmlp_b_s': out['m_gmlp_b_s'], 'm_pool_w': out['m_pool_w'], 'm_pool_scale': out['m_pool_scale'], 'm_w_out': out['m_w_out'], 'm_norm_ffn': out['m_norm_ffn'], 'm_w_ffn_gate': out['m_w_ffn_gate'], 'm_w_ffn_up': out['m_w_ffn_up'], 'm_w_ffn_down': out['m_w_ffn_down'], 'm_norm_ple': out['m_norm_ple'], 'm_w_ple_gate': out['m_w_ple_gate'], 'm_w_ple_proj': out['m_w_ple_proj'], 'v_norm_mix': out['v_norm_mix'], 'v_w_in': out['v_w_in'], 'v_q_norm': out['v_q_norm'], 'v_k_norm': out['v_k_norm'], 'v_forget_bias': out['v_forget_bias'], 'v_gmlp_v_norm': out['v_gmlp_v_norm'], 'v_gmlp_w_s': out['v_gmlp_w_s'], 'v_gmlp_b_s': out['v_gmlp_b_s'], 'v_pool_w': out['v_pool_w'], 'v_pool_scale': out['v_pool_scale'], 'v_w_out': out['v_w_out'], 'v_norm_ffn': out['v_norm_ffn'], 'v_w_ffn_gate': out['v_w_ffn_gate'], 'v_w_ffn_up': out['v_w_ffn_up'], 'v_w_ffn_down': out['v_w_ffn_down'], 'v_norm_ple': out['v_norm_ple'], 'v_w_ple_gate': out['v_w_ple_gate'], 'v_w_ple_proj': out['v_w_ple_proj']}


def _loss(weights, diff, rest, loss_target):
    with _jax.named_scope("forward"):
        args = {**rest, TWIN_DIFF_INPUT: diff, **{k: w.astype(_WEIGHT_DTYPES[k]) for k, w in weights.items()}}
        y = _forward(args)
    with _jax.named_scope("loss_head"):
        err = _jnp.square(y.astype(_jnp.float32) - loss_target)
        return 0.5 * _jnp.sum(_jnp.mean(err, axis=-1)) if err.ndim else 0.5 * err


def _adamw(w, g, m, v):
    m = ADAM_B1 * m + (1.0 - ADAM_B1) * g
    v = ADAM_B2 * v + (1.0 - ADAM_B2) * _jnp.square(g)
    m_hat = m / (1.0 - ADAM_B1 ** ADAM_STEP)
    v_hat = v / (1.0 - ADAM_B2 ** ADAM_STEP)
    delta = -ADAM_LR * (m_hat / (_jnp.sqrt(v_hat) + ADAM_EPS) + ADAM_WD * w)
    return delta, m, v


def reference(x, p, norm_mix, w_in, q_norm, k_norm, forget_bias, gmlp_v_norm, gmlp_w_s, gmlp_b_s, pool_w, pool_scale, w_out, norm_ffn, w_ffn_gate, w_ffn_up, w_ffn_down, norm_ple, w_ple_gate, w_ple_proj, loss_target, m_norm_mix, m_w_in, m_q_norm, m_k_norm, m_forget_bias, m_gmlp_v_norm, m_gmlp_w_s, m_gmlp_b_s, m_pool_w, m_pool_scale, m_w_out, m_norm_ffn, m_w_ffn_gate, m_w_ffn_up, m_w_ffn_down, m_norm_ple, m_w_ple_gate, m_w_ple_proj, v_norm_mix, v_w_in, v_q_norm, v_k_norm, v_forget_bias, v_gmlp_v_norm, v_gmlp_w_s, v_gmlp_b_s, v_pool_w, v_pool_scale, v_w_out, v_norm_ffn, v_w_ffn_gate, v_w_ffn_up, v_w_ffn_down, v_norm_ple, v_w_ple_gate, v_w_ple_proj):
    given = dict(x=x, p=p, norm_mix=norm_mix, w_in=w_in, q_norm=q_norm, k_norm=k_norm, forget_bias=forget_bias, gmlp_v_norm=gmlp_v_norm, gmlp_w_s=gmlp_w_s, gmlp_b_s=gmlp_b_s, pool_w=pool_w, pool_scale=pool_scale, w_out=w_out, norm_ffn=norm_ffn, w_ffn_gate=w_ffn_gate, w_ffn_up=w_ffn_up, w_ffn_down=w_ffn_down, norm_ple=norm_ple, w_ple_gate=w_ple_gate, w_ple_proj=w_ple_proj, loss_target=loss_target, m_norm_mix=m_norm_mix, m_w_in=m_w_in, m_q_norm=m_q_norm, m_k_norm=m_k_norm, m_forget_bias=m_forget_bias, m_gmlp_v_norm=m_gmlp_v_norm, m_gmlp_w_s=m_gmlp_w_s, m_gmlp_b_s=m_gmlp_b_s, m_pool_w=m_pool_w, m_pool_scale=m_pool_scale, m_w_out=m_w_out, m_norm_ffn=m_norm_ffn, m_w_ffn_gate=m_w_ffn_gate, m_w_ffn_up=m_w_ffn_up, m_w_ffn_down=m_w_ffn_down, m_norm_ple=m_norm_ple, m_w_ple_gate=m_w_ple_gate, m_w_ple_proj=m_w_ple_proj, v_norm_mix=v_norm_mix, v_w_in=v_w_in, v_q_norm=v_q_norm, v_k_norm=v_k_norm, v_forget_bias=v_forget_bias, v_gmlp_v_norm=v_gmlp_v_norm, v_gmlp_w_s=v_gmlp_w_s, v_gmlp_b_s=v_gmlp_b_s, v_pool_w=v_pool_w, v_pool_scale=v_pool_scale, v_w_out=v_w_out, v_norm_ffn=v_norm_ffn, v_w_ffn_gate=v_w_ffn_gate, v_w_ffn_up=v_w_ffn_up, v_w_ffn_down=v_w_ffn_down, v_norm_ple=v_norm_ple, v_w_ple_gate=v_w_ple_gate, v_w_ple_proj=v_w_ple_proj)
    weights = {n: given[n] for n in TWIN_WEIGHTS}
    shared = {n: given[n] for n in SHARED_INPUTS}
    per_example = {n: given[n] for n in ['x', 'p']}
    grad_fn = _jax.value_and_grad(_loss, argnums=(0, 1))

    def one_microbatch(ex, loss_target):
        ex = dict(ex)
        diff = ex.pop(TWIN_DIFF_INPUT)
        return grad_fn(weights, diff, {**shared, **ex}, loss_target)

    if N_MICROBATCH == 1:
        loss, (grad_w, grad_x) = one_microbatch(per_example, given["loss_target"])
    else:
        def body(carry, xs):
            loss_sum, grad_sum = carry
            l_k, (gw_k, gx_k) = one_microbatch(xs[0], xs[1])
            with _jax.named_scope("update"):
                return (loss_sum + l_k, _jax.tree.map(_jnp.add, grad_sum, gw_k)), gx_k

        init = (_jnp.zeros((), _jnp.float32), _jax.tree.map(_jnp.zeros_like, weights))
        (loss, grad_w), grad_x = _jax.lax.scan(body, init, (per_example, given["loss_target"]))
    with _jax.named_scope("update"):
        delta_w, new_m, new_v = {}, {}, {}
        for n in TWIN_WEIGHTS:
            delta_w[n], new_m[n], new_v[n] = _adamw(weights[n], grad_w[n], given["m_" + n], given["v_" + n])
    return (loss, grad_x, *[grad_w[n] for n in TWIN_WEIGHTS], *[delta_w[n] for n in TWIN_WEIGHTS],
            *[new_m[n] for n in TWIN_WEIGHTS], *[new_v[n] for n in TWIN_WEIGHTS])
```

```python
import jax
import jax.numpy as jnp
from jax import lax
from jax.experimental import pallas as pl
from jax.experimental.pallas import tpu as pltpu

f32, bf16 = jnp.float32, jnp.bfloat16
S = jax.ShapeDtypeStruct
BS = pl.BlockSpec
ANY = pl.BlockSpec(memory_space=pl.ANY)
MESH = pl.DeviceIdType.MESH

EPS = 1e-6
HEAD = 128
POOL_WINDOWS = (2, 4, 8, 16)
NEG = -1e30
N_CHIPS = 4
N_DEV = 8
VMEM_LIMIT = 56 * 1024 * 1024

ADAM_LR, ADAM_B1, ADAM_B2, ADAM_EPS, ADAM_WD, ADAM_STEP = 0.001, 0.9, 0.999, 1e-08, 0.01, 10

BIG = ("w_in", "w_out", "w_ffn_gate", "w_ffn_up", "w_ffn_down", "w_ple_gate", "w_ple_proj")
SMALL = ("norm_mix", "q_norm", "k_norm", "forget_bias", "gmlp_v_norm", "gmlp_w_s", "gmlp_b_s", "pool_w",
         "pool_scale", "norm_ffn", "norm_ple")
WEIGHTS = ("norm_mix", "w_in", "q_norm", "k_norm", "forget_bias", "gmlp_v_norm", "gmlp_w_s", "gmlp_b_s", "pool_w",
           "pool_scale", "w_out", "norm_ffn", "w_ffn_gate", "w_ffn_up", "w_ffn_down", "norm_ple", "w_ple_gate",
           "w_ple_proj")


def _tile(n, target, mult):
    best = None
    for t in range(mult, min(n, target) + 1, mult):
        if n % t == 0:
            best = t
    return best if best is not None else n


def _params(**kw):
    return pltpu.CompilerParams(vmem_limit_bytes=VMEM_LIMIT, **kw)


def _dot(a, b, kind):
    dims = {"nn": (((1,), (0,)), ((), ())), "nt": (((1,), (1,)), ((), ())), "tn": (((0,), (0,)), ((), ()))}[kind]
    return lax.dot_general(a.astype(bf16), b.astype(bf16), dims, preferred_element_type=f32)


def _my_place():
    x, y, c = lax.axis_index("x"), lax.axis_index("y"), lax.axis_index("c")
    return x, y, c, 2 * x + y


def _matmul(name, kind, grid, pairs, extras, outs, epilogue=None):
    n_p, n_e = len(pairs), len(extras)

    def body(*refs):
        a_refs, b_refs = refs[:n_p], refs[n_p:2 * n_p]
        e_refs = refs[2 * n_p:2 * n_p + n_e]
        o_refs = refs[2 * n_p + n_e:]
        accs = []
        for a_ref, b_ref in zip(a_refs, b_refs):
            if len(b_ref.shape) == 3:
                w = b_ref.shape[2]
                acc = None
                for s in range(b_ref.shape[0]):
                    d = _dot(a_ref[:, s * w:(s + 1) * w], b_ref[s], kind)
                    acc = d if acc is None else acc + d
            else:
                acc = _dot(a_ref[...], b_ref[...], kind)
            accs.append(acc)
        if epilogue is None:
            res = accs[0]
            for t in accs[1:]:
                res = res + t
            res = (res,)
        else:
            res = epilogue(accs, [e[...] for e in e_refs])
        for o_ref, o in zip(o_refs, res):
            o_ref[...] = o.astype(o_ref.dtype)

    in_arrays = [p[0][0] for p in pairs] + [p[1][0] for p in pairs] + [e[0] for e in extras]
    in_specs = [p[0][1] for p in pairs] + [p[1][1] for p in pairs] + [e[1] for e in extras]
    res = pl.pallas_call(
        body, name=name, grid=grid, in_specs=in_specs,
        out_specs=[o[2] for o in outs], out_shape=[S(o[0], o[1]) for o in outs],
        compiler_params=_params(),
    )(*in_arrays)
    return res


def _rms_fwd(name, x, g):
    T, D = x.shape
    tr = _tile(T, 256, 8)

    def body(x_ref, g_ref, o_ref):
        xv = x_ref[...]
        r = lax.rsqrt(jnp.mean(xv * xv, axis=-1, keepdims=True) + EPS)
        o_ref[...] = (xv * r * g_ref[...]).astype(o_ref.dtype)

    return pl.pallas_call(
        body, name=name, grid=(T // tr,),
        in_specs=[BS((tr, D), lambda i: (i, 0)), BS((1, D), lambda i: (0, 0))],
        out_specs=BS((tr, D), lambda i: (i, 0)), out_shape=S((T, D), bf16),
    )(x, g.reshape(1, D))


def _rms_bwd(name, dxn, x, g, dres):
    T, D = x.shape
    tr = _tile(T, 256, 8)

    def body(dxn_ref, x_ref, g_ref, dres_ref, dx_ref, dxb_ref, dg_ref):
        i = pl.program_id(0)
        xv = x_ref[...]
        r = lax.rsqrt(jnp.mean(xv * xv, axis=-1, keepdims=True) + EPS)
        xh = xv * r
        dxn_v = dxn_ref[...]
        dxh = dxn_v * g_ref[...]
        dx = dres_ref[...] + r * (dxh - xh * jnp.mean(dxh * xh, axis=-1, keepdims=True))
        dx_ref[...] = dx
        dxb_ref[...] = dx.astype(bf16)
        part = jnp.sum(dxn_v * xh, axis=0, keepdims=True)

        @pl.when(i == 0)
        def _():
            dg_ref[...] = part

        @pl.when(i > 0)
        def _():
            dg_ref[...] += part

    row = BS((tr, D), lambda i: (i, 0))
    vec = BS((1, D), lambda i: (0, 0))
    return pl.pallas_call(
        body, name=name, grid=(T // tr,),
        in_specs=[row, row, vec, row], out_specs=[row, row, vec],
        out_shape=[S((T, D), f32), S((T, D), bf16), S((1, D), f32)],
    )(dxn, x, g.reshape(1, D), dres)


def _loss_grad(name, y, tgt):
    T, D = y.shape
    tr = _tile(T, 256, 8)

    def body(y_ref, t_ref, dy_ref, l_ref):
        i = pl.program_id(0)
        e = y_ref[...] - t_ref[...]
        dy_ref[...] = e * (1.0 / D)
        part = 0.5 * jnp.sum(jnp.mean(e * e, axis=-1, keepdims=True), axis=0, keepdims=True)

        @pl.when(i == 0)
        def _():
            l_ref[...] = jnp.zeros_like(l_ref)

        l_ref[...] += jnp.broadcast_to(part, l_ref.shape)

    row = BS((tr, D), lambda i: (i, 0))
    return pl.pallas_call(
        body, name=name, grid=(T // tr,), in_specs=[row, row],
        out_specs=[row, BS((8, 128), lambda i: (0, 0))],
        out_shape=[S((T, D), f32), S((8, 128), f32)],
    )(y, tgt)


def _ple_bwd_elem(name, dh, gate, e):
    T, D = dh.shape
    tr = _tile(T, 256, 16)

    def body(dh_ref, g_ref, e_ref, de_ref, dz_ref):
        d = dh_ref[...]
        g = g_ref[...].astype(f32)
        de_ref[...] = (d * g).astype(bf16)
        dz_ref[...] = (d * e_ref[...].astype(f32) * g * (1.0 - g)).astype(bf16)

    row = BS((tr, D), lambda i: (i, 0))
    return pl.pallas_call(
        body, name=name, grid=(T // tr,), in_specs=[row, row, row], out_specs=[row, row],
        out_shape=[S((T, D), bf16), S((T, D), bf16)],
    )(dh, gate, e)


def _gelu_and_grad(x):
    k0, k1 = 0.7978845608028654, 0.044715
    th = jnp.tanh(k0 * (x + k1 * x * x * x))
    val = 0.5 * x * (1.0 + th)
    grad = 0.5 * (1.0 + th) + 0.5 * x * (1.0 - th * th) * (k0 * (1.0 + 3.0 * k1 * x * x))
    return val, grad


def _fgate_fwd(name, pf, fb):
    T = pf.shape[0]

    def body(pf_ref, fb_ref, c_ref, ct_ref):
        xv = jax.nn.log_sigmoid(pf_ref[...] + fb_ref[...])
        row = lax.broadcasted_iota(jnp.int32, xv.shape, 0)
        s = 1
        while s < T:
            xv = xv + jnp.where(row >= s, pltpu.roll(xv, s, 0), 0.0)
            s *= 2
        c_ref[...] = xv
        ct_ref[...] = xv.T

    return pl.pallas_call(body, name=name, out_shape=[S((T, HEAD), f32), S((HEAD, T), f32)])(pf, fb)


def _fgate_bwd(name, dct, pf, fb):
    T = pf.shape[0]

    def body(dct_ref, pf_ref, fb_ref, dpf_ref, dfb_ref):
        xv = dct_ref[...].T
        row = lax.broadcasted_iota(jnp.int32, xv.shape, 0)
        s = 1
        while s < T:
            xv = xv + jnp.where(row + s < T, pltpu.roll(xv, T - s, 0), 0.0)
            s *= 2
        df = xv * jax.nn.sigmoid(-(pf_ref[...] + fb_ref[...]))
        dpf_ref[...] = df.astype(bf16)
        dfb_ref[...] = jnp.sum(df, axis=0, keepdims=True)

    return pl.pallas_call(body, name=name, out_shape=[S((T, HEAD), bf16), S((1, HEAD), f32)])(dct, pf, fb)


def _qk_norm(name, P, qg, kg, A):
    T = P.shape[0]
    tr = _tile(T, 256, 16)
    n_heads = A // HEAD

    def body(q_ref, k_ref, v_ref, qg_ref, kg_ref, qn_ref, kn_ref, vb_ref):
        for h in range(n_heads):
            sl = slice(h * HEAD, (h + 1) * HEAD)
            for src, g_ref, dst in ((q_ref, qg_ref, qn_ref), (k_ref, kg_ref, kn_ref)):
                xv = src[:, sl]
                r = lax.rsqrt(jnp.mean(xv * xv, axis=-1, keepdims=True) + EPS)
                dst[:, sl] = (xv * r * g_ref[...]).astype(bf16)
        vb_ref[...] = v_ref[...].astype(bf16)

    vec = BS((1, HEAD), lambda i: (0, 0))
    out = BS((tr, A), lambda i: (i, 0))
    return pl.pallas_call(
        body, name=name, grid=(T // tr,),
        in_specs=[BS((tr, A), lambda i: (i, 0)), BS((tr, A), lambda i: (i, 1)), BS((tr, A), lambda i: (i, 2)), vec, vec],
        out_specs=[out, out, out], out_shape=[S((T, A), bf16)] * 3,
    )(P, P, P, qg, kg)


def _attn_fwd(name, qn, kn, vb, c_col, c_row, tb):
    T, A = qn.shape
    H = A // HEAD
    nb = T // tb
    scale = HEAD ** -0.5

    def body(q_ref, k_ref, v_ref, cq_ref, ck_ref, o_ref, o32_ref, lse_ref):
        i = pl.program_id(1)
        q = q_ref[...]
        cq = cq_ref[...]
        rows = i * tb + lax.broadcasted_iota(jnp.int32, (tb, tb), 0)
        lcols = lax.broadcasted_iota(jnp.int32, (tb, tb), 1)

        def step(j, carry):
            m, l, acc = carry
            koff = pl.multiple_of(j * tb, tb)
            k = k_ref[pl.ds(koff, tb), :]
            v = v_ref[pl.ds(koff, tb), :]
            s = _dot(q, k, "nt") * scale + (cq - ck_ref[j])
            s = jnp.where(rows >= koff + lcols, s, NEG)
            m_new = jnp.maximum(m, jnp.max(s, axis=-1, keepdims=True))
            alpha = jnp.exp(m - m_new)
            p = jnp.exp(s - m_new)
            l = l * alpha + jnp.sum(p, axis=-1, keepdims=True)
            acc = acc * alpha + _dot(p, v, "nn")
            return m_new, l, acc

        init = (jnp.full((tb, 1), NEG, f32), jnp.zeros((tb, 1), f32), jnp.zeros((tb, HEAD), f32))
        m, l, acc = lax.fori_loop(0, i + 1, step, init)
        o = acc / l
        o_ref[...] = o.astype(bf16)
        o32_ref[...] = o
        lse_ref[...] = m + jnp.log(l)

    return pl.pallas_call(
        body, name=name, grid=(H, nb),
        in_specs=[BS((tb, HEAD), lambda h, i: (i, h)), BS((T, HEAD), lambda h, i: (0, h)),
                  BS((T, HEAD), lambda h, i: (0, h)), BS((None, tb, 1), lambda h, i: (h, i, 0)),
                  BS((None, nb, 1, tb), lambda h, i: (h, 0, 0, 0))],
        out_specs=[BS((tb, HEAD), lambda h, i: (i, h)), BS((tb, HEAD), lambda h, i: (i, h)),
                   BS((None, tb, 1), lambda h, i: (h, i, 0))],
        out_shape=[S((T, A), bf16), S((T, A), f32), S((H, T, 1), f32)],
    )(qn, kn, vb, c_col, c_row)


def _attn_bwd(name, qn, kn, vb, o, dmix, lse, c_col, c_row, P, qg, kg, tb):
    T, A = qn.shape
    H = A // HEAD
    nb = T // tb
    scale = HEAD ** -0.5

    def body(q_ref, k_ref, v_ref, o_ref, do_ref, lse_ref, cq_ref, ck_ref, qraw_ref, kraw_ref, qg_ref, kg_ref,
             dq_out, dk_out, dv_out, dc_out, dqg_out, dkg_out, dq_acc, dk_acc, delta_s):
        h = pl.program_id(0)
        dq_acc[...] = jnp.zeros_like(dq_acc)
        delta_s[...] = jnp.sum(do_ref[...].astype(bf16).astype(f32) * o_ref[...], axis=-1, keepdims=True)
        lrows = lax.broadcasted_iota(jnp.int32, (tb, tb), 0)
        lcols = lax.broadcasted_iota(jnp.int32, (tb, tb), 1)

        def kblock(j, _):
            koff = pl.multiple_of(j * tb, tb)
            k = k_ref[pl.ds(koff, tb), :]
            v = v_ref[pl.ds(koff, tb), :]
            ck = ck_ref[j]

            def qblock(i, carry):
                dk, dv, dc = carry
                qoff = pl.multiple_of(i * tb, tb)
                q = q_ref[pl.ds(qoff, tb), :]
                do = do_ref[pl.ds(qoff, tb), :].astype(bf16)
                s = _dot(q, k, "nt") * scale + (cq_ref[pl.ds(qoff, tb), :] - ck)
                s = jnp.where(qoff + lrows >= koff + lcols, s, NEG)
                p = jnp.exp(s - lse_ref[pl.ds(qoff, tb), :])
                dv = dv + _dot(p, do, "tn")
                dp = _dot(do, v, "nt")
                ds = p * (dp - delta_s[pl.ds(qoff, tb), :])
                dc = dc - jnp.sum(ds, axis=0, keepdims=True)
                dsb = (ds * scale).astype(bf16)
                dk = dk + _dot(dsb, q, "tn")
                dq_acc[pl.ds(qoff, tb), :] += _dot(dsb, k, "nn")
                return dk, dv, dc

            init = (jnp.zeros((tb, HEAD), f32), jnp.zeros((tb, HEAD), f32), jnp.zeros((1, tb), f32))
            dk, dv, dc = lax.fori_loop(j, nb, qblock, init)
            dk_acc[pl.ds(koff, tb), :] = dk
            dv_out[pl.ds(koff, tb), :] = dv.astype(bf16)
            dc_out[j] = dc
            return 0

        lax.fori_loop(0, nb, kblock, 0)

        for raw_ref, g_ref, acc_ref, d_out, dg_out in ((qraw_ref, qg_ref, dq_acc, dq_out, dqg_out),
                                                       (kraw_ref, kg_ref, dk_acc, dk_out, dkg_out)):
            xv = raw_ref[...]
            r = lax.rsqrt(jnp.mean(xv * xv, axis=-1, keepdims=True) + EPS)
            xh = xv * r
            dn = acc_ref[...]
            dxh = dn * g_ref[...]
            d_out[...] = (r * (dxh - xh * jnp.mean(dxh * xh, axis=-1, keepdims=True))).astype(bf16)
            part = jnp.sum(dn * xh, axis=0, keepdims=True)

            @pl.when(h == 0)
            def _():
                dg_out[...] = part

            @pl.when(h > 0)
            def _():
                dg_out[...] += part

    head = lambda off: BS((T, HEAD), lambda h: (0, off + h))
    vec = BS((1, HEAD), lambda h: (0, 0))
    return pl.pallas_call(
        body, name=name, grid=(H,),
        in_specs=[head(0), head(0), head(0), head(0), head(0),
                  BS((None, T, 1), lambda h: (h, 0, 0)), BS((None, T, 1), lambda h: (h, 0, 0)),
                  BS((None, nb, 1, tb), lambda h: (h, 0, 0, 0)), head(0), head(H), vec, vec],
        out_specs=[head(0), head(0), head(0), BS((None, nb, 1, tb), lambda h: (h, 0, 0, 0)), vec, vec],
        out_shape=[S((T, A), bf16)] * 3 + [S((H, nb, 1, tb), f32), S((1, HEAD), f32), S((1, HEAD), f32)],
        scratch_shapes=[pltpu.VMEM((T, HEAD), f32), pltpu.VMEM((T, HEAD), f32), pltpu.VMEM((T, 1), f32)],
        compiler_params=_params(),
    )(qn, kn, vb, o, dmix, lse, c_col, c_row, P, P, qg, kg)


def _gmlp_fwd(name, P, gain, ws, b, col_u, col_v, Wd):
    T = P.shape[0]
    G = Wd // HEAD
    tr = _tile(T, 512, HEAD)

    def body(u_ref, v_ref, gain_ref, ws_ref, b_ref, y_ref):
        tril = lax.broadcasted_iota(jnp.int32, (HEAD, HEAD), 0) >= lax.broadcasted_iota(jnp.int32, (HEAD, HEAD), 1)
        wm = jnp.where(tril, ws_ref[...], 0.0).astype(bf16)
        for n in range(tr // HEAD):
            rows = slice(n * HEAD, (n + 1) * HEAD)
            u = jax.nn.gelu(u_ref[rows, :])
            a = jax.nn.gelu(v_ref[rows, :])
            r = lax.rsqrt(jnp.mean(a * a, axis=-1, keepdims=True) + EPS)
            vn = a * r * gain_ref[...]
            mixed = _dot(wm, vn, "nn") + b_ref[...]
            y_ref[rows, :] = (u * mixed).astype(bf16)

    return pl.pallas_call(
        body, name=name, grid=(G, T // tr),
        in_specs=[BS((tr, HEAD), lambda g, i: (i, col_u + g)), BS((tr, HEAD), lambda g, i: (i, col_v + g)),
                  BS((None, 1, HEAD), lambda g, i: (g, 0, 0)), BS((None, HEAD, HEAD), lambda g, i: (g, 0, 0)),
                  BS((None, HEAD, 1), lambda g, i: (g, 0, 0))],
        out_specs=BS((tr, HEAD), lambda g, i: (i, g)), out_shape=S((T, Wd), bf16),
    )(P, P, gain, ws, b)


def _gmlp_bwd(name, P, dmix, gain, ws, b, col_u, col_v, col_dy, Wd):
    T = P.shape[0]
    G = Wd // HEAD
    tr = _tile(T, 512, HEAD)

    def body(u_ref, v_ref, dy_ref, gain_ref, ws_ref, b_ref, du_ref, dv_ref, dws_ref, db_ref, dgain_ref):
        i = pl.program_id(1)
        tril = lax.broadcasted_iota(jnp.int32, (HEAD, HEAD), 0) >= lax.broadcasted_iota(jnp.int32, (HEAD, HEAD), 1)
        wm = jnp.where(tril, ws_ref[...], 0.0).astype(bf16)
        gain_v = gain_ref[...]
        dw = jnp.zeros((HEAD, HEAD), f32)
        db = jnp.zeros((HEAD, 1), f32)
        dgain = jnp.zeros((1, HEAD), f32)
        for n in range(tr // HEAD):
            rows = slice(n * HEAD, (n + 1) * HEAD)
            u, du_dx = _gelu_and_grad(u_ref[rows, :])
            a, da_dx = _gelu_and_grad(v_ref[rows, :])
            dy = dy_ref[rows, :]
            r = lax.rsqrt(jnp.mean(a * a, axis=-1, keepdims=True) + EPS)
            ah = a * r
            vnb = (ah * gain_v).astype(bf16)
            mixed = _dot(wm, vnb, "nn") + b_ref[...]
            dm = dy * u
            dmb = dm.astype(bf16)
            du_ref[rows, :] = (dy * mixed * du_dx).astype(bf16)
            db = db + jnp.sum(dm, axis=1, keepdims=True)
            dw = dw + _dot(dmb, vnb, "nt")
            dvn = _dot(wm, dmb, "tn")
            dgain = dgain + jnp.sum(dvn * ah, axis=0, keepdims=True)
            dah = dvn * gain_v
            da = r * (dah - ah * jnp.mean(dah * ah, axis=-1, keepdims=True))
            dv_ref[rows, :] = (da * da_dx).astype(bf16)
        dw = jnp.where(tril, dw, 0.0)

        @pl.when(i == 0)
        def _():
            dws_ref[...] = dw
            db_ref[...] = db
            dgain_ref[...] = dgain

        @pl.when(i > 0)
        def _():
            dws_ref[...] += dw
            db_ref[...] += db
            dgain_ref[...] += dgain

    out = BS((tr, HEAD), lambda g, i: (i, g))
    return pl.pallas_call(
        body, name=name, grid=(G, T // tr),
        in_specs=[BS((tr, HEAD), lambda g, i: (i, col_u + g)), BS((tr, HEAD), lambda g, i: (i, col_v + g)),
                  BS((tr, HEAD), lambda g, i: (i, col_dy + g)),
                  BS((None, 1, HEAD), lambda g, i: (g, 0, 0)), BS((None, HEAD, HEAD), lambda g, i: (g, 0, 0)),
                  BS((None, HEAD, 1), lambda g, i: (g, 0, 0))],
        out_specs=[out, out, BS((None, HEAD, HEAD), lambda g, i: (g, 0, 0)), BS((None, HEAD, 1), lambda g, i: (g, 0, 0)),
                   BS((None, 1, HEAD), lambda g, i: (g, 0, 0))],
        out_shape=[S((T, Wd), bf16), S((T, Wd), bf16), S((G, HEAD, HEAD), f32), S((G, HEAD, 1), f32),
                   S((G, 1, HEAD), f32)],
    )(P, P, dmix, gain, ws, b)


def _trailing_window(xv, w, row):
    k = 1
    while k < w:
        xv = xv + jnp.where(row >= k, pltpu.roll(xv, k, 0), 0.0)
        k *= 2
    return xv


def _leading_window(xv, w, row, T):
    k = 1
    while k < w:
        xv = xv + jnp.where(row + k < T, pltpu.roll(xv, T - k, 0), 0.0)
        k *= 2
    return xv


def _pool_fwd(name, P, pw, ps, col_x, Wd):
    T = P.shape[0]
    Gp = Wd // HEAD

    def body(x_ref, pw_ref, ps_ref, y_ref):
        row = lax.broadcasted_iota(jnp.int32, (T, HEAD), 0)
        for g in range(Gp):
            w = POOL_WINDOWS[g]
            sl = slice(g * HEAD, (g + 1) * HEAD)
            xv = x_ref[:, sl]
            cnt = jnp.minimum(row + 1, w).astype(f32)
            d = _trailing_window(xv, w, row) / cnt - xv
            y_ref[:, sl] = (_dot(d, pw_ref[g], "nn") * ps_ref[:, sl]).astype(bf16)

    return pl.pallas_call(
        body, name=name, grid=(1,),
        in_specs=[BS((T, Wd), lambda i: (0, col_x)), BS((Gp, HEAD, HEAD), lambda i: (0, 0, 0)), BS((1, Wd), lambda i: (0, 0))],
        out_specs=BS((T, Wd), lambda i: (0, 0)), out_shape=S((T, Wd), bf16), compiler_params=_params(),
    )(P, pw, ps)


def _pool_bwd(name, P, dmix, pw, ps, col_x, col_dy, Wd):
    T = P.shape[0]
    Gp = Wd // HEAD

    def body(x_ref, dy_ref, pw_ref, ps_ref, dx_ref, dpw_ref, dps_ref):
        row = lax.broadcasted_iota(jnp.int32, (T, HEAD), 0)
        for g in range(Gp):
            w = POOL_WINDOWS[g]
            sl = slice(g * HEAD, (g + 1) * HEAD)
            xv = x_ref[:, sl]
            cnt = jnp.minimum(row + 1, w).astype(f32)
            d = (_trailing_window(xv, w, row) / cnt - xv).astype(bf16)
            pwb = pw_ref[g].astype(bf16)
            z = _dot(d, pwb, "nn")
            dy = dy_ref[:, sl]
            dps_ref[:, sl] = jnp.sum(dy * z, axis=0, keepdims=True)
            dzb = (dy * ps_ref[:, sl]).astype(bf16)
            dpw_ref[g] = _dot(d, dzb, "tn")
            dd = _dot(dzb, pwb, "nt")
            dx_ref[:, sl] = (_leading_window(dd / cnt, w, row, T) - dd).astype(bf16)

    return pl.pallas_call(
        body, name=name, grid=(1,),
        in_specs=[BS((T, Wd), lambda i: (0, col_x)), BS((T, Wd), lambda i: (0, col_dy)),
                  BS((Gp, HEAD, HEAD), lambda i: (0, 0, 0)), BS((1, Wd), lambda i: (0, 0))],
        out_specs=[BS((T, Wd), lambda i: (0, 0)), BS((Gp, HEAD, HEAD), lambda i: (0, 0, 0)), BS((1, Wd), lambda i: (0, 0))],
        out_shape=[S((T, Wd), bf16), S((Gp, HEAD, HEAD), f32), S((1, Wd), f32)], compiler_params=_params(),
    )(P, dmix, pw, ps)


def _adamw(name, w, g, m, v):
    R, C = w.shape
    lanes = -(-C // 128) * 128
    tr = _tile(R, max(8, (512 * 1024) // lanes // 8 * 8), 8)
    c1 = 1.0 - ADAM_B1 ** ADAM_STEP
    c2 = 1.0 - ADAM_B2 ** ADAM_STEP

    def body(w_ref, g_ref, m_ref, v_ref, d_ref, nm_ref, nv_ref):
        gv = g_ref[...]
        nm = ADAM_B1 * m_ref[...] + (1.0 - ADAM_B1) * gv
        nv = ADAM_B2 * v_ref[...] + (1.0 - ADAM_B2) * (gv * gv)
        d_ref[...] = -ADAM_LR * ((nm / c1) / (jnp.sqrt(nv / c2) + ADAM_EPS) + ADAM_WD * w_ref[...])
        nm_ref[...] = nm
        nv_ref[...] = nv

    blk = BS((tr, C), lambda i: (i, 0))
    return pl.pallas_call(
        body, name=name, grid=(R // tr,), in_specs=[blk] * 4, out_specs=[blk] * 3, out_shape=[S((R, C), f32)] * 3,
    )(w, g, m, v)


def _adamw_layer(name, layer, w_all, m_all, v_all, g, prev):
    L, R, C = w_all.shape
    lanes = -(-C // 128) * 128
    tr = _tile(R, max(8, (512 * 1024) // lanes // 8 * 8), 8)
    c1 = 1.0 - ADAM_B1 ** ADAM_STEP
    c2 = 1.0 - ADAM_B2 ** ADAM_STEP
    n_prev = 0 if prev is None else 4

    def body(w_ref, m_ref, v_ref, g_ref, *rest):
        go_ref, d_ref, nm_ref, nv_ref = rest[n_prev:]
        gv = g_ref[...]
        nm = ADAM_B1 * m_ref[...] + (1.0 - ADAM_B1) * gv
        nv = ADAM_B2 * v_ref[...] + (1.0 - ADAM_B2) * (gv * gv)
        d_ref[...] = -ADAM_LR * ((nm / c1) / (jnp.sqrt(nv / c2) + ADAM_EPS) + ADAM_WD * w_ref[...])
        nm_ref[...] = nm
        nv_ref[...] = nv
        go_ref[...] = gv

    slab = BS((None, tr, C), lambda r: (layer, r, 0))
    return pl.pallas_call(
        body, name=name, grid=(R // tr,),
        in_specs=[slab, slab, slab, BS((tr, C), lambda r: (r, 0))] + [ANY] * n_prev,
        out_specs=[slab] * 4, out_shape=[S((L, R, C), f32)] * 4,
        input_output_aliases={4 + k: k for k in range(n_prev)},
    )(w_all, m_all, v_all, g, *(prev or ()))


def _chip_of(k):
    return k // 2, k % 2


def _all_gather_chips(name, shards):
    n = len(shards)

    def body(*refs):
        ins, outs = refs[:n], refs[n:2 * n]
        send_sems, recv_sems, loc_sems = refs[2 * n:]
        x, y, c, j0 = _my_place()
        sends, locs = [], []
        for a in range(n):
            half = ins[a].shape[0] // 2
            lo = c * half
            loc = pltpu.make_async_copy(ins[a], outs[a].at[j0], loc_sems.at[a])
            loc.start()
            locs.append(loc)
            for r in (1, 2, 3):
                px, py = _chip_of(j0 ^ r)
                cp = pltpu.make_async_remote_copy(
                    src_ref=ins[a].at[pl.ds(lo, half)], dst_ref=outs[a].at[j0, pl.ds(lo, half)],
                    send_sem=send_sems.at[a, r - 1], recv_sem=recv_sems.at[a, r - 1],
                    device_id=(px, py, c), device_id_type=MESH)
                cp.start()
                sends.append(cp)
        for a in range(n):
            half = ins[a].shape[0] // 2
            lo = c * half
            for r in (1, 2, 3):
                k = j0 ^ r
                px, py = _chip_of(k)
                landed = outs[a].at[k, pl.ds(lo, half)]
                pltpu.make_async_remote_copy(
                    src_ref=landed, dst_ref=landed, send_sem=send_sems.at[a, r - 1], recv_sem=recv_sems.at[a, r - 1],
                    device_id=(px, py, c), device_id_type=MESH).wait_recv()
                fw = pltpu.make_async_remote_copy(
                    src_ref=landed, dst_ref=landed, send_sem=send_sems.at[a, 2 + r], recv_sem=recv_sems.at[a, 2 + r],
                    device_id=(x, y, 1 - c), device_id_type=MESH)
                fw.start()
                sends.append(fw)
        for a in range(n):
            half = ins[a].shape[0] // 2
            for r in (1, 2, 3):
                other = outs[a].at[j0 ^ r, pl.ds((1 - c) * half, half)]
                pltpu.make_async_remote_copy(
                    src_ref=other, dst_ref=other, send_sem=send_sems.at[a, 2 + r], recv_sem=recv_sems.at[a, 2 + r],
                    device_id=(x, y, 1 - c), device_id_type=MESH).wait_recv()
        for cp in sends:
            cp.wait_send()
        for loc in locs:
            loc.wait()

    return pl.pallas_call(
        body, name=name, in_specs=[ANY] * n, out_specs=[ANY] * n,
        out_shape=[S((N_CHIPS,) + s.shape, s.dtype) for s in shards],
        scratch_shapes=[pltpu.SemaphoreType.DMA((n, 6)), pltpu.SemaphoreType.DMA((n, 6)), pltpu.SemaphoreType.DMA((n,))],
    )(*shards)


def _sibling_halves(name, grads):
    n = len(grads)

    def body(*refs):
        ins, outs = refs[:n], refs[n:2 * n]
        send_sems, recv_sems = refs[2 * n:]
        x, y, c, _ = _my_place()
        cps = []
        for a in range(n):
            half = ins[a].shape[1] // 2
            cp = pltpu.make_async_remote_copy(
                src_ref=ins[a].at[:, pl.ds((1 - c) * half, half), :], dst_ref=outs[a],
                send_sem=send_sems.at[a], recv_sem=recv_sems.at[a], device_id=(x, y, 1 - c), device_id_type=MESH)
            cp.start()
            cps.append(cp)
        for cp in cps:
            cp.wait_recv()
        for cp in cps:
            cp.wait_send()

    return pl.pallas_call(
        body, name=name, in_specs=[ANY] * n, out_specs=[ANY] * n,
        out_shape=[S((N_CHIPS, g.shape[1] // 2, g.shape[2]), g.dtype) for g in grads],
        scratch_shapes=[pltpu.SemaphoreType.DMA((n,)), pltpu.SemaphoreType.DMA((n,))],
    )(*grads)


def _chip_scatter(name, parts):
    n = len(parts)

    def body(*refs):
        ins, outs = refs[:n], refs[n:2 * n]
        send_sems, recv_sems, loc_sems = refs[2 * n:]
        x, y, c, j0 = _my_place()
        cps, locs = [], []
        for a in range(n):
            loc = pltpu.make_async_copy(ins[a].at[j0], outs[a].at[j0], loc_sems.at[a])
            loc.start()
            locs.append(loc)
            for r in (1, 2, 3):
                k = j0 ^ r
                px, py = _chip_of(k)
                cp = pltpu.make_async_remote_copy(
                    src_ref=ins[a].at[k], dst_ref=outs[a].at[j0], send_sem=send_sems.at[a, r - 1],
                    recv_sem=recv_sems.at[a, r - 1], device_id=(px, py, c), device_id_type=MESH)
                cp.start()
                cps.append(cp)
        for a in range(n):
            for r in (1, 2, 3):
                k = j0 ^ r
                px, py = _chip_of(k)
                pltpu.make_async_remote_copy(
                    src_ref=outs[a].at[k], dst_ref=outs[a].at[k], send_sem=send_sems.at[a, r - 1],
                    recv_sem=recv_sems.at[a, r - 1], device_id=(px, py, c), device_id_type=MESH).wait_recv()
        for cp in cps:
            cp.wait_send()
        for loc in locs:
            loc.wait()

    return pl.pallas_call(
        body, name=name, in_specs=[ANY] * n, out_specs=[ANY] * n,
        out_shape=[S(p.shape, p.dtype) for p in parts],
        scratch_shapes=[pltpu.SemaphoreType.DMA((n, 3)), pltpu.SemaphoreType.DMA((n, 3)), pltpu.SemaphoreType.DMA((n,))],
    )(*parts)


def _sibling_join(name, halves):
    n = len(halves)

    def body(*refs):
        ins, outs = refs[:n], refs[n:2 * n]
        send_sems, recv_sems, loc_sems = refs[2 * n:]
        x, y, c, _ = _my_place()
        cps, locs = [], []
        for a in range(n):
            half = ins[a].shape[0]
            mine = outs[a].at[pl.ds(c * half, half)]
            loc = pltpu.make_async_copy(ins[a], mine, loc_sems.at[a])
            loc.start()
            locs.append(loc)
            cp = pltpu.make_async_remote_copy(
                src_ref=ins[a], dst_ref=mine, send_sem=send_sems.at[a], recv_sem=recv_sems.at[a],
                device_id=(x, y, 1 - c), device_id_type=MESH)
            cp.start()
            cps.append(cp)
        for a in range(n):
            half = ins[a].shape[0]
            theirs = outs[a].at[pl.ds((1 - c) * half, half)]
            pltpu.make_async_remote_copy(
                src_ref=theirs, dst_ref=theirs, send_sem=send_sems.at[a], recv_sem=recv_sems.at[a],
                device_id=(x, y, 1 - c), device_id_type=MESH).wait_recv()
        for cp in cps:
            cp.wait_send()
        for loc in locs:
            loc.wait()

    return pl.pallas_call(
        body, name=name, in_specs=[ANY] * n, out_specs=[ANY] * n,
        out_shape=[S((2 * h.shape[0], h.shape[1]), h.dtype) for h in halves],
        scratch_shapes=[pltpu.SemaphoreType.DMA((n,)), pltpu.SemaphoreType.DMA((n,)), pltpu.SemaphoreType.DMA((n,))],
    )(*halves)


def _all_reduce_small(name, g8):
    _, R, L = g8.shape

    def body(g_ref, out_ref, land, red, send1, recv1, send2, recv2):
        x, y, c, _ = _my_place()
        me = 4 * x + 2 * y + c
        peers = []
        for r in range(1, N_DEV):
            q = me ^ r
            peers.append((q, (q // 4, (q // 2) % 2, q % 2)))
        first = []
        for r, (q, dev) in enumerate(peers):
            cp = pltpu.make_async_remote_copy(src_ref=g_ref.at[q], dst_ref=land.at[me], send_sem=send1.at[r],
                                              recv_sem=recv1.at[r], device_id=dev, device_id_type=MESH)
            cp.start()
            first.append(cp)
        land[me] = g_ref[me]
        for r, (q, dev) in enumerate(peers):
            pltpu.make_async_remote_copy(src_ref=land.at[q], dst_ref=land.at[q], send_sem=send1.at[r],
                                         recv_sem=recv1.at[r], device_id=dev, device_id_type=MESH).wait_recv()
        acc = land[0]
        for d in range(1, N_DEV):
            acc = acc + land[d]
        red[...] = acc
        out_ref[me] = acc
        second = []
        for r, (q, dev) in enumerate(peers):
            cp = pltpu.make_async_remote_copy(src_ref=red, dst_ref=out_ref.at[me], send_sem=send2.at[r],
                                              recv_sem=recv2.at[r], device_id=dev, device_id_type=MESH)
            cp.start()
            second.append(cp)
        for r, (q, dev) in enumerate(peers):
            pltpu.make_async_remote_copy(src_ref=out_ref.at[q], dst_ref=out_ref.at[q], send_sem=send2.at[r],
                                         recv_sem=recv2.at[r], device_id=dev, device_id_type=MESH).wait_recv()
        for cp in first + second:
            cp.wait_send()

    vm = pl.BlockSpec(memory_space=pltpu.VMEM)
    return pl.pallas_call(
        body, name=name, in_specs=[vm], out_specs=vm, out_shape=S(g8.shape, f32),
        scratch_shapes=[pltpu.VMEM((N_DEV, R, L), f32), pltpu.VMEM((R, L), f32)]
        + [pltpu.SemaphoreType.DMA((N_DEV - 1,))] * 4,
        compiler_params=_params(),
    )(g8)


def _pair_sum(name, g4, sib, place):
    _, rows, cols = g4.shape
    half = rows // 2
    lanes = -(-cols // 128) * 128
    tr = _tile(half, max(16, (512 * 1024) // lanes // 16 * 16), 16)
    nb = half // tr

    def body(place_ref, g_ref, s_ref, pb_ref, own_ref):
        j = pl.program_id(1)
        t = g_ref[...] + s_ref[...]
        pb_ref[...] = t.astype(bf16)

        @pl.when(j == place_ref[1])
        def _():
            own_ref[...] = t

    grid_spec = pltpu.PrefetchScalarGridSpec(
        num_scalar_prefetch=1, grid=(nb, N_CHIPS),
        in_specs=[BS((None, tr, cols), lambda i, j, pr: (j, pr[0] * nb + i, 0)), BS((None, tr, cols), lambda i, j, pr: (j, i, 0))],
        out_specs=[BS((None, tr, cols), lambda i, j, pr: (j, i, 0)), BS((tr, cols), lambda i, j, pr: (i, 0))],
    )
    return pl.pallas_call(
        body, name=name, grid_spec=grid_spec,
        out_shape=[S((N_CHIPS, half, cols), bf16), S((half, cols), f32)],
    )(place, g4, sib)


def _chip_sum(name, own, got, place):
    half, cols = own.shape
    lanes = -(-cols // 128) * 128
    tr = _tile(half, max(16, (512 * 1024) // lanes // 16 * 16), 16)

    def body(place_ref, own_ref, got_ref, o_ref):
        j0 = place_ref[1]
        acc = None
        for k in range(N_CHIPS):
            t = jnp.where(j0 == k, own_ref[...], got_ref[k].astype(f32))
            acc = t if acc is None else acc + t
        o_ref[...] = acc

    grid_spec = pltpu.PrefetchScalarGridSpec(
        num_scalar_prefetch=1, grid=(half // tr,),
        in_specs=[BS((tr, cols), lambda i, pr: (i, 0)), BS((N_CHIPS, tr, cols), lambda i, pr: (0, i, 0))],
        out_specs=BS((tr, cols), lambda i, pr: (i, 0)),
    )
    return pl.pallas_call(body, name=name, grid_spec=grid_spec, out_shape=S((half, cols), f32))(place, own, got)


def _reduce_scatter(tag, grads, place):
    n = len(grads)
    sib = _sibling_halves(f"rs_a_{tag}", grads)
    pb, own = [], []
    for a in range(n):
        p, o = _pair_sum(f"rs_pair_{tag}_{a}", grads[a], sib[a], place)
        pb.append(p)
        own.append(o)
    got = _chip_scatter(f"rs_b_{tag}", pb)
    fin = [_chip_sum(f"rs_sum_{tag}_{a}", own[a], got[a], place) for a in range(n)]
    return _sibling_join(f"rs_c_{tag}", fin)


def _pack_rows(arrs, n_rows):
    flat = jnp.concatenate([a.reshape(-1).astype(f32) for a in arrs])
    return jnp.pad(flat, (0, n_rows * 128 - flat.shape[0])).reshape(n_rows, 128)


def kernel(x, p, norm_mix, w_in, q_norm, k_norm, forget_bias, gmlp_v_norm, gmlp_w_s, gmlp_b_s, pool_w, pool_scale, w_out, norm_ffn, w_ffn_gate, w_ffn_up, w_ffn_down, norm_ple, w_ple_gate, w_ple_proj, loss_target, m_norm_mix, m_w_in, m_q_norm, m_k_norm, m_forget_bias, m_gmlp_v_norm, m_gmlp_w_s, m_gmlp_b_s, m_pool_w, m_pool_scale, m_w_out, m_norm_ffn, m_w_ffn_gate, m_w_ffn_up, m_w_ffn_down, m_norm_ple, m_w_ple_gate, m_w_ple_proj, v_norm_mix, v_w_in, v_q_norm, v_k_norm, v_forget_bias, v_gmlp_v_norm, v_gmlp_w_s, v_gmlp_b_s, v_pool_w, v_pool_scale, v_w_out, v_norm_ffn, v_w_ffn_gate, v_w_ffn_up, v_w_ffn_down, v_norm_ple, v_w_ple_gate, v_w_ple_proj):
    W = dict(norm_mix=norm_mix, w_in=w_in, q_norm=q_norm, k_norm=k_norm, forget_bias=forget_bias,
             gmlp_v_norm=gmlp_v_norm, gmlp_w_s=gmlp_w_s, gmlp_b_s=gmlp_b_s, pool_w=pool_w, pool_scale=pool_scale,
             w_out=w_out, norm_ffn=norm_ffn, w_ffn_gate=w_ffn_gate, w_ffn_up=w_ffn_up, w_ffn_down=w_ffn_down,
             norm_ple=norm_ple, w_ple_gate=w_ple_gate, w_ple_proj=w_ple_proj)
    M = dict(norm_mix=m_norm_mix, w_in=m_w_in, q_norm=m_q_norm, k_norm=m_k_norm, forget_bias=m_forget_bias,
             gmlp_v_norm=m_gmlp_v_norm, gmlp_w_s=m_gmlp_w_s, gmlp_b_s=m_gmlp_b_s, pool_w=m_pool_w,
             pool_scale=m_pool_scale, w_out=m_w_out, norm_ffn=m_norm_ffn, w_ffn_gate=m_w_ffn_gate,
             w_ffn_up=m_w_ffn_up, w_ffn_down=m_w_ffn_down, norm_ple=m_norm_ple, w_ple_gate=m_w_ple_gate,
             w_ple_proj=m_w_ple_proj)
    V = dict(norm_mix=v_norm_mix, w_in=v_w_in, q_norm=v_q_norm, k_norm=v_k_norm, forget_bias=v_forget_bias,
             gmlp_v_norm=v_gmlp_v_norm, gmlp_w_s=v_gmlp_w_s, gmlp_b_s=v_gmlp_b_s, pool_w=v_pool_w,
             pool_scale=v_pool_scale, w_out=v_w_out, norm_ffn=v_norm_ffn, w_ffn_gate=v_w_ffn_gate,
             w_ffn_up=v_w_ffn_up, w_ffn_down=v_w_ffn_down, norm_ple=v_norm_ple, w_ple_gate=v_w_ple_gate,
             w_ple_proj=v_w_ple_proj)

    L = w_in.shape[0]
    _, T, D = x.shape
    A, Wd = D // 2, D // 4
    H = A // HEAD
    G = gmlp_w_s.shape[1]
    Gp = pool_w.shape[1]
    DP4 = w_in.shape[2]
    DP = N_CHIPS * DP4
    NM = 3 * A + 3 * Wd
    FS = w_ffn_gate.shape[2]
    FF = N_CHIPS * FS
    DS = D // N_CHIPS
    PL = p.shape[-1]
    assert Wd // G == HEAD and Wd // Gp == HEAD and DP == NM + H and H <= HEAD
    assert all(w & (w - 1) == 0 for w in POOL_WINDOWS[:Gp])
    tb = _tile(T, 256, HEAD)
    nb = T // tb
    tm = _tile(T, 512, 16)
    tn = _tile(NM, 512, 128)
    tnd = _tile(D, 512, 128)
    tkf = _tile(FF, 512, 128)
    c_ax, j0 = lax.axis_index("c"), 2 * lax.axis_index("x") + lax.axis_index("y")
    place = jnp.stack([c_ax, j0]).astype(jnp.int32)
    col_gu, col_gv, col_xp = 3 * A // HEAD, (3 * A + Wd) // HEAD, (3 * A + 2 * Wd) // Wd
    col_dg, col_dp = A // HEAD, (A + Wd) // Wd

    Wf = []
    for i in range(L):
        g = _all_gather_chips(f"ag_{i}", [W[n][i].astype(bf16) for n in BIG])
        win = g[0].transpose(1, 0, 2).reshape(D, DP)
        Wf.append(dict(
            w_main=jnp.concatenate([win[:, :3 * A], win[:, 3 * A + H:]], axis=1),
            w_f=jnp.pad(win[:, 3 * A:3 * A + H], ((0, 0), (0, HEAD - H))),
            w_out=g[1].reshape(D, D), w_gate=g[2], w_up=g[3], w_down=g[4].reshape(FF, D),
            w_pg=g[5].reshape(D, D), w_pp=g[6]))

    h = x.reshape(T, D)
    pb16 = p.reshape(L, T, PL).astype(bf16)
    saved = []
    full = lambda shp: BS(shp, lambda i, j: (0,) * len(shp))

    for i in range(L):
        w = Wf[i]
        sv = dict(h0=h)
        xn1 = _rms_fwd(f"rms1_{i}", h, norm_mix[i])
        (P,) = _matmul(f"proj_{i}", "nn", (T // tm, NM // tn),
                       [((xn1, BS((tm, D), lambda i, j: (i, 0))), (w["w_main"], BS((D, tn), lambda i, j: (0, j))))], [],
                       [((T, NM), f32, BS((tm, tn), lambda i, j: (i, j)))])
        (Pf,) = _matmul(f"projf_{i}", "nn", (T // tm, 1),
                        [((xn1, BS((tm, D), lambda i, j: (i, 0))), (w["w_f"], BS((D, HEAD), lambda i, j: (0, 0))))], [],
                        [((T, HEAD), f32, BS((tm, HEAD), lambda i, j: (i, 0)))])
        fb = jnp.pad(forget_bias[i], (0, HEAD - H)).reshape(1, HEAD)
        cc, ct = _fgate_fwd(f"fgate_{i}", Pf, fb)
        c_col = ct[:H].reshape(H, T, 1)
        c_row = ct[:H].reshape(H, nb, 1, tb)
        qg, kg = q_norm[i].reshape(1, HEAD), k_norm[i].reshape(1, HEAD)
        qn, kn, vb = _qk_norm(f"qknorm_{i}", P, qg, kg, A)
        y_attn, o32, lse = _attn_fwd(f"attn_{i}", qn, kn, vb, c_col, c_row, tb)
        gain = gmlp_v_norm[i].reshape(G, 1, HEAD)
        bs = gmlp_b_s[i].reshape(G, HEAD, 1)
        y_gmlp = _gmlp_fwd(f"gmlp_{i}", P, gain, gmlp_w_s[i], bs, col_gu, col_gv, Wd)
        ps = pool_scale[i].reshape(1, Wd)
        y_pool = _pool_fwd(f"pool_{i}", P, pool_w[i], ps, col_xp, Wd)
        mix = jnp.concatenate([y_attn, y_gmlp, y_pool], axis=1)
        (h1,) = _matmul(f"out_{i}", "nn", (T // tm, D // tnd),
                        [((mix, BS((tm, D), lambda i, j: (i, 0))), (w["w_out"], BS((D, tnd), lambda i, j: (0, j))))],
                        [(h, BS((tm, tnd), lambda i, j: (i, j)))],
                        [((T, D), f32, BS((tm, tnd), lambda i, j: (i, j)))],
                        epilogue=lambda accs, ex: (accs[0] + ex[0],))
        xn2 = _rms_fwd(f"rms2_{i}", h1, norm_ffn[i])

        def ffn_epi(accs, ex):
            g_, u_ = accs
            return g_, u_, g_ * jax.nn.sigmoid(g_) * u_

        ffo = BS((tm, FS), lambda i, j: (i, j))
        Gt, Ut, act = _matmul(f"ffn1_{i}", "nn", (T // tm, N_CHIPS),
                              [((xn2, BS((tm, D), lambda i, j: (i, 0))), (w["w_gate"], BS((None, D, FS), lambda i, j: (j, 0, 0)))),
                               ((xn2, BS((tm, D), lambda i, j: (i, 0))), (w["w_up"], BS((None, D, FS), lambda i, j: (j, 0, 0))))],
                              [], [((T, FF), bf16, ffo)] * 3, epilogue=ffn_epi)
        (h2,) = _matmul(f"ffn2_{i}", "nn", (T // tm, D // tnd),
                        [((act, BS((tm, FF), lambda i, j: (i, 0))), (w["w_down"], BS((FF, tnd), lambda i, j: (0, j))))],
                        [(h1, BS((tm, tnd), lambda i, j: (i, j)))],
                        [((T, D), f32, BS((tm, tnd), lambda i, j: (i, j)))],
                        epilogue=lambda accs, ex: (accs[0] + ex[0],))
        xn3 = _rms_fwd(f"rms3_{i}", h2, norm_ple[i])

        def ple_epi(accs, ex):
            gate = jax.nn.sigmoid(accs[0])
            return ex[0] + accs[1] * gate, gate, accs[1]

        dso = BS((tm, DS), lambda i, j: (i, j))
        h3, gate, e = _matmul(f"ple_{i}", "nn", (T // tm, N_CHIPS),
                              [((xn3, BS((tm, D), lambda i, j: (i, 0))), (w["w_pg"], BS((D, DS), lambda i, j: (0, j)))),
                               ((pb16[i], BS((tm, PL), lambda i, j: (i, 0))), (w["w_pp"], BS((None, PL, DS), lambda i, j: (j, 0, 0))))],
                              [(h2, dso)], [((T, D), f32, dso), ((T, D), bf16, dso), ((T, D), bf16, dso)], epilogue=ple_epi)
        sv.update(xn1=xn1, P=P, Pf=Pf, fb=fb, c_col=c_col, c_row=c_row, qn=qn, kn=kn, vb=vb, o32=o32, lse=lse,
                  mix=mix, h1=h1, xn2=xn2, Gt=Gt, Ut=Ut, act=act, h2=h2, xn3=xn3, gate=gate, e=e)
        saved.append(sv)
        h = h3

    dh, loss_tile = _loss_grad("loss", h, loss_target.reshape(T, D))

    small_g = {n: [None] * L for n in SMALL}
    big_out = {}
    for i in reversed(range(L)):
        w, sv = Wf[i], saved[i]
        de, dz = _ple_bwd_elem(f"ple_bw_{i}", dh, sv["gate"], sv["e"])
        (d_wpp,) = _matmul(f"d_wpp_{i}", "tn", (N_CHIPS, 1),
                           [((pb16[i], BS((T, PL), lambda i, j: (0, 0))), (de, BS((T, DS), lambda i, j: (0, i))))], [],
                           [((N_CHIPS, PL, DS), f32, BS((None, PL, DS), lambda i, j: (i, 0, 0)))])
        (d_wpg,) = _matmul(f"d_wpg_{i}", "tn", (D // tnd, D // tnd),
                           [((sv["xn3"], BS((T, tnd), lambda i, j: (0, i))), (dz, BS((T, tnd), lambda i, j: (0, j))))], [],
                           [((D, D), f32, BS((tnd, tnd), lambda i, j: (i, j)))])
        (dxn3,) = _matmul(f"d_xn3_{i}", "nt", (T // tm, D // tnd),
                          [((dz, BS((tm, D), lambda i, j: (i, 0))), (w["w_pg"], BS((tnd, D), lambda i, j: (j, 0))))], [],
                          [((T, D), f32, BS((tm, tnd), lambda i, j: (i, j)))])
        dh2, dh2b, dg = _rms_bwd(f"rms3_bw_{i}", dxn3, sv["h2"], norm_ple[i], dh)
        small_g["norm_ple"][i] = dg.reshape(D)

        def dffn_epi(accs, ex):
            da = accs[0]
            g_, u_ = ex[0].astype(f32), ex[1].astype(f32)
            sg = jax.nn.sigmoid(g_)
            return da * u_ * (sg * (1.0 + g_ * (1.0 - sg))), da * (g_ * sg)

        ffo = BS((tm, FS), lambda i, j: (i, j))
        dG, dU = _matmul(f"d_act_{i}", "nt", (T // tm, N_CHIPS),
                         [((dh2b, BS((tm, D), lambda i, j: (i, 0))), (w["w_down"], BS((FS, D), lambda i, j: (j, 0))))],
                         [(sv["Gt"], ffo), (sv["Ut"], ffo)], [((T, FF), bf16, ffo)] * 2, epilogue=dffn_epi)
        (d_wd,) = _matmul(f"d_wd_{i}", "tn", (FF // tkf, D // tnd),
                          [((sv["act"], BS((T, tkf), lambda i, j: (0, i))), (dh2b, BS((T, tnd), lambda i, j: (0, j))))], [],
                          [((FF, D), f32, BS((tkf, tnd), lambda i, j: (i, j)))])
        gu_out = BS((None, tnd, FS), lambda i, j: (j, i, 0))
        d_wg, d_wu = _matmul(f"d_wgu_{i}", "tn", (D // tnd, N_CHIPS),
                             [((sv["xn2"], BS((T, tnd), lambda i, j: (0, i))), (dG, BS((T, FS), lambda i, j: (0, j)))),
                              ((sv["xn2"], BS((T, tnd), lambda i, j: (0, i))), (dU, BS((T, FS), lambda i, j: (0, j))))], [],
                             [((N_CHIPS, D, FS), f32, gu_out)] * 2, epilogue=lambda accs, ex: (accs[0], accs[1]))
        tm2 = _tile(T, 256, 16)
        (dxn2,) = _matmul(f"d_xn2_{i}", "nt", (T // tm2, D // tnd),
                          [((dG, BS((tm2, FF), lambda i, j: (i, 0))), (w["w_gate"], BS((N_CHIPS, tnd, FS), lambda i, j: (0, j, 0)))),
                           ((dU, BS((tm2, FF), lambda i, j: (i, 0))), (w["w_up"], BS((N_CHIPS, tnd, FS), lambda i, j: (0, j, 0))))], [],
                          [((T, D), f32, BS((tm2, tnd), lambda i, j: (i, j)))])
        dh1, dh1b, dg = _rms_bwd(f"rms2_bw_{i}", dxn2, sv["h1"], norm_ffn[i], dh2)
        small_g["norm_ffn"][i] = dg.reshape(D)
        (dmix,) = _matmul(f"d_mix_{i}", "nt", (T // tm, D // tnd),
                          [((dh1b, BS((tm, D), lambda i, j: (i, 0))), (w["w_out"], BS((tnd, D), lambda i, j: (j, 0))))], [],
                          [((T, D), f32, BS((tm, tnd), lambda i, j: (i, j)))])
        (d_wout,) = _matmul(f"d_wout_{i}", "tn", (D // tnd, D // tnd),
                            [((sv["mix"], BS((T, tnd), lambda i, j: (0, i))), (dh1b, BS((T, tnd), lambda i, j: (0, j))))], [],
                            [((D, D), f32, BS((tnd, tnd), lambda i, j: (i, j)))])
        qg, kg = q_norm[i].reshape(1, HEAD), k_norm[i].reshape(1, HEAD)
        dq, dk, dv, dc_row, dqg, dkg = _attn_bwd(f"attn_bw_{i}", sv["qn"], sv["kn"], sv["vb"], sv["o32"], dmix,
                                                 sv["lse"], sv["c_col"], sv["c_row"], sv["P"], qg, kg, tb)
        small_g["q_norm"][i] = dqg.reshape(HEAD)
        small_g["k_norm"][i] = dkg.reshape(HEAD)
        dct = jnp.pad(dc_row.reshape(H, T), ((0, HEAD - H), (0, 0)))
        dPf, dfb = _fgate_bwd(f"fgate_bw_{i}", dct, sv["Pf"], sv["fb"])
        small_g["forget_bias"][i] = dfb[0, :H]
        gain = gmlp_v_norm[i].reshape(G, 1, HEAD)
        bs = gmlp_b_s[i].reshape(G, HEAD, 1)
        dgu, dgv, dws, dbs, dgain = _gmlp_bwd(f"gmlp_bw_{i}", sv["P"], dmix, gain, gmlp_w_s[i], bs, col_gu, col_gv,
                                              col_dg, Wd)
        small_g["gmlp_w_s"][i] = dws
        small_g["gmlp_b_s"][i] = dbs.reshape(G, HEAD)
        small_g["gmlp_v_norm"][i] = dgain.reshape(G, HEAD)
        ps = pool_scale[i].reshape(1, Wd)
        dxp, dpw, dps = _pool_bwd(f"pool_bw_{i}", sv["P"], dmix, pool_w[i], ps, col_xp, col_dp, Wd)
        small_g["pool_w"][i] = dpw
        small_g["pool_scale"][i] = dps.reshape(Wd)
        dP = jnp.concatenate([dq, dk, dv, dgu, dgv, dxp], axis=1)
        (d_wmain,) = _matmul(f"d_wmain_{i}", "tn", (D // tnd, NM // tn),
                             [((sv["xn1"], BS((T, tnd), lambda i, j: (0, i))), (dP, BS((T, tn), lambda i, j: (0, j))))], [],
                             [((D, NM), f32, BS((tnd, tn), lambda i, j: (i, j)))])
        (d_wf,) = _matmul(f"d_wf_{i}", "tn", (D // tnd, 1),
                          [((sv["xn1"], BS((T, tnd), lambda i, j: (0, i))), (dPf, BS((T, HEAD), lambda i, j: (0, 0))))], [],
                          [((D, HEAD), f32, BS((tnd, HEAD), lambda i, j: (i, 0)))])
        (dxn1,) = _matmul(f"d_xn1_{i}", "nt", (T // tm2, D // tnd),
                          [((dP, BS((tm2, NM), lambda i, j: (i, 0))), (w["w_main"], BS((tnd, NM), lambda i, j: (j, 0)))),
                           ((dPf, BS((tm2, HEAD), lambda i, j: (i, 0))), (w["w_f"], BS((tnd, HEAD), lambda i, j: (j, 0))))], [],
                          [((T, D), f32, BS((tm2, tnd), lambda i, j: (i, j)))])
        dh0, _, dg = _rms_bwd(f"rms1_bw_{i}", dxn1, sv["h0"], norm_mix[i], dh1)
        small_g["norm_mix"][i] = dg.reshape(D)
        dh = dh0

        d_win = jnp.concatenate([d_wmain[:, :3 * A], d_wf[:, :H], d_wmain[:, 3 * A:]], axis=1)
        grads4 = [d_win.reshape(D, N_CHIPS, DP4).transpose(1, 0, 2), d_wout.reshape(N_CHIPS, DS, D), d_wg, d_wu,
                  d_wd.reshape(N_CHIPS, FS, D), d_wpg.reshape(N_CHIPS, DS, D), d_wpp]
        red = _reduce_scatter(str(i), grads4, place)
        for n, r in zip(BIG, red):
            big_out[n] = _adamw_layer(f"adamw_{n}_{i}", i, W[n], M[n], V[n], r.reshape(W[n].shape[1:]), big_out.get(n))

    small_full = {n: jnp.stack(small_g[n]) for n in SMALL}
    n_small = sum(int(W[n].size) for n in SMALL) + 1
    rows8 = -(-n_small // (128 * 64)) * 8
    packed = _pack_rows([small_full[n] for n in SMALL] + [loss_tile[0, :1]], N_DEV * rows8)
    summed = _all_reduce_small("allreduce_small", packed.reshape(N_DEV, rows8, 128)).reshape(-1)
    grads, off = {}, 0
    for n in SMALL:
        grads[n] = summed[off:off + W[n].size].reshape(W[n].shape)
        off += W[n].size
    loss = summed[off]

    delta, new_m, new_v = {}, {}, {}
    wp, mp, vp = (_pack_rows([t[n] for n in SMALL], N_DEV * rows8) for t in (W, M, V))
    ds_, ms_, vs_ = _adamw("adamw_small", wp, summed.reshape(-1, 128), mp, vp)
    off = 0
    for n in SMALL:
        sz = W[n].size
        delta[n], new_m[n], new_v[n] = (t.reshape(-1)[off:off + sz].reshape(W[n].shape) for t in (ds_, ms_, vs_))
        off += sz
    for n in BIG:
        grads[n], delta[n], new_m[n], new_v[n] = big_out[n]

    return (loss, dh.reshape(1, T, D), *[grads[n] for n in WEIGHTS], *[delta[n] for n in WEIGHTS],
            *[new_m[n] for n in WEIGHTS], *[new_v[n] for n in WEIGHTS])
```

```python
import jax
import jax.numpy as jnp
from jax import lax
from jax.experimental import pallas as pl
from jax.experimental.pallas import tpu as pltpu

f32, bf16 = jnp.float32, jnp.bfloat16
S = jax.ShapeDtypeStruct
BS = pl.BlockSpec
ANY = pl.BlockSpec(memory_space=pl.ANY)
MESH = pl.DeviceIdType.MESH

EPS = 1e-6
HEAD = 128
POOL_WINDOWS = (2, 4, 8, 16)
NEG = -1e30
N_CHIPS = 4
N_DEV = 8
VMEM_LIMIT = 56 * 1024 * 1024

ADAM_LR, ADAM_B1, ADAM_B2, ADAM_EPS, ADAM_WD, ADAM_STEP = 0.001, 0.9, 0.999, 1e-08, 0.01, 10

BIG = ("w_in", "w_out", "w_ffn_gate", "w_ffn_up", "w_ffn_down", "w_ple_gate", "w_ple_proj")
SMALL = ("norm_mix", "q_norm", "k_norm", "forget_bias", "gmlp_v_norm", "gmlp_w_s", "gmlp_b_s", "pool_w",
         "pool_scale", "norm_ffn", "norm_ple")
WEIGHTS = ("norm_mix", "w_in", "q_norm", "k_norm", "forget_bias", "gmlp_v_norm", "gmlp_w_s", "gmlp_b_s", "pool_w",
           "pool_scale", "w_out", "norm_ffn", "w_ffn_gate", "w_ffn_up", "w_ffn_down", "norm_ple", "w_ple_gate",
           "w_ple_proj")


def _tile(n, target, mult):
    best = None
    for t in range(mult, min(n, target) + 1, mult):
        if n % t == 0:
            best = t
    return best if best is not None else n


def _params(**kw):
    return pltpu.CompilerParams(vmem_limit_bytes=VMEM_LIMIT, **kw)


def _dot(a, b, kind):
    dims = {"nn": (((1,), (0,)), ((), ())), "nt": (((1,), (1,)), ((), ())), "tn": (((0,), (0,)), ((), ()))}[kind]
    return lax.dot_general(a.astype(bf16), b.astype(bf16), dims, preferred_element_type=f32)


def _my_place():
    x, y, c = lax.axis_index("x"), lax.axis_index("y"), lax.axis_index("c")
    return x, y, c, 2 * x + y


def _matmul(name, kind, grid, pairs, extras, outs, epilogue=None):
    n_p, n_e = len(pairs), len(extras)

    def body(*refs):
        a_refs, b_refs = refs[:n_p], refs[n_p:2 * n_p]
        e_refs = refs[2 * n_p:2 * n_p + n_e]
        o_refs = refs[2 * n_p + n_e:]
        accs = []
        for a_ref, b_ref in zip(a_refs, b_refs):
            if len(b_ref.shape) == 3:
                w = b_ref.shape[2]
                acc = None
                for s in range(b_ref.shape[0]):
                    d = _dot(a_ref[:, s * w:(s + 1) * w], b_ref[s], kind)
                    acc = d if acc is None else acc + d
            else:
                acc = _dot(a_ref[...], b_ref[...], kind)
            accs.append(acc)
        if epilogue is None:
            res = accs[0]
            for t in accs[1:]:
                res = res + t
            res = (res,)
        else:
            res = epilogue(accs, [e[...] for e in e_refs])
        for o_ref, o in zip(o_refs, res):
            o_ref[...] = o.astype(o_ref.dtype)

    in_arrays = [p[0][0] for p in pairs] + [p[1][0] for p in pairs] + [e[0] for e in extras]
    in_specs = [p[0][1] for p in pairs] + [p[1][1] for p in pairs] + [e[1] for e in extras]
    res = pl.pallas_call(
        body, name=name, grid=grid, in_specs=in_specs,
        out_specs=[o[2] for o in outs], out_shape=[S(o[0], o[1]) for o in outs],
        compiler_params=_params(),
    )(*in_arrays)
    return res


def _rms_fwd(name, x, g):
    T, D = x.shape
    tr = _tile(T, 256, 8)

    def body(x_ref, g_ref, o_ref):
        xv = x_ref[...]
        r = lax.rsqrt(jnp.mean(xv * xv, axis=-1, keepdims=True) + EPS)
        o_ref[...] = (xv * r * g_ref[...]).astype(o_ref.dtype)

    return pl.pallas_call(
        body, name=name, grid=(T // tr,),
        in_specs=[BS((tr, D), lambda i: (i, 0)), BS((1, D), lambda i: (0, 0))],
        out_specs=BS((tr, D), lambda i: (i, 0)), out_shape=S((T, D), bf16),
    )(x, g.reshape(1, D))


def _rms_bwd(name, dxn, x, g, dres):
    T, D = x.shape
    tr = _tile(T, 256, 8)

    def body(dxn_ref, x_ref, g_ref, dres_ref, dx_ref, dxb_ref, dg_ref):
        i = pl.program_id(0)
        xv = x_ref[...]
        r = lax.rsqrt(jnp.mean(xv * xv, axis=-1, keepdims=True) + EPS)
        xh = xv * r
        dxn_v = dxn_ref[...]
        dxh = dxn_v * g_ref[...]
        dx = dres_ref[...] + r * (dxh - xh * jnp.mean(dxh * xh, axis=-1, keepdims=True))
        dx_ref[...] = dx
        dxb_ref[...] = dx.astype(bf16)
        part = jnp.sum(dxn_v * xh, axis=0, keepdims=True)

        @pl.when(i == 0)
        def _():
            dg_ref[...] = part

        @pl.when(i > 0)
        def _():
            dg_ref[...] += part

    row = BS((tr, D), lambda i: (i, 0))
    vec = BS((1, D), lambda i: (0, 0))
    return pl.pallas_call(
        body, name=name, grid=(T // tr,),
        in_specs=[row, row, vec, row], out_specs=[row, row, vec],
        out_shape=[S((T, D), f32), S((T, D), bf16), S((1, D), f32)],
    )(dxn, x, g.reshape(1, D), dres)


def _loss_grad(name, y, tgt):
    T, D = y.shape
    tr = _tile(T, 256, 8)

    def body(y_ref, t_ref, dy_ref, l_ref):
        i = pl.program_id(0)
        e = y_ref[...] - t_ref[...]
        dy_ref[...] = e * (1.0 / D)
        part = 0.5 * jnp.sum(jnp.mean(e * e, axis=-1, keepdims=True), axis=0, keepdims=True)

        @pl.when(i == 0)
        def _():
            l_ref[...] = jnp.zeros_like(l_ref)

        l_ref[...] += jnp.broadcast_to(part, l_ref.shape)

    row = BS((tr, D), lambda i: (i, 0))
    return pl.pallas_call(
        body, name=name, grid=(T // tr,), in_specs=[row, row],
        out_specs=[row, BS((8, 128), lambda i: (0, 0))],
        out_shape=[S((T, D), f32), S((8, 128), f32)],
    )(y, tgt)


def _ple_bwd_elem(name, dh, gate, e):
    T, D = dh.shape
    tr = _tile(T, 256, 16)

    def body(dh_ref, g_ref, e_ref, de_ref, dz_ref):
        d = dh_ref[...]
        g = g_ref[...].astype(f32)
        de_ref[...] = (d * g).astype(bf16)
        dz_ref[...] = (d * e_ref[...].astype(f32) * g * (1.0 - g)).astype(bf16)

    row = BS((tr, D), lambda i: (i, 0))
    return pl.pallas_call(
        body, name=name, grid=(T // tr,), in_specs=[row, row, row], out_specs=[row, row],
        out_shape=[S((T, D), bf16), S((T, D), bf16)],
    )(dh, gate, e)


def _gelu_and_grad(x):
    k0, k1 = 0.7978845608028654, 0.044715
    th = jnp.tanh(k0 * (x + k1 * x * x * x))
    val = 0.5 * x * (1.0 + th)
    grad = 0.5 * (1.0 + th) + 0.5 * x * (1.0 - th * th) * (k0 * (1.0 + 3.0 * k1 * x * x))
    return val, grad


def _fgate_fwd(name, pf, fb):
    T = pf.shape[0]

    def body(pf_ref, fb_ref, c_ref, ct_ref):
        xv = jax.nn.log_sigmoid(pf_ref[...] + fb_ref[...])
        row = lax.broadcasted_iota(jnp.int32, xv.shape, 0)
        s = 1
        while s < T:
            xv = xv + jnp.where(row >= s, pltpu.roll(xv, s, 0), 0.0)
            s *= 2
        c_ref[...] = xv
        ct_ref[...] = xv.T

    return pl.pallas_call(body, name=name, out_shape=[S((T, HEAD), f32), S((HEAD, T), f32)])(pf, fb)


def _fgate_bwd(name, dct, pf, fb):
    T = pf.shape[0]

    def body(dct_ref, pf_ref, fb_ref, dpf_ref, dfb_ref):
        xv = dct_ref[...].T
        row = lax.broadcasted_iota(jnp.int32, xv.shape, 0)
        s = 1
        while s < T:
            xv = xv + jnp.where(row + s < T, pltpu.roll(xv, T - s, 0), 0.0)
            s *= 2
        df = xv * jax.nn.sigmoid(-(pf_ref[...] + fb_ref[...]))
        dpf_ref[...] = df.astype(bf16)
        dfb_ref[...] = jnp.sum(df, axis=0, keepdims=True)

    return pl.pallas_call(body, name=name, out_shape=[S((T, HEAD), bf16), S((1, HEAD), f32)])(dct, pf, fb)


def _qk_norm(name, P, qg, kg, A):
    T = P.shape[0]
    tr = _tile(T, 256, 16)
    n_heads = A // HEAD

    def body(q_ref, k_ref, v_ref, qg_ref, kg_ref, qn_ref, kn_ref, vb_ref):
        for h in range(n_heads):
            sl = slice(h * HEAD, (h + 1) * HEAD)
            for src, g_ref, dst in ((q_ref, qg_ref, qn_ref), (k_ref, kg_ref, kn_ref)):
                xv = src[:, sl]
                r = lax.rsqrt(jnp.mean(xv * xv, axis=-1, keepdims=True) + EPS)
                dst[:, sl] = (xv * r * g_ref[...]).astype(bf16)
        vb_ref[...] = v_ref[...].astype(bf16)

    vec = BS((1, HEAD), lambda i: (0, 0))
    out = BS((tr, A), lambda i: (i, 0))
    return pl.pallas_call(
        body, name=name, grid=(T // tr,),
        in_specs=[BS((tr, A), lambda i: (i, 0)), BS((tr, A), lambda i: (i, 1)), BS((tr, A), lambda i: (i, 2)), vec, vec],
        out_specs=[out, out, out], out_shape=[S((T, A), bf16)] * 3,
    )(P, P, P, qg, kg)


def _attn_fwd(name, qn, kn, vb, c_col, c_row, tb):
    T, A = qn.shape
    H = A // HEAD
    nb = T // tb
    scale = HEAD ** -0.5

    def body(q_ref, k_ref, v_ref, cq_ref, ck_ref, o_ref, o32_ref, lse_ref):
        i = pl.program_id(1)
        q = q_ref[...]
        cq = cq_ref[...]
        rows = i * tb + lax.broadcasted_iota(jnp.int32, (tb, tb), 0)
        lcols = lax.broadcasted_iota(jnp.int32, (tb, tb), 1)

        def step(j, carry):
            m, l, acc = carry
            koff = pl.multiple_of(j * tb, tb)
            k = k_ref[pl.ds(koff, tb), :]
            v = v_ref[pl.ds(koff, tb), :]
            s = _dot(q, k, "nt") * scale + (cq - ck_ref[j])
            s = jnp.where(rows >= koff + lcols, s, NEG)
            m_new = jnp.maximum(m, jnp.max(s, axis=-1, keepdims=True))
            alpha = jnp.exp(m - m_new)
            p = jnp.exp(s - m_new)
            l = l * alpha + jnp.sum(p, axis=-1, keepdims=True)
            acc = acc * alpha + _dot(p, v, "nn")
            return m_new, l, acc

        init = (jnp.full((tb, 1), NEG, f32), jnp.zeros((tb, 1), f32), jnp.zeros((tb, HEAD), f32))
        m, l, acc = lax.fori_loop(0, i + 1, step, init)
        o = acc / l
        o_ref[...] = o.astype(bf16)
        o32_ref[...] = o
        lse_ref[...] = m + jnp.log(l)

    return pl.pallas_call(
        body, name=name, grid=(H, nb),
        in_specs=[BS((tb, HEAD), lambda h, i: (i, h)), BS((T, HEAD), lambda h, i: (0, h)),
                  BS((T, HEAD), lambda h, i: (0, h)), BS((None, tb, 1), lambda h, i: (h, i, 0)),
                  BS((None, nb, 1, tb), lambda h, i: (h, 0, 0, 0))],
        out_specs=[BS((tb, HEAD), lambda h, i: (i, h)), BS((tb, HEAD), lambda h, i: (i, h)),
                   BS((None, tb, 1), lambda h, i: (h, i, 0))],
        out_shape=[S((T, A), bf16), S((T, A), f32), S((H, T, 1), f32)],
    )(qn, kn, vb, c_col, c_row)


def _attn_bwd(name, qn, kn, vb, o, dmix, lse, c_col, c_row, P, qg, kg, tb):
    T, A = qn.shape
    H = A // HEAD
    nb = T // tb
    scale = HEAD ** -0.5

    def body(q_ref, k_ref, v_ref, o_ref, do_ref, lse_ref, cq_ref, ck_ref, qraw_ref, kraw_ref, qg_ref, kg_ref,
             dq_out, dk_out, dv_out, dc_out, dqg_out, dkg_out, dq_acc, dk_acc, delta_s):
        h = pl.program_id(0)
        dq_acc[...] = jnp.zeros_like(dq_acc)
        delta_s[...] = jnp.sum(do_ref[...].astype(bf16).astype(f32) * o_ref[...], axis=-1, keepdims=True)
        lrows = lax.broadcasted_iota(jnp.int32, (tb, tb), 0)
        lcols = lax.broadcasted_iota(jnp.int32, (tb, tb), 1)

        def kblock(j, _):
            koff = pl.multiple_of(j * tb, tb)
            k = k_ref[pl.ds(koff, tb), :]
            v = v_ref[pl.ds(koff, tb), :]
            ck = ck_ref[j]

            def qblock(i, carry):
                dk, dv, dc = carry
                qoff = pl.multiple_of(i * tb, tb)
                q = q_ref[pl.ds(qoff, tb), :]
                do = do_ref[pl.ds(qoff, tb), :].astype(bf16)
                s = _dot(q, k, "nt") * scale + (cq_ref[pl.ds(qoff, tb), :] - ck)
                s = jnp.where(qoff + lrows >= koff + lcols, s, NEG)
                p = jnp.exp(s - lse_ref[pl.ds(qoff, tb), :])
                dv = dv + _dot(p, do, "tn")
                dp = _dot(do, v, "nt")
                ds = p * (dp - delta_s[pl.ds(qoff, tb), :])
                dc = dc - jnp.sum(ds, axis=0, keepdims=True)
                dsb = (ds * scale).astype(bf16)
                dk = dk + _dot(dsb, q, "tn")
                dq_acc[pl.ds(qoff, tb), :] += _dot(dsb, k, "nn")
                return dk, dv, dc

            init = (jnp.zeros((tb, HEAD), f32), jnp.zeros((tb, HEAD), f32), jnp.zeros((1, tb), f32))
            dk, dv, dc = lax.fori_loop(j, nb, qblock, init)
            dk_acc[pl.ds(koff, tb), :] = dk
            dv_out[pl.ds(koff, tb), :] = dv.astype(bf16)
            dc_out[j] = dc
            return 0

        lax.fori_loop(0, nb, kblock, 0)

        for raw_ref, g_ref, acc_ref, d_out, dg_out in ((qraw_ref, qg_ref, dq_acc, dq_out, dqg_out),
                                                       (kraw_ref, kg_ref, dk_acc, dk_out, dkg_out)):
            xv = raw_ref[...]
            r = lax.rsqrt(jnp.mean(xv * xv, axis=-1, keepdims=True) + EPS)
            xh = xv * r
            dn = acc_ref[...]
            dxh = dn * g_ref[...]
            d_out[...] = (r * (dxh - xh * jnp.mean(dxh * xh, axis=-1, keepdims=True))).astype(bf16)
            part = jnp.sum(dn * xh, axis=0, keepdims=True)

            @pl.when(h == 0)
            def _():
                dg_out[...] = part

            @pl.when(h > 0)
            def _():
                dg_out[...] += part

    head = lambda off: BS((T, HEAD), lambda h: (0, off + h))
    vec = BS((1, HEAD), lambda h: (0, 0))
    return pl.pallas_call(
        body, name=name, grid=(H,),
        in_specs=[head(0), head(0), head(0), head(0), head(0),
                  BS((None, T, 1), lambda h: (h, 0, 0)), BS((None, T, 1), lambda h: (h, 0, 0)),
                  BS((None, nb, 1, tb), lambda h: (h, 0, 0, 0)), head(0), head(H), vec, vec],
        out_specs=[head(0), head(0), head(0), BS((None, nb, 1, tb), lambda h: (h, 0, 0, 0)), vec, vec],
        out_shape=[S((T, A), bf16)] * 3 + [S((H, nb, 1, tb), f32), S((1, HEAD), f32), S((1, HEAD), f32)],
        scratch_shapes=[pltpu.VMEM((T, HEAD), f32), pltpu.VMEM((T, HEAD), f32), pltpu.VMEM((T, 1), f32)],
        compiler_params=_params(),
    )(qn, kn, vb, o, dmix, lse, c_col, c_row, P, P, qg, kg)


def _gmlp_fwd(name, P, gain, ws, b, col_u, col_v, Wd):
    T = P.shape[0]
    G = Wd // HEAD
    tr = _tile(T, 512, HEAD)

    def body(u_ref, v_ref, gain_ref, ws_ref, b_ref, y_ref):
        tril = lax.broadcasted_iota(jnp.int32, (HEAD, HEAD), 0) >= lax.broadcasted_iota(jnp.int32, (HEAD, HEAD), 1)
        wm = jnp.where(tril, ws_ref[...], 0.0).astype(bf16)
        for n in range(tr // HEAD):
            rows = slice(n * HEAD, (n + 1) * HEAD)
            u = jax.nn.gelu(u_ref[rows, :])
            a = jax.nn.gelu(v_ref[rows, :])
            r = lax.rsqrt(jnp.mean(a * a, axis=-1, keepdims=True) + EPS)
            vn = a * r * gain_ref[...]
            mixed = _dot(wm, vn, "nn") + b_ref[...]
            y_ref[rows, :] = (u * mixed).astype(bf16)

    return pl.pallas_call(
        body, name=name, grid=(G, T // tr),
        in_specs=[BS((tr, HEAD), lambda g, i: (i, col_u + g)), BS((tr, HEAD), lambda g, i: (i, col_v + g)),
                  BS((None, 1, HEAD), lambda g, i: (g, 0, 0)), BS((None, HEAD, HEAD), lambda g, i: (g, 0, 0)),
                  BS((None, HEAD, 1), lambda g, i: (g, 0, 0))],
        out_specs=BS((tr, HEAD), lambda g, i: (i, g)), out_shape=S((T, Wd), bf16),
    )(P, P, gain, ws, b)


def _gmlp_bwd(name, P, dmix, gain, ws, b, col_u, col_v, col_dy, Wd):
    T = P.shape[0]
    G = Wd // HEAD
    tr = _tile(T, 512, HEAD)

    def body(u_ref, v_ref, dy_ref, gain_ref, ws_ref, b_ref, du_ref, dv_ref, dws_ref, db_ref, dgain_ref):
        i = pl.program_id(1)
        tril = lax.broadcasted_iota(jnp.int32, (HEAD, HEAD), 0) >= lax.broadcasted_iota(jnp.int32, (HEAD, HEAD), 1)
        wm = jnp.where(tril, ws_ref[...], 0.0).astype(bf16)
        gain_v = gain_ref[...]
        dw = jnp.zeros((HEAD, HEAD), f32)
        db = jnp.zeros((HEAD, 1), f32)
        dgain = jnp.zeros((1, HEAD), f32)
        for n in range(tr // HEAD):
            rows = slice(n * HEAD, (n + 1) * HEAD)
            u, du_dx = _gelu_and_grad(u_ref[rows, :])
            a, da_dx = _gelu_and_grad(v_ref[rows, :])
            dy = dy_ref[rows, :]
            r = lax.rsqrt(jnp.mean(a * a, axis=-1, keepdims=True) + EPS)
            ah = a * r
            vnb = (ah * gain_v).astype(bf16)
            mixed = _dot(wm, vnb, "nn") + b_ref[...]
            dm = dy * u
            dmb = dm.astype(bf16)
            du_ref[rows, :] = (dy * mixed * du_dx).astype(bf16)
            db = db + jnp.sum(dm, axis=1, keepdims=True)
            dw = dw + _dot(dmb, vnb, "nt")
            dvn = _dot(wm, dmb, "tn")
            dgain = dgain + jnp.sum(dvn * ah, axis=0, keepdims=True)
            dah = dvn * gain_v
            da = r * (dah - ah * jnp.mean(dah * ah, axis=-1, keepdims=True))
            dv_ref[rows, :] = (da * da_dx).astype(bf16)
        dw = jnp.where(tril, dw, 0.0)

        @pl.when(i == 0)
        def _():
            dws_ref[...] = dw
            db_ref[...] = db
            dgain_ref[...] = dgain

        @pl.when(i > 0)
        def _():
            dws_ref[...] += dw
            db_ref[...] += db
            dgain_ref[...] += dgain

    out = BS((tr, HEAD), lambda g, i: (i, g))
    return pl.pallas_call(
        body, name=name, grid=(G, T // tr),
        in_specs=[BS((tr, HEAD), lambda g, i: (i, col_u + g)), BS((tr, HEAD), lambda g, i: (i, col_v + g)),
                  BS((tr, HEAD), lambda g, i: (i, col_dy + g)),
                  BS((None, 1, HEAD), lambda g, i: (g, 0, 0)), BS((None, HEAD, HEAD), lambda g, i: (g, 0, 0)),
                  BS((None, HEAD, 1), lambda g, i: (g, 0, 0))],
        out_specs=[out, out, BS((None, HEAD, HEAD), lambda g, i: (g, 0, 0)), BS((None, HEAD, 1), lambda g, i: (g, 0, 0)),
                   BS((None, 1, HEAD), lambda g, i: (g, 0, 0))],
        out_shape=[S((T, Wd), bf16), S((T, Wd), bf16), S((G, HEAD, HEAD), f32), S((G, HEAD, 1), f32),
                   S((G, 1, HEAD), f32)],
    )(P, P, dmix, gain, ws, b)


def _trailing_window(xv, w, row):
    k = 1
    while k < w:
        xv = xv + jnp.where(row >= k, pltpu.roll(xv, k, 0), 0.0)
        k *= 2
    return xv


def _leading_window(xv, w, row, T):
    k = 1
    while k < w:
        xv = xv + jnp.where(row + k < T, pltpu.roll(xv, T - k, 0), 0.0)
        k *= 2
    return xv


def _pool_fwd(name, P, pw, ps, col_x, Wd):
    T = P.shape[0]
    Gp = Wd // HEAD

    def body(x_ref, pw_ref, ps_ref, y_ref):
        row = lax.broadcasted_iota(jnp.int32, (T, HEAD), 0)
        for g in range(Gp):
            w = POOL_WINDOWS[g]
            sl = slice(g * HEAD, (g + 1) * HEAD)
            xv = x_ref[:, sl]
            cnt = jnp.minimum(row + 1, w).astype(f32)
            d = _trailing_window(xv, w, row) / cnt - xv
            y_ref[:, sl] = (_dot(d, pw_ref[g], "nn") * ps_ref[:, sl]).astype(bf16)

    return pl.pallas_call(
        body, name=name, grid=(1,),
        in_specs=[BS((T, Wd), lambda i: (0, col_x)), BS((Gp, HEAD, HEAD), lambda i: (0, 0, 0)), BS((1, Wd), lambda i: (0, 0))],
        out_specs=BS((T, Wd), lambda i: (0, 0)), out_shape=S((T, Wd), bf16), compiler_params=_params(),
    )(P, pw, ps)


def _pool_bwd(name, P, dmix, pw, ps, col_x, col_dy, Wd):
    T = P.shape[0]
    Gp = Wd // HEAD

    def body(x_ref, dy_ref, pw_ref, ps_ref, dx_ref, dpw_ref, dps_ref):
        row = lax.broadcasted_iota(jnp.int32, (T, HEAD), 0)
        for g in range(Gp):
            w = POOL_WINDOWS[g]
            sl = slice(g * HEAD, (g + 1) * HEAD)
            xv = x_ref[:, sl]
            cnt = jnp.minimum(row + 1, w).astype(f32)
            d = (_trailing_window(xv, w, row) / cnt - xv).astype(bf16)
            pwb = pw_ref[g].astype(bf16)
            z = _dot(d, pwb, "nn")
            dy = dy_ref[:, sl]
            dps_ref[:, sl] = jnp.sum(dy * z, axis=0, keepdims=True)
            dzb = (dy * ps_ref[:, sl]).astype(bf16)
            dpw_ref[g] = _dot(d, dzb, "tn")
            dd = _dot(dzb, pwb, "nt")
            dx_ref[:, sl] = (_leading_window(dd / cnt, w, row, T) - dd).astype(bf16)

    return pl.pallas_call(
        body, name=name, grid=(1,),
        in_specs=[BS((T, Wd), lambda i: (0, col_x)), BS((T, Wd), lambda i: (0, col_dy)),
                  BS((Gp, HEAD, HEAD), lambda i: (0, 0, 0)), BS((1, Wd), lambda i: (0, 0))],
        out_specs=[BS((T, Wd), lambda i: (0, 0)), BS((Gp, HEAD, HEAD), lambda i: (0, 0, 0)), BS((1, Wd), lambda i: (0, 0))],
        out_shape=[S((T, Wd), bf16), S((Gp, HEAD, HEAD), f32), S((1, Wd), f32)], compiler_params=_params(),
    )(P, dmix, pw, ps)


def _adamw(name, w, g, m, v):
    R, C = w.shape
    lanes = -(-C // 128) * 128
    tr = _tile(R, max(8, (512 * 1024) // lanes // 8 * 8), 8)
    c1 = 1.0 - ADAM_B1 ** ADAM_STEP
    c2 = 1.0 - ADAM_B2 ** ADAM_STEP

    def body(w_ref, g_ref, m_ref, v_ref, d_ref, nm_ref, nv_ref):
        gv = g_ref[...]
        nm = ADAM_B1 * m_ref[...] + (1.0 - ADAM_B1) * gv
        nv = ADAM_B2 * v_ref[...] + (1.0 - ADAM_B2) * (gv * gv)
        d_ref[...] = -ADAM_LR * ((nm / c1) / (jnp.sqrt(nv / c2) + ADAM_EPS) + ADAM_WD * w_ref[...])
        nm_ref[...] = nm
        nv_ref[...] = nv

    blk = BS((tr, C), lambda i: (i, 0))
    return pl.pallas_call(
        body, name=name, grid=(R // tr,), in_specs=[blk] * 4, out_specs=[blk] * 3, out_shape=[S((R, C), f32)] * 3,
    )(w, g, m, v)


def _adamw_layer(name, layer, w_all, m_all, v_all, g, prev):
    L, R, C = w_all.shape
    lanes = -(-C // 128) * 128
    tr = _tile(R, max(8, (512 * 1024) // lanes // 8 * 8), 8)
    c1 = 1.0 - ADAM_B1 ** ADAM_STEP
    c2 = 1.0 - ADAM_B2 ** ADAM_STEP
    n_prev = 0 if prev is None else 4

    def body(w_ref, m_ref, v_ref, g_ref, *rest):
        go_ref, d_ref, nm_ref, nv_ref = rest[n_prev:]
        gv = g_ref[...]
        nm = ADAM_B1 * m_ref[...] + (1.0 - ADAM_B1) * gv
        nv = ADAM_B2 * v_ref[...] + (1.0 - ADAM_B2) * (gv * gv)
        d_ref[...] = -ADAM_LR * ((nm / c1) / (jnp.sqrt(nv / c2) + ADAM_EPS) + ADAM_WD * w_ref[...])
        nm_ref[...] = nm
        nv_ref[...] = nv
        go_ref[...] = gv

    slab = BS((None, tr, C), lambda r: (layer, r, 0))
    return pl.pallas_call(
        body, name=name, grid=(R // tr,),
        in_specs=[slab, slab, slab, BS((tr, C), lambda r: (r, 0))] + [ANY] * n_prev,
        out_specs=[slab] * 4, out_shape=[S((L, R, C), f32)] * 4,
        input_output_aliases={4 + k: k for k in range(n_prev)},
    )(w_all, m_all, v_all, g, *(prev or ()))


def _chip_of(k):
    return k // 2, k % 2


def _all_gather_chips(name, shards):
    n = len(shards)

    def body(*refs):
        ins, outs = refs[:n], refs[n:2 * n]
        send_sems, recv_sems = refs[2 * n:]
        x, y, c, j0 = _my_place()
        sends = []
        for a in range(n):
            half = ins[a].shape[0] // 2
            lo = c * half
            own = pltpu.make_async_remote_copy(
                src_ref=ins[a], dst_ref=outs[a].at[j0], send_sem=send_sems.at[a, 6], recv_sem=recv_sems.at[a, 6],
                device_id=(x, y, 1 - c), device_id_type=MESH)
            own.start()
            sends.append(own)
            for r in (1, 2, 3):
                px, py = _chip_of(j0 ^ r)
                cp = pltpu.make_async_remote_copy(
                    src_ref=ins[a].at[pl.ds(lo, half)], dst_ref=outs[a].at[j0, pl.ds(lo, half)],
                    send_sem=send_sems.at[a, r - 1], recv_sem=recv_sems.at[a, r - 1],
                    device_id=(px, py, c), device_id_type=MESH)
                cp.start()
                sends.append(cp)
        for a in range(n):
            half = ins[a].shape[0] // 2
            lo = c * half
            for r in (1, 2, 3):
                k = j0 ^ r
                px, py = _chip_of(k)
                landed = outs[a].at[k, pl.ds(lo, half)]
                pltpu.make_async_remote_copy(
                    src_ref=landed, dst_ref=landed, send_sem=send_sems.at[a, r - 1], recv_sem=recv_sems.at[a, r - 1],
                    device_id=(px, py, c), device_id_type=MESH).wait_recv()
                fw = pltpu.make_async_remote_copy(
                    src_ref=landed, dst_ref=landed, send_sem=send_sems.at[a, 2 + r], recv_sem=recv_sems.at[a, 2 + r],
                    device_id=(x, y, 1 - c), device_id_type=MESH)
                fw.start()
                sends.append(fw)
        for a in range(n):
            half = ins[a].shape[0] // 2
            for r in (1, 2, 3):
                other = outs[a].at[j0 ^ r, pl.ds((1 - c) * half, half)]
                pltpu.make_async_remote_copy(
                    src_ref=other, dst_ref=other, send_sem=send_sems.at[a, 2 + r], recv_sem=recv_sems.at[a, 2 + r],
                    device_id=(x, y, 1 - c), device_id_type=MESH).wait_recv()
            mine = outs[a].at[j0]
            pltpu.make_async_remote_copy(
                src_ref=mine, dst_ref=mine, send_sem=send_sems.at[a, 6], recv_sem=recv_sems.at[a, 6],
                device_id=(x, y, 1 - c), device_id_type=MESH).wait_recv()
        for cp in sends:
            cp.wait_send()

    return pl.pallas_call(
        body, name=name, in_specs=[ANY] * n, out_specs=[ANY] * n,
        out_shape=[S((N_CHIPS,) + s.shape, s.dtype) for s in shards],
        scratch_shapes=[pltpu.SemaphoreType.DMA((n, 7)), pltpu.SemaphoreType.DMA((n, 7))],
    )(*shards)


def _sibling_halves(name, grads):
    n = len(grads)

    def body(*refs):
        ins, outs = refs[:n], refs[n:2 * n]
        send_sems, recv_sems = refs[2 * n:]
        x, y, c, _ = _my_place()
        cps = []
        for a in range(n):
            half = ins[a].shape[1] // 2
            cp = pltpu.make_async_remote_copy(
                src_ref=ins[a].at[:, pl.ds((1 - c) * half, half), :], dst_ref=outs[a],
                send_sem=send_sems.at[a], recv_sem=recv_sems.at[a], device_id=(x, y, 1 - c), device_id_type=MESH)
            cp.start()
            cps.append(cp)
        for cp in cps:
            cp.wait_recv()
        for cp in cps:
            cp.wait_send()

    return pl.pallas_call(
        body, name=name, in_specs=[ANY] * n, out_specs=[ANY] * n,
        out_shape=[S((N_CHIPS, g.shape[1] // 2, g.shape[2]), g.dtype) for g in grads],
        scratch_shapes=[pltpu.SemaphoreType.DMA((n,)), pltpu.SemaphoreType.DMA((n,))],
    )(*grads)


def _chip_scatter(name, parts):
    n = len(parts)

    def body(*refs):
        ins, outs = refs[:n], refs[n:2 * n]
        send_sems, recv_sems = refs[2 * n:]
        x, y, c, j0 = _my_place()
        cps = []
        for a in range(n):
            for r in (1, 2, 3):
                k = j0 ^ r
                px, py = _chip_of(k)
                cp = pltpu.make_async_remote_copy(
                    src_ref=ins[a].at[k], dst_ref=outs[a].at[j0], send_sem=send_sems.at[a, r - 1],
                    recv_sem=recv_sems.at[a, r - 1], device_id=(px, py, c), device_id_type=MESH)
                cp.start()
                cps.append(cp)
        for a in range(n):
            for r in (1, 2, 3):
                k = j0 ^ r
                px, py = _chip_of(k)
                pltpu.make_async_remote_copy(
                    src_ref=outs[a].at[k], dst_ref=outs[a].at[k], send_sem=send_sems.at[a, r - 1],
                    recv_sem=recv_sems.at[a, r - 1], device_id=(px, py, c), device_id_type=MESH).wait_recv()
        for cp in cps:
            cp.wait_send()

    return pl.pallas_call(
        body, name=name, in_specs=[ANY] * n, out_specs=[ANY] * n,
        out_shape=[S(p.shape, p.dtype) for p in parts],
        scratch_shapes=[pltpu.SemaphoreType.DMA((n, 3)), pltpu.SemaphoreType.DMA((n, 3))],
    )(*parts)


def _sibling_join(name, sums):
    n = len(sums)

    def body(*refs):
        outs = refs[n:2 * n]
        send_sems, recv_sems = refs[2 * n:]
        x, y, c, _ = _my_place()
        cps = []
        for a in range(n):
            half = outs[a].shape[0] // 2
            mine = outs[a].at[pl.ds(c * half, half)]
            cp = pltpu.make_async_remote_copy(
                src_ref=mine, dst_ref=mine, send_sem=send_sems.at[a], recv_sem=recv_sems.at[a],
                device_id=(x, y, 1 - c), device_id_type=MESH)
            cp.start()
            cps.append(cp)
        for a in range(n):
            half = outs[a].shape[0] // 2
            theirs = outs[a].at[pl.ds((1 - c) * half, half)]
            pltpu.make_async_remote_copy(
                src_ref=theirs, dst_ref=theirs, send_sem=send_sems.at[a], recv_sem=recv_sems.at[a],
                device_id=(x, y, 1 - c), device_id_type=MESH).wait_recv()
        for cp in cps:
            cp.wait_send()

    return pl.pallas_call(
        body, name=name, in_specs=[ANY] * n, out_specs=[ANY] * n,
        out_shape=[S(s.shape, s.dtype) for s in sums],
        input_output_aliases={a: a for a in range(n)},
        scratch_shapes=[pltpu.SemaphoreType.DMA((n,)), pltpu.SemaphoreType.DMA((n,))],
    )(*sums)


def _all_reduce_small(name, g8):
    _, R, L = g8.shape

    def body(g_ref, out_ref, land, red, send1, recv1, send2, recv2):
        x, y, c, _ = _my_place()
        me = 4 * x + 2 * y + c
        peers = []
        for r in range(1, N_DEV):
            q = me ^ r
            peers.append((q, (q // 4, (q // 2) % 2, q % 2)))
        first = []
        for r, (q, dev) in enumerate(peers):
            cp = pltpu.make_async_remote_copy(src_ref=g_ref.at[q], dst_ref=land.at[me], send_sem=send1.at[r],
                                              recv_sem=recv1.at[r], device_id=dev, device_id_type=MESH)
            cp.start()
            first.append(cp)
        land[me] = g_ref[me]
        for r, (q, dev) in enumerate(peers):
            pltpu.make_async_remote_copy(src_ref=land.at[q], dst_ref=land.at[q], send_sem=send1.at[r],
                                         recv_sem=recv1.at[r], device_id=dev, device_id_type=MESH).wait_recv()
        acc = land[0]
        for d in range(1, N_DEV):
            acc = acc + land[d]
        red[...] = acc
        out_ref[me] = acc
        second = []
        for r, (q, dev) in enumerate(peers):
            cp = pltpu.make_async_remote_copy(src_ref=red, dst_ref=out_ref.at[me], send_sem=send2.at[r],
                                              recv_sem=recv2.at[r], device_id=dev, device_id_type=MESH)
            cp.start()
            second.append(cp)
        for r, (q, dev) in enumerate(peers):
            pltpu.make_async_remote_copy(src_ref=out_ref.at[q], dst_ref=out_ref.at[q], send_sem=send2.at[r],
                                         recv_sem=recv2.at[r], device_id=dev, device_id_type=MESH).wait_recv()
        for cp in first + second:
            cp.wait_send()

    vm = pl.BlockSpec(memory_space=pltpu.VMEM)
    return pl.pallas_call(
        body, name=name, in_specs=[vm], out_specs=vm, out_shape=S(g8.shape, f32),
        scratch_shapes=[pltpu.VMEM((N_DEV, R, L), f32), pltpu.VMEM((R, L), f32)]
        + [pltpu.SemaphoreType.DMA((N_DEV - 1,))] * 4,
        compiler_params=_params(),
    )(g8)


def _pair_sum(name, g4, sib, place):
    _, rows, cols = g4.shape
    half = rows // 2
    lanes = -(-cols // 128) * 128
    tr = _tile(half, max(16, (512 * 1024) // lanes // 16 * 16), 16)
    nb = half // tr

    def body(place_ref, g_ref, s_ref, pb_ref, own_ref):
        j = pl.program_id(1)
        t = g_ref[...] + s_ref[...]
        pb_ref[...] = t.astype(bf16)

        @pl.when(j == place_ref[1])
        def _():
            own_ref[...] = t

    grid_spec = pltpu.PrefetchScalarGridSpec(
        num_scalar_prefetch=1, grid=(nb, N_CHIPS),
        in_specs=[BS((None, tr, cols), lambda i, j, pr: (j, pr[0] * nb + i, 0)), BS((None, tr, cols), lambda i, j, pr: (j, i, 0))],
        out_specs=[BS((None, tr, cols), lambda i, j, pr: (j, i, 0)), BS((tr, cols), lambda i, j, pr: (i, 0))],
    )
    return pl.pallas_call(
        body, name=name, grid_spec=grid_spec,
        out_shape=[S((N_CHIPS, half, cols), bf16), S((half, cols), f32)],
    )(place, g4, sib)


def _chip_sum(name, own, got, place):
    half, cols = own.shape
    lanes = -(-cols // 128) * 128
    tr = _tile(half, max(16, (512 * 1024) // lanes // 16 * 16), 16)
    nb = half // tr

    def body(place_ref, own_ref, *rest):
        got_refs, o_ref = rest[:N_CHIPS], rest[N_CHIPS]
        j0 = place_ref[1]
        acc = None
        for k in range(N_CHIPS):
            t = jnp.where(j0 == k, own_ref[...], got_refs[k][...].astype(f32))
            acc = t if acc is None else acc + t
        o_ref[...] = acc

    def slot(k):
        return BS((None, tr, cols), lambda i, pr: (jnp.where(pr[1] == k, (k + 1) % N_CHIPS, k), i, 0))

    grid_spec = pltpu.PrefetchScalarGridSpec(
        num_scalar_prefetch=1, grid=(nb,),
        in_specs=[BS((tr, cols), lambda i, pr: (i, 0))] + [slot(k) for k in range(N_CHIPS)],
        out_specs=BS((tr, cols), lambda i, pr: (pr[0] * nb + i, 0)),
    )
    return pl.pallas_call(body, name=name, grid_spec=grid_spec, out_shape=S((2 * half, cols), f32))(
        place, own, got, got, got, got)


def _reduce_scatter(tag, grads, place):
    n = len(grads)
    sib = _sibling_halves(f"rs_a_{tag}", grads)
    pb, own = [], []
    for a in range(n):
        p, o = _pair_sum(f"rs_pair_{tag}_{a}", grads[a], sib[a], place)
        pb.append(p)
        own.append(o)
    got = _chip_scatter(f"rs_b_{tag}", pb)
    fin = [_chip_sum(f"rs_sum_{tag}_{a}", own[a], got[a], place) for a in range(n)]
    return _sibling_join(f"rs_c_{tag}", fin)


def _pack_rows(arrs, n_rows):
    flat = jnp.concatenate([a.reshape(-1).astype(f32) for a in arrs])
    return jnp.pad(flat, (0, n_rows * 128 - flat.shape[0])).reshape(n_rows, 128)


def kernel(x, p, norm_mix, w_in, q_norm, k_norm, forget_bias, gmlp_v_norm, gmlp_w_s, gmlp_b_s, pool_w, pool_scale, w_out, norm_ffn, w_ffn_gate, w_ffn_up, w_ffn_down, norm_ple, w_ple_gate, w_ple_proj, loss_target, m_norm_mix, m_w_in, m_q_norm, m_k_norm, m_forget_bias, m_gmlp_v_norm, m_gmlp_w_s, m_gmlp_b_s, m_pool_w, m_pool_scale, m_w_out, m_norm_ffn, m_w_ffn_gate, m_w_ffn_up, m_w_ffn_down, m_norm_ple, m_w_ple_gate, m_w_ple_proj, v_norm_mix, v_w_in, v_q_norm, v_k_norm, v_forget_bias, v_gmlp_v_norm, v_gmlp_w_s, v_gmlp_b_s, v_pool_w, v_pool_scale, v_w_out, v_norm_ffn, v_w_ffn_gate, v_w_ffn_up, v_w_ffn_down, v_norm_ple, v_w_ple_gate, v_w_ple_proj):
    W = dict(norm_mix=norm_mix, w_in=w_in, q_norm=q_norm, k_norm=k_norm, forget_bias=forget_bias,
             gmlp_v_norm=gmlp_v_norm, gmlp_w_s=gmlp_w_s, gmlp_b_s=gmlp_b_s, pool_w=pool_w, pool_scale=pool_scale,
             w_out=w_out, norm_ffn=norm_ffn, w_ffn_gate=w_ffn_gate, w_ffn_up=w_ffn_up, w_ffn_down=w_ffn_down,
             norm_ple=norm_ple, w_ple_gate=w_ple_gate, w_ple_proj=w_ple_proj)
    M = dict(norm_mix=m_norm_mix, w_in=m_w_in, q_norm=m_q_norm, k_norm=m_k_norm, forget_bias=m_forget_bias,
             gmlp_v_norm=m_gmlp_v_norm, gmlp_w_s=m_gmlp_w_s, gmlp_b_s=m_gmlp_b_s, pool_w=m_pool_w,
             pool_scale=m_pool_scale, w_out=m_w_out, norm_ffn=m_norm_ffn, w_ffn_gate=m_w_ffn_gate,
             w_ffn_up=m_w_ffn_up, w_ffn_down=m_w_ffn_down, norm_ple=m_norm_ple, w_ple_gate=m_w_ple_gate,
             w_ple_proj=m_w_ple_proj)
    V = dict(norm_mix=v_norm_mix, w_in=v_w_in, q_norm=v_q_norm, k_norm=v_k_norm, forget_bias=v_forget_bias,
             gmlp_v_norm=v_gmlp_v_norm, gmlp_w_s=v_gmlp_w_s, gmlp_b_s=v_gmlp_b_s, pool_w=v_pool_w,
             pool_scale=v_pool_scale, w_out=v_w_out, norm_ffn=v_norm_ffn, w_ffn_gate=v_w_ffn_gate,
             w_ffn_up=v_w_ffn_up, w_ffn_down=v_w_ffn_down, norm_ple=v_norm_ple, w_ple_gate=v_w_ple_gate,
             w_ple_proj=v_w_ple_proj)

    L = w_in.shape[0]
    _, T, D = x.shape
    A, Wd = D // 2, D // 4
    H = A // HEAD
    G = gmlp_w_s.shape[1]
    Gp = pool_w.shape[1]
    DP4 = w_in.shape[2]
    DP = N_CHIPS * DP4
    NM = 3 * A + 3 * Wd
    FS = w_ffn_gate.shape[2]
    FF = N_CHIPS * FS
    DS = D // N_CHIPS
    PL = p.shape[-1]
    assert Wd // G == HEAD and Wd // Gp == HEAD and DP == NM + H and H <= HEAD
    assert all(w & (w - 1) == 0 for w in POOL_WINDOWS[:Gp])
    tb = _tile(T, 256, HEAD)
    nb = T // tb
    tm = _tile(T, 512, 16)
    tn = _tile(NM, 512, 128)
    tnd = _tile(D, 512, 128)
    tkf = _tile(FF, 512, 128)
    c_ax, j0 = lax.axis_index("c"), 2 * lax.axis_index("x") + lax.axis_index("y")
    place = jnp.stack([c_ax, j0]).astype(jnp.int32)
    col_gu, col_gv, col_xp = 3 * A // HEAD, (3 * A + Wd) // HEAD, (3 * A + 2 * Wd) // Wd
    col_dg, col_dp = A // HEAD, (A + Wd) // Wd

    Wf = []
    for i in range(L):
        g = _all_gather_chips(f"ag_{i}", [W[n][i].astype(bf16) for n in BIG])
        win = g[0].transpose(1, 0, 2).reshape(D, DP)
        Wf.append(dict(
            w_main=jnp.concatenate([win[:, :3 * A], win[:, 3 * A + H:]], axis=1),
            w_f=jnp.pad(win[:, 3 * A:3 * A + H], ((0, 0), (0, HEAD - H))),
            w_out=g[1].reshape(D, D), w_gate=g[2], w_up=g[3], w_down=g[4].reshape(FF, D),
            w_pg=g[5].reshape(D, D), w_pp=g[6]))

    h = x.reshape(T, D)
    pb16 = p.reshape(L, T, PL).astype(bf16)
    saved = []
    full = lambda shp: BS(shp, lambda i, j: (0,) * len(shp))

    for i in range(L):
        w = Wf[i]
        sv = dict(h0=h)
        xn1 = _rms_fwd(f"rms1_{i}", h, norm_mix[i])
        (P,) = _matmul(f"proj_{i}", "nn", (T // tm, NM // tn),
                       [((xn1, BS((tm, D), lambda i, j: (i, 0))), (w["w_main"], BS((D, tn), lambda i, j: (0, j))))], [],
                       [((T, NM), f32, BS((tm, tn), lambda i, j: (i, j)))])
        (Pf,) = _matmul(f"projf_{i}", "nn", (T // tm, 1),
                        [((xn1, BS((tm, D), lambda i, j: (i, 0))), (w["w_f"], BS((D, HEAD), lambda i, j: (0, 0))))], [],
                        [((T, HEAD), f32, BS((tm, HEAD), lambda i, j: (i, 0)))])
        fb = jnp.pad(forget_bias[i], (0, HEAD - H)).reshape(1, HEAD)
        cc, ct = _fgate_fwd(f"fgate_{i}", Pf, fb)
        c_col = ct[:H].reshape(H, T, 1)
        c_row = ct[:H].reshape(H, nb, 1, tb)
        qg, kg = q_norm[i].reshape(1, HEAD), k_norm[i].reshape(1, HEAD)
        qn, kn, vb = _qk_norm(f"qknorm_{i}", P, qg, kg, A)
        y_attn, o32, lse = _attn_fwd(f"attn_{i}", qn, kn, vb, c_col, c_row, tb)
        gain = gmlp_v_norm[i].reshape(G, 1, HEAD)
        bs = gmlp_b_s[i].reshape(G, HEAD, 1)
        y_gmlp = _gmlp_fwd(f"gmlp_{i}", P, gain, gmlp_w_s[i], bs, col_gu, col_gv, Wd)
        ps = pool_scale[i].reshape(1, Wd)
        y_pool = _pool_fwd(f"pool_{i}", P, pool_w[i], ps, col_xp, Wd)
        mix = jnp.concatenate([y_attn, y_gmlp, y_pool], axis=1)
        (h1,) = _matmul(f"out_{i}", "nn", (T // tm, D // tnd),
                        [((mix, BS((tm, D), lambda i, j: (i, 0))), (w["w_out"], BS((D, tnd), lambda i, j: (0, j))))],
                        [(h, BS((tm, tnd), lambda i, j: (i, j)))],
                        [((T, D), f32, BS((tm, tnd), lambda i, j: (i, j)))],
                        epilogue=lambda accs, ex: (accs[0] + ex[0],))
        xn2 = _rms_fwd(f"rms2_{i}", h1, norm_ffn[i])

        def ffn_epi(accs, ex):
            g_, u_ = accs
            return g_, u_, g_ * jax.nn.sigmoid(g_) * u_

        ffo = BS((tm, FS), lambda i, j: (i, j))
        Gt, Ut, act = _matmul(f"ffn1_{i}", "nn", (T // tm, N_CHIPS),
                              [((xn2, BS((tm, D), lambda i, j: (i, 0))), (w["w_gate"], BS((None, D, FS), lambda i, j: (j, 0, 0)))),
                               ((xn2, BS((tm, D), lambda i, j: (i, 0))), (w["w_up"], BS((None, D, FS), lambda i, j: (j, 0, 0))))],
                              [], [((T, FF), bf16, ffo)] * 3, epilogue=ffn_epi)
        (h2,) = _matmul(f"ffn2_{i}", "nn", (T // tm, D // tnd),
                        [((act, BS((tm, FF), lambda i, j: (i, 0))), (w["w_down"], BS((FF, tnd), lambda i, j: (0, j))))],
                        [(h1, BS((tm, tnd), lambda i, j: (i, j)))],
                        [((T, D), f32, BS((tm, tnd), lambda i, j: (i, j)))],
                        epilogue=lambda accs, ex: (accs[0] + ex[0],))
        xn3 = _rms_fwd(f"rms3_{i}", h2, norm_ple[i])

        def ple_epi(accs, ex):
            gate = jax.nn.sigmoid(accs[0])
            return ex[0] + accs[1] * gate, gate, accs[1]

        dso = BS((tm, DS), lambda i, j: (i, j))
        h3, gate, e = _matmul(f"ple_{i}", "nn", (T // tm, N_CHIPS),
                              [((xn3, BS((tm, D), lambda i, j: (i, 0))), (w["w_pg"], BS((D, DS), lambda i, j: (0, j)))),
                               ((pb16[i], BS((tm, PL), lambda i, j: (i, 0))), (w["w_pp"], BS((None, PL, DS), lambda i, j: (j, 0, 0))))],
                              [(h2, dso)], [((T, D), f32, dso), ((T, D), bf16, dso), ((T, D), bf16, dso)], epilogue=ple_epi)
        sv.update(xn1=xn1, P=P, Pf=Pf, fb=fb, c_col=c_col, c_row=c_row, qn=qn, kn=kn, vb=vb, o32=o32, lse=lse,
                  mix=mix, h1=h1, xn2=xn2, Gt=Gt, Ut=Ut, act=act, h2=h2, xn3=xn3, gate=gate, e=e)
        saved.append(sv)
        h = h3

    dh, loss_tile = _loss_grad("loss", h, loss_target.reshape(T, D))

    small_g = {n: [None] * L for n in SMALL}
    big_out = {}
    for i in reversed(range(L)):
        w, sv = Wf[i], saved[i]
        de, dz = _ple_bwd_elem(f"ple_bw_{i}", dh, sv["gate"], sv["e"])
        (d_wpp,) = _matmul(f"d_wpp_{i}", "tn", (N_CHIPS, 1),
                           [((pb16[i], BS((T, PL), lambda i, j: (0, 0))), (de, BS((T, DS), lambda i, j: (0, i))))], [],
                           [((N_CHIPS, PL, DS), f32, BS((None, PL, DS), lambda i, j: (i, 0, 0)))])
        (d_wpg,) = _matmul(f"d_wpg_{i}", "tn", (D // tnd, D // tnd),
                           [((sv["xn3"], BS((T, tnd), lambda i, j: (0, i))), (dz, BS((T, tnd), lambda i, j: (0, j))))], [],
                           [((D, D), f32, BS((tnd, tnd), lambda i, j: (i, j)))])
        (dxn3,) = _matmul(f"d_xn3_{i}", "nt", (T // tm, D // tnd),
                          [((dz, BS((tm, D), lambda i, j: (i, 0))), (w["w_pg"], BS((tnd, D), lambda i, j: (j, 0))))], [],
                          [((T, D), f32, BS((tm, tnd), lambda i, j: (i, j)))])
        dh2, dh2b, dg = _rms_bwd(f"rms3_bw_{i}", dxn3, sv["h2"], norm_ple[i], dh)
        small_g["norm_ple"][i] = dg.reshape(D)

        def dffn_epi(accs, ex):
            da = accs[0]
            g_, u_ = ex[0].astype(f32), ex[1].astype(f32)
            sg = jax.nn.sigmoid(g_)
            return da * u_ * (sg * (1.0 + g_ * (1.0 - sg))), da * (g_ * sg)

        ffo = BS((tm, FS), lambda i, j: (i, j))
        dG, dU = _matmul(f"d_act_{i}", "nt", (T // tm, N_CHIPS),
                         [((dh2b, BS((tm, D), lambda i, j: (i, 0))), (w["w_down"], BS((FS, D), lambda i, j: (j, 0))))],
                         [(sv["Gt"], ffo), (sv["Ut"], ffo)], [((T, FF), bf16, ffo)] * 2, epilogue=dffn_epi)
        (d_wd,) = _matmul(f"d_wd_{i}", "tn", (FF // tkf, D // tnd),
                          [((sv["act"], BS((T, tkf), lambda i, j: (0, i))), (dh2b, BS((T, tnd), lambda i, j: (0, j))))], [],
                          [((FF, D), f32, BS((tkf, tnd), lambda i, j: (i, j)))])
        gu_out = BS((None, tnd, FS), lambda i, j: (j, i, 0))
        d_wg, d_wu = _matmul(f"d_wgu_{i}", "tn", (D // tnd, N_CHIPS),
                             [((sv["xn2"], BS((T, tnd), lambda i, j: (0, i))), (dG, BS((T, FS), lambda i, j: (0, j)))),
                              ((sv["xn2"], BS((T, tnd), lambda i, j: (0, i))), (dU, BS((T, FS), lambda i, j: (0, j))))], [],
                             [((N_CHIPS, D, FS), f32, gu_out)] * 2, epilogue=lambda accs, ex: (accs[0], accs[1]))
        tm2 = _tile(T, 256, 16)
        (dxn2,) = _matmul(f"d_xn2_{i}", "nt", (T // tm2, D // tnd),
                          [((dG, BS((tm2, FF), lambda i, j: (i, 0))), (w["w_gate"], BS((N_CHIPS, tnd, FS), lambda i, j: (0, j, 0)))),
                           ((dU, BS((tm2, FF), lambda i, j: (i, 0))), (w["w_up"], BS((N_CHIPS, tnd, FS), lambda i, j: (0, j, 0))))], [],
                          [((T, D), f32, BS((tm2, tnd), lambda i, j: (i, j)))])
        dh1, dh1b, dg = _rms_bwd(f"rms2_bw_{i}", dxn2, sv["h1"], norm_ffn[i], dh2)
        small_g["norm_ffn"][i] = dg.reshape(D)
        (dmix,) = _matmul(f"d_mix_{i}", "nt", (T // tm, D // tnd),
                          [((dh1b, BS((tm, D), lambda i, j: (i, 0))), (w["w_out"], BS((tnd, D), lambda i, j: (j, 0))))], [],
                          [((T, D), f32, BS((tm, tnd), lambda i, j: (i, j)))])
        (d_wout,) = _matmul(f"d_wout_{i}", "tn", (D // tnd, D // tnd),
                            [((sv["mix"], BS((T, tnd), lambda i, j: (0, i))), (dh1b, BS((T, tnd), lambda i, j: (0, j))))], [],
                            [((D, D), f32, BS((tnd, tnd), lambda i, j: (i, j)))])
        qg, kg = q_norm[i].reshape(1, HEAD), k_norm[i].reshape(1, HEAD)
        dq, dk, dv, dc_row, dqg, dkg = _attn_bwd(f"attn_bw_{i}", sv["qn"], sv["kn"], sv["vb"], sv["o32"], dmix,
                                                 sv["lse"], sv["c_col"], sv["c_row"], sv["P"], qg, kg, tb)
        small_g["q_norm"][i] = dqg.reshape(HEAD)
        small_g["k_norm"][i] = dkg.reshape(HEAD)
        dct = jnp.pad(dc_row.reshape(H, T), ((0, HEAD - H), (0, 0)))
        dPf, dfb = _fgate_bwd(f"fgate_bw_{i}", dct, sv["Pf"], sv["fb"])
        small_g["forget_bias"][i] = dfb[0, :H]
        gain = gmlp_v_norm[i].reshape(G, 1, HEAD)
        bs = gmlp_b_s[i].reshape(G, HEAD, 1)
        dgu, dgv, dws, dbs, dgain = _gmlp_bwd(f"gmlp_bw_{i}", sv["P"], dmix, gain, gmlp_w_s[i], bs, col_gu, col_gv,
                                              col_dg, Wd)
        small_g["gmlp_w_s"][i] = dws
        small_g["gmlp_b_s"][i] = dbs.reshape(G, HEAD)
        small_g["gmlp_v_norm"][i] = dgain.reshape(G, HEAD)
        ps = pool_scale[i].reshape(1, Wd)
        dxp, dpw, dps = _pool_bwd(f"pool_bw_{i}", sv["P"], dmix, pool_w[i], ps, col_xp, col_dp, Wd)
        small_g["pool_w"][i] = dpw
        small_g["pool_scale"][i] = dps.reshape(Wd)
        dP = jnp.concatenate([dq, dk, dv, dgu, dgv, dxp], axis=1)
        (d_wmain,) = _matmul(f"d_wmain_{i}", "tn", (D // tnd, NM // tn),
                             [((sv["xn1"], BS((T, tnd), lambda i, j: (0, i))), (dP, BS((T, tn), lambda i, j: (0, j))))], [],
                             [((D, NM), f32, BS((tnd, tn), lambda i, j: (i, j)))])
        (d_wf,) = _matmul(f"d_wf_{i}", "tn", (D // tnd, 1),
                          [((sv["xn1"], BS((T, tnd), lambda i, j: (0, i))), (dPf, BS((T, HEAD), lambda i, j: (0, 0))))], [],
                          [((D, HEAD), f32, BS((tnd, HEAD), lambda i, j: (i, 0)))])
        (dxn1,) = _matmul(f"d_xn1_{i}", "nt", (T // tm2, D // tnd),
                          [((dP, BS((tm2, NM), lambda i, j: (i, 0))), (w["w_main"], BS((tnd, NM), lambda i, j: (j, 0)))),
                           ((dPf, BS((tm2, HEAD), lambda i, j: (i, 0))), (w["w_f"], BS((tnd, HEAD), lambda i, j: (j, 0))))], [],
                          [((T, D), f32, BS((tm2, tnd), lambda i, j: (i, j)))])
        dh0, _, dg = _rms_bwd(f"rms1_bw_{i}", dxn1, sv["h0"], norm_mix[i], dh1)
        small_g["norm_mix"][i] = dg.reshape(D)
        dh = dh0

        d_win = jnp.concatenate([d_wmain[:, :3 * A], d_wf[:, :H], d_wmain[:, 3 * A:]], axis=1)
        grads4 = [d_win.reshape(D, N_CHIPS, DP4).transpose(1, 0, 2), d_wout.reshape(N_CHIPS, DS, D), d_wg, d_wu,
                  d_wd.reshape(N_CHIPS, FS, D), d_wpg.reshape(N_CHIPS, DS, D), d_wpp]
        red = _reduce_scatter(str(i), grads4, place)
        for n, r in zip(BIG, red):
            big_out[n] = _adamw_layer(f"adamw_{n}_{i}", i, W[n], M[n], V[n], r.reshape(W[n].shape[1:]), big_out.get(n))

    small_full = {n: jnp.stack(small_g[n]) for n in SMALL}
    n_small = sum(int(W[n].size) for n in SMALL) + 1
    rows8 = -(-n_small // (128 * 64)) * 8
    packed = _pack_rows([small_full[n] for n in SMALL] + [loss_tile[0, :1]], N_DEV * rows8)
    summed = _all_reduce_small("allreduce_small", packed.reshape(N_DEV, rows8, 128)).reshape(-1)
    grads, off = {}, 0
    for n in SMALL:
        grads[n] = summed[off:off + W[n].size].reshape(W[n].shape)
        off += W[n].size
    loss = summed[off]

    delta, new_m, new_v = {}, {}, {}
    wp, mp, vp = (_pack_rows([t[n] for n in SMALL], N_DEV * rows8) for t in (W, M, V))
    ds_, ms_, vs_ = _adamw("adamw_small", wp, summed.reshape(-1, 128), mp, vp)
    off = 0
    for n in SMALL:
        sz = W[n].size
        delta[n], new_m[n], new_v[n] = (t.reshape(-1)[off:off + sz].reshape(W[n].shape) for t in (ds_, ms_, vs_))
        off += sz
    for n in BIG:
        grads[n], delta[n], new_m[n], new_v[n] = big_out[n]

    return (loss, dh.reshape(1, T, D), *[grads[n] for n in WEIGHTS], *[delta[n] for n in WEIGHTS],
            *[new_m[n] for n in WEIGHTS], *[new_v[n] for n in WEIGHTS])
```

```python
import jax
import jax.numpy as jnp
from jax import lax
from jax.experimental import pallas as pl
from jax.experimental.pallas import tpu as pltpu

f32, bf16 = jnp.float32, jnp.bfloat16
S = jax.ShapeDtypeStruct
BS = pl.BlockSpec
ANY = pl.BlockSpec(memory_space=pl.ANY)
MESH = pl.DeviceIdType.MESH

EPS = 1e-6
HEAD = 128
POOL_WINDOWS = (2, 4, 8, 16)
NEG = -1e30
N_CHIPS = 4
N_DEV = 8
VMEM_LIMIT = 56 * 1024 * 1024

ADAM_LR, ADAM_B1, ADAM_B2, ADAM_EPS, ADAM_WD, ADAM_STEP = 0.001, 0.9, 0.999, 1e-08, 0.01, 10

BIG = ("w_in", "w_out", "w_ffn_gate", "w_ffn_up", "w_ffn_down", "w_ple_gate", "w_ple_proj")
SMALL = ("norm_mix", "q_norm", "k_norm", "forget_bias", "gmlp_v_norm", "gmlp_w_s", "gmlp_b_s", "pool_w",
         "pool_scale", "norm_ffn", "norm_ple")
WEIGHTS = ("norm_mix", "w_in", "q_norm", "k_norm", "forget_bias", "gmlp_v_norm", "gmlp_w_s", "gmlp_b_s", "pool_w",
           "pool_scale", "w_out", "norm_ffn", "w_ffn_gate", "w_ffn_up", "w_ffn_down", "norm_ple", "w_ple_gate",
           "w_ple_proj")


def _tile(n, target, mult):
    best = None
    for t in range(mult, min(n, target) + 1, mult):
        if n % t == 0:
            best = t
    return best if best is not None else n


def _params(**kw):
    return pltpu.CompilerParams(vmem_limit_bytes=VMEM_LIMIT, **kw)


def _dot(a, b, kind):
    dims = {"nn": (((1,), (0,)), ((), ())), "nt": (((1,), (1,)), ((), ())), "tn": (((0,), (0,)), ((), ()))}[kind]
    return lax.dot_general(a.astype(bf16), b.astype(bf16), dims, preferred_element_type=f32)


def _my_place():
    x, y, c = lax.axis_index("x"), lax.axis_index("y"), lax.axis_index("c")
    return x, y, c, 2 * x + y


def _matmul(name, kind, grid, pairs, extras, outs, epilogue=None):
    n_p, n_e = len(pairs), len(extras)

    def body(*refs):
        a_refs, b_refs = refs[:n_p], refs[n_p:2 * n_p]
        e_refs = refs[2 * n_p:2 * n_p + n_e]
        o_refs = refs[2 * n_p + n_e:]
        accs = []
        for a_ref, b_ref in zip(a_refs, b_refs):
            if len(b_ref.shape) == 3:
                w = b_ref.shape[2]
                acc = None
                for s in range(b_ref.shape[0]):
                    d = _dot(a_ref[:, s * w:(s + 1) * w], b_ref[s], kind)
                    acc = d if acc is None else acc + d
            else:
                acc = _dot(a_ref[...], b_ref[...], kind)
            accs.append(acc)
        if epilogue is None:
            res = accs[0]
            for t in accs[1:]:
                res = res + t
            res = (res,)
        else:
            res = epilogue(accs, [e[...] for e in e_refs])
        for o_ref, o in zip(o_refs, res):
            o_ref[...] = o.astype(o_ref.dtype)

    in_arrays = [p[0][0] for p in pairs] + [p[1][0] for p in pairs] + [e[0] for e in extras]
    in_specs = [p[0][1] for p in pairs] + [p[1][1] for p in pairs] + [e[1] for e in extras]
    res = pl.pallas_call(
        body, name=name, grid=grid, in_specs=in_specs,
        out_specs=[o[2] for o in outs], out_shape=[S(o[0], o[1]) for o in outs],
        compiler_params=_params(),
    )(*in_arrays)
    return res


def _rms_fwd(name, x, g):
    T, D = x.shape
    tr = _tile(T, 256, 8)

    def body(x_ref, g_ref, o_ref):
        xv = x_ref[...]
        r = lax.rsqrt(jnp.mean(xv * xv, axis=-1, keepdims=True) + EPS)
        o_ref[...] = (xv * r * g_ref[...]).astype(o_ref.dtype)

    return pl.pallas_call(
        body, name=name, grid=(T // tr,),
        in_specs=[BS((tr, D), lambda i: (i, 0)), BS((1, D), lambda i: (0, 0))],
        out_specs=BS((tr, D), lambda i: (i, 0)), out_shape=S((T, D), bf16),
    )(x, g.reshape(1, D))


def _rms_bwd(name, dxn, x, g, dres):
    T, D = x.shape
    tr = _tile(T, 256, 8)

    def body(dxn_ref, x_ref, g_ref, dres_ref, dx_ref, dxb_ref, dg_ref):
        i = pl.program_id(0)
        xv = x_ref[...]
        r = lax.rsqrt(jnp.mean(xv * xv, axis=-1, keepdims=True) + EPS)
        xh = xv * r
        dxn_v = dxn_ref[...]
        dxh = dxn_v * g_ref[...]
        dx = dres_ref[...] + r * (dxh - xh * jnp.mean(dxh * xh, axis=-1, keepdims=True))
        dx_ref[...] = dx
        dxb_ref[...] = dx.astype(bf16)
        part = jnp.sum(dxn_v * xh, axis=0, keepdims=True)

        @pl.when(i == 0)
        def _():
            dg_ref[...] = part

        @pl.when(i > 0)
        def _():
            dg_ref[...] += part

    row = BS((tr, D), lambda i: (i, 0))
    vec = BS((1, D), lambda i: (0, 0))
    return pl.pallas_call(
        body, name=name, grid=(T // tr,),
        in_specs=[row, row, vec, row], out_specs=[row, row, vec],
        out_shape=[S((T, D), f32), S((T, D), bf16), S((1, D), f32)],
    )(dxn, x, g.reshape(1, D), dres)


def _loss_grad(name, y, tgt):
    T, D = y.shape
    tr = _tile(T, 256, 8)

    def body(y_ref, t_ref, dy_ref, l_ref):
        i = pl.program_id(0)
        e = y_ref[...] - t_ref[...]
        dy_ref[...] = e * (1.0 / D)
        part = 0.5 * jnp.sum(jnp.mean(e * e, axis=-1, keepdims=True), axis=0, keepdims=True)

        @pl.when(i == 0)
        def _():
            l_ref[...] = jnp.zeros_like(l_ref)

        l_ref[...] += jnp.broadcast_to(part, l_ref.shape)

    row = BS((tr, D), lambda i: (i, 0))
    return pl.pallas_call(
        body, name=name, grid=(T // tr,), in_specs=[row, row],
        out_specs=[row, BS((8, 128), lambda i: (0, 0))],
        out_shape=[S((T, D), f32), S((8, 128), f32)],
    )(y, tgt)


def _ple_bwd_elem(name, dh, gate, e):
    T, D = dh.shape
    tr = _tile(T, 256, 16)

    def body(dh_ref, g_ref, e_ref, de_ref, dz_ref):
        d = dh_ref[...]
        g = g_ref[...].astype(f32)
        de_ref[...] = (d * g).astype(bf16)
        dz_ref[...] = (d * e_ref[...].astype(f32) * g * (1.0 - g)).astype(bf16)

    row = BS((tr, D), lambda i: (i, 0))
    return pl.pallas_call(
        body, name=name, grid=(T // tr,), in_specs=[row, row, row], out_specs=[row, row],
        out_shape=[S((T, D), bf16), S((T, D), bf16)],
    )(dh, gate, e)


def _gelu_and_grad(x):
    k0, k1 = 0.7978845608028654, 0.044715
    th = jnp.tanh(k0 * (x + k1 * x * x * x))
    val = 0.5 * x * (1.0 + th)
    grad = 0.5 * (1.0 + th) + 0.5 * x * (1.0 - th * th) * (k0 * (1.0 + 3.0 * k1 * x * x))
    return val, grad


def _fgate_fwd(name, pf, fb):
    T = pf.shape[0]

    def body(pf_ref, fb_ref, c_ref, ct_ref):
        xv = jax.nn.log_sigmoid(pf_ref[...] + fb_ref[...])
        row = lax.broadcasted_iota(jnp.int32, xv.shape, 0)
        s = 1
        while s < T:
            xv = xv + jnp.where(row >= s, pltpu.roll(xv, s, 0), 0.0)
            s *= 2
        c_ref[...] = xv
        ct_ref[...] = xv.T

    return pl.pallas_call(body, name=name, out_shape=[S((T, HEAD), f32), S((HEAD, T), f32)])(pf, fb)


def _fgate_bwd(name, dct, pf, fb):
    T = pf.shape[0]

    def body(dct_ref, pf_ref, fb_ref, dpf_ref, dfb_ref):
        xv = dct_ref[...].T
        row = lax.broadcasted_iota(jnp.int32, xv.shape, 0)
        s = 1
        while s < T:
            xv = xv + jnp.where(row + s < T, pltpu.roll(xv, T - s, 0), 0.0)
            s *= 2
        df = xv * jax.nn.sigmoid(-(pf_ref[...] + fb_ref[...]))
        dpf_ref[...] = df.astype(bf16)
        dfb_ref[...] = jnp.sum(df, axis=0, keepdims=True)

    return pl.pallas_call(body, name=name, out_shape=[S((T, HEAD), bf16), S((1, HEAD), f32)])(dct, pf, fb)


def _qk_norm(name, P, qg, kg, A):
    T = P.shape[0]
    tr = _tile(T, 256, 16)
    n_heads = A // HEAD

    def body(q_ref, k_ref, v_ref, qg_ref, kg_ref, qn_ref, kn_ref, vb_ref):
        for h in range(n_heads):
            sl = slice(h * HEAD, (h + 1) * HEAD)
            for src, g_ref, dst in ((q_ref, qg_ref, qn_ref), (k_ref, kg_ref, kn_ref)):
                xv = src[:, sl]
                r = lax.rsqrt(jnp.mean(xv * xv, axis=-1, keepdims=True) + EPS)
                dst[:, sl] = (xv * r * g_ref[...]).astype(bf16)
        vb_ref[...] = v_ref[...].astype(bf16)

    vec = BS((1, HEAD), lambda i: (0, 0))
    out = BS((tr, A), lambda i: (i, 0))
    return pl.pallas_call(
        body, name=name, grid=(T // tr,),
        in_specs=[BS((tr, A), lambda i: (i, 0)), BS((tr, A), lambda i: (i, 1)), BS((tr, A), lambda i: (i, 2)), vec, vec],
        out_specs=[out, out, out], out_shape=[S((T, A), bf16)] * 3,
    )(P, P, P, qg, kg)


def _attn_fwd(name, qn, kn, vb, c_col, c_row, tb):
    T, A = qn.shape
    H = A // HEAD
    nb = T // tb
    scale = HEAD ** -0.5

    def body(q_ref, k_ref, v_ref, cq_ref, ck_ref, o_ref, o32_ref, lse_ref):
        i = pl.program_id(1)
        q = q_ref[...]
        cq = cq_ref[...]
        rows = i * tb + lax.broadcasted_iota(jnp.int32, (tb, tb), 0)
        lcols = lax.broadcasted_iota(jnp.int32, (tb, tb), 1)

        def step(j, carry):
            m, l, acc = carry
            koff = pl.multiple_of(j * tb, tb)
            k = k_ref[pl.ds(koff, tb), :]
            v = v_ref[pl.ds(koff, tb), :]
            s = _dot(q, k, "nt") * scale + (cq - ck_ref[j])
            s = jnp.where(rows >= koff + lcols, s, NEG)
            m_new = jnp.maximum(m, jnp.max(s, axis=-1, keepdims=True))
            alpha = jnp.exp(m - m_new)
            p = jnp.exp(s - m_new)
            l = l * alpha + jnp.sum(p, axis=-1, keepdims=True)
            acc = acc * alpha + _dot(p, v, "nn")
            return m_new, l, acc

        init = (jnp.full((tb, 1), NEG, f32), jnp.zeros((tb, 1), f32), jnp.zeros((tb, HEAD), f32))
        m, l, acc = lax.fori_loop(0, i + 1, step, init)
        o = acc / l
        o_ref[...] = o.astype(bf16)
        o32_ref[...] = o
        lse_ref[...] = m + jnp.log(l)

    return pl.pallas_call(
        body, name=name, grid=(H, nb),
        in_specs=[BS((tb, HEAD), lambda h, i: (i, h)), BS((T, HEAD), lambda h, i: (0, h)),
                  BS((T, HEAD), lambda h, i: (0, h)), BS((None, tb, 1), lambda h, i: (h, i, 0)),
                  BS((None, nb, 1, tb), lambda h, i: (h, 0, 0, 0))],
        out_specs=[BS((tb, HEAD), lambda h, i: (i, h)), BS((tb, HEAD), lambda h, i: (i, h)),
                   BS((None, tb, 1), lambda h, i: (h, i, 0))],
        out_shape=[S((T, A), bf16), S((T, A), f32), S((H, T, 1), f32)],
    )(qn, kn, vb, c_col, c_row)


def _attn_bwd(name, qn, kn, vb, o, dmix, lse, c_col, c_row, P, qg, kg, tb):
    T, A = qn.shape
    H = A // HEAD
    nb = T // tb
    scale = HEAD ** -0.5

    def body(q_ref, k_ref, v_ref, o_ref, do_ref, lse_ref, cq_ref, ck_ref, qraw_ref, kraw_ref, qg_ref, kg_ref,
             dq_out, dk_out, dv_out, dc_out, dqg_out, dkg_out, dq_acc, dk_acc, delta_s):
        h = pl.program_id(0)
        dq_acc[...] = jnp.zeros_like(dq_acc)
        delta_s[...] = jnp.sum(do_ref[...].astype(bf16).astype(f32) * o_ref[...], axis=-1, keepdims=True)
        lrows = lax.broadcasted_iota(jnp.int32, (tb, tb), 0)
        lcols = lax.broadcasted_iota(jnp.int32, (tb, tb), 1)

        def kblock(j, _):
            koff = pl.multiple_of(j * tb, tb)
            k = k_ref[pl.ds(koff, tb), :]
            v = v_ref[pl.ds(koff, tb), :]
            ck = ck_ref[j]

            def qblock(i, carry):
                dk, dv, dc = carry
                qoff = pl.multiple_of(i * tb, tb)
                q = q_ref[pl.ds(qoff, tb), :]
                do = do_ref[pl.ds(qoff, tb), :].astype(bf16)
                s = _dot(q, k, "nt") * scale + (cq_ref[pl.ds(qoff, tb), :] - ck)
                s = jnp.where(qoff + lrows >= koff + lcols, s, NEG)
                p = jnp.exp(s - lse_ref[pl.ds(qoff, tb), :])
                dv = dv + _dot(p, do, "tn")
                dp = _dot(do, v, "nt")
                ds = p * (dp - delta_s[pl.ds(qoff, tb), :])
                dc = dc - jnp.sum(ds, axis=0, keepdims=True)
                dsb = (ds * scale).astype(bf16)
                dk = dk + _dot(dsb, q, "tn")
                dq_acc[pl.ds(qoff, tb), :] += _dot(dsb, k, "nn")
                return dk, dv, dc

            init = (jnp.zeros((tb, HEAD), f32), jnp.zeros((tb, HEAD), f32), jnp.zeros((1, tb), f32))
            dk, dv, dc = lax.fori_loop(j, nb, qblock, init)
            dk_acc[pl.ds(koff, tb), :] = dk
            dv_out[pl.ds(koff, tb), :] = dv.astype(bf16)
            dc_out[j] = dc
            return 0

        lax.fori_loop(0, nb, kblock, 0)

        for raw_ref, g_ref, acc_ref, d_out, dg_out in ((qraw_ref, qg_ref, dq_acc, dq_out, dqg_out),
                                                       (kraw_ref, kg_ref, dk_acc, dk_out, dkg_out)):
            xv = raw_ref[...]
            r = lax.rsqrt(jnp.mean(xv * xv, axis=-1, keepdims=True) + EPS)
            xh = xv * r
            dn = acc_ref[...]
            dxh = dn * g_ref[...]
            d_out[...] = (r * (dxh - xh * jnp.mean(dxh * xh, axis=-1, keepdims=True))).astype(bf16)
            part = jnp.sum(dn * xh, axis=0, keepdims=True)

            @pl.when(h == 0)
            def _():
                dg_out[...] = part

            @pl.when(h > 0)
            def _():
                dg_out[...] += part

    head = lambda off: BS((T, HEAD), lambda h: (0, off + h))
    vec = BS((1, HEAD), lambda h: (0, 0))
    return pl.pallas_call(
        body, name=name, grid=(H,),
        in_specs=[head(0), head(0), head(0), head(0), head(0),
                  BS((None, T, 1), lambda h: (h, 0, 0)), BS((None, T, 1), lambda h: (h, 0, 0)),
                  BS((None, nb, 1, tb), lambda h: (h, 0, 0, 0)), head(0), head(H), vec, vec],
        out_specs=[head(0), head(0), head(0), BS((None, nb, 1, tb), lambda h: (h, 0, 0, 0)), vec, vec],
        out_shape=[S((T, A), bf16)] * 3 + [S((H, nb, 1, tb), f32), S((1, HEAD), f32), S((1, HEAD), f32)],
        scratch_shapes=[pltpu.VMEM((T, HEAD), f32), pltpu.VMEM((T, HEAD), f32), pltpu.VMEM((T, 1), f32)],
        compiler_params=_params(),
    )(qn, kn, vb, o, dmix, lse, c_col, c_row, P, P, qg, kg)


def _gmlp_fwd(name, P, gain, ws, b, col_u, col_v, Wd):
    T = P.shape[0]
    G = Wd // HEAD
    tr = _tile(T, 512, HEAD)

    def body(u_ref, v_ref, gain_ref, ws_ref, b_ref, y_ref):
        tril = lax.broadcasted_iota(jnp.int32, (HEAD, HEAD), 0) >= lax.broadcasted_iota(jnp.int32, (HEAD, HEAD), 1)
        wm = jnp.where(tril, ws_ref[...], 0.0).astype(bf16)
        for n in range(tr // HEAD):
            rows = slice(n * HEAD, (n + 1) * HEAD)
            u = jax.nn.gelu(u_ref[rows, :])
            a = jax.nn.gelu(v_ref[rows, :])
            r = lax.rsqrt(jnp.mean(a * a, axis=-1, keepdims=True) + EPS)
            vn = a * r * gain_ref[...]
            mixed = _dot(wm, vn, "nn") + b_ref[...]
            y_ref[rows, :] = (u * mixed).astype(bf16)

    return pl.pallas_call(
        body, name=name, grid=(G, T // tr),
        in_specs=[BS((tr, HEAD), lambda g, i: (i, col_u + g)), BS((tr, HEAD), lambda g, i: (i, col_v + g)),
                  BS((None, 1, HEAD), lambda g, i: (g, 0, 0)), BS((None, HEAD, HEAD), lambda g, i: (g, 0, 0)),
                  BS((None, HEAD, 1), lambda g, i: (g, 0, 0))],
        out_specs=BS((tr, HEAD), lambda g, i: (i, g)), out_shape=S((T, Wd), bf16),
    )(P, P, gain, ws, b)


def _gmlp_bwd(name, P, dmix, gain, ws, b, col_u, col_v, col_dy, Wd):
    T = P.shape[0]
    G = Wd // HEAD
    tr = _tile(T, 512, HEAD)

    def body(u_ref, v_ref, dy_ref, gain_ref, ws_ref, b_ref, du_ref, dv_ref, dws_ref, db_ref, dgain_ref):
        i = pl.program_id(1)
        tril = lax.broadcasted_iota(jnp.int32, (HEAD, HEAD), 0) >= lax.broadcasted_iota(jnp.int32, (HEAD, HEAD), 1)
        wm = jnp.where(tril, ws_ref[...], 0.0).astype(bf16)
        gain_v = gain_ref[...]
        dw = jnp.zeros((HEAD, HEAD), f32)
        db = jnp.zeros((HEAD, 1), f32)
        dgain = jnp.zeros((1, HEAD), f32)
        for n in range(tr // HEAD):
            rows = slice(n * HEAD, (n + 1) * HEAD)
            u, du_dx = _gelu_and_grad(u_ref[rows, :])
            a, da_dx = _gelu_and_grad(v_ref[rows, :])
            dy = dy_ref[rows, :]
            r = lax.rsqrt(jnp.mean(a * a, axis=-1, keepdims=True) + EPS)
            ah = a * r
            vnb = (ah * gain_v).astype(bf16)
            mixed = _dot(wm, vnb, "nn") + b_ref[...]
            dm = dy * u
            dmb = dm.astype(bf16)
            du_ref[rows, :] = (dy * mixed * du_dx).astype(bf16)
            db = db + jnp.sum(dm, axis=1, keepdims=True)
            dw = dw + _dot(dmb, vnb, "nt")
            dvn = _dot(wm, dmb, "tn")
            dgain = dgain + jnp.sum(dvn * ah, axis=0, keepdims=True)
            dah = dvn * gain_v
            da = r * (dah - ah * jnp.mean(dah * ah, axis=-1, keepdims=True))
            dv_ref[rows, :] = (da * da_dx).astype(bf16)
        dw = jnp.where(tril, dw, 0.0)

        @pl.when(i == 0)
        def _():
            dws_ref[...] = dw
            db_ref[...] = db
            dgain_ref[...] = dgain

        @pl.when(i > 0)
        def _():
            dws_ref[...] += dw
            db_ref[...] += db
            dgain_ref[...] += dgain

    out = BS((tr, HEAD), lambda g, i: (i, g))
    return pl.pallas_call(
        body, name=name, grid=(G, T // tr),
        in_specs=[BS((tr, HEAD), lambda g, i: (i, col_u + g)), BS((tr, HEAD), lambda g, i: (i, col_v + g)),
                  BS((tr, HEAD), lambda g, i: (i, col_dy + g)),
                  BS((None, 1, HEAD), lambda g, i: (g, 0, 0)), BS((None, HEAD, HEAD), lambda g, i: (g, 0, 0)),
                  BS((None, HEAD, 1), lambda g, i: (g, 0, 0))],
        out_specs=[out, out, BS((None, HEAD, HEAD), lambda g, i: (g, 0, 0)), BS((None, HEAD, 1), lambda g, i: (g, 0, 0)),
                   BS((None, 1, HEAD), lambda g, i: (g, 0, 0))],
        out_shape=[S((T, Wd), bf16), S((T, Wd), bf16), S((G, HEAD, HEAD), f32), S((G, HEAD, 1), f32),
                   S((G, 1, HEAD), f32)],
    )(P, P, dmix, gain, ws, b)


def _trailing_window(xv, w, row):
    k = 1
    while k < w:
        xv = xv + jnp.where(row >= k, pltpu.roll(xv, k, 0), 0.0)
        k *= 2
    return xv


def _leading_window(xv, w, row, T):
    k = 1
    while k < w:
        xv = xv + jnp.where(row + k < T, pltpu.roll(xv, T - k, 0), 0.0)
        k *= 2
    return xv


def _pool_fwd(name, P, pw, ps, col_x, Wd):
    T = P.shape[0]
    Gp = Wd // HEAD

    def body(x_ref, pw_ref, ps_ref, y_ref):
        row = lax.broadcasted_iota(jnp.int32, (T, HEAD), 0)
        for g in range(Gp):
            w = POOL_WINDOWS[g]
            sl = slice(g * HEAD, (g + 1) * HEAD)
            xv = x_ref[:, sl]
            cnt = jnp.minimum(row + 1, w).astype(f32)
            d = _trailing_window(xv, w, row) / cnt - xv
            y_ref[:, sl] = (_dot(d, pw_ref[g], "nn") * ps_ref[:, sl]).astype(bf16)

    return pl.pallas_call(
        body, name=name, grid=(1,),
        in_specs=[BS((T, Wd), lambda i: (0, col_x)), BS((Gp, HEAD, HEAD), lambda i: (0, 0, 0)), BS((1, Wd), lambda i: (0, 0))],
        out_specs=BS((T, Wd), lambda i: (0, 0)), out_shape=S((T, Wd), bf16), compiler_params=_params(),
    )(P, pw, ps)


def _pool_bwd(name, P, dmix, pw, ps, col_x, col_dy, Wd):
    T = P.shape[0]
    Gp = Wd // HEAD

    def body(x_ref, dy_ref, pw_ref, ps_ref, dx_ref, dpw_ref, dps_ref):
        row = lax.broadcasted_iota(jnp.int32, (T, HEAD), 0)
        for g in range(Gp):
            w = POOL_WINDOWS[g]
            sl = slice(g * HEAD, (g + 1) * HEAD)
            xv = x_ref[:, sl]
            cnt = jnp.minimum(row + 1, w).astype(f32)
            d = (_trailing_window(xv, w, row) / cnt - xv).astype(bf16)
            pwb = pw_ref[g].astype(bf16)
            z = _dot(d, pwb, "nn")
            dy = dy_ref[:, sl]
            dps_ref[:, sl] = jnp.sum(dy * z, axis=0, keepdims=True)
            dzb = (dy * ps_ref[:, sl]).astype(bf16)
            dpw_ref[g] = _dot(d, dzb, "tn")
            dd = _dot(dzb, pwb, "nt")
            dx_ref[:, sl] = (_leading_window(dd / cnt, w, row, T) - dd).astype(bf16)

    return pl.pallas_call(
        body, name=name, grid=(1,),
        in_specs=[BS((T, Wd), lambda i: (0, col_x)), BS((T, Wd), lambda i: (0, col_dy)),
                  BS((Gp, HEAD, HEAD), lambda i: (0, 0, 0)), BS((1, Wd), lambda i: (0, 0))],
        out_specs=[BS((T, Wd), lambda i: (0, 0)), BS((Gp, HEAD, HEAD), lambda i: (0, 0, 0)), BS((1, Wd), lambda i: (0, 0))],
        out_shape=[S((T, Wd), bf16), S((Gp, HEAD, HEAD), f32), S((1, Wd), f32)], compiler_params=_params(),
    )(P, dmix, pw, ps)


def _adamw(name, w, g, m, v):
    R, C = w.shape
    lanes = -(-C // 128) * 128
    tr = _tile(R, max(8, (512 * 1024) // lanes // 8 * 8), 8)
    c1 = 1.0 - ADAM_B1 ** ADAM_STEP
    c2 = 1.0 - ADAM_B2 ** ADAM_STEP

    def body(w_ref, g_ref, m_ref, v_ref, d_ref, nm_ref, nv_ref):
        gv = g_ref[...]
        nm = ADAM_B1 * m_ref[...] + (1.0 - ADAM_B1) * gv
        nv = ADAM_B2 * v_ref[...] + (1.0 - ADAM_B2) * (gv * gv)
        d_ref[...] = -ADAM_LR * ((nm / c1) / (jnp.sqrt(nv / c2) + ADAM_EPS) + ADAM_WD * w_ref[...])
        nm_ref[...] = nm
        nv_ref[...] = nv

    blk = BS((tr, C), lambda i: (i, 0))
    return pl.pallas_call(
        body, name=name, grid=(R // tr,), in_specs=[blk] * 4, out_specs=[blk] * 3, out_shape=[S((R, C), f32)] * 3,
    )(w, g, m, v)


def _adamw_layer(name, layer, w_all, m_all, v_all, g, prev):
    L, R, C = w_all.shape
    lanes = -(-C // 128) * 128
    tr = _tile(R, max(8, (512 * 1024) // lanes // 8 * 8), 8)
    c1 = 1.0 - ADAM_B1 ** ADAM_STEP
    c2 = 1.0 - ADAM_B2 ** ADAM_STEP
    n_prev = 0 if prev is None else 4

    def body(w_ref, m_ref, v_ref, g_ref, *rest):
        go_ref, d_ref, nm_ref, nv_ref = rest[n_prev:]
        gv = g_ref[...]
        nm = ADAM_B1 * m_ref[...] + (1.0 - ADAM_B1) * gv
        nv = ADAM_B2 * v_ref[...] + (1.0 - ADAM_B2) * (gv * gv)
        d_ref[...] = -ADAM_LR * ((nm / c1) / (jnp.sqrt(nv / c2) + ADAM_EPS) + ADAM_WD * w_ref[...])
        nm_ref[...] = nm
        nv_ref[...] = nv
        go_ref[...] = gv

    slab = BS((None, tr, C), lambda r: (layer, r, 0))
    return pl.pallas_call(
        body, name=name, grid=(R // tr,),
        in_specs=[slab, slab, slab, BS((tr, C), lambda r: (r, 0))] + [ANY] * n_prev,
        out_specs=[slab] * 4, out_shape=[S((L, R, C), f32)] * 4,
        input_output_aliases={4 + k: k for k in range(n_prev)},
    )(w_all, m_all, v_all, g, *(prev or ()))


def _chip_of(k):
    return k // 2, k % 2


def _remote(src, dst, send_sems, recv_sems, idx, dev):
    return pltpu.make_async_remote_copy(src_ref=src, dst_ref=dst, send_sem=send_sems.at[idx], recv_sem=recv_sems.at[idx],
                                        device_id=dev, device_id_type=MESH)


def _plan_gather_chips(n):
    def plan(refs, ss, rs, base):
        ins, lands = refs[:n], refs[n:]
        x, y, c, j0 = _my_place()
        sib = (x, y, 1 - c)
        sends, recvs = [], []
        for a in range(n):
            half = ins[a].shape[0] // 2
            lo = c * half
            sends.append(_remote(ins[a], lands[a].at[j0], ss, rs, base + 4 * a + 3, sib))
            recvs.append(_remote(lands[a].at[j0], lands[a].at[j0], ss, rs, base + 4 * a + 3, sib))
            for r in (1, 2, 3):
                k = j0 ^ r
                dev = (*_chip_of(k), c)
                sends.append(_remote(ins[a].at[pl.ds(lo, half)], lands[a].at[j0, pl.ds(lo, half)], ss, rs,
                                     base + 4 * a + r - 1, dev))
                landed = lands[a].at[k, pl.ds(lo, half)]
                recvs.append(_remote(landed, landed, ss, rs, base + 4 * a + r - 1, dev))
        return sends, recvs
    return plan, 4 * n


def _plan_gather_forward(n):
    def plan(refs, ss, rs, base):
        x, y, c, j0 = _my_place()
        sib = (x, y, 1 - c)
        sends, recvs = [], []
        for a in range(n):
            half = refs[a].shape[1] // 2
            for r in (1, 2, 3):
                k = j0 ^ r
                landed = refs[a].at[k, pl.ds(c * half, half)]
                other = refs[a].at[k, pl.ds((1 - c) * half, half)]
                sends.append(_remote(landed, landed, ss, rs, base + 3 * a + r - 1, sib))
                recvs.append(_remote(other, other, ss, rs, base + 3 * a + r - 1, sib))
        return sends, recvs
    return plan, 3 * n


def _plan_sibling_halves(n):
    def plan(refs, ss, rs, base):
        ins, lands = refs[:n], refs[n:]
        x, y, c, _ = _my_place()
        sib = (x, y, 1 - c)
        sends, recvs = [], []
        for a in range(n):
            half = ins[a].shape[1] // 2
            sends.append(_remote(ins[a].at[:, pl.ds((1 - c) * half, half), :], lands[a], ss, rs, base + a, sib))
            recvs.append(_remote(lands[a], lands[a], ss, rs, base + a, sib))
        return sends, recvs
    return plan, n


def _plan_chip_scatter(n):
    def plan(refs, ss, rs, base):
        ins, lands = refs[:n], refs[n:]
        x, y, c, j0 = _my_place()
        sends, recvs = [], []
        for a in range(n):
            for r in (1, 2, 3):
                k = j0 ^ r
                dev = (*_chip_of(k), c)
                sends.append(_remote(ins[a].at[k], lands[a].at[j0], ss, rs, base + 3 * a + r - 1, dev))
                recvs.append(_remote(lands[a].at[k], lands[a].at[k], ss, rs, base + 3 * a + r - 1, dev))
        return sends, recvs
    return plan, 3 * n


def _plan_sibling_join(n):
    def plan(refs, ss, rs, base):
        x, y, c, _ = _my_place()
        sib = (x, y, 1 - c)
        sends, recvs = [], []
        for a in range(n):
            half = refs[a].shape[0] // 2
            mine = refs[a].at[pl.ds(c * half, half)]
            theirs = refs[a].at[pl.ds((1 - c) * half, half)]
            sends.append(_remote(mine, mine, ss, rs, base + a, sib))
            recvs.append(_remote(theirs, theirs, ss, rs, base + a, sib))
        return sends, recvs
    return plan, n


def _plan_all(parts):
    def plan(refs, ss, rs, base):
        sends, recvs, ro = [], [], 0
        for (sub, n_sems), n_refs in parts:
            s_, r_ = sub(refs[ro:ro + n_refs], ss, rs, base)
            sends += s_
            recvs += r_
            ro += n_refs
            base += n_sems
        return sends, recvs
    return plan, sum(p[0][1] for p in parts)


_HBM = pl.BlockSpec(memory_space=pltpu.HBM)
_SEM = pl.BlockSpec(memory_space=pltpu.SEMAPHORE)
_EFFECT = pltpu.SideEffectType.DATAFLOW_SIDE_EFFECTING


def _exchange_start(name, plan, bufs, after):
    plan_fn, n_sems = plan
    n = len(bufs)

    def body(*refs):
        ss, rs, token = refs[n + len(after)], refs[n + len(after) + 1], refs[-1]
        sends, _ = plan_fn(refs[:n], ss, rs, 0)
        for cp in sends:
            cp.start()
        token[...] = jnp.zeros_like(token)

    res = pl.pallas_call(
        body, name=name,
        out_shape=(pltpu.SemaphoreType.DMA((n_sems,)), pltpu.SemaphoreType.DMA((n_sems,)),
                   *[pltpu.HBM(b.shape, b.dtype) for b in bufs], S((8, 128), f32)),
        in_specs=[_HBM] * n + [ANY] * len(after),
        out_specs=(_SEM, _SEM, *[_HBM] * n, pl.BlockSpec(memory_space=pltpu.VMEM)),
        input_output_aliases={k: 2 + k for k in range(n)},
        compiler_params=pltpu.CompilerParams(has_side_effects=_EFFECT),
    )(*[pltpu.with_memory_space_constraint(b, pltpu.HBM) for b in bufs], *after)
    return res[0], res[1], list(res[2:2 + n]), res[-1]


def _exchange_wait(name, plan, send_sems, recv_sems, bufs, after):
    plan_fn, _ = plan
    n = len(bufs)

    def body(*refs):
        ss, rs, token = refs[n], refs[n + 1], refs[-1]
        sends, recvs = plan_fn(refs[:n], ss, rs, 0)
        for cp in recvs:
            cp.wait_recv()
        for cp in sends:
            cp.wait_send()
        token[...] = jnp.zeros_like(token)

    res = pl.pallas_call(
        body, name=name,
        out_shape=(*[pltpu.HBM(b.shape, b.dtype) for b in bufs], S((8, 128), f32)),
        in_specs=[_HBM] * n + [_SEM, _SEM] + [ANY] * len(after),
        out_specs=(*[_HBM] * n, pl.BlockSpec(memory_space=pltpu.VMEM)),
        input_output_aliases={k: k for k in range(n)},
        compiler_params=pltpu.CompilerParams(has_side_effects=_EFFECT),
    )(*bufs, send_sems, recv_sems, *after)
    return list(res[:n]), res[-1]


class _Order:
    def __init__(self, first):
        self.marker = first
        self.token = None

    def _after(self):
        return [self.marker] + ([] if self.token is None else [self.token])

    def start(self, name, plan, bufs):
        ss, rs, thru, self.token = _exchange_start(name, plan, bufs, self._after())
        return name, plan, ss, rs, thru

    def wait(self, handle):
        name, plan, ss, rs, thru = handle
        out, self.token = _exchange_wait(name + "_wait", plan, ss, rs, thru, self._after())
        return out

    def follows(self, small):
        return small if self.token is None else small + self.token[0, 0]

    def done(self, result):
        self.marker = result[:1, :1]


def _all_reduce_small(name, g8):
    _, R, L = g8.shape

    def body(g_ref, out_ref, land, red, send1, recv1, send2, recv2):
        x, y, c, _ = _my_place()
        me = 4 * x + 2 * y + c
        peers = []
        for r in range(1, N_DEV):
            q = me ^ r
            peers.append((q, (q // 4, (q // 2) % 2, q % 2)))
        first = []
        for r, (q, dev) in enumerate(peers):
            cp = pltpu.make_async_remote_copy(src_ref=g_ref.at[q], dst_ref=land.at[me], send_sem=send1.at[r],
                                              recv_sem=recv1.at[r], device_id=dev, device_id_type=MESH)
            cp.start()
            first.append(cp)
        land[me] = g_ref[me]
        for r, (q, dev) in enumerate(peers):
            pltpu.make_async_remote_copy(src_ref=land.at[q], dst_ref=land.at[q], send_sem=send1.at[r],
                                         recv_sem=recv1.at[r], device_id=dev, device_id_type=MESH).wait_recv()
        acc = land[0]
        for d in range(1, N_DEV):
            acc = acc + land[d]
        red[...] = acc
        out_ref[me] = acc
        second = []
        for r, (q, dev) in enumerate(peers):
            cp = pltpu.make_async_remote_copy(src_ref=red, dst_ref=out_ref.at[me], send_sem=send2.at[r],
                                              recv_sem=recv2.at[r], device_id=dev, device_id_type=MESH)
            cp.start()
            second.append(cp)
        for r, (q, dev) in enumerate(peers):
            pltpu.make_async_remote_copy(src_ref=out_ref.at[q], dst_ref=out_ref.at[q], send_sem=send2.at[r],
                                         recv_sem=recv2.at[r], device_id=dev, device_id_type=MESH).wait_recv()
        for cp in first + second:
            cp.wait_send()

    vm = pl.BlockSpec(memory_space=pltpu.VMEM)
    return pl.pallas_call(
        body, name=name, in_specs=[vm], out_specs=vm, out_shape=S(g8.shape, f32),
        scratch_shapes=[pltpu.VMEM((N_DEV, R, L), f32), pltpu.VMEM((R, L), f32)]
        + [pltpu.SemaphoreType.DMA((N_DEV - 1,))] * 4,
        compiler_params=_params(),
    )(g8)


def _pair_sum(name, g4, sib):
    _, rows, cols = g4.shape
    half = rows // 2
    lanes = -(-cols // 128) * 128
    tr = _tile(half, max(16, (512 * 1024) // lanes // 16 * 16), 16)
    nb = half // tr

    def body(g_ref, s_ref, pb_ref, own_ref):
        j = pl.program_id(1)
        t = g_ref[...] + s_ref[...]
        pb_ref[...] = t.astype(bf16)

        @pl.when(j == _my_place()[3])
        def _():
            own_ref[...] = t

    return pl.pallas_call(
        body, name=name, grid=(nb, N_CHIPS),
        in_specs=[BS((None, tr, cols), lambda i, j: (j, lax.axis_index("c") * nb + i, 0)),
                  BS((None, tr, cols), lambda i, j: (j, i, 0))],
        out_specs=[BS((None, tr, cols), lambda i, j: (j, i, 0)), BS((tr, cols), lambda i, j: (i, 0))],
        out_shape=[S((N_CHIPS, half, cols), bf16), S((half, cols), f32)],
    )(g4, sib)


def _chip_sum(name, own, got):
    half, cols = own.shape
    lanes = -(-cols // 128) * 128
    tr = _tile(half, max(16, (512 * 1024) // lanes // 16 * 16), 16)
    nb = half // tr

    def body(own_ref, *rest):
        got_refs, o_ref = rest[:N_CHIPS], rest[N_CHIPS]
        j0 = _my_place()[3]
        acc = None
        for k in range(N_CHIPS):
            t = jnp.where(j0 == k, own_ref[...], got_refs[k][...].astype(f32))
            acc = t if acc is None else acc + t
        o_ref[...] = acc

    def slot(k):
        return BS((None, tr, cols), lambda i: (jnp.where(_my_place()[3] == k, (k + 1) % N_CHIPS, k), i, 0))

    return pl.pallas_call(
        body, name=name, grid=(nb,),
        in_specs=[BS((tr, cols), lambda i: (i, 0))] + [slot(k) for k in range(N_CHIPS)],
        out_specs=BS((tr, cols), lambda i: (lax.axis_index("c") * nb + i, 0)),
        out_shape=S((2 * half, cols), f32),
    )(own, got, got, got, got)


def _pack_rows(arrs, n_rows):
    flat = jnp.concatenate([a.reshape(-1).astype(f32) for a in arrs])
    return jnp.pad(flat, (0, n_rows * 128 - flat.shape[0])).reshape(n_rows, 128)


def kernel(x, p, norm_mix, w_in, q_norm, k_norm, forget_bias, gmlp_v_norm, gmlp_w_s, gmlp_b_s, pool_w, pool_scale, w_out, norm_ffn, w_ffn_gate, w_ffn_up, w_ffn_down, norm_ple, w_ple_gate, w_ple_proj, loss_target, m_norm_mix, m_w_in, m_q_norm, m_k_norm, m_forget_bias, m_gmlp_v_norm, m_gmlp_w_s, m_gmlp_b_s, m_pool_w, m_pool_scale, m_w_out, m_norm_ffn, m_w_ffn_gate, m_w_ffn_up, m_w_ffn_down, m_norm_ple, m_w_ple_gate, m_w_ple_proj, v_norm_mix, v_w_in, v_q_norm, v_k_norm, v_forget_bias, v_gmlp_v_norm, v_gmlp_w_s, v_gmlp_b_s, v_pool_w, v_pool_scale, v_w_out, v_norm_ffn, v_w_ffn_gate, v_w_ffn_up, v_w_ffn_down, v_norm_ple, v_w_ple_gate, v_w_ple_proj):
    W = dict(norm_mix=norm_mix, w_in=w_in, q_norm=q_norm, k_norm=k_norm, forget_bias=forget_bias,
             gmlp_v_norm=gmlp_v_norm, gmlp_w_s=gmlp_w_s, gmlp_b_s=gmlp_b_s, pool_w=pool_w, pool_scale=pool_scale,
             w_out=w_out, norm_ffn=norm_ffn, w_ffn_gate=w_ffn_gate, w_ffn_up=w_ffn_up, w_ffn_down=w_ffn_down,
             norm_ple=norm_ple, w_ple_gate=w_ple_gate, w_ple_proj=w_ple_proj)
    M = dict(norm_mix=m_norm_mix, w_in=m_w_in, q_norm=m_q_norm, k_norm=m_k_norm, forget_bias=m_forget_bias,
             gmlp_v_norm=m_gmlp_v_norm, gmlp_w_s=m_gmlp_w_s, gmlp_b_s=m_gmlp_b_s, pool_w=m_pool_w,
             pool_scale=m_pool_scale, w_out=m_w_out, norm_ffn=m_norm_ffn, w_ffn_gate=m_w_ffn_gate,
             w_ffn_up=m_w_ffn_up, w_ffn_down=m_w_ffn_down, norm_ple=m_norm_ple, w_ple_gate=m_w_ple_gate,
             w_ple_proj=m_w_ple_proj)
    V = dict(norm_mix=v_norm_mix, w_in=v_w_in, q_norm=v_q_norm, k_norm=v_k_norm, forget_bias=v_forget_bias,
             gmlp_v_norm=v_gmlp_v_norm, gmlp_w_s=v_gmlp_w_s, gmlp_b_s=v_gmlp_b_s, pool_w=v_pool_w,
             pool_scale=v_pool_scale, w_out=v_w_out, norm_ffn=v_norm_ffn, w_ffn_gate=v_w_ffn_gate,
             w_ffn_up=v_w_ffn_up, w_ffn_down=v_w_ffn_down, norm_ple=v_norm_ple, w_ple_gate=v_w_ple_gate,
             w_ple_proj=v_w_ple_proj)

    L = w_in.shape[0]
    _, T, D = x.shape
    A, Wd = D // 2, D // 4
    H = A // HEAD
    G = gmlp_w_s.shape[1]
    Gp = pool_w.shape[1]
    DP4 = w_in.shape[2]
    DP = N_CHIPS * DP4
    NM = 3 * A + 3 * Wd
    FS = w_ffn_gate.shape[2]
    FF = N_CHIPS * FS
    DS = D // N_CHIPS
    PL = p.shape[-1]
    assert Wd // G == HEAD and Wd // Gp == HEAD and DP == NM + H and H <= HEAD
    assert all(w & (w - 1) == 0 for w in POOL_WINDOWS[:Gp])
    tb = _tile(T, 256, HEAD)
    nb = T // tb
    tm = _tile(T, 512, 16)
    tn = _tile(NM, 512, 128)
    tnd = _tile(D, 512, 128)
    tkf = _tile(FF, 512, 128)
    col_gu, col_gv, col_xp = 3 * A // HEAD, (3 * A + Wd) // HEAD, (3 * A + 2 * Wd) // Wd
    col_dg, col_dp = A // HEAD, (A + Wd) // Wd

    nw = len(BIG)
    order = _Order(x[0, :1, :1])

    def gather_start(i):
        shards = [W[n][i].astype(bf16) for n in BIG]
        lands = [lax.empty((N_CHIPS,) + s.shape, bf16) for s in shards]
        return order.start(f"ag_ici_{i}", _plan_gather_chips(nw), shards + lands)

    def gather_forward(i, handle):
        return order.start(f"ag_d2d_{i}", _plan_gather_forward(nw), order.wait(handle)[nw:])

    def gathered(handle):
        g = order.wait(handle)

        def cols(lo, hi):
            out = []
            for j in range(N_CHIPS):
                s, e = max(lo, j * DP4), min(hi, (j + 1) * DP4)
                if s < e:
                    out.append(g[0][j][:, s - j * DP4:e - j * DP4])
            return out

        return dict(
            w_main=jnp.concatenate(cols(0, 3 * A) + cols(3 * A + H, DP), axis=1),
            w_f=jnp.pad(jnp.concatenate(cols(3 * A, 3 * A + H), axis=1), ((0, 0), (0, HEAD - H))),
            w_out=g[1].reshape(D, D), w_gate=g[2], w_up=g[3], w_down=g[4].reshape(FF, D),
            w_pg=g[5].reshape(D, D), w_pp=g[6])

    Wf = [None] * L
    ici = gather_start(0)
    d2d = gather_forward(0, ici)
    ici = gather_start(1) if L > 1 else None
    Wf[0] = gathered(d2d)

    h = x.reshape(T, D)
    pb16 = p.reshape(L, T, PL).astype(bf16)
    saved = []

    for i in range(L):
        w = Wf[i]
        sv = dict(h0=h)
        xn1 = _rms_fwd(f"rms1_{i}", h, order.follows(norm_mix[i]))
        (P,) = _matmul(f"proj_{i}", "nn", (T // tm, NM // tn),
                       [((xn1, BS((tm, D), lambda i, j: (i, 0))), (w["w_main"], BS((D, tn), lambda i, j: (0, j))))], [],
                       [((T, NM), f32, BS((tm, tn), lambda i, j: (i, j)))])
        (Pf,) = _matmul(f"projf_{i}", "nn", (T // tm, 1),
                        [((xn1, BS((tm, D), lambda i, j: (i, 0))), (w["w_f"], BS((D, HEAD), lambda i, j: (0, 0))))], [],
                        [((T, HEAD), f32, BS((tm, HEAD), lambda i, j: (i, 0)))])
        fb = jnp.pad(forget_bias[i], (0, HEAD - H)).reshape(1, HEAD)
        cc, ct = _fgate_fwd(f"fgate_{i}", Pf, fb)
        c_col = ct[:H].reshape(H, T, 1)
        c_row = ct[:H].reshape(H, nb, 1, tb)
        qg, kg = q_norm[i].reshape(1, HEAD), k_norm[i].reshape(1, HEAD)
        qn, kn, vb = _qk_norm(f"qknorm_{i}", P, qg, kg, A)
        y_attn, o32, lse = _attn_fwd(f"attn_{i}", qn, kn, vb, c_col, c_row, tb)
        gain = gmlp_v_norm[i].reshape(G, 1, HEAD)
        bs = gmlp_b_s[i].reshape(G, HEAD, 1)
        y_gmlp = _gmlp_fwd(f"gmlp_{i}", P, gain, gmlp_w_s[i], bs, col_gu, col_gv, Wd)
        ps = pool_scale[i].reshape(1, Wd)
        y_pool = _pool_fwd(f"pool_{i}", P, pool_w[i], ps, col_xp, Wd)
        mix = jnp.concatenate([y_attn, y_gmlp, y_pool], axis=1)
        (h1,) = _matmul(f"out_{i}", "nn", (T // tm, D // tnd),
                        [((mix, BS((tm, D), lambda i, j: (i, 0))), (w["w_out"], BS((D, tnd), lambda i, j: (0, j))))],
                        [(h, BS((tm, tnd), lambda i, j: (i, j)))],
                        [((T, D), f32, BS((tm, tnd), lambda i, j: (i, j)))],
                        epilogue=lambda accs, ex: (accs[0] + ex[0],))
        xn2 = _rms_fwd(f"rms2_{i}", h1, norm_ffn[i])

        def ffn_epi(accs, ex):
            g_, u_ = accs
            return g_, u_, g_ * jax.nn.sigmoid(g_) * u_

        ffo = BS((tm, FS), lambda i, j: (i, j))
        Gt, Ut, act = _matmul(f"ffn1_{i}", "nn", (T // tm, N_CHIPS),
                              [((xn2, BS((tm, D), lambda i, j: (i, 0))), (w["w_gate"], BS((None, D, FS), lambda i, j: (j, 0, 0)))),
                               ((xn2, BS((tm, D), lambda i, j: (i, 0))), (w["w_up"], BS((None, D, FS), lambda i, j: (j, 0, 0))))],
                              [], [((T, FF), bf16, ffo)] * 3, epilogue=ffn_epi)
        (h2,) = _matmul(f"ffn2_{i}", "nn", (T // tm, D // tnd),
                        [((act, BS((tm, FF), lambda i, j: (i, 0))), (w["w_down"], BS((FF, tnd), lambda i, j: (0, j))))],
                        [(h1, BS((tm, tnd), lambda i, j: (i, j)))],
                        [((T, D), f32, BS((tm, tnd), lambda i, j: (i, j)))],
                        epilogue=lambda accs, ex: (accs[0] + ex[0],))
        order.done(h2)
        if i + 1 < L:
            d2d = gather_forward(i + 1, ici)
            ici = gather_start(i + 2) if i + 2 < L else None
        xn3 = _rms_fwd(f"rms3_{i}", h2, order.follows(norm_ple[i]))

        def ple_epi(accs, ex):
            gate = jax.nn.sigmoid(accs[0])
            return ex[0] + accs[1] * gate, gate, accs[1]

        dso = BS((tm, DS), lambda i, j: (i, j))
        h3, gate, e = _matmul(f"ple_{i}", "nn", (T // tm, N_CHIPS),
                              [((xn3, BS((tm, D), lambda i, j: (i, 0))), (w["w_pg"], BS((D, DS), lambda i, j: (0, j)))),
                               ((pb16[i], BS((tm, PL), lambda i, j: (i, 0))), (w["w_pp"], BS((None, PL, DS), lambda i, j: (j, 0, 0))))],
                              [(h2, dso)], [((T, D), f32, dso), ((T, D), bf16, dso), ((T, D), bf16, dso)], epilogue=ple_epi)
        sv.update(xn1=xn1, P=P, Pf=Pf, fb=fb, c_col=c_col, c_row=c_row, qn=qn, kn=kn, vb=vb, o32=o32, lse=lse,
                  mix=mix, h1=h1, xn2=xn2, Gt=Gt, Ut=Ut, act=act, h2=h2, xn3=xn3, gate=gate, e=e)
        saved.append(sv)
        h = h3
        order.done(h3)
        if i + 1 < L:
            Wf[i + 1] = gathered(d2d)

    dh, loss_tile = _loss_grad("loss", h, loss_target.reshape(T, D))

    small_g = {n: [None] * L for n in SMALL}
    big_out = {}
    pipe = dict(a=None, b=None, c=None, handle=None, inflight=[], count=0)

    def pump(ua):
        ub, uc, ud = pipe["a"], pipe["b"], pipe["c"]
        if pipe["handle"] is not None:
            out = order.wait(pipe["handle"])
            for u, key, cnt in pipe["inflight"]:
                u[key], out = out[:cnt], out[cnt:]
        if ub is not None:
            n_u = len(ub["names"])
            g4, sib = ub["a"][:n_u], ub["a"][n_u:]
            pairs = [_pair_sum(f"rs_pair_{ub['tag']}_{a}", g4[a], sib[a]) for a in range(n_u)]
            ub["pb"], ub["own"] = [t[0] for t in pairs], [t[1] for t in pairs]
        if uc is not None:
            n_u = len(uc["names"])
            got = uc["b"][n_u:]
            uc["sum"] = [_chip_sum(f"rs_sum_{uc['tag']}_{a}", uc["own"][a], got[a]) for a in range(n_u)]
        if ud is not None:
            for n, r in zip(ud["names"], ud["c"]):
                big_out[n] = _adamw_layer(f"adamw_{n}_{ud['layer']}", ud["layer"], W[n], M[n], V[n],
                                          r.reshape(W[n].shape[1:]), big_out.get(n))
        parts, bufs, inflight = [], [], []
        if ua is not None:
            n_u = len(ua["names"])
            lands = [lax.empty((N_CHIPS, g.shape[1] // 2, g.shape[2]), f32) for g in ua["grads"]]
            parts.append((_plan_sibling_halves(n_u), 2 * n_u))
            bufs += ua["grads"] + lands
            inflight.append((ua, "a", 2 * n_u))
        if ub is not None:
            n_u = len(ub["names"])
            lands = [lax.empty(t.shape, bf16) for t in ub["pb"]]
            parts.append((_plan_chip_scatter(n_u), 2 * n_u))
            bufs += ub["pb"] + lands
            inflight.append((ub, "b", 2 * n_u))
        if uc is not None:
            n_u = len(uc["names"])
            parts.append((_plan_sibling_join(n_u), n_u))
            bufs += uc["sum"]
            inflight.append((uc, "c", n_u))
        pipe["count"] += 1
        pipe["handle"] = order.start(f"rs_{pipe['count']}", _plan_all(parts), bufs) if parts else None
        pipe.update(a=ua, b=ub, c=uc, inflight=inflight)

    dh1 = None
    for i in reversed(range(L)):
        w, sv = Wf[i], saved[i]
        if dh1 is not None:
            dh, _, dg = _rms_bwd(f"rms1_bw_{i + 1}", dxn1, saved[i + 1]["h0"], order.follows(norm_mix[i + 1]), dh1)
            small_g["norm_mix"][i + 1] = dg.reshape(D)
        de, dz = _ple_bwd_elem(f"ple_bw_{i}", dh, sv["gate"], sv["e"])
        (d_wpp,) = _matmul(f"d_wpp_{i}", "tn", (N_CHIPS, 1),
                           [((pb16[i], BS((T, PL), lambda i, j: (0, 0))), (de, BS((T, DS), lambda i, j: (0, i))))], [],
                           [((N_CHIPS, PL, DS), f32, BS((None, PL, DS), lambda i, j: (i, 0, 0)))])
        (d_wpg,) = _matmul(f"d_wpg_{i}", "tn", (D // tnd, D // tnd),
                           [((sv["xn3"], BS((T, tnd), lambda i, j: (0, i))), (dz, BS((T, tnd), lambda i, j: (0, j))))], [],
                           [((D, D), f32, BS((tnd, tnd), lambda i, j: (i, j)))])
        (dxn3,) = _matmul(f"d_xn3_{i}", "nt", (T // tm, D // tnd),
                          [((dz, BS((tm, D), lambda i, j: (i, 0))), (w["w_pg"], BS((tnd, D), lambda i, j: (j, 0))))], [],
                          [((T, D), f32, BS((tm, tnd), lambda i, j: (i, j)))])
        dh2, dh2b, dg = _rms_bwd(f"rms3_bw_{i}", dxn3, sv["h2"], norm_ple[i], dh)
        small_g["norm_ple"][i] = dg.reshape(D)

        def dffn_epi(accs, ex):
            da = accs[0]
            g_, u_ = ex[0].astype(f32), ex[1].astype(f32)
            sg = jax.nn.sigmoid(g_)
            return da * u_ * (sg * (1.0 + g_ * (1.0 - sg))), da * (g_ * sg)

        ffo = BS((tm, FS), lambda i, j: (i, j))
        dG, dU = _matmul(f"d_act_{i}", "nt", (T // tm, N_CHIPS),
                         [((dh2b, BS((tm, D), lambda i, j: (i, 0))), (w["w_down"], BS((FS, D), lambda i, j: (j, 0))))],
                         [(sv["Gt"], ffo), (sv["Ut"], ffo)], [((T, FF), bf16, ffo)] * 2, epilogue=dffn_epi)
        (d_wd,) = _matmul(f"d_wd_{i}", "tn", (FF // tkf, D // tnd),
                          [((sv["act"], BS((T, tkf), lambda i, j: (0, i))), (dh2b, BS((T, tnd), lambda i, j: (0, j))))], [],
                          [((FF, D), f32, BS((tkf, tnd), lambda i, j: (i, j)))])
        gu_out = BS((None, tnd, FS), lambda i, j: (j, i, 0))
        d_wg, d_wu = _matmul(f"d_wgu_{i}", "tn", (D // tnd, N_CHIPS),
                             [((sv["xn2"], BS((T, tnd), lambda i, j: (0, i))), (dG, BS((T, FS), lambda i, j: (0, j)))),
                              ((sv["xn2"], BS((T, tnd), lambda i, j: (0, i))), (dU, BS((T, FS), lambda i, j: (0, j))))], [],
                             [((N_CHIPS, D, FS), f32, gu_out)] * 2, epilogue=lambda accs, ex: (accs[0], accs[1]))
        tm2 = _tile(T, 256, 16)
        (dxn2,) = _matmul(f"d_xn2_{i}", "nt", (T // tm2, D // tnd),
                          [((dG, BS((tm2, FF), lambda i, j: (i, 0))), (w["w_gate"], BS((N_CHIPS, tnd, FS), lambda i, j: (0, j, 0)))),
                           ((dU, BS((tm2, FF), lambda i, j: (i, 0))), (w["w_up"], BS((N_CHIPS, tnd, FS), lambda i, j: (0, j, 0))))], [],
                          [((T, D), f32, BS((tm2, tnd), lambda i, j: (i, j)))])
        order.done(dxn2)
        pump(dict(tag=f"{i}f", layer=i, names=["w_ffn_gate", "w_ffn_up", "w_ffn_down", "w_ple_gate", "w_ple_proj"],
                  grads=[d_wg, d_wu, d_wd.reshape(N_CHIPS, FS, D), d_wpg.reshape(N_CHIPS, DS, D), d_wpp]))
        dh1, dh1b, dg = _rms_bwd(f"rms2_bw_{i}", dxn2, sv["h1"], order.follows(norm_ffn[i]), dh2)
        small_g["norm_ffn"][i] = dg.reshape(D)
        (dmix,) = _matmul(f"d_mix_{i}", "nt", (T // tm, D // tnd),
                          [((dh1b, BS((tm, D), lambda i, j: (i, 0))), (w["w_out"], BS((tnd, D), lambda i, j: (j, 0))))], [],
                          [((T, D), f32, BS((tm, tnd), lambda i, j: (i, j)))])
        (d_wout,) = _matmul(f"d_wout_{i}", "tn", (D // tnd, D // tnd),
                            [((sv["mix"], BS((T, tnd), lambda i, j: (0, i))), (dh1b, BS((T, tnd), lambda i, j: (0, j))))], [],
                            [((D, D), f32, BS((tnd, tnd), lambda i, j: (i, j)))])
        qg, kg = q_norm[i].reshape(1, HEAD), k_norm[i].reshape(1, HEAD)
        dq, dk, dv, dc_row, dqg, dkg = _attn_bwd(f"attn_bw_{i}", sv["qn"], sv["kn"], sv["vb"], sv["o32"], dmix,
                                                 sv["lse"], sv["c_col"], sv["c_row"], sv["P"], qg, kg, tb)
        small_g["q_norm"][i] = dqg.reshape(HEAD)
        small_g["k_norm"][i] = dkg.reshape(HEAD)
        dct = jnp.pad(dc_row.reshape(H, T), ((0, HEAD - H), (0, 0)))
        dPf, dfb = _fgate_bwd(f"fgate_bw_{i}", dct, sv["Pf"], sv["fb"])
        small_g["forget_bias"][i] = dfb[0, :H]
        gain = gmlp_v_norm[i].reshape(G, 1, HEAD)
        bs = gmlp_b_s[i].reshape(G, HEAD, 1)
        dgu, dgv, dws, dbs, dgain = _gmlp_bwd(f"gmlp_bw_{i}", sv["P"], dmix, gain, gmlp_w_s[i], bs, col_gu, col_gv,
                                              col_dg, Wd)
        small_g["gmlp_w_s"][i] = dws
        small_g["gmlp_b_s"][i] = dbs.reshape(G, HEAD)
        small_g["gmlp_v_norm"][i] = dgain.reshape(G, HEAD)
        ps = pool_scale[i].reshape(1, Wd)
        dxp, dpw, dps = _pool_bwd(f"pool_bw_{i}", sv["P"], dmix, pool_w[i], ps, col_xp, col_dp, Wd)
        small_g["pool_w"][i] = dpw
        small_g["pool_scale"][i] = dps.reshape(Wd)
        dP = jnp.concatenate([dq, dk, dv, dgu, dgv, dxp], axis=1)
        (d_wmain,) = _matmul(f"d_wmain_{i}", "tn", (D // tnd, NM // tn),
                             [((sv["xn1"], BS((T, tnd), lambda i, j: (0, i))), (dP, BS((T, tn), lambda i, j: (0, j))))], [],
                             [((D, NM), f32, BS((tnd, tn), lambda i, j: (i, j)))])
        (d_wf,) = _matmul(f"d_wf_{i}", "tn", (D // tnd, 1),
                          [((sv["xn1"], BS((T, tnd), lambda i, j: (0, i))), (dPf, BS((T, HEAD), lambda i, j: (0, 0))))], [],
                          [((D, HEAD), f32, BS((tnd, HEAD), lambda i, j: (i, 0)))])
        (dxn1,) = _matmul(f"d_xn1_{i}", "nt", (T // tm2, D // tnd),
                          [((dP, BS((tm2, NM), lambda i, j: (i, 0))), (w["w_main"], BS((tnd, NM), lambda i, j: (j, 0)))),
                           ((dPf, BS((tm2, HEAD), lambda i, j: (i, 0))), (w["w_f"], BS((tnd, HEAD), lambda i, j: (j, 0))))], [],
                          [((T, D), f32, BS((tm2, tnd), lambda i, j: (i, j)))])
        def win_cols(lo, hi):
            out = []
            for src, s0, e0, off in ((d_wmain, 0, 3 * A, 0), (d_wf, 3 * A, 3 * A + H, 3 * A), (d_wmain, 3 * A + H, DP, H)):
                s, e = max(lo, s0), min(hi, e0)
                if s < e:
                    out.append(src[:, s - off:e - off])
            return out

        d_win4 = jnp.stack([jnp.concatenate(win_cols(j * DP4, (j + 1) * DP4), axis=1) for j in range(N_CHIPS)])
        order.done(dxn1)
        pump(dict(tag=f"{i}m", layer=i, names=["w_in", "w_out"], grads=[d_win4, d_wout.reshape(N_CHIPS, DS, D)]))

    dh, _, dg = _rms_bwd("rms1_bw_0", dxn1, saved[0]["h0"], order.follows(norm_mix[0]), dh1)
    small_g["norm_mix"][0] = dg.reshape(D)
    order.done(dh)
    for _ in range(3):
        pump(None)

    small_full = {n: jnp.stack(small_g[n]) for n in SMALL}
    n_small = sum(int(W[n].size) for n in SMALL) + 1
    rows8 = -(-n_small // (128 * 64)) * 8
    packed = _pack_rows([small_full[n] for n in SMALL] + [loss_tile[0, :1]], N_DEV * rows8)
    summed = _all_reduce_small("allreduce_small", packed.reshape(N_DEV, rows8, 128)).reshape(-1)
    grads, off = {}, 0
    for n in SMALL:
        grads[n] = summed[off:off + W[n].size].reshape(W[n].shape)
        off += W[n].size
    loss = summed[off]

    delta, new_m, new_v = {}, {}, {}
    wp, mp, vp = (_pack_rows([t[n] for n in SMALL], N_DEV * rows8) for t in (W, M, V))
    ds_, ms_, vs_ = _adamw("adamw_small", wp, summed.reshape(-1, 128), mp, vp)
    off = 0
    for n in SMALL:
        sz = W[n].size
        delta[n], new_m[n], new_v[n] = (t.reshape(-1)[off:off + sz].reshape(W[n].shape) for t in (ds_, ms_, vs_))
        off += sz
    for n in BIG:
        grads[n], delta[n], new_m[n], new_v[n] = big_out[n]

    return (loss, dh.reshape(1, T, D), *[grads[n] for n in WEIGHTS], *[delta[n] for n in WEIGHTS],
            *[new_m[n] for n in WEIGHTS], *[new_v[n] for n in WEIGHTS])
```

```python
import jax
import jax.numpy as jnp
from jax import lax
from jax.experimental import pallas as pl
from jax.experimental.pallas import tpu as pltpu

f32, bf16 = jnp.float32, jnp.bfloat16
S = jax.ShapeDtypeStruct
BS = pl.BlockSpec
ANY = pl.BlockSpec(memory_space=pl.ANY)
MESH = pl.DeviceIdType.MESH

EPS = 1e-6
HEAD = 128
POOL_WINDOWS = (2, 4, 8, 16)
NEG = -1e30
N_CHIPS = 4
N_DEV = 8
VMEM_LIMIT = 56 * 1024 * 1024

ADAM_LR, ADAM_B1, ADAM_B2, ADAM_EPS, ADAM_WD, ADAM_STEP = 0.001, 0.9, 0.999, 1e-08, 0.01, 10

BIG = ("w_in", "w_out", "w_ffn_gate", "w_ffn_up", "w_ffn_down", "w_ple_gate", "w_ple_proj")
SMALL = ("norm_mix", "q_norm", "k_norm", "forget_bias", "gmlp_v_norm", "gmlp_w_s", "gmlp_b_s", "pool_w",
         "pool_scale", "norm_ffn", "norm_ple")
WEIGHTS = ("norm_mix", "w_in", "q_norm", "k_norm", "forget_bias", "gmlp_v_norm", "gmlp_w_s", "gmlp_b_s", "pool_w",
           "pool_scale", "w_out", "norm_ffn", "w_ffn_gate", "w_ffn_up", "w_ffn_down", "norm_ple", "w_ple_gate",
           "w_ple_proj")


def _tile(n, target, mult):
    best = None
    for t in range(mult, min(n, target) + 1, mult):
        if n % t == 0:
            best = t
    return best if best is not None else n


def _params(**kw):
    return pltpu.CompilerParams(vmem_limit_bytes=VMEM_LIMIT, **kw)


def _dot(a, b, kind):
    dims = {"nn": (((1,), (0,)), ((), ())), "nt": (((1,), (1,)), ((), ())), "tn": (((0,), (0,)), ((), ()))}[kind]
    return lax.dot_general(a.astype(bf16), b.astype(bf16), dims, preferred_element_type=f32)


def _heads_per_program(n_heads):
    return 2 if n_heads % 2 == 0 else 1


def _my_place():
    x, y, c = lax.axis_index("x"), lax.axis_index("y"), lax.axis_index("c")
    return x, y, c, 2 * x + y


def _matmul(name, kind, grid, pairs, extras, outs, epilogue=None, after=None):
    n_p, n_e = len(pairs), len(extras)
    tokens = [] if after is None else [(after, BS((8, 128), lambda *_: (0, 0)))]

    def body(*refs):
        a_refs, b_refs = refs[:n_p], refs[n_p:2 * n_p]
        e_refs = refs[2 * n_p:2 * n_p + n_e]
        o_refs = refs[2 * n_p + n_e + len(tokens):]
        accs = []
        for a_ref, b_ref in zip(a_refs, b_refs):
            if len(b_ref.shape) == 3:
                w = b_ref.shape[2]
                acc = None
                for s in range(b_ref.shape[0]):
                    d = _dot(a_ref[:, s * w:(s + 1) * w], b_ref[s], kind)
                    acc = d if acc is None else acc + d
            else:
                acc = _dot(a_ref[...], b_ref[...], kind)
            accs.append(acc)
        if epilogue is None:
            res = accs[0]
            for t in accs[1:]:
                res = res + t
            res = (res,)
        else:
            res = epilogue(accs, [e[...] for e in e_refs])
        for o_ref, o in zip(o_refs, res):
            o_ref[...] = o.astype(o_ref.dtype)

    in_arrays = [p[0][0] for p in pairs] + [p[1][0] for p in pairs] + [e[0] for e in extras + tokens]
    in_specs = [p[0][1] for p in pairs] + [p[1][1] for p in pairs] + [e[1] for e in extras + tokens]
    res = pl.pallas_call(
        body, name=name, grid=grid, in_specs=in_specs,
        out_specs=[o[2] for o in outs], out_shape=[S(o[0], o[1]) for o in outs],
        compiler_params=_params(),
    )(*in_arrays)
    return res


def _rms_fwd(name, x, g):
    T, D = x.shape
    tr = _tile(T, 256, 8)

    def body(x_ref, g_ref, o_ref):
        xv = x_ref[...]
        r = lax.rsqrt(jnp.mean(xv * xv, axis=-1, keepdims=True) + EPS)
        o_ref[...] = (xv * r * g_ref[...]).astype(o_ref.dtype)

    return pl.pallas_call(
        body, name=name, grid=(T // tr,),
        in_specs=[BS((tr, D), lambda i: (i, 0)), BS((1, D), lambda i: (0, 0))],
        out_specs=BS((tr, D), lambda i: (i, 0)), out_shape=S((T, D), bf16),
    )(x, g.reshape(1, D))


def _rms_bwd(name, dxn, x, g, dres):
    T, D = x.shape
    tr = _tile(T, 256, 8)

    def body(dxn_ref, x_ref, g_ref, dres_ref, dx_ref, dxb_ref, dg_ref):
        i = pl.program_id(0)
        xv = x_ref[...]
        r = lax.rsqrt(jnp.mean(xv * xv, axis=-1, keepdims=True) + EPS)
        xh = xv * r
        dxn_v = dxn_ref[...]
        dxh = dxn_v * g_ref[...]
        dx = dres_ref[...] + r * (dxh - xh * jnp.mean(dxh * xh, axis=-1, keepdims=True))
        dx_ref[...] = dx
        dxb_ref[...] = dx.astype(bf16)
        part = jnp.sum(dxn_v * xh, axis=0, keepdims=True)

        @pl.when(i == 0)
        def _():
            dg_ref[...] = part

        @pl.when(i > 0)
        def _():
            dg_ref[...] += part

    row = BS((tr, D), lambda i: (i, 0))
    vec = BS((1, D), lambda i: (0, 0))
    return pl.pallas_call(
        body, name=name, grid=(T // tr,),
        in_specs=[row, row, vec, row], out_specs=[row, row, vec],
        out_shape=[S((T, D), f32), S((T, D), bf16), S((1, D), f32)],
    )(dxn, x, g.reshape(1, D), dres)


def _loss_grad(name, y, tgt):
    T, D = y.shape
    tr = _tile(T, 256, 8)

    def body(y_ref, t_ref, dy_ref, l_ref):
        i = pl.program_id(0)
        e = y_ref[...] - t_ref[...]
        dy_ref[...] = e * (1.0 / D)
        part = 0.5 * jnp.sum(jnp.mean(e * e, axis=-1, keepdims=True), axis=0, keepdims=True)

        @pl.when(i == 0)
        def _():
            l_ref[...] = jnp.zeros_like(l_ref)

        l_ref[...] += jnp.broadcast_to(part, l_ref.shape)

    row = BS((tr, D), lambda i: (i, 0))
    return pl.pallas_call(
        body, name=name, grid=(T // tr,), in_specs=[row, row],
        out_specs=[row, BS((8, 128), lambda i: (0, 0))],
        out_shape=[S((T, D), f32), S((8, 128), f32)],
    )(y, tgt)


def _ple_bwd_elem(name, dh, gate, e):
    T, D = dh.shape
    tr = _tile(T, 256, 16)

    def body(dh_ref, g_ref, e_ref, de_ref, dz_ref):
        d = dh_ref[...]
        g = g_ref[...].astype(f32)
        de_ref[...] = (d * g).astype(bf16)
        dz_ref[...] = (d * e_ref[...].astype(f32) * g * (1.0 - g)).astype(bf16)

    row = BS((tr, D), lambda i: (i, 0))
    return pl.pallas_call(
        body, name=name, grid=(T // tr,), in_specs=[row, row, row], out_specs=[row, row],
        out_shape=[S((T, D), bf16), S((T, D), bf16)],
    )(dh, gate, e)


def _gelu_and_grad(x):
    k0, k1 = 0.7978845608028654, 0.044715
    th = jnp.tanh(k0 * (x + k1 * x * x * x))
    val = 0.5 * x * (1.0 + th)
    grad = 0.5 * (1.0 + th) + 0.5 * x * (1.0 - th * th) * (k0 * (1.0 + 3.0 * k1 * x * x))
    return val, grad


def _fgate_fwd(name, pf, fb):
    T = pf.shape[0]

    def body(pf_ref, fb_ref, c_ref, ct_ref):
        xv = jax.nn.log_sigmoid(pf_ref[...] + fb_ref[...])
        row = lax.broadcasted_iota(jnp.int32, xv.shape, 0)
        s = 1
        while s < T:
            xv = xv + jnp.where(row >= s, pltpu.roll(xv, s, 0), 0.0)
            s *= 2
        c_ref[...] = xv
        ct_ref[...] = xv.T

    return pl.pallas_call(body, name=name, out_shape=[S((T, HEAD), f32), S((HEAD, T), f32)])(pf, fb)


def _fgate_bwd(name, dct, pf, fb):
    T = pf.shape[0]

    def body(dct_ref, pf_ref, fb_ref, dpf_ref, dfb_ref):
        xv = dct_ref[...].T
        row = lax.broadcasted_iota(jnp.int32, xv.shape, 0)
        s = 1
        while s < T:
            xv = xv + jnp.where(row + s < T, pltpu.roll(xv, T - s, 0), 0.0)
            s *= 2
        df = xv * jax.nn.sigmoid(-(pf_ref[...] + fb_ref[...]))
        dpf_ref[...] = df.astype(bf16)
        dfb_ref[...] = jnp.sum(df, axis=0, keepdims=True)

    return pl.pallas_call(body, name=name, out_shape=[S((T, HEAD), bf16), S((1, HEAD), f32)])(dct, pf, fb)


def _qk_norm(name, P, qg, kg, A):
    T = P.shape[0]
    tr = _tile(T, 256, 16)
    n_heads = A // HEAD

    def body(q_ref, k_ref, v_ref, qg_ref, kg_ref, qn_ref, kn_ref, vb_ref):
        for h in range(n_heads):
            sl = slice(h * HEAD, (h + 1) * HEAD)
            for src, g_ref, dst in ((q_ref, qg_ref, qn_ref), (k_ref, kg_ref, kn_ref)):
                xv = src[:, sl]
                r = lax.rsqrt(jnp.mean(xv * xv, axis=-1, keepdims=True) + EPS)
                dst[:, sl] = (xv * r * g_ref[...]).astype(bf16)
        vb_ref[...] = v_ref[...].astype(bf16)

    vec = BS((1, HEAD), lambda i: (0, 0))
    out = BS((tr, A), lambda i: (i, 0))
    return pl.pallas_call(
        body, name=name, grid=(T // tr,),
        in_specs=[BS((tr, A), lambda i: (i, 0)), BS((tr, A), lambda i: (i, 1)), BS((tr, A), lambda i: (i, 2)), vec, vec],
        out_specs=[out, out, out], out_shape=[S((T, A), bf16)] * 3,
    )(P, P, P, qg, kg)


def _attn_fwd(name, qn, kn, vb, c_col, c_row, tb):
    T, A = qn.shape
    H = A // HEAD
    nb = T // tb
    scale = HEAD ** -0.5
    hp = _heads_per_program(H)
    wide = hp * HEAD

    def body(q_ref, k_ref, v_ref, cq_ref, ck_ref, o_ref, o32_ref, lse_ref):
        i = pl.program_id(1)
        below = lax.broadcasted_iota(jnp.int32, (tb, tb), 0) >= lax.broadcasted_iota(jnp.int32, (tb, tb), 1)

        def block(j, carry, diagonal):
            koff = pl.multiple_of(j * tb, tb)
            out = []
            for hh in range(hp):
                m, l, acc = carry[hh]
                sl = slice(hh * HEAD, (hh + 1) * HEAD)
                k = k_ref[pl.ds(koff, tb), sl]
                v = v_ref[pl.ds(koff, tb), sl]
                s = _dot(q_ref[:, sl], k, "nt") * scale + (cq_ref[hh] - ck_ref[hh, j])
                if diagonal:
                    s = jnp.where(below, s, NEG)
                m_new = jnp.maximum(m, jnp.max(s, axis=-1, keepdims=True))
                alpha = jnp.exp(m - m_new)
                p = jnp.exp(s - m_new)
                l = l * alpha + jnp.sum(p, axis=-1, keepdims=True)
                acc = acc * alpha + _dot(p, v, "nn")
                out.append((m_new, l, acc))
            return tuple(out)

        init = tuple((jnp.full((tb, 1), NEG, f32), jnp.zeros((tb, 1), f32), jnp.zeros((tb, HEAD), f32))
                     for _ in range(hp))
        carry = lax.fori_loop(0, i, lambda j, c: block(j, c, False), init)
        carry = block(i, carry, True)
        for hh in range(hp):
            m, l, acc = carry[hh]
            sl = slice(hh * HEAD, (hh + 1) * HEAD)
            o = acc / l
            o_ref[:, sl] = o.astype(bf16)
            o32_ref[:, sl] = o
            lse_ref[hh] = m + jnp.log(l)

    return pl.pallas_call(
        body, name=name, grid=(H // hp, nb),
        in_specs=[BS((tb, wide), lambda h, i: (i, h)), BS((T, wide), lambda h, i: (0, h)),
                  BS((T, wide), lambda h, i: (0, h)), BS((hp, tb, 1), lambda h, i: (h, i, 0)),
                  BS((hp, nb, 1, tb), lambda h, i: (h, 0, 0, 0))],
        out_specs=[BS((tb, wide), lambda h, i: (i, h)), BS((tb, wide), lambda h, i: (i, h)),
                   BS((hp, tb, 1), lambda h, i: (h, i, 0))],
        out_shape=[S((T, A), bf16), S((T, A), f32), S((H, T, 1), f32)],
    )(qn, kn, vb, c_col, c_row)


def _attn_bwd(name, qn, kn, vb, o, dmix, lse, c_col, c_row, P, qg, kg, tb):
    T, A = qn.shape
    H = A // HEAD
    nb = T // tb
    scale = HEAD ** -0.5
    hp = _heads_per_program(H)
    wide = hp * HEAD

    def body(q_ref, k_ref, v_ref, o_ref, do_ref, lse_ref, cq_ref, ck_ref, qraw_ref, kraw_ref, qg_ref, kg_ref,
             dq_out, dk_out, dv_out, dc_out, dqg_out, dkg_out, dq_acc, dk_acc, delta_s):
        h = pl.program_id(0)
        dq_acc[...] = jnp.zeros_like(dq_acc)
        below = lax.broadcasted_iota(jnp.int32, (tb, tb), 0) >= lax.broadcasted_iota(jnp.int32, (tb, tb), 1)
        for hh in range(hp):
            sl = slice(hh * HEAD, (hh + 1) * HEAD)
            delta_s[hh] = jnp.sum(do_ref[:, sl].astype(bf16).astype(f32) * o_ref[:, sl], axis=-1, keepdims=True)

        def kblock(j, _):
            koff = pl.multiple_of(j * tb, tb)

            def qblock(i, carry, diagonal):
                qoff = pl.multiple_of(i * tb, tb)
                out = []
                for hh in range(hp):
                    dk, dv, dc = carry[hh]
                    sl = slice(hh * HEAD, (hh + 1) * HEAD)
                    k = k_ref[pl.ds(koff, tb), sl]
                    v = v_ref[pl.ds(koff, tb), sl]
                    q = q_ref[pl.ds(qoff, tb), sl]
                    do = do_ref[pl.ds(qoff, tb), sl].astype(bf16)
                    s = _dot(q, k, "nt") * scale + (cq_ref[hh, pl.ds(qoff, tb), :] - ck_ref[hh, j])
                    if diagonal:
                        s = jnp.where(below, s, NEG)
                    p = jnp.exp(s - lse_ref[hh, pl.ds(qoff, tb), :])
                    dv = dv + _dot(p, do, "tn")
                    dp = _dot(do, v, "nt")
                    ds = p * (dp - delta_s[hh, pl.ds(qoff, tb), :])
                    dc = dc - jnp.sum(ds, axis=0, keepdims=True)
                    dsb = (ds * scale).astype(bf16)
                    dk = dk + _dot(dsb, q, "tn")
                    dq_acc[pl.ds(qoff, tb), sl] += _dot(dsb, k, "nn")
                    out.append((dk, dv, dc))
                return tuple(out)

            init = tuple((jnp.zeros((tb, HEAD), f32), jnp.zeros((tb, HEAD), f32), jnp.zeros((1, tb), f32))
                         for _ in range(hp))
            carry = qblock(j, init, True)
            carry = lax.fori_loop(j + 1, nb, lambda i, c: qblock(i, c, False), carry)
            for hh in range(hp):
                dk, dv, dc = carry[hh]
                sl = slice(hh * HEAD, (hh + 1) * HEAD)
                dk_acc[pl.ds(koff, tb), sl] = dk
                dv_out[pl.ds(koff, tb), sl] = dv.astype(bf16)
                dc_out[hh, j] = dc
            return 0

        lax.fori_loop(0, nb, kblock, 0)

        for raw_ref, g_ref, acc_ref, d_out, dg_out in ((qraw_ref, qg_ref, dq_acc, dq_out, dqg_out),
                                                       (kraw_ref, kg_ref, dk_acc, dk_out, dkg_out)):
            part = jnp.zeros((1, HEAD), f32)
            for hh in range(hp):
                sl = slice(hh * HEAD, (hh + 1) * HEAD)
                xv = raw_ref[:, sl]
                r = lax.rsqrt(jnp.mean(xv * xv, axis=-1, keepdims=True) + EPS)
                xh = xv * r
                dn = acc_ref[:, sl]
                dxh = dn * g_ref[...]
                d_out[:, sl] = (r * (dxh - xh * jnp.mean(dxh * xh, axis=-1, keepdims=True))).astype(bf16)
                part = part + jnp.sum(dn * xh, axis=0, keepdims=True)

            @pl.when(h == 0)
            def _():
                dg_out[...] = part

            @pl.when(h > 0)
            def _():
                dg_out[...] += part

    heads = lambda off: BS((T, wide), lambda h: (0, off + h))
    col = BS((hp, T, 1), lambda h: (h, 0, 0))
    row = BS((hp, nb, 1, tb), lambda h: (h, 0, 0, 0))
    vec = BS((1, HEAD), lambda h: (0, 0))
    return pl.pallas_call(
        body, name=name, grid=(H // hp,),
        in_specs=[heads(0), heads(0), heads(0), heads(0), heads(0), col, col, row, heads(0), heads(H // hp), vec, vec],
        out_specs=[heads(0), heads(0), heads(0), row, vec, vec],
        out_shape=[S((T, A), bf16)] * 3 + [S((H, nb, 1, tb), f32), S((1, HEAD), f32), S((1, HEAD), f32)],
        scratch_shapes=[pltpu.VMEM((T, wide), f32), pltpu.VMEM((T, wide), f32), pltpu.VMEM((hp, T, 1), f32)],
        compiler_params=_params(),
    )(qn, kn, vb, o, dmix, lse, c_col, c_row, P, P, qg, kg)


def _gmlp_fwd(name, P, gain, ws, b, col_u, col_v, Wd):
    T = P.shape[0]
    G = Wd // HEAD
    tr = _tile(T, 512, HEAD)

    def body(u_ref, v_ref, gain_ref, ws_ref, b_ref, y_ref):
        tril = lax.broadcasted_iota(jnp.int32, (HEAD, HEAD), 0) >= lax.broadcasted_iota(jnp.int32, (HEAD, HEAD), 1)
        wm = jnp.where(tril, ws_ref[...], 0.0).astype(bf16)
        for n in range(tr // HEAD):
            rows = slice(n * HEAD, (n + 1) * HEAD)
            u = jax.nn.gelu(u_ref[rows, :])
            a = jax.nn.gelu(v_ref[rows, :])
            r = lax.rsqrt(jnp.mean(a * a, axis=-1, keepdims=True) + EPS)
            vn = a * r * gain_ref[...]
            mixed = _dot(wm, vn, "nn") + b_ref[...]
            y_ref[rows, :] = (u * mixed).astype(bf16)

    return pl.pallas_call(
        body, name=name, grid=(G, T // tr),
        in_specs=[BS((tr, HEAD), lambda g, i: (i, col_u + g)), BS((tr, HEAD), lambda g, i: (i, col_v + g)),
                  BS((None, 1, HEAD), lambda g, i: (g, 0, 0)), BS((None, HEAD, HEAD), lambda g, i: (g, 0, 0)),
                  BS((None, HEAD, 1), lambda g, i: (g, 0, 0))],
        out_specs=BS((tr, HEAD), lambda g, i: (i, g)), out_shape=S((T, Wd), bf16),
    )(P, P, gain, ws, b)


def _gmlp_bwd(name, P, dmix, gain, ws, b, col_u, col_v, col_dy, Wd):
    T = P.shape[0]
    G = Wd // HEAD
    tr = _tile(T, 512, HEAD)

    def body(u_ref, v_ref, dy_ref, gain_ref, ws_ref, b_ref, du_ref, dv_ref, dws_ref, db_ref, dgain_ref):
        i = pl.program_id(1)
        tril = lax.broadcasted_iota(jnp.int32, (HEAD, HEAD), 0) >= lax.broadcasted_iota(jnp.int32, (HEAD, HEAD), 1)
        wm = jnp.where(tril, ws_ref[...], 0.0).astype(bf16)
        gain_v = gain_ref[...]
        dw = jnp.zeros((HEAD, HEAD), f32)
        db = jnp.zeros((HEAD, 1), f32)
        dgain = jnp.zeros((1, HEAD), f32)
        for n in range(tr // HEAD):
            rows = slice(n * HEAD, (n + 1) * HEAD)
            u, du_dx = _gelu_and_grad(u_ref[rows, :])
            a, da_dx = _gelu_and_grad(v_ref[rows, :])
            dy = dy_ref[rows, :]
            r = lax.rsqrt(jnp.mean(a * a, axis=-1, keepdims=True) + EPS)
            ah = a * r
            vnb = (ah * gain_v).astype(bf16)
            mixed = _dot(wm, vnb, "nn") + b_ref[...]
            dm = dy * u
            dmb = dm.astype(bf16)
            du_ref[rows, :] = (dy * mixed * du_dx).astype(bf16)
            db = db + jnp.sum(dm, axis=1, keepdims=True)
            dw = dw + _dot(dmb, vnb, "nt")
            dvn = _dot(wm, dmb, "tn")
            dgain = dgain + jnp.sum(dvn * ah, axis=0, keepdims=True)
            dah = dvn * gain_v
            da = r * (dah - ah * jnp.mean(dah * ah, axis=-1, keepdims=True))
            dv_ref[rows, :] = (da * da_dx).astype(bf16)
        dw = jnp.where(tril, dw, 0.0)

        @pl.when(i == 0)
        def _():
            dws_ref[...] = dw
            db_ref[...] = db
            dgain_ref[...] = dgain

        @pl.when(i > 0)
        def _():
            dws_ref[...] += dw
            db_ref[...] += db
            dgain_ref[...] += dgain

    out = BS((tr, HEAD), lambda g, i: (i, g))
    return pl.pallas_call(
        body, name=name, grid=(G, T // tr),
        in_specs=[BS((tr, HEAD), lambda g, i: (i, col_u + g)), BS((tr, HEAD), lambda g, i: (i, col_v + g)),
                  BS((tr, HEAD), lambda g, i: (i, col_dy + g)),
                  BS((None, 1, HEAD), lambda g, i: (g, 0, 0)), BS((None, HEAD, HEAD), lambda g, i: (g, 0, 0)),
                  BS((None, HEAD, 1), lambda g, i: (g, 0, 0))],
        out_specs=[out, out, BS((None, HEAD, HEAD), lambda g, i: (g, 0, 0)), BS((None, HEAD, 1), lambda g, i: (g, 0, 0)),
                   BS((None, 1, HEAD), lambda g, i: (g, 0, 0))],
        out_shape=[S((T, Wd), bf16), S((T, Wd), bf16), S((G, HEAD, HEAD), f32), S((G, HEAD, 1), f32),
                   S((G, 1, HEAD), f32)],
    )(P, P, dmix, gain, ws, b)


def _trailing_window(xv, w, row):
    k = 1
    while k < w:
        xv = xv + jnp.where(row >= k, pltpu.roll(xv, k, 0), 0.0)
        k *= 2
    return xv


def _leading_window(xv, w, row, T):
    k = 1
    while k < w:
        xv = xv + jnp.where(row + k < T, pltpu.roll(xv, T - k, 0), 0.0)
        k *= 2
    return xv


def _pool_fwd(name, P, pw, ps, col_x, Wd):
    T = P.shape[0]
    Gp = Wd // HEAD

    def body(x_ref, pw_ref, ps_ref, y_ref):
        row = lax.broadcasted_iota(jnp.int32, (T, HEAD), 0)
        for g in range(Gp):
            w = POOL_WINDOWS[g]
            sl = slice(g * HEAD, (g + 1) * HEAD)
            xv = x_ref[:, sl]
            cnt = jnp.minimum(row + 1, w).astype(f32)
            d = _trailing_window(xv, w, row) / cnt - xv
            y_ref[:, sl] = (_dot(d, pw_ref[g], "nn") * ps_ref[:, sl]).astype(bf16)

    return pl.pallas_call(
        body, name=name, grid=(1,),
        in_specs=[BS((T, Wd), lambda i: (0, col_x)), BS((Gp, HEAD, HEAD), lambda i: (0, 0, 0)), BS((1, Wd), lambda i: (0, 0))],
        out_specs=BS((T, Wd), lambda i: (0, 0)), out_shape=S((T, Wd), bf16), compiler_params=_params(),
    )(P, pw, ps)


def _pool_bwd(name, P, dmix, pw, ps, col_x, col_dy, Wd):
    T = P.shape[0]
    Gp = Wd // HEAD

    def body(x_ref, dy_ref, pw_ref, ps_ref, dx_ref, dpw_ref, dps_ref):
        row = lax.broadcasted_iota(jnp.int32, (T, HEAD), 0)
        for g in range(Gp):
            w = POOL_WINDOWS[g]
            sl = slice(g * HEAD, (g + 1) * HEAD)
            xv = x_ref[:, sl]
            cnt = jnp.minimum(row + 1, w).astype(f32)
            d = (_trailing_window(xv, w, row) / cnt - xv).astype(bf16)
            pwb = pw_ref[g].astype(bf16)
            z = _dot(d, pwb, "nn")
            dy = dy_ref[:, sl]
            dps_ref[:, sl] = jnp.sum(dy * z, axis=0, keepdims=True)
            dzb = (dy * ps_ref[:, sl]).astype(bf16)
            dpw_ref[g] = _dot(d, dzb, "tn")
            dd = _dot(dzb, pwb, "nt")
            dx_ref[:, sl] = (_leading_window(dd / cnt, w, row, T) - dd).astype(bf16)

    return pl.pallas_call(
        body, name=name, grid=(1,),
        in_specs=[BS((T, Wd), lambda i: (0, col_x)), BS((T, Wd), lambda i: (0, col_dy)),
                  BS((Gp, HEAD, HEAD), lambda i: (0, 0, 0)), BS((1, Wd), lambda i: (0, 0))],
        out_specs=[BS((T, Wd), lambda i: (0, 0)), BS((Gp, HEAD, HEAD), lambda i: (0, 0, 0)), BS((1, Wd), lambda i: (0, 0))],
        out_shape=[S((T, Wd), bf16), S((Gp, HEAD, HEAD), f32), S((1, Wd), f32)], compiler_params=_params(),
    )(P, dmix, pw, ps)


def _adamw(name, w, g, m, v):
    R, C = w.shape
    lanes = -(-C // 128) * 128
    tr = _tile(R, max(8, (512 * 1024) // lanes // 8 * 8), 8)
    c1 = 1.0 - ADAM_B1 ** ADAM_STEP
    c2 = 1.0 - ADAM_B2 ** ADAM_STEP

    def body(w_ref, g_ref, m_ref, v_ref, d_ref, nm_ref, nv_ref):
        gv = g_ref[...]
        nm = ADAM_B1 * m_ref[...] + (1.0 - ADAM_B1) * gv
        nv = ADAM_B2 * v_ref[...] + (1.0 - ADAM_B2) * (gv * gv)
        d_ref[...] = -ADAM_LR * ((nm / c1) / (jnp.sqrt(nv / c2) + ADAM_EPS) + ADAM_WD * w_ref[...])
        nm_ref[...] = nm
        nv_ref[...] = nv

    blk = BS((tr, C), lambda i: (i, 0))
    return pl.pallas_call(
        body, name=name, grid=(R // tr,), in_specs=[blk] * 4, out_specs=[blk] * 3, out_shape=[S((R, C), f32)] * 3,
    )(w, g, m, v)


def _adamw_layer(name, layer, w_all, m_all, v_all, g, prev):
    L, R, C = w_all.shape
    lanes = -(-C // 128) * 128
    tr = _tile(R, max(8, (512 * 1024) // lanes // 8 * 8), 8)
    c1 = 1.0 - ADAM_B1 ** ADAM_STEP
    c2 = 1.0 - ADAM_B2 ** ADAM_STEP
    n_prev = 0 if prev is None else 4

    def body(w_ref, m_ref, v_ref, g_ref, *rest):
        go_ref, d_ref, nm_ref, nv_ref = rest[n_prev:]
        gv = g_ref[...]
        nm = ADAM_B1 * m_ref[...] + (1.0 - ADAM_B1) * gv
        nv = ADAM_B2 * v_ref[...] + (1.0 - ADAM_B2) * (gv * gv)
        d_ref[...] = -ADAM_LR * ((nm / c1) / (jnp.sqrt(nv / c2) + ADAM_EPS) + ADAM_WD * w_ref[...])
        nm_ref[...] = nm
        nv_ref[...] = nv
        go_ref[...] = gv

    slab = BS((None, tr, C), lambda r: (layer, r, 0))
    return pl.pallas_call(
        body, name=name, grid=(R // tr,),
        in_specs=[slab, slab, slab, BS((tr, C), lambda r: (r, 0))] + [ANY] * n_prev,
        out_specs=[slab] * 4, out_shape=[S((L, R, C), f32)] * 4,
        input_output_aliases={4 + k: k for k in range(n_prev)},
    )(w_all, m_all, v_all, g, *(prev or ()))


def _chip_of(k):
    return k // 2, k % 2


def _remote(src, dst, send_sems, recv_sems, idx, dev):
    return pltpu.make_async_remote_copy(src_ref=src, dst_ref=dst, send_sem=send_sems.at[idx], recv_sem=recv_sems.at[idx],
                                        device_id=dev, device_id_type=MESH)


def _plan_gather_chips(n):
    def plan(refs, ss, rs, base):
        ins, lands = refs[:n], refs[n:]
        x, y, c, j0 = _my_place()
        sib = (x, y, 1 - c)
        sends, recvs = [], []
        for a in range(n):
            half = ins[a].shape[0] // 2
            lo = c * half
            sends.append(_remote(ins[a], lands[a].at[j0], ss, rs, base + 4 * a + 3, sib))
            recvs.append(_remote(lands[a].at[j0], lands[a].at[j0], ss, rs, base + 4 * a + 3, sib))
            for r in (1, 2, 3):
                k = j0 ^ r
                dev = (*_chip_of(k), c)
                sends.append(_remote(ins[a].at[pl.ds(lo, half)], lands[a].at[j0, pl.ds(lo, half)], ss, rs,
                                     base + 4 * a + r - 1, dev))
                landed = lands[a].at[k, pl.ds(lo, half)]
                recvs.append(_remote(landed, landed, ss, rs, base + 4 * a + r - 1, dev))
        return sends, recvs
    return plan, 4 * n


def _plan_gather_forward(n):
    def plan(refs, ss, rs, base):
        x, y, c, j0 = _my_place()
        sib = (x, y, 1 - c)
        sends, recvs = [], []
        for a in range(n):
            half = refs[a].shape[1] // 2
            for r in (1, 2, 3):
                k = j0 ^ r
                landed = refs[a].at[k, pl.ds(c * half, half)]
                other = refs[a].at[k, pl.ds((1 - c) * half, half)]
                sends.append(_remote(landed, landed, ss, rs, base + 3 * a + r - 1, sib))
                recvs.append(_remote(other, other, ss, rs, base + 3 * a + r - 1, sib))
        return sends, recvs
    return plan, 3 * n


def _plan_sibling_halves(n):
    def plan(refs, ss, rs, base):
        ins, lands = refs[:n], refs[n:]
        x, y, c, _ = _my_place()
        sib = (x, y, 1 - c)
        sends, recvs = [], []
        for a in range(n):
            half = ins[a].shape[1] // 2
            sends.append(_remote(ins[a].at[:, pl.ds((1 - c) * half, half), :], lands[a], ss, rs, base + a, sib))
            recvs.append(_remote(lands[a], lands[a], ss, rs, base + a, sib))
        return sends, recvs
    return plan, n


def _plan_chip_scatter(n):
    def plan(refs, ss, rs, base):
        ins, lands = refs[:n], refs[n:]
        x, y, c, j0 = _my_place()
        sends, recvs = [], []
        for a in range(n):
            for r in (1, 2, 3):
                k = j0 ^ r
                dev = (*_chip_of(k), c)
                sends.append(_remote(ins[a].at[k], lands[a].at[j0], ss, rs, base + 3 * a + r - 1, dev))
                recvs.append(_remote(lands[a].at[k], lands[a].at[k], ss, rs, base + 3 * a + r - 1, dev))
        return sends, recvs
    return plan, 3 * n


def _plan_sibling_join(n):
    def plan(refs, ss, rs, base):
        x, y, c, _ = _my_place()
        sib = (x, y, 1 - c)
        sends, recvs = [], []
        for a in range(n):
            half = refs[a].shape[0] // 2
            mine = refs[a].at[pl.ds(c * half, half)]
            theirs = refs[a].at[pl.ds((1 - c) * half, half)]
            sends.append(_remote(mine, mine, ss, rs, base + a, sib))
            recvs.append(_remote(theirs, theirs, ss, rs, base + a, sib))
        return sends, recvs
    return plan, n


_HBM = pl.BlockSpec(memory_space=pltpu.HBM)
_SEM = pl.BlockSpec(memory_space=pltpu.SEMAPHORE)
_EFFECT = pltpu.SideEffectType.DATAFLOW_SIDE_EFFECTING


def _exchange_start(name, plan, bufs, after):
    plan_fn, n_sems = plan
    n = len(bufs)

    def body(*refs):
        ss, rs, token = refs[n + len(after)], refs[n + len(after) + 1], refs[-1]
        sends, _ = plan_fn(refs[:n], ss, rs, 0)
        for cp in sends:
            cp.start()
        token[...] = jnp.zeros_like(token)

    res = pl.pallas_call(
        body, name=name,
        out_shape=(pltpu.SemaphoreType.DMA((n_sems,)), pltpu.SemaphoreType.DMA((n_sems,)),
                   *[pltpu.HBM(b.shape, b.dtype) for b in bufs], S((8, 128), f32)),
        in_specs=[_HBM] * n + [ANY] * len(after),
        out_specs=(_SEM, _SEM, *[_HBM] * n, pl.BlockSpec(memory_space=pltpu.VMEM)),
        input_output_aliases={k: 2 + k for k in range(n)},
        compiler_params=pltpu.CompilerParams(has_side_effects=_EFFECT),
    )(*[pltpu.with_memory_space_constraint(b, pltpu.HBM) for b in bufs], *after)
    return res[0], res[1], list(res[2:2 + n]), res[-1]


def _exchange_wait(name, plan, send_sems, recv_sems, bufs, after):
    plan_fn, _ = plan
    n = len(bufs)

    def body(*refs):
        ss, rs, token = refs[n], refs[n + 1], refs[-1]
        sends, recvs = plan_fn(refs[:n], ss, rs, 0)
        for cp in recvs:
            cp.wait_recv()
        for cp in sends:
            cp.wait_send()
        token[...] = jnp.zeros_like(token)

    res = pl.pallas_call(
        body, name=name,
        out_shape=(*[pltpu.HBM(b.shape, b.dtype) for b in bufs], S((8, 128), f32)),
        in_specs=[_HBM] * n + [_SEM, _SEM] + [ANY] * len(after),
        out_specs=(*[_HBM] * n, pl.BlockSpec(memory_space=pltpu.VMEM)),
        input_output_aliases={k: k for k in range(n)},
        compiler_params=pltpu.CompilerParams(has_side_effects=_EFFECT),
    )(*bufs, send_sems, recv_sems, *after)
    return list(res[:n]), res[-1]


class _Order:
    def __init__(self, first):
        self.marker = first
        self.token = None

    def _after(self):
        return [self.marker] + ([] if self.token is None else [self.token])

    def start(self, name, plan, bufs):
        ss, rs, thru, self.token = _exchange_start(name, plan, bufs, self._after())
        return name, plan, ss, rs, thru

    def wait(self, handle):
        name, plan, ss, rs, thru = handle
        out, self.token = _exchange_wait(name + "_wait", plan, ss, rs, thru, self._after())
        return out

    def follows(self, small):
        return small if self.token is None else small + self.token[0, 0]

    def done(self, result):
        self.marker = result[(slice(0, 1),) * result.ndim].reshape(1, 1)


def _all_reduce_small(name, g8):
    _, R, L = g8.shape

    def body(g_ref, out_ref, land, red, send1, recv1, send2, recv2):
        x, y, c, _ = _my_place()
        me = 4 * x + 2 * y + c
        peers = []
        for r in range(1, N_DEV):
            q = me ^ r
            peers.append((q, (q // 4, (q // 2) % 2, q % 2)))
        first = []
        for r, (q, dev) in enumerate(peers):
            cp = pltpu.make_async_remote_copy(src_ref=g_ref.at[q], dst_ref=land.at[me], send_sem=send1.at[r],
                                              recv_sem=recv1.at[r], device_id=dev, device_id_type=MESH)
            cp.start()
            first.append(cp)
        land[me] = g_ref[me]
        for r, (q, dev) in enumerate(peers):
            pltpu.make_async_remote_copy(src_ref=land.at[q], dst_ref=land.at[q], send_sem=send1.at[r],
                                         recv_sem=recv1.at[r], device_id=dev, device_id_type=MESH).wait_recv()
        acc = land[0]
        for d in range(1, N_DEV):
            acc = acc + land[d]
        red[...] = acc
        out_ref[me] = acc
        second = []
        for r, (q, dev) in enumerate(peers):
            cp = pltpu.make_async_remote_copy(src_ref=red, dst_ref=out_ref.at[me], send_sem=send2.at[r],
                                              recv_sem=recv2.at[r], device_id=dev, device_id_type=MESH)
            cp.start()
            second.append(cp)
        for r, (q, dev) in enumerate(peers):
            pltpu.make_async_remote_copy(src_ref=out_ref.at[q], dst_ref=out_ref.at[q], send_sem=send2.at[r],
                                         recv_sem=recv2.at[r], device_id=dev, device_id_type=MESH).wait_recv()
        for cp in first + second:
            cp.wait_send()

    vm = pl.BlockSpec(memory_space=pltpu.VMEM)
    return pl.pallas_call(
        body, name=name, in_specs=[vm], out_specs=vm, out_shape=S(g8.shape, f32),
        scratch_shapes=[pltpu.VMEM((N_DEV, R, L), f32), pltpu.VMEM((R, L), f32)]
        + [pltpu.SemaphoreType.DMA((N_DEV - 1,))] * 4,
        compiler_params=_params(),
    )(g8)


def _pair_sum(name, g4, sib):
    _, rows, cols = g4.shape
    half = rows // 2
    lanes = -(-cols // 128) * 128
    tr = _tile(half, max(16, (512 * 1024) // lanes // 16 * 16), 16)
    nb = half // tr

    def body(g_ref, s_ref, pb_ref, own_ref):
        j = pl.program_id(1)
        t = g_ref[...] + s_ref[...]
        pb_ref[...] = t.astype(bf16)

        @pl.when(j == _my_place()[3])
        def _():
            own_ref[...] = t

    return pl.pallas_call(
        body, name=name, grid=(nb, N_CHIPS),
        in_specs=[BS((None, tr, cols), lambda i, j: (j, lax.axis_index("c") * nb + i, 0)),
                  BS((None, tr, cols), lambda i, j: (j, i, 0))],
        out_specs=[BS((None, tr, cols), lambda i, j: (j, i, 0)), BS((tr, cols), lambda i, j: (i, 0))],
        out_shape=[S((N_CHIPS, half, cols), bf16), S((half, cols), f32)],
    )(g4, sib)


def _chip_sum(name, own, got):
    half, cols = own.shape
    lanes = -(-cols // 128) * 128
    tr = _tile(half, max(16, (512 * 1024) // lanes // 16 * 16), 16)
    nb = half // tr

    def body(own_ref, *rest):
        got_refs, o_ref = rest[:N_CHIPS], rest[N_CHIPS]
        j0 = _my_place()[3]
        acc = None
        for k in range(N_CHIPS):
            t = jnp.where(j0 == k, own_ref[...], got_refs[k][...].astype(f32))
            acc = t if acc is None else acc + t
        o_ref[...] = acc

    def slot(k):
        return BS((None, tr, cols), lambda i: (jnp.where(_my_place()[3] == k, (k + 1) % N_CHIPS, k), i, 0))

    return pl.pallas_call(
        body, name=name, grid=(nb,),
        in_specs=[BS((tr, cols), lambda i: (i, 0))] + [slot(k) for k in range(N_CHIPS)],
        out_specs=BS((tr, cols), lambda i: (lax.axis_index("c") * nb + i, 0)),
        out_shape=S((2 * half, cols), f32),
    )(own, got, got, got, got)


def _pack_rows(arrs, n_rows):
    flat = jnp.concatenate([a.reshape(-1).astype(f32) for a in arrs])
    return jnp.pad(flat, (0, n_rows * 128 - flat.shape[0])).reshape(n_rows, 128)


def kernel(x, p, norm_mix, w_in, q_norm, k_norm, forget_bias, gmlp_v_norm, gmlp_w_s, gmlp_b_s, pool_w, pool_scale, w_out, norm_ffn, w_ffn_gate, w_ffn_up, w_ffn_down, norm_ple, w_ple_gate, w_ple_proj, loss_target, m_norm_mix, m_w_in, m_q_norm, m_k_norm, m_forget_bias, m_gmlp_v_norm, m_gmlp_w_s, m_gmlp_b_s, m_pool_w, m_pool_scale, m_w_out, m_norm_ffn, m_w_ffn_gate, m_w_ffn_up, m_w_ffn_down, m_norm_ple, m_w_ple_gate, m_w_ple_proj, v_norm_mix, v_w_in, v_q_norm, v_k_norm, v_forget_bias, v_gmlp_v_norm, v_gmlp_w_s, v_gmlp_b_s, v_pool_w, v_pool_scale, v_w_out, v_norm_ffn, v_w_ffn_gate, v_w_ffn_up, v_w_ffn_down, v_norm_ple, v_w_ple_gate, v_w_ple_proj):
    W = dict(norm_mix=norm_mix, w_in=w_in, q_norm=q_norm, k_norm=k_norm, forget_bias=forget_bias,
             gmlp_v_norm=gmlp_v_norm, gmlp_w_s=gmlp_w_s, gmlp_b_s=gmlp_b_s, pool_w=pool_w, pool_scale=pool_scale,
             w_out=w_out, norm_ffn=norm_ffn, w_ffn_gate=w_ffn_gate, w_ffn_up=w_ffn_up, w_ffn_down=w_ffn_down,
             norm_ple=norm_ple, w_ple_gate=w_ple_gate, w_ple_proj=w_ple_proj)
    M = dict(norm_mix=m_norm_mix, w_in=m_w_in, q_norm=m_q_norm, k_norm=m_k_norm, forget_bias=m_forget_bias,
             gmlp_v_norm=m_gmlp_v_norm, gmlp_w_s=m_gmlp_w_s, gmlp_b_s=m_gmlp_b_s, pool_w=m_pool_w,
             pool_scale=m_pool_scale, w_out=m_w_out, norm_ffn=m_norm_ffn, w_ffn_gate=m_w_ffn_gate,
             w_ffn_up=m_w_ffn_up, w_ffn_down=m_w_ffn_down, norm_ple=m_norm_ple, w_ple_gate=m_w_ple_gate,
             w_ple_proj=m_w_ple_proj)
    V = dict(norm_mix=v_norm_mix, w_in=v_w_in, q_norm=v_q_norm, k_norm=v_k_norm, forget_bias=v_forget_bias,
             gmlp_v_norm=v_gmlp_v_norm, gmlp_w_s=v_gmlp_w_s, gmlp_b_s=v_gmlp_b_s, pool_w=v_pool_w,
             pool_scale=v_pool_scale, w_out=v_w_out, norm_ffn=v_norm_ffn, w_ffn_gate=v_w_ffn_gate,
             w_ffn_up=v_w_ffn_up, w_ffn_down=v_w_ffn_down, norm_ple=v_norm_ple, w_ple_gate=v_w_ple_gate,
             w_ple_proj=v_w_ple_proj)

    L = w_in.shape[0]
    _, T, D = x.shape
    A, Wd = D // 2, D // 4
    H = A // HEAD
    G = gmlp_w_s.shape[1]
    Gp = pool_w.shape[1]
    DP4 = w_in.shape[2]
    DP = N_CHIPS * DP4
    NM = 3 * A + 3 * Wd
    FS = w_ffn_gate.shape[2]
    FF = N_CHIPS * FS
    DS = D // N_CHIPS
    PL = p.shape[-1]
    assert Wd // G == HEAD and Wd // Gp == HEAD and DP == NM + H and H <= HEAD
    assert all(w & (w - 1) == 0 for w in POOL_WINDOWS[:Gp])
    tb = _tile(T, 256, HEAD)
    nb = T // tb
    tm = _tile(T, 512, 16)
    tn = _tile(NM, 512, 128)
    tnd = _tile(D, 512, 128)
    tkf = _tile(FF, 512, 128)
    col_gu, col_gv, col_xp = 3 * A // HEAD, (3 * A + Wd) // HEAD, (3 * A + 2 * Wd) // Wd
    col_dg, col_dp = A // HEAD, (A + Wd) // Wd

    order = _Order(x[0, :1, :1])

    def gather_start(i, names, tag=""):
        shards = [W[n][i].astype(bf16) for n in names]
        lands = [lax.empty((N_CHIPS,) + s.shape, bf16) for s in shards]
        return names, order.start(f"ag_ici_{i}{tag}", _plan_gather_chips(len(names)), shards + lands)

    def gather_forward(i, started, tag=""):
        names, handle = started
        return names, order.start(f"ag_d2d_{i}{tag}", _plan_gather_forward(len(names)), order.wait(handle)[len(names):])

    def gathered(forwarded):
        names, handle = forwarded
        g = dict(zip(names, order.wait(handle)))
        out = {}
        if "w_in" in g:
            def cols(lo, hi):
                parts = []
                for j in range(N_CHIPS):
                    s, e = max(lo, j * DP4), min(hi, (j + 1) * DP4)
                    if s < e:
                        parts.append(g["w_in"][j][:, s - j * DP4:e - j * DP4])
                return parts

            out["w_main"] = jnp.concatenate(cols(0, 3 * A) + cols(3 * A + H, DP), axis=1)
            out["w_f"] = jnp.pad(jnp.concatenate(cols(3 * A, 3 * A + H), axis=1), ((0, 0), (0, HEAD - H)))
        if "w_out" in g:
            out["w_out"] = g["w_out"].reshape(D, D)
        if "w_ffn_gate" in g:
            out.update(w_gate=g["w_ffn_gate"], w_up=g["w_ffn_up"], w_down=g["w_ffn_down"].reshape(FF, D),
                       w_pg=g["w_ple_gate"].reshape(D, D), w_pp=g["w_ple_proj"])
        return out

    Wf = [None] * L
    first = [gather_start(0, ["w_in"], "a"), gather_start(0, ["w_out"], "b"), gather_start(0, BIG[2:], "c")]
    d2d = gather_forward(0, first[0], "a")
    ici = gather_start(1, BIG) if L > 1 else None
    Wf[0] = gathered(d2d)

    h = x.reshape(T, D)
    pb16 = p.reshape(L, T, PL).astype(bf16)
    saved = []

    for i in range(L):
        w = Wf[i]
        sv = dict(h0=h)
        xn1 = _rms_fwd(f"rms1_{i}", h, order.follows(norm_mix[i]))
        (P,) = _matmul(f"proj_{i}", "nn", (T // tm, NM // tn),
                       [((xn1, BS((tm, D), lambda i, j: (i, 0))), (w["w_main"], BS((D, tn), lambda i, j: (0, j))))], [],
                       [((T, NM), f32, BS((tm, tn), lambda i, j: (i, j)))])
        (Pf,) = _matmul(f"projf_{i}", "nn", (T // tm, 1),
                        [((xn1, BS((tm, D), lambda i, j: (i, 0))), (w["w_f"], BS((D, HEAD), lambda i, j: (0, 0))))], [],
                        [((T, HEAD), f32, BS((tm, HEAD), lambda i, j: (i, 0)))])
        fb = jnp.pad(forget_bias[i], (0, HEAD - H)).reshape(1, HEAD)
        cc, ct = _fgate_fwd(f"fgate_{i}", Pf, fb)
        c_col = ct[:H].reshape(H, T, 1)
        c_row = ct[:H].reshape(H, nb, 1, tb)
        qg, kg = q_norm[i].reshape(1, HEAD), k_norm[i].reshape(1, HEAD)
        qn, kn, vb = _qk_norm(f"qknorm_{i}", P, qg, kg, A)
        y_attn, o32, lse = _attn_fwd(f"attn_{i}", qn, kn, vb, c_col, c_row, tb)
        gain = gmlp_v_norm[i].reshape(G, 1, HEAD)
        bs = gmlp_b_s[i].reshape(G, HEAD, 1)
        y_gmlp = _gmlp_fwd(f"gmlp_{i}", P, gain, gmlp_w_s[i], bs, col_gu, col_gv, Wd)
        ps = pool_scale[i].reshape(1, Wd)
        y_pool = _pool_fwd(f"pool_{i}", P, pool_w[i], ps, col_xp, Wd)
        mix = jnp.concatenate([y_attn, y_gmlp, y_pool], axis=1)
        if i == 0:
            order.done(mix)
            w.update(gathered(gather_forward(0, first[1], "b")))
        (h1,) = _matmul(f"out_{i}", "nn", (T // tm, D // tnd),
                        [((mix, BS((tm, D), lambda i, j: (i, 0))), (w["w_out"], BS((D, tnd), lambda i, j: (0, j))))],
                        [(h, BS((tm, tnd), lambda i, j: (i, j)))],
                        [((T, D), f32, BS((tm, tnd), lambda i, j: (i, j)))],
                        epilogue=lambda accs, ex: (accs[0] + ex[0],), after=order.token)
        xn2 = _rms_fwd(f"rms2_{i}", h1, norm_ffn[i])
        if i == 0:
            order.done(xn2)
            w.update(gathered(gather_forward(0, first[2], "c")))

        def ffn_epi(accs, ex):
            g_, u_ = accs
            return g_, u_, g_ * jax.nn.sigmoid(g_) * u_

        ffo = BS((tm, FS), lambda i, j: (i, j))
        Gt, Ut, act = _matmul(f"ffn1_{i}", "nn", (T // tm, N_CHIPS),
                              [((xn2, BS((tm, D), lambda i, j: (i, 0))), (w["w_gate"], BS((None, D, FS), lambda i, j: (j, 0, 0)))),
                               ((xn2, BS((tm, D), lambda i, j: (i, 0))), (w["w_up"], BS((None, D, FS), lambda i, j: (j, 0, 0))))],
                              [], [((T, FF), bf16, ffo)] * 3, epilogue=ffn_epi, after=order.token)
        (h2,) = _matmul(f"ffn2_{i}", "nn", (T // tm, D // tnd),
                        [((act, BS((tm, FF), lambda i, j: (i, 0))), (w["w_down"], BS((FF, tnd), lambda i, j: (0, j))))],
                        [(h1, BS((tm, tnd), lambda i, j: (i, j)))],
                        [((T, D), f32, BS((tm, tnd), lambda i, j: (i, j)))],
                        epilogue=lambda accs, ex: (accs[0] + ex[0],))
        order.done(h2)
        if i + 1 < L:
            d2d = gather_forward(i + 1, ici)
            ici = gather_start(i + 2, BIG) if i + 2 < L else None
        xn3 = _rms_fwd(f"rms3_{i}", h2, order.follows(norm_ple[i]))

        def ple_epi(accs, ex):
            gate = jax.nn.sigmoid(accs[0])
            return ex[0] + accs[1] * gate, gate, accs[1]

        dso = BS((tm, DS), lambda i, j: (i, j))
        h3, gate, e = _matmul(f"ple_{i}", "nn", (T // tm, N_CHIPS),
                              [((xn3, BS((tm, D), lambda i, j: (i, 0))), (w["w_pg"], BS((D, DS), lambda i, j: (0, j)))),
                               ((pb16[i], BS((tm, PL), lambda i, j: (i, 0))), (w["w_pp"], BS((None, PL, DS), lambda i, j: (j, 0, 0))))],
                              [(h2, dso)], [((T, D), f32, dso), ((T, D), bf16, dso), ((T, D), bf16, dso)], epilogue=ple_epi)
        sv.update(xn1=xn1, P=P, Pf=Pf, fb=fb, c_col=c_col, c_row=c_row, qn=qn, kn=kn, vb=vb, o32=o32, lse=lse,
                  mix=mix, h1=h1, xn2=xn2, Gt=Gt, Ut=Ut, act=act, h2=h2, xn3=xn3, gate=gate, e=e)
        saved.append(sv)
        h = h3
        order.done(h3)
        if i + 1 < L:
            Wf[i + 1] = gathered(d2d)

    dh, loss_tile = _loss_grad("loss", h, loss_target.reshape(T, D))

    small_g = {n: [None] * L for n in SMALL}
    big_out = {}

    def stage_a(u):
        n_u = len(u["names"])
        lands = [lax.empty((N_CHIPS, g.shape[1] // 2, g.shape[2]), f32) for g in u["grads"]]
        u["a"] = order.start(f"rs_a_{u['tag']}", _plan_sibling_halves(n_u), u["grads"] + lands)

    def stage_pair(u):
        n_u = len(u["names"])
        out = order.wait(u["a"])
        pairs = [_pair_sum(f"rs_pair_{u['tag']}_{a}", out[a], out[n_u + a]) for a in range(n_u)]
        u["pb"], u["own"] = [t[0] for t in pairs], [t[1] for t in pairs]

    def stage_b(u):
        lands = [lax.empty(t.shape, bf16) for t in u["pb"]]
        u["b"] = order.start(f"rs_b_{u['tag']}", _plan_chip_scatter(len(u["names"])), u["pb"] + lands)

    def stage_sum(u):
        n_u = len(u["names"])
        out = order.wait(u["b"])
        u["sum"] = [_chip_sum(f"rs_sum_{u['tag']}_{a}", u["own"][a], out[n_u + a]) for a in range(n_u)]

    def stage_c(u):
        u["c"] = order.start(f"rs_c_{u['tag']}", _plan_sibling_join(len(u["names"])), u["sum"])

    def stage_adamw(u):
        for n, r in zip(u["names"], order.wait(u["c"])):
            big_out[n] = _adamw_layer(f"adamw_{n}_{u['layer']}", u["layer"], W[n], M[n], V[n],
                                      r.reshape(W[n].shape[1:]), big_out.get(n))

    dh1 = prev_f = prev_m = None
    for i in reversed(range(L)):
        w, sv = Wf[i], saved[i]
        if dh1 is not None:
            dh, _, dg = _rms_bwd(f"rms1_bw_{i + 1}", dxn1, saved[i + 1]["h0"], order.follows(norm_mix[i + 1]), dh1)
            small_g["norm_mix"][i + 1] = dg.reshape(D)
        de, dz = _ple_bwd_elem(f"ple_bw_{i}", dh, sv["gate"], sv["e"])
        (d_wpp,) = _matmul(f"d_wpp_{i}", "tn", (N_CHIPS, 1),
                           [((pb16[i], BS((T, PL), lambda i, j: (0, 0))), (de, BS((T, DS), lambda i, j: (0, i))))], [],
                           [((N_CHIPS, PL, DS), f32, BS((None, PL, DS), lambda i, j: (i, 0, 0)))])
        (d_wpg,) = _matmul(f"d_wpg_{i}", "tn", (D // tnd, D // tnd),
                           [((sv["xn3"], BS((T, tnd), lambda i, j: (0, i))), (dz, BS((T, tnd), lambda i, j: (0, j))))], [],
                           [((D, D), f32, BS((tnd, tnd), lambda i, j: (i, j)))])
        order.done(dz)
        if prev_m is not None:
            stage_pair(prev_m)
        (dxn3,) = _matmul(f"d_xn3_{i}", "nt", (T // tm, D // tnd),
                          [((dz, BS((tm, D), lambda i, j: (i, 0))), (w["w_pg"], BS((tnd, D), lambda i, j: (j, 0))))], [],
                          [((T, D), f32, BS((tm, tnd), lambda i, j: (i, j)))], after=order.token)
        dh2, dh2b, dg = _rms_bwd(f"rms3_bw_{i}", dxn3, sv["h2"], norm_ple[i], dh)
        small_g["norm_ple"][i] = dg.reshape(D)

        def dffn_epi(accs, ex):
            da = accs[0]
            g_, u_ = ex[0].astype(f32), ex[1].astype(f32)
            sg = jax.nn.sigmoid(g_)
            return da * u_ * (sg * (1.0 + g_ * (1.0 - sg))), da * (g_ * sg)

        ffo = BS((tm, FS), lambda i, j: (i, j))
        dG, dU = _matmul(f"d_act_{i}", "nt", (T // tm, N_CHIPS),
                         [((dh2b, BS((tm, D), lambda i, j: (i, 0))), (w["w_down"], BS((FS, D), lambda i, j: (j, 0))))],
                         [(sv["Gt"], ffo), (sv["Ut"], ffo)], [((T, FF), bf16, ffo)] * 2, epilogue=dffn_epi)
        (d_wd,) = _matmul(f"d_wd_{i}", "tn", (FF // tkf, D // tnd),
                          [((sv["act"], BS((T, tkf), lambda i, j: (0, i))), (dh2b, BS((T, tnd), lambda i, j: (0, j))))], [],
                          [((FF, D), f32, BS((tkf, tnd), lambda i, j: (i, j)))])
        gu_out = BS((None, tnd, FS), lambda i, j: (j, i, 0))
        d_wg, d_wu = _matmul(f"d_wgu_{i}", "tn", (D // tnd, N_CHIPS),
                             [((sv["xn2"], BS((T, tnd), lambda i, j: (0, i))), (dG, BS((T, FS), lambda i, j: (0, j)))),
                              ((sv["xn2"], BS((T, tnd), lambda i, j: (0, i))), (dU, BS((T, FS), lambda i, j: (0, j))))], [],
                             [((N_CHIPS, D, FS), f32, gu_out)] * 2, epilogue=lambda accs, ex: (accs[0], accs[1]))
        order.done(dG)
        unit_f = dict(tag=f"{i}f", layer=i, names=["w_ffn_gate", "w_ffn_up", "w_ffn_down", "w_ple_gate", "w_ple_proj"],
                      grads=[d_wg, d_wu, d_wd.reshape(N_CHIPS, FS, D), d_wpg.reshape(N_CHIPS, DS, D), d_wpp])
        stage_a(unit_f)
        if prev_f is not None:
            stage_sum(prev_f)
            stage_c(prev_f)
        if prev_m is not None:
            stage_b(prev_m)
        tm2 = _tile(T, 256, 16)
        (dxn2,) = _matmul(f"d_xn2_{i}", "nt", (T // tm2, D // tnd),
                          [((dG, BS((tm2, FF), lambda i, j: (i, 0))), (w["w_gate"], BS((N_CHIPS, tnd, FS), lambda i, j: (0, j, 0)))),
                           ((dU, BS((tm2, FF), lambda i, j: (i, 0))), (w["w_up"], BS((N_CHIPS, tnd, FS), lambda i, j: (0, j, 0))))], [],
                          [((T, D), f32, BS((tm2, tnd), lambda i, j: (i, j)))], after=order.token)
        dh1, dh1b, dg = _rms_bwd(f"rms2_bw_{i}", dxn2, sv["h1"], norm_ffn[i], dh2)
        small_g["norm_ffn"][i] = dg.reshape(D)
        (dmix,) = _matmul(f"d_mix_{i}", "nt", (T // tm, D // tnd),
                          [((dh1b, BS((tm, D), lambda i, j: (i, 0))), (w["w_out"], BS((tnd, D), lambda i, j: (j, 0))))], [],
                          [((T, D), f32, BS((tm, tnd), lambda i, j: (i, j)))])
        (d_wout,) = _matmul(f"d_wout_{i}", "tn", (D // tnd, D // tnd),
                            [((sv["mix"], BS((T, tnd), lambda i, j: (0, i))), (dh1b, BS((T, tnd), lambda i, j: (0, j))))], [],
                            [((D, D), f32, BS((tnd, tnd), lambda i, j: (i, j)))])
        order.done(dmix)
        stage_pair(unit_f)
        stage_b(unit_f)
        if prev_f is not None:
            stage_adamw(prev_f)
        qg, kg = q_norm[i].reshape(1, HEAD), k_norm[i].reshape(1, HEAD)
        dq, dk, dv, dc_row, dqg, dkg = _attn_bwd(f"attn_bw_{i}", sv["qn"], sv["kn"], sv["vb"], sv["o32"], dmix,
                                                 sv["lse"], sv["c_col"], sv["c_row"], sv["P"], order.follows(qg), kg, tb)
        small_g["q_norm"][i] = dqg.reshape(HEAD)
        small_g["k_norm"][i] = dkg.reshape(HEAD)
        dct = jnp.pad(dc_row.reshape(H, T), ((0, HEAD - H), (0, 0)))
        dPf, dfb = _fgate_bwd(f"fgate_bw_{i}", dct, sv["Pf"], sv["fb"])
        small_g["forget_bias"][i] = dfb[0, :H]
        gain = gmlp_v_norm[i].reshape(G, 1, HEAD)
        bs = gmlp_b_s[i].reshape(G, HEAD, 1)
        dgu, dgv, dws, dbs, dgain = _gmlp_bwd(f"gmlp_bw_{i}", sv["P"], dmix, gain, gmlp_w_s[i], bs, col_gu, col_gv,
                                              col_dg, Wd)
        small_g["gmlp_w_s"][i] = dws
        small_g["gmlp_b_s"][i] = dbs.reshape(G, HEAD)
        small_g["gmlp_v_norm"][i] = dgain.reshape(G, HEAD)
        ps = pool_scale[i].reshape(1, Wd)
        dxp, dpw, dps = _pool_bwd(f"pool_bw_{i}", sv["P"], dmix, pool_w[i], ps, col_xp, col_dp, Wd)
        small_g["pool_w"][i] = dpw
        small_g["pool_scale"][i] = dps.reshape(Wd)
        dP = jnp.concatenate([dq, dk, dv, dgu, dgv, dxp], axis=1)
        (d_wmain,) = _matmul(f"d_wmain_{i}", "tn", (D // tnd, NM // tn),
                             [((sv["xn1"], BS((T, tnd), lambda i, j: (0, i))), (dP, BS((T, tn), lambda i, j: (0, j))))], [],
                             [((D, NM), f32, BS((tnd, tn), lambda i, j: (i, j)))])
        (d_wf,) = _matmul(f"d_wf_{i}", "tn", (D // tnd, 1),
                          [((sv["xn1"], BS((T, tnd), lambda i, j: (0, i))), (dPf, BS((T, HEAD), lambda i, j: (0, 0))))], [],
                          [((D, HEAD), f32, BS((tnd, HEAD), lambda i, j: (i, 0)))])
        def win_cols(lo, hi):
            out = []
            for src, s0, e0, off in ((d_wmain, 0, 3 * A, 0), (d_wf, 3 * A, 3 * A + H, 3 * A), (d_wmain, 3 * A + H, DP, H)):
                s, e = max(lo, s0), min(hi, e0)
                if s < e:
                    out.append(src[:, s - off:e - off])
            return out

        d_win4 = jnp.stack([jnp.concatenate(win_cols(j * DP4, (j + 1) * DP4), axis=1) for j in range(N_CHIPS)])
        order.done(dP)
        unit_m = dict(tag=f"{i}m", layer=i, names=["w_in", "w_out"], grads=[d_win4, d_wout.reshape(N_CHIPS, DS, D)])
        stage_a(unit_m)
        if prev_m is not None:
            stage_sum(prev_m)
            stage_c(prev_m)
        (dxn1,) = _matmul(f"d_xn1_{i}", "nt", (T // tm2, D // tnd),
                          [((dP, BS((tm2, NM), lambda i, j: (i, 0))), (w["w_main"], BS((tnd, NM), lambda i, j: (j, 0)))),
                           ((dPf, BS((tm2, HEAD), lambda i, j: (i, 0))), (w["w_f"], BS((tnd, HEAD), lambda i, j: (j, 0))))], [],
                          [((T, D), f32, BS((tm2, tnd), lambda i, j: (i, j)))], after=order.token)
        order.done(dxn1)
        if prev_m is not None:
            stage_adamw(prev_m)
        prev_f, prev_m = unit_f, unit_m

    dh, _, dg = _rms_bwd("rms1_bw_0", dxn1, saved[0]["h0"], order.follows(norm_mix[0]), dh1)
    small_g["norm_mix"][0] = dg.reshape(D)
    order.done(dh)
    stage_pair(prev_m)
    stage_b(prev_m)
    stage_sum(prev_f)
    stage_c(prev_f)

    small_full = {n: jnp.stack(small_g[n]) for n in SMALL}
    n_small = sum(int(W[n].size) for n in SMALL) + 1
    rows8 = -(-n_small // (128 * 64)) * 8
    packed = order.follows(_pack_rows([small_full[n] for n in SMALL] + [loss_tile[0, :1]], N_DEV * rows8))
    summed = _all_reduce_small("allreduce_small", packed.reshape(N_DEV, rows8, 128)).reshape(-1)
    order.done(summed)
    stage_adamw(prev_f)
    stage_sum(prev_m)
    stage_c(prev_m)
    stage_adamw(prev_m)
    grads, off = {}, 0
    for n in SMALL:
        grads[n] = summed[off:off + W[n].size].reshape(W[n].shape)
        off += W[n].size
    loss = summed[off]

    delta, new_m, new_v = {}, {}, {}
    wp, mp, vp = (_pack_rows([t[n] for n in SMALL], N_DEV * rows8) for t in (W, M, V))
    ds_, ms_, vs_ = _adamw("adamw_small", wp, summed.reshape(-1, 128), mp, vp)
    off = 0
    for n in SMALL:
        sz = W[n].size
        delta[n], new_m[n], new_v[n] = (t.reshape(-1)[off:off + sz].reshape(W[n].shape) for t in (ds_, ms_, vs_))
        off += sz
    for n in BIG:
        grads[n], delta[n], new_m[n], new_v[n] = big_out[n]

    return (loss, dh.reshape(1, T, D), *[grads[n] for n in WEIGHTS], *[delta[n] for n in WEIGHTS],
            *[new_m[n] for n in WEIGHTS], *[new_v[n] for n in WEIGHTS])
```

```python
import jax
import jax.numpy as jnp
from jax import lax
from jax.experimental import pallas as pl
from jax.experimental.pallas import tpu as pltpu

f32, bf16 = jnp.float32, jnp.bfloat16
S = jax.ShapeDtypeStruct
BS = pl.BlockSpec
ANY = pl.BlockSpec(memory_space=pl.ANY)
MESH = pl.DeviceIdType.MESH

EPS = 1e-6
HEAD = 128
POOL_WINDOWS = (2, 4, 8, 16)
NEG = -1e30
N_CHIPS = 4
N_DEV = 8
VMEM_LIMIT = 56 * 1024 * 1024

ADAM_LR, ADAM_B1, ADAM_B2, ADAM_EPS, ADAM_WD, ADAM_STEP = 0.001, 0.9, 0.999, 1e-08, 0.01, 10

BIG = ("w_in", "w_out", "w_ffn_gate", "w_ffn_up", "w_ffn_down", "w_ple_gate", "w_ple_proj")
SMALL = ("norm_mix", "q_norm", "k_norm", "forget_bias", "gmlp_v_norm", "gmlp_w_s", "gmlp_b_s", "pool_w",
         "pool_scale", "norm_ffn", "norm_ple")
WEIGHTS = ("norm_mix", "w_in", "q_norm", "k_norm", "forget_bias", "gmlp_v_norm", "gmlp_w_s", "gmlp_b_s", "pool_w",
           "pool_scale", "w_out", "norm_ffn", "w_ffn_gate", "w_ffn_up", "w_ffn_down", "norm_ple", "w_ple_gate",
           "w_ple_proj")


def _tile(n, target, mult):
    best = None
    for t in range(mult, min(n, target) + 1, mult):
        if n % t == 0:
            best = t
    return best if best is not None else n


def _params(**kw):
    return pltpu.CompilerParams(vmem_limit_bytes=VMEM_LIMIT, **kw)


def _dot(a, b, kind):
    dims = {"nn": (((1,), (0,)), ((), ())), "nt": (((1,), (1,)), ((), ())), "tn": (((0,), (0,)), ((), ()))}[kind]
    return lax.dot_general(a.astype(bf16), b.astype(bf16), dims, preferred_element_type=f32)


def _heads_per_program(n_heads):
    return 2 if n_heads % 2 == 0 else 1


def _my_place():
    x, y, c = lax.axis_index("x"), lax.axis_index("y"), lax.axis_index("c")
    return x, y, c, 2 * x + y


def _matmul(name, kind, grid, pairs, extras, outs, epilogue=None, after=None):
    n_p, n_e = len(pairs), len(extras)
    tokens = [] if after is None else [(after, BS((8, 128), lambda *_: (0, 0)))]

    def body(*refs):
        a_refs, b_refs = refs[:n_p], refs[n_p:2 * n_p]
        e_refs = refs[2 * n_p:2 * n_p + n_e]
        o_refs = refs[2 * n_p + n_e + len(tokens):]
        accs = []
        for a_ref, b_ref in zip(a_refs, b_refs):
            if len(b_ref.shape) == 3:
                w = b_ref.shape[2]
                acc = None
                for s in range(b_ref.shape[0]):
                    d = _dot(a_ref[:, s * w:(s + 1) * w], b_ref[s], kind)
                    acc = d if acc is None else acc + d
            else:
                acc = _dot(a_ref[...], b_ref[...], kind)
            accs.append(acc)
        if epilogue is None:
            res = accs[0]
            for t in accs[1:]:
                res = res + t
            res = (res,)
        else:
            res = epilogue(accs, [e[...] for e in e_refs])
        for o_ref, o in zip(o_refs, res):
            o_ref[...] = o.astype(o_ref.dtype)

    in_arrays = [p[0][0] for p in pairs] + [p[1][0] for p in pairs] + [e[0] for e in extras + tokens]
    in_specs = [p[0][1] for p in pairs] + [p[1][1] for p in pairs] + [e[1] for e in extras + tokens]
    res = pl.pallas_call(
        body, name=name, grid=grid, in_specs=in_specs,
        out_specs=[o[2] for o in outs], out_shape=[S(o[0], o[1]) for o in outs],
        compiler_params=_params(),
    )(*in_arrays)
    return res


def _rms_fwd(name, x, g):
    T, D = x.shape
    tr = _tile(T, 256, 8)

    def body(x_ref, g_ref, o_ref):
        xv = x_ref[...]
        r = lax.rsqrt(jnp.mean(xv * xv, axis=-1, keepdims=True) + EPS)
        o_ref[...] = (xv * r * g_ref[...]).astype(o_ref.dtype)

    return pl.pallas_call(
        body, name=name, grid=(T // tr,),
        in_specs=[BS((tr, D), lambda i: (i, 0)), BS((1, D), lambda i: (0, 0))],
        out_specs=BS((tr, D), lambda i: (i, 0)), out_shape=S((T, D), bf16),
    )(x, g.reshape(1, D))


def _rms_bwd(name, dxn, x, g, dres):
    T, D = x.shape
    tr = _tile(T, 256, 8)

    def body(dxn_ref, x_ref, g_ref, dres_ref, dx_ref, dxb_ref, dg_ref):
        i = pl.program_id(0)
        xv = x_ref[...]
        r = lax.rsqrt(jnp.mean(xv * xv, axis=-1, keepdims=True) + EPS)
        xh = xv * r
        dxn_v = dxn_ref[...]
        dxh = dxn_v * g_ref[...]
        dx = dres_ref[...] + r * (dxh - xh * jnp.mean(dxh * xh, axis=-1, keepdims=True))
        dx_ref[...] = dx
        dxb_ref[...] = dx.astype(bf16)
        part = jnp.sum(dxn_v * xh, axis=0, keepdims=True)

        @pl.when(i == 0)
        def _():
            dg_ref[...] = part

        @pl.when(i > 0)
        def _():
            dg_ref[...] += part

    row = BS((tr, D), lambda i: (i, 0))
    vec = BS((1, D), lambda i: (0, 0))
    return pl.pallas_call(
        body, name=name, grid=(T // tr,),
        in_specs=[row, row, vec, row], out_specs=[row, row, vec],
        out_shape=[S((T, D), f32), S((T, D), bf16), S((1, D), f32)],
    )(dxn, x, g.reshape(1, D), dres)


def _loss_grad(name, y, tgt):
    T, D = y.shape
    tr = _tile(T, 256, 8)

    def body(y_ref, t_ref, dy_ref, l_ref):
        i = pl.program_id(0)
        e = y_ref[...] - t_ref[...]
        dy_ref[...] = e * (1.0 / D)
        part = 0.5 * jnp.sum(jnp.mean(e * e, axis=-1, keepdims=True), axis=0, keepdims=True)

        @pl.when(i == 0)
        def _():
            l_ref[...] = jnp.zeros_like(l_ref)

        l_ref[...] += jnp.broadcast_to(part, l_ref.shape)

    row = BS((tr, D), lambda i: (i, 0))
    return pl.pallas_call(
        body, name=name, grid=(T // tr,), in_specs=[row, row],
        out_specs=[row, BS((8, 128), lambda i: (0, 0))],
        out_shape=[S((T, D), f32), S((8, 128), f32)],
    )(y, tgt)


def _ple_bwd_elem(name, dh, gate, e):
    T, D = dh.shape
    tr = _tile(T, 256, 16)

    def body(dh_ref, g_ref, e_ref, de_ref, dz_ref):
        d = dh_ref[...]
        g = g_ref[...].astype(f32)
        de_ref[...] = (d * g).astype(bf16)
        dz_ref[...] = (d * e_ref[...].astype(f32) * g * (1.0 - g)).astype(bf16)

    row = BS((tr, D), lambda i: (i, 0))
    return pl.pallas_call(
        body, name=name, grid=(T // tr,), in_specs=[row, row, row], out_specs=[row, row],
        out_shape=[S((T, D), bf16), S((T, D), bf16)],
    )(dh, gate, e)


def _cast_layer(name, w_all, layer):
    _, R, C = w_all.shape
    lanes = -(-C // 128) * 128
    tr = _tile(R, max(16, (1024 * 1024) // lanes // 16 * 16), 16)

    def body(w_ref, o_ref):
        o_ref[...] = w_ref[...].astype(bf16)

    return pl.pallas_call(
        body, name=name, grid=(R // tr,), in_specs=[BS((None, tr, C), lambda r: (layer, r, 0))],
        out_specs=BS((tr, C), lambda r: (r, 0)), out_shape=S((R, C), bf16),
    )(w_all)


def _gelu_and_grad(x):
    k0, k1 = 0.7978845608028654, 0.044715
    th = jnp.tanh(k0 * (x + k1 * x * x * x))
    val = 0.5 * x * (1.0 + th)
    grad = 0.5 * (1.0 + th) + 0.5 * x * (1.0 - th * th) * (k0 * (1.0 + 3.0 * k1 * x * x))
    return val, grad


def _fgate_fwd(name, pf, fb):
    T = pf.shape[0]

    def body(pf_ref, fb_ref, c_ref, ct_ref):
        xv = jax.nn.log_sigmoid(pf_ref[...] + fb_ref[...])
        row = lax.broadcasted_iota(jnp.int32, xv.shape, 0)
        s = 1
        while s < T:
            xv = xv + jnp.where(row >= s, pltpu.roll(xv, s, 0), 0.0)
            s *= 2
        c_ref[...] = xv
        ct_ref[...] = xv.T

    return pl.pallas_call(body, name=name, out_shape=[S((T, HEAD), f32), S((HEAD, T), f32)])(pf, fb)


def _fgate_bwd(name, dct, pf, fb):
    T = pf.shape[0]

    def body(dct_ref, pf_ref, fb_ref, dpf_ref, dfb_ref):
        xv = dct_ref[...].T
        row = lax.broadcasted_iota(jnp.int32, xv.shape, 0)
        s = 1
        while s < T:
            xv = xv + jnp.where(row + s < T, pltpu.roll(xv, T - s, 0), 0.0)
            s *= 2
        df = xv * jax.nn.sigmoid(-(pf_ref[...] + fb_ref[...]))
        dpf_ref[...] = df.astype(bf16)
        dfb_ref[...] = jnp.sum(df, axis=0, keepdims=True)

    return pl.pallas_call(body, name=name, out_shape=[S((T, HEAD), bf16), S((1, HEAD), f32)])(dct, pf, fb)


def _qk_norm(name, P, qg, kg, A):
    T = P.shape[0]
    tr = _tile(T, 256, 16)
    n_heads = A // HEAD

    def body(q_ref, k_ref, v_ref, qg_ref, kg_ref, qn_ref, kn_ref, vb_ref):
        for h in range(n_heads):
            sl = slice(h * HEAD, (h + 1) * HEAD)
            for src, g_ref, dst in ((q_ref, qg_ref, qn_ref), (k_ref, kg_ref, kn_ref)):
                xv = src[:, sl]
                r = lax.rsqrt(jnp.mean(xv * xv, axis=-1, keepdims=True) + EPS)
                dst[:, sl] = (xv * r * g_ref[...]).astype(bf16)
        vb_ref[...] = v_ref[...].astype(bf16)

    vec = BS((1, HEAD), lambda i: (0, 0))
    out = BS((tr, A), lambda i: (i, 0))
    return pl.pallas_call(
        body, name=name, grid=(T // tr,),
        in_specs=[BS((tr, A), lambda i: (i, 0)), BS((tr, A), lambda i: (i, 1)), BS((tr, A), lambda i: (i, 2)), vec, vec],
        out_specs=[out, out, out], out_shape=[S((T, A), bf16)] * 3,
    )(P, P, P, qg, kg)


def _attn_fwd(name, qn, kn, vb, c_col, c_row, tb):
    T, A = qn.shape
    H = A // HEAD
    nb = T // tb
    scale = HEAD ** -0.5
    hp = _heads_per_program(H)
    wide = hp * HEAD

    def body(q_ref, k_ref, v_ref, cq_ref, ck_ref, o_ref, o32_ref, lse_ref):
        i = pl.program_id(1)
        below = lax.broadcasted_iota(jnp.int32, (tb, tb), 0) >= lax.broadcasted_iota(jnp.int32, (tb, tb), 1)

        def block(j, carry, diagonal):
            koff = pl.multiple_of(j * tb, tb)
            out = []
            for hh in range(hp):
                m, l, acc = carry[hh]
                sl = slice(hh * HEAD, (hh + 1) * HEAD)
                k = k_ref[pl.ds(koff, tb), sl]
                v = v_ref[pl.ds(koff, tb), sl]
                s = _dot(q_ref[:, sl], k, "nt") * scale + (cq_ref[hh] - ck_ref[hh, j])
                if diagonal:
                    s = jnp.where(below, s, NEG)
                m_new = jnp.maximum(m, jnp.max(s, axis=-1, keepdims=True))
                alpha = jnp.exp(m - m_new)
                p = jnp.exp(s - m_new)
                l = l * alpha + jnp.sum(p, axis=-1, keepdims=True)
                acc = acc * alpha + _dot(p, v, "nn")
                out.append((m_new, l, acc))
            return tuple(out)

        init = tuple((jnp.full((tb, 1), NEG, f32), jnp.zeros((tb, 1), f32), jnp.zeros((tb, HEAD), f32))
                     for _ in range(hp))
        carry = lax.fori_loop(0, i, lambda j, c: block(j, c, False), init)
        carry = block(i, carry, True)
        for hh in range(hp):
            m, l, acc = carry[hh]
            sl = slice(hh * HEAD, (hh + 1) * HEAD)
            o = acc / l
            o_ref[:, sl] = o.astype(bf16)
            o32_ref[:, sl] = o
            lse_ref[hh] = m + jnp.log(l)

    return pl.pallas_call(
        body, name=name, grid=(H // hp, nb),
        in_specs=[BS((tb, wide), lambda h, i: (i, h)), BS((T, wide), lambda h, i: (0, h)),
                  BS((T, wide), lambda h, i: (0, h)), BS((hp, tb, 1), lambda h, i: (h, i, 0)),
                  BS((hp, nb, 1, tb), lambda h, i: (h, 0, 0, 0))],
        out_specs=[BS((tb, wide), lambda h, i: (i, h)), BS((tb, wide), lambda h, i: (i, h)),
                   BS((hp, tb, 1), lambda h, i: (h, i, 0))],
        out_shape=[S((T, A), bf16), S((T, A), f32), S((H, T, 1), f32)],
    )(qn, kn, vb, c_col, c_row)


def _attn_bwd(name, qn, kn, vb, o, dmix, lse, c_col, c_row, P, qg, kg, tb):
    T, A = qn.shape
    H = A // HEAD
    nb = T // tb
    scale = HEAD ** -0.5
    hp = _heads_per_program(H)
    wide = hp * HEAD

    def body(q_ref, k_ref, v_ref, o_ref, do_ref, lse_ref, cq_ref, ck_ref, qraw_ref, kraw_ref, qg_ref, kg_ref,
             dq_out, dk_out, dv_out, dc_out, dqg_out, dkg_out, dq_acc, dk_acc, delta_s):
        h = pl.program_id(0)
        dq_acc[...] = jnp.zeros_like(dq_acc)
        below = lax.broadcasted_iota(jnp.int32, (tb, tb), 0) >= lax.broadcasted_iota(jnp.int32, (tb, tb), 1)
        for hh in range(hp):
            sl = slice(hh * HEAD, (hh + 1) * HEAD)
            delta_s[hh] = jnp.sum(do_ref[:, sl].astype(bf16).astype(f32) * o_ref[:, sl], axis=-1, keepdims=True)

        def kblock(j, _):
            koff = pl.multiple_of(j * tb, tb)

            def qblock(i, carry, diagonal):
                qoff = pl.multiple_of(i * tb, tb)
                out = []
                for hh in range(hp):
                    dk, dv, dc = carry[hh]
                    sl = slice(hh * HEAD, (hh + 1) * HEAD)
                    k = k_ref[pl.ds(koff, tb), sl]
                    v = v_ref[pl.ds(koff, tb), sl]
                    q = q_ref[pl.ds(qoff, tb), sl]
                    do = do_ref[pl.ds(qoff, tb), sl].astype(bf16)
                    s = _dot(q, k, "nt") * scale + (cq_ref[hh, pl.ds(qoff, tb), :] - ck_ref[hh, j])
                    if diagonal:
                        s = jnp.where(below, s, NEG)
                    p = jnp.exp(s - lse_ref[hh, pl.ds(qoff, tb), :])
                    dv = dv + _dot(p, do, "tn")
                    dp = _dot(do, v, "nt")
                    ds = p * (dp - delta_s[hh, pl.ds(qoff, tb), :])
                    dc = dc - jnp.sum(ds, axis=0, keepdims=True)
                    dsb = (ds * scale).astype(bf16)
                    dk = dk + _dot(dsb, q, "tn")
                    dq_acc[pl.ds(qoff, tb), sl] += _dot(dsb, k, "nn")
                    out.append((dk, dv, dc))
                return tuple(out)

            init = tuple((jnp.zeros((tb, HEAD), f32), jnp.zeros((tb, HEAD), f32), jnp.zeros((1, tb), f32))
                         for _ in range(hp))
            carry = qblock(j, init, True)
            carry = lax.fori_loop(j + 1, nb, lambda i, c: qblock(i, c, False), carry)
            for hh in range(hp):
                dk, dv, dc = carry[hh]
                sl = slice(hh * HEAD, (hh + 1) * HEAD)
                dk_acc[pl.ds(koff, tb), sl] = dk
                dv_out[pl.ds(koff, tb), sl] = dv.astype(bf16)
                dc_out[hh, j] = dc
            return 0

        lax.fori_loop(0, nb, kblock, 0)

        for raw_ref, g_ref, acc_ref, d_out, dg_out in ((qraw_ref, qg_ref, dq_acc, dq_out, dqg_out),
                                                       (kraw_ref, kg_ref, dk_acc, dk_out, dkg_out)):
            part = jnp.zeros((1, HEAD), f32)
            for hh in range(hp):
                sl = slice(hh * HEAD, (hh + 1) * HEAD)
                xv = raw_ref[:, sl]
                r = lax.rsqrt(jnp.mean(xv * xv, axis=-1, keepdims=True) + EPS)
                xh = xv * r
                dn = acc_ref[:, sl]
                dxh = dn * g_ref[...]
                d_out[:, sl] = (r * (dxh - xh * jnp.mean(dxh * xh, axis=-1, keepdims=True))).astype(bf16)
                part = part + jnp.sum(dn * xh, axis=0, keepdims=True)

            @pl.when(h == 0)
            def _():
                dg_out[...] = part

            @pl.when(h > 0)
            def _():
                dg_out[...] += part

    heads = lambda off: BS((T, wide), lambda h: (0, off + h))
    col = BS((hp, T, 1), lambda h: (h, 0, 0))
    row = BS((hp, nb, 1, tb), lambda h: (h, 0, 0, 0))
    vec = BS((1, HEAD), lambda h: (0, 0))
    return pl.pallas_call(
        body, name=name, grid=(H // hp,),
        in_specs=[heads(0), heads(0), heads(0), heads(0), heads(0), col, col, row, heads(0), heads(H // hp), vec, vec],
        out_specs=[heads(0), heads(0), heads(0), row, vec, vec],
        out_shape=[S((T, A), bf16)] * 3 + [S((H, nb, 1, tb), f32), S((1, HEAD), f32), S((1, HEAD), f32)],
        scratch_shapes=[pltpu.VMEM((T, wide), f32), pltpu.VMEM((T, wide), f32), pltpu.VMEM((hp, T, 1), f32)],
        compiler_params=_params(),
    )(qn, kn, vb, o, dmix, lse, c_col, c_row, P, P, qg, kg)


def _gmlp_fwd(name, P, gain, ws, b, col_u, col_v, Wd):
    T = P.shape[0]
    G = Wd // HEAD
    tr = _tile(T, 512, HEAD)

    def body(u_ref, v_ref, gain_ref, ws_ref, b_ref, y_ref):
        tril = lax.broadcasted_iota(jnp.int32, (HEAD, HEAD), 0) >= lax.broadcasted_iota(jnp.int32, (HEAD, HEAD), 1)
        wm = jnp.where(tril, ws_ref[...], 0.0).astype(bf16)
        for n in range(tr // HEAD):
            rows = slice(n * HEAD, (n + 1) * HEAD)
            u = jax.nn.gelu(u_ref[rows, :])
            a = jax.nn.gelu(v_ref[rows, :])
            r = lax.rsqrt(jnp.mean(a * a, axis=-1, keepdims=True) + EPS)
            vn = a * r * gain_ref[...]
            mixed = _dot(wm, vn, "nn") + b_ref[...]
            y_ref[rows, :] = (u * mixed).astype(bf16)

    return pl.pallas_call(
        body, name=name, grid=(G, T // tr),
        in_specs=[BS((tr, HEAD), lambda g, i: (i, col_u + g)), BS((tr, HEAD), lambda g, i: (i, col_v + g)),
                  BS((None, 1, HEAD), lambda g, i: (g, 0, 0)), BS((None, HEAD, HEAD), lambda g, i: (g, 0, 0)),
                  BS((None, HEAD, 1), lambda g, i: (g, 0, 0))],
        out_specs=BS((tr, HEAD), lambda g, i: (i, g)), out_shape=S((T, Wd), bf16),
    )(P, P, gain, ws, b)


def _gmlp_bwd(name, P, dmix, gain, ws, b, col_u, col_v, col_dy, Wd):
    T = P.shape[0]
    G = Wd // HEAD
    tr = _tile(T, 512, HEAD)

    def body(u_ref, v_ref, dy_ref, gain_ref, ws_ref, b_ref, du_ref, dv_ref, dws_ref, db_ref, dgain_ref):
        i = pl.program_id(1)
        tril = lax.broadcasted_iota(jnp.int32, (HEAD, HEAD), 0) >= lax.broadcasted_iota(jnp.int32, (HEAD, HEAD), 1)
        wm = jnp.where(tril, ws_ref[...], 0.0).astype(bf16)
        gain_v = gain_ref[...]
        dw = jnp.zeros((HEAD, HEAD), f32)
        db = jnp.zeros((HEAD, 1), f32)
        dgain = jnp.zeros((1, HEAD), f32)
        for n in range(tr // HEAD):
            rows = slice(n * HEAD, (n + 1) * HEAD)
            u, du_dx = _gelu_and_grad(u_ref[rows, :])
            a, da_dx = _gelu_and_grad(v_ref[rows, :])
            dy = dy_ref[rows, :]
            r = lax.rsqrt(jnp.mean(a * a, axis=-1, keepdims=True) + EPS)
            ah = a * r
            vnb = (ah * gain_v).astype(bf16)
            mixed = _dot(wm, vnb, "nn") + b_ref[...]
            dm = dy * u
            dmb = dm.astype(bf16)
            du_ref[rows, :] = (dy * mixed * du_dx).astype(bf16)
            db = db + jnp.sum(dm, axis=1, keepdims=True)
            dw = dw + _dot(dmb, vnb, "nt")
            dvn = _dot(wm, dmb, "tn")
            dgain = dgain + jnp.sum(dvn * ah, axis=0, keepdims=True)
            dah = dvn * gain_v
            da = r * (dah - ah * jnp.mean(dah * ah, axis=-1, keepdims=True))
            dv_ref[rows, :] = (da * da_dx).astype(bf16)
        dw = jnp.where(tril, dw, 0.0)

        @pl.when(i == 0)
        def _():
            dws_ref[...] = dw
            db_ref[...] = db
            dgain_ref[...] = dgain

        @pl.when(i > 0)
        def _():
            dws_ref[...] += dw
            db_ref[...] += db
            dgain_ref[...] += dgain

    out = BS((tr, HEAD), lambda g, i: (i, g))
    return pl.pallas_call(
        body, name=name, grid=(G, T // tr),
        in_specs=[BS((tr, HEAD), lambda g, i: (i, col_u + g)), BS((tr, HEAD), lambda g, i: (i, col_v + g)),
                  BS((tr, HEAD), lambda g, i: (i, col_dy + g)),
                  BS((None, 1, HEAD), lambda g, i: (g, 0, 0)), BS((None, HEAD, HEAD), lambda g, i: (g, 0, 0)),
                  BS((None, HEAD, 1), lambda g, i: (g, 0, 0))],
        out_specs=[out, out, BS((None, HEAD, HEAD), lambda g, i: (g, 0, 0)), BS((None, HEAD, 1), lambda g, i: (g, 0, 0)),
                   BS((None, 1, HEAD), lambda g, i: (g, 0, 0))],
        out_shape=[S((T, Wd), bf16), S((T, Wd), bf16), S((G, HEAD, HEAD), f32), S((G, HEAD, 1), f32),
                   S((G, 1, HEAD), f32)],
    )(P, P, dmix, gain, ws, b)


def _trailing_window(xv, w, row):
    k = 1
    while k < w:
        xv = xv + jnp.where(row >= k, pltpu.roll(xv, k, 0), 0.0)
        k *= 2
    return xv


def _leading_window(xv, w, row, T):
    k = 1
    while k < w:
        xv = xv + jnp.where(row + k < T, pltpu.roll(xv, T - k, 0), 0.0)
        k *= 2
    return xv


def _pool_fwd(name, P, pw, ps, col_x, Wd):
    T = P.shape[0]
    Gp = Wd // HEAD

    def body(x_ref, pw_ref, ps_ref, y_ref):
        row = lax.broadcasted_iota(jnp.int32, (T, HEAD), 0)
        for g in range(Gp):
            w = POOL_WINDOWS[g]
            sl = slice(g * HEAD, (g + 1) * HEAD)
            xv = x_ref[:, sl]
            cnt = jnp.minimum(row + 1, w).astype(f32)
            d = _trailing_window(xv, w, row) / cnt - xv
            y_ref[:, sl] = (_dot(d, pw_ref[g], "nn") * ps_ref[:, sl]).astype(bf16)

    return pl.pallas_call(
        body, name=name, grid=(1,),
        in_specs=[BS((T, Wd), lambda i: (0, col_x)), BS((Gp, HEAD, HEAD), lambda i: (0, 0, 0)), BS((1, Wd), lambda i: (0, 0))],
        out_specs=BS((T, Wd), lambda i: (0, 0)), out_shape=S((T, Wd), bf16), compiler_params=_params(),
    )(P, pw, ps)


def _pool_bwd(name, P, dmix, pw, ps, col_x, col_dy, Wd):
    T = P.shape[0]
    Gp = Wd // HEAD

    def body(x_ref, dy_ref, pw_ref, ps_ref, dx_ref, dpw_ref, dps_ref):
        row = lax.broadcasted_iota(jnp.int32, (T, HEAD), 0)
        for g in range(Gp):
            w = POOL_WINDOWS[g]
            sl = slice(g * HEAD, (g + 1) * HEAD)
            xv = x_ref[:, sl]
            cnt = jnp.minimum(row + 1, w).astype(f32)
            d = (_trailing_window(xv, w, row) / cnt - xv).astype(bf16)
            pwb = pw_ref[g].astype(bf16)
            z = _dot(d, pwb, "nn")
            dy = dy_ref[:, sl]
            dps_ref[:, sl] = jnp.sum(dy * z, axis=0, keepdims=True)
            dzb = (dy * ps_ref[:, sl]).astype(bf16)
            dpw_ref[g] = _dot(d, dzb, "tn")
            dd = _dot(dzb, pwb, "nt")
            dx_ref[:, sl] = (_leading_window(dd / cnt, w, row, T) - dd).astype(bf16)

    return pl.pallas_call(
        body, name=name, grid=(1,),
        in_specs=[BS((T, Wd), lambda i: (0, col_x)), BS((T, Wd), lambda i: (0, col_dy)),
                  BS((Gp, HEAD, HEAD), lambda i: (0, 0, 0)), BS((1, Wd), lambda i: (0, 0))],
        out_specs=[BS((T, Wd), lambda i: (0, 0)), BS((Gp, HEAD, HEAD), lambda i: (0, 0, 0)), BS((1, Wd), lambda i: (0, 0))],
        out_shape=[S((T, Wd), bf16), S((Gp, HEAD, HEAD), f32), S((1, Wd), f32)], compiler_params=_params(),
    )(P, dmix, pw, ps)


def _adamw(name, w, g, m, v):
    R, C = w.shape
    lanes = -(-C // 128) * 128
    tr = _tile(R, max(8, (512 * 1024) // lanes // 8 * 8), 8)
    c1 = 1.0 - ADAM_B1 ** ADAM_STEP
    c2 = 1.0 - ADAM_B2 ** ADAM_STEP

    def body(w_ref, g_ref, m_ref, v_ref, d_ref, nm_ref, nv_ref):
        gv = g_ref[...]
        nm = ADAM_B1 * m_ref[...] + (1.0 - ADAM_B1) * gv
        nv = ADAM_B2 * v_ref[...] + (1.0 - ADAM_B2) * (gv * gv)
        d_ref[...] = -ADAM_LR * ((nm / c1) / (jnp.sqrt(nv / c2) + ADAM_EPS) + ADAM_WD * w_ref[...])
        nm_ref[...] = nm
        nv_ref[...] = nv

    blk = BS((tr, C), lambda i: (i, 0))
    return pl.pallas_call(
        body, name=name, grid=(R // tr,), in_specs=[blk] * 4, out_specs=[blk] * 3, out_shape=[S((R, C), f32)] * 3,
    )(w, g, m, v)


def _adamw_layer(name, layer, w_all, m_all, v_all, g, prev):
    L, R, C = w_all.shape
    lanes = -(-C // 128) * 128
    tr = _tile(R, max(8, (512 * 1024) // lanes // 8 * 8), 8)
    c1 = 1.0 - ADAM_B1 ** ADAM_STEP
    c2 = 1.0 - ADAM_B2 ** ADAM_STEP
    n_prev = 0 if prev is None else 4

    def body(w_ref, m_ref, v_ref, g_ref, *rest):
        go_ref, d_ref, nm_ref, nv_ref = rest[n_prev:]
        gv = g_ref[...]
        nm = ADAM_B1 * m_ref[...] + (1.0 - ADAM_B1) * gv
        nv = ADAM_B2 * v_ref[...] + (1.0 - ADAM_B2) * (gv * gv)
        d_ref[...] = -ADAM_LR * ((nm / c1) / (jnp.sqrt(nv / c2) + ADAM_EPS) + ADAM_WD * w_ref[...])
        nm_ref[...] = nm
        nv_ref[...] = nv
        go_ref[...] = gv

    slab = BS((None, tr, C), lambda r: (layer, r, 0))
    return pl.pallas_call(
        body, name=name, grid=(R // tr,),
        in_specs=[slab, slab, slab, BS((tr, C), lambda r: (r, 0))] + [ANY] * n_prev,
        out_specs=[slab] * 4, out_shape=[S((L, R, C), f32)] * 4,
        input_output_aliases={4 + k: k for k in range(n_prev)},
    )(w_all, m_all, v_all, g, *(prev or ()))


def _chip_of(k):
    return k // 2, k % 2


def _remote(src, dst, send_sems, recv_sems, idx, dev):
    return pltpu.make_async_remote_copy(src_ref=src, dst_ref=dst, send_sem=send_sems.at[idx], recv_sem=recv_sems.at[idx],
                                        device_id=dev, device_id_type=MESH)


def _plan_gather_chips(n):
    def plan(refs, ss, rs, base):
        ins, lands = refs[:n], refs[n:]
        x, y, c, j0 = _my_place()
        sib = (x, y, 1 - c)
        sends, recvs = [], []
        for a in range(n):
            half = ins[a].shape[0] // 2
            lo = c * half
            sends.append(_remote(ins[a], lands[a].at[j0], ss, rs, base + 4 * a + 3, sib))
            recvs.append(_remote(lands[a].at[j0], lands[a].at[j0], ss, rs, base + 4 * a + 3, sib))
            for r in (1, 2, 3):
                k = j0 ^ r
                dev = (*_chip_of(k), c)
                sends.append(_remote(ins[a].at[pl.ds(lo, half)], lands[a].at[j0, pl.ds(lo, half)], ss, rs,
                                     base + 4 * a + r - 1, dev))
                landed = lands[a].at[k, pl.ds(lo, half)]
                recvs.append(_remote(landed, landed, ss, rs, base + 4 * a + r - 1, dev))
        return sends, recvs
    return plan, 4 * n


def _plan_gather_forward(n):
    def plan(refs, ss, rs, base):
        x, y, c, j0 = _my_place()
        sib = (x, y, 1 - c)
        sends, recvs = [], []
        for a in range(n):
            half = refs[a].shape[1] // 2
            for r in (1, 2, 3):
                k = j0 ^ r
                landed = refs[a].at[k, pl.ds(c * half, half)]
                other = refs[a].at[k, pl.ds((1 - c) * half, half)]
                sends.append(_remote(landed, landed, ss, rs, base + 3 * a + r - 1, sib))
                recvs.append(_remote(other, other, ss, rs, base + 3 * a + r - 1, sib))
        return sends, recvs
    return plan, 3 * n


def _plan_sibling_halves(n):
    def plan(refs, ss, rs, base):
        ins, lands = refs[:n], refs[n:]
        x, y, c, _ = _my_place()
        sib = (x, y, 1 - c)
        sends, recvs = [], []
        for a in range(n):
            half = ins[a].shape[1] // 2
            sends.append(_remote(ins[a].at[:, pl.ds((1 - c) * half, half), :], lands[a], ss, rs, base + a, sib))
            recvs.append(_remote(lands[a], lands[a], ss, rs, base + a, sib))
        return sends, recvs
    return plan, n


def _plan_chip_scatter(n):
    def plan(refs, ss, rs, base):
        ins, lands = refs[:n], refs[n:]
        x, y, c, j0 = _my_place()
        sends, recvs = [], []
        for a in range(n):
            for r in (1, 2, 3):
                k = j0 ^ r
                dev = (*_chip_of(k), c)
                sends.append(_remote(ins[a].at[k], lands[a].at[j0], ss, rs, base + 3 * a + r - 1, dev))
                recvs.append(_remote(lands[a].at[k], lands[a].at[k], ss, rs, base + 3 * a + r - 1, dev))
        return sends, recvs
    return plan, 3 * n


def _plan_sibling_join(n):
    def plan(refs, ss, rs, base):
        x, y, c, _ = _my_place()
        sib = (x, y, 1 - c)
        sends, recvs = [], []
        for a in range(n):
            half = refs[a].shape[0] // 2
            mine = refs[a].at[pl.ds(c * half, half)]
            theirs = refs[a].at[pl.ds((1 - c) * half, half)]
            sends.append(_remote(mine, mine, ss, rs, base + a, sib))
            recvs.append(_remote(theirs, theirs, ss, rs, base + a, sib))
        return sends, recvs
    return plan, n


_HBM = pl.BlockSpec(memory_space=pltpu.HBM)
_SEM = pl.BlockSpec(memory_space=pltpu.SEMAPHORE)
_EFFECT = pltpu.SideEffectType.DATAFLOW_SIDE_EFFECTING


def _exchange_start(name, plan, bufs, after):
    plan_fn, n_sems = plan
    n = len(bufs)

    def body(*refs):
        ss, rs, token = refs[n + len(after)], refs[n + len(after) + 1], refs[-1]
        sends, _ = plan_fn(refs[:n], ss, rs, 0)
        for cp in sends:
            cp.start()
        token[...] = jnp.zeros_like(token)

    res = pl.pallas_call(
        body, name=name,
        out_shape=(pltpu.SemaphoreType.DMA((n_sems,)), pltpu.SemaphoreType.DMA((n_sems,)),
                   *[pltpu.HBM(b.shape, b.dtype) for b in bufs], S((8, 128), f32)),
        in_specs=[_HBM] * n + [ANY] * len(after),
        out_specs=(_SEM, _SEM, *[_HBM] * n, pl.BlockSpec(memory_space=pltpu.VMEM)),
        input_output_aliases={k: 2 + k for k in range(n)},
        compiler_params=pltpu.CompilerParams(has_side_effects=_EFFECT),
    )(*[pltpu.with_memory_space_constraint(b, pltpu.HBM) for b in bufs], *after)
    return res[0], res[1], list(res[2:2 + n]), res[-1]


def _exchange_wait(name, plan, send_sems, recv_sems, bufs, after):
    plan_fn, _ = plan
    n = len(bufs)

    def body(*refs):
        ss, rs, token = refs[n], refs[n + 1], refs[-1]
        sends, recvs = plan_fn(refs[:n], ss, rs, 0)
        for cp in recvs:
            cp.wait_recv()
        for cp in sends:
            cp.wait_send()
        token[...] = jnp.zeros_like(token)

    res = pl.pallas_call(
        body, name=name,
        out_shape=(*[pltpu.HBM(b.shape, b.dtype) for b in bufs], S((8, 128), f32)),
        in_specs=[_HBM] * n + [_SEM, _SEM] + [ANY] * len(after),
        out_specs=(*[_HBM] * n, pl.BlockSpec(memory_space=pltpu.VMEM)),
        input_output_aliases={k: k for k in range(n)},
        compiler_params=pltpu.CompilerParams(has_side_effects=_EFFECT),
    )(*bufs, send_sems, recv_sems, *after)
    return list(res[:n]), res[-1]


class _Order:
    def __init__(self, first):
        self.marker = first
        self.token = None

    def _after(self):
        return [self.marker] + ([] if self.token is None else [self.token])

    def start(self, name, plan, bufs):
        ss, rs, thru, self.token = _exchange_start(name, plan, bufs, self._after())
        return name, plan, ss, rs, thru

    def wait(self, handle):
        name, plan, ss, rs, thru = handle
        out, self.token = _exchange_wait(name + "_wait", plan, ss, rs, thru, self._after())
        return out

    def follows(self, small):
        return small if self.token is None else small + self.token[0, 0]

    def done(self, result):
        self.marker = result[(slice(0, 1),) * result.ndim].reshape(1, 1)


def _all_reduce_small(name, g8):
    _, R, L = g8.shape

    def body(g_ref, out_ref, land, red, send1, recv1, send2, recv2):
        x, y, c, _ = _my_place()
        me = 4 * x + 2 * y + c
        peers = []
        for r in range(1, N_DEV):
            q = me ^ r
            peers.append((q, (q // 4, (q // 2) % 2, q % 2)))
        first = []
        for r, (q, dev) in enumerate(peers):
            cp = pltpu.make_async_remote_copy(src_ref=g_ref.at[q], dst_ref=land.at[me], send_sem=send1.at[r],
                                              recv_sem=recv1.at[r], device_id=dev, device_id_type=MESH)
            cp.start()
            first.append(cp)
        land[me] = g_ref[me]
        for r, (q, dev) in enumerate(peers):
            pltpu.make_async_remote_copy(src_ref=land.at[q], dst_ref=land.at[q], send_sem=send1.at[r],
                                         recv_sem=recv1.at[r], device_id=dev, device_id_type=MESH).wait_recv()
        acc = land[0]
        for d in range(1, N_DEV):
            acc = acc + land[d]
        red[...] = acc
        out_ref[me] = acc
        second = []
        for r, (q, dev) in enumerate(peers):
            cp = pltpu.make_async_remote_copy(src_ref=red, dst_ref=out_ref.at[me], send_sem=send2.at[r],
                                              recv_sem=recv2.at[r], device_id=dev, device_id_type=MESH)
            cp.start()
            second.append(cp)
        for r, (q, dev) in enumerate(peers):
            pltpu.make_async_remote_copy(src_ref=out_ref.at[q], dst_ref=out_ref.at[q], send_sem=send2.at[r],
                                         recv_sem=recv2.at[r], device_id=dev, device_id_type=MESH).wait_recv()
        for cp in first + second:
            cp.wait_send()

    vm = pl.BlockSpec(memory_space=pltpu.VMEM)
    return pl.pallas_call(
        body, name=name, in_specs=[vm], out_specs=vm, out_shape=S(g8.shape, f32),
        scratch_shapes=[pltpu.VMEM((N_DEV, R, L), f32), pltpu.VMEM((R, L), f32)]
        + [pltpu.SemaphoreType.DMA((N_DEV - 1,))] * 4,
        compiler_params=_params(),
    )(g8)


def _pair_sum(name, g4, sib):
    _, rows, cols = g4.shape
    half = rows // 2
    lanes = -(-cols // 128) * 128
    tr = _tile(half, max(16, (512 * 1024) // lanes // 16 * 16), 16)
    nb = half // tr

    def body(g_ref, s_ref, pb_ref, own_ref):
        j = pl.program_id(1)
        t = g_ref[...] + s_ref[...]
        pb_ref[...] = t.astype(bf16)

        @pl.when(j == _my_place()[3])
        def _():
            own_ref[...] = t

    return pl.pallas_call(
        body, name=name, grid=(nb, N_CHIPS),
        in_specs=[BS((None, tr, cols), lambda i, j: (j, lax.axis_index("c") * nb + i, 0)),
                  BS((None, tr, cols), lambda i, j: (j, i, 0))],
        out_specs=[BS((None, tr, cols), lambda i, j: (j, i, 0)), BS((tr, cols), lambda i, j: (i, 0))],
        out_shape=[S((N_CHIPS, half, cols), bf16), S((half, cols), f32)],
    )(g4, sib)


def _chip_sum(name, own, got):
    half, cols = own.shape
    lanes = -(-cols // 128) * 128
    tr = _tile(half, max(16, (512 * 1024) // lanes // 16 * 16), 16)
    nb = half // tr

    def body(own_ref, *rest):
        got_refs, o_ref = rest[:N_CHIPS], rest[N_CHIPS]
        j0 = _my_place()[3]
        acc = None
        for k in range(N_CHIPS):
            t = jnp.where(j0 == k, own_ref[...], got_refs[k][...].astype(f32))
            acc = t if acc is None else acc + t
        o_ref[...] = acc

    def slot(k):
        return BS((None, tr, cols), lambda i: (jnp.where(_my_place()[3] == k, (k + 1) % N_CHIPS, k), i, 0))

    return pl.pallas_call(
        body, name=name, grid=(nb,),
        in_specs=[BS((tr, cols), lambda i: (i, 0))] + [slot(k) for k in range(N_CHIPS)],
        out_specs=BS((tr, cols), lambda i: (lax.axis_index("c") * nb + i, 0)),
        out_shape=S((2 * half, cols), f32),
    )(own, got, got, got, got)


def _pack_rows(arrs, n_rows):
    flat = jnp.concatenate([a.reshape(-1).astype(f32) for a in arrs])
    return jnp.pad(flat, (0, n_rows * 128 - flat.shape[0])).reshape(n_rows, 128)


def kernel(x, p, norm_mix, w_in, q_norm, k_norm, forget_bias, gmlp_v_norm, gmlp_w_s, gmlp_b_s, pool_w, pool_scale, w_out, norm_ffn, w_ffn_gate, w_ffn_up, w_ffn_down, norm_ple, w_ple_gate, w_ple_proj, loss_target, m_norm_mix, m_w_in, m_q_norm, m_k_norm, m_forget_bias, m_gmlp_v_norm, m_gmlp_w_s, m_gmlp_b_s, m_pool_w, m_pool_scale, m_w_out, m_norm_ffn, m_w_ffn_gate, m_w_ffn_up, m_w_ffn_down, m_norm_ple, m_w_ple_gate, m_w_ple_proj, v_norm_mix, v_w_in, v_q_norm, v_k_norm, v_forget_bias, v_gmlp_v_norm, v_gmlp_w_s, v_gmlp_b_s, v_pool_w, v_pool_scale, v_w_out, v_norm_ffn, v_w_ffn_gate, v_w_ffn_up, v_w_ffn_down, v_norm_ple, v_w_ple_gate, v_w_ple_proj):
    W = dict(norm_mix=norm_mix, w_in=w_in, q_norm=q_norm, k_norm=k_norm, forget_bias=forget_bias,
             gmlp_v_norm=gmlp_v_norm, gmlp_w_s=gmlp_w_s, gmlp_b_s=gmlp_b_s, pool_w=pool_w, pool_scale=pool_scale,
             w_out=w_out, norm_ffn=norm_ffn, w_ffn_gate=w_ffn_gate, w_ffn_up=w_ffn_up, w_ffn_down=w_ffn_down,
             norm_ple=norm_ple, w_ple_gate=w_ple_gate, w_ple_proj=w_ple_proj)
    M = dict(norm_mix=m_norm_mix, w_in=m_w_in, q_norm=m_q_norm, k_norm=m_k_norm, forget_bias=m_forget_bias,
             gmlp_v_norm=m_gmlp_v_norm, gmlp_w_s=m_gmlp_w_s, gmlp_b_s=m_gmlp_b_s, pool_w=m_pool_w,
             pool_scale=m_pool_scale, w_out=m_w_out, norm_ffn=m_norm_ffn, w_ffn_gate=m_w_ffn_gate,
             w_ffn_up=m_w_ffn_up, w_ffn_down=m_w_ffn_down, norm_ple=m_norm_ple, w_ple_gate=m_w_ple_gate,
             w_ple_proj=m_w_ple_proj)
    V = dict(norm_mix=v_norm_mix, w_in=v_w_in, q_norm=v_q_norm, k_norm=v_k_norm, forget_bias=v_forget_bias,
             gmlp_v_norm=v_gmlp_v_norm, gmlp_w_s=v_gmlp_w_s, gmlp_b_s=v_gmlp_b_s, pool_w=v_pool_w,
             pool_scale=v_pool_scale, w_out=v_w_out, norm_ffn=v_norm_ffn, w_ffn_gate=v_w_ffn_gate,
             w_ffn_up=v_w_ffn_up, w_ffn_down=v_w_ffn_down, norm_ple=v_norm_ple, w_ple_gate=v_w_ple_gate,
             w_ple_proj=v_w_ple_proj)

    L = w_in.shape[0]
    _, T, D = x.shape
    A, Wd = D // 2, D // 4
    H = A // HEAD
    G = gmlp_w_s.shape[1]
    Gp = pool_w.shape[1]
    DP4 = w_in.shape[2]
    DP = N_CHIPS * DP4
    NM = 3 * A + 3 * Wd
    FS = w_ffn_gate.shape[2]
    FF = N_CHIPS * FS
    DS = D // N_CHIPS
    PL = p.shape[-1]
    assert Wd // G == HEAD and Wd // Gp == HEAD and DP == NM + H and H <= HEAD
    assert all(w & (w - 1) == 0 for w in POOL_WINDOWS[:Gp])
    tb = _tile(T, 256, HEAD)
    nb = T // tb
    tm = _tile(T, 512, 16)
    tmw = _tile(T, 1024, 16)
    tn = _tile(NM, 512, 128)
    tnd = _tile(D, 512, 128)
    tkd = _tile(D, 1024, 128)
    tnw = _tile(D, 1024, 128)
    col_gu, col_gv, col_xp = 3 * A // HEAD, (3 * A + Wd) // HEAD, (3 * A + 2 * Wd) // Wd
    col_dg, col_dp = A // HEAD, (A + Wd) // Wd

    order = _Order(x[0, :1, :1])

    def gather_start(i, names, tag=""):
        shards = [_cast_layer(f"cast_{n}_{i}", W[n], i) for n in names]
        lands = [lax.empty((N_CHIPS,) + s.shape, bf16) for s in shards]
        return names, order.start(f"ag_ici_{i}{tag}", _plan_gather_chips(len(names)), shards + lands)

    def gather_forward(i, started, tag=""):
        names, handle = started
        return names, order.start(f"ag_d2d_{i}{tag}", _plan_gather_forward(len(names)), order.wait(handle)[len(names):])

    def gathered(forwarded):
        names, handle = forwarded
        g = dict(zip(names, order.wait(handle)))
        out = {}
        if "w_in" in g:
            def cols(lo, hi):
                parts = []
                for j in range(N_CHIPS):
                    s, e = max(lo, j * DP4), min(hi, (j + 1) * DP4)
                    if s < e:
                        parts.append(g["w_in"][j][:, s - j * DP4:e - j * DP4])
                return parts

            out["w_main"] = jnp.concatenate(cols(0, 3 * A) + cols(3 * A + H, DP), axis=1)
            out["w_f"] = jnp.pad(jnp.concatenate(cols(3 * A, 3 * A + H), axis=1), ((0, 0), (0, HEAD - H)))
        if "w_out" in g:
            out["w_out"] = g["w_out"].reshape(D, D)
        if "w_ffn_gate" in g:
            out.update(w_gate=g["w_ffn_gate"], w_up=g["w_ffn_up"], w_down=g["w_ffn_down"].reshape(FF, D),
                       w_pg=g["w_ple_gate"].reshape(D, D), w_pp=g["w_ple_proj"])
        return out

    Wf = [None] * L
    first = [gather_start(0, ["w_in"], "a"), gather_start(0, ["w_out"], "b"), gather_start(0, BIG[2:], "c")]
    d2d = gather_forward(0, first[0], "a")
    ici = gather_start(1, BIG) if L > 1 else None
    Wf[0] = gathered(d2d)

    h = x.reshape(T, D)
    pb16 = p.reshape(L, T, PL).astype(bf16)
    saved = []

    for i in range(L):
        w = Wf[i]
        sv = dict(h0=h)
        xn1 = _rms_fwd(f"rms1_{i}", h, order.follows(norm_mix[i]))
        (P,) = _matmul(f"proj_{i}", "nn", (T // tmw, NM // tn),
                       [((xn1, BS((tmw, D), lambda i, j: (i, 0))), (w["w_main"], BS((D, tn), lambda i, j: (0, j))))], [],
                       [((T, NM), f32, BS((tmw, tn), lambda i, j: (i, j)))])
        (Pf,) = _matmul(f"projf_{i}", "nn", (T // tm, 1),
                        [((xn1, BS((tm, D), lambda i, j: (i, 0))), (w["w_f"], BS((D, HEAD), lambda i, j: (0, 0))))], [],
                        [((T, HEAD), f32, BS((tm, HEAD), lambda i, j: (i, 0)))])
        fb = jnp.pad(forget_bias[i], (0, HEAD - H)).reshape(1, HEAD)
        cc, ct = _fgate_fwd(f"fgate_{i}", Pf, fb)
        c_col = ct[:H].reshape(H, T, 1)
        c_row = ct[:H].reshape(H, nb, 1, tb)
        qg, kg = q_norm[i].reshape(1, HEAD), k_norm[i].reshape(1, HEAD)
        qn, kn, vb = _qk_norm(f"qknorm_{i}", P, qg, kg, A)
        y_attn, o32, lse = _attn_fwd(f"attn_{i}", qn, kn, vb, c_col, c_row, tb)
        gain = gmlp_v_norm[i].reshape(G, 1, HEAD)
        bs = gmlp_b_s[i].reshape(G, HEAD, 1)
        y_gmlp = _gmlp_fwd(f"gmlp_{i}", P, gain, gmlp_w_s[i], bs, col_gu, col_gv, Wd)
        ps = pool_scale[i].reshape(1, Wd)
        y_pool = _pool_fwd(f"pool_{i}", P, pool_w[i], ps, col_xp, Wd)
        mix = jnp.concatenate([y_attn, y_gmlp, y_pool], axis=1)
        if i == 0:
            order.done(mix)
            w.update(gathered(gather_forward(0, first[1], "b")))
        (h1,) = _matmul(f"out_{i}", "nn", (T // tmw, D // tnd),
                        [((mix, BS((tmw, D), lambda i, j: (i, 0))), (w["w_out"], BS((D, tnd), lambda i, j: (0, j))))],
                        [(h, BS((tmw, tnd), lambda i, j: (i, j)))],
                        [((T, D), f32, BS((tmw, tnd), lambda i, j: (i, j)))],
                        epilogue=lambda accs, ex: (accs[0] + ex[0],), after=order.token)
        xn2 = _rms_fwd(f"rms2_{i}", h1, norm_ffn[i])
        if i == 0:
            order.done(xn2)
            w.update(gathered(gather_forward(0, first[2], "c")))

        def ffn_epi(accs, ex):
            g_, u_ = accs
            return g_, u_, g_ * jax.nn.sigmoid(g_) * u_

        ffo = BS((tm, FS), lambda i, j: (i, j))
        Gt, Ut, act = _matmul(f"ffn1_{i}", "nn", (T // tm, N_CHIPS),
                              [((xn2, BS((tm, D), lambda i, j: (i, 0))), (w["w_gate"], BS((None, D, FS), lambda i, j: (j, 0, 0)))),
                               ((xn2, BS((tm, D), lambda i, j: (i, 0))), (w["w_up"], BS((None, D, FS), lambda i, j: (j, 0, 0))))],
                              [], [((T, FF), bf16, ffo)] * 3, epilogue=ffn_epi, after=order.token)
        (h2,) = _matmul(f"ffn2_{i}", "nn", (T // tmw, D // tnd),
                        [((act, BS((tmw, FF), lambda i, j: (i, 0))), (w["w_down"], BS((FF, tnd), lambda i, j: (0, j))))],
                        [(h1, BS((tmw, tnd), lambda i, j: (i, j)))],
                        [((T, D), f32, BS((tmw, tnd), lambda i, j: (i, j)))],
                        epilogue=lambda accs, ex: (accs[0] + ex[0],))
        order.done(h2)
        if i + 1 < L:
            d2d = gather_forward(i + 1, ici)
            ici = gather_start(i + 2, BIG) if i + 2 < L else None
        xn3 = _rms_fwd(f"rms3_{i}", h2, order.follows(norm_ple[i]))

        def ple_epi(accs, ex):
            gate = jax.nn.sigmoid(accs[0])
            return ex[0] + accs[1] * gate, gate, accs[1]

        dso = BS((tmw, DS), lambda i, j: (i, j))
        h3, gate, e = _matmul(f"ple_{i}", "nn", (T // tmw, N_CHIPS),
                              [((xn3, BS((tmw, D), lambda i, j: (i, 0))), (w["w_pg"], BS((D, DS), lambda i, j: (0, j)))),
                               ((pb16[i], BS((tmw, PL), lambda i, j: (i, 0))), (w["w_pp"], BS((None, PL, DS), lambda i, j: (j, 0, 0))))],
                              [(h2, dso)], [((T, D), f32, dso), ((T, D), bf16, dso), ((T, D), bf16, dso)], epilogue=ple_epi)
        sv.update(xn1=xn1, P=P, Pf=Pf, fb=fb, c_col=c_col, c_row=c_row, qn=qn, kn=kn, vb=vb, o32=o32, lse=lse,
                  mix=mix, h1=h1, xn2=xn2, Gt=Gt, Ut=Ut, act=act, h2=h2, xn3=xn3, gate=gate, e=e)
        saved.append(sv)
        h = h3
        order.done(h3)
        if i + 1 < L:
            Wf[i + 1] = gathered(d2d)

    dh, loss_tile = _loss_grad("loss", h, loss_target.reshape(T, D))

    small_g = {n: [None] * L for n in SMALL}
    big_out = {}

    def stage_a(u):
        n_u = len(u["names"])
        lands = [lax.empty((N_CHIPS, g.shape[1] // 2, g.shape[2]), f32) for g in u["grads"]]
        u["a"] = order.start(f"rs_a_{u['tag']}", _plan_sibling_halves(n_u), u["grads"] + lands)

    def stage_pair(u):
        n_u = len(u["names"])
        out = order.wait(u["a"])
        pairs = [_pair_sum(f"rs_pair_{u['tag']}_{a}", out[a], out[n_u + a]) for a in range(n_u)]
        u["pb"], u["own"] = [t[0] for t in pairs], [t[1] for t in pairs]

    def stage_b(u):
        lands = [lax.empty(t.shape, bf16) for t in u["pb"]]
        u["b"] = order.start(f"rs_b_{u['tag']}", _plan_chip_scatter(len(u["names"])), u["pb"] + lands)

    def stage_sum(u):
        n_u = len(u["names"])
        out = order.wait(u["b"])
        u["sum"] = [_chip_sum(f"rs_sum_{u['tag']}_{a}", u["own"][a], out[n_u + a]) for a in range(n_u)]

    def stage_c(u):
        u["c"] = order.start(f"rs_c_{u['tag']}", _plan_sibling_join(len(u["names"])), u["sum"])

    def stage_adamw(u):
        for n, r in zip(u["names"], order.wait(u["c"])):
            big_out[n] = _adamw_layer(f"adamw_{n}_{u['layer']}", u["layer"], W[n], M[n], V[n],
                                      r.reshape(W[n].shape[1:]), big_out.get(n))

    dh1 = prev_f = prev_m = None
    for i in reversed(range(L)):
        w, sv = Wf[i], saved[i]
        if dh1 is not None:
            dh, _, dg = _rms_bwd(f"rms1_bw_{i + 1}", dxn1, saved[i + 1]["h0"], order.follows(norm_mix[i + 1]), dh1)
            small_g["norm_mix"][i + 1] = dg.reshape(D)
        de, dz = _ple_bwd_elem(f"ple_bw_{i}", dh, sv["gate"], sv["e"])
        (d_wpp,) = _matmul(f"d_wpp_{i}", "tn", (N_CHIPS, 1),
                           [((pb16[i], BS((T, PL), lambda i, j: (0, 0))), (de, BS((T, DS), lambda i, j: (0, i))))], [],
                           [((N_CHIPS, PL, DS), f32, BS((None, PL, DS), lambda i, j: (i, 0, 0)))])
        (d_wpg,) = _matmul(f"d_wpg_{i}", "tn", (D // tkd, D // tnd),
                           [((sv["xn3"], BS((T, tkd), lambda i, j: (0, i))), (dz, BS((T, tnd), lambda i, j: (0, j))))], [],
                           [((D, D), f32, BS((tkd, tnd), lambda i, j: (i, j)))])
        order.done(dz)
        if prev_m is not None:
            stage_pair(prev_m)
        (dxn3,) = _matmul(f"d_xn3_{i}", "nt", (T // tmw, D // tnd),
                          [((dz, BS((tmw, D), lambda i, j: (i, 0))), (w["w_pg"], BS((tnd, D), lambda i, j: (j, 0))))], [],
                          [((T, D), f32, BS((tmw, tnd), lambda i, j: (i, j)))], after=order.token)
        dh2, dh2b, dg = _rms_bwd(f"rms3_bw_{i}", dxn3, sv["h2"], norm_ple[i], dh)
        small_g["norm_ple"][i] = dg.reshape(D)

        def dffn_epi(accs, ex):
            da = accs[0]
            g_, u_ = ex[0].astype(f32), ex[1].astype(f32)
            sg = jax.nn.sigmoid(g_)
            return da * u_ * (sg * (1.0 + g_ * (1.0 - sg))), da * (g_ * sg)

        ffo = BS((tm, FS), lambda j, i: (i, j))
        dG, dU = _matmul(f"d_act_{i}", "nt", (N_CHIPS, T // tm),
                         [((dh2b, BS((tm, D), lambda j, i: (i, 0))), (w["w_down"], BS((FS, D), lambda j, i: (j, 0))))],
                         [(sv["Gt"], ffo), (sv["Ut"], ffo)], [((T, FF), bf16, ffo)] * 2, epilogue=dffn_epi)
        (d_wd,) = _matmul(f"d_wd_{i}", "tn", (N_CHIPS, D // tnw),
                          [((sv["act"], BS((T, FS), lambda i, j: (0, i))), (dh2b, BS((T, tnw), lambda i, j: (0, j))))], [],
                          [((FF, D), f32, BS((FS, tnw), lambda i, j: (i, j)))])
        gu_out = BS((None, tnd, FS), lambda j, i: (j, i, 0))
        d_wg, d_wu = _matmul(f"d_wgu_{i}", "tn", (N_CHIPS, D // tnd),
                             [((sv["xn2"], BS((T, tnd), lambda j, i: (0, i))), (dG, BS((T, FS), lambda j, i: (0, j)))),
                              ((sv["xn2"], BS((T, tnd), lambda j, i: (0, i))), (dU, BS((T, FS), lambda j, i: (0, j))))], [],
                             [((N_CHIPS, D, FS), f32, gu_out)] * 2, epilogue=lambda accs, ex: (accs[0], accs[1]))
        order.done(dG)
        unit_f = dict(tag=f"{i}f", layer=i, names=["w_ffn_gate", "w_ffn_up", "w_ffn_down", "w_ple_gate", "w_ple_proj"],
                      grads=[d_wg, d_wu, d_wd.reshape(N_CHIPS, FS, D), d_wpg.reshape(N_CHIPS, DS, D), d_wpp])
        stage_a(unit_f)
        if prev_f is not None:
            stage_sum(prev_f)
            stage_c(prev_f)
        if prev_m is not None:
            stage_b(prev_m)
        tm2 = _tile(T, 256, 16)
        (dxn2,) = _matmul(f"d_xn2_{i}", "nt", (D // tnd, T // tm2),
                          [((dG, BS((tm2, FF), lambda j, i: (i, 0))), (w["w_gate"], BS((N_CHIPS, tnd, FS), lambda j, i: (0, j, 0)))),
                           ((dU, BS((tm2, FF), lambda j, i: (i, 0))), (w["w_up"], BS((N_CHIPS, tnd, FS), lambda j, i: (0, j, 0))))], [],
                          [((T, D), f32, BS((tm2, tnd), lambda j, i: (i, j)))], after=order.token)
        dh1, dh1b, dg = _rms_bwd(f"rms2_bw_{i}", dxn2, sv["h1"], norm_ffn[i], dh2)
        small_g["norm_ffn"][i] = dg.reshape(D)
        (dmix,) = _matmul(f"d_mix_{i}", "nt", (T // tmw, D // tnd),
                          [((dh1b, BS((tmw, D), lambda i, j: (i, 0))), (w["w_out"], BS((tnd, D), lambda i, j: (j, 0))))], [],
                          [((T, D), f32, BS((tmw, tnd), lambda i, j: (i, j)))])
        (d_wout,) = _matmul(f"d_wout_{i}", "tn", (D // tkd, D // tnd),
                            [((sv["mix"], BS((T, tkd), lambda i, j: (0, i))), (dh1b, BS((T, tnd), lambda i, j: (0, j))))], [],
                            [((D, D), f32, BS((tkd, tnd), lambda i, j: (i, j)))])
        order.done(dmix)
        stage_pair(unit_f)
        stage_b(unit_f)
        if prev_f is not None:
            stage_adamw(prev_f)
        qg, kg = q_norm[i].reshape(1, HEAD), k_norm[i].reshape(1, HEAD)
        dq, dk, dv, dc_row, dqg, dkg = _attn_bwd(f"attn_bw_{i}", sv["qn"], sv["kn"], sv["vb"], sv["o32"], dmix,
                                                 sv["lse"], sv["c_col"], sv["c_row"], sv["P"], order.follows(qg), kg, tb)
        small_g["q_norm"][i] = dqg.reshape(HEAD)
        small_g["k_norm"][i] = dkg.reshape(HEAD)
        dct = jnp.pad(dc_row.reshape(H, T), ((0, HEAD - H), (0, 0)))
        dPf, dfb = _fgate_bwd(f"fgate_bw_{i}", dct, sv["Pf"], sv["fb"])
        small_g["forget_bias"][i] = dfb[0, :H]
        gain = gmlp_v_norm[i].reshape(G, 1, HEAD)
        bs = gmlp_b_s[i].reshape(G, HEAD, 1)
        dgu, dgv, dws, dbs, dgain = _gmlp_bwd(f"gmlp_bw_{i}", sv["P"], dmix, gain, gmlp_w_s[i], bs, col_gu, col_gv,
                                              col_dg, Wd)
        small_g["gmlp_w_s"][i] = dws
        small_g["gmlp_b_s"][i] = dbs.reshape(G, HEAD)
        small_g["gmlp_v_norm"][i] = dgain.reshape(G, HEAD)
        ps = pool_scale[i].reshape(1, Wd)
        dxp, dpw, dps = _pool_bwd(f"pool_bw_{i}", sv["P"], dmix, pool_w[i], ps, col_xp, col_dp, Wd)
        small_g["pool_w"][i] = dpw
        small_g["pool_scale"][i] = dps.reshape(Wd)
        dP = jnp.concatenate([dq, dk, dv, dgu, dgv, dxp], axis=1)
        (d_wmain,) = _matmul(f"d_wmain_{i}", "tn", (D // tkd, NM // tn),
                             [((sv["xn1"], BS((T, tkd), lambda i, j: (0, i))), (dP, BS((T, tn), lambda i, j: (0, j))))], [],
                             [((D, NM), f32, BS((tkd, tn), lambda i, j: (i, j)))])
        (d_wf,) = _matmul(f"d_wf_{i}", "tn", (D // tnd, 1),
                          [((sv["xn1"], BS((T, tnd), lambda i, j: (0, i))), (dPf, BS((T, HEAD), lambda i, j: (0, 0))))], [],
                          [((D, HEAD), f32, BS((tnd, HEAD), lambda i, j: (i, 0)))])
        def win_cols(lo, hi):
            out = []
            for src, s0, e0, off in ((d_wmain, 0, 3 * A, 0), (d_wf, 3 * A, 3 * A + H, 3 * A), (d_wmain, 3 * A + H, DP, H)):
                s, e = max(lo, s0), min(hi, e0)
                if s < e:
                    out.append(src[:, s - off:e - off])
            return out

        d_win4 = jnp.stack([jnp.concatenate(win_cols(j * DP4, (j + 1) * DP4), axis=1) for j in range(N_CHIPS)])
        order.done(dP)
        unit_m = dict(tag=f"{i}m", layer=i, names=["w_in", "w_out"], grads=[d_win4, d_wout.reshape(N_CHIPS, DS, D)])
        stage_a(unit_m)
        if prev_m is not None:
            stage_sum(prev_m)
            stage_c(prev_m)
        (dxn1,) = _matmul(f"d_xn1_{i}", "nt", (D // tnd, T // tm),
                          [((dP, BS((tm, NM), lambda j, i: (i, 0))), (w["w_main"], BS((tnd, NM), lambda j, i: (j, 0)))),
                           ((dPf, BS((tm, HEAD), lambda j, i: (i, 0))), (w["w_f"], BS((tnd, HEAD), lambda j, i: (j, 0))))], [],
                          [((T, D), f32, BS((tm, tnd), lambda j, i: (i, j)))], after=order.token)
        order.done(dxn1)
        if prev_m is not None:
            stage_adamw(prev_m)
        prev_f, prev_m = unit_f, unit_m

    dh, _, dg = _rms_bwd("rms1_bw_0", dxn1, saved[0]["h0"], order.follows(norm_mix[0]), dh1)
    small_g["norm_mix"][0] = dg.reshape(D)
    order.done(dh)
    stage_pair(prev_m)
    stage_b(prev_m)
    stage_sum(prev_f)
    stage_c(prev_f)

    small_full = {n: jnp.stack(small_g[n]) for n in SMALL}
    n_small = sum(int(W[n].size) for n in SMALL) + 1
    rows8 = -(-n_small // (128 * 64)) * 8
    packed = order.follows(_pack_rows([small_full[n] for n in SMALL] + [loss_tile[0, :1]], N_DEV * rows8))
    summed = _all_reduce_small("allreduce_small", packed.reshape(N_DEV, rows8, 128)).reshape(-1)
    order.done(summed)
    stage_adamw(prev_f)
    stage_sum(prev_m)
    stage_c(prev_m)
    stage_adamw(prev_m)
    grads, off = {}, 0
    for n in SMALL:
        grads[n] = summed[off:off + W[n].size].reshape(W[n].shape)
        off += W[n].size
    loss = summed[off]

    delta, new_m, new_v = {}, {}, {}
    wp, mp, vp = (_pack_rows([t[n] for n in SMALL], N_DEV * rows8) for t in (W, M, V))
    ds_, ms_, vs_ = _adamw("adamw_small", wp, summed.reshape(-1, 128), mp, vp)
    off = 0
    for n in SMALL:
        sz = W[n].size
        delta[n], new_m[n], new_v[n] = (t.reshape(-1)[off:off + sz].reshape(W[n].shape) for t in (ds_, ms_, vs_))
        off += sz
    for n in BIG:
        grads[n], delta[n], new_m[n], new_v[n] = big_out[n]

    return (loss, dh.reshape(1, T, D), *[grads[n] for n in WEIGHTS], *[delta[n] for n in WEIGHTS],
            *[new_m[n] for n in WEIGHTS], *[new_v[n] for n in WEIGHTS])
```

```python
import jax
import jax.numpy as jnp
from jax import lax
from jax.experimental import pallas as pl
from jax.experimental.pallas import tpu as pltpu

f32, bf16 = jnp.float32, jnp.bfloat16
S = jax.ShapeDtypeStruct
BS = pl.BlockSpec
ANY = pl.BlockSpec(memory_space=pl.ANY)
MESH = pl.DeviceIdType.MESH

EPS = 1e-6
HEAD = 128
POOL_WINDOWS = (2, 4, 8, 16)
NEG = -1e30
N_CHIPS = 4
N_DEV = 8
VMEM_LIMIT = 56 * 1024 * 1024

ADAM_LR, ADAM_B1, ADAM_B2, ADAM_EPS, ADAM_WD, ADAM_STEP = 0.001, 0.9, 0.999, 1e-08, 0.01, 10

BIG = ("w_in", "w_out", "w_ffn_gate", "w_ffn_up", "w_ffn_down", "w_ple_gate", "w_ple_proj")
SMALL = ("norm_mix", "q_norm", "k_norm", "forget_bias", "gmlp_v_norm", "gmlp_w_s", "gmlp_b_s", "pool_w",
         "pool_scale", "norm_ffn", "norm_ple")
WEIGHTS = ("norm_mix", "w_in", "q_norm", "k_norm", "forget_bias", "gmlp_v_norm", "gmlp_w_s", "gmlp_b_s", "pool_w",
           "pool_scale", "w_out", "norm_ffn", "w_ffn_gate", "w_ffn_up", "w_ffn_down", "norm_ple", "w_ple_gate",
           "w_ple_proj")


def _tile(n, target, mult):
    best = None
    for t in range(mult, min(n, target) + 1, mult):
        if n % t == 0:
            best = t
    return best if best is not None else n


def _params(**kw):
    return pltpu.CompilerParams(vmem_limit_bytes=VMEM_LIMIT, **kw)


def _dot(a, b, kind):
    dims = {"nn": (((1,), (0,)), ((), ())), "nt": (((1,), (1,)), ((), ())), "tn": (((0,), (0,)), ((), ()))}[kind]
    return lax.dot_general(a.astype(bf16), b.astype(bf16), dims, preferred_element_type=f32)


def _heads_per_program(n_heads):
    return 2 if n_heads % 2 == 0 else 1


def _my_place():
    x, y, c = lax.axis_index("x"), lax.axis_index("y"), lax.axis_index("c")
    return x, y, c, 2 * x + y


def _matmul(name, kind, grid, pairs, extras, outs, epilogue=None, after=None):
    n_p, n_e = len(pairs), len(extras)
    tokens = [] if after is None else [(after, BS((8, 128), lambda *_: (0, 0)))]

    def body(*refs):
        a_refs, b_refs = refs[:n_p], refs[n_p:2 * n_p]
        e_refs = refs[2 * n_p:2 * n_p + n_e]
        o_refs = refs[2 * n_p + n_e + len(tokens):]
        accs = []
        for a_ref, b_ref in zip(a_refs, b_refs):
            if len(b_ref.shape) == 3:
                w = b_ref.shape[2]
                acc = None
                for s in range(b_ref.shape[0]):
                    d = _dot(a_ref[:, s * w:(s + 1) * w], b_ref[s], kind)
                    acc = d if acc is None else acc + d
            else:
                acc = _dot(a_ref[...], b_ref[...], kind)
            accs.append(acc)
        if epilogue is None:
            res = accs[0]
            for t in accs[1:]:
                res = res + t
            res = (res,)
        else:
            res = epilogue(accs, [e[...] for e in e_refs])
        for o_ref, o in zip(o_refs, res):
            o_ref[...] = o.astype(o_ref.dtype)

    in_arrays = [p[0][0] for p in pairs] + [p[1][0] for p in pairs] + [e[0] for e in extras + tokens]
    in_specs = [p[0][1] for p in pairs] + [p[1][1] for p in pairs] + [e[1] for e in extras + tokens]
    res = pl.pallas_call(
        body, name=name, grid=grid, in_specs=in_specs,
        out_specs=[o[2] for o in outs], out_shape=[S(o[0], o[1]) for o in outs],
        compiler_params=_params(),
    )(*in_arrays)
    return res


def _rms_fwd(name, x, g):
    T, D = x.shape
    tr = _tile(T, 256, 8)

    def body(x_ref, g_ref, o_ref):
        xv = x_ref[...]
        r = lax.rsqrt(jnp.mean(xv * xv, axis=-1, keepdims=True) + EPS)
        o_ref[...] = (xv * r * g_ref[...]).astype(o_ref.dtype)

    return pl.pallas_call(
        body, name=name, grid=(T // tr,),
        in_specs=[BS((tr, D), lambda i: (i, 0)), BS((1, D), lambda i: (0, 0))],
        out_specs=BS((tr, D), lambda i: (i, 0)), out_shape=S((T, D), bf16),
    )(x, g.reshape(1, D))


def _rms_bwd(name, dxn, x, g, dres):
    T, D = x.shape
    tr = _tile(T, 256, 8)

    def body(dxn_ref, x_ref, g_ref, dres_ref, dx_ref, dxb_ref, dg_ref):
        i = pl.program_id(0)
        xv = x_ref[...]
        r = lax.rsqrt(jnp.mean(xv * xv, axis=-1, keepdims=True) + EPS)
        xh = xv * r
        dxn_v = dxn_ref[...]
        dxh = dxn_v * g_ref[...]
        dx = dres_ref[...] + r * (dxh - xh * jnp.mean(dxh * xh, axis=-1, keepdims=True))
        dx_ref[...] = dx
        dxb_ref[...] = dx.astype(bf16)
        part = jnp.sum(dxn_v * xh, axis=0, keepdims=True)

        @pl.when(i == 0)
        def _():
            dg_ref[...] = part

        @pl.when(i > 0)
        def _():
            dg_ref[...] += part

    row = BS((tr, D), lambda i: (i, 0))
    vec = BS((1, D), lambda i: (0, 0))
    return pl.pallas_call(
        body, name=name, grid=(T // tr,),
        in_specs=[row, row, vec, row], out_specs=[row, row, vec],
        out_shape=[S((T, D), f32), S((T, D), bf16), S((1, D), f32)],
    )(dxn, x, g.reshape(1, D), dres)


def _loss_grad(name, y, tgt):
    T, D = y.shape
    tr = _tile(T, 256, 8)

    def body(y_ref, t_ref, dy_ref, l_ref):
        i = pl.program_id(0)
        e = y_ref[...] - t_ref[...]
        dy_ref[...] = e * (1.0 / D)
        part = 0.5 * jnp.sum(jnp.mean(e * e, axis=-1, keepdims=True), axis=0, keepdims=True)

        @pl.when(i == 0)
        def _():
            l_ref[...] = jnp.zeros_like(l_ref)

        l_ref[...] += jnp.broadcast_to(part, l_ref.shape)

    row = BS((tr, D), lambda i: (i, 0))
    return pl.pallas_call(
        body, name=name, grid=(T // tr,), in_specs=[row, row],
        out_specs=[row, BS((8, 128), lambda i: (0, 0))],
        out_shape=[S((T, D), f32), S((8, 128), f32)],
    )(y, tgt)


def _ple_bwd_elem(name, dh, gate, e):
    T, D = dh.shape
    tr = _tile(T, 256, 16)

    def body(dh_ref, g_ref, e_ref, de_ref, dz_ref):
        d = dh_ref[...]
        g = g_ref[...].astype(f32)
        de_ref[...] = (d * g).astype(bf16)
        dz_ref[...] = (d * e_ref[...].astype(f32) * g * (1.0 - g)).astype(bf16)

    row = BS((tr, D), lambda i: (i, 0))
    return pl.pallas_call(
        body, name=name, grid=(T // tr,), in_specs=[row, row, row], out_specs=[row, row],
        out_shape=[S((T, D), bf16), S((T, D), bf16)],
    )(dh, gate, e)


def _cast_layer(name, w_all, layer):
    _, R, C = w_all.shape
    lanes = -(-C // 128) * 128
    tr = _tile(R, max(16, (1024 * 1024) // lanes // 16 * 16), 16)

    def body(w_ref, o_ref):
        o_ref[...] = w_ref[...].astype(bf16)

    return pl.pallas_call(
        body, name=name, grid=(R // tr,), in_specs=[BS((None, tr, C), lambda r: (layer, r, 0))],
        out_specs=BS((tr, C), lambda r: (r, 0)), out_shape=S((R, C), bf16),
    )(w_all)


def _gelu_and_grad(x):
    k0, k1 = 0.7978845608028654, 0.044715
    th = jnp.tanh(k0 * (x + k1 * x * x * x))
    val = 0.5 * x * (1.0 + th)
    grad = 0.5 * (1.0 + th) + 0.5 * x * (1.0 - th * th) * (k0 * (1.0 + 3.0 * k1 * x * x))
    return val, grad


def _fgate_fwd(name, pf, fb):
    T = pf.shape[0]

    def body(pf_ref, fb_ref, c_ref, ct_ref):
        xv = jax.nn.log_sigmoid(pf_ref[...] + fb_ref[...])
        row = lax.broadcasted_iota(jnp.int32, xv.shape, 0)
        s = 1
        while s < T:
            xv = xv + jnp.where(row >= s, pltpu.roll(xv, s, 0), 0.0)
            s *= 2
        c_ref[...] = xv
        ct_ref[...] = xv.T

    return pl.pallas_call(body, name=name, out_shape=[S((T, HEAD), f32), S((HEAD, T), f32)])(pf, fb)


def _fgate_bwd(name, dct, pf, fb):
    T = pf.shape[0]

    def body(dct_ref, pf_ref, fb_ref, dpf_ref, dfb_ref):
        xv = dct_ref[...].T
        row = lax.broadcasted_iota(jnp.int32, xv.shape, 0)
        s = 1
        while s < T:
            xv = xv + jnp.where(row + s < T, pltpu.roll(xv, T - s, 0), 0.0)
            s *= 2
        df = xv * jax.nn.sigmoid(-(pf_ref[...] + fb_ref[...]))
        dpf_ref[...] = df.astype(bf16)
        dfb_ref[...] = jnp.sum(df, axis=0, keepdims=True)

    return pl.pallas_call(body, name=name, out_shape=[S((T, HEAD), bf16), S((1, HEAD), f32)])(dct, pf, fb)


def _qk_norm(name, P, qg, kg, A):
    T = P.shape[0]
    tr = _tile(T, 256, 16)
    n_heads = A // HEAD

    def body(q_ref, k_ref, v_ref, qg_ref, kg_ref, qn_ref, kn_ref, vb_ref):
        for h in range(n_heads):
            sl = slice(h * HEAD, (h + 1) * HEAD)
            for src, g_ref, dst in ((q_ref, qg_ref, qn_ref), (k_ref, kg_ref, kn_ref)):
                xv = src[:, sl]
                r = lax.rsqrt(jnp.mean(xv * xv, axis=-1, keepdims=True) + EPS)
                dst[:, sl] = (xv * r * g_ref[...]).astype(bf16)
        vb_ref[...] = v_ref[...].astype(bf16)

    vec = BS((1, HEAD), lambda i: (0, 0))
    out = BS((tr, A), lambda i: (i, 0))
    return pl.pallas_call(
        body, name=name, grid=(T // tr,),
        in_specs=[BS((tr, A), lambda i: (i, 0)), BS((tr, A), lambda i: (i, 1)), BS((tr, A), lambda i: (i, 2)), vec, vec],
        out_specs=[out, out, out], out_shape=[S((T, A), bf16)] * 3,
    )(P, P, P, qg, kg)


def _attn_fwd(name, qn, kn, vb, c_col, c_row, tb):
    T, A = qn.shape
    H = A // HEAD
    nb = T // tb
    scale = HEAD ** -0.5
    hp = _heads_per_program(H)
    wide = hp * HEAD

    def body(q_ref, k_ref, v_ref, cq_ref, ck_ref, o_ref, o32_ref, lse_ref):
        i = pl.program_id(1)
        below = lax.broadcasted_iota(jnp.int32, (tb, tb), 0) >= lax.broadcasted_iota(jnp.int32, (tb, tb), 1)

        def qk(j, hh):
            sl = slice(hh * HEAD, (hh + 1) * HEAD)
            return _dot(q_ref[:, sl], k_ref[pl.ds(pl.multiple_of(j * tb, tb), tb), sl], "nt")

        def block(j, carry, diagonal):
            koff = pl.multiple_of(j * tb, tb)
            out = []
            for hh in range(hp):
                m, l, acc, qk_j = carry[hh]
                sl = slice(hh * HEAD, (hh + 1) * HEAD)
                qk_next = qk_j if diagonal else qk(j + 1, hh)
                s = qk_j * scale + (cq_ref[hh] - ck_ref[hh, j])
                if diagonal:
                    s = jnp.where(below, s, NEG)
                m_new = jnp.maximum(m, jnp.max(s, axis=-1, keepdims=True))
                alpha = jnp.exp(m - m_new)
                p = jnp.exp(s - m_new)
                l = l * alpha + jnp.sum(p, axis=-1, keepdims=True)
                acc = acc * alpha + _dot(p, v_ref[pl.ds(koff, tb), sl], "nn")
                out.append((m_new, l, acc, qk_next))
            return tuple(out)

        init = tuple((jnp.full((tb, 1), NEG, f32), jnp.zeros((tb, 1), f32), jnp.zeros((tb, HEAD), f32), qk(0, hh))
                     for hh in range(hp))
        carry = lax.fori_loop(0, i, lambda j, c: block(j, c, False), init)
        carry = block(i, carry, True)
        for hh in range(hp):
            m, l, acc, _ = carry[hh]
            sl = slice(hh * HEAD, (hh + 1) * HEAD)
            o = acc / l
            o_ref[:, sl] = o.astype(bf16)
            o32_ref[:, sl] = o
            lse_ref[hh] = m + jnp.log(l)

    return pl.pallas_call(
        body, name=name, grid=(H // hp, nb),
        in_specs=[BS((tb, wide), lambda h, i: (i, h)), BS((T, wide), lambda h, i: (0, h)),
                  BS((T, wide), lambda h, i: (0, h)), BS((hp, tb, 1), lambda h, i: (h, i, 0)),
                  BS((hp, nb, 1, tb), lambda h, i: (h, 0, 0, 0))],
        out_specs=[BS((tb, wide), lambda h, i: (i, h)), BS((tb, wide), lambda h, i: (i, h)),
                   BS((hp, tb, 1), lambda h, i: (h, i, 0))],
        out_shape=[S((T, A), bf16), S((T, A), f32), S((H, T, 1), f32)],
    )(qn, kn, vb, c_col, c_row)


def _attn_bwd(name, qn, kn, vb, o, dmix, lse, c_col, c_row, P, qg, kg, tb):
    T, A = qn.shape
    H = A // HEAD
    nb = T // tb
    scale = HEAD ** -0.5
    hp = _heads_per_program(H)
    wide = hp * HEAD

    def body(q_ref, k_ref, v_ref, o_ref, do_ref, lse_ref, cq_ref, ck_ref, qraw_ref, kraw_ref, qg_ref, kg_ref,
             dq_out, dk_out, dv_out, dc_out, dqg_out, dkg_out, dq_acc, dk_acc, delta_s):
        h = pl.program_id(0)
        dq_acc[...] = jnp.zeros_like(dq_acc)
        below = lax.broadcasted_iota(jnp.int32, (tb, tb), 0) >= lax.broadcasted_iota(jnp.int32, (tb, tb), 1)
        for hh in range(hp):
            sl = slice(hh * HEAD, (hh + 1) * HEAD)
            delta_s[hh] = jnp.sum(do_ref[:, sl].astype(bf16).astype(f32) * o_ref[:, sl], axis=-1, keepdims=True)

        def kblock(j, _):
            koff = pl.multiple_of(j * tb, tb)

            def products(i, hh):
                sl = slice(hh * HEAD, (hh + 1) * HEAD)
                qoff = pl.multiple_of(i * tb, tb)
                return (_dot(q_ref[pl.ds(qoff, tb), sl], k_ref[pl.ds(koff, tb), sl], "nt"),
                        _dot(do_ref[pl.ds(qoff, tb), sl], v_ref[pl.ds(koff, tb), sl], "nt"))

            def qblock(i, carry, diagonal):
                qoff = pl.multiple_of(i * tb, tb)
                out = []
                for hh in range(hp):
                    dk, dv, dc, qk_i, dp = carry[hh]
                    sl = slice(hh * HEAD, (hh + 1) * HEAD)
                    ahead = products(jnp.minimum(i + 1, nb - 1), hh)
                    k = k_ref[pl.ds(koff, tb), sl]
                    q = q_ref[pl.ds(qoff, tb), sl]
                    do = do_ref[pl.ds(qoff, tb), sl].astype(bf16)
                    s = qk_i * scale + (cq_ref[hh, pl.ds(qoff, tb), :] - ck_ref[hh, j])
                    if diagonal:
                        s = jnp.where(below, s, NEG)
                    p = jnp.exp(s - lse_ref[hh, pl.ds(qoff, tb), :])
                    dv = dv + _dot(p, do, "tn")
                    ds = p * (dp - delta_s[hh, pl.ds(qoff, tb), :])
                    dc = dc - jnp.sum(ds, axis=0, keepdims=True)
                    dsb = (ds * scale).astype(bf16)
                    dk = dk + _dot(dsb, q, "tn")
                    dq_acc[pl.ds(qoff, tb), sl] += _dot(dsb, k, "nn")
                    out.append((dk, dv, dc, *ahead))
                return tuple(out)

            init = tuple((jnp.zeros((tb, HEAD), f32), jnp.zeros((tb, HEAD), f32), jnp.zeros((1, tb), f32),
                          *products(j, hh)) for hh in range(hp))
            carry = qblock(j, init, True)
            carry = lax.fori_loop(j + 1, nb, lambda i, c: qblock(i, c, False), carry)
            for hh in range(hp):
                dk, dv, dc = carry[hh][:3]
                sl = slice(hh * HEAD, (hh + 1) * HEAD)
                dk_acc[pl.ds(koff, tb), sl] = dk
                dv_out[pl.ds(koff, tb), sl] = dv.astype(bf16)
                dc_out[hh, j] = dc
            return 0

        lax.fori_loop(0, nb, kblock, 0)

        for raw_ref, g_ref, acc_ref, d_out, dg_out in ((qraw_ref, qg_ref, dq_acc, dq_out, dqg_out),
                                                       (kraw_ref, kg_ref, dk_acc, dk_out, dkg_out)):
            part = jnp.zeros((1, HEAD), f32)
            for hh in range(hp):
                sl = slice(hh * HEAD, (hh + 1) * HEAD)
                xv = raw_ref[:, sl]
                r = lax.rsqrt(jnp.mean(xv * xv, axis=-1, keepdims=True) + EPS)
                xh = xv * r
                dn = acc_ref[:, sl]
                dxh = dn * g_ref[...]
                d_out[:, sl] = (r * (dxh - xh * jnp.mean(dxh * xh, axis=-1, keepdims=True))).astype(bf16)
                part = part + jnp.sum(dn * xh, axis=0, keepdims=True)

            @pl.when(h == 0)
            def _():
                dg_out[...] = part

            @pl.when(h > 0)
            def _():
                dg_out[...] += part

    heads = lambda off: BS((T, wide), lambda h: (0, off + h))
    col = BS((hp, T, 1), lambda h: (h, 0, 0))
    row = BS((hp, nb, 1, tb), lambda h: (h, 0, 0, 0))
    vec = BS((1, HEAD), lambda h: (0, 0))
    return pl.pallas_call(
        body, name=name, grid=(H // hp,),
        in_specs=[heads(0), heads(0), heads(0), heads(0), heads(0), col, col, row, heads(0), heads(H // hp), vec, vec],
        out_specs=[heads(0), heads(0), heads(0), row, vec, vec],
        out_shape=[S((T, A), bf16)] * 3 + [S((H, nb, 1, tb), f32), S((1, HEAD), f32), S((1, HEAD), f32)],
        scratch_shapes=[pltpu.VMEM((T, wide), f32), pltpu.VMEM((T, wide), f32), pltpu.VMEM((hp, T, 1), f32)],
        compiler_params=_params(),
    )(qn, kn, vb, o, dmix, lse, c_col, c_row, P, P, qg, kg)


def _gmlp_fwd(name, P, gain, ws, b, col_u, col_v, Wd):
    T = P.shape[0]
    G = Wd // HEAD
    tr = _tile(T, 512, HEAD)

    def body(u_ref, v_ref, gain_ref, ws_ref, b_ref, y_ref):
        tril = lax.broadcasted_iota(jnp.int32, (HEAD, HEAD), 0) >= lax.broadcasted_iota(jnp.int32, (HEAD, HEAD), 1)
        wm = jnp.where(tril, ws_ref[...], 0.0).astype(bf16)
        for n in range(tr // HEAD):
            rows = slice(n * HEAD, (n + 1) * HEAD)
            u = jax.nn.gelu(u_ref[rows, :])
            a = jax.nn.gelu(v_ref[rows, :])
            r = lax.rsqrt(jnp.mean(a * a, axis=-1, keepdims=True) + EPS)
            vn = a * r * gain_ref[...]
            mixed = _dot(wm, vn, "nn") + b_ref[...]
            y_ref[rows, :] = (u * mixed).astype(bf16)

    return pl.pallas_call(
        body, name=name, grid=(G, T // tr),
        in_specs=[BS((tr, HEAD), lambda g, i: (i, col_u + g)), BS((tr, HEAD), lambda g, i: (i, col_v + g)),
                  BS((None, 1, HEAD), lambda g, i: (g, 0, 0)), BS((None, HEAD, HEAD), lambda g, i: (g, 0, 0)),
                  BS((None, HEAD, 1), lambda g, i: (g, 0, 0))],
        out_specs=BS((tr, HEAD), lambda g, i: (i, g)), out_shape=S((T, Wd), bf16),
    )(P, P, gain, ws, b)


def _gmlp_bwd(name, P, dmix, gain, ws, b, col_u, col_v, col_dy, Wd):
    T = P.shape[0]
    G = Wd // HEAD
    tr = _tile(T, 512, HEAD)

    def body(u_ref, v_ref, dy_ref, gain_ref, ws_ref, b_ref, du_ref, dv_ref, dws_ref, db_ref, dgain_ref):
        i = pl.program_id(1)
        tril = lax.broadcasted_iota(jnp.int32, (HEAD, HEAD), 0) >= lax.broadcasted_iota(jnp.int32, (HEAD, HEAD), 1)
        wm = jnp.where(tril, ws_ref[...], 0.0).astype(bf16)
        gain_v = gain_ref[...]
        dw = jnp.zeros((HEAD, HEAD), f32)
        db = jnp.zeros((HEAD, 1), f32)
        dgain = jnp.zeros((1, HEAD), f32)
        for n in range(tr // HEAD):
            rows = slice(n * HEAD, (n + 1) * HEAD)
            u, du_dx = _gelu_and_grad(u_ref[rows, :])
            a, da_dx = _gelu_and_grad(v_ref[rows, :])
            dy = dy_ref[rows, :]
            r = lax.rsqrt(jnp.mean(a * a, axis=-1, keepdims=True) + EPS)
            ah = a * r
            vnb = (ah * gain_v).astype(bf16)
            mixed = _dot(wm, vnb, "nn") + b_ref[...]
            dm = dy * u
            dmb = dm.astype(bf16)
            du_ref[rows, :] = (dy * mixed * du_dx).astype(bf16)
            db = db + jnp.sum(dm, axis=1, keepdims=True)
            dw = dw + _dot(dmb, vnb, "nt")
            dvn = _dot(wm, dmb, "tn")
            dgain = dgain + jnp.sum(dvn * ah, axis=0, keepdims=True)
            dah = dvn * gain_v
            da = r * (dah - ah * jnp.mean(dah * ah, axis=-1, keepdims=True))
            dv_ref[rows, :] = (da * da_dx).astype(bf16)
        dw = jnp.where(tril, dw, 0.0)

        @pl.when(i == 0)
        def _():
            dws_ref[...] = dw
            db_ref[...] = db
            dgain_ref[...] = dgain

        @pl.when(i > 0)
        def _():
            dws_ref[...] += dw
            db_ref[...] += db
            dgain_ref[...] += dgain

    out = BS((tr, HEAD), lambda g, i: (i, g))
    return pl.pallas_call(
        body, name=name, grid=(G, T // tr),
        in_specs=[BS((tr, HEAD), lambda g, i: (i, col_u + g)), BS((tr, HEAD), lambda g, i: (i, col_v + g)),
                  BS((tr, HEAD), lambda g, i: (i, col_dy + g)),
                  BS((None, 1, HEAD), lambda g, i: (g, 0, 0)), BS((None, HEAD, HEAD), lambda g, i: (g, 0, 0)),
                  BS((None, HEAD, 1), lambda g, i: (g, 0, 0))],
        out_specs=[out, out, BS((None, HEAD, HEAD), lambda g, i: (g, 0, 0)), BS((None, HEAD, 1), lambda g, i: (g, 0, 0)),
                   BS((None, 1, HEAD), lambda g, i: (g, 0, 0))],
        out_shape=[S((T, Wd), bf16), S((T, Wd), bf16), S((G, HEAD, HEAD), f32), S((G, HEAD, 1), f32),
                   S((G, 1, HEAD), f32)],
    )(P, P, dmix, gain, ws, b)


def _trailing_window(xv, w, row):
    k = 1
    while k < w:
        xv = xv + jnp.where(row >= k, pltpu.roll(xv, k, 0), 0.0)
        k *= 2
    return xv


def _leading_window(xv, w, row, T):
    k = 1
    while k < w:
        xv = xv + jnp.where(row + k < T, pltpu.roll(xv, T - k, 0), 0.0)
        k *= 2
    return xv


def _pool_fwd(name, P, pw, ps, col_x, Wd):
    T = P.shape[0]
    Gp = Wd // HEAD

    def body(x_ref, pw_ref, ps_ref, y_ref):
        row = lax.broadcasted_iota(jnp.int32, (T, HEAD), 0)
        for g in range(Gp):
            w = POOL_WINDOWS[g]
            sl = slice(g * HEAD, (g + 1) * HEAD)
            xv = x_ref[:, sl]
            cnt = jnp.minimum(row + 1, w).astype(f32)
            d = _trailing_window(xv, w, row) / cnt - xv
            y_ref[:, sl] = (_dot(d, pw_ref[g], "nn") * ps_ref[:, sl]).astype(bf16)

    return pl.pallas_call(
        body, name=name, grid=(1,),
        in_specs=[BS((T, Wd), lambda i: (0, col_x)), BS((Gp, HEAD, HEAD), lambda i: (0, 0, 0)), BS((1, Wd), lambda i: (0, 0))],
        out_specs=BS((T, Wd), lambda i: (0, 0)), out_shape=S((T, Wd), bf16), compiler_params=_params(),
    )(P, pw, ps)


def _pool_bwd(name, P, dmix, pw, ps, col_x, col_dy, Wd):
    T = P.shape[0]
    Gp = Wd // HEAD

    def body(x_ref, dy_ref, pw_ref, ps_ref, dx_ref, dpw_ref, dps_ref):
        row = lax.broadcasted_iota(jnp.int32, (T, HEAD), 0)
        for g in range(Gp):
            w = POOL_WINDOWS[g]
            sl = slice(g * HEAD, (g + 1) * HEAD)
            xv = x_ref[:, sl]
            cnt = jnp.minimum(row + 1, w).astype(f32)
            d = (_trailing_window(xv, w, row) / cnt - xv).astype(bf16)
            pwb = pw_ref[g].astype(bf16)
            z = _dot(d, pwb, "nn")
            dy = dy_ref[:, sl]
            dps_ref[:, sl] = jnp.sum(dy * z, axis=0, keepdims=True)
            dzb = (dy * ps_ref[:, sl]).astype(bf16)
            dpw_ref[g] = _dot(d, dzb, "tn")
            dd = _dot(dzb, pwb, "nt")
            dx_ref[:, sl] = (_leading_window(dd / cnt, w, row, T) - dd).astype(bf16)

    return pl.pallas_call(
        body, name=name, grid=(1,),
        in_specs=[BS((T, Wd), lambda i: (0, col_x)), BS((T, Wd), lambda i: (0, col_dy)),
                  BS((Gp, HEAD, HEAD), lambda i: (0, 0, 0)), BS((1, Wd), lambda i: (0, 0))],
        out_specs=[BS((T, Wd), lambda i: (0, 0)), BS((Gp, HEAD, HEAD), lambda i: (0, 0, 0)), BS((1, Wd), lambda i: (0, 0))],
        out_shape=[S((T, Wd), bf16), S((Gp, HEAD, HEAD), f32), S((1, Wd), f32)], compiler_params=_params(),
    )(P, dmix, pw, ps)


def _adamw(name, w, g, m, v):
    R, C = w.shape
    lanes = -(-C // 128) * 128
    tr = _tile(R, max(8, (512 * 1024) // lanes // 8 * 8), 8)
    c1 = 1.0 - ADAM_B1 ** ADAM_STEP
    c2 = 1.0 - ADAM_B2 ** ADAM_STEP

    def body(w_ref, g_ref, m_ref, v_ref, d_ref, nm_ref, nv_ref):
        gv = g_ref[...]
        nm = ADAM_B1 * m_ref[...] + (1.0 - ADAM_B1) * gv
        nv = ADAM_B2 * v_ref[...] + (1.0 - ADAM_B2) * (gv * gv)
        d_ref[...] = -ADAM_LR * ((nm / c1) / (jnp.sqrt(nv / c2) + ADAM_EPS) + ADAM_WD * w_ref[...])
        nm_ref[...] = nm
        nv_ref[...] = nv

    blk = BS((tr, C), lambda i: (i, 0))
    return pl.pallas_call(
        body, name=name, grid=(R // tr,), in_specs=[blk] * 4, out_specs=[blk] * 3, out_shape=[S((R, C), f32)] * 3,
    )(w, g, m, v)


def _adamw_layer(name, layer, w_all, m_all, v_all, g, prev):
    L, R, C = w_all.shape
    lanes = -(-C // 128) * 128
    tr = _tile(R, max(8, (512 * 1024) // lanes // 8 * 8), 8)
    c1 = 1.0 - ADAM_B1 ** ADAM_STEP
    c2 = 1.0 - ADAM_B2 ** ADAM_STEP
    n_prev = 0 if prev is None else 4

    def body(w_ref, m_ref, v_ref, g_ref, *rest):
        go_ref, d_ref, nm_ref, nv_ref = rest[n_prev:]
        gv = g_ref[...]
        nm = ADAM_B1 * m_ref[...] + (1.0 - ADAM_B1) * gv
        nv = ADAM_B2 * v_ref[...] + (1.0 - ADAM_B2) * (gv * gv)
        d_ref[...] = -ADAM_LR * ((nm / c1) / (jnp.sqrt(nv / c2) + ADAM_EPS) + ADAM_WD * w_ref[...])
        nm_ref[...] = nm
        nv_ref[...] = nv
        go_ref[...] = gv

    slab = BS((None, tr, C), lambda r: (layer, r, 0))
    return pl.pallas_call(
        body, name=name, grid=(R // tr,),
        in_specs=[slab, slab, slab, BS((tr, C), lambda r: (r, 0))] + [ANY] * n_prev,
        out_specs=[slab] * 4, out_shape=[S((L, R, C), f32)] * 4,
        input_output_aliases={4 + k: k for k in range(n_prev)},
    )(w_all, m_all, v_all, g, *(prev or ()))


def _chip_of(k):
    return k // 2, k % 2


def _remote(src, dst, send_sems, recv_sems, idx, dev):
    return pltpu.make_async_remote_copy(src_ref=src, dst_ref=dst, send_sem=send_sems.at[idx], recv_sem=recv_sems.at[idx],
                                        device_id=dev, device_id_type=MESH)


def _plan_gather_chips(n):
    def plan(refs, ss, rs, base):
        ins, lands = refs[:n], refs[n:]
        x, y, c, j0 = _my_place()
        sib = (x, y, 1 - c)
        sends, recvs = [], []
        for a in range(n):
            half = ins[a].shape[0] // 2
            lo = c * half
            sends.append(_remote(ins[a], lands[a].at[j0], ss, rs, base + 4 * a + 3, sib))
            recvs.append(_remote(lands[a].at[j0], lands[a].at[j0], ss, rs, base + 4 * a + 3, sib))
            for r in (1, 2, 3):
                k = j0 ^ r
                dev = (*_chip_of(k), c)
                sends.append(_remote(ins[a].at[pl.ds(lo, half)], lands[a].at[j0, pl.ds(lo, half)], ss, rs,
                                     base + 4 * a + r - 1, dev))
                landed = lands[a].at[k, pl.ds(lo, half)]
                recvs.append(_remote(landed, landed, ss, rs, base + 4 * a + r - 1, dev))
        return sends, recvs
    return plan, 4 * n


def _plan_gather_forward(n):
    def plan(refs, ss, rs, base):
        x, y, c, j0 = _my_place()
        sib = (x, y, 1 - c)
        sends, recvs = [], []
        for a in range(n):
            half = refs[a].shape[1] // 2
            for r in (1, 2, 3):
                k = j0 ^ r
                landed = refs[a].at[k, pl.ds(c * half, half)]
                other = refs[a].at[k, pl.ds((1 - c) * half, half)]
                sends.append(_remote(landed, landed, ss, rs, base + 3 * a + r - 1, sib))
                recvs.append(_remote(other, other, ss, rs, base + 3 * a + r - 1, sib))
        return sends, recvs
    return plan, 3 * n


def _plan_sibling_halves(n):
    def plan(refs, ss, rs, base):
        ins, lands = refs[:n], refs[n:]
        x, y, c, _ = _my_place()
        sib = (x, y, 1 - c)
        sends, recvs = [], []
        for a in range(n):
            half = ins[a].shape[1] // 2
            sends.append(_remote(ins[a].at[:, pl.ds((1 - c) * half, half), :], lands[a], ss, rs, base + a, sib))
            recvs.append(_remote(lands[a], lands[a], ss, rs, base + a, sib))
        return sends, recvs
    return plan, n


def _plan_chip_scatter(n):
    def plan(refs, ss, rs, base):
        ins, lands = refs[:n], refs[n:]
        x, y, c, j0 = _my_place()
        sends, recvs = [], []
        for a in range(n):
            for r in (1, 2, 3):
                k = j0 ^ r
                dev = (*_chip_of(k), c)
                sends.append(_remote(ins[a].at[k], lands[a].at[j0], ss, rs, base + 3 * a + r - 1, dev))
                recvs.append(_remote(lands[a].at[k], lands[a].at[k], ss, rs, base + 3 * a + r - 1, dev))
        return sends, recvs
    return plan, 3 * n


def _plan_sibling_join(n):
    def plan(refs, ss, rs, base):
        x, y, c, _ = _my_place()
        sib = (x, y, 1 - c)
        sends, recvs = [], []
        for a in range(n):
            half = refs[a].shape[0] // 2
            mine = refs[a].at[pl.ds(c * half, half)]
            theirs = refs[a].at[pl.ds((1 - c) * half, half)]
            sends.append(_remote(mine, mine, ss, rs, base + a, sib))
            recvs.append(_remote(theirs, theirs, ss, rs, base + a, sib))
        return sends, recvs
    return plan, n


_HBM = pl.BlockSpec(memory_space=pltpu.HBM)
_SEM = pl.BlockSpec(memory_space=pltpu.SEMAPHORE)
_EFFECT = pltpu.SideEffectType.DATAFLOW_SIDE_EFFECTING


def _exchange_start(name, plan, bufs, after):
    plan_fn, n_sems = plan
    n = len(bufs)

    def body(*refs):
        ss, rs, token = refs[n + len(after)], refs[n + len(after) + 1], refs[-1]
        sends, _ = plan_fn(refs[:n], ss, rs, 0)
        for cp in sends:
            cp.start()
        token[...] = jnp.zeros_like(token)

    res = pl.pallas_call(
        body, name=name,
        out_shape=(pltpu.SemaphoreType.DMA((n_sems,)), pltpu.SemaphoreType.DMA((n_sems,)),
                   *[pltpu.HBM(b.shape, b.dtype) for b in bufs], S((8, 128), f32)),
        in_specs=[_HBM] * n + [ANY] * len(after),
        out_specs=(_SEM, _SEM, *[_HBM] * n, pl.BlockSpec(memory_space=pltpu.VMEM)),
        input_output_aliases={k: 2 + k for k in range(n)},
        compiler_params=pltpu.CompilerParams(has_side_effects=_EFFECT),
    )(*[pltpu.with_memory_space_constraint(b, pltpu.HBM) for b in bufs], *after)
    return res[0], res[1], list(res[2:2 + n]), res[-1]


def _exchange_wait(name, plan, send_sems, recv_sems, bufs, after):
    plan_fn, _ = plan
    n = len(bufs)

    def body(*refs):
        ss, rs, token = refs[n], refs[n + 1], refs[-1]
        sends, recvs = plan_fn(refs[:n], ss, rs, 0)
        for cp in recvs:
            cp.wait_recv()
        for cp in sends:
            cp.wait_send()
        token[...] = jnp.zeros_like(token)

    res = pl.pallas_call(
        body, name=name,
        out_shape=(*[pltpu.HBM(b.shape, b.dtype) for b in bufs], S((8, 128), f32)),
        in_specs=[_HBM] * n + [_SEM, _SEM] + [ANY] * len(after),
        out_specs=(*[_HBM] * n, pl.BlockSpec(memory_space=pltpu.VMEM)),
        input_output_aliases={k: k for k in range(n)},
        compiler_params=pltpu.CompilerParams(has_side_effects=_EFFECT),
    )(*bufs, send_sems, recv_sems, *after)
    return list(res[:n]), res[-1]


class _Order:
    def __init__(self, first):
        self.marker = first
        self.token = None

    def _after(self):
        return [self.marker] + ([] if self.token is None else [self.token])

    def start(self, name, plan, bufs):
        ss, rs, thru, self.token = _exchange_start(name, plan, bufs, self._after())
        return name, plan, ss, rs, thru

    def wait(self, handle):
        name, plan, ss, rs, thru = handle
        out, self.token = _exchange_wait(name + "_wait", plan, ss, rs, thru, self._after())
        return out

    def follows(self, small):
        return small if self.token is None else small + self.token[0, 0]

    def done(self, result):
        self.marker = result[(slice(0, 1),) * result.ndim].reshape(1, 1)


def _all_reduce_small(name, g8):
    _, R, L = g8.shape

    def body(g_ref, out_ref, land, red, send1, recv1, send2, recv2):
        x, y, c, _ = _my_place()
        me = 4 * x + 2 * y + c
        peers = []
        for r in range(1, N_DEV):
            q = me ^ r
            peers.append((q, (q // 4, (q // 2) % 2, q % 2)))
        first = []
        for r, (q, dev) in enumerate(peers):
            cp = pltpu.make_async_remote_copy(src_ref=g_ref.at[q], dst_ref=land.at[me], send_sem=send1.at[r],
                                              recv_sem=recv1.at[r], device_id=dev, device_id_type=MESH)
            cp.start()
            first.append(cp)
        land[me] = g_ref[me]
        for r, (q, dev) in enumerate(peers):
            pltpu.make_async_remote_copy(src_ref=land.at[q], dst_ref=land.at[q], send_sem=send1.at[r],
                                         recv_sem=recv1.at[r], device_id=dev, device_id_type=MESH).wait_recv()
        acc = land[0]
        for d in range(1, N_DEV):
            acc = acc + land[d]
        red[...] = acc
        out_ref[me] = acc
        second = []
        for r, (q, dev) in enumerate(peers):
            cp = pltpu.make_async_remote_copy(src_ref=red, dst_ref=out_ref.at[me], send_sem=send2.at[r],
                                              recv_sem=recv2.at[r], device_id=dev, device_id_type=MESH)
            cp.start()
            second.append(cp)
        for r, (q, dev) in enumerate(peers):
            pltpu.make_async_remote_copy(src_ref=out_ref.at[q], dst_ref=out_ref.at[q], send_sem=send2.at[r],
                                         recv_sem=recv2.at[r], device_id=dev, device_id_type=MESH).wait_recv()
        for cp in first + second:
            cp.wait_send()

    vm = pl.BlockSpec(memory_space=pltpu.VMEM)
    return pl.pallas_call(
        body, name=name, in_specs=[vm], out_specs=vm, out_shape=S(g8.shape, f32),
        scratch_shapes=[pltpu.VMEM((N_DEV, R, L), f32), pltpu.VMEM((R, L), f32)]
        + [pltpu.SemaphoreType.DMA((N_DEV - 1,))] * 4,
        compiler_params=_params(),
    )(g8)


def _pair_sum(name, g4, sib):
    _, rows, cols = g4.shape
    half = rows // 2
    lanes = -(-cols // 128) * 128
    tr = _tile(half, max(16, (512 * 1024) // lanes // 16 * 16), 16)
    nb = half // tr

    def body(g_ref, s_ref, pb_ref, own_ref):
        j = pl.program_id(1)
        t = g_ref[...] + s_ref[...]
        pb_ref[...] = t.astype(bf16)

        @pl.when(j == _my_place()[3])
        def _():
            own_ref[...] = t

    return pl.pallas_call(
        body, name=name, grid=(nb, N_CHIPS),
        in_specs=[BS((None, tr, cols), lambda i, j: (j, lax.axis_index("c") * nb + i, 0)),
                  BS((None, tr, cols), lambda i, j: (j, i, 0))],
        out_specs=[BS((None, tr, cols), lambda i, j: (j, i, 0)), BS((tr, cols), lambda i, j: (i, 0))],
        out_shape=[S((N_CHIPS, half, cols), bf16), S((half, cols), f32)],
    )(g4, sib)


def _chip_sum(name, own, got):
    half, cols = own.shape
    lanes = -(-cols // 128) * 128
    tr = _tile(half, max(16, (512 * 1024) // lanes // 16 * 16), 16)
    nb = half // tr

    def body(own_ref, *rest):
        got_refs, o_ref = rest[:N_CHIPS], rest[N_CHIPS]
        j0 = _my_place()[3]
        acc = None
        for k in range(N_CHIPS):
            t = jnp.where(j0 == k, own_ref[...], got_refs[k][...].astype(f32))
            acc = t if acc is None else acc + t
        o_ref[...] = acc

    def slot(k):
        return BS((None, tr, cols), lambda i: (jnp.where(_my_place()[3] == k, (k + 1) % N_CHIPS, k), i, 0))

    return pl.pallas_call(
        body, name=name, grid=(nb,),
        in_specs=[BS((tr, cols), lambda i: (i, 0))] + [slot(k) for k in range(N_CHIPS)],
        out_specs=BS((tr, cols), lambda i: (lax.axis_index("c") * nb + i, 0)),
        out_shape=S((2 * half, cols), f32),
    )(own, got, got, got, got)


def _rows_of(size):
    return -(-size // 1024) * 8


def _pack_rows(arrs, n_rows):
    parts = []
    for a in arrs:
        rows = _rows_of(a.size)
        if a.size % 128 == 0:
            part = a.astype(f32).reshape(-1, 128)
            part = jnp.pad(part, ((0, rows - part.shape[0]), (0, 0)))
        else:
            part = jnp.pad(a.reshape(-1).astype(f32), (0, rows * 128 - a.size)).reshape(rows, 128)
        parts.append(part)
    used = sum(p.shape[0] for p in parts)
    return jnp.concatenate(parts + [jnp.zeros((n_rows - used, 128), f32)], axis=0)


def _unpack_rows(packed, shapes):
    out, row = [], 0
    for shp in shapes:
        size = 1
        for d in shp:
            size *= d
        rows = packed[row:row + _rows_of(size)]
        out.append(rows[:size // 128].reshape(shp) if size % 128 == 0 else rows.reshape(-1)[:size].reshape(shp))
        row += _rows_of(size)
    return out


def kernel(x, p, norm_mix, w_in, q_norm, k_norm, forget_bias, gmlp_v_norm, gmlp_w_s, gmlp_b_s, pool_w, pool_scale, w_out, norm_ffn, w_ffn_gate, w_ffn_up, w_ffn_down, norm_ple, w_ple_gate, w_ple_proj, loss_target, m_norm_mix, m_w_in, m_q_norm, m_k_norm, m_forget_bias, m_gmlp_v_norm, m_gmlp_w_s, m_gmlp_b_s, m_pool_w, m_pool_scale, m_w_out, m_norm_ffn, m_w_ffn_gate, m_w_ffn_up, m_w_ffn_down, m_norm_ple, m_w_ple_gate, m_w_ple_proj, v_norm_mix, v_w_in, v_q_norm, v_k_norm, v_forget_bias, v_gmlp_v_norm, v_gmlp_w_s, v_gmlp_b_s, v_pool_w, v_pool_scale, v_w_out, v_norm_ffn, v_w_ffn_gate, v_w_ffn_up, v_w_ffn_down, v_norm_ple, v_w_ple_gate, v_w_ple_proj):
    W = dict(norm_mix=norm_mix, w_in=w_in, q_norm=q_norm, k_norm=k_norm, forget_bias=forget_bias,
             gmlp_v_norm=gmlp_v_norm, gmlp_w_s=gmlp_w_s, gmlp_b_s=gmlp_b_s, pool_w=pool_w, pool_scale=pool_scale,
             w_out=w_out, norm_ffn=norm_ffn, w_ffn_gate=w_ffn_gate, w_ffn_up=w_ffn_up, w_ffn_down=w_ffn_down,
             norm_ple=norm_ple, w_ple_gate=w_ple_gate, w_ple_proj=w_ple_proj)
    M = dict(norm_mix=m_norm_mix, w_in=m_w_in, q_norm=m_q_norm, k_norm=m_k_norm, forget_bias=m_forget_bias,
             gmlp_v_norm=m_gmlp_v_norm, gmlp_w_s=m_gmlp_w_s, gmlp_b_s=m_gmlp_b_s, pool_w=m_pool_w,
             pool_scale=m_pool_scale, w_out=m_w_out, norm_ffn=m_norm_ffn, w_ffn_gate=m_w_ffn_gate,
             w_ffn_up=m_w_ffn_up, w_ffn_down=m_w_ffn_down, norm_ple=m_norm_ple, w_ple_gate=m_w_ple_gate,
             w_ple_proj=m_w_ple_proj)
    V = dict(norm_mix=v_norm_mix, w_in=v_w_in, q_norm=v_q_norm, k_norm=v_k_norm, forget_bias=v_forget_bias,
             gmlp_v_norm=v_gmlp_v_norm, gmlp_w_s=v_gmlp_w_s, gmlp_b_s=v_gmlp_b_s, pool_w=v_pool_w,
             pool_scale=v_pool_scale, w_out=v_w_out, norm_ffn=v_norm_ffn, w_ffn_gate=v_w_ffn_gate,
             w_ffn_up=v_w_ffn_up, w_ffn_down=v_w_ffn_down, norm_ple=v_norm_ple, w_ple_gate=v_w_ple_gate,
             w_ple_proj=v_w_ple_proj)

    L = w_in.shape[0]
    _, T, D = x.shape
    A, Wd = D // 2, D // 4
    H = A // HEAD
    G = gmlp_w_s.shape[1]
    Gp = pool_w.shape[1]
    DP4 = w_in.shape[2]
    DP = N_CHIPS * DP4
    NM = 3 * A + 3 * Wd
    FS = w_ffn_gate.shape[2]
    FF = N_CHIPS * FS
    DS = D // N_CHIPS
    PL = p.shape[-1]
    assert Wd // G == HEAD and Wd // Gp == HEAD and DP == NM + H and H <= HEAD
    assert all(w & (w - 1) == 0 for w in POOL_WINDOWS[:Gp])
    tb = _tile(T, 256, HEAD)
    nb = T // tb
    tm = _tile(T, 512, 16)
    tmw = _tile(T, 1024, 16)
    tn = _tile(NM, 512, 128)
    tnd = _tile(D, 512, 128)
    tkd = _tile(D, 1024, 128)
    tnw = _tile(D, 1024, 128)
    col_gu, col_gv, col_xp = 3 * A // HEAD, (3 * A + Wd) // HEAD, (3 * A + 2 * Wd) // Wd
    col_dg, col_dp = A // HEAD, (A + Wd) // Wd

    order = _Order(x[0, :1, :1])

    def gather_start(i, names, tag=""):
        shards = [_cast_layer(f"cast_{n}_{i}", W[n], i) for n in names]
        lands = [lax.empty((N_CHIPS,) + s.shape, bf16) for s in shards]
        return names, order.start(f"ag_ici_{i}{tag}", _plan_gather_chips(len(names)), shards + lands)

    def gather_forward(i, started, tag=""):
        names, handle = started
        return names, order.start(f"ag_d2d_{i}{tag}", _plan_gather_forward(len(names)), order.wait(handle)[len(names):])

    def gathered(forwarded):
        names, handle = forwarded
        g = dict(zip(names, order.wait(handle)))
        out = {}
        if "w_in" in g:
            def cols(lo, hi):
                parts = []
                for j in range(N_CHIPS):
                    s, e = max(lo, j * DP4), min(hi, (j + 1) * DP4)
                    if s < e:
                        parts.append(g["w_in"][j][:, s - j * DP4:e - j * DP4])
                return parts

            out["w_main"] = jnp.concatenate(cols(0, 3 * A) + cols(3 * A + H, DP), axis=1)
            out["w_f"] = jnp.pad(jnp.concatenate(cols(3 * A, 3 * A + H), axis=1), ((0, 0), (0, HEAD - H)))
        if "w_out" in g:
            out["w_out"] = g["w_out"].reshape(D, D)
        if "w_ffn_gate" in g:
            out.update(w_gate=g["w_ffn_gate"], w_up=g["w_ffn_up"], w_down=g["w_ffn_down"].reshape(FF, D),
                       w_pg=g["w_ple_gate"].reshape(D, D), w_pp=g["w_ple_proj"])
        return out

    Wf = [None] * L
    first = [gather_start(0, ["w_in"], "a"), gather_start(0, ["w_out"], "b"), gather_start(0, BIG[2:], "c")]
    d2d = gather_forward(0, first[0], "a")
    ici = gather_start(1, BIG) if L > 1 else None
    Wf[0] = gathered(d2d)

    h = x.reshape(T, D)
    pb16 = p.reshape(L, T, PL).astype(bf16)
    saved = []

    for i in range(L):
        w = Wf[i]
        sv = dict(h0=h)
        xn1 = _rms_fwd(f"rms1_{i}", h, order.follows(norm_mix[i]))
        (P,) = _matmul(f"proj_{i}", "nn", (T // tmw, NM // tn),
                       [((xn1, BS((tmw, D), lambda i, j: (i, 0))), (w["w_main"], BS((D, tn), lambda i, j: (0, j))))], [],
                       [((T, NM), f32, BS((tmw, tn), lambda i, j: (i, j)))])
        (Pf,) = _matmul(f"projf_{i}", "nn", (T // tm, 1),
                        [((xn1, BS((tm, D), lambda i, j: (i, 0))), (w["w_f"], BS((D, HEAD), lambda i, j: (0, 0))))], [],
                        [((T, HEAD), f32, BS((tm, HEAD), lambda i, j: (i, 0)))])
        fb = jnp.pad(forget_bias[i], (0, HEAD - H)).reshape(1, HEAD)
        cc, ct = _fgate_fwd(f"fgate_{i}", Pf, fb)
        c_col = ct[:H].reshape(H, T, 1)
        c_row = ct[:H].reshape(H, nb, 1, tb)
        qg, kg = q_norm[i].reshape(1, HEAD), k_norm[i].reshape(1, HEAD)
        qn, kn, vb = _qk_norm(f"qknorm_{i}", P, qg, kg, A)
        y_attn, o32, lse = _attn_fwd(f"attn_{i}", qn, kn, vb, c_col, c_row, tb)
        gain = gmlp_v_norm[i].reshape(G, 1, HEAD)
        bs = gmlp_b_s[i].reshape(G, HEAD, 1)
        y_gmlp = _gmlp_fwd(f"gmlp_{i}", P, gain, gmlp_w_s[i], bs, col_gu, col_gv, Wd)
        ps = pool_scale[i].reshape(1, Wd)
        y_pool = _pool_fwd(f"pool_{i}", P, pool_w[i], ps, col_xp, Wd)
        mix = jnp.concatenate([y_attn, y_gmlp, y_pool], axis=1)
        if i == 0:
            order.done(mix)
            w.update(gathered(gather_forward(0, first[1], "b")))
        (h1,) = _matmul(f"out_{i}", "nn", (T // tmw, D // tnd),
                        [((mix, BS((tmw, D), lambda i, j: (i, 0))), (w["w_out"], BS((D, tnd), lambda i, j: (0, j))))],
                        [(h, BS((tmw, tnd), lambda i, j: (i, j)))],
                        [((T, D), f32, BS((tmw, tnd), lambda i, j: (i, j)))],
                        epilogue=lambda accs, ex: (accs[0] + ex[0],), after=order.token)
        xn2 = _rms_fwd(f"rms2_{i}", h1, norm_ffn[i])
        if i == 0:
            order.done(xn2)
            w.update(gathered(gather_forward(0, first[2], "c")))

        def ffn_epi(accs, ex):
            g_, u_ = accs
            return g_, u_, g_ * jax.nn.sigmoid(g_) * u_

        ffo = BS((tm, FS), lambda i, j: (i, j))
        Gt, Ut, act = _matmul(f"ffn1_{i}", "nn", (T // tm, N_CHIPS),
                              [((xn2, BS((tm, D), lambda i, j: (i, 0))), (w["w_gate"], BS((None, D, FS), lambda i, j: (j, 0, 0)))),
                               ((xn2, BS((tm, D), lambda i, j: (i, 0))), (w["w_up"], BS((None, D, FS), lambda i, j: (j, 0, 0))))],
                              [], [((T, FF), bf16, ffo)] * 3, epilogue=ffn_epi, after=order.token)
        (h2,) = _matmul(f"ffn2_{i}", "nn", (T // tmw, D // tnd),
                        [((act, BS((tmw, FF), lambda i, j: (i, 0))), (w["w_down"], BS((FF, tnd), lambda i, j: (0, j))))],
                        [(h1, BS((tmw, tnd), lambda i, j: (i, j)))],
                        [((T, D), f32, BS((tmw, tnd), lambda i, j: (i, j)))],
                        epilogue=lambda accs, ex: (accs[0] + ex[0],))
        order.done(h2)
        if i + 1 < L:
            d2d = gather_forward(i + 1, ici)
            ici = gather_start(i + 2, BIG) if i + 2 < L else None
        xn3 = _rms_fwd(f"rms3_{i}", h2, order.follows(norm_ple[i]))

        def ple_epi(accs, ex):
            gate = jax.nn.sigmoid(accs[0])
            return ex[0] + accs[1] * gate, gate, accs[1]

        dso = BS((tmw, DS), lambda i, j: (i, j))
        h3, gate, e = _matmul(f"ple_{i}", "nn", (T // tmw, N_CHIPS),
                              [((xn3, BS((tmw, D), lambda i, j: (i, 0))), (w["w_pg"], BS((D, DS), lambda i, j: (0, j)))),
                               ((pb16[i], BS((tmw, PL), lambda i, j: (i, 0))), (w["w_pp"], BS((None, PL, DS), lambda i, j: (j, 0, 0))))],
                              [(h2, dso)], [((T, D), f32, dso), ((T, D), bf16, dso), ((T, D), bf16, dso)], epilogue=ple_epi)
        sv.update(xn1=xn1, P=P, Pf=Pf, fb=fb, c_col=c_col, c_row=c_row, qn=qn, kn=kn, vb=vb, o32=o32, lse=lse,
                  mix=mix, h1=h1, xn2=xn2, Gt=Gt, Ut=Ut, act=act, h2=h2, xn3=xn3, gate=gate, e=e)
        saved.append(sv)
        h = h3
        order.done(h3)
        if i + 1 < L:
            Wf[i + 1] = gathered(d2d)

    dh, loss_tile = _loss_grad("loss", h, loss_target.reshape(T, D))

    small_g = {n: [None] * L for n in SMALL}
    big_out = {}

    def stage_a(u):
        n_u = len(u["names"])
        lands = [lax.empty((N_CHIPS, g.shape[1] // 2, g.shape[2]), f32) for g in u["grads"]]
        u["a"] = order.start(f"rs_a_{u['tag']}", _plan_sibling_halves(n_u), u["grads"] + lands)

    def stage_pair(u):
        n_u = len(u["names"])
        out = order.wait(u["a"])
        pairs = [_pair_sum(f"rs_pair_{u['tag']}_{a}", out[a], out[n_u + a]) for a in range(n_u)]
        u["pb"], u["own"] = [t[0] for t in pairs], [t[1] for t in pairs]

    def stage_b(u):
        lands = [lax.empty(t.shape, bf16) for t in u["pb"]]
        u["b"] = order.start(f"rs_b_{u['tag']}", _plan_chip_scatter(len(u["names"])), u["pb"] + lands)

    def stage_sum(u):
        n_u = len(u["names"])
        out = order.wait(u["b"])
        u["sum"] = [_chip_sum(f"rs_sum_{u['tag']}_{a}", u["own"][a], out[n_u + a]) for a in range(n_u)]

    def stage_c(u):
        u["c"] = order.start(f"rs_c_{u['tag']}", _plan_sibling_join(len(u["names"])), u["sum"])

    def stage_adamw(u):
        for n, r in zip(u["names"], order.wait(u["c"])):
            big_out[n] = _adamw_layer(f"adamw_{n}_{u['layer']}", u["layer"], W[n], M[n], V[n],
                                      r.reshape(W[n].shape[1:]), big_out.get(n))

    dh1 = prev_f = prev_m = None
    for i in reversed(range(L)):
        w, sv = Wf[i], saved[i]
        if dh1 is not None:
            dh, _, dg = _rms_bwd(f"rms1_bw_{i + 1}", dxn1, saved[i + 1]["h0"], order.follows(norm_mix[i + 1]), dh1)
            small_g["norm_mix"][i + 1] = dg.reshape(D)
        de, dz = _ple_bwd_elem(f"ple_bw_{i}", dh, sv["gate"], sv["e"])
        (d_wpp,) = _matmul(f"d_wpp_{i}", "tn", (N_CHIPS, 1),
                           [((pb16[i], BS((T, PL), lambda i, j: (0, 0))), (de, BS((T, DS), lambda i, j: (0, i))))], [],
                           [((N_CHIPS, PL, DS), f32, BS((None, PL, DS), lambda i, j: (i, 0, 0)))])
        (d_wpg,) = _matmul(f"d_wpg_{i}", "tn", (D // tkd, D // tnd),
                           [((sv["xn3"], BS((T, tkd), lambda i, j: (0, i))), (dz, BS((T, tnd), lambda i, j: (0, j))))], [],
                           [((D, D), f32, BS((tkd, tnd), lambda i, j: (i, j)))])
        order.done(dz)
        if prev_m is not None:
            stage_pair(prev_m)
        (dxn3,) = _matmul(f"d_xn3_{i}", "nt", (T // tmw, D // tnd),
                          [((dz, BS((tmw, D), lambda i, j: (i, 0))), (w["w_pg"], BS((tnd, D), lambda i, j: (j, 0))))], [],
                          [((T, D), f32, BS((tmw, tnd), lambda i, j: (i, j)))], after=order.token)
        dh2, dh2b, dg = _rms_bwd(f"rms3_bw_{i}", dxn3, sv["h2"], norm_ple[i], dh)
        small_g["norm_ple"][i] = dg.reshape(D)

        def dffn_epi(accs, ex):
            da = accs[0]
            g_, u_ = ex[0].astype(f32), ex[1].astype(f32)
            sg = jax.nn.sigmoid(g_)
            return da * u_ * (sg * (1.0 + g_ * (1.0 - sg))), da * (g_ * sg)

        ffo = BS((tm, FS), lambda j, i: (i, j))
        dG, dU = _matmul(f"d_act_{i}", "nt", (N_CHIPS, T // tm),
                         [((dh2b, BS((tm, D), lambda j, i: (i, 0))), (w["w_down"], BS((FS, D), lambda j, i: (j, 0))))],
                         [(sv["Gt"], ffo), (sv["Ut"], ffo)], [((T, FF), bf16, ffo)] * 2, epilogue=dffn_epi)
        (d_wd,) = _matmul(f"d_wd_{i}", "tn", (N_CHIPS, D // tnw),
                          [((sv["act"], BS((T, FS), lambda i, j: (0, i))), (dh2b, BS((T, tnw), lambda i, j: (0, j))))], [],
                          [((FF, D), f32, BS((FS, tnw), lambda i, j: (i, j)))])
        gu_out = BS((None, tnd, FS), lambda j, i: (j, i, 0))
        d_wg, d_wu = _matmul(f"d_wgu_{i}", "tn", (N_CHIPS, D // tnd),
                             [((sv["xn2"], BS((T, tnd), lambda j, i: (0, i))), (dG, BS((T, FS), lambda j, i: (0, j)))),
                              ((sv["xn2"], BS((T, tnd), lambda j, i: (0, i))), (dU, BS((T, FS), lambda j, i: (0, j))))], [],
                             [((N_CHIPS, D, FS), f32, gu_out)] * 2, epilogue=lambda accs, ex: (accs[0], accs[1]))
        order.done(dG)
        unit_f = dict(tag=f"{i}f", layer=i, names=["w_ffn_gate", "w_ffn_up", "w_ffn_down", "w_ple_gate", "w_ple_proj"],
                      grads=[d_wg, d_wu, d_wd.reshape(N_CHIPS, FS, D), d_wpg.reshape(N_CHIPS, DS, D), d_wpp])
        stage_a(unit_f)
        if prev_f is not None:
            stage_sum(prev_f)
            stage_c(prev_f)
        if prev_m is not None:
            stage_b(prev_m)
        tm2 = _tile(T, 256, 16)
        (dxn2,) = _matmul(f"d_xn2_{i}", "nt", (D // tnd, T // tm2),
                          [((dG, BS((tm2, FF), lambda j, i: (i, 0))), (w["w_gate"], BS((N_CHIPS, tnd, FS), lambda j, i: (0, j, 0)))),
                           ((dU, BS((tm2, FF), lambda j, i: (i, 0))), (w["w_up"], BS((N_CHIPS, tnd, FS), lambda j, i: (0, j, 0))))], [],
                          [((T, D), f32, BS((tm2, tnd), lambda j, i: (i, j)))], after=order.token)
        dh1, dh1b, dg = _rms_bwd(f"rms2_bw_{i}", dxn2, sv["h1"], norm_ffn[i], dh2)
        small_g["norm_ffn"][i] = dg.reshape(D)
        (dmix,) = _matmul(f"d_mix_{i}", "nt", (T // tmw, D // tnd),
                          [((dh1b, BS((tmw, D), lambda i, j: (i, 0))), (w["w_out"], BS((tnd, D), lambda i, j: (j, 0))))], [],
                          [((T, D), f32, BS((tmw, tnd), lambda i, j: (i, j)))])
        (d_wout,) = _matmul(f"d_wout_{i}", "tn", (D // tkd, D // tnd),
                            [((sv["mix"], BS((T, tkd), lambda i, j: (0, i))), (dh1b, BS((T, tnd), lambda i, j: (0, j))))], [],
                            [((D, D), f32, BS((tkd, tnd), lambda i, j: (i, j)))])
        order.done(dmix)
        stage_pair(unit_f)
        stage_b(unit_f)
        if prev_f is not None:
            stage_adamw(prev_f)
        qg, kg = q_norm[i].reshape(1, HEAD), k_norm[i].reshape(1, HEAD)
        dq, dk, dv, dc_row, dqg, dkg = _attn_bwd(f"attn_bw_{i}", sv["qn"], sv["kn"], sv["vb"], sv["o32"], dmix,
                                                 sv["lse"], sv["c_col"], sv["c_row"], sv["P"], order.follows(qg), kg, tb)
        small_g["q_norm"][i] = dqg.reshape(HEAD)
        small_g["k_norm"][i] = dkg.reshape(HEAD)
        dct = jnp.pad(dc_row.reshape(H, T), ((0, HEAD - H), (0, 0)))
        dPf, dfb = _fgate_bwd(f"fgate_bw_{i}", dct, sv["Pf"], sv["fb"])
        small_g["forget_bias"][i] = dfb[0, :H]
        gain = gmlp_v_norm[i].reshape(G, 1, HEAD)
        bs = gmlp_b_s[i].reshape(G, HEAD, 1)
        dgu, dgv, dws, dbs, dgain = _gmlp_bwd(f"gmlp_bw_{i}", sv["P"], dmix, gain, gmlp_w_s[i], bs, col_gu, col_gv,
                                              col_dg, Wd)
        small_g["gmlp_w_s"][i] = dws
        small_g["gmlp_b_s"][i] = dbs.reshape(G, HEAD)
        small_g["gmlp_v_norm"][i] = dgain.reshape(G, HEAD)
        ps = pool_scale[i].reshape(1, Wd)
        dxp, dpw, dps = _pool_bwd(f"pool_bw_{i}", sv["P"], dmix, pool_w[i], ps, col_xp, col_dp, Wd)
        small_g["pool_w"][i] = dpw
        small_g["pool_scale"][i] = dps.reshape(Wd)
        dP = jnp.concatenate([dq, dk, dv, dgu, dgv, dxp], axis=1)
        (d_wmain,) = _matmul(f"d_wmain_{i}", "tn", (D // tkd, NM // tn),
                             [((sv["xn1"], BS((T, tkd), lambda i, j: (0, i))), (dP, BS((T, tn), lambda i, j: (0, j))))], [],
                             [((D, NM), f32, BS((tkd, tn), lambda i, j: (i, j)))])
        (d_wf,) = _matmul(f"d_wf_{i}", "tn", (D // tnd, 1),
                          [((sv["xn1"], BS((T, tnd), lambda i, j: (0, i))), (dPf, BS((T, HEAD), lambda i, j: (0, 0))))], [],
                          [((D, HEAD), f32, BS((tnd, HEAD), lambda i, j: (i, 0)))])
        def win_cols(lo, hi):
            out = []
            for src, s0, e0, off in ((d_wmain, 0, 3 * A, 0), (d_wf, 3 * A, 3 * A + H, 3 * A), (d_wmain, 3 * A + H, DP, H)):
                s, e = max(lo, s0), min(hi, e0)
                if s < e:
                    out.append(src[:, s - off:e - off])
            return out

        d_win4 = jnp.stack([jnp.concatenate(win_cols(j * DP4, (j + 1) * DP4), axis=1) for j in range(N_CHIPS)])
        order.done(dP)
        unit_m = dict(tag=f"{i}m", layer=i, names=["w_in", "w_out"], grads=[d_win4, d_wout.reshape(N_CHIPS, DS, D)])
        stage_a(unit_m)
        if prev_m is not None:
            stage_sum(prev_m)
            stage_c(prev_m)
        (dxn1,) = _matmul(f"d_xn1_{i}", "nt", (D // tnd, T // tm),
                          [((dP, BS((tm, NM), lambda j, i: (i, 0))), (w["w_main"], BS((tnd, NM), lambda j, i: (j, 0)))),
                           ((dPf, BS((tm, HEAD), lambda j, i: (i, 0))), (w["w_f"], BS((tnd, HEAD), lambda j, i: (j, 0))))], [],
                          [((T, D), f32, BS((tm, tnd), lambda j, i: (i, j)))], after=order.token)
        order.done(dxn1)
        if prev_m is not None:
            stage_adamw(prev_m)
        prev_f, prev_m = unit_f, unit_m

    dh, _, dg = _rms_bwd("rms1_bw_0", dxn1, saved[0]["h0"], order.follows(norm_mix[0]), dh1)
    small_g["norm_mix"][0] = dg.reshape(D)
    order.done(dh)
    stage_pair(prev_m)
    stage_b(prev_m)
    stage_sum(prev_f)
    stage_c(prev_f)

    small_full = {n: jnp.stack(small_g[n]) for n in SMALL}
    small_shapes = [W[n].shape for n in SMALL]
    n_rows = sum(_rows_of(W[n].size) for n in SMALL) + _rows_of(1)
    rows8 = -(-n_rows // 64) * 8
    packed = order.follows(_pack_rows([small_full[n] for n in SMALL] + [loss_tile[0, :1]], N_DEV * rows8))
    summed = _all_reduce_small("allreduce_small", packed.reshape(N_DEV, rows8, 128)).reshape(-1, 128)
    order.done(summed)
    stage_adamw(prev_f)
    stage_sum(prev_m)
    stage_c(prev_m)
    stage_adamw(prev_m)
    *small_grads, loss_row = _unpack_rows(summed, small_shapes + [(1,)])
    grads = dict(zip(SMALL, small_grads))
    loss = loss_row[0]

    wp, mp, vp = (_pack_rows([t[n] for n in SMALL], N_DEV * rows8) for t in (W, M, V))
    delta, new_m, new_v = (dict(zip(SMALL, _unpack_rows(t, small_shapes)))
                           for t in _adamw("adamw_small", wp, summed, mp, vp))
    for n in BIG:
        grads[n], delta[n], new_m[n], new_v[n] = big_out[n]

    return (loss, dh.reshape(1, T, D), *[grads[n] for n in WEIGHTS], *[delta[n] for n in WEIGHTS],
            *[new_m[n] for n in WEIGHTS], *[new_v[n] for n in WEIGHTS])
```

```python
import jax
import jax.numpy as jnp
from jax import lax
from jax.experimental import pallas as pl
from jax.experimental.pallas import tpu as pltpu

f32, bf16 = jnp.float32, jnp.bfloat16
S = jax.ShapeDtypeStruct
BS = pl.BlockSpec
ANY = pl.BlockSpec(memory_space=pl.ANY)
MESH = pl.DeviceIdType.MESH

EPS = 1e-6
HEAD = 128
POOL_WINDOWS = (2, 4, 8, 16)
NEG = -1e30
N_CHIPS = 4
N_DEV = 8
VMEM_LIMIT = 56 * 1024 * 1024

ADAM_LR, ADAM_B1, ADAM_B2, ADAM_EPS, ADAM_WD, ADAM_STEP = 0.001, 0.9, 0.999, 1e-08, 0.01, 10

BIG = ("w_in", "w_out", "w_ffn_gate", "w_ffn_up", "w_ffn_down", "w_ple_gate", "w_ple_proj")
SMALL = ("norm_mix", "q_norm", "k_norm", "forget_bias", "gmlp_v_norm", "gmlp_w_s", "gmlp_b_s", "pool_w",
         "pool_scale", "norm_ffn", "norm_ple")
WEIGHTS = ("norm_mix", "w_in", "q_norm", "k_norm", "forget_bias", "gmlp_v_norm", "gmlp_w_s", "gmlp_b_s", "pool_w",
           "pool_scale", "w_out", "norm_ffn", "w_ffn_gate", "w_ffn_up", "w_ffn_down", "norm_ple", "w_ple_gate",
           "w_ple_proj")


def _tile(n, target, mult):
    best = None
    for t in range(mult, min(n, target) + 1, mult):
        if n % t == 0:
            best = t
    return best if best is not None else n


def _params(**kw):
    return pltpu.CompilerParams(vmem_limit_bytes=VMEM_LIMIT, **kw)


def _dot(a, b, kind):
    dims = {"nn": (((1,), (0,)), ((), ())), "nt": (((1,), (1,)), ((), ())), "tn": (((0,), (0,)), ((), ()))}[kind]
    return lax.dot_general(a.astype(bf16), b.astype(bf16), dims, preferred_element_type=f32)


def _heads_per_program(n_heads):
    return 2 if n_heads % 2 == 0 else 1


def _my_place():
    x, y, c = lax.axis_index("x"), lax.axis_index("y"), lax.axis_index("c")
    return x, y, c, 2 * x + y


def _matmul(name, kind, grid, pairs, extras, outs, epilogue=None, after=None):
    n_p, n_e = len(pairs), len(extras)
    tokens = [] if after is None else [(after, BS((8, 128), lambda *_: (0, 0)))]

    def body(*refs):
        a_refs, b_refs = refs[:n_p], refs[n_p:2 * n_p]
        e_refs = refs[2 * n_p:2 * n_p + n_e]
        o_refs = refs[2 * n_p + n_e + len(tokens):]
        accs = []
        for a_ref, b_ref in zip(a_refs, b_refs):
            if len(b_ref.shape) == 3:
                w = b_ref.shape[2]
                acc = None
                for s in range(b_ref.shape[0]):
                    d = _dot(a_ref[:, s * w:(s + 1) * w], b_ref[s], kind)
                    acc = d if acc is None else acc + d
            else:
                acc = _dot(a_ref[...], b_ref[...], kind)
            accs.append(acc)
        if epilogue is None:
            res = accs[0]
            for t in accs[1:]:
                res = res + t
            res = (res,)
        else:
            res = epilogue(accs, [e[...] for e in e_refs])
        for o_ref, o in zip(o_refs, res):
            o_ref[...] = o.astype(o_ref.dtype)

    in_arrays = [p[0][0] for p in pairs] + [p[1][0] for p in pairs] + [e[0] for e in extras + tokens]
    in_specs = [p[0][1] for p in pairs] + [p[1][1] for p in pairs] + [e[1] for e in extras + tokens]
    res = pl.pallas_call(
        body, name=name, grid=grid, in_specs=in_specs,
        out_specs=[o[2] for o in outs], out_shape=[S(o[0], o[1]) for o in outs],
        compiler_params=_params(),
    )(*in_arrays)
    return res


def _rms_fwd(name, x, g):
    T, D = x.shape
    tr = _tile(T, 256, 8)

    def body(x_ref, g_ref, o_ref):
        xv = x_ref[...]
        r = lax.rsqrt(jnp.mean(xv * xv, axis=-1, keepdims=True) + EPS)
        o_ref[...] = (xv * r * g_ref[...]).astype(o_ref.dtype)

    return pl.pallas_call(
        body, name=name, grid=(T // tr,),
        in_specs=[BS((tr, D), lambda i: (i, 0)), BS((1, D), lambda i: (0, 0))],
        out_specs=BS((tr, D), lambda i: (i, 0)), out_shape=S((T, D), bf16),
    )(x, g.reshape(1, D))


def _rms_bwd(name, dxn, x, g, dres):
    T, D = x.shape
    tr = _tile(T, 256, 8)

    def body(dxn_ref, x_ref, g_ref, dres_ref, dx_ref, dxb_ref, dg_ref):
        i = pl.program_id(0)
        xv = x_ref[...]
        r = lax.rsqrt(jnp.mean(xv * xv, axis=-1, keepdims=True) + EPS)
        xh = xv * r
        dxn_v = dxn_ref[...]
        dxh = dxn_v * g_ref[...]
        dx = dres_ref[...] + r * (dxh - xh * jnp.mean(dxh * xh, axis=-1, keepdims=True))
        dx_ref[...] = dx
        dxb_ref[...] = dx.astype(bf16)
        part = jnp.sum(dxn_v * xh, axis=0, keepdims=True)

        @pl.when(i == 0)
        def _():
            dg_ref[...] = part

        @pl.when(i > 0)
        def _():
            dg_ref[...] += part

    row = BS((tr, D), lambda i: (i, 0))
    vec = BS((1, D), lambda i: (0, 0))
    return pl.pallas_call(
        body, name=name, grid=(T // tr,),
        in_specs=[row, row, vec, row], out_specs=[row, row, vec],
        out_shape=[S((T, D), f32), S((T, D), bf16), S((1, D), f32)],
    )(dxn, x, g.reshape(1, D), dres)


def _loss_grad(name, y, tgt):
    T, D = y.shape
    tr = _tile(T, 256, 8)

    def body(y_ref, t_ref, dy_ref, l_ref):
        i = pl.program_id(0)
        e = y_ref[...] - t_ref[...]
        dy_ref[...] = e * (1.0 / D)
        part = 0.5 * jnp.sum(jnp.mean(e * e, axis=-1, keepdims=True), axis=0, keepdims=True)

        @pl.when(i == 0)
        def _():
            l_ref[...] = jnp.zeros_like(l_ref)

        l_ref[...] += jnp.broadcast_to(part, l_ref.shape)

    row = BS((tr, D), lambda i: (i, 0))
    return pl.pallas_call(
        body, name=name, grid=(T // tr,), in_specs=[row, row],
        out_specs=[row, BS((8, 128), lambda i: (0, 0))],
        out_shape=[S((T, D), f32), S((8, 128), f32)],
    )(y, tgt)


def _ple_bwd_elem(name, dh, gate, e):
    T, D = dh.shape
    tr = _tile(T, 256, 16)

    def body(dh_ref, g_ref, e_ref, de_ref, dz_ref):
        d = dh_ref[...]
        g = g_ref[...].astype(f32)
        de_ref[...] = (d * g).astype(bf16)
        dz_ref[...] = (d * e_ref[...].astype(f32) * g * (1.0 - g)).astype(bf16)

    row = BS((tr, D), lambda i: (i, 0))
    return pl.pallas_call(
        body, name=name, grid=(T // tr,), in_specs=[row, row, row], out_specs=[row, row],
        out_shape=[S((T, D), bf16), S((T, D), bf16)],
    )(dh, gate, e)


def _cast_layer(name, w_all, layer):
    _, R, C = w_all.shape
    lanes = -(-C // 128) * 128
    tr = _tile(R, max(16, (1024 * 1024) // lanes // 16 * 16), 16)

    def body(w_ref, o_ref):
        o_ref[...] = w_ref[...].astype(bf16)

    return pl.pallas_call(
        body, name=name, grid=(R // tr,), in_specs=[BS((None, tr, C), lambda r: (layer, r, 0))],
        out_specs=BS((tr, C), lambda r: (r, 0)), out_shape=S((R, C), bf16),
    )(w_all)


def _gelu_and_grad(x):
    k0, k1 = 0.7978845608028654, 0.044715
    th = jnp.tanh(k0 * (x + k1 * x * x * x))
    val = 0.5 * x * (1.0 + th)
    grad = 0.5 * (1.0 + th) + 0.5 * x * (1.0 - th * th) * (k0 * (1.0 + 3.0 * k1 * x * x))
    return val, grad


def _fgate_fwd(name, pf, fb):
    T = pf.shape[0]

    def body(pf_ref, fb_ref, c_ref, ct_ref):
        xv = jax.nn.log_sigmoid(pf_ref[...] + fb_ref[...])
        row = lax.broadcasted_iota(jnp.int32, xv.shape, 0)
        s = 1
        while s < T:
            xv = xv + jnp.where(row >= s, pltpu.roll(xv, s, 0), 0.0)
            s *= 2
        c_ref[...] = xv
        ct_ref[...] = xv.T

    return pl.pallas_call(body, name=name, out_shape=[S((T, HEAD), f32), S((HEAD, T), f32)])(pf, fb)


def _fgate_bwd(name, dct, pf, fb):
    T = pf.shape[0]

    def body(dct_ref, pf_ref, fb_ref, dpf_ref, dfb_ref):
        xv = dct_ref[...].T
        row = lax.broadcasted_iota(jnp.int32, xv.shape, 0)
        s = 1
        while s < T:
            xv = xv + jnp.where(row + s < T, pltpu.roll(xv, T - s, 0), 0.0)
            s *= 2
        df = xv * jax.nn.sigmoid(-(pf_ref[...] + fb_ref[...]))
        dpf_ref[...] = df.astype(bf16)
        dfb_ref[...] = jnp.sum(df, axis=0, keepdims=True)

    return pl.pallas_call(body, name=name, out_shape=[S((T, HEAD), bf16), S((1, HEAD), f32)])(dct, pf, fb)


def _qk_norm(name, P, qg, kg, A):
    T = P.shape[0]
    tr = _tile(T, 256, 16)
    n_heads = A // HEAD

    def body(q_ref, k_ref, v_ref, qg_ref, kg_ref, qn_ref, kn_ref, vb_ref):
        for h in range(n_heads):
            sl = slice(h * HEAD, (h + 1) * HEAD)
            for src, g_ref, dst in ((q_ref, qg_ref, qn_ref), (k_ref, kg_ref, kn_ref)):
                xv = src[:, sl]
                r = lax.rsqrt(jnp.mean(xv * xv, axis=-1, keepdims=True) + EPS)
                dst[:, sl] = (xv * r * g_ref[...]).astype(bf16)
        vb_ref[...] = v_ref[...].astype(bf16)

    vec = BS((1, HEAD), lambda i: (0, 0))
    out = BS((tr, A), lambda i: (i, 0))
    return pl.pallas_call(
        body, name=name, grid=(T // tr,),
        in_specs=[BS((tr, A), lambda i: (i, 0)), BS((tr, A), lambda i: (i, 1)), BS((tr, A), lambda i: (i, 2)), vec, vec],
        out_specs=[out, out, out], out_shape=[S((T, A), bf16)] * 3,
    )(P, P, P, qg, kg)


def _attn_fwd(name, qn, kn, vb, c_col, c_row, tb):
    T, A = qn.shape
    H = A // HEAD
    nb = T // tb
    scale = HEAD ** -0.5
    hp = _heads_per_program(H)
    wide = hp * HEAD

    def body(q_ref, k_ref, v_ref, cq_ref, ck_ref, o_ref, o32_ref, lse_ref):
        i = pl.program_id(1)
        below = lax.broadcasted_iota(jnp.int32, (tb, tb), 0) >= lax.broadcasted_iota(jnp.int32, (tb, tb), 1)

        def block(j, carry, diagonal):
            koff = pl.multiple_of(j * tb, tb)
            out = []
            for hh in range(hp):
                m, l, acc = carry[hh]
                sl = slice(hh * HEAD, (hh + 1) * HEAD)
                k = k_ref[pl.ds(koff, tb), sl]
                v = v_ref[pl.ds(koff, tb), sl]
                s = _dot(q_ref[:, sl], k, "nt") * scale + (cq_ref[hh] - ck_ref[hh, j])
                if diagonal:
                    s = jnp.where(below, s, NEG)
                m_new = jnp.maximum(m, jnp.max(s, axis=-1, keepdims=True))
                alpha = jnp.exp(m - m_new)
                p = jnp.exp(s - m_new)
                l = l * alpha + jnp.sum(p, axis=-1, keepdims=True)
                acc = acc * alpha + _dot(p, v, "nn")
                out.append((m_new, l, acc))
            return tuple(out)

        init = tuple((jnp.full((tb, 1), NEG, f32), jnp.zeros((tb, 1), f32), jnp.zeros((tb, HEAD), f32))
                     for _ in range(hp))
        carry = lax.fori_loop(0, i, lambda j, c: block(j, c, False), init)
        carry = block(i, carry, True)
        for hh in range(hp):
            m, l, acc = carry[hh]
            sl = slice(hh * HEAD, (hh + 1) * HEAD)
            o = acc / l
            o_ref[:, sl] = o.astype(bf16)
            o32_ref[:, sl] = o
            lse_ref[hh] = m + jnp.log(l)

    return pl.pallas_call(
        body, name=name, grid=(H // hp, nb),
        in_specs=[BS((tb, wide), lambda h, i: (i, h)), BS((T, wide), lambda h, i: (0, h)),
                  BS((T, wide), lambda h, i: (0, h)), BS((hp, tb, 1), lambda h, i: (h, i, 0)),
                  BS((hp, nb, 1, tb), lambda h, i: (h, 0, 0, 0))],
        out_specs=[BS((tb, wide), lambda h, i: (i, h)), BS((tb, wide), lambda h, i: (i, h)),
                   BS((hp, tb, 1), lambda h, i: (h, i, 0))],
        out_shape=[S((T, A), bf16), S((T, A), f32), S((H, T, 1), f32)],
    )(qn, kn, vb, c_col, c_row)


def _attn_bwd(name, qn, kn, vb, o, dmix, lse, c_col, c_row, P, qg, kg, tb):
    T, A = qn.shape
    H = A // HEAD
    nb = T // tb
    scale = HEAD ** -0.5
    hp = _heads_per_program(H)
    wide = hp * HEAD

    def body(q_ref, k_ref, v_ref, o_ref, do_ref, lse_ref, cq_ref, ck_ref, qraw_ref, kraw_ref, qg_ref, kg_ref,
             dq_out, dk_out, dv_out, dc_out, dqg_out, dkg_out, dq_acc, dk_acc, delta_s):
        h = pl.program_id(0)
        dq_acc[...] = jnp.zeros_like(dq_acc)
        below = lax.broadcasted_iota(jnp.int32, (tb, tb), 0) >= lax.broadcasted_iota(jnp.int32, (tb, tb), 1)
        for hh in range(hp):
            sl = slice(hh * HEAD, (hh + 1) * HEAD)
            delta_s[hh] = jnp.sum(do_ref[:, sl].astype(bf16).astype(f32) * o_ref[:, sl], axis=-1, keepdims=True)

        def kblock(j, _):
            koff = pl.multiple_of(j * tb, tb)

            def products(i, hh):
                sl = slice(hh * HEAD, (hh + 1) * HEAD)
                qoff = pl.multiple_of(i * tb, tb)
                return (_dot(q_ref[pl.ds(qoff, tb), sl], k_ref[pl.ds(koff, tb), sl], "nt"),
                        _dot(do_ref[pl.ds(qoff, tb), sl], v_ref[pl.ds(koff, tb), sl], "nt"))

            def qblock(i, carry, diagonal):
                qoff = pl.multiple_of(i * tb, tb)
                out = []
                for hh in range(hp):
                    dk, dv, dc, qk_i, dp = carry[hh]
                    sl = slice(hh * HEAD, (hh + 1) * HEAD)
                    ahead = products(jnp.minimum(i + 1, nb - 1), hh)
                    k = k_ref[pl.ds(koff, tb), sl]
                    q = q_ref[pl.ds(qoff, tb), sl]
                    do = do_ref[pl.ds(qoff, tb), sl].astype(bf16)
                    s = qk_i * scale + (cq_ref[hh, pl.ds(qoff, tb), :] - ck_ref[hh, j])
                    if diagonal:
                        s = jnp.where(below, s, NEG)
                    p = jnp.exp(s - lse_ref[hh, pl.ds(qoff, tb), :])
                    dv = dv + _dot(p, do, "tn")
                    ds = p * (dp - delta_s[hh, pl.ds(qoff, tb), :])
                    dc = dc - jnp.sum(ds, axis=0, keepdims=True)
                    dsb = (ds * scale).astype(bf16)
                    dk = dk + _dot(dsb, q, "tn")
                    dq_acc[pl.ds(qoff, tb), sl] += _dot(dsb, k, "nn")
                    out.append((dk, dv, dc, *ahead))
                return tuple(out)

            init = tuple((jnp.zeros((tb, HEAD), f32), jnp.zeros((tb, HEAD), f32), jnp.zeros((1, tb), f32),
                          *products(j, hh)) for hh in range(hp))
            carry = qblock(j, init, True)
            carry = lax.fori_loop(j + 1, nb, lambda i, c: qblock(i, c, False), carry)
            for hh in range(hp):
                dk, dv, dc = carry[hh][:3]
                sl = slice(hh * HEAD, (hh + 1) * HEAD)
                dk_acc[pl.ds(koff, tb), sl] = dk
                dv_out[pl.ds(koff, tb), sl] = dv.astype(bf16)
                dc_out[hh, j] = dc
            return 0

        lax.fori_loop(0, nb, kblock, 0)

        for raw_ref, g_ref, acc_ref, d_out, dg_out in ((qraw_ref, qg_ref, dq_acc, dq_out, dqg_out),
                                                       (kraw_ref, kg_ref, dk_acc, dk_out, dkg_out)):
            part = jnp.zeros((1, HEAD), f32)
            for hh in range(hp):
                sl = slice(hh * HEAD, (hh + 1) * HEAD)
                xv = raw_ref[:, sl]
                r = lax.rsqrt(jnp.mean(xv * xv, axis=-1, keepdims=True) + EPS)
                xh = xv * r
                dn = acc_ref[:, sl]
                dxh = dn * g_ref[...]
                d_out[:, sl] = (r * (dxh - xh * jnp.mean(dxh * xh, axis=-1, keepdims=True))).astype(bf16)
                part = part + jnp.sum(dn * xh, axis=0, keepdims=True)

            @pl.when(h == 0)
            def _():
                dg_out[...] = part

            @pl.when(h > 0)
            def _():
                dg_out[...] += part

    heads = lambda off: BS((T, wide), lambda h: (0, off + h))
    col = BS((hp, T, 1), lambda h: (h, 0, 0))
    row = BS((hp, nb, 1, tb), lambda h: (h, 0, 0, 0))
    vec = BS((1, HEAD), lambda h: (0, 0))
    return pl.pallas_call(
        body, name=name, grid=(H // hp,),
        in_specs=[heads(0), heads(0), heads(0), heads(0), heads(0), col, col, row, heads(0), heads(H // hp), vec, vec],
        out_specs=[heads(0), heads(0), heads(0), row, vec, vec],
        out_shape=[S((T, A), bf16)] * 3 + [S((H, nb, 1, tb), f32), S((1, HEAD), f32), S((1, HEAD), f32)],
        scratch_shapes=[pltpu.VMEM((T, wide), f32), pltpu.VMEM((T, wide), f32), pltpu.VMEM((hp, T, 1), f32)],
        compiler_params=_params(),
    )(qn, kn, vb, o, dmix, lse, c_col, c_row, P, P, qg, kg)


def _gmlp_fwd(name, P, gain, ws, b, col_u, col_v, Wd):
    T = P.shape[0]
    G = Wd // HEAD
    tr = _tile(T, 512, HEAD)

    def body(u_ref, v_ref, gain_ref, ws_ref, b_ref, y_ref):
        tril = lax.broadcasted_iota(jnp.int32, (HEAD, HEAD), 0) >= lax.broadcasted_iota(jnp.int32, (HEAD, HEAD), 1)
        wm = jnp.where(tril, ws_ref[...], 0.0).astype(bf16)
        for n in range(tr // HEAD):
            rows = slice(n * HEAD, (n + 1) * HEAD)
            u = jax.nn.gelu(u_ref[rows, :])
            a = jax.nn.gelu(v_ref[rows, :])
            r = lax.rsqrt(jnp.mean(a * a, axis=-1, keepdims=True) + EPS)
            vn = a * r * gain_ref[...]
            mixed = _dot(wm, vn, "nn") + b_ref[...]
            y_ref[rows, :] = (u * mixed).astype(bf16)

    return pl.pallas_call(
        body, name=name, grid=(G, T // tr),
        in_specs=[BS((tr, HEAD), lambda g, i: (i, col_u + g)), BS((tr, HEAD), lambda g, i: (i, col_v + g)),
                  BS((None, 1, HEAD), lambda g, i: (g, 0, 0)), BS((None, HEAD, HEAD), lambda g, i: (g, 0, 0)),
                  BS((None, HEAD, 1), lambda g, i: (g, 0, 0))],
        out_specs=BS((tr, HEAD), lambda g, i: (i, g)), out_shape=S((T, Wd), bf16),
    )(P, P, gain, ws, b)


def _gmlp_bwd(name, P, dmix, gain, ws, b, col_u, col_v, col_dy, Wd):
    T = P.shape[0]
    G = Wd // HEAD
    tr = _tile(T, 512, HEAD)

    def body(u_ref, v_ref, dy_ref, gain_ref, ws_ref, b_ref, du_ref, dv_ref, dws_ref, db_ref, dgain_ref):
        i = pl.program_id(1)
        tril = lax.broadcasted_iota(jnp.int32, (HEAD, HEAD), 0) >= lax.broadcasted_iota(jnp.int32, (HEAD, HEAD), 1)
        wm = jnp.where(tril, ws_ref[...], 0.0).astype(bf16)
        gain_v = gain_ref[...]
        dw = jnp.zeros((HEAD, HEAD), f32)
        db = jnp.zeros((HEAD, 1), f32)
        dgain = jnp.zeros((1, HEAD), f32)
        for n in range(tr // HEAD):
            rows = slice(n * HEAD, (n + 1) * HEAD)
            u, du_dx = _gelu_and_grad(u_ref[rows, :])
            a, da_dx = _gelu_and_grad(v_ref[rows, :])
            dy = dy_ref[rows, :]
            r = lax.rsqrt(jnp.mean(a * a, axis=-1, keepdims=True) + EPS)
            ah = a * r
            vnb = (ah * gain_v).astype(bf16)
            mixed = _dot(wm, vnb, "nn") + b_ref[...]
            dm = dy * u
            dmb = dm.astype(bf16)
            du_ref[rows, :] = (dy * mixed * du_dx).astype(bf16)
            db = db + jnp.sum(dm, axis=1, keepdims=True)
            dw = dw + _dot(dmb, vnb, "nt")
            dvn = _dot(wm, dmb, "tn")
            dgain = dgain + jnp.sum(dvn * ah, axis=0, keepdims=True)
            dah = dvn * gain_v
            da = r * (dah - ah * jnp.mean(dah * ah, axis=-1, keepdims=True))
            dv_ref[rows, :] = (da * da_dx).astype(bf16)
        dw = jnp.where(tril, dw, 0.0)

        @pl.when(i == 0)
        def _():
            dws_ref[...] = dw
            db_ref[...] = db
            dgain_ref[...] = dgain

        @pl.when(i > 0)
        def _():
            dws_ref[...] += dw
            db_ref[...] += db
            dgain_ref[...] += dgain

    out = BS((tr, HEAD), lambda g, i: (i, g))
    return pl.pallas_call(
        body, name=name, grid=(G, T // tr),
        in_specs=[BS((tr, HEAD), lambda g, i: (i, col_u + g)), BS((tr, HEAD), lambda g, i: (i, col_v + g)),
                  BS((tr, HEAD), lambda g, i: (i, col_dy + g)),
                  BS((None, 1, HEAD), lambda g, i: (g, 0, 0)), BS((None, HEAD, HEAD), lambda g, i: (g, 0, 0)),
                  BS((None, HEAD, 1), lambda g, i: (g, 0, 0))],
        out_specs=[out, out, BS((None, HEAD, HEAD), lambda g, i: (g, 0, 0)), BS((None, HEAD, 1), lambda g, i: (g, 0, 0)),
                   BS((None, 1, HEAD), lambda g, i: (g, 0, 0))],
        out_shape=[S((T, Wd), bf16), S((T, Wd), bf16), S((G, HEAD, HEAD), f32), S((G, HEAD, 1), f32),
                   S((G, 1, HEAD), f32)],
    )(P, P, dmix, gain, ws, b)


def _trailing_window(xv, w, row):
    k = 1
    while k < w:
        xv = xv + jnp.where(row >= k, pltpu.roll(xv, k, 0), 0.0)
        k *= 2
    return xv


def _leading_window(xv, w, row, T):
    k = 1
    while k < w:
        xv = xv + jnp.where(row + k < T, pltpu.roll(xv, T - k, 0), 0.0)
        k *= 2
    return xv


def _pool_fwd(name, P, pw, ps, col_x, Wd):
    T = P.shape[0]
    Gp = Wd // HEAD

    def body(x_ref, pw_ref, ps_ref, y_ref):
        row = lax.broadcasted_iota(jnp.int32, (T, HEAD), 0)
        for g in range(Gp):
            w = POOL_WINDOWS[g]
            sl = slice(g * HEAD, (g + 1) * HEAD)
            xv = x_ref[:, sl]
            cnt = jnp.minimum(row + 1, w).astype(f32)
            d = _trailing_window(xv, w, row) / cnt - xv
            y_ref[:, sl] = (_dot(d, pw_ref[g], "nn") * ps_ref[:, sl]).astype(bf16)

    return pl.pallas_call(
        body, name=name, grid=(1,),
        in_specs=[BS((T, Wd), lambda i: (0, col_x)), BS((Gp, HEAD, HEAD), lambda i: (0, 0, 0)), BS((1, Wd), lambda i: (0, 0))],
        out_specs=BS((T, Wd), lambda i: (0, 0)), out_shape=S((T, Wd), bf16), compiler_params=_params(),
    )(P, pw, ps)


def _pool_bwd(name, P, dmix, pw, ps, col_x, col_dy, Wd):
    T = P.shape[0]
    Gp = Wd // HEAD

    def body(x_ref, dy_ref, pw_ref, ps_ref, dx_ref, dpw_ref, dps_ref):
        row = lax.broadcasted_iota(jnp.int32, (T, HEAD), 0)
        for g in range(Gp):
            w = POOL_WINDOWS[g]
            sl = slice(g * HEAD, (g + 1) * HEAD)
            xv = x_ref[:, sl]
            cnt = jnp.minimum(row + 1, w).astype(f32)
            d = (_trailing_window(xv, w, row) / cnt - xv).astype(bf16)
            pwb = pw_ref[g].astype(bf16)
            z = _dot(d, pwb, "nn")
            dy = dy_ref[:, sl]
            dps_ref[:, sl] = jnp.sum(dy * z, axis=0, keepdims=True)
            dzb = (dy * ps_ref[:, sl]).astype(bf16)
            dpw_ref[g] = _dot(d, dzb, "tn")
            dd = _dot(dzb, pwb, "nt")
            dx_ref[:, sl] = (_leading_window(dd / cnt, w, row, T) - dd).astype(bf16)

    return pl.pallas_call(
        body, name=name, grid=(1,),
        in_specs=[BS((T, Wd), lambda i: (0, col_x)), BS((T, Wd), lambda i: (0, col_dy)),
                  BS((Gp, HEAD, HEAD), lambda i: (0, 0, 0)), BS((1, Wd), lambda i: (0, 0))],
        out_specs=[BS((T, Wd), lambda i: (0, 0)), BS((Gp, HEAD, HEAD), lambda i: (0, 0, 0)), BS((1, Wd), lambda i: (0, 0))],
        out_shape=[S((T, Wd), bf16), S((Gp, HEAD, HEAD), f32), S((1, Wd), f32)], compiler_params=_params(),
    )(P, dmix, pw, ps)


def _adamw(name, w, g, m, v):
    R, C = w.shape
    lanes = -(-C // 128) * 128
    tr = _tile(R, max(8, (512 * 1024) // lanes // 8 * 8), 8)
    c1 = 1.0 - ADAM_B1 ** ADAM_STEP
    c2 = 1.0 - ADAM_B2 ** ADAM_STEP

    def body(w_ref, g_ref, m_ref, v_ref, d_ref, nm_ref, nv_ref):
        gv = g_ref[...]
        nm = ADAM_B1 * m_ref[...] + (1.0 - ADAM_B1) * gv
        nv = ADAM_B2 * v_ref[...] + (1.0 - ADAM_B2) * (gv * gv)
        d_ref[...] = -ADAM_LR * ((nm / c1) / (jnp.sqrt(nv / c2) + ADAM_EPS) + ADAM_WD * w_ref[...])
        nm_ref[...] = nm
        nv_ref[...] = nv

    blk = BS((tr, C), lambda i: (i, 0))
    return pl.pallas_call(
        body, name=name, grid=(R // tr,), in_specs=[blk] * 4, out_specs=[blk] * 3, out_shape=[S((R, C), f32)] * 3,
    )(w, g, m, v)


def _sum_adamw_half(name, layer, own, got, w_all, m_all, v_all, prev):
    L, R, C = w_all.shape
    half = R // 2
    assert own.shape == (half, C)
    lanes = -(-C // 128) * 128
    tr = _tile(half, max(16, (256 * 1024) // lanes // 16 * 16), 16)
    nb = half // tr
    c1 = 1.0 - ADAM_B1 ** ADAM_STEP
    c2 = 1.0 - ADAM_B2 ** ADAM_STEP
    n_prev = 0 if prev is None else 4

    def body(own_ref, *rest):
        got_refs = rest[:N_CHIPS]
        w_ref, m_ref, v_ref = rest[N_CHIPS:N_CHIPS + 3]
        go_ref, d_ref, nm_ref, nv_ref = rest[N_CHIPS + 3 + n_prev:]
        j0 = _my_place()[3]
        gv = None
        for k in range(N_CHIPS):
            t = jnp.where(j0 == k, own_ref[...], got_refs[k][...].astype(f32))
            gv = t if gv is None else gv + t
        nm = ADAM_B1 * m_ref[...] + (1.0 - ADAM_B1) * gv
        nv = ADAM_B2 * v_ref[...] + (1.0 - ADAM_B2) * (gv * gv)
        d_ref[...] = -ADAM_LR * ((nm / c1) / (jnp.sqrt(nv / c2) + ADAM_EPS) + ADAM_WD * w_ref[...])
        nm_ref[...] = nm
        nv_ref[...] = nv
        go_ref[...] = gv

    def slot(k):
        return BS((None, tr, C), lambda r: (jnp.where(_my_place()[3] == k, (k + 1) % N_CHIPS, k), r, 0))

    slab = BS((None, tr, C), lambda r: (layer, lax.axis_index("c") * nb + r, 0))
    return pl.pallas_call(
        body, name=name, grid=(nb,),
        in_specs=[BS((tr, C), lambda r: (r, 0))] + [slot(k) for k in range(N_CHIPS)] + [slab] * 3 + [ANY] * n_prev,
        out_specs=[slab] * 4, out_shape=[S((L, R, C), f32)] * 4,
        input_output_aliases={N_CHIPS + 4 + k: k for k in range(n_prev)},
    )(own, got, got, got, got, w_all, m_all, v_all, *(prev or ()))


def _adamw_layer(name, layer, w_all, m_all, v_all, g, prev):
    L, R, C = w_all.shape
    lanes = -(-C // 128) * 128
    tr = _tile(R, max(8, (512 * 1024) // lanes // 8 * 8), 8)
    c1 = 1.0 - ADAM_B1 ** ADAM_STEP
    c2 = 1.0 - ADAM_B2 ** ADAM_STEP
    n_prev = 0 if prev is None else 4

    def body(w_ref, m_ref, v_ref, g_ref, *rest):
        go_ref, d_ref, nm_ref, nv_ref = rest[n_prev:]
        gv = g_ref[...]
        nm = ADAM_B1 * m_ref[...] + (1.0 - ADAM_B1) * gv
        nv = ADAM_B2 * v_ref[...] + (1.0 - ADAM_B2) * (gv * gv)
        d_ref[...] = -ADAM_LR * ((nm / c1) / (jnp.sqrt(nv / c2) + ADAM_EPS) + ADAM_WD * w_ref[...])
        nm_ref[...] = nm
        nv_ref[...] = nv
        go_ref[...] = gv

    slab = BS((None, tr, C), lambda r: (layer, r, 0))
    return pl.pallas_call(
        body, name=name, grid=(R // tr,),
        in_specs=[slab, slab, slab, BS((tr, C), lambda r: (r, 0))] + [ANY] * n_prev,
        out_specs=[slab] * 4, out_shape=[S((L, R, C), f32)] * 4,
        input_output_aliases={4 + k: k for k in range(n_prev)},
    )(w_all, m_all, v_all, g, *(prev or ()))


def _chip_of(k):
    return k // 2, k % 2


def _remote(src, dst, send_sems, recv_sems, idx, dev):
    return pltpu.make_async_remote_copy(src_ref=src, dst_ref=dst, send_sem=send_sems.at[idx], recv_sem=recv_sems.at[idx],
                                        device_id=dev, device_id_type=MESH)


def _plan_gather_chips(n):
    def plan(refs, ss, rs, base):
        ins, lands = refs[:n], refs[n:]
        x, y, c, j0 = _my_place()
        sib = (x, y, 1 - c)
        sends, recvs = [], []
        for a in range(n):
            half = ins[a].shape[0] // 2
            lo = c * half
            sends.append(_remote(ins[a], lands[a].at[j0], ss, rs, base + 4 * a + 3, sib))
            recvs.append(_remote(lands[a].at[j0], lands[a].at[j0], ss, rs, base + 4 * a + 3, sib))
            for r in (1, 2, 3):
                k = j0 ^ r
                dev = (*_chip_of(k), c)
                sends.append(_remote(ins[a].at[pl.ds(lo, half)], lands[a].at[j0, pl.ds(lo, half)], ss, rs,
                                     base + 4 * a + r - 1, dev))
                landed = lands[a].at[k, pl.ds(lo, half)]
                recvs.append(_remote(landed, landed, ss, rs, base + 4 * a + r - 1, dev))
        return sends, recvs
    return plan, 4 * n


def _plan_gather_forward(n):
    def plan(refs, ss, rs, base):
        x, y, c, j0 = _my_place()
        sib = (x, y, 1 - c)
        sends, recvs = [], []
        for a in range(n):
            half = refs[a].shape[1] // 2
            for r in (1, 2, 3):
                k = j0 ^ r
                landed = refs[a].at[k, pl.ds(c * half, half)]
                other = refs[a].at[k, pl.ds((1 - c) * half, half)]
                sends.append(_remote(landed, landed, ss, rs, base + 3 * a + r - 1, sib))
                recvs.append(_remote(other, other, ss, rs, base + 3 * a + r - 1, sib))
        return sends, recvs
    return plan, 3 * n


def _plan_sibling_halves(n):
    def plan(refs, ss, rs, base):
        ins, lands = refs[:n], refs[n:]
        x, y, c, _ = _my_place()
        sib = (x, y, 1 - c)
        sends, recvs = [], []
        for a in range(n):
            half = ins[a].shape[1] // 2
            sends.append(_remote(ins[a].at[:, pl.ds((1 - c) * half, half), :], lands[a], ss, rs, base + a, sib))
            recvs.append(_remote(lands[a], lands[a], ss, rs, base + a, sib))
        return sends, recvs
    return plan, n


def _plan_chip_scatter(n):
    def plan(refs, ss, rs, base):
        ins, lands = refs[:n], refs[n:]
        x, y, c, j0 = _my_place()
        sends, recvs = [], []
        for a in range(n):
            for r in (1, 2, 3):
                k = j0 ^ r
                dev = (*_chip_of(k), c)
                sends.append(_remote(ins[a].at[k], lands[a].at[j0], ss, rs, base + 3 * a + r - 1, dev))
                recvs.append(_remote(lands[a].at[k], lands[a].at[k], ss, rs, base + 3 * a + r - 1, dev))
        return sends, recvs
    return plan, 3 * n


def _plan_sibling_join(n):
    def plan(refs, ss, rs, base):
        x, y, c, _ = _my_place()
        sib = (x, y, 1 - c)
        sends, recvs = [], []
        for a in range(n):
            half = refs[a].shape[0] // 2
            mine = refs[a].at[pl.ds(c * half, half)]
            theirs = refs[a].at[pl.ds((1 - c) * half, half)]
            sends.append(_remote(mine, mine, ss, rs, base + a, sib))
            recvs.append(_remote(theirs, theirs, ss, rs, base + a, sib))
        return sends, recvs
    return plan, n


def _plan_sibling_rows(n, layer):
    def plan(refs, ss, rs, base):
        x, y, c, _ = _my_place()
        sib = (x, y, 1 - c)
        sends, recvs = [], []
        for a in range(n):
            half = refs[a].shape[1] // 2
            mine = refs[a].at[layer, pl.ds(c * half, half)]
            theirs = refs[a].at[layer, pl.ds((1 - c) * half, half)]
            sends.append(_remote(mine, mine, ss, rs, base + a, sib))
            recvs.append(_remote(theirs, theirs, ss, rs, base + a, sib))
        return sends, recvs
    return plan, n


_HBM = pl.BlockSpec(memory_space=pltpu.HBM)
_SEM = pl.BlockSpec(memory_space=pltpu.SEMAPHORE)
_EFFECT = pltpu.SideEffectType.DATAFLOW_SIDE_EFFECTING


def _exchange_start(name, plan, bufs, after):
    plan_fn, n_sems = plan
    n = len(bufs)

    def body(*refs):
        ss, rs, token = refs[n + len(after)], refs[n + len(after) + 1], refs[-1]
        sends, _ = plan_fn(refs[:n], ss, rs, 0)
        for cp in sends:
            cp.start()
        token[...] = jnp.zeros_like(token)

    res = pl.pallas_call(
        body, name=name,
        out_shape=(pltpu.SemaphoreType.DMA((n_sems,)), pltpu.SemaphoreType.DMA((n_sems,)),
                   *[pltpu.HBM(b.shape, b.dtype) for b in bufs], S((8, 128), f32)),
        in_specs=[_HBM] * n + [ANY] * len(after),
        out_specs=(_SEM, _SEM, *[_HBM] * n, pl.BlockSpec(memory_space=pltpu.VMEM)),
        input_output_aliases={k: 2 + k for k in range(n)},
        compiler_params=pltpu.CompilerParams(has_side_effects=_EFFECT),
    )(*[pltpu.with_memory_space_constraint(b, pltpu.HBM) for b in bufs], *after)
    return res[0], res[1], list(res[2:2 + n]), res[-1]


def _exchange_wait(name, plan, send_sems, recv_sems, bufs, after):
    plan_fn, _ = plan
    n = len(bufs)

    def body(*refs):
        ss, rs, token = refs[n], refs[n + 1], refs[-1]
        sends, recvs = plan_fn(refs[:n], ss, rs, 0)
        for cp in recvs:
            cp.wait_recv()
        for cp in sends:
            cp.wait_send()
        token[...] = jnp.zeros_like(token)

    res = pl.pallas_call(
        body, name=name,
        out_shape=(*[pltpu.HBM(b.shape, b.dtype) for b in bufs], S((8, 128), f32)),
        in_specs=[_HBM] * n + [_SEM, _SEM] + [ANY] * len(after),
        out_specs=(*[_HBM] * n, pl.BlockSpec(memory_space=pltpu.VMEM)),
        input_output_aliases={k: k for k in range(n)},
        compiler_params=pltpu.CompilerParams(has_side_effects=_EFFECT),
    )(*bufs, send_sems, recv_sems, *after)
    return list(res[:n]), res[-1]


class _Order:
    def __init__(self, first):
        self.marker = first
        self.token = None

    def _after(self):
        return [self.marker] + ([] if self.token is None else [self.token])

    def start(self, name, plan, bufs):
        ss, rs, thru, self.token = _exchange_start(name, plan, bufs, self._after())
        return name, plan, ss, rs, thru

    def wait(self, handle):
        name, plan, ss, rs, thru = handle
        out, self.token = _exchange_wait(name + "_wait", plan, ss, rs, thru, self._after())
        return out

    def follows(self, small):
        return small if self.token is None else small + self.token[0, 0]

    def done(self, result):
        self.marker = result[(slice(0, 1),) * result.ndim].reshape(1, 1)


def _all_reduce_small(name, g8):
    _, R, L = g8.shape

    def body(g_ref, out_ref, land, red, send1, recv1, send2, recv2):
        x, y, c, _ = _my_place()
        me = 4 * x + 2 * y + c
        peers = []
        for r in range(1, N_DEV):
            q = me ^ r
            peers.append((q, (q // 4, (q // 2) % 2, q % 2)))
        first = []
        for r, (q, dev) in enumerate(peers):
            cp = pltpu.make_async_remote_copy(src_ref=g_ref.at[q], dst_ref=land.at[me], send_sem=send1.at[r],
                                              recv_sem=recv1.at[r], device_id=dev, device_id_type=MESH)
            cp.start()
            first.append(cp)
        land[me] = g_ref[me]
        for r, (q, dev) in enumerate(peers):
            pltpu.make_async_remote_copy(src_ref=land.at[q], dst_ref=land.at[q], send_sem=send1.at[r],
                                         recv_sem=recv1.at[r], device_id=dev, device_id_type=MESH).wait_recv()
        acc = land[0]
        for d in range(1, N_DEV):
            acc = acc + land[d]
        red[...] = acc
        out_ref[me] = acc
        second = []
        for r, (q, dev) in enumerate(peers):
            cp = pltpu.make_async_remote_copy(src_ref=red, dst_ref=out_ref.at[me], send_sem=send2.at[r],
                                              recv_sem=recv2.at[r], device_id=dev, device_id_type=MESH)
            cp.start()
            second.append(cp)
        for r, (q, dev) in enumerate(peers):
            pltpu.make_async_remote_copy(src_ref=out_ref.at[q], dst_ref=out_ref.at[q], send_sem=send2.at[r],
                                         recv_sem=recv2.at[r], device_id=dev, device_id_type=MESH).wait_recv()
        for cp in first + second:
            cp.wait_send()

    vm = pl.BlockSpec(memory_space=pltpu.VMEM)
    return pl.pallas_call(
        body, name=name, in_specs=[vm], out_specs=vm, out_shape=S(g8.shape, f32),
        scratch_shapes=[pltpu.VMEM((N_DEV, R, L), f32), pltpu.VMEM((R, L), f32)]
        + [pltpu.SemaphoreType.DMA((N_DEV - 1,))] * 4,
        compiler_params=_params(),
    )(g8)


def _pair_sum(name, g4, sib):
    _, rows, cols = g4.shape
    half = rows // 2
    lanes = -(-cols // 128) * 128
    tr = _tile(half, max(16, (512 * 1024) // lanes // 16 * 16), 16)
    nb = half // tr

    def body(g_ref, s_ref, pb_ref, own_ref):
        j = pl.program_id(1)
        t = g_ref[...] + s_ref[...]
        pb_ref[...] = t.astype(bf16)

        @pl.when(j == _my_place()[3])
        def _():
            own_ref[...] = t

    return pl.pallas_call(
        body, name=name, grid=(nb, N_CHIPS),
        in_specs=[BS((None, tr, cols), lambda i, j: (j, lax.axis_index("c") * nb + i, 0)),
                  BS((None, tr, cols), lambda i, j: (j, i, 0))],
        out_specs=[BS((None, tr, cols), lambda i, j: (j, i, 0)), BS((tr, cols), lambda i, j: (i, 0))],
        out_shape=[S((N_CHIPS, half, cols), bf16), S((half, cols), f32)],
    )(g4, sib)


def _chip_sum(name, own, got):
    half, cols = own.shape
    lanes = -(-cols // 128) * 128
    tr = _tile(half, max(16, (512 * 1024) // lanes // 16 * 16), 16)
    nb = half // tr

    def body(own_ref, *rest):
        got_refs, o_ref = rest[:N_CHIPS], rest[N_CHIPS]
        j0 = _my_place()[3]
        acc = None
        for k in range(N_CHIPS):
            t = jnp.where(j0 == k, own_ref[...], got_refs[k][...].astype(f32))
            acc = t if acc is None else acc + t
        o_ref[...] = acc

    def slot(k):
        return BS((None, tr, cols), lambda i: (jnp.where(_my_place()[3] == k, (k + 1) % N_CHIPS, k), i, 0))

    return pl.pallas_call(
        body, name=name, grid=(nb,),
        in_specs=[BS((tr, cols), lambda i: (i, 0))] + [slot(k) for k in range(N_CHIPS)],
        out_specs=BS((tr, cols), lambda i: (lax.axis_index("c") * nb + i, 0)),
        out_shape=S((2 * half, cols), f32),
    )(own, got, got, got, got)


def _rows_of(size):
    return -(-size // 1024) * 8


def _pack_rows(arrs, n_rows):
    parts = []
    for a in arrs:
        rows = _rows_of(a.size)
        if a.size % 128 == 0:
            part = a.astype(f32).reshape(-1, 128)
            part = jnp.pad(part, ((0, rows - part.shape[0]), (0, 0)))
        else:
            part = jnp.pad(a.reshape(-1).astype(f32), (0, rows * 128 - a.size)).reshape(rows, 128)
        parts.append(part)
    used = sum(p.shape[0] for p in parts)
    return jnp.concatenate(parts + [jnp.zeros((n_rows - used, 128), f32)], axis=0)


def _unpack_rows(packed, shapes):
    out, row = [], 0
    for shp in shapes:
        size = 1
        for d in shp:
            size *= d
        rows = packed[row:row + _rows_of(size)]
        out.append(rows[:size // 128].reshape(shp) if size % 128 == 0 else rows.reshape(-1)[:size].reshape(shp))
        row += _rows_of(size)
    return out


def kernel(x, p, norm_mix, w_in, q_norm, k_norm, forget_bias, gmlp_v_norm, gmlp_w_s, gmlp_b_s, pool_w, pool_scale, w_out, norm_ffn, w_ffn_gate, w_ffn_up, w_ffn_down, norm_ple, w_ple_gate, w_ple_proj, loss_target, m_norm_mix, m_w_in, m_q_norm, m_k_norm, m_forget_bias, m_gmlp_v_norm, m_gmlp_w_s, m_gmlp_b_s, m_pool_w, m_pool_scale, m_w_out, m_norm_ffn, m_w_ffn_gate, m_w_ffn_up, m_w_ffn_down, m_norm_ple, m_w_ple_gate, m_w_ple_proj, v_norm_mix, v_w_in, v_q_norm, v_k_norm, v_forget_bias, v_gmlp_v_norm, v_gmlp_w_s, v_gmlp_b_s, v_pool_w, v_pool_scale, v_w_out, v_norm_ffn, v_w_ffn_gate, v_w_ffn_up, v_w_ffn_down, v_norm_ple, v_w_ple_gate, v_w_ple_proj):
    W = dict(norm_mix=norm_mix, w_in=w_in, q_norm=q_norm, k_norm=k_norm, forget_bias=forget_bias,
             gmlp_v_norm=gmlp_v_norm, gmlp_w_s=gmlp_w_s, gmlp_b_s=gmlp_b_s, pool_w=pool_w, pool_scale=pool_scale,
             w_out=w_out, norm_ffn=norm_ffn, w_ffn_gate=w_ffn_gate, w_ffn_up=w_ffn_up, w_ffn_down=w_ffn_down,
             norm_ple=norm_ple, w_ple_gate=w_ple_gate, w_ple_proj=w_ple_proj)
    M = dict(norm_mix=m_norm_mix, w_in=m_w_in, q_norm=m_q_norm, k_norm=m_k_norm, forget_bias=m_forget_bias,
             gmlp_v_norm=m_gmlp_v_norm, gmlp_w_s=m_gmlp_w_s, gmlp_b_s=m_gmlp_b_s, pool_w=m_pool_w,
             pool_scale=m_pool_scale, w_out=m_w_out, norm_ffn=m_norm_ffn, w_ffn_gate=m_w_ffn_gate,
             w_ffn_up=m_w_ffn_up, w_ffn_down=m_w_ffn_down, norm_ple=m_norm_ple, w_ple_gate=m_w_ple_gate,
             w_ple_proj=m_w_ple_proj)
    V = dict(norm_mix=v_norm_mix, w_in=v_w_in, q_norm=v_q_norm, k_norm=v_k_norm, forget_bias=v_forget_bias,
             gmlp_v_norm=v_gmlp_v_norm, gmlp_w_s=v_gmlp_w_s, gmlp_b_s=v_gmlp_b_s, pool_w=v_pool_w,
             pool_scale=v_pool_scale, w_out=v_w_out, norm_ffn=v_norm_ffn, w_ffn_gate=v_w_ffn_gate,
             w_ffn_up=v_w_ffn_up, w_ffn_down=v_w_ffn_down, norm_ple=v_norm_ple, w_ple_gate=v_w_ple_gate,
             w_ple_proj=v_w_ple_proj)

    L = w_in.shape[0]
    _, T, D = x.shape
    A, Wd = D // 2, D // 4
    H = A // HEAD
    G = gmlp_w_s.shape[1]
    Gp = pool_w.shape[1]
    DP4 = w_in.shape[2]
    DP = N_CHIPS * DP4
    NM = 3 * A + 3 * Wd
    FS = w_ffn_gate.shape[2]
    FF = N_CHIPS * FS
    DS = D // N_CHIPS
    PL = p.shape[-1]
    assert Wd // G == HEAD and Wd // Gp == HEAD and DP == NM + H and H <= HEAD
    assert all(w & (w - 1) == 0 for w in POOL_WINDOWS[:Gp])
    tb = _tile(T, 256, HEAD)
    nb = T // tb
    tm = _tile(T, 512, 16)
    tmw = _tile(T, 1024, 16)
    tn = _tile(NM, 512, 128)
    tnd = _tile(D, 512, 128)
    tkd = _tile(D, 1024, 128)
    tnw = _tile(D, 1024, 128)
    col_gu, col_gv, col_xp = 3 * A // HEAD, (3 * A + Wd) // HEAD, (3 * A + 2 * Wd) // Wd
    col_dg, col_dp = A // HEAD, (A + Wd) // Wd

    order = _Order(x[0, :1, :1])

    def gather_start(i, names, tag=""):
        shards = [_cast_layer(f"cast_{n}_{i}", W[n], i) for n in names]
        lands = [lax.empty((N_CHIPS,) + s.shape, bf16) for s in shards]
        return names, order.start(f"ag_ici_{i}{tag}", _plan_gather_chips(len(names)), shards + lands)

    def gather_forward(i, started, tag=""):
        names, handle = started
        return names, order.start(f"ag_d2d_{i}{tag}", _plan_gather_forward(len(names)), order.wait(handle)[len(names):])

    def gathered(forwarded):
        names, handle = forwarded
        g = dict(zip(names, order.wait(handle)))
        out = {}
        if "w_in" in g:
            def cols(lo, hi):
                parts = []
                for j in range(N_CHIPS):
                    s, e = max(lo, j * DP4), min(hi, (j + 1) * DP4)
                    if s < e:
                        parts.append(g["w_in"][j][:, s - j * DP4:e - j * DP4])
                return parts

            out["w_main"] = jnp.concatenate(cols(0, 3 * A) + cols(3 * A + H, DP), axis=1)
            out["w_f"] = jnp.pad(jnp.concatenate(cols(3 * A, 3 * A + H), axis=1), ((0, 0), (0, HEAD - H)))
        if "w_out" in g:
            out["w_out"] = g["w_out"].reshape(D, D)
        if "w_ffn_gate" in g:
            out.update(w_gate=g["w_ffn_gate"], w_up=g["w_ffn_up"], w_down=g["w_ffn_down"].reshape(FF, D),
                       w_pg=g["w_ple_gate"].reshape(D, D), w_pp=g["w_ple_proj"])
        return out

    Wf = [None] * L
    first = [gather_start(0, ["w_in"], "a"), gather_start(0, ["w_out"], "b"), gather_start(0, BIG[2:], "c")]
    d2d = gather_forward(0, first[0], "a")
    ici = gather_start(1, BIG) if L > 1 else None
    Wf[0] = gathered(d2d)

    h = x.reshape(T, D)
    pb16 = p.reshape(L, T, PL).astype(bf16)
    saved = []

    for i in range(L):
        w = Wf[i]
        sv = dict(h0=h)
        xn1 = _rms_fwd(f"rms1_{i}", h, order.follows(norm_mix[i]))
        (P,) = _matmul(f"proj_{i}", "nn", (T // tmw, NM // tn),
                       [((xn1, BS((tmw, D), lambda i, j: (i, 0))), (w["w_main"], BS((D, tn), lambda i, j: (0, j))))], [],
                       [((T, NM), f32, BS((tmw, tn), lambda i, j: (i, j)))])
        (Pf,) = _matmul(f"projf_{i}", "nn", (T // tm, 1),
                        [((xn1, BS((tm, D), lambda i, j: (i, 0))), (w["w_f"], BS((D, HEAD), lambda i, j: (0, 0))))], [],
                        [((T, HEAD), f32, BS((tm, HEAD), lambda i, j: (i, 0)))])
        fb = jnp.pad(forget_bias[i], (0, HEAD - H)).reshape(1, HEAD)
        cc, ct = _fgate_fwd(f"fgate_{i}", Pf, fb)
        c_col = ct[:H].reshape(H, T, 1)
        c_row = ct[:H].reshape(H, nb, 1, tb)
        qg, kg = q_norm[i].reshape(1, HEAD), k_norm[i].reshape(1, HEAD)
        qn, kn, vb = _qk_norm(f"qknorm_{i}", P, qg, kg, A)
        y_attn, o32, lse = _attn_fwd(f"attn_{i}", qn, kn, vb, c_col, c_row, tb)
        gain = gmlp_v_norm[i].reshape(G, 1, HEAD)
        bs = gmlp_b_s[i].reshape(G, HEAD, 1)
        y_gmlp = _gmlp_fwd(f"gmlp_{i}", P, gain, gmlp_w_s[i], bs, col_gu, col_gv, Wd)
        ps = pool_scale[i].reshape(1, Wd)
        y_pool = _pool_fwd(f"pool_{i}", P, pool_w[i], ps, col_xp, Wd)
        mix = jnp.concatenate([y_attn, y_gmlp, y_pool], axis=1)
        if i == 0:
            order.done(mix)
            w.update(gathered(gather_forward(0, first[1], "b")))
        (h1,) = _matmul(f"out_{i}", "nn", (T // tmw, D // tnd),
                        [((mix, BS((tmw, D), lambda i, j: (i, 0))), (w["w_out"], BS((D, tnd), lambda i, j: (0, j))))],
                        [(h, BS((tmw, tnd), lambda i, j: (i, j)))],
                        [((T, D), f32, BS((tmw, tnd), lambda i, j: (i, j)))],
                        epilogue=lambda accs, ex: (accs[0] + ex[0],), after=order.token)
        xn2 = _rms_fwd(f"rms2_{i}", h1, norm_ffn[i])
        if i == 0:
            order.done(xn2)
            w.update(gathered(gather_forward(0, first[2], "c")))

        def ffn_epi(accs, ex):
            g_, u_ = accs
            return g_, u_, g_ * jax.nn.sigmoid(g_) * u_

        ffo = BS((tm, FS), lambda i, j: (i, j))
        Gt, Ut, act = _matmul(f"ffn1_{i}", "nn", (T // tm, N_CHIPS),
                              [((xn2, BS((tm, D), lambda i, j: (i, 0))), (w["w_gate"], BS((None, D, FS), lambda i, j: (j, 0, 0)))),
                               ((xn2, BS((tm, D), lambda i, j: (i, 0))), (w["w_up"], BS((None, D, FS), lambda i, j: (j, 0, 0))))],
                              [], [((T, FF), bf16, ffo)] * 3, epilogue=ffn_epi, after=order.token)
        (h2,) = _matmul(f"ffn2_{i}", "nn", (T // tmw, D // tnd),
                        [((act, BS((tmw, FF), lambda i, j: (i, 0))), (w["w_down"], BS((FF, tnd), lambda i, j: (0, j))))],
                        [(h1, BS((tmw, tnd), lambda i, j: (i, j)))],
                        [((T, D), f32, BS((tmw, tnd), lambda i, j: (i, j)))],
                        epilogue=lambda accs, ex: (accs[0] + ex[0],))
        order.done(h2)
        if i + 1 < L:
            d2d = gather_forward(i + 1, ici)
            ici = gather_start(i + 2, BIG) if i + 2 < L else None
        xn3 = _rms_fwd(f"rms3_{i}", h2, order.follows(norm_ple[i]))

        def ple_epi(accs, ex):
            gate = jax.nn.sigmoid(accs[0])
            return ex[0] + accs[1] * gate, gate, accs[1]

        dso = BS((tmw, DS), lambda i, j: (i, j))
        h3, gate, e = _matmul(f"ple_{i}", "nn", (T // tmw, N_CHIPS),
                              [((xn3, BS((tmw, D), lambda i, j: (i, 0))), (w["w_pg"], BS((D, DS), lambda i, j: (0, j)))),
                               ((pb16[i], BS((tmw, PL), lambda i, j: (i, 0))), (w["w_pp"], BS((None, PL, DS), lambda i, j: (j, 0, 0))))],
                              [(h2, dso)], [((T, D), f32, dso), ((T, D), bf16, dso), ((T, D), bf16, dso)], epilogue=ple_epi)
        sv.update(xn1=xn1, P=P, Pf=Pf, fb=fb, c_col=c_col, c_row=c_row, qn=qn, kn=kn, vb=vb, o32=o32, lse=lse,
                  mix=mix, h1=h1, xn2=xn2, Gt=Gt, Ut=Ut, act=act, h2=h2, xn3=xn3, gate=gate, e=e)
        saved.append(sv)
        h = h3
        order.done(h3)
        if i + 1 < L:
            Wf[i + 1] = gathered(d2d)

    dh, loss_tile = _loss_grad("loss", h, loss_target.reshape(T, D))

    small_g = {n: [None] * L for n in SMALL}
    big_out = {}

    def stage_a(u):
        n_u = len(u["names"])
        lands = [lax.empty((N_CHIPS, g.shape[1] // 2, g.shape[2]), f32) for g in u["grads"]]
        u["a"] = order.start(f"rs_a_{u['tag']}", _plan_sibling_halves(n_u), u["grads"] + lands)

    def stage_pair(u):
        n_u = len(u["names"])
        out = order.wait(u["a"])
        pairs = [_pair_sum(f"rs_pair_{u['tag']}_{a}", out[a], out[n_u + a]) for a in range(n_u)]
        u["pb"], u["own"] = [t[0] for t in pairs], [t[1] for t in pairs]

    def stage_b(u):
        lands = [lax.empty(t.shape, bf16) for t in u["pb"]]
        u["b"] = order.start(f"rs_b_{u['tag']}", _plan_chip_scatter(len(u["names"])), u["pb"] + lands)

    def stage_sum(u):
        n_u = len(u["names"])
        out = order.wait(u["b"])
        u["sum"] = [_chip_sum(f"rs_sum_{u['tag']}_{a}", u["own"][a], out[n_u + a]) for a in range(n_u)]

    def stage_c(u):
        u["c"] = order.start(f"rs_c_{u['tag']}", _plan_sibling_join(len(u["names"])), u["sum"])

    def stage_adamw(u):
        for n, r in zip(u["names"], order.wait(u["c"])):
            big_out[n] = _adamw_layer(f"adamw_{n}_{u['layer']}", u["layer"], W[n], M[n], V[n],
                                      r.reshape(W[n].shape[1:]), big_out.get(n))

    def stage_sum_adamw(u):
        n_u = len(u["names"])
        out = order.wait(u["b"])
        for a, n in enumerate(u["names"]):
            big_out[n] = _sum_adamw_half(f"adamw_{n}_{u['layer']}", u["layer"], u["own"][a], out[n_u + a],
                                         W[n], M[n], V[n], big_out.get(n))

    def stage_share(u):
        bufs = [t for n in u["names"] for t in big_out[n]]
        u["c"] = order.start(f"rs_c_{u['tag']}", _plan_sibling_rows(len(bufs), u["layer"]), bufs)

    def stage_shared(u):
        out = order.wait(u["c"])
        for a, n in enumerate(u["names"]):
            big_out[n] = out[4 * a:4 * a + 4]

    dh1 = prev_f = prev_m = None
    for i in reversed(range(L)):
        w, sv = Wf[i], saved[i]
        if dh1 is not None:
            dh, _, dg = _rms_bwd(f"rms1_bw_{i + 1}", dxn1, saved[i + 1]["h0"], order.follows(norm_mix[i + 1]), dh1)
            small_g["norm_mix"][i + 1] = dg.reshape(D)
        de, dz = _ple_bwd_elem(f"ple_bw_{i}", dh, sv["gate"], sv["e"])
        (d_wpp,) = _matmul(f"d_wpp_{i}", "tn", (N_CHIPS, 1),
                           [((pb16[i], BS((T, PL), lambda i, j: (0, 0))), (de, BS((T, DS), lambda i, j: (0, i))))], [],
                           [((N_CHIPS, PL, DS), f32, BS((None, PL, DS), lambda i, j: (i, 0, 0)))])
        (d_wpg,) = _matmul(f"d_wpg_{i}", "tn", (D // tkd, D // tnd),
                           [((sv["xn3"], BS((T, tkd), lambda i, j: (0, i))), (dz, BS((T, tnd), lambda i, j: (0, j))))], [],
                           [((D, D), f32, BS((tkd, tnd), lambda i, j: (i, j)))])
        order.done(dz)
        if prev_m is not None:
            stage_pair(prev_m)
        (dxn3,) = _matmul(f"d_xn3_{i}", "nt", (T // tmw, D // tnd),
                          [((dz, BS((tmw, D), lambda i, j: (i, 0))), (w["w_pg"], BS((tnd, D), lambda i, j: (j, 0))))], [],
                          [((T, D), f32, BS((tmw, tnd), lambda i, j: (i, j)))], after=order.token)
        dh2, dh2b, dg = _rms_bwd(f"rms3_bw_{i}", dxn3, sv["h2"], norm_ple[i], dh)
        small_g["norm_ple"][i] = dg.reshape(D)

        def dffn_epi(accs, ex):
            da = accs[0]
            g_, u_ = ex[0].astype(f32), ex[1].astype(f32)
            sg = jax.nn.sigmoid(g_)
            return da * u_ * (sg * (1.0 + g_ * (1.0 - sg))), da * (g_ * sg)

        ffo = BS((tm, FS), lambda j, i: (i, j))
        dG, dU = _matmul(f"d_act_{i}", "nt", (N_CHIPS, T // tm),
                         [((dh2b, BS((tm, D), lambda j, i: (i, 0))), (w["w_down"], BS((FS, D), lambda j, i: (j, 0))))],
                         [(sv["Gt"], ffo), (sv["Ut"], ffo)], [((T, FF), bf16, ffo)] * 2, epilogue=dffn_epi)
        (d_wd,) = _matmul(f"d_wd_{i}", "tn", (N_CHIPS, D // tnw),
                          [((sv["act"], BS((T, FS), lambda i, j: (0, i))), (dh2b, BS((T, tnw), lambda i, j: (0, j))))], [],
                          [((FF, D), f32, BS((FS, tnw), lambda i, j: (i, j)))])
        gu_out = BS((None, tnd, FS), lambda j, i: (j, i, 0))
        d_wg, d_wu = _matmul(f"d_wgu_{i}", "tn", (N_CHIPS, D // tnd),
                             [((sv["xn2"], BS((T, tnd), lambda j, i: (0, i))), (dG, BS((T, FS), lambda j, i: (0, j)))),
                              ((sv["xn2"], BS((T, tnd), lambda j, i: (0, i))), (dU, BS((T, FS), lambda j, i: (0, j))))], [],
                             [((N_CHIPS, D, FS), f32, gu_out)] * 2, epilogue=lambda accs, ex: (accs[0], accs[1]))
        order.done(dG)
        unit_f = dict(tag=f"{i}f", layer=i, names=["w_ffn_gate", "w_ffn_up", "w_ffn_down", "w_ple_gate", "w_ple_proj"],
                      grads=[d_wg, d_wu, d_wd.reshape(N_CHIPS, FS, D), d_wpg.reshape(N_CHIPS, DS, D), d_wpp])
        stage_a(unit_f)
        if prev_f is not None:
            stage_sum_adamw(prev_f)
            stage_share(prev_f)
        if prev_m is not None:
            stage_b(prev_m)
        tm2 = _tile(T, 256, 16)
        (dxn2,) = _matmul(f"d_xn2_{i}", "nt", (D // tnd, T // tm2),
                          [((dG, BS((tm2, FF), lambda j, i: (i, 0))), (w["w_gate"], BS((N_CHIPS, tnd, FS), lambda j, i: (0, j, 0)))),
                           ((dU, BS((tm2, FF), lambda j, i: (i, 0))), (w["w_up"], BS((N_CHIPS, tnd, FS), lambda j, i: (0, j, 0))))], [],
                          [((T, D), f32, BS((tm2, tnd), lambda j, i: (i, j)))], after=order.token)
        dh1, dh1b, dg = _rms_bwd(f"rms2_bw_{i}", dxn2, sv["h1"], norm_ffn[i], dh2)
        small_g["norm_ffn"][i] = dg.reshape(D)
        (dmix,) = _matmul(f"d_mix_{i}", "nt", (T // tmw, D // tnd),
                          [((dh1b, BS((tmw, D), lambda i, j: (i, 0))), (w["w_out"], BS((tnd, D), lambda i, j: (j, 0))))], [],
                          [((T, D), f32, BS((tmw, tnd), lambda i, j: (i, j)))])
        (d_wout,) = _matmul(f"d_wout_{i}", "tn", (D // tkd, D // tnd),
                            [((sv["mix"], BS((T, tkd), lambda i, j: (0, i))), (dh1b, BS((T, tnd), lambda i, j: (0, j))))], [],
                            [((D, D), f32, BS((tkd, tnd), lambda i, j: (i, j)))])
        order.done(dmix)
        stage_pair(unit_f)
        stage_b(unit_f)
        if prev_f is not None:
            stage_shared(prev_f)
        qg, kg = q_norm[i].reshape(1, HEAD), k_norm[i].reshape(1, HEAD)
        dq, dk, dv, dc_row, dqg, dkg = _attn_bwd(f"attn_bw_{i}", sv["qn"], sv["kn"], sv["vb"], sv["o32"], dmix,
                                                 sv["lse"], sv["c_col"], sv["c_row"], sv["P"], order.follows(qg), kg, tb)
        small_g["q_norm"][i] = dqg.reshape(HEAD)
        small_g["k_norm"][i] = dkg.reshape(HEAD)
        dct = jnp.pad(dc_row.reshape(H, T), ((0, HEAD - H), (0, 0)))
        dPf, dfb = _fgate_bwd(f"fgate_bw_{i}", dct, sv["Pf"], sv["fb"])
        small_g["forget_bias"][i] = dfb[0, :H]
        gain = gmlp_v_norm[i].reshape(G, 1, HEAD)
        bs = gmlp_b_s[i].reshape(G, HEAD, 1)
        dgu, dgv, dws, dbs, dgain = _gmlp_bwd(f"gmlp_bw_{i}", sv["P"], dmix, gain, gmlp_w_s[i], bs, col_gu, col_gv,
                                              col_dg, Wd)
        small_g["gmlp_w_s"][i] = dws
        small_g["gmlp_b_s"][i] = dbs.reshape(G, HEAD)
        small_g["gmlp_v_norm"][i] = dgain.reshape(G, HEAD)
        ps = pool_scale[i].reshape(1, Wd)
        dxp, dpw, dps = _pool_bwd(f"pool_bw_{i}", sv["P"], dmix, pool_w[i], ps, col_xp, col_dp, Wd)
        small_g["pool_w"][i] = dpw
        small_g["pool_scale"][i] = dps.reshape(Wd)
        dP = jnp.concatenate([dq, dk, dv, dgu, dgv, dxp], axis=1)
        (d_wmain,) = _matmul(f"d_wmain_{i}", "tn", (D // tkd, NM // tn),
                             [((sv["xn1"], BS((T, tkd), lambda i, j: (0, i))), (dP, BS((T, tn), lambda i, j: (0, j))))], [],
                             [((D, NM), f32, BS((tkd, tn), lambda i, j: (i, j)))])
        (d_wf,) = _matmul(f"d_wf_{i}", "tn", (D // tnd, 1),
                          [((sv["xn1"], BS((T, tnd), lambda i, j: (0, i))), (dPf, BS((T, HEAD), lambda i, j: (0, 0))))], [],
                          [((D, HEAD), f32, BS((tnd, HEAD), lambda i, j: (i, 0)))])
        def win_cols(lo, hi):
            out = []
            for src, s0, e0, off in ((d_wmain, 0, 3 * A, 0), (d_wf, 3 * A, 3 * A + H, 3 * A), (d_wmain, 3 * A + H, DP, H)):
                s, e = max(lo, s0), min(hi, e0)
                if s < e:
                    out.append(src[:, s - off:e - off])
            return out

        d_win4 = jnp.stack([jnp.concatenate(win_cols(j * DP4, (j + 1) * DP4), axis=1) for j in range(N_CHIPS)])
        order.done(dP)
        unit_m = dict(tag=f"{i}m", layer=i, names=["w_in", "w_out"], grads=[d_win4, d_wout.reshape(N_CHIPS, DS, D)])
        stage_a(unit_m)
        if prev_m is not None:
            stage_sum_adamw(prev_m)
            stage_share(prev_m)
        (dxn1,) = _matmul(f"d_xn1_{i}", "nt", (D // tnd, T // tm),
                          [((dP, BS((tm, NM), lambda j, i: (i, 0))), (w["w_main"], BS((tnd, NM), lambda j, i: (j, 0)))),
                           ((dPf, BS((tm, HEAD), lambda j, i: (i, 0))), (w["w_f"], BS((tnd, HEAD), lambda j, i: (j, 0))))], [],
                          [((T, D), f32, BS((tm, tnd), lambda j, i: (i, j)))], after=order.token)
        order.done(dxn1)
        if prev_m is not None:
            stage_shared(prev_m)
        prev_f, prev_m = unit_f, unit_m

    dh, _, dg = _rms_bwd("rms1_bw_0", dxn1, saved[0]["h0"], order.follows(norm_mix[0]), dh1)
    small_g["norm_mix"][0] = dg.reshape(D)
    order.done(dh)
    stage_pair(prev_m)
    stage_b(prev_m)
    stage_sum(prev_f)
    stage_c(prev_f)

    small_full = {n: jnp.stack(small_g[n]) for n in SMALL}
    small_shapes = [W[n].shape for n in SMALL]
    n_rows = sum(_rows_of(W[n].size) for n in SMALL) + _rows_of(1)
    rows8 = -(-n_rows // 64) * 8
    packed = order.follows(_pack_rows([small_full[n] for n in SMALL] + [loss_tile[0, :1]], N_DEV * rows8))
    summed = _all_reduce_small("allreduce_small", packed.reshape(N_DEV, rows8, 128)).reshape(-1, 128)
    order.done(summed)
    stage_adamw(prev_f)
    stage_sum(prev_m)
    stage_c(prev_m)
    stage_adamw(prev_m)
    *small_grads, loss_row = _unpack_rows(summed, small_shapes + [(1,)])
    grads = dict(zip(SMALL, small_grads))
    loss = loss_row[0]

    wp, mp, vp = (_pack_rows([t[n] for n in SMALL], N_DEV * rows8) for t in (W, M, V))
    delta, new_m, new_v = (dict(zip(SMALL, _unpack_rows(t, small_shapes)))
                           for t in _adamw("adamw_small", wp, summed, mp, vp))
    for n in BIG:
        grads[n], delta[n], new_m[n], new_v[n] = big_out[n]

    return (loss, dh.reshape(1, T, D), *[grads[n] for n in WEIGHTS], *[delta[n] for n in WEIGHTS],
            *[new_m[n] for n in WEIGHTS], *[new_v[n] for n in WEIGHTS])
```

```python
import jax
import jax.numpy as jnp
from jax import lax
from jax.experimental import pallas as pl
from jax.experimental.pallas import tpu as pltpu

f32, bf16 = jnp.float32, jnp.bfloat16
S = jax.ShapeDtypeStruct
BS = pl.BlockSpec
ANY = pl.BlockSpec(memory_space=pl.ANY)
MESH = pl.DeviceIdType.MESH

EPS = 1e-6
HEAD = 128
POOL_WINDOWS = (2, 4, 8, 16)
NEG = -1e30
N_CHIPS = 4
N_DEV = 8
VMEM_LIMIT = 56 * 1024 * 1024

ADAM_LR, ADAM_B1, ADAM_B2, ADAM_EPS, ADAM_WD, ADAM_STEP = 0.001, 0.9, 0.999, 1e-08, 0.01, 10

BIG = ("w_in", "w_out", "w_ffn_gate", "w_ffn_up", "w_ffn_down", "w_ple_gate", "w_ple_proj")
SMALL = ("norm_mix", "q_norm", "k_norm", "forget_bias", "gmlp_v_norm", "gmlp_w_s", "gmlp_b_s", "pool_w",
         "pool_scale", "norm_ffn", "norm_ple")
WEIGHTS = ("norm_mix", "w_in", "q_norm", "k_norm", "forget_bias", "gmlp_v_norm", "gmlp_w_s", "gmlp_b_s", "pool_w",
           "pool_scale", "w_out", "norm_ffn", "w_ffn_gate", "w_ffn_up", "w_ffn_down", "norm_ple", "w_ple_gate",
           "w_ple_proj")


def _tile(n, target, mult):
    best = None
    for t in range(mult, min(n, target) + 1, mult):
        if n % t == 0:
            best = t
    return best if best is not None else n


def _params(**kw):
    return pltpu.CompilerParams(vmem_limit_bytes=VMEM_LIMIT, **kw)


def _dot(a, b, kind):
    dims = {"nn": (((1,), (0,)), ((), ())), "nt": (((1,), (1,)), ((), ())), "tn": (((0,), (0,)), ((), ()))}[kind]
    return lax.dot_general(a.astype(bf16), b.astype(bf16), dims, preferred_element_type=f32)


def _heads_per_program(n_heads):
    return 2 if n_heads % 2 == 0 else 1


def _my_place():
    x, y, c = lax.axis_index("x"), lax.axis_index("y"), lax.axis_index("c")
    return x, y, c, 2 * x + y


def _matmul(name, kind, grid, pairs, extras, outs, epilogue=None, after=None):
    n_p, n_e = len(pairs), len(extras)
    tokens = [] if after is None else [(after, BS((8, 128), lambda *_: (0, 0)))]

    def body(*refs):
        a_refs, b_refs = refs[:n_p], refs[n_p:2 * n_p]
        e_refs = refs[2 * n_p:2 * n_p + n_e]
        o_refs = refs[2 * n_p + n_e + len(tokens):]
        accs = []
        for a_ref, b_ref in zip(a_refs, b_refs):
            if len(b_ref.shape) == 3:
                w = b_ref.shape[2]
                acc = None
                for s in range(b_ref.shape[0]):
                    d = _dot(a_ref[:, s * w:(s + 1) * w], b_ref[s], kind)
                    acc = d if acc is None else acc + d
            else:
                acc = _dot(a_ref[...], b_ref[...], kind)
            accs.append(acc)
        if epilogue is None:
            res = accs[0]
            for t in accs[1:]:
                res = res + t
            res = (res,)
        else:
            res = epilogue(accs, [e[...] for e in e_refs])
        for o_ref, o in zip(o_refs, res):
            o_ref[...] = o.astype(o_ref.dtype)

    in_arrays = [p[0][0] for p in pairs] + [p[1][0] for p in pairs] + [e[0] for e in extras + tokens]
    in_specs = [p[0][1] for p in pairs] + [p[1][1] for p in pairs] + [e[1] for e in extras + tokens]
    res = pl.pallas_call(
        body, name=name, grid=grid, in_specs=in_specs,
        out_specs=[o[2] for o in outs], out_shape=[S(o[0], o[1]) for o in outs],
        compiler_params=_params(),
    )(*in_arrays)
    return res


def _rms_fwd(name, x, g):
    T, D = x.shape
    tr = _tile(T, 256, 8)

    def body(x_ref, g_ref, o_ref):
        xv = x_ref[...]
        r = lax.rsqrt(jnp.mean(xv * xv, axis=-1, keepdims=True) + EPS)
        o_ref[...] = (xv * r * g_ref[...]).astype(o_ref.dtype)

    return pl.pallas_call(
        body, name=name, grid=(T // tr,),
        in_specs=[BS((tr, D), lambda i: (i, 0)), BS((1, D), lambda i: (0, 0))],
        out_specs=BS((tr, D), lambda i: (i, 0)), out_shape=S((T, D), bf16),
    )(x, g.reshape(1, D))


def _rms_bwd(name, dxn, x, g, dres):
    T, D = x.shape
    tr = _tile(T, 256, 8)

    def body(dxn_ref, x_ref, g_ref, dres_ref, dx_ref, dxb_ref, dg_ref):
        i = pl.program_id(0)
        xv = x_ref[...]
        r = lax.rsqrt(jnp.mean(xv * xv, axis=-1, keepdims=True) + EPS)
        xh = xv * r
        dxn_v = dxn_ref[...]
        dxh = dxn_v * g_ref[...]
        dx = dres_ref[...] + r * (dxh - xh * jnp.mean(dxh * xh, axis=-1, keepdims=True))
        dx_ref[...] = dx
        dxb_ref[...] = dx.astype(bf16)
        part = jnp.sum(dxn_v * xh, axis=0, keepdims=True)

        @pl.when(i == 0)
        def _():
            dg_ref[...] = part

        @pl.when(i > 0)
        def _():
            dg_ref[...] += part

    row = BS((tr, D), lambda i: (i, 0))
    vec = BS((1, D), lambda i: (0, 0))
    return pl.pallas_call(
        body, name=name, grid=(T // tr,),
        in_specs=[row, row, vec, row], out_specs=[row, row, vec],
        out_shape=[S((T, D), f32), S((T, D), bf16), S((1, D), f32)],
    )(dxn, x, g.reshape(1, D), dres)


def _loss_grad(name, y, tgt):
    T, D = y.shape
    tr = _tile(T, 256, 8)

    def body(y_ref, t_ref, dy_ref, l_ref):
        i = pl.program_id(0)
        e = y_ref[...] - t_ref[...]
        dy_ref[...] = e * (1.0 / D)
        part = 0.5 * jnp.sum(jnp.mean(e * e, axis=-1, keepdims=True), axis=0, keepdims=True)

        @pl.when(i == 0)
        def _():
            l_ref[...] = jnp.zeros_like(l_ref)

        l_ref[...] += jnp.broadcast_to(part, l_ref.shape)

    row = BS((tr, D), lambda i: (i, 0))
    return pl.pallas_call(
        body, name=name, grid=(T // tr,), in_specs=[row, row],
        out_specs=[row, BS((8, 128), lambda i: (0, 0))],
        out_shape=[S((T, D), f32), S((8, 128), f32)],
    )(y, tgt)


def _ple_bwd_elem(name, dh, gate, e):
    T, D = dh.shape
    tr = _tile(T, 256, 16)

    def body(dh_ref, g_ref, e_ref, de_ref, dz_ref):
        d = dh_ref[...]
        g = g_ref[...].astype(f32)
        de_ref[...] = (d * g).astype(bf16)
        dz_ref[...] = (d * e_ref[...].astype(f32) * g * (1.0 - g)).astype(bf16)

    row = BS((tr, D), lambda i: (i, 0))
    return pl.pallas_call(
        body, name=name, grid=(T // tr,), in_specs=[row, row, row], out_specs=[row, row],
        out_shape=[S((T, D), bf16), S((T, D), bf16)],
    )(dh, gate, e)


def _cast_layer(name, w_all, layer):
    _, R, C = w_all.shape
    lanes = -(-C // 128) * 128
    tr = _tile(R, max(16, (1024 * 1024) // lanes // 16 * 16), 16)

    def body(w_ref, o_ref):
        o_ref[...] = w_ref[...].astype(bf16)

    return pl.pallas_call(
        body, name=name, grid=(R // tr,), in_specs=[BS((None, tr, C), lambda r: (layer, r, 0))],
        out_specs=BS((tr, C), lambda r: (r, 0)), out_shape=S((R, C), bf16),
    )(w_all)


def _gelu_and_grad(x):
    k0, k1 = 0.7978845608028654, 0.044715
    th = jnp.tanh(k0 * (x + k1 * x * x * x))
    val = 0.5 * x * (1.0 + th)
    grad = 0.5 * (1.0 + th) + 0.5 * x * (1.0 - th * th) * (k0 * (1.0 + 3.0 * k1 * x * x))
    return val, grad


def _fgate_fwd(name, pf, fb):
    T = pf.shape[0]

    def body(pf_ref, fb_ref, c_ref, ct_ref):
        xv = jax.nn.log_sigmoid(pf_ref[...] + fb_ref[...])
        row = lax.broadcasted_iota(jnp.int32, xv.shape, 0)
        s = 1
        while s < T:
            xv = xv + jnp.where(row >= s, pltpu.roll(xv, s, 0), 0.0)
            s *= 2
        c_ref[...] = xv
        ct_ref[...] = xv.T

    return pl.pallas_call(body, name=name, out_shape=[S((T, HEAD), f32), S((HEAD, T), f32)])(pf, fb)


def _fgate_bwd(name, dct, pf, fb):
    T = pf.shape[0]

    def body(dct_ref, pf_ref, fb_ref, dpf_ref, dfb_ref):
        xv = dct_ref[...].T
        row = lax.broadcasted_iota(jnp.int32, xv.shape, 0)
        s = 1
        while s < T:
            xv = xv + jnp.where(row + s < T, pltpu.roll(xv, T - s, 0), 0.0)
            s *= 2
        df = xv * jax.nn.sigmoid(-(pf_ref[...] + fb_ref[...]))
        dpf_ref[...] = df.astype(bf16)
        dfb_ref[...] = jnp.sum(df, axis=0, keepdims=True)

    return pl.pallas_call(body, name=name, out_shape=[S((T, HEAD), bf16), S((1, HEAD), f32)])(dct, pf, fb)


def _qk_norm(name, P, qg, kg, A):
    T = P.shape[0]
    tr = _tile(T, 256, 16)
    n_heads = A // HEAD

    def body(q_ref, k_ref, v_ref, qg_ref, kg_ref, qn_ref, kn_ref, vb_ref):
        for h in range(n_heads):
            sl = slice(h * HEAD, (h + 1) * HEAD)
            for src, g_ref, dst in ((q_ref, qg_ref, qn_ref), (k_ref, kg_ref, kn_ref)):
                xv = src[:, sl]
                r = lax.rsqrt(jnp.mean(xv * xv, axis=-1, keepdims=True) + EPS)
                dst[:, sl] = (xv * r * g_ref[...]).astype(bf16)
        vb_ref[...] = v_ref[...].astype(bf16)

    vec = BS((1, HEAD), lambda i: (0, 0))
    out = BS((tr, A), lambda i: (i, 0))
    return pl.pallas_call(
        body, name=name, grid=(T // tr,),
        in_specs=[BS((tr, A), lambda i: (i, 0)), BS((tr, A), lambda i: (i, 1)), BS((tr, A), lambda i: (i, 2)), vec, vec],
        out_specs=[out, out, out], out_shape=[S((T, A), bf16)] * 3,
    )(P, P, P, qg, kg)


def _attn_fwd(name, qn, kn, vb, c_col, c_row, tb):
    T, A = qn.shape
    H = A // HEAD
    nb = T // tb
    scale = HEAD ** -0.5
    hp = _heads_per_program(H)
    wide = hp * HEAD

    def body(q_ref, k_ref, v_ref, cq_ref, ck_ref, o_ref, o32_ref, lse_ref):
        i = pl.program_id(1)
        below = lax.broadcasted_iota(jnp.int32, (tb, tb), 0) >= lax.broadcasted_iota(jnp.int32, (tb, tb), 1)

        def block(j, carry, diagonal):
            koff = pl.multiple_of(j * tb, tb)
            out = []
            for hh in range(hp):
                m, l, acc = carry[hh]
                sl = slice(hh * HEAD, (hh + 1) * HEAD)
                k = k_ref[pl.ds(koff, tb), sl]
                v = v_ref[pl.ds(koff, tb), sl]
                s = _dot(q_ref[:, sl], k, "nt") * scale + (cq_ref[hh] - ck_ref[hh, j])
                if diagonal:
                    s = jnp.where(below, s, NEG)
                m_new = jnp.maximum(m, jnp.max(s, axis=-1, keepdims=True))
                alpha = jnp.exp(m - m_new)
                p = jnp.exp(s - m_new)
                l = l * alpha + jnp.sum(p, axis=-1, keepdims=True)
                acc = acc * alpha + _dot(p, v, "nn")
                out.append((m_new, l, acc))
            return tuple(out)

        init = tuple((jnp.full((tb, 1), NEG, f32), jnp.zeros((tb, 1), f32), jnp.zeros((tb, HEAD), f32))
                     for _ in range(hp))
        carry = lax.fori_loop(0, i, lambda j, c: block(j, c, False), init)
        carry = block(i, carry, True)
        for hh in range(hp):
            m, l, acc = carry[hh]
            sl = slice(hh * HEAD, (hh + 1) * HEAD)
            o = acc / l
            o_ref[:, sl] = o.astype(bf16)
            o32_ref[:, sl] = o
            lse_ref[hh] = m + jnp.log(l)

    return pl.pallas_call(
        body, name=name, grid=(H // hp, nb),
        in_specs=[BS((tb, wide), lambda h, i: (i, h)), BS((T, wide), lambda h, i: (0, h)),
                  BS((T, wide), lambda h, i: (0, h)), BS((hp, tb, 1), lambda h, i: (h, i, 0)),
                  BS((hp, nb, 1, tb), lambda h, i: (h, 0, 0, 0))],
        out_specs=[BS((tb, wide), lambda h, i: (i, h)), BS((tb, wide), lambda h, i: (i, h)),
                   BS((hp, tb, 1), lambda h, i: (h, i, 0))],
        out_shape=[S((T, A), bf16), S((T, A), f32), S((H, T, 1), f32)],
    )(qn, kn, vb, c_col, c_row)


def _attn_bwd(name, qn, kn, vb, o, dmix, lse, c_col, c_row, P, qg, kg, tb):
    T, A = qn.shape
    H = A // HEAD
    nb = T // tb
    scale = HEAD ** -0.5
    hp = _heads_per_program(H)
    wide = hp * HEAD

    def body(q_ref, k_ref, v_ref, o_ref, do_ref, lse_ref, cq_ref, ck_ref, qraw_ref, kraw_ref, qg_ref, kg_ref,
             dq_out, dk_out, dv_out, dc_out, dqg_out, dkg_out, dq_acc, dk_acc, delta_s):
        h = pl.program_id(0)
        dq_acc[...] = jnp.zeros_like(dq_acc)
        below = lax.broadcasted_iota(jnp.int32, (tb, tb), 0) >= lax.broadcasted_iota(jnp.int32, (tb, tb), 1)
        for hh in range(hp):
            sl = slice(hh * HEAD, (hh + 1) * HEAD)
            delta_s[hh] = jnp.sum(do_ref[:, sl].astype(bf16).astype(f32) * o_ref[:, sl], axis=-1, keepdims=True)

        def kblock(j, _):
            koff = pl.multiple_of(j * tb, tb)

            def products(i, hh):
                sl = slice(hh * HEAD, (hh + 1) * HEAD)
                qoff = pl.multiple_of(i * tb, tb)
                return (_dot(q_ref[pl.ds(qoff, tb), sl], k_ref[pl.ds(koff, tb), sl], "nt"),
                        _dot(do_ref[pl.ds(qoff, tb), sl], v_ref[pl.ds(koff, tb), sl], "nt"))

            def qblock(i, carry, diagonal):
                qoff = pl.multiple_of(i * tb, tb)
                out = []
                for hh in range(hp):
                    dk, dv, dc, qk_i, dp = carry[hh]
                    sl = slice(hh * HEAD, (hh + 1) * HEAD)
                    ahead = products(jnp.minimum(i + 1, nb - 1), hh)
                    k = k_ref[pl.ds(koff, tb), sl]
                    q = q_ref[pl.ds(qoff, tb), sl]
                    do = do_ref[pl.ds(qoff, tb), sl].astype(bf16)
                    s = qk_i * scale + (cq_ref[hh, pl.ds(qoff, tb), :] - ck_ref[hh, j])
                    if diagonal:
                        s = jnp.where(below, s, NEG)
                    p = jnp.exp(s - lse_ref[hh, pl.ds(qoff, tb), :])
                    dv = dv + _dot(p, do, "tn")
                    ds = p * (dp - delta_s[hh, pl.ds(qoff, tb), :])
                    dc = dc - jnp.sum(ds, axis=0, keepdims=True)
                    dsb = (ds * scale).astype(bf16)
                    dk = dk + _dot(dsb, q, "tn")
                    dq_acc[pl.ds(qoff, tb), sl] += _dot(dsb, k, "nn")
                    out.append((dk, dv, dc, *ahead))
                return tuple(out)

            init = tuple((jnp.zeros((tb, HEAD), f32), jnp.zeros((tb, HEAD), f32), jnp.zeros((1, tb), f32),
                          *products(j, hh)) for hh in range(hp))
            carry = qblock(j, init, True)
            carry = lax.fori_loop(j + 1, nb, lambda i, c: qblock(i, c, False), carry)
            for hh in range(hp):
                dk, dv, dc = carry[hh][:3]
                sl = slice(hh * HEAD, (hh + 1) * HEAD)
                dk_acc[pl.ds(koff, tb), sl] = dk
                dv_out[pl.ds(koff, tb), sl] = dv.astype(bf16)
                dc_out[hh, j] = dc
            return 0

        lax.fori_loop(0, nb, kblock, 0)

        for raw_ref, g_ref, acc_ref, d_out, dg_out in ((qraw_ref, qg_ref, dq_acc, dq_out, dqg_out),
                                                       (kraw_ref, kg_ref, dk_acc, dk_out, dkg_out)):
            part = jnp.zeros((1, HEAD), f32)
            for hh in range(hp):
                sl = slice(hh * HEAD, (hh + 1) * HEAD)
                xv = raw_ref[:, sl]
                r = lax.rsqrt(jnp.mean(xv * xv, axis=-1, keepdims=True) + EPS)
                xh = xv * r
                dn = acc_ref[:, sl]
                dxh = dn * g_ref[...]
                d_out[:, sl] = (r * (dxh - xh * jnp.mean(dxh * xh, axis=-1, keepdims=True))).astype(bf16)
                part = part + jnp.sum(dn * xh, axis=0, keepdims=True)

            @pl.when(h == 0)
            def _():
                dg_out[...] = part

            @pl.when(h > 0)
            def _():
                dg_out[...] += part

    heads = lambda off: BS((T, wide), lambda h: (0, off + h))
    col = BS((hp, T, 1), lambda h: (h, 0, 0))
    row = BS((hp, nb, 1, tb), lambda h: (h, 0, 0, 0))
    vec = BS((1, HEAD), lambda h: (0, 0))
    return pl.pallas_call(
        body, name=name, grid=(H // hp,),
        in_specs=[heads(0), heads(0), heads(0), heads(0), heads(0), col, col, row, heads(0), heads(H // hp), vec, vec],
        out_specs=[heads(0), heads(0), heads(0), row, vec, vec],
        out_shape=[S((T, A), bf16)] * 3 + [S((H, nb, 1, tb), f32), S((1, HEAD), f32), S((1, HEAD), f32)],
        scratch_shapes=[pltpu.VMEM((T, wide), f32), pltpu.VMEM((T, wide), f32), pltpu.VMEM((hp, T, 1), f32)],
        compiler_params=_params(),
    )(qn, kn, vb, o, dmix, lse, c_col, c_row, P, P, qg, kg)


def _gmlp_fwd(name, P, gain, ws, b, col_u, col_v, Wd):
    T = P.shape[0]
    G = Wd // HEAD
    tr = _tile(T, 512, HEAD)

    def body(u_ref, v_ref, gain_ref, ws_ref, b_ref, y_ref):
        tril = lax.broadcasted_iota(jnp.int32, (HEAD, HEAD), 0) >= lax.broadcasted_iota(jnp.int32, (HEAD, HEAD), 1)
        wm = jnp.where(tril, ws_ref[...], 0.0).astype(bf16)
        for n in range(tr // HEAD):
            rows = slice(n * HEAD, (n + 1) * HEAD)
            u = jax.nn.gelu(u_ref[rows, :])
            a = jax.nn.gelu(v_ref[rows, :])
            r = lax.rsqrt(jnp.mean(a * a, axis=-1, keepdims=True) + EPS)
            vn = a * r * gain_ref[...]
            mixed = _dot(wm, vn, "nn") + b_ref[...]
            y_ref[rows, :] = (u * mixed).astype(bf16)

    return pl.pallas_call(
        body, name=name, grid=(G, T // tr),
        in_specs=[BS((tr, HEAD), lambda g, i: (i, col_u + g)), BS((tr, HEAD), lambda g, i: (i, col_v + g)),
                  BS((None, 1, HEAD), lambda g, i: (g, 0, 0)), BS((None, HEAD, HEAD), lambda g, i: (g, 0, 0)),
                  BS((None, HEAD, 1), lambda g, i: (g, 0, 0))],
        out_specs=BS((tr, HEAD), lambda g, i: (i, g)), out_shape=S((T, Wd), bf16),
    )(P, P, gain, ws, b)


def _gmlp_bwd(name, P, dmix, gain, ws, b, col_u, col_v, col_dy, Wd):
    T = P.shape[0]
    G = Wd // HEAD
    tr = _tile(T, 512, HEAD)

    def body(u_ref, v_ref, dy_ref, gain_ref, ws_ref, b_ref, du_ref, dv_ref, dws_ref, db_ref, dgain_ref):
        i = pl.program_id(1)
        tril = lax.broadcasted_iota(jnp.int32, (HEAD, HEAD), 0) >= lax.broadcasted_iota(jnp.int32, (HEAD, HEAD), 1)
        wm = jnp.where(tril, ws_ref[...], 0.0).astype(bf16)
        gain_v = gain_ref[...]
        dw = jnp.zeros((HEAD, HEAD), f32)
        db = jnp.zeros((HEAD, 1), f32)
        dgain = jnp.zeros((1, HEAD), f32)
        for n in range(tr // HEAD):
            rows = slice(n * HEAD, (n + 1) * HEAD)
            u, du_dx = _gelu_and_grad(u_ref[rows, :])
            a, da_dx = _gelu_and_grad(v_ref[rows, :])
            dy = dy_ref[rows, :]
            r = lax.rsqrt(jnp.mean(a * a, axis=-1, keepdims=True) + EPS)
            ah = a * r
            vnb = (ah * gain_v).astype(bf16)
            mixed = _dot(wm, vnb, "nn") + b_ref[...]
            dm = dy * u
            dmb = dm.astype(bf16)
            du_ref[rows, :] = (dy * mixed * du_dx).astype(bf16)
            db = db + jnp.sum(dm, axis=1, keepdims=True)
            dw = dw + _dot(dmb, vnb, "nt")
            dvn = _dot(wm, dmb, "tn")
            dgain = dgain + jnp.sum(dvn * ah, axis=0, keepdims=True)
            dah = dvn * gain_v
            da = r * (dah - ah * jnp.mean(dah * ah, axis=-1, keepdims=True))
            dv_ref[rows, :] = (da * da_dx).astype(bf16)
        dw = jnp.where(tril, dw, 0.0)

        @pl.when(i == 0)
        def _():
            dws_ref[...] = dw
            db_ref[...] = db
            dgain_ref[...] = dgain

        @pl.when(i > 0)
        def _():
            dws_ref[...] += dw
            db_ref[...] += db
            dgain_ref[...] += dgain

    out = BS((tr, HEAD), lambda g, i: (i, g))
    return pl.pallas_call(
        body, name=name, grid=(G, T // tr),
        in_specs=[BS((tr, HEAD), lambda g, i: (i, col_u + g)), BS((tr, HEAD), lambda g, i: (i, col_v + g)),
                  BS((tr, HEAD), lambda g, i: (i, col_dy + g)),
                  BS((None, 1, HEAD), lambda g, i: (g, 0, 0)), BS((None, HEAD, HEAD), lambda g, i: (g, 0, 0)),
                  BS((None, HEAD, 1), lambda g, i: (g, 0, 0))],
        out_specs=[out, out, BS((None, HEAD, HEAD), lambda g, i: (g, 0, 0)), BS((None, HEAD, 1), lambda g, i: (g, 0, 0)),
                   BS((None, 1, HEAD), lambda g, i: (g, 0, 0))],
        out_shape=[S((T, Wd), bf16), S((T, Wd), bf16), S((G, HEAD, HEAD), f32), S((G, HEAD, 1), f32),
                   S((G, 1, HEAD), f32)],
    )(P, P, dmix, gain, ws, b)


def _trailing_window(xv, w, row):
    k = 1
    while k < w:
        xv = xv + jnp.where(row >= k, pltpu.roll(xv, k, 0), 0.0)
        k *= 2
    return xv


def _leading_window(xv, w, row, T):
    k = 1
    while k < w:
        xv = xv + jnp.where(row + k < T, pltpu.roll(xv, T - k, 0), 0.0)
        k *= 2
    return xv


def _pool_fwd(name, P, pw, ps, col_x, Wd):
    T = P.shape[0]
    Gp = Wd // HEAD

    def body(x_ref, pw_ref, ps_ref, y_ref):
        row = lax.broadcasted_iota(jnp.int32, (T, HEAD), 0)
        for g in range(Gp):
            w = POOL_WINDOWS[g]
            sl = slice(g * HEAD, (g + 1) * HEAD)
            xv = x_ref[:, sl]
            cnt = jnp.minimum(row + 1, w).astype(f32)
            d = _trailing_window(xv, w, row) / cnt - xv
            y_ref[:, sl] = (_dot(d, pw_ref[g], "nn") * ps_ref[:, sl]).astype(bf16)

    return pl.pallas_call(
        body, name=name, grid=(1,),
        in_specs=[BS((T, Wd), lambda i: (0, col_x)), BS((Gp, HEAD, HEAD), lambda i: (0, 0, 0)), BS((1, Wd), lambda i: (0, 0))],
        out_specs=BS((T, Wd), lambda i: (0, 0)), out_shape=S((T, Wd), bf16), compiler_params=_params(),
    )(P, pw, ps)


def _pool_bwd(name, P, dmix, pw, ps, col_x, col_dy, Wd):
    T = P.shape[0]
    Gp = Wd // HEAD

    def body(x_ref, dy_ref, pw_ref, ps_ref, dx_ref, dpw_ref, dps_ref):
        row = lax.broadcasted_iota(jnp.int32, (T, HEAD), 0)
        for g in range(Gp):
            w = POOL_WINDOWS[g]
            sl = slice(g * HEAD, (g + 1) * HEAD)
            xv = x_ref[:, sl]
            cnt = jnp.minimum(row + 1, w).astype(f32)
            d = (_trailing_window(xv, w, row) / cnt - xv).astype(bf16)
            pwb = pw_ref[g].astype(bf16)
            z = _dot(d, pwb, "nn")
            dy = dy_ref[:, sl]
            dps_ref[:, sl] = jnp.sum(dy * z, axis=0, keepdims=True)
            dzb = (dy * ps_ref[:, sl]).astype(bf16)
            dpw_ref[g] = _dot(d, dzb, "tn")
            dd = _dot(dzb, pwb, "nt")
            dx_ref[:, sl] = (_leading_window(dd / cnt, w, row, T) - dd).astype(bf16)

    return pl.pallas_call(
        body, name=name, grid=(1,),
        in_specs=[BS((T, Wd), lambda i: (0, col_x)), BS((T, Wd), lambda i: (0, col_dy)),
                  BS((Gp, HEAD, HEAD), lambda i: (0, 0, 0)), BS((1, Wd), lambda i: (0, 0))],
        out_specs=[BS((T, Wd), lambda i: (0, 0)), BS((Gp, HEAD, HEAD), lambda i: (0, 0, 0)), BS((1, Wd), lambda i: (0, 0))],
        out_shape=[S((T, Wd), bf16), S((Gp, HEAD, HEAD), f32), S((1, Wd), f32)], compiler_params=_params(),
    )(P, dmix, pw, ps)


def _adamw(name, w, g, m, v):
    R, C = w.shape
    lanes = -(-C // 128) * 128
    tr = _tile(R, max(8, (512 * 1024) // lanes // 8 * 8), 8)
    c1 = 1.0 - ADAM_B1 ** ADAM_STEP
    c2 = 1.0 - ADAM_B2 ** ADAM_STEP

    def body(w_ref, g_ref, m_ref, v_ref, d_ref, nm_ref, nv_ref):
        gv = g_ref[...]
        nm = ADAM_B1 * m_ref[...] + (1.0 - ADAM_B1) * gv
        nv = ADAM_B2 * v_ref[...] + (1.0 - ADAM_B2) * (gv * gv)
        d_ref[...] = -ADAM_LR * ((nm / c1) / (jnp.sqrt(nv / c2) + ADAM_EPS) + ADAM_WD * w_ref[...])
        nm_ref[...] = nm
        nv_ref[...] = nv

    blk = BS((tr, C), lambda i: (i, 0))
    return pl.pallas_call(
        body, name=name, grid=(R // tr,), in_specs=[blk] * 4, out_specs=[blk] * 3, out_shape=[S((R, C), f32)] * 3,
    )(w, g, m, v)


def _adamw_layer(name, layer, w_all, m_all, v_all, g, prev):
    L, R, C = w_all.shape
    lanes = -(-C // 128) * 128
    tr = _tile(R, max(8, (512 * 1024) // lanes // 8 * 8), 8)
    c1 = 1.0 - ADAM_B1 ** ADAM_STEP
    c2 = 1.0 - ADAM_B2 ** ADAM_STEP
    n_prev = 0 if prev is None else 4

    def body(w_ref, m_ref, v_ref, g_ref, *rest):
        go_ref, d_ref, nm_ref, nv_ref = rest[n_prev:]
        gv = g_ref[...]
        nm = ADAM_B1 * m_ref[...] + (1.0 - ADAM_B1) * gv
        nv = ADAM_B2 * v_ref[...] + (1.0 - ADAM_B2) * (gv * gv)
        d_ref[...] = -ADAM_LR * ((nm / c1) / (jnp.sqrt(nv / c2) + ADAM_EPS) + ADAM_WD * w_ref[...])
        nm_ref[...] = nm
        nv_ref[...] = nv
        go_ref[...] = gv

    slab = BS((None, tr, C), lambda r: (layer, r, 0))
    return pl.pallas_call(
        body, name=name, grid=(R // tr,),
        in_specs=[slab, slab, slab, BS((tr, C), lambda r: (r, 0))] + [ANY] * n_prev,
        out_specs=[slab] * 4, out_shape=[S((L, R, C), f32)] * 4,
        input_output_aliases={4 + k: k for k in range(n_prev)},
    )(w_all, m_all, v_all, g, *(prev or ()))


def _chip_of(k):
    return k // 2, k % 2


def _remote(src, dst, send_sems, recv_sems, idx, dev):
    return pltpu.make_async_remote_copy(src_ref=src, dst_ref=dst, send_sem=send_sems.at[idx], recv_sem=recv_sems.at[idx],
                                        device_id=dev, device_id_type=MESH)


def _plan_gather_near(n):
    def plan(refs, ss, rs, base):
        ins, lands = refs[:n], refs[n:]
        x, y, c, j0 = _my_place()
        sib = (x, y, 1 - c)
        sends, recvs = [], []
        for a in range(n):
            half = ins[a].shape[0] // 2
            lo = c * half
            sends.append(_remote(ins[a], lands[a].at[j0], ss, rs, base + 3 * a + 2, sib))
            recvs.append(_remote(lands[a].at[j0], lands[a].at[j0], ss, rs, base + 3 * a + 2, sib))
            for r in (1, 2):
                k = j0 ^ r
                dev = (*_chip_of(k), c)
                sends.append(_remote(ins[a].at[pl.ds(lo, half)], lands[a].at[j0, pl.ds(lo, half)], ss, rs,
                                     base + 3 * a + r - 1, dev))
                landed = lands[a].at[k, pl.ds(lo, half)]
                recvs.append(_remote(landed, landed, ss, rs, base + 3 * a + r - 1, dev))
        return sends, recvs
    return plan, 3 * n


def _plan_gather_relay(n):
    def plan(refs, ss, rs, base):
        x, y, c, j0 = _my_place()
        sib = (x, y, 1 - c)
        sends, recvs = [], []
        for a in range(n):
            half = refs[a].shape[1] // 2
            quarter = half // 2
            lo = c * half
            far = j0 ^ 3
            for r, to, off in ((1, 2, 0), (2, 1, quarter)):
                dev = (*_chip_of(j0 ^ to), c)
                piece = refs[a].at[j0 ^ r, pl.ds(lo + off, quarter)]
                sends.append(_remote(piece, piece, ss, rs, base + 4 * a + to - 1, dev))
                lands_here = refs[a].at[far, pl.ds(lo + off, quarter)]
                recvs.append(_remote(lands_here, lands_here, ss, rs, base + 4 * a + to - 1, dev))
                mine = refs[a].at[j0 ^ r, pl.ds(lo, half)]
                theirs = refs[a].at[j0 ^ r, pl.ds((1 - c) * half, half)]
                sends.append(_remote(mine, mine, ss, rs, base + 4 * a + 1 + r, sib))
                recvs.append(_remote(theirs, theirs, ss, rs, base + 4 * a + 1 + r, sib))
        return sends, recvs
    return plan, 4 * n


def _plan_gather_far(n):
    def plan(refs, ss, rs, base):
        x, y, c, j0 = _my_place()
        sib = (x, y, 1 - c)
        sends, recvs = [], []
        for a in range(n):
            half = refs[a].shape[1] // 2
            mine = refs[a].at[j0 ^ 3, pl.ds(c * half, half)]
            theirs = refs[a].at[j0 ^ 3, pl.ds((1 - c) * half, half)]
            sends.append(_remote(mine, mine, ss, rs, base + a, sib))
            recvs.append(_remote(theirs, theirs, ss, rs, base + a, sib))
        return sends, recvs
    return plan, n


def _plan_sibling_halves(n):
    def plan(refs, ss, rs, base):
        ins, lands = refs[:n], refs[n:]
        x, y, c, _ = _my_place()
        sib = (x, y, 1 - c)
        sends, recvs = [], []
        for a in range(n):
            half = ins[a].shape[1] // 2
            sends.append(_remote(ins[a].at[:, pl.ds((1 - c) * half, half), :], lands[a], ss, rs, base + a, sib))
            recvs.append(_remote(lands[a], lands[a], ss, rs, base + a, sib))
        return sends, recvs
    return plan, n


def _plan_chip_scatter(n):
    def plan(refs, ss, rs, base):
        ins, lands = refs[:n], refs[n:]
        x, y, c, j0 = _my_place()
        sends, recvs = [], []
        for a in range(n):
            for r in (1, 2, 3):
                k = j0 ^ r
                dev = (*_chip_of(k), c)
                sends.append(_remote(ins[a].at[k], lands[a].at[j0], ss, rs, base + 3 * a + r - 1, dev))
                recvs.append(_remote(lands[a].at[k], lands[a].at[k], ss, rs, base + 3 * a + r - 1, dev))
        return sends, recvs
    return plan, 3 * n


def _plan_sibling_join(n):
    def plan(refs, ss, rs, base):
        x, y, c, _ = _my_place()
        sib = (x, y, 1 - c)
        sends, recvs = [], []
        for a in range(n):
            half = refs[a].shape[0] // 2
            mine = refs[a].at[pl.ds(c * half, half)]
            theirs = refs[a].at[pl.ds((1 - c) * half, half)]
            sends.append(_remote(mine, mine, ss, rs, base + a, sib))
            recvs.append(_remote(theirs, theirs, ss, rs, base + a, sib))
        return sends, recvs
    return plan, n


_HBM = pl.BlockSpec(memory_space=pltpu.HBM)
_SEM = pl.BlockSpec(memory_space=pltpu.SEMAPHORE)
_EFFECT = pltpu.SideEffectType.DATAFLOW_SIDE_EFFECTING


def _exchange_start(name, plan, bufs, after):
    plan_fn, n_sems = plan
    n = len(bufs)

    def body(*refs):
        ss, rs, token = refs[n + len(after)], refs[n + len(after) + 1], refs[-1]
        sends, _ = plan_fn(refs[:n], ss, rs, 0)
        for cp in sends:
            cp.start()
        token[...] = jnp.zeros_like(token)

    res = pl.pallas_call(
        body, name=name,
        out_shape=(pltpu.SemaphoreType.DMA((n_sems,)), pltpu.SemaphoreType.DMA((n_sems,)),
                   *[pltpu.HBM(b.shape, b.dtype) for b in bufs], S((8, 128), f32)),
        in_specs=[_HBM] * n + [ANY] * len(after),
        out_specs=(_SEM, _SEM, *[_HBM] * n, pl.BlockSpec(memory_space=pltpu.VMEM)),
        input_output_aliases={k: 2 + k for k in range(n)},
        compiler_params=pltpu.CompilerParams(has_side_effects=_EFFECT),
    )(*[pltpu.with_memory_space_constraint(b, pltpu.HBM) for b in bufs], *after)
    return res[0], res[1], list(res[2:2 + n]), res[-1]


def _exchange_wait(name, plan, send_sems, recv_sems, bufs, after):
    plan_fn, _ = plan
    n = len(bufs)

    def body(*refs):
        ss, rs, token = refs[n], refs[n + 1], refs[-1]
        sends, recvs = plan_fn(refs[:n], ss, rs, 0)
        for cp in recvs:
            cp.wait_recv()
        for cp in sends:
            cp.wait_send()
        token[...] = jnp.zeros_like(token)

    res = pl.pallas_call(
        body, name=name,
        out_shape=(*[pltpu.HBM(b.shape, b.dtype) for b in bufs], S((8, 128), f32)),
        in_specs=[_HBM] * n + [_SEM, _SEM] + [ANY] * len(after),
        out_specs=(*[_HBM] * n, pl.BlockSpec(memory_space=pltpu.VMEM)),
        input_output_aliases={k: k for k in range(n)},
        compiler_params=pltpu.CompilerParams(has_side_effects=_EFFECT),
    )(*bufs, send_sems, recv_sems, *after)
    return list(res[:n]), res[-1]


class _Order:
    def __init__(self, first):
        self.marker = first
        self.token = None

    def _after(self):
        return [self.marker] + ([] if self.token is None else [self.token])

    def start(self, name, plan, bufs):
        ss, rs, thru, self.token = _exchange_start(name, plan, bufs, self._after())
        return name, plan, ss, rs, thru

    def wait(self, handle):
        name, plan, ss, rs, thru = handle
        out, self.token = _exchange_wait(name + "_wait", plan, ss, rs, thru, self._after())
        return out

    def follows(self, small):
        return small if self.token is None else small + self.token[0, 0]

    def done(self, result):
        self.marker = result[(slice(0, 1),) * result.ndim].reshape(1, 1)


def _all_reduce_small(name, g8):
    _, R, L = g8.shape

    def body(g_ref, out_ref, land, red, send1, recv1, send2, recv2):
        x, y, c, _ = _my_place()
        me = 4 * x + 2 * y + c
        peers = []
        for r in range(1, N_DEV):
            q = me ^ r
            peers.append((q, (q // 4, (q // 2) % 2, q % 2)))
        first = []
        for r, (q, dev) in enumerate(peers):
            cp = pltpu.make_async_remote_copy(src_ref=g_ref.at[q], dst_ref=land.at[me], send_sem=send1.at[r],
                                              recv_sem=recv1.at[r], device_id=dev, device_id_type=MESH)
            cp.start()
            first.append(cp)
        land[me] = g_ref[me]
        for r, (q, dev) in enumerate(peers):
            pltpu.make_async_remote_copy(src_ref=land.at[q], dst_ref=land.at[q], send_sem=send1.at[r],
                                         recv_sem=recv1.at[r], device_id=dev, device_id_type=MESH).wait_recv()
        acc = land[0]
        for d in range(1, N_DEV):
            acc = acc + land[d]
        red[...] = acc
        out_ref[me] = acc
        second = []
        for r, (q, dev) in enumerate(peers):
            cp = pltpu.make_async_remote_copy(src_ref=red, dst_ref=out_ref.at[me], send_sem=send2.at[r],
                                              recv_sem=recv2.at[r], device_id=dev, device_id_type=MESH)
            cp.start()
            second.append(cp)
        for r, (q, dev) in enumerate(peers):
            pltpu.make_async_remote_copy(src_ref=out_ref.at[q], dst_ref=out_ref.at[q], send_sem=send2.at[r],
                                         recv_sem=recv2.at[r], device_id=dev, device_id_type=MESH).wait_recv()
        for cp in first + second:
            cp.wait_send()

    vm = pl.BlockSpec(memory_space=pltpu.VMEM)
    return pl.pallas_call(
        body, name=name, in_specs=[vm], out_specs=vm, out_shape=S(g8.shape, f32),
        scratch_shapes=[pltpu.VMEM((N_DEV, R, L), f32), pltpu.VMEM((R, L), f32)]
        + [pltpu.SemaphoreType.DMA((N_DEV - 1,))] * 4,
        compiler_params=_params(),
    )(g8)


def _pair_sum(name, g4, sib):
    _, rows, cols = g4.shape
    half = rows // 2
    lanes = -(-cols // 128) * 128
    tr = _tile(half, max(16, (512 * 1024) // lanes // 16 * 16), 16)
    nb = half // tr

    def body(g_ref, s_ref, pb_ref, own_ref):
        j = pl.program_id(1)
        t = g_ref[...] + s_ref[...]
        pb_ref[...] = t.astype(bf16)

        @pl.when(j == _my_place()[3])
        def _():
            own_ref[...] = t

    return pl.pallas_call(
        body, name=name, grid=(nb, N_CHIPS),
        in_specs=[BS((None, tr, cols), lambda i, j: (j, lax.axis_index("c") * nb + i, 0)),
                  BS((None, tr, cols), lambda i, j: (j, i, 0))],
        out_specs=[BS((None, tr, cols), lambda i, j: (j, i, 0)), BS((tr, cols), lambda i, j: (i, 0))],
        out_shape=[S((N_CHIPS, half, cols), bf16), S((half, cols), f32)],
    )(g4, sib)


def _chip_sum(name, own, got):
    half, cols = own.shape
    lanes = -(-cols // 128) * 128
    tr = _tile(half, max(16, (512 * 1024) // lanes // 16 * 16), 16)
    nb = half // tr

    def body(own_ref, *rest):
        got_refs, o_ref = rest[:N_CHIPS], rest[N_CHIPS]
        j0 = _my_place()[3]
        acc = None
        for k in range(N_CHIPS):
            t = jnp.where(j0 == k, own_ref[...], got_refs[k][...].astype(f32))
            acc = t if acc is None else acc + t
        o_ref[...] = acc

    def slot(k):
        return BS((None, tr, cols), lambda i: (jnp.where(_my_place()[3] == k, (k + 1) % N_CHIPS, k), i, 0))

    return pl.pallas_call(
        body, name=name, grid=(nb,),
        in_specs=[BS((tr, cols), lambda i: (i, 0))] + [slot(k) for k in range(N_CHIPS)],
        out_specs=BS((tr, cols), lambda i: (lax.axis_index("c") * nb + i, 0)),
        out_shape=S((2 * half, cols), f32),
    )(own, got, got, got, got)


def _rows_of(size):
    return -(-size // 1024) * 8


def _pack_rows(arrs, n_rows):
    parts = []
    for a in arrs:
        rows = _rows_of(a.size)
        if a.size % 128 == 0:
            part = a.astype(f32).reshape(-1, 128)
            part = jnp.pad(part, ((0, rows - part.shape[0]), (0, 0)))
        else:
            part = jnp.pad(a.reshape(-1).astype(f32), (0, rows * 128 - a.size)).reshape(rows, 128)
        parts.append(part)
    used = sum(p.shape[0] for p in parts)
    return jnp.concatenate(parts + [jnp.zeros((n_rows - used, 128), f32)], axis=0)


def _unpack_rows(packed, shapes):
    out, row = [], 0
    for shp in shapes:
        size = 1
        for d in shp:
            size *= d
        rows = packed[row:row + _rows_of(size)]
        out.append(rows[:size // 128].reshape(shp) if size % 128 == 0 else rows.reshape(-1)[:size].reshape(shp))
        row += _rows_of(size)
    return out


def kernel(x, p, norm_mix, w_in, q_norm, k_norm, forget_bias, gmlp_v_norm, gmlp_w_s, gmlp_b_s, pool_w, pool_scale, w_out, norm_ffn, w_ffn_gate, w_ffn_up, w_ffn_down, norm_ple, w_ple_gate, w_ple_proj, loss_target, m_norm_mix, m_w_in, m_q_norm, m_k_norm, m_forget_bias, m_gmlp_v_norm, m_gmlp_w_s, m_gmlp_b_s, m_pool_w, m_pool_scale, m_w_out, m_norm_ffn, m_w_ffn_gate, m_w_ffn_up, m_w_ffn_down, m_norm_ple, m_w_ple_gate, m_w_ple_proj, v_norm_mix, v_w_in, v_q_norm, v_k_norm, v_forget_bias, v_gmlp_v_norm, v_gmlp_w_s, v_gmlp_b_s, v_pool_w, v_pool_scale, v_w_out, v_norm_ffn, v_w_ffn_gate, v_w_ffn_up, v_w_ffn_down, v_norm_ple, v_w_ple_gate, v_w_ple_proj):
    W = dict(norm_mix=norm_mix, w_in=w_in, q_norm=q_norm, k_norm=k_norm, forget_bias=forget_bias,
             gmlp_v_norm=gmlp_v_norm, gmlp_w_s=gmlp_w_s, gmlp_b_s=gmlp_b_s, pool_w=pool_w, pool_scale=pool_scale,
             w_out=w_out, norm_ffn=norm_ffn, w_ffn_gate=w_ffn_gate, w_ffn_up=w_ffn_up, w_ffn_down=w_ffn_down,
             norm_ple=norm_ple, w_ple_gate=w_ple_gate, w_ple_proj=w_ple_proj)
    M = dict(norm_mix=m_norm_mix, w_in=m_w_in, q_norm=m_q_norm, k_norm=m_k_norm, forget_bias=m_forget_bias,
             gmlp_v_norm=m_gmlp_v_norm, gmlp_w_s=m_gmlp_w_s, gmlp_b_s=m_gmlp_b_s, pool_w=m_pool_w,
             pool_scale=m_pool_scale, w_out=m_w_out, norm_ffn=m_norm_ffn, w_ffn_gate=m_w_ffn_gate,
             w_ffn_up=m_w_ffn_up, w_ffn_down=m_w_ffn_down, norm_ple=m_norm_ple, w_ple_gate=m_w_ple_gate,
             w_ple_proj=m_w_ple_proj)
    V = dict(norm_mix=v_norm_mix, w_in=v_w_in, q_norm=v_q_norm, k_norm=v_k_norm, forget_bias=v_forget_bias,
             gmlp_v_norm=v_gmlp_v_norm, gmlp_w_s=v_gmlp_w_s, gmlp_b_s=v_gmlp_b_s, pool_w=v_pool_w,
             pool_scale=v_pool_scale, w_out=v_w_out, norm_ffn=v_norm_ffn, w_ffn_gate=v_w_ffn_gate,
             w_ffn_up=v_w_ffn_up, w_ffn_down=v_w_ffn_down, norm_ple=v_norm_ple, w_ple_gate=v_w_ple_gate,
             w_ple_proj=v_w_ple_proj)

    L = w_in.shape[0]
    _, T, D = x.shape
    A, Wd = D // 2, D // 4
    H = A // HEAD
    G = gmlp_w_s.shape[1]
    Gp = pool_w.shape[1]
    DP4 = w_in.shape[2]
    DP = N_CHIPS * DP4
    NM = 3 * A + 3 * Wd
    FS = w_ffn_gate.shape[2]
    FF = N_CHIPS * FS
    DS = D // N_CHIPS
    PL = p.shape[-1]
    assert Wd // G == HEAD and Wd // Gp == HEAD and DP == NM + H and H <= HEAD
    assert all(w & (w - 1) == 0 for w in POOL_WINDOWS[:Gp])
    tb = _tile(T, 256, HEAD)
    nb = T // tb
    tm = _tile(T, 512, 16)
    tmw = _tile(T, 1024, 16)
    tn = _tile(NM, 512, 128)
    tnd = _tile(D, 512, 128)
    tkd = _tile(D, 1024, 128)
    tnw = _tile(D, 1024, 128)
    col_gu, col_gv, col_xp = 3 * A // HEAD, (3 * A + Wd) // HEAD, (3 * A + 2 * Wd) // Wd
    col_dg, col_dp = A // HEAD, (A + Wd) // Wd

    order = _Order(x[0, :1, :1])

    def gather_near(i, names, tag=""):
        shards = [_cast_layer(f"cast_{n}_{i}", W[n], i) for n in names]
        lands = [lax.empty((N_CHIPS,) + s.shape, bf16) for s in shards]
        return names, order.start(f"ag_near_{i}{tag}", _plan_gather_near(len(names)), shards + lands)

    def gather_relay(i, near, tag=""):
        names, handle = near
        return names, order.start(f"ag_relay_{i}{tag}", _plan_gather_relay(len(names)), order.wait(handle)[len(names):])

    def gather_far(i, relayed, tag=""):
        names, handle = relayed
        return names, order.start(f"ag_far_{i}{tag}", _plan_gather_far(len(names)), order.wait(handle))

    def gathered(far):
        names, handle = far
        g = dict(zip(names, order.wait(handle)))
        out = {}
        if "w_in" in g:
            def cols(lo, hi):
                parts = []
                for j in range(N_CHIPS):
                    s, e = max(lo, j * DP4), min(hi, (j + 1) * DP4)
                    if s < e:
                        parts.append(g["w_in"][j][:, s - j * DP4:e - j * DP4])
                return parts

            out["w_main"] = jnp.concatenate(cols(0, 3 * A) + cols(3 * A + H, DP), axis=1)
            out["w_f"] = jnp.pad(jnp.concatenate(cols(3 * A, 3 * A + H), axis=1), ((0, 0), (0, HEAD - H)))
        if "w_out" in g:
            out["w_out"] = g["w_out"].reshape(D, D)
        if "w_ffn_gate" in g:
            out.update(w_gate=g["w_ffn_gate"], w_up=g["w_ffn_up"], w_down=g["w_ffn_down"].reshape(FF, D),
                       w_pg=g["w_ple_gate"].reshape(D, D), w_pp=g["w_ple_proj"])
        return out

    Wf = [None] * L
    relayed = gather_relay(0, gather_near(0, BIG[:2], "a"), "a")
    near_rest = gather_near(0, BIG[2:], "c")
    Wf[0] = gathered(gather_far(0, relayed, "a"))
    near = relayed = None

    h = x.reshape(T, D)
    pb16 = p.reshape(L, T, PL).astype(bf16)
    saved = []

    for i in range(L):
        w = Wf[i]
        sv = dict(h0=h)
        xn1 = _rms_fwd(f"rms1_{i}", h, order.follows(norm_mix[i]))
        (P,) = _matmul(f"proj_{i}", "nn", (T // tmw, NM // tn),
                       [((xn1, BS((tmw, D), lambda i, j: (i, 0))), (w["w_main"], BS((D, tn), lambda i, j: (0, j))))], [],
                       [((T, NM), f32, BS((tmw, tn), lambda i, j: (i, j)))])
        (Pf,) = _matmul(f"projf_{i}", "nn", (T // tm, 1),
                        [((xn1, BS((tm, D), lambda i, j: (i, 0))), (w["w_f"], BS((D, HEAD), lambda i, j: (0, 0))))], [],
                        [((T, HEAD), f32, BS((tm, HEAD), lambda i, j: (i, 0)))])
        fb = jnp.pad(forget_bias[i], (0, HEAD - H)).reshape(1, HEAD)
        cc, ct = _fgate_fwd(f"fgate_{i}", Pf, fb)
        c_col = ct[:H].reshape(H, T, 1)
        c_row = ct[:H].reshape(H, nb, 1, tb)
        qg, kg = q_norm[i].reshape(1, HEAD), k_norm[i].reshape(1, HEAD)
        qn, kn, vb = _qk_norm(f"qknorm_{i}", P, qg, kg, A)
        y_attn, o32, lse = _attn_fwd(f"attn_{i}", qn, kn, vb, c_col, c_row, tb)
        if i == 0:
            order.done(y_attn)
            relayed_rest = gather_relay(0, near_rest, "c")
            near = gather_near(1, BIG) if L > 1 else None
        gain = gmlp_v_norm[i].reshape(G, 1, HEAD)
        bs = gmlp_b_s[i].reshape(G, HEAD, 1)
        y_gmlp = _gmlp_fwd(f"gmlp_{i}", P, order.follows(gain), gmlp_w_s[i], bs, col_gu, col_gv, Wd)
        ps = pool_scale[i].reshape(1, Wd)
        y_pool = _pool_fwd(f"pool_{i}", P, pool_w[i], ps, col_xp, Wd)
        mix = jnp.concatenate([y_attn, y_gmlp, y_pool], axis=1)
        (h1,) = _matmul(f"out_{i}", "nn", (T // tmw, D // tnd),
                        [((mix, BS((tmw, D), lambda i, j: (i, 0))), (w["w_out"], BS((D, tnd), lambda i, j: (0, j))))],
                        [(h, BS((tmw, tnd), lambda i, j: (i, j)))],
                        [((T, D), f32, BS((tmw, tnd), lambda i, j: (i, j)))],
                        epilogue=lambda accs, ex: (accs[0] + ex[0],))
        xn2 = _rms_fwd(f"rms2_{i}", h1, norm_ffn[i])
        order.done(xn2)
        if i == 0:
            w.update(gathered(gather_far(0, relayed_rest, "c")))
        elif i + 1 < L:
            relayed = gather_relay(i + 1, near)
            near = gather_near(i + 2, BIG) if i + 2 < L else None

        def ffn_epi(accs, ex):
            g_, u_ = accs
            return g_, u_, g_ * jax.nn.sigmoid(g_) * u_

        ffo = BS((tm, FS), lambda i, j: (i, j))
        Gt, Ut, act = _matmul(f"ffn1_{i}", "nn", (T // tm, N_CHIPS),
                              [((xn2, BS((tm, D), lambda i, j: (i, 0))), (w["w_gate"], BS((None, D, FS), lambda i, j: (j, 0, 0)))),
                               ((xn2, BS((tm, D), lambda i, j: (i, 0))), (w["w_up"], BS((None, D, FS), lambda i, j: (j, 0, 0))))],
                              [], [((T, FF), bf16, ffo)] * 3, epilogue=ffn_epi, after=order.token)
        (h2,) = _matmul(f"ffn2_{i}", "nn", (T // tmw, D // tnd),
                        [((act, BS((tmw, FF), lambda i, j: (i, 0))), (w["w_down"], BS((FF, tnd), lambda i, j: (0, j))))],
                        [(h1, BS((tmw, tnd), lambda i, j: (i, j)))],
                        [((T, D), f32, BS((tmw, tnd), lambda i, j: (i, j)))],
                        epilogue=lambda accs, ex: (accs[0] + ex[0],))
        if i == 0 and L > 1:
            order.done(h2)
            relayed = gather_relay(1, near)
            near = gather_near(2, BIG) if L > 2 else None
        xn3 = _rms_fwd(f"rms3_{i}", h2, order.follows(norm_ple[i]))

        def ple_epi(accs, ex):
            gate = jax.nn.sigmoid(accs[0])
            return ex[0] + accs[1] * gate, gate, accs[1]

        dso = BS((tmw, DS), lambda i, j: (i, j))
        h3, gate, e = _matmul(f"ple_{i}", "nn", (T // tmw, N_CHIPS),
                              [((xn3, BS((tmw, D), lambda i, j: (i, 0))), (w["w_pg"], BS((D, DS), lambda i, j: (0, j)))),
                               ((pb16[i], BS((tmw, PL), lambda i, j: (i, 0))), (w["w_pp"], BS((None, PL, DS), lambda i, j: (j, 0, 0))))],
                              [(h2, dso)], [((T, D), f32, dso), ((T, D), bf16, dso), ((T, D), bf16, dso)], epilogue=ple_epi)
        sv.update(xn1=xn1, P=P, Pf=Pf, fb=fb, c_col=c_col, c_row=c_row, qn=qn, kn=kn, vb=vb, o32=o32, lse=lse,
                  mix=mix, h1=h1, xn2=xn2, Gt=Gt, Ut=Ut, act=act, h2=h2, xn3=xn3, gate=gate, e=e)
        saved.append(sv)
        h = h3
        order.done(h3)
        if i + 1 < L:
            Wf[i + 1] = gathered(gather_far(i + 1, relayed))

    dh, loss_tile = _loss_grad("loss", h, loss_target.reshape(T, D))

    small_g = {n: [None] * L for n in SMALL}
    big_out = {}

    def stage_a(u):
        n_u = len(u["names"])
        lands = [lax.empty((N_CHIPS, g.shape[1] // 2, g.shape[2]), f32) for g in u["grads"]]
        u["a"] = order.start(f"rs_a_{u['tag']}", _plan_sibling_halves(n_u), u["grads"] + lands)

    def stage_pair(u):
        n_u = len(u["names"])
        out = order.wait(u["a"])
        pairs = [_pair_sum(f"rs_pair_{u['tag']}_{a}", out[a], out[n_u + a]) for a in range(n_u)]
        u["pb"], u["own"] = [t[0] for t in pairs], [t[1] for t in pairs]

    def stage_b(u):
        lands = [lax.empty(t.shape, bf16) for t in u["pb"]]
        u["b"] = order.start(f"rs_b_{u['tag']}", _plan_chip_scatter(len(u["names"])), u["pb"] + lands)

    def stage_sum(u):
        n_u = len(u["names"])
        out = order.wait(u["b"])
        u["sum"] = [_chip_sum(f"rs_sum_{u['tag']}_{a}", u["own"][a], out[n_u + a]) for a in range(n_u)]

    def stage_c(u):
        u["c"] = order.start(f"rs_c_{u['tag']}", _plan_sibling_join(len(u["names"])), u["sum"])

    def stage_adamw(u):
        for n, r in zip(u["names"], order.wait(u["c"])):
            big_out[n] = _adamw_layer(f"adamw_{n}_{u['layer']}", u["layer"], W[n], M[n], V[n],
                                      r.reshape(W[n].shape[1:]), big_out.get(n))
    dh1 = prev_f = prev_m = None
    for i in reversed(range(L)):
        w, sv = Wf[i], saved[i]
        if dh1 is not None:
            dh, _, dg = _rms_bwd(f"rms1_bw_{i + 1}", dxn1, saved[i + 1]["h0"], order.follows(norm_mix[i + 1]), dh1)
            small_g["norm_mix"][i + 1] = dg.reshape(D)
        de, dz = _ple_bwd_elem(f"ple_bw_{i}", dh, sv["gate"], sv["e"])
        (d_wpp,) = _matmul(f"d_wpp_{i}", "tn", (N_CHIPS, 1),
                           [((pb16[i], BS((T, PL), lambda i, j: (0, 0))), (de, BS((T, DS), lambda i, j: (0, i))))], [],
                           [((N_CHIPS, PL, DS), f32, BS((None, PL, DS), lambda i, j: (i, 0, 0)))])
        (d_wpg,) = _matmul(f"d_wpg_{i}", "tn", (D // tkd, D // tnd),
                           [((sv["xn3"], BS((T, tkd), lambda i, j: (0, i))), (dz, BS((T, tnd), lambda i, j: (0, j))))], [],
                           [((D, D), f32, BS((tkd, tnd), lambda i, j: (i, j)))])
        order.done(dz)
        if prev_m is not None:
            stage_pair(prev_m)
        (dxn3,) = _matmul(f"d_xn3_{i}", "nt", (T // tmw, D // tnd),
                          [((dz, BS((tmw, D), lambda i, j: (i, 0))), (w["w_pg"], BS((tnd, D), lambda i, j: (j, 0))))], [],
                          [((T, D), f32, BS((tmw, tnd), lambda i, j: (i, j)))], after=order.token)
        dh2, dh2b, dg = _rms_bwd(f"rms3_bw_{i}", dxn3, sv["h2"], norm_ple[i], dh)
        small_g["norm_ple"][i] = dg.reshape(D)

        def dffn_epi(accs, ex):
            da = accs[0]
            g_, u_ = ex[0].astype(f32), ex[1].astype(f32)
            sg = jax.nn.sigmoid(g_)
            return da * u_ * (sg * (1.0 + g_ * (1.0 - sg))), da * (g_ * sg)

        ffo = BS((tm, FS), lambda j, i: (i, j))
        dG, dU = _matmul(f"d_act_{i}", "nt", (N_CHIPS, T // tm),
                         [((dh2b, BS((tm, D), lambda j, i: (i, 0))), (w["w_down"], BS((FS, D), lambda j, i: (j, 0))))],
                         [(sv["Gt"], ffo), (sv["Ut"], ffo)], [((T, FF), bf16, ffo)] * 2, epilogue=dffn_epi)
        (d_wd,) = _matmul(f"d_wd_{i}", "tn", (N_CHIPS, D // tnw),
                          [((sv["act"], BS((T, FS), lambda i, j: (0, i))), (dh2b, BS((T, tnw), lambda i, j: (0, j))))], [],
                          [((FF, D), f32, BS((FS, tnw), lambda i, j: (i, j)))])
        gu_out = BS((None, tnd, FS), lambda j, i: (j, i, 0))
        d_wg, d_wu = _matmul(f"d_wgu_{i}", "tn", (N_CHIPS, D // tnd),
                             [((sv["xn2"], BS((T, tnd), lambda j, i: (0, i))), (dG, BS((T, FS), lambda j, i: (0, j)))),
                              ((sv["xn2"], BS((T, tnd), lambda j, i: (0, i))), (dU, BS((T, FS), lambda j, i: (0, j))))], [],
                             [((N_CHIPS, D, FS), f32, gu_out)] * 2, epilogue=lambda accs, ex: (accs[0], accs[1]))
        order.done(dG)
        unit_f = dict(tag=f"{i}f", layer=i, names=["w_ffn_gate", "w_ffn_up", "w_ffn_down", "w_ple_gate", "w_ple_proj"],
                      grads=[d_wg, d_wu, d_wd.reshape(N_CHIPS, FS, D), d_wpg.reshape(N_CHIPS, DS, D), d_wpp])
        stage_a(unit_f)
        if prev_f is not None:
            stage_sum(prev_f)
            stage_c(prev_f)
        if prev_m is not None:
            stage_b(prev_m)
        tm2 = _tile(T, 256, 16)
        (dxn2,) = _matmul(f"d_xn2_{i}", "nt", (D // tnd, T // tm2),
                          [((dG, BS((tm2, FF), lambda j, i: (i, 0))), (w["w_gate"], BS((N_CHIPS, tnd, FS), lambda j, i: (0, j, 0)))),
                           ((dU, BS((tm2, FF), lambda j, i: (i, 0))), (w["w_up"], BS((N_CHIPS, tnd, FS), lambda j, i: (0, j, 0))))], [],
                          [((T, D), f32, BS((tm2, tnd), lambda j, i: (i, j)))], after=order.token)
        dh1, dh1b, dg = _rms_bwd(f"rms2_bw_{i}", dxn2, sv["h1"], norm_ffn[i], dh2)
        small_g["norm_ffn"][i] = dg.reshape(D)
        (dmix,) = _matmul(f"d_mix_{i}", "nt", (T // tmw, D // tnd),
                          [((dh1b, BS((tmw, D), lambda i, j: (i, 0))), (w["w_out"], BS((tnd, D), lambda i, j: (j, 0))))], [],
                          [((T, D), f32, BS((tmw, tnd), lambda i, j: (i, j)))])
        (d_wout,) = _matmul(f"d_wout_{i}", "tn", (D // tkd, D // tnd),
                            [((sv["mix"], BS((T, tkd), lambda i, j: (0, i))), (dh1b, BS((T, tnd), lambda i, j: (0, j))))], [],
                            [((D, D), f32, BS((tkd, tnd), lambda i, j: (i, j)))])
        order.done(dmix)
        stage_pair(unit_f)
        stage_b(unit_f)
        if prev_f is not None:
            stage_adamw(prev_f)
        qg, kg = q_norm[i].reshape(1, HEAD), k_norm[i].reshape(1, HEAD)
        dq, dk, dv, dc_row, dqg, dkg = _attn_bwd(f"attn_bw_{i}", sv["qn"], sv["kn"], sv["vb"], sv["o32"], dmix,
                                                 sv["lse"], sv["c_col"], sv["c_row"], sv["P"], order.follows(qg), kg, tb)
        small_g["q_norm"][i] = dqg.reshape(HEAD)
        small_g["k_norm"][i] = dkg.reshape(HEAD)
        dct = jnp.pad(dc_row.reshape(H, T), ((0, HEAD - H), (0, 0)))
        dPf, dfb = _fgate_bwd(f"fgate_bw_{i}", dct, sv["Pf"], sv["fb"])
        small_g["forget_bias"][i] = dfb[0, :H]
        gain = gmlp_v_norm[i].reshape(G, 1, HEAD)
        bs = gmlp_b_s[i].reshape(G, HEAD, 1)
        dgu, dgv, dws, dbs, dgain = _gmlp_bwd(f"gmlp_bw_{i}", sv["P"], dmix, gain, gmlp_w_s[i], bs, col_gu, col_gv,
                                              col_dg, Wd)
        small_g["gmlp_w_s"][i] = dws
        small_g["gmlp_b_s"][i] = dbs.reshape(G, HEAD)
        small_g["gmlp_v_norm"][i] = dgain.reshape(G, HEAD)
        ps = pool_scale[i].reshape(1, Wd)
        dxp, dpw, dps = _pool_bwd(f"pool_bw_{i}", sv["P"], dmix, pool_w[i], ps, col_xp, col_dp, Wd)
        small_g["pool_w"][i] = dpw
        small_g["pool_scale"][i] = dps.reshape(Wd)
        dP = jnp.concatenate([dq, dk, dv, dgu, dgv, dxp], axis=1)
        (d_wmain,) = _matmul(f"d_wmain_{i}", "tn", (D // tkd, NM // tn),
                             [((sv["xn1"], BS((T, tkd), lambda i, j: (0, i))), (dP, BS((T, tn), lambda i, j: (0, j))))], [],
                             [((D, NM), f32, BS((tkd, tn), lambda i, j: (i, j)))])
        (d_wf,) = _matmul(f"d_wf_{i}", "tn", (D // tnd, 1),
                          [((sv["xn1"], BS((T, tnd), lambda i, j: (0, i))), (dPf, BS((T, HEAD), lambda i, j: (0, 0))))], [],
                          [((D, HEAD), f32, BS((tnd, HEAD), lambda i, j: (i, 0)))])
        def win_cols(lo, hi):
            out = []
            for src, s0, e0, off in ((d_wmain, 0, 3 * A, 0), (d_wf, 3 * A, 3 * A + H, 3 * A), (d_wmain, 3 * A + H, DP, H)):
                s, e = max(lo, s0), min(hi, e0)
                if s < e:
                    out.append(src[:, s - off:e - off])
            return out

        d_win4 = jnp.stack([jnp.concatenate(win_cols(j * DP4, (j + 1) * DP4), axis=1) for j in range(N_CHIPS)])
        order.done(dP)
        unit_m = dict(tag=f"{i}m", layer=i, names=["w_in", "w_out"], grads=[d_win4, d_wout.reshape(N_CHIPS, DS, D)])
        stage_a(unit_m)
        if prev_m is not None:
            stage_sum(prev_m)
            stage_c(prev_m)
        (dxn1,) = _matmul(f"d_xn1_{i}", "nt", (D // tnd, T // tm),
                          [((dP, BS((tm, NM), lambda j, i: (i, 0))), (w["w_main"], BS((tnd, NM), lambda j, i: (j, 0)))),
                           ((dPf, BS((tm, HEAD), lambda j, i: (i, 0))), (w["w_f"], BS((tnd, HEAD), lambda j, i: (j, 0))))], [],
                          [((T, D), f32, BS((tm, tnd), lambda j, i: (i, j)))], after=order.token)
        order.done(dxn1)
        if prev_m is not None:
            stage_adamw(prev_m)
        prev_f, prev_m = unit_f, unit_m

    dh, _, dg = _rms_bwd("rms1_bw_0", dxn1, saved[0]["h0"], order.follows(norm_mix[0]), dh1)
    small_g["norm_mix"][0] = dg.reshape(D)
    order.done(dh)
    stage_pair(prev_m)
    stage_b(prev_m)
    stage_sum(prev_f)
    stage_c(prev_f)

    small_full = {n: jnp.stack(small_g[n]) for n in SMALL}
    small_shapes = [W[n].shape for n in SMALL]
    n_rows = sum(_rows_of(W[n].size) for n in SMALL) + _rows_of(1)
    rows8 = -(-n_rows // 64) * 8
    packed = order.follows(_pack_rows([small_full[n] for n in SMALL] + [loss_tile[0, :1]], N_DEV * rows8))
    summed = _all_reduce_small("allreduce_small", packed.reshape(N_DEV, rows8, 128)).reshape(-1, 128)
    order.done(summed)
    stage_adamw(prev_f)
    stage_sum(prev_m)
    stage_c(prev_m)
    stage_adamw(prev_m)
    *small_grads, loss_row = _unpack_rows(summed, small_shapes + [(1,)])
    grads = dict(zip(SMALL, small_grads))
    loss = loss_row[0]

    wp, mp, vp = (_pack_rows([t[n] for n in SMALL], N_DEV * rows8) for t in (W, M, V))
    delta, new_m, new_v = (dict(zip(SMALL, _unpack_rows(t, small_shapes)))
                           for t in _adamw("adamw_small", wp, summed, mp, vp))
    for n in BIG:
        grads[n], delta[n], new_m[n], new_v[n] = big_out[n]

    return (loss, dh.reshape(1, T, D), *[grads[n] for n in WEIGHTS], *[delta[n] for n in WEIGHTS],
            *[new_m[n] for n in WEIGHTS], *[new_v[n] for n in WEIGHTS])
```

```python
import jax
import jax.numpy as jnp
from jax import lax
from jax.experimental import pallas as pl
from jax.experimental.pallas import tpu as pltpu

f32, bf16 = jnp.float32, jnp.bfloat16
S = jax.ShapeDtypeStruct
BS = pl.BlockSpec
ANY = pl.BlockSpec(memory_space=pl.ANY)
MESH = pl.DeviceIdType.MESH

EPS = 1e-6
HEAD = 128
POOL_WINDOWS = (2, 4, 8, 16)
NEG = -1e30
N_CHIPS = 4
N_DEV = 8
VMEM_LIMIT = 56 * 1024 * 1024

ADAM_LR, ADAM_B1, ADAM_B2, ADAM_EPS, ADAM_WD, ADAM_STEP = 0.001, 0.9, 0.999, 1e-08, 0.01, 10

BIG = ("w_in", "w_out", "w_ffn_gate", "w_ffn_up", "w_ffn_down", "w_ple_gate", "w_ple_proj")
SMALL = ("norm_mix", "q_norm", "k_norm", "forget_bias", "gmlp_v_norm", "gmlp_w_s", "gmlp_b_s", "pool_w",
         "pool_scale", "norm_ffn", "norm_ple")
WEIGHTS = ("norm_mix", "w_in", "q_norm", "k_norm", "forget_bias", "gmlp_v_norm", "gmlp_w_s", "gmlp_b_s", "pool_w",
           "pool_scale", "w_out", "norm_ffn", "w_ffn_gate", "w_ffn_up", "w_ffn_down", "norm_ple", "w_ple_gate",
           "w_ple_proj")


def _tile(n, target, mult):
    best = None
    for t in range(mult, min(n, target) + 1, mult):
        if n % t == 0:
            best = t
    return best if best is not None else n


def _params(**kw):
    return pltpu.CompilerParams(vmem_limit_bytes=VMEM_LIMIT, **kw)


def _dot(a, b, kind):
    dims = {"nn": (((1,), (0,)), ((), ())), "nt": (((1,), (1,)), ((), ())), "tn": (((0,), (0,)), ((), ()))}[kind]
    return lax.dot_general(a.astype(bf16), b.astype(bf16), dims, preferred_element_type=f32)


def _heads_per_program(n_heads, want=2):
    while n_heads % want:
        want //= 2
    return want


def _my_place():
    x, y, c = lax.axis_index("x"), lax.axis_index("y"), lax.axis_index("c")
    return x, y, c, 2 * x + y


def _matmul(name, kind, grid, pairs, extras, outs, epilogue=None, after=None):
    n_p, n_e = len(pairs), len(extras)
    tokens = [] if after is None else [(after, BS((8, 128), lambda *_: (0, 0)))]

    def body(*refs):
        a_refs, b_refs = refs[:n_p], refs[n_p:2 * n_p]
        e_refs = refs[2 * n_p:2 * n_p + n_e]
        o_refs = refs[2 * n_p + n_e + len(tokens):]
        accs = []
        for a_ref, b_ref in zip(a_refs, b_refs):
            if len(b_ref.shape) == 3:
                w = b_ref.shape[2]
                acc = None
                for s in range(b_ref.shape[0]):
                    d = _dot(a_ref[:, s * w:(s + 1) * w], b_ref[s], kind)
                    acc = d if acc is None else acc + d
            else:
                acc = _dot(a_ref[...], b_ref[...], kind)
            accs.append(acc)
        if epilogue is None:
            res = accs[0]
            for t in accs[1:]:
                res = res + t
            res = (res,)
        else:
            res = epilogue(accs, [e[...] for e in e_refs])
        for o_ref, o in zip(o_refs, res):
            o_ref[...] = o.astype(o_ref.dtype)

    in_arrays = [p[0][0] for p in pairs] + [p[1][0] for p in pairs] + [e[0] for e in extras + tokens]
    in_specs = [p[0][1] for p in pairs] + [p[1][1] for p in pairs] + [e[1] for e in extras + tokens]
    res = pl.pallas_call(
        body, name=name, grid=grid, in_specs=in_specs,
        out_specs=[o[2] for o in outs], out_shape=[S(o[0], o[1]) for o in outs],
        compiler_params=_params(),
    )(*in_arrays)
    return res


def _rms_fwd(name, x, g):
    T, D = x.shape
    tr = _tile(T, 256, 8)

    def body(x_ref, g_ref, o_ref):
        xv = x_ref[...]
        r = lax.rsqrt(jnp.mean(xv * xv, axis=-1, keepdims=True) + EPS)
        o_ref[...] = (xv * r * g_ref[...]).astype(o_ref.dtype)

    return pl.pallas_call(
        body, name=name, grid=(T // tr,),
        in_specs=[BS((tr, D), lambda i: (i, 0)), BS((1, D), lambda i: (0, 0))],
        out_specs=BS((tr, D), lambda i: (i, 0)), out_shape=S((T, D), bf16),
    )(x, g.reshape(1, D))


def _rms_bwd(name, dxn, x, g, dres):
    T, D = x.shape
    tr = _tile(T, 256, 8)

    def body(dxn_ref, x_ref, g_ref, dres_ref, dx_ref, dxb_ref, dg_ref):
        i = pl.program_id(0)
        xv = x_ref[...]
        r = lax.rsqrt(jnp.mean(xv * xv, axis=-1, keepdims=True) + EPS)
        xh = xv * r
        dxn_v = dxn_ref[...]
        dxh = dxn_v * g_ref[...]
        dx = dres_ref[...] + r * (dxh - xh * jnp.mean(dxh * xh, axis=-1, keepdims=True))
        dx_ref[...] = dx
        dxb_ref[...] = dx.astype(bf16)
        part = jnp.sum(dxn_v * xh, axis=0, keepdims=True)

        @pl.when(i == 0)
        def _():
            dg_ref[...] = part

        @pl.when(i > 0)
        def _():
            dg_ref[...] += part

    row = BS((tr, D), lambda i: (i, 0))
    vec = BS((1, D), lambda i: (0, 0))
    return pl.pallas_call(
        body, name=name, grid=(T // tr,),
        in_specs=[row, row, vec, row], out_specs=[row, row, vec],
        out_shape=[S((T, D), f32), S((T, D), bf16), S((1, D), f32)],
    )(dxn, x, g.reshape(1, D), dres)


def _loss_grad(name, y, tgt):
    T, D = y.shape
    tr = _tile(T, 256, 8)

    def body(y_ref, t_ref, dy_ref, l_ref):
        i = pl.program_id(0)
        e = y_ref[...] - t_ref[...]
        dy_ref[...] = e * (1.0 / D)
        part = 0.5 * jnp.sum(jnp.mean(e * e, axis=-1, keepdims=True), axis=0, keepdims=True)

        @pl.when(i == 0)
        def _():
            l_ref[...] = jnp.zeros_like(l_ref)

        l_ref[...] += jnp.broadcast_to(part, l_ref.shape)

    row = BS((tr, D), lambda i: (i, 0))
    return pl.pallas_call(
        body, name=name, grid=(T // tr,), in_specs=[row, row],
        out_specs=[row, BS((8, 128), lambda i: (0, 0))],
        out_shape=[S((T, D), f32), S((8, 128), f32)],
    )(y, tgt)


def _ple_bwd_elem(name, dh, gate, e):
    T, D = dh.shape
    tr = _tile(T, 256, 16)

    def body(dh_ref, g_ref, e_ref, de_ref, dz_ref):
        d = dh_ref[...]
        g = g_ref[...].astype(f32)
        de_ref[...] = (d * g).astype(bf16)
        dz_ref[...] = (d * e_ref[...].astype(f32) * g * (1.0 - g)).astype(bf16)

    row = BS((tr, D), lambda i: (i, 0))
    return pl.pallas_call(
        body, name=name, grid=(T // tr,), in_specs=[row, row, row], out_specs=[row, row],
        out_shape=[S((T, D), bf16), S((T, D), bf16)],
    )(dh, gate, e)


def _cast_layer(name, w_all, layer):
    _, R, C = w_all.shape
    lanes = -(-C // 128) * 128
    tr = _tile(R, max(16, (1024 * 1024) // lanes // 16 * 16), 16)

    def body(w_ref, o_ref):
        o_ref[...] = w_ref[...].astype(bf16)

    return pl.pallas_call(
        body, name=name, grid=(R // tr,), in_specs=[BS((None, tr, C), lambda r: (layer, r, 0))],
        out_specs=BS((tr, C), lambda r: (r, 0)), out_shape=S((R, C), bf16),
    )(w_all)


def _gelu_and_grad(x):
    k0, k1 = 0.7978845608028654, 0.044715
    th = jnp.tanh(k0 * (x + k1 * x * x * x))
    val = 0.5 * x * (1.0 + th)
    grad = 0.5 * (1.0 + th) + 0.5 * x * (1.0 - th * th) * (k0 * (1.0 + 3.0 * k1 * x * x))
    return val, grad


def _fgate_fwd(name, pf, fb):
    T = pf.shape[0]

    def body(pf_ref, fb_ref, c_ref, ct_ref):
        xv = jax.nn.log_sigmoid(pf_ref[...] + fb_ref[...])
        row = lax.broadcasted_iota(jnp.int32, xv.shape, 0)
        s = 1
        while s < T:
            xv = xv + jnp.where(row >= s, pltpu.roll(xv, s, 0), 0.0)
            s *= 2
        c_ref[...] = xv
        ct_ref[...] = xv.T

    return pl.pallas_call(body, name=name, out_shape=[S((T, HEAD), f32), S((HEAD, T), f32)])(pf, fb)


def _fgate_bwd(name, dct, pf, fb):
    T = pf.shape[0]

    def body(dct_ref, pf_ref, fb_ref, dpf_ref, dfb_ref):
        xv = dct_ref[...].T
        row = lax.broadcasted_iota(jnp.int32, xv.shape, 0)
        s = 1
        while s < T:
            xv = xv + jnp.where(row + s < T, pltpu.roll(xv, T - s, 0), 0.0)
            s *= 2
        df = xv * jax.nn.sigmoid(-(pf_ref[...] + fb_ref[...]))
        dpf_ref[...] = df.astype(bf16)
        dfb_ref[...] = jnp.sum(df, axis=0, keepdims=True)

    return pl.pallas_call(body, name=name, out_shape=[S((T, HEAD), bf16), S((1, HEAD), f32)])(dct, pf, fb)


def _qk_norm(name, P, qg, kg, A):
    T = P.shape[0]
    tr = _tile(T, 256, 16)
    n_heads = A // HEAD

    def body(q_ref, k_ref, v_ref, qg_ref, kg_ref, qn_ref, kn_ref, vb_ref):
        for h in range(n_heads):
            sl = slice(h * HEAD, (h + 1) * HEAD)
            for src, g_ref, dst in ((q_ref, qg_ref, qn_ref), (k_ref, kg_ref, kn_ref)):
                xv = src[:, sl]
                r = lax.rsqrt(jnp.mean(xv * xv, axis=-1, keepdims=True) + EPS)
                dst[:, sl] = (xv * r * g_ref[...]).astype(bf16)
        vb_ref[...] = v_ref[...].astype(bf16)

    vec = BS((1, HEAD), lambda i: (0, 0))
    out = BS((tr, A), lambda i: (i, 0))
    return pl.pallas_call(
        body, name=name, grid=(T // tr,),
        in_specs=[BS((tr, A), lambda i: (i, 0)), BS((tr, A), lambda i: (i, 1)), BS((tr, A), lambda i: (i, 2)), vec, vec],
        out_specs=[out, out, out], out_shape=[S((T, A), bf16)] * 3,
    )(P, P, P, qg, kg)


def _attn_fwd(name, qn, kn, vb, c_col, c_row, tb, mix_width):
    T, A = qn.shape
    H = A // HEAD
    nb = T // tb
    scale = HEAD ** -0.5
    hp = _heads_per_program(H, 4)
    wide = hp * HEAD

    def body(q_ref, k_ref, v_ref, cq_ref, ck_ref, o_ref, o32_ref, lse_ref):
        i = pl.program_id(1)
        below = lax.broadcasted_iota(jnp.int32, (tb, tb), 0) >= lax.broadcasted_iota(jnp.int32, (tb, tb), 1)

        def block(j, carry, diagonal):
            koff = pl.multiple_of(j * tb, tb)
            out = []
            for hh in range(hp):
                m, l, acc = carry[hh]
                sl = slice(hh * HEAD, (hh + 1) * HEAD)
                k = k_ref[pl.ds(koff, tb), sl]
                v = v_ref[pl.ds(koff, tb), sl]
                s = _dot(q_ref[:, sl], k, "nt") * scale + (cq_ref[hh] - ck_ref[hh, j])
                if diagonal:
                    s = jnp.where(below, s, NEG)
                m_new = jnp.maximum(m, jnp.max(s, axis=-1, keepdims=True))
                alpha = jnp.exp(m - m_new)
                p = jnp.exp(s - m_new)
                l = l * alpha + jnp.sum(p, axis=-1, keepdims=True)
                acc = acc * alpha + _dot(p, v, "nn")
                out.append((m_new, l, acc))
            return tuple(out)

        init = tuple((jnp.full((tb, 1), NEG, f32), jnp.zeros((tb, 1), f32), jnp.zeros((tb, HEAD), f32))
                     for _ in range(hp))
        carry = lax.fori_loop(0, i, lambda j, c: block(j, c, False), init)
        carry = block(i, carry, True)
        for hh in range(hp):
            m, l, acc = carry[hh]
            sl = slice(hh * HEAD, (hh + 1) * HEAD)
            o = acc / l
            o_ref[:, sl] = o.astype(bf16)
            o32_ref[:, sl] = o
            lse_ref[hh] = m + jnp.log(l)

    return pl.pallas_call(
        body, name=name, grid=(H // hp, nb),
        in_specs=[BS((tb, wide), lambda h, i: (i, h)), BS((T, wide), lambda h, i: (0, h)),
                  BS((T, wide), lambda h, i: (0, h)), BS((hp, tb, 1), lambda h, i: (h, i, 0)),
                  BS((hp, nb, 1, tb), lambda h, i: (h, 0, 0, 0))],
        out_specs=[BS((tb, wide), lambda h, i: (i, h)), BS((tb, wide), lambda h, i: (i, h)),
                   BS((hp, tb, 1), lambda h, i: (h, i, 0))],
        out_shape=[S((T, mix_width), bf16), S((T, A), f32), S((H, T, 1), f32)],
    )(qn, kn, vb, c_col, c_row)


def _attn_bwd(name, qn, kn, vb, o, dmix, lse, c_col, c_row, P, qg, kg, tb):
    T, A = qn.shape
    H = A // HEAD
    nb = T // tb
    scale = HEAD ** -0.5
    hp = _heads_per_program(H)
    wide = hp * HEAD

    def body(q_ref, k_ref, v_ref, o_ref, do_ref, lse_ref, cq_ref, ck_ref, qraw_ref, kraw_ref, qg_ref, kg_ref,
             dq_out, dk_out, dv_out, dc_out, dqg_out, dkg_out, dq_acc, dk_acc, delta_s):
        h = pl.program_id(0)
        dq_acc[...] = jnp.zeros_like(dq_acc)
        below = lax.broadcasted_iota(jnp.int32, (tb, tb), 0) >= lax.broadcasted_iota(jnp.int32, (tb, tb), 1)
        for hh in range(hp):
            sl = slice(hh * HEAD, (hh + 1) * HEAD)
            delta_s[hh] = jnp.sum(do_ref[:, sl].astype(bf16).astype(f32) * o_ref[:, sl], axis=-1, keepdims=True)

        def kblock(j, _):
            koff = pl.multiple_of(j * tb, tb)

            def products(i, hh):
                sl = slice(hh * HEAD, (hh + 1) * HEAD)
                qoff = pl.multiple_of(i * tb, tb)
                return (_dot(q_ref[pl.ds(qoff, tb), sl], k_ref[pl.ds(koff, tb), sl], "nt"),
                        _dot(do_ref[pl.ds(qoff, tb), sl], v_ref[pl.ds(koff, tb), sl], "nt"))

            def qblock(i, carry, diagonal):
                qoff = pl.multiple_of(i * tb, tb)
                out = []
                for hh in range(hp):
                    dk, dv, dc, qk_i, dp = carry[hh]
                    sl = slice(hh * HEAD, (hh + 1) * HEAD)
                    ahead = products(jnp.minimum(i + 1, nb - 1), hh)
                    k = k_ref[pl.ds(koff, tb), sl]
                    q = q_ref[pl.ds(qoff, tb), sl]
                    do = do_ref[pl.ds(qoff, tb), sl].astype(bf16)
                    s = qk_i * scale + (cq_ref[hh, pl.ds(qoff, tb), :] - ck_ref[hh, j])
                    if diagonal:
                        s = jnp.where(below, s, NEG)
                    p = jnp.exp(s - lse_ref[hh, pl.ds(qoff, tb), :])
                    dv = dv + _dot(p, do, "tn")
                    ds = p * (dp - delta_s[hh, pl.ds(qoff, tb), :])
                    dc = dc - jnp.sum(ds, axis=0, keepdims=True)
                    dsb = (ds * scale).astype(bf16)
                    dk = dk + _dot(dsb, q, "tn")
                    dq_acc[pl.ds(qoff, tb), sl] += _dot(dsb, k, "nn")
                    out.append((dk, dv, dc, *ahead))
                return tuple(out)

            init = tuple((jnp.zeros((tb, HEAD), f32), jnp.zeros((tb, HEAD), f32), jnp.zeros((1, tb), f32),
                          *products(j, hh)) for hh in range(hp))
            carry = qblock(j, init, True)
            carry = lax.fori_loop(j + 1, nb, lambda i, c: qblock(i, c, False), carry)
            for hh in range(hp):
                dk, dv, dc = carry[hh][:3]
                sl = slice(hh * HEAD, (hh + 1) * HEAD)
                dk_acc[pl.ds(koff, tb), sl] = dk
                dv_out[pl.ds(koff, tb), sl] = dv.astype(bf16)
                dc_out[hh, j] = dc
            return 0

        lax.fori_loop(0, nb, kblock, 0)

        for raw_ref, g_ref, acc_ref, d_out, dg_out in ((qraw_ref, qg_ref, dq_acc, dq_out, dqg_out),
                                                       (kraw_ref, kg_ref, dk_acc, dk_out, dkg_out)):
            part = jnp.zeros((1, HEAD), f32)
            for hh in range(hp):
                sl = slice(hh * HEAD, (hh + 1) * HEAD)
                xv = raw_ref[:, sl]
                r = lax.rsqrt(jnp.mean(xv * xv, axis=-1, keepdims=True) + EPS)
                xh = xv * r
                dn = acc_ref[:, sl]
                dxh = dn * g_ref[...]
                d_out[:, sl] = (r * (dxh - xh * jnp.mean(dxh * xh, axis=-1, keepdims=True))).astype(bf16)
                part = part + jnp.sum(dn * xh, axis=0, keepdims=True)

            @pl.when(h == 0)
            def _():
                dg_out[...] = part

            @pl.when(h > 0)
            def _():
                dg_out[...] += part

    heads = lambda off: BS((T, wide), lambda h: (0, off + h))
    col = BS((hp, T, 1), lambda h: (h, 0, 0))
    row = BS((hp, nb, 1, tb), lambda h: (h, 0, 0, 0))
    vec = BS((1, HEAD), lambda h: (0, 0))
    return pl.pallas_call(
        body, name=name, grid=(H // hp,),
        in_specs=[heads(0), heads(0), heads(0), heads(0), heads(0), col, col, row, heads(0), heads(H // hp), vec, vec],
        out_specs=[heads(0), heads(0), heads(0), row, vec, vec],
        out_shape=[S((T, A), bf16)] * 3 + [S((H, nb, 1, tb), f32), S((1, HEAD), f32), S((1, HEAD), f32)],
        scratch_shapes=[pltpu.VMEM((T, wide), f32), pltpu.VMEM((T, wide), f32), pltpu.VMEM((hp, T, 1), f32)],
        compiler_params=_params(),
    )(qn, kn, vb, o, dmix, lse, c_col, c_row, P, P, qg, kg)


def _gmlp_fwd(name, P, mix, gain, ws, b, col_u, col_v, col_y, Wd):
    T = P.shape[0]
    G = Wd // HEAD
    tr = _tile(T, 512, HEAD)

    def body(u_ref, v_ref, gain_ref, ws_ref, b_ref, mix_ref, y_ref):
        tril = lax.broadcasted_iota(jnp.int32, (HEAD, HEAD), 0) >= lax.broadcasted_iota(jnp.int32, (HEAD, HEAD), 1)
        wm = jnp.where(tril, ws_ref[...], 0.0).astype(bf16)
        for n in range(tr // HEAD):
            rows = slice(n * HEAD, (n + 1) * HEAD)
            u = jax.nn.gelu(u_ref[rows, :])
            a = jax.nn.gelu(v_ref[rows, :])
            r = lax.rsqrt(jnp.mean(a * a, axis=-1, keepdims=True) + EPS)
            vn = a * r * gain_ref[...]
            mixed = _dot(wm, vn, "nn") + b_ref[...]
            y_ref[rows, :] = (u * mixed).astype(bf16)

    return pl.pallas_call(
        body, name=name, grid=(G, T // tr),
        in_specs=[BS((tr, HEAD), lambda g, i: (i, col_u + g)), BS((tr, HEAD), lambda g, i: (i, col_v + g)),
                  BS((None, 1, HEAD), lambda g, i: (g, 0, 0)), BS((None, HEAD, HEAD), lambda g, i: (g, 0, 0)),
                  BS((None, HEAD, 1), lambda g, i: (g, 0, 0)), ANY],
        out_specs=BS((tr, HEAD), lambda g, i: (i, col_y + g)), out_shape=S(mix.shape, bf16),
        input_output_aliases={5: 0},
    )(P, P, gain, ws, b, mix)


def _gmlp_bwd(name, P, dmix, gain, ws, b, col_u, col_v, col_dy, Wd):
    T = P.shape[0]
    G = Wd // HEAD
    tr = _tile(T, 512, HEAD)

    def body(u_ref, v_ref, dy_ref, gain_ref, ws_ref, b_ref, du_ref, dv_ref, dws_ref, db_ref, dgain_ref):
        i = pl.program_id(1)
        tril = lax.broadcasted_iota(jnp.int32, (HEAD, HEAD), 0) >= lax.broadcasted_iota(jnp.int32, (HEAD, HEAD), 1)
        wm = jnp.where(tril, ws_ref[...], 0.0).astype(bf16)
        gain_v = gain_ref[...]
        dw = jnp.zeros((HEAD, HEAD), f32)
        db = jnp.zeros((HEAD, 1), f32)
        dgain = jnp.zeros((1, HEAD), f32)
        for n in range(tr // HEAD):
            rows = slice(n * HEAD, (n + 1) * HEAD)
            u, du_dx = _gelu_and_grad(u_ref[rows, :])
            a, da_dx = _gelu_and_grad(v_ref[rows, :])
            dy = dy_ref[rows, :]
            r = lax.rsqrt(jnp.mean(a * a, axis=-1, keepdims=True) + EPS)
            ah = a * r
            vnb = (ah * gain_v).astype(bf16)
            mixed = _dot(wm, vnb, "nn") + b_ref[...]
            dm = dy * u
            dmb = dm.astype(bf16)
            du_ref[rows, :] = (dy * mixed * du_dx).astype(bf16)
            db = db + jnp.sum(dm, axis=1, keepdims=True)
            dw = dw + _dot(dmb, vnb, "nt")
            dvn = _dot(wm, dmb, "tn")
            dgain = dgain + jnp.sum(dvn * ah, axis=0, keepdims=True)
            dah = dvn * gain_v
            da = r * (dah - ah * jnp.mean(dah * ah, axis=-1, keepdims=True))
            dv_ref[rows, :] = (da * da_dx).astype(bf16)
        dw = jnp.where(tril, dw, 0.0)

        @pl.when(i == 0)
        def _():
            dws_ref[...] = dw
            db_ref[...] = db
            dgain_ref[...] = dgain

        @pl.when(i > 0)
        def _():
            dws_ref[...] += dw
            db_ref[...] += db
            dgain_ref[...] += dgain

    out = BS((tr, HEAD), lambda g, i: (i, g))
    return pl.pallas_call(
        body, name=name, grid=(G, T // tr),
        in_specs=[BS((tr, HEAD), lambda g, i: (i, col_u + g)), BS((tr, HEAD), lambda g, i: (i, col_v + g)),
                  BS((tr, HEAD), lambda g, i: (i, col_dy + g)),
                  BS((None, 1, HEAD), lambda g, i: (g, 0, 0)), BS((None, HEAD, HEAD), lambda g, i: (g, 0, 0)),
                  BS((None, HEAD, 1), lambda g, i: (g, 0, 0))],
        out_specs=[out, out, BS((None, HEAD, HEAD), lambda g, i: (g, 0, 0)), BS((None, HEAD, 1), lambda g, i: (g, 0, 0)),
                   BS((None, 1, HEAD), lambda g, i: (g, 0, 0))],
        out_shape=[S((T, Wd), bf16), S((T, Wd), bf16), S((G, HEAD, HEAD), f32), S((G, HEAD, 1), f32),
                   S((G, 1, HEAD), f32)],
    )(P, P, dmix, gain, ws, b)


def _trailing_window(xv, w, row):
    k = 1
    while k < w:
        xv = xv + jnp.where(row >= k, pltpu.roll(xv, k, 0), 0.0)
        k *= 2
    return xv


def _leading_window(xv, w, row, T):
    k = 1
    while k < w:
        xv = xv + jnp.where(row + k < T, pltpu.roll(xv, T - k, 0), 0.0)
        k *= 2
    return xv


def _pool_fwd(name, P, mix, pw, ps, col_x, col_y, Wd):
    T = P.shape[0]
    Gp = Wd // HEAD

    def body(x_ref, pw_ref, ps_ref, mix_ref, y_ref):
        row = lax.broadcasted_iota(jnp.int32, (T, HEAD), 0)
        for g in range(Gp):
            w = POOL_WINDOWS[g]
            sl = slice(g * HEAD, (g + 1) * HEAD)
            xv = x_ref[:, sl]
            cnt = jnp.minimum(row + 1, w).astype(f32)
            d = _trailing_window(xv, w, row) / cnt - xv
            y_ref[:, sl] = (_dot(d, pw_ref[g], "nn") * ps_ref[:, sl]).astype(bf16)

    return pl.pallas_call(
        body, name=name, grid=(1,),
        in_specs=[BS((T, Wd), lambda i: (0, col_x)), BS((Gp, HEAD, HEAD), lambda i: (0, 0, 0)), BS((1, Wd), lambda i: (0, 0)),
                  ANY],
        out_specs=BS((T, Wd), lambda i: (0, col_y)), out_shape=S(mix.shape, bf16), input_output_aliases={3: 0},
        compiler_params=_params(),
    )(P, pw, ps, mix)


def _pool_bwd(name, P, dmix, pw, ps, col_x, col_dy, Wd):
    T = P.shape[0]
    Gp = Wd // HEAD

    def body(x_ref, dy_ref, pw_ref, ps_ref, dx_ref, dpw_ref, dps_ref):
        row = lax.broadcasted_iota(jnp.int32, (T, HEAD), 0)
        for g in range(Gp):
            w = POOL_WINDOWS[g]
            sl = slice(g * HEAD, (g + 1) * HEAD)
            xv = x_ref[:, sl]
            cnt = jnp.minimum(row + 1, w).astype(f32)
            d = (_trailing_window(xv, w, row) / cnt - xv).astype(bf16)
            pwb = pw_ref[g].astype(bf16)
            z = _dot(d, pwb, "nn")
            dy = dy_ref[:, sl]
            dps_ref[:, sl] = jnp.sum(dy * z, axis=0, keepdims=True)
            dzb = (dy * ps_ref[:, sl]).astype(bf16)
            dpw_ref[g] = _dot(d, dzb, "tn")
            dd = _dot(dzb, pwb, "nt")
            dx_ref[:, sl] = (_leading_window(dd / cnt, w, row, T) - dd).astype(bf16)

    return pl.pallas_call(
        body, name=name, grid=(1,),
        in_specs=[BS((T, Wd), lambda i: (0, col_x)), BS((T, Wd), lambda i: (0, col_dy)),
                  BS((Gp, HEAD, HEAD), lambda i: (0, 0, 0)), BS((1, Wd), lambda i: (0, 0))],
        out_specs=[BS((T, Wd), lambda i: (0, 0)), BS((Gp, HEAD, HEAD), lambda i: (0, 0, 0)), BS((1, Wd), lambda i: (0, 0))],
        out_shape=[S((T, Wd), bf16), S((Gp, HEAD, HEAD), f32), S((1, Wd), f32)], compiler_params=_params(),
    )(P, dmix, pw, ps)


def _adamw(name, w, g, m, v):
    R, C = w.shape
    lanes = -(-C // 128) * 128
    tr = _tile(R, max(8, (512 * 1024) // lanes // 8 * 8), 8)
    c1 = 1.0 - ADAM_B1 ** ADAM_STEP
    c2 = 1.0 - ADAM_B2 ** ADAM_STEP

    def body(w_ref, g_ref, m_ref, v_ref, d_ref, nm_ref, nv_ref):
        gv = g_ref[...]
        nm = ADAM_B1 * m_ref[...] + (1.0 - ADAM_B1) * gv
        nv = ADAM_B2 * v_ref[...] + (1.0 - ADAM_B2) * (gv * gv)
        d_ref[...] = -ADAM_LR * ((nm / c1) / (jnp.sqrt(nv / c2) + ADAM_EPS) + ADAM_WD * w_ref[...])
        nm_ref[...] = nm
        nv_ref[...] = nv

    blk = BS((tr, C), lambda i: (i, 0))
    return pl.pallas_call(
        body, name=name, grid=(R // tr,), in_specs=[blk] * 4, out_specs=[blk] * 3, out_shape=[S((R, C), f32)] * 3,
    )(w, g, m, v)


def _adamw_layer(name, layer, w_all, m_all, v_all, g, prev):
    L, R, C = w_all.shape
    lanes = -(-C // 128) * 128
    tr = _tile(R, max(8, (512 * 1024) // lanes // 8 * 8), 8)
    c1 = 1.0 - ADAM_B1 ** ADAM_STEP
    c2 = 1.0 - ADAM_B2 ** ADAM_STEP
    n_prev = 0 if prev is None else 4

    def body(w_ref, m_ref, v_ref, g_ref, *rest):
        go_ref, d_ref, nm_ref, nv_ref = rest[n_prev:]
        gv = g_ref[...]
        nm = ADAM_B1 * m_ref[...] + (1.0 - ADAM_B1) * gv
        nv = ADAM_B2 * v_ref[...] + (1.0 - ADAM_B2) * (gv * gv)
        d_ref[...] = -ADAM_LR * ((nm / c1) / (jnp.sqrt(nv / c2) + ADAM_EPS) + ADAM_WD * w_ref[...])
        nm_ref[...] = nm
        nv_ref[...] = nv
        go_ref[...] = gv

    slab = BS((None, tr, C), lambda r: (layer, r, 0))
    return pl.pallas_call(
        body, name=name, grid=(R // tr,),
        in_specs=[slab, slab, slab, BS((tr, C), lambda r: (r, 0))] + [ANY] * n_prev,
        out_specs=[slab] * 4, out_shape=[S((L, R, C), f32)] * 4,
        input_output_aliases={4 + k: k for k in range(n_prev)},
    )(w_all, m_all, v_all, g, *(prev or ()))


def _chip_of(k):
    return k // 2, k % 2


def _remote(src, dst, send_sems, recv_sems, idx, dev):
    return pltpu.make_async_remote_copy(src_ref=src, dst_ref=dst, send_sem=send_sems.at[idx], recv_sem=recv_sems.at[idx],
                                        device_id=dev, device_id_type=MESH)


def _plan_gather_near(n):
    def plan(refs, ss, rs, base):
        ins, lands = refs[:n], refs[n:]
        x, y, c, j0 = _my_place()
        sib = (x, y, 1 - c)
        sends, recvs = [], []
        for a in range(n):
            half = ins[a].shape[0] // 2
            lo = c * half
            sends.append(_remote(ins[a], lands[a].at[j0], ss, rs, base + 3 * a + 2, sib))
            recvs.append(_remote(lands[a].at[j0], lands[a].at[j0], ss, rs, base + 3 * a + 2, sib))
            for r in (1, 2):
                k = j0 ^ r
                dev = (*_chip_of(k), c)
                sends.append(_remote(ins[a].at[pl.ds(lo, half)], lands[a].at[j0, pl.ds(lo, half)], ss, rs,
                                     base + 3 * a + r - 1, dev))
                landed = lands[a].at[k, pl.ds(lo, half)]
                recvs.append(_remote(landed, landed, ss, rs, base + 3 * a + r - 1, dev))
        return sends, recvs
    return plan, 3 * n


def _plan_gather_relay(n):
    def plan(refs, ss, rs, base):
        x, y, c, j0 = _my_place()
        sib = (x, y, 1 - c)
        sends, recvs = [], []
        for a in range(n):
            half = refs[a].shape[1] // 2
            quarter = half // 2
            lo = c * half
            far = j0 ^ 3
            for r, to, off in ((1, 2, 0), (2, 1, quarter)):
                dev = (*_chip_of(j0 ^ to), c)
                piece = refs[a].at[j0 ^ r, pl.ds(lo + off, quarter)]
                sends.append(_remote(piece, piece, ss, rs, base + 4 * a + to - 1, dev))
                lands_here = refs[a].at[far, pl.ds(lo + off, quarter)]
                recvs.append(_remote(lands_here, lands_here, ss, rs, base + 4 * a + to - 1, dev))
                mine = refs[a].at[j0 ^ r, pl.ds(lo, half)]
                theirs = refs[a].at[j0 ^ r, pl.ds((1 - c) * half, half)]
                sends.append(_remote(mine, mine, ss, rs, base + 4 * a + 1 + r, sib))
                recvs.append(_remote(theirs, theirs, ss, rs, base + 4 * a + 1 + r, sib))
        return sends, recvs
    return plan, 4 * n


def _plan_gather_far(n):
    def plan(refs, ss, rs, base):
        x, y, c, j0 = _my_place()
        sib = (x, y, 1 - c)
        sends, recvs = [], []
        for a in range(n):
            half = refs[a].shape[1] // 2
            mine = refs[a].at[j0 ^ 3, pl.ds(c * half, half)]
            theirs = refs[a].at[j0 ^ 3, pl.ds((1 - c) * half, half)]
            sends.append(_remote(mine, mine, ss, rs, base + a, sib))
            recvs.append(_remote(theirs, theirs, ss, rs, base + a, sib))
        return sends, recvs
    return plan, n


def _plan_sibling_halves(n):
    def plan(refs, ss, rs, base):
        ins, lands = refs[:n], refs[n:]
        x, y, c, _ = _my_place()
        sib = (x, y, 1 - c)
        sends, recvs = [], []
        for a in range(n):
            half = ins[a].shape[1] // 2
            sends.append(_remote(ins[a].at[:, pl.ds((1 - c) * half, half), :], lands[a], ss, rs, base + a, sib))
            recvs.append(_remote(lands[a], lands[a], ss, rs, base + a, sib))
        return sends, recvs
    return plan, n


def _plan_chip_scatter(n):
    def plan(refs, ss, rs, base):
        ins, lands = refs[:n], refs[n:]
        x, y, c, j0 = _my_place()
        sends, recvs = [], []
        for a in range(n):
            for r in (1, 2, 3):
                k = j0 ^ r
                dev = (*_chip_of(k), c)
                sends.append(_remote(ins[a].at[k], lands[a].at[j0], ss, rs, base + 3 * a + r - 1, dev))
                recvs.append(_remote(lands[a].at[k], lands[a].at[k], ss, rs, base + 3 * a + r - 1, dev))
        return sends, recvs
    return plan, 3 * n


def _plan_sibling_join(n):
    def plan(refs, ss, rs, base):
        x, y, c, _ = _my_place()
        sib = (x, y, 1 - c)
        sends, recvs = [], []
        for a in range(n):
            half = refs[a].shape[0] // 2
            mine = refs[a].at[pl.ds(c * half, half)]
            theirs = refs[a].at[pl.ds((1 - c) * half, half)]
            sends.append(_remote(mine, mine, ss, rs, base + a, sib))
            recvs.append(_remote(theirs, theirs, ss, rs, base + a, sib))
        return sends, recvs
    return plan, n


_HBM = pl.BlockSpec(memory_space=pltpu.HBM)
_SEM = pl.BlockSpec(memory_space=pltpu.SEMAPHORE)
_EFFECT = pltpu.SideEffectType.DATAFLOW_SIDE_EFFECTING


def _exchange_start(name, plan, bufs, after):
    plan_fn, n_sems = plan
    n = len(bufs)

    def body(*refs):
        ss, rs, token = refs[n + len(after)], refs[n + len(after) + 1], refs[-1]
        sends, _ = plan_fn(refs[:n], ss, rs, 0)
        for cp in sends:
            cp.start()
        token[...] = jnp.zeros_like(token)

    res = pl.pallas_call(
        body, name=name,
        out_shape=(pltpu.SemaphoreType.DMA((n_sems,)), pltpu.SemaphoreType.DMA((n_sems,)),
                   *[pltpu.HBM(b.shape, b.dtype) for b in bufs], S((8, 128), f32)),
        in_specs=[_HBM] * n + [ANY] * len(after),
        out_specs=(_SEM, _SEM, *[_HBM] * n, pl.BlockSpec(memory_space=pltpu.VMEM)),
        input_output_aliases={k: 2 + k for k in range(n)},
        compiler_params=pltpu.CompilerParams(has_side_effects=_EFFECT),
    )(*[pltpu.with_memory_space_constraint(b, pltpu.HBM) for b in bufs], *after)
    return res[0], res[1], list(res[2:2 + n]), res[-1]


def _exchange_wait(name, plan, send_sems, recv_sems, bufs, after):
    plan_fn, _ = plan
    n = len(bufs)

    def body(*refs):
        ss, rs, token = refs[n], refs[n + 1], refs[-1]
        sends, recvs = plan_fn(refs[:n], ss, rs, 0)
        for cp in recvs:
            cp.wait_recv()
        for cp in sends:
            cp.wait_send()
        token[...] = jnp.zeros_like(token)

    res = pl.pallas_call(
        body, name=name,
        out_shape=(*[pltpu.HBM(b.shape, b.dtype) for b in bufs], S((8, 128), f32)),
        in_specs=[_HBM] * n + [_SEM, _SEM] + [ANY] * len(after),
        out_specs=(*[_HBM] * n, pl.BlockSpec(memory_space=pltpu.VMEM)),
        input_output_aliases={k: k for k in range(n)},
        compiler_params=pltpu.CompilerParams(has_side_effects=_EFFECT),
    )(*bufs, send_sems, recv_sems, *after)
    return list(res[:n]), res[-1]


class _Order:
    def __init__(self, first):
        self.marker = first
        self.token = None

    def _after(self):
        return [self.marker] + ([] if self.token is None else [self.token])

    def start(self, name, plan, bufs):
        ss, rs, thru, self.token = _exchange_start(name, plan, bufs, self._after())
        return name, plan, ss, rs, thru

    def wait(self, handle):
        name, plan, ss, rs, thru = handle
        out, self.token = _exchange_wait(name + "_wait", plan, ss, rs, thru, self._after())
        return out

    def follows(self, small):
        return small if self.token is None else small + self.token[0, 0]

    def done(self, result):
        self.marker = result[(slice(0, 1),) * result.ndim].reshape(1, 1)


def _all_reduce_small(name, g8):
    _, R, L = g8.shape

    def body(g_ref, out_ref, land, red, send1, recv1, send2, recv2):
        x, y, c, _ = _my_place()
        me = 4 * x + 2 * y + c
        peers = []
        for r in range(1, N_DEV):
            q = me ^ r
            peers.append((q, (q // 4, (q // 2) % 2, q % 2)))
        first = []
        for r, (q, dev) in enumerate(peers):
            cp = pltpu.make_async_remote_copy(src_ref=g_ref.at[q], dst_ref=land.at[me], send_sem=send1.at[r],
                                              recv_sem=recv1.at[r], device_id=dev, device_id_type=MESH)
            cp.start()
            first.append(cp)
        land[me] = g_ref[me]
        for r, (q, dev) in enumerate(peers):
            pltpu.make_async_remote_copy(src_ref=land.at[q], dst_ref=land.at[q], send_sem=send1.at[r],
                                         recv_sem=recv1.at[r], device_id=dev, device_id_type=MESH).wait_recv()
        acc = land[0]
        for d in range(1, N_DEV):
            acc = acc + land[d]
        red[...] = acc
        out_ref[me] = acc
        second = []
        for r, (q, dev) in enumerate(peers):
            cp = pltpu.make_async_remote_copy(src_ref=red, dst_ref=out_ref.at[me], send_sem=send2.at[r],
                                              recv_sem=recv2.at[r], device_id=dev, device_id_type=MESH)
            cp.start()
            second.append(cp)
        for r, (q, dev) in enumerate(peers):
            pltpu.make_async_remote_copy(src_ref=out_ref.at[q], dst_ref=out_ref.at[q], send_sem=send2.at[r],
                                         recv_sem=recv2.at[r], device_id=dev, device_id_type=MESH).wait_recv()
        for cp in first + second:
            cp.wait_send()

    vm = pl.BlockSpec(memory_space=pltpu.VMEM)
    return pl.pallas_call(
        body, name=name, in_specs=[vm], out_specs=vm, out_shape=S(g8.shape, f32),
        scratch_shapes=[pltpu.VMEM((N_DEV, R, L), f32), pltpu.VMEM((R, L), f32)]
        + [pltpu.SemaphoreType.DMA((N_DEV - 1,))] * 4,
        compiler_params=_params(),
    )(g8)


def _pair_sum(name, g4, sib):
    _, rows, cols = g4.shape
    half = rows // 2
    lanes = -(-cols // 128) * 128
    tr = _tile(half, max(16, (512 * 1024) // lanes // 16 * 16), 16)
    nb = half // tr

    def body(g_ref, s_ref, pb_ref, own_ref):
        j = pl.program_id(1)
        t = g_ref[...] + s_ref[...]
        pb_ref[...] = t.astype(bf16)

        @pl.when(j == _my_place()[3])
        def _():
            own_ref[...] = t

    return pl.pallas_call(
        body, name=name, grid=(nb, N_CHIPS),
        in_specs=[BS((None, tr, cols), lambda i, j: (j, lax.axis_index("c") * nb + i, 0)),
                  BS((None, tr, cols), lambda i, j: (j, i, 0))],
        out_specs=[BS((None, tr, cols), lambda i, j: (j, i, 0)), BS((tr, cols), lambda i, j: (i, 0))],
        out_shape=[S((N_CHIPS, half, cols), bf16), S((half, cols), f32)],
    )(g4, sib)


def _chip_sum(name, own, got):
    half, cols = own.shape
    lanes = -(-cols // 128) * 128
    tr = _tile(half, max(16, (512 * 1024) // lanes // 16 * 16), 16)
    nb = half // tr

    def body(own_ref, *rest):
        got_refs, o_ref = rest[:N_CHIPS], rest[N_CHIPS]
        j0 = _my_place()[3]
        acc = None
        for k in range(N_CHIPS):
            t = jnp.where(j0 == k, own_ref[...], got_refs[k][...].astype(f32))
            acc = t if acc is None else acc + t
        o_ref[...] = acc

    def slot(k):
        return BS((None, tr, cols), lambda i: (jnp.where(_my_place()[3] == k, (k + 1) % N_CHIPS, k), i, 0))

    return pl.pallas_call(
        body, name=name, grid=(nb,),
        in_specs=[BS((tr, cols), lambda i: (i, 0))] + [slot(k) for k in range(N_CHIPS)],
        out_specs=BS((tr, cols), lambda i: (lax.axis_index("c") * nb + i, 0)),
        out_shape=S((2 * half, cols), f32),
    )(own, got, got, got, got)


def _rows_of(size):
    return -(-size // 1024) * 8


def _pack_rows(arrs, n_rows):
    parts = []
    for a in arrs:
        rows = _rows_of(a.size)
        if a.size % 128 == 0:
            part = a.astype(f32).reshape(-1, 128)
            part = jnp.pad(part, ((0, rows - part.shape[0]), (0, 0)))
        else:
            part = jnp.pad(a.reshape(-1).astype(f32), (0, rows * 128 - a.size)).reshape(rows, 128)
        parts.append(part)
    used = sum(p.shape[0] for p in parts)
    return jnp.concatenate(parts + [jnp.zeros((n_rows - used, 128), f32)], axis=0)


def _unpack_rows(packed, shapes):
    out, row = [], 0
    for shp in shapes:
        size = 1
        for d in shp:
            size *= d
        rows = packed[row:row + _rows_of(size)]
        out.append(rows[:size // 128].reshape(shp) if size % 128 == 0 else rows.reshape(-1)[:size].reshape(shp))
        row += _rows_of(size)
    return out


def kernel(x, p, norm_mix, w_in, q_norm, k_norm, forget_bias, gmlp_v_norm, gmlp_w_s, gmlp_b_s, pool_w, pool_scale, w_out, norm_ffn, w_ffn_gate, w_ffn_up, w_ffn_down, norm_ple, w_ple_gate, w_ple_proj, loss_target, m_norm_mix, m_w_in, m_q_norm, m_k_norm, m_forget_bias, m_gmlp_v_norm, m_gmlp_w_s, m_gmlp_b_s, m_pool_w, m_pool_scale, m_w_out, m_norm_ffn, m_w_ffn_gate, m_w_ffn_up, m_w_ffn_down, m_norm_ple, m_w_ple_gate, m_w_ple_proj, v_norm_mix, v_w_in, v_q_norm, v_k_norm, v_forget_bias, v_gmlp_v_norm, v_gmlp_w_s, v_gmlp_b_s, v_pool_w, v_pool_scale, v_w_out, v_norm_ffn, v_w_ffn_gate, v_w_ffn_up, v_w_ffn_down, v_norm_ple, v_w_ple_gate, v_w_ple_proj):
    W = dict(norm_mix=norm_mix, w_in=w_in, q_norm=q_norm, k_norm=k_norm, forget_bias=forget_bias,
             gmlp_v_norm=gmlp_v_norm, gmlp_w_s=gmlp_w_s, gmlp_b_s=gmlp_b_s, pool_w=pool_w, pool_scale=pool_scale,
             w_out=w_out, norm_ffn=norm_ffn, w_ffn_gate=w_ffn_gate, w_ffn_up=w_ffn_up, w_ffn_down=w_ffn_down,
             norm_ple=norm_ple, w_ple_gate=w_ple_gate, w_ple_proj=w_ple_proj)
    M = dict(norm_mix=m_norm_mix, w_in=m_w_in, q_norm=m_q_norm, k_norm=m_k_norm, forget_bias=m_forget_bias,
             gmlp_v_norm=m_gmlp_v_norm, gmlp_w_s=m_gmlp_w_s, gmlp_b_s=m_gmlp_b_s, pool_w=m_pool_w,
             pool_scale=m_pool_scale, w_out=m_w_out, norm_ffn=m_norm_ffn, w_ffn_gate=m_w_ffn_gate,
             w_ffn_up=m_w_ffn_up, w_ffn_down=m_w_ffn_down, norm_ple=m_norm_ple, w_ple_gate=m_w_ple_gate,
             w_ple_proj=m_w_ple_proj)
    V = dict(norm_mix=v_norm_mix, w_in=v_w_in, q_norm=v_q_norm, k_norm=v_k_norm, forget_bias=v_forget_bias,
             gmlp_v_norm=v_gmlp_v_norm, gmlp_w_s=v_gmlp_w_s, gmlp_b_s=v_gmlp_b_s, pool_w=v_pool_w,
             pool_scale=v_pool_scale, w_out=v_w_out, norm_ffn=v_norm_ffn, w_ffn_gate=v_w_ffn_gate,
             w_ffn_up=v_w_ffn_up, w_ffn_down=v_w_ffn_down, norm_ple=v_norm_ple, w_ple_gate=v_w_ple_gate,
             w_ple_proj=v_w_ple_proj)

    L = w_in.shape[0]
    _, T, D = x.shape
    A, Wd = D // 2, D // 4
    H = A // HEAD
    G = gmlp_w_s.shape[1]
    Gp = pool_w.shape[1]
    DP4 = w_in.shape[2]
    DP = N_CHIPS * DP4
    NM = 3 * A + 3 * Wd
    FS = w_ffn_gate.shape[2]
    FF = N_CHIPS * FS
    DS = D // N_CHIPS
    PL = p.shape[-1]
    assert Wd // G == HEAD and Wd // Gp == HEAD and DP == NM + H and H <= HEAD
    assert all(w & (w - 1) == 0 for w in POOL_WINDOWS[:Gp])
    tb = _tile(T, 256, HEAD)
    nb = T // tb
    tm = _tile(T, 512, 16)
    tmw = _tile(T, 1024, 16)
    tn = _tile(NM, 512, 128)
    tnd = _tile(D, 512, 128)
    tkd = _tile(D, 1024, 128)
    tnw = _tile(D, 1024, 128)
    col_gu, col_gv, col_xp = 3 * A // HEAD, (3 * A + Wd) // HEAD, (3 * A + 2 * Wd) // Wd
    col_dg, col_dp = A // HEAD, (A + Wd) // Wd

    order = _Order(x[0, :1, :1])

    def gather_near(i, names, tag=""):
        shards = [_cast_layer(f"cast_{n}_{i}", W[n], i) for n in names]
        lands = [lax.empty((N_CHIPS,) + s.shape, bf16) for s in shards]
        return names, order.start(f"ag_near_{i}{tag}", _plan_gather_near(len(names)), shards + lands)

    def gather_relay(i, near, tag=""):
        names, handle = near
        return names, order.start(f"ag_relay_{i}{tag}", _plan_gather_relay(len(names)), order.wait(handle)[len(names):])

    def gather_far(i, relayed, tag=""):
        names, handle = relayed
        return names, order.start(f"ag_far_{i}{tag}", _plan_gather_far(len(names)), order.wait(handle))

    def gathered(far):
        names, handle = far
        g = dict(zip(names, order.wait(handle)))
        out = {}
        if "w_in" in g:
            def cols(lo, hi):
                parts = []
                for j in range(N_CHIPS):
                    s, e = max(lo, j * DP4), min(hi, (j + 1) * DP4)
                    if s < e:
                        parts.append(g["w_in"][j][:, s - j * DP4:e - j * DP4])
                return parts

            out["w_main"] = jnp.concatenate(cols(0, 3 * A) + cols(3 * A + H, DP), axis=1)
            out["w_f"] = jnp.pad(jnp.concatenate(cols(3 * A, 3 * A + H), axis=1), ((0, 0), (0, HEAD - H)))
        if "w_out" in g:
            out["w_out"] = g["w_out"].reshape(D, D)
        if "w_ffn_gate" in g:
            out.update(w_gate=g["w_ffn_gate"], w_up=g["w_ffn_up"], w_down=g["w_ffn_down"].reshape(FF, D),
                       w_pg=g["w_ple_gate"].reshape(D, D), w_pp=g["w_ple_proj"])
        return out

    Wf = [None] * L
    relayed = gather_relay(0, gather_near(0, BIG[:2], "a"), "a")
    near_rest = gather_near(0, BIG[2:], "c")
    Wf[0] = gathered(gather_far(0, relayed, "a"))
    near = relayed = None

    h = x.reshape(T, D)
    pb16 = p.reshape(L, T, PL).astype(bf16)
    saved = []

    for i in range(L):
        w = Wf[i]
        sv = dict(h0=h)
        xn1 = _rms_fwd(f"rms1_{i}", h, order.follows(norm_mix[i]))
        (P,) = _matmul(f"proj_{i}", "nn", (T // tmw, NM // tn),
                       [((xn1, BS((tmw, D), lambda i, j: (i, 0))), (w["w_main"], BS((D, tn), lambda i, j: (0, j))))], [],
                       [((T, NM), f32, BS((tmw, tn), lambda i, j: (i, j)))])
        (Pf,) = _matmul(f"projf_{i}", "nn", (T // tm, 1),
                        [((xn1, BS((tm, D), lambda i, j: (i, 0))), (w["w_f"], BS((D, HEAD), lambda i, j: (0, 0))))], [],
                        [((T, HEAD), f32, BS((tm, HEAD), lambda i, j: (i, 0)))])
        fb = jnp.pad(forget_bias[i], (0, HEAD - H)).reshape(1, HEAD)
        cc, ct = _fgate_fwd(f"fgate_{i}", Pf, fb)
        c_col = ct[:H].reshape(H, T, 1)
        c_row = ct[:H].reshape(H, nb, 1, tb)
        qg, kg = q_norm[i].reshape(1, HEAD), k_norm[i].reshape(1, HEAD)
        qn, kn, vb = _qk_norm(f"qknorm_{i}", P, qg, kg, A)
        mix, o32, lse = _attn_fwd(f"attn_{i}", qn, kn, vb, c_col, c_row, tb, D)
        if i == 0:
            order.done(o32)
            relayed_rest = gather_relay(0, near_rest, "c")
            near = gather_near(1, BIG) if L > 1 else None
        gain = gmlp_v_norm[i].reshape(G, 1, HEAD)
        bs = gmlp_b_s[i].reshape(G, HEAD, 1)
        mix = _gmlp_fwd(f"gmlp_{i}", P, mix, order.follows(gain), gmlp_w_s[i], bs, col_gu, col_gv, col_dg, Wd)
        ps = pool_scale[i].reshape(1, Wd)
        mix = _pool_fwd(f"pool_{i}", P, mix, pool_w[i], ps, col_xp, col_dp, Wd)
        (h1,) = _matmul(f"out_{i}", "nn", (T // tmw, D // tnd),
                        [((mix, BS((tmw, D), lambda i, j: (i, 0))), (w["w_out"], BS((D, tnd), lambda i, j: (0, j))))],
                        [(h, BS((tmw, tnd), lambda i, j: (i, j)))],
                        [((T, D), f32, BS((tmw, tnd), lambda i, j: (i, j)))],
                        epilogue=lambda accs, ex: (accs[0] + ex[0],))
        xn2 = _rms_fwd(f"rms2_{i}", h1, norm_ffn[i])
        order.done(xn2)
        if i == 0:
            w.update(gathered(gather_far(0, relayed_rest, "c")))
        elif i + 1 < L:
            relayed = gather_relay(i + 1, near)
            near = gather_near(i + 2, BIG) if i + 2 < L else None

        def ffn_epi(accs, ex):
            g_, u_ = accs
            return g_, u_, g_ * jax.nn.sigmoid(g_) * u_

        ffo = BS((tm, FS), lambda i, j: (i, j))
        Gt, Ut, act = _matmul(f"ffn1_{i}", "nn", (T // tm, N_CHIPS),
                              [((xn2, BS((tm, D), lambda i, j: (i, 0))), (w["w_gate"], BS((None, D, FS), lambda i, j: (j, 0, 0)))),
                               ((xn2, BS((tm, D), lambda i, j: (i, 0))), (w["w_up"], BS((None, D, FS), lambda i, j: (j, 0, 0))))],
                              [], [((T, FF), bf16, ffo)] * 3, epilogue=ffn_epi, after=order.token)
        (h2,) = _matmul(f"ffn2_{i}", "nn", (T // tmw, D // tnd),
                        [((act, BS((tmw, FF), lambda i, j: (i, 0))), (w["w_down"], BS((FF, tnd), lambda i, j: (0, j))))],
                        [(h1, BS((tmw, tnd), lambda i, j: (i, j)))],
                        [((T, D), f32, BS((tmw, tnd), lambda i, j: (i, j)))],
                        epilogue=lambda accs, ex: (accs[0] + ex[0],))
        if i == 0 and L > 1:
            order.done(h2)
            relayed = gather_relay(1, near)
            near = gather_near(2, BIG) if L > 2 else None
        xn3 = _rms_fwd(f"rms3_{i}", h2, order.follows(norm_ple[i]))

        def ple_epi(accs, ex):
            gate = jax.nn.sigmoid(accs[0])
            return ex[0] + accs[1] * gate, gate, accs[1]

        dso = BS((tmw, DS), lambda i, j: (i, j))
        h3, gate, e = _matmul(f"ple_{i}", "nn", (T // tmw, N_CHIPS),
                              [((xn3, BS((tmw, D), lambda i, j: (i, 0))), (w["w_pg"], BS((D, DS), lambda i, j: (0, j)))),
                               ((pb16[i], BS((tmw, PL), lambda i, j: (i, 0))), (w["w_pp"], BS((None, PL, DS), lambda i, j: (j, 0, 0))))],
                              [(h2, dso)], [((T, D), f32, dso), ((T, D), bf16, dso), ((T, D), bf16, dso)], epilogue=ple_epi)
        sv.update(xn1=xn1, P=P, Pf=Pf, fb=fb, c_col=c_col, c_row=c_row, qn=qn, kn=kn, vb=vb, o32=o32, lse=lse,
                  mix=mix, h1=h1, xn2=xn2, Gt=Gt, Ut=Ut, act=act, h2=h2, xn3=xn3, gate=gate, e=e)
        saved.append(sv)
        h = h3
        order.done(h3)
        if i + 1 < L:
            Wf[i + 1] = gathered(gather_far(i + 1, relayed))

    dh, loss_tile = _loss_grad("loss", h, loss_target.reshape(T, D))

    small_g = {n: [None] * L for n in SMALL}
    big_out = {}

    def stage_a(u):
        n_u = len(u["names"])
        lands = [lax.empty((N_CHIPS, g.shape[1] // 2, g.shape[2]), f32) for g in u["grads"]]
        u["a"] = order.start(f"rs_a_{u['tag']}", _plan_sibling_halves(n_u), u["grads"] + lands)

    def stage_pair(u):
        n_u = len(u["names"])
        out = order.wait(u["a"])
        pairs = [_pair_sum(f"rs_pair_{u['tag']}_{a}", out[a], out[n_u + a]) for a in range(n_u)]
        u["pb"], u["own"] = [t[0] for t in pairs], [t[1] for t in pairs]

    def stage_b(u):
        lands = [lax.empty(t.shape, bf16) for t in u["pb"]]
        u["b"] = order.start(f"rs_b_{u['tag']}", _plan_chip_scatter(len(u["names"])), u["pb"] + lands)

    def stage_sum(u):
        n_u = len(u["names"])
        out = order.wait(u["b"])
        u["sum"] = [_chip_sum(f"rs_sum_{u['tag']}_{a}", u["own"][a], out[n_u + a]) for a in range(n_u)]

    def stage_c(u):
        u["c"] = order.start(f"rs_c_{u['tag']}", _plan_sibling_join(len(u["names"])), u["sum"])

    def stage_adamw(u):
        for n, r in zip(u["names"], order.wait(u["c"])):
            big_out[n] = _adamw_layer(f"adamw_{n}_{u['layer']}", u["layer"], W[n], M[n], V[n],
                                      r.reshape(W[n].shape[1:]), big_out.get(n))
    dh1 = prev_f = prev_m = None
    for i in reversed(range(L)):
        w, sv = Wf[i], saved[i]
        if dh1 is not None:
            dh, _, dg = _rms_bwd(f"rms1_bw_{i + 1}", dxn1, saved[i + 1]["h0"], order.follows(norm_mix[i + 1]), dh1)
            small_g["norm_mix"][i + 1] = dg.reshape(D)
        de, dz = _ple_bwd_elem(f"ple_bw_{i}", dh, sv["gate"], sv["e"])
        (d_wpp,) = _matmul(f"d_wpp_{i}", "tn", (N_CHIPS, 1),
                           [((pb16[i], BS((T, PL), lambda i, j: (0, 0))), (de, BS((T, DS), lambda i, j: (0, i))))], [],
                           [((N_CHIPS, PL, DS), f32, BS((None, PL, DS), lambda i, j: (i, 0, 0)))])
        (d_wpg,) = _matmul(f"d_wpg_{i}", "tn", (D // tkd, D // tnd),
                           [((sv["xn3"], BS((T, tkd), lambda i, j: (0, i))), (dz, BS((T, tnd), lambda i, j: (0, j))))], [],
                           [((D, D), f32, BS((tkd, tnd), lambda i, j: (i, j)))])
        order.done(dz)
        if prev_m is not None:
            stage_pair(prev_m)
        (dxn3,) = _matmul(f"d_xn3_{i}", "nt", (T // tmw, D // tnd),
                          [((dz, BS((tmw, D), lambda i, j: (i, 0))), (w["w_pg"], BS((tnd, D), lambda i, j: (j, 0))))], [],
                          [((T, D), f32, BS((tmw, tnd), lambda i, j: (i, j)))], after=order.token)
        dh2, dh2b, dg = _rms_bwd(f"rms3_bw_{i}", dxn3, sv["h2"], norm_ple[i], dh)
        small_g["norm_ple"][i] = dg.reshape(D)

        def dffn_epi(accs, ex):
            da = accs[0]
            g_, u_ = ex[0].astype(f32), ex[1].astype(f32)
            sg = jax.nn.sigmoid(g_)
            return da * u_ * (sg * (1.0 + g_ * (1.0 - sg))), da * (g_ * sg)

        ffo = BS((tm, FS), lambda j, i: (i, j))
        dG, dU = _matmul(f"d_act_{i}", "nt", (N_CHIPS, T // tm),
                         [((dh2b, BS((tm, D), lambda j, i: (i, 0))), (w["w_down"], BS((FS, D), lambda j, i: (j, 0))))],
                         [(sv["Gt"], ffo), (sv["Ut"], ffo)], [((T, FF), bf16, ffo)] * 2, epilogue=dffn_epi)
        (d_wd,) = _matmul(f"d_wd_{i}", "tn", (N_CHIPS, D // tnw),
                          [((sv["act"], BS((T, FS), lambda i, j: (0, i))), (dh2b, BS((T, tnw), lambda i, j: (0, j))))], [],
                          [((FF, D), f32, BS((FS, tnw), lambda i, j: (i, j)))])
        gu_out = BS((None, tnd, FS), lambda j, i: (j, i, 0))
        d_wg, d_wu = _matmul(f"d_wgu_{i}", "tn", (N_CHIPS, D // tnd),
                             [((sv["xn2"], BS((T, tnd), lambda j, i: (0, i))), (dG, BS((T, FS), lambda j, i: (0, j)))),
                              ((sv["xn2"], BS((T, tnd), lambda j, i: (0, i))), (dU, BS((T, FS), lambda j, i: (0, j))))], [],
                             [((N_CHIPS, D, FS), f32, gu_out)] * 2, epilogue=lambda accs, ex: (accs[0], accs[1]))
        order.done(dG)
        unit_f = dict(tag=f"{i}f", layer=i, names=["w_ffn_gate", "w_ffn_up", "w_ffn_down", "w_ple_gate", "w_ple_proj"],
                      grads=[d_wg, d_wu, d_wd.reshape(N_CHIPS, FS, D), d_wpg.reshape(N_CHIPS, DS, D), d_wpp])
        stage_a(unit_f)
        if prev_f is not None:
            stage_sum(prev_f)
            stage_c(prev_f)
        if prev_m is not None:
            stage_b(prev_m)
        tm2 = _tile(T, 256, 16)
        (dxn2,) = _matmul(f"d_xn2_{i}", "nt", (D // tnd, T // tm2),
                          [((dG, BS((tm2, FF), lambda j, i: (i, 0))), (w["w_gate"], BS((N_CHIPS, tnd, FS), lambda j, i: (0, j, 0)))),
                           ((dU, BS((tm2, FF), lambda j, i: (i, 0))), (w["w_up"], BS((N_CHIPS, tnd, FS), lambda j, i: (0, j, 0))))], [],
                          [((T, D), f32, BS((tm2, tnd), lambda j, i: (i, j)))], after=order.token)
        dh1, dh1b, dg = _rms_bwd(f"rms2_bw_{i}", dxn2, sv["h1"], norm_ffn[i], dh2)
        small_g["norm_ffn"][i] = dg.reshape(D)
        (dmix,) = _matmul(f"d_mix_{i}", "nt", (T // tmw, D // tnd),
                          [((dh1b, BS((tmw, D), lambda i, j: (i, 0))), (w["w_out"], BS((tnd, D), lambda i, j: (j, 0))))], [],
                          [((T, D), f32, BS((tmw, tnd), lambda i, j: (i, j)))])
        (d_wout,) = _matmul(f"d_wout_{i}", "tn", (D // tkd, D // tnd),
                            [((sv["mix"], BS((T, tkd), lambda i, j: (0, i))), (dh1b, BS((T, tnd), lambda i, j: (0, j))))], [],
                            [((D, D), f32, BS((tkd, tnd), lambda i, j: (i, j)))])
        order.done(dmix)
        stage_pair(unit_f)
        stage_b(unit_f)
        if prev_f is not None:
            stage_adamw(prev_f)
        qg, kg = q_norm[i].reshape(1, HEAD), k_norm[i].reshape(1, HEAD)
        dq, dk, dv, dc_row, dqg, dkg = _attn_bwd(f"attn_bw_{i}", sv["qn"], sv["kn"], sv["vb"], sv["o32"], dmix,
                                                 sv["lse"], sv["c_col"], sv["c_row"], sv["P"], order.follows(qg), kg, tb)
        small_g["q_norm"][i] = dqg.reshape(HEAD)
        small_g["k_norm"][i] = dkg.reshape(HEAD)
        dct = jnp.pad(dc_row.reshape(H, T), ((0, HEAD - H), (0, 0)))
        dPf, dfb = _fgate_bwd(f"fgate_bw_{i}", dct, sv["Pf"], sv["fb"])
        small_g["forget_bias"][i] = dfb[0, :H]
        gain = gmlp_v_norm[i].reshape(G, 1, HEAD)
        bs = gmlp_b_s[i].reshape(G, HEAD, 1)
        dgu, dgv, dws, dbs, dgain = _gmlp_bwd(f"gmlp_bw_{i}", sv["P"], dmix, gain, gmlp_w_s[i], bs, col_gu, col_gv,
                                              col_dg, Wd)
        small_g["gmlp_w_s"][i] = dws
        small_g["gmlp_b_s"][i] = dbs.reshape(G, HEAD)
        small_g["gmlp_v_norm"][i] = dgain.reshape(G, HEAD)
        ps = pool_scale[i].reshape(1, Wd)
        dxp, dpw, dps = _pool_bwd(f"pool_bw_{i}", sv["P"], dmix, pool_w[i], ps, col_xp, col_dp, Wd)
        small_g["pool_w"][i] = dpw
        small_g["pool_scale"][i] = dps.reshape(Wd)
        dP = jnp.concatenate([dq, dk, dv, dgu, dgv, dxp], axis=1)
        (d_wmain,) = _matmul(f"d_wmain_{i}", "tn", (D // tkd, NM // tn),
                             [((sv["xn1"], BS((T, tkd), lambda i, j: (0, i))), (dP, BS((T, tn), lambda i, j: (0, j))))], [],
                             [((D, NM), f32, BS((tkd, tn), lambda i, j: (i, j)))])
        (d_wf,) = _matmul(f"d_wf_{i}", "tn", (D // tnd, 1),
                          [((sv["xn1"], BS((T, tnd), lambda i, j: (0, i))), (dPf, BS((T, HEAD), lambda i, j: (0, 0))))], [],
                          [((D, HEAD), f32, BS((tnd, HEAD), lambda i, j: (i, 0)))])
        def win_cols(lo, hi):
            out = []
            for src, s0, e0, off in ((d_wmain, 0, 3 * A, 0), (d_wf, 3 * A, 3 * A + H, 3 * A), (d_wmain, 3 * A + H, DP, H)):
                s, e = max(lo, s0), min(hi, e0)
                if s < e:
                    out.append(src[:, s - off:e - off])
            return out

        d_win4 = jnp.stack([jnp.concatenate(win_cols(j * DP4, (j + 1) * DP4), axis=1) for j in range(N_CHIPS)])
        order.done(dP)
        unit_m = dict(tag=f"{i}m", layer=i, names=["w_in", "w_out"], grads=[d_win4, d_wout.reshape(N_CHIPS, DS, D)])
        stage_a(unit_m)
        if prev_m is not None:
            stage_sum(prev_m)
            stage_c(prev_m)
        (dxn1,) = _matmul(f"d_xn1_{i}", "nt", (D // tnd, T // tm),
                          [((dP, BS((tm, NM), lambda j, i: (i, 0))), (w["w_main"], BS((tnd, NM), lambda j, i: (j, 0)))),
                           ((dPf, BS((tm, HEAD), lambda j, i: (i, 0))), (w["w_f"], BS((tnd, HEAD), lambda j, i: (j, 0))))], [],
                          [((T, D), f32, BS((tm, tnd), lambda j, i: (i, j)))], after=order.token)
        order.done(dxn1)
        if prev_m is not None:
            stage_adamw(prev_m)
        prev_f, prev_m = unit_f, unit_m

    dh, _, dg = _rms_bwd("rms1_bw_0", dxn1, saved[0]["h0"], order.follows(norm_mix[0]), dh1)
    small_g["norm_mix"][0] = dg.reshape(D)
    order.done(dh)
    stage_pair(prev_m)
    stage_b(prev_m)
    stage_sum(prev_f)
    stage_c(prev_f)

    small_full = {n: jnp.stack(small_g[n]) for n in SMALL}
    small_shapes = [W[n].shape for n in SMALL]
    n_rows = sum(_rows_of(W[n].size) for n in SMALL) + _rows_of(1)
    rows8 = -(-n_rows // 64) * 8
    packed = order.follows(_pack_rows([small_full[n] for n in SMALL] + [loss_tile[0, :1]], N_DEV * rows8))
    summed = _all_reduce_small("allreduce_small", packed.reshape(N_DEV, rows8, 128)).reshape(-1, 128)
    order.done(summed)
    stage_adamw(prev_f)
    stage_sum(prev_m)
    stage_c(prev_m)
    stage_adamw(prev_m)
    *small_grads, loss_row = _unpack_rows(summed, small_shapes + [(1,)])
    grads = dict(zip(SMALL, small_grads))
    loss = loss_row[0]

    wp, mp, vp = (_pack_rows([t[n] for n in SMALL], N_DEV * rows8) for t in (W, M, V))
    delta, new_m, new_v = (dict(zip(SMALL, _unpack_rows(t, small_shapes)))
                           for t in _adamw("adamw_small", wp, summed, mp, vp))
    for n in BIG:
        grads[n], delta[n], new_m[n], new_v[n] = big_out[n]

    return (loss, dh.reshape(1, T, D), *[grads[n] for n in WEIGHTS], *[delta[n] for n in WEIGHTS],
            *[new_m[n] for n in WEIGHTS], *[new_v[n] for n in WEIGHTS])
```

```python
import jax
import jax.numpy as jnp
from jax import lax
from jax.experimental import pallas as pl
from jax.experimental.pallas import tpu as pltpu

f32, bf16 = jnp.float32, jnp.bfloat16
S = jax.ShapeDtypeStruct
BS = pl.BlockSpec
ANY = pl.BlockSpec(memory_space=pl.ANY)
MESH = pl.DeviceIdType.MESH

EPS = 1e-6
HEAD = 128
POOL_WINDOWS = (2, 4, 8, 16)
NEG = -1e30
N_CHIPS = 4
N_DEV = 8
VMEM_LIMIT = 56 * 1024 * 1024

ADAM_LR, ADAM_B1, ADAM_B2, ADAM_EPS, ADAM_WD, ADAM_STEP = 0.001, 0.9, 0.999, 1e-08, 0.01, 10

BIG = ("w_in", "w_out", "w_ffn_gate", "w_ffn_up", "w_ffn_down", "w_ple_gate", "w_ple_proj")
SMALL = ("norm_mix", "q_norm", "k_norm", "forget_bias", "gmlp_v_norm", "gmlp_w_s", "gmlp_b_s", "pool_w",
         "pool_scale", "norm_ffn", "norm_ple")
WEIGHTS = ("norm_mix", "w_in", "q_norm", "k_norm", "forget_bias", "gmlp_v_norm", "gmlp_w_s", "gmlp_b_s", "pool_w",
           "pool_scale", "w_out", "norm_ffn", "w_ffn_gate", "w_ffn_up", "w_ffn_down", "norm_ple", "w_ple_gate",
           "w_ple_proj")


def _tile(n, target, mult):
    best = None
    for t in range(mult, min(n, target) + 1, mult):
        if n % t == 0:
            best = t
    return best if best is not None else n


def _params(**kw):
    return pltpu.CompilerParams(vmem_limit_bytes=VMEM_LIMIT, **kw)


def _dot(a, b, kind):
    dims = {"nn": (((1,), (0,)), ((), ())), "nt": (((1,), (1,)), ((), ())), "tn": (((0,), (0,)), ((), ()))}[kind]
    return lax.dot_general(a.astype(bf16), b.astype(bf16), dims, preferred_element_type=f32)


def _heads_per_program(n_heads, want=2):
    while n_heads % want:
        want //= 2
    return want


def _my_place():
    x, y, c = lax.axis_index("x"), lax.axis_index("y"), lax.axis_index("c")
    return x, y, c, 2 * x + y


def _matmul(name, kind, grid, pairs, extras, outs, epilogue=None, after=None):
    n_p, n_e = len(pairs), len(extras)
    tokens = [] if after is None else [(after, BS((8, 128), lambda *_: (0, 0)))]

    def body(*refs):
        a_refs, b_refs = refs[:n_p], refs[n_p:2 * n_p]
        e_refs = refs[2 * n_p:2 * n_p + n_e]
        o_refs = refs[2 * n_p + n_e + len(tokens):]
        accs = []
        for a_ref, b_ref in zip(a_refs, b_refs):
            if len(b_ref.shape) == 3:
                w = b_ref.shape[2]
                acc = None
                for s in range(b_ref.shape[0]):
                    d = _dot(a_ref[:, s * w:(s + 1) * w], b_ref[s], kind)
                    acc = d if acc is None else acc + d
            else:
                acc = _dot(a_ref[...], b_ref[...], kind)
            accs.append(acc)
        if epilogue is None:
            res = accs[0]
            for t in accs[1:]:
                res = res + t
            res = (res,)
        else:
            res = epilogue(accs, [e[...] for e in e_refs])
        for o_ref, o in zip(o_refs, res):
            o_ref[...] = o.astype(o_ref.dtype)

    in_arrays = [p[0][0] for p in pairs] + [p[1][0] for p in pairs] + [e[0] for e in extras + tokens]
    in_specs = [p[0][1] for p in pairs] + [p[1][1] for p in pairs] + [e[1] for e in extras + tokens]
    res = pl.pallas_call(
        body, name=name, grid=grid, in_specs=in_specs,
        out_specs=[o[2] for o in outs], out_shape=[S(o[0], o[1]) for o in outs],
        compiler_params=_params(),
    )(*in_arrays)
    return res


def _rms_fwd(name, x, g):
    T, D = x.shape
    tr = _tile(T, 256, 8)

    def body(x_ref, g_ref, o_ref):
        xv = x_ref[...]
        r = lax.rsqrt(jnp.mean(xv * xv, axis=-1, keepdims=True) + EPS)
        o_ref[...] = (xv * r * g_ref[...]).astype(o_ref.dtype)

    return pl.pallas_call(
        body, name=name, grid=(T // tr,),
        in_specs=[BS((tr, D), lambda i: (i, 0)), BS((1, D), lambda i: (0, 0))],
        out_specs=BS((tr, D), lambda i: (i, 0)), out_shape=S((T, D), bf16),
    )(x, g.reshape(1, D))


def _rms_bwd(name, dxn, x, g, dres):
    T, D = x.shape
    tr = _tile(T, 256, 8)

    def body(dxn_ref, x_ref, g_ref, dres_ref, dx_ref, dxb_ref, dg_ref):
        i = pl.program_id(0)
        xv = x_ref[...]
        r = lax.rsqrt(jnp.mean(xv * xv, axis=-1, keepdims=True) + EPS)
        xh = xv * r
        dxn_v = dxn_ref[...]
        dxh = dxn_v * g_ref[...]
        dx = dres_ref[...] + r * (dxh - xh * jnp.mean(dxh * xh, axis=-1, keepdims=True))
        dx_ref[...] = dx
        dxb_ref[...] = dx.astype(bf16)
        part = jnp.sum(dxn_v * xh, axis=0, keepdims=True)

        @pl.when(i == 0)
        def _():
            dg_ref[...] = part

        @pl.when(i > 0)
        def _():
            dg_ref[...] += part

    row = BS((tr, D), lambda i: (i, 0))
    vec = BS((1, D), lambda i: (0, 0))
    return pl.pallas_call(
        body, name=name, grid=(T // tr,),
        in_specs=[row, row, vec, row], out_specs=[row, row, vec],
        out_shape=[S((T, D), f32), S((T, D), bf16), S((1, D), f32)],
    )(dxn, x, g.reshape(1, D), dres)


def _loss_grad(name, y, tgt):
    T, D = y.shape
    tr = _tile(T, 256, 8)

    def body(y_ref, t_ref, dy_ref, l_ref):
        i = pl.program_id(0)
        e = y_ref[...] - t_ref[...]
        dy_ref[...] = e * (1.0 / D)
        part = 0.5 * jnp.sum(jnp.mean(e * e, axis=-1, keepdims=True), axis=0, keepdims=True)

        @pl.when(i == 0)
        def _():
            l_ref[...] = jnp.zeros_like(l_ref)

        l_ref[...] += jnp.broadcast_to(part, l_ref.shape)

    row = BS((tr, D), lambda i: (i, 0))
    return pl.pallas_call(
        body, name=name, grid=(T // tr,), in_specs=[row, row],
        out_specs=[row, BS((8, 128), lambda i: (0, 0))],
        out_shape=[S((T, D), f32), S((8, 128), f32)],
    )(y, tgt)


def _ple_bwd_elem(name, dh, gate, e):
    T, D = dh.shape
    tr = _tile(T, 256, 16)

    def body(dh_ref, g_ref, e_ref, de_ref, dz_ref):
        d = dh_ref[...]
        g = g_ref[...].astype(f32)
        de_ref[...] = (d * g).astype(bf16)
        dz_ref[...] = (d * e_ref[...].astype(f32) * g * (1.0 - g)).astype(bf16)

    row = BS((tr, D), lambda i: (i, 0))
    return pl.pallas_call(
        body, name=name, grid=(T // tr,), in_specs=[row, row, row], out_specs=[row, row],
        out_shape=[S((T, D), bf16), S((T, D), bf16)],
    )(dh, gate, e)


def _cast_layer(name, w_all, layer):
    _, R, C = w_all.shape
    lanes = -(-C // 128) * 128
    tr = _tile(R, max(16, (1024 * 1024) // lanes // 16 * 16), 16)

    def body(w_ref, o_ref):
        o_ref[...] = w_ref[...].astype(bf16)

    return pl.pallas_call(
        body, name=name, grid=(R // tr,), in_specs=[BS((None, tr, C), lambda r: (layer, r, 0))],
        out_specs=BS((tr, C), lambda r: (r, 0)), out_shape=S((R, C), bf16),
    )(w_all)


def _win_assemble(name, g_win, A, H):
    _, D, DP4 = g_win.shape
    NM = N_CHIPS * DP4 - H
    tr = _tile(D, 256, 16)

    def body(g_ref, m_ref, f_ref):
        full = jnp.concatenate([g_ref[j] for j in range(N_CHIPS)], axis=1)
        m_ref[...] = jnp.concatenate([full[:, :3 * A], full[:, 3 * A + H:]], axis=1)
        f_ref[...] = jnp.concatenate([full[:, 3 * A:3 * A + H], jnp.zeros((tr, HEAD - H), bf16)], axis=1)

    return pl.pallas_call(
        body, name=name, grid=(D // tr,), in_specs=[BS((N_CHIPS, tr, DP4), lambda i: (0, i, 0))],
        out_specs=[BS((tr, NM), lambda i: (i, 0)), BS((tr, HEAD), lambda i: (i, 0))],
        out_shape=[S((D, NM), bf16), S((D, HEAD), bf16)],
    )(g_win)


def _dwin_split(name, d_wmain, d_wf, A, H):
    D, NM = d_wmain.shape
    DP4 = (NM + H) // N_CHIPS
    tr = _tile(D, 256, 8)

    def body(m_ref, f_ref, o_ref):
        m = m_ref[...]
        full = jnp.concatenate([m[:, :3 * A], f_ref[:, :H], m[:, 3 * A:]], axis=1)
        for j in range(N_CHIPS):
            o_ref[j] = full[:, j * DP4:(j + 1) * DP4]

    return pl.pallas_call(
        body, name=name, grid=(D // tr,),
        in_specs=[BS((tr, NM), lambda i: (i, 0)), BS((tr, HEAD), lambda i: (i, 0))],
        out_specs=BS((N_CHIPS, tr, DP4), lambda i: (0, i, 0)), out_shape=S((N_CHIPS, D, DP4), f32),
        compiler_params=_params(),
    )(d_wmain, d_wf)


def _gelu_and_grad(x):
    k0, k1 = 0.7978845608028654, 0.044715
    th = jnp.tanh(k0 * (x + k1 * x * x * x))
    val = 0.5 * x * (1.0 + th)
    grad = 0.5 * (1.0 + th) + 0.5 * x * (1.0 - th * th) * (k0 * (1.0 + 3.0 * k1 * x * x))
    return val, grad


def _fgate_fwd(name, pf, fb):
    T = pf.shape[0]

    def body(pf_ref, fb_ref, c_ref, ct_ref):
        xv = jax.nn.log_sigmoid(pf_ref[...] + fb_ref[...])
        row = lax.broadcasted_iota(jnp.int32, xv.shape, 0)
        s = 1
        while s < T:
            xv = xv + jnp.where(row >= s, pltpu.roll(xv, s, 0), 0.0)
            s *= 2
        c_ref[...] = xv
        ct_ref[...] = xv.T

    return pl.pallas_call(body, name=name, out_shape=[S((T, HEAD), f32), S((HEAD, T), f32)])(pf, fb)


def _fgate_bwd(name, dct, pf, fb):
    T = pf.shape[0]

    def body(dct_ref, pf_ref, fb_ref, dpf_ref, dfb_ref):
        xv = dct_ref[...].T
        row = lax.broadcasted_iota(jnp.int32, xv.shape, 0)
        s = 1
        while s < T:
            xv = xv + jnp.where(row + s < T, pltpu.roll(xv, T - s, 0), 0.0)
            s *= 2
        df = xv * jax.nn.sigmoid(-(pf_ref[...] + fb_ref[...]))
        dpf_ref[...] = df.astype(bf16)
        dfb_ref[...] = jnp.sum(df, axis=0, keepdims=True)

    return pl.pallas_call(body, name=name, out_shape=[S((T, HEAD), bf16), S((1, HEAD), f32)])(dct, pf, fb)


def _qk_norm(name, P, qg, kg, A):
    T = P.shape[0]
    tr = _tile(T, 256, 16)
    n_heads = A // HEAD

    def body(q_ref, k_ref, v_ref, qg_ref, kg_ref, qn_ref, kn_ref, vb_ref):
        for h in range(n_heads):
            sl = slice(h * HEAD, (h + 1) * HEAD)
            for src, g_ref, dst in ((q_ref, qg_ref, qn_ref), (k_ref, kg_ref, kn_ref)):
                xv = src[:, sl]
                r = lax.rsqrt(jnp.mean(xv * xv, axis=-1, keepdims=True) + EPS)
                dst[:, sl] = (xv * r * g_ref[...]).astype(bf16)
        vb_ref[...] = v_ref[...].astype(bf16)

    vec = BS((1, HEAD), lambda i: (0, 0))
    out = BS((tr, A), lambda i: (i, 0))
    return pl.pallas_call(
        body, name=name, grid=(T // tr,),
        in_specs=[BS((tr, A), lambda i: (i, 0)), BS((tr, A), lambda i: (i, 1)), BS((tr, A), lambda i: (i, 2)), vec, vec],
        out_specs=[out, out, out], out_shape=[S((T, A), bf16)] * 3,
    )(P, P, P, qg, kg)


def _attn_fwd(name, qn, kn, vb, c_col, c_row, tb, mix_width):
    T, A = qn.shape
    H = A // HEAD
    nb = T // tb
    scale = HEAD ** -0.5
    hp = _heads_per_program(H, 4)
    wide = hp * HEAD

    def body(q_ref, k_ref, v_ref, cq_ref, ck_ref, o_ref, o32_ref, lse_ref):
        i = pl.program_id(1)
        below = lax.broadcasted_iota(jnp.int32, (tb, tb), 0) >= lax.broadcasted_iota(jnp.int32, (tb, tb), 1)

        def block(j, carry, diagonal):
            koff = pl.multiple_of(j * tb, tb)
            out = []
            for hh in range(hp):
                m, l, acc = carry[hh]
                sl = slice(hh * HEAD, (hh + 1) * HEAD)
                k = k_ref[pl.ds(koff, tb), sl]
                v = v_ref[pl.ds(koff, tb), sl]
                s = _dot(q_ref[:, sl], k, "nt") * scale + (cq_ref[hh] - ck_ref[hh, j])
                if diagonal:
                    s = jnp.where(below, s, NEG)
                m_new = jnp.maximum(m, jnp.max(s, axis=-1, keepdims=True))
                alpha = jnp.exp(m - m_new)
                p = jnp.exp(s - m_new)
                l = l * alpha + jnp.sum(p, axis=-1, keepdims=True)
                acc = acc * alpha + _dot(p, v, "nn")
                out.append((m_new, l, acc))
            return tuple(out)

        init = tuple((jnp.full((tb, 1), NEG, f32), jnp.zeros((tb, 1), f32), jnp.zeros((tb, HEAD), f32))
                     for _ in range(hp))
        carry = lax.fori_loop(0, i, lambda j, c: block(j, c, False), init)
        carry = block(i, carry, True)
        for hh in range(hp):
            m, l, acc = carry[hh]
            sl = slice(hh * HEAD, (hh + 1) * HEAD)
            o = acc / l
            o_ref[:, sl] = o.astype(bf16)
            o32_ref[:, sl] = o
            lse_ref[hh] = m + jnp.log(l)

    return pl.pallas_call(
        body, name=name, grid=(H // hp, nb),
        in_specs=[BS((tb, wide), lambda h, i: (i, h)), BS((T, wide), lambda h, i: (0, h)),
                  BS((T, wide), lambda h, i: (0, h)), BS((hp, tb, 1), lambda h, i: (h, i, 0)),
                  BS((hp, nb, 1, tb), lambda h, i: (h, 0, 0, 0))],
        out_specs=[BS((tb, wide), lambda h, i: (i, h)), BS((tb, wide), lambda h, i: (i, h)),
                   BS((hp, tb, 1), lambda h, i: (h, i, 0))],
        out_shape=[S((T, mix_width), bf16), S((T, A), f32), S((H, T, 1), f32)],
    )(qn, kn, vb, c_col, c_row)


def _attn_bwd(name, qn, kn, vb, o, dmix, lse, c_col, c_row, P, qg, kg, tb):
    T, A = qn.shape
    H = A // HEAD
    nb = T // tb
    scale = HEAD ** -0.5
    hp = _heads_per_program(H)
    wide = hp * HEAD

    def body(q_ref, k_ref, v_ref, o_ref, do_ref, lse_ref, cq_ref, ck_ref, qraw_ref, kraw_ref, qg_ref, kg_ref,
             dq_out, dk_out, dv_out, dc_out, dqg_out, dkg_out, dq_acc, dk_acc, delta_s):
        h = pl.program_id(0)
        dq_acc[...] = jnp.zeros_like(dq_acc)
        below = lax.broadcasted_iota(jnp.int32, (tb, tb), 0) >= lax.broadcasted_iota(jnp.int32, (tb, tb), 1)
        for hh in range(hp):
            sl = slice(hh * HEAD, (hh + 1) * HEAD)
            delta_s[hh] = jnp.sum(do_ref[:, sl].astype(bf16).astype(f32) * o_ref[:, sl], axis=-1, keepdims=True)

        def kblock(j, _):
            koff = pl.multiple_of(j * tb, tb)

            def products(i, hh):
                sl = slice(hh * HEAD, (hh + 1) * HEAD)
                qoff = pl.multiple_of(i * tb, tb)
                return (_dot(q_ref[pl.ds(qoff, tb), sl], k_ref[pl.ds(koff, tb), sl], "nt"),
                        _dot(do_ref[pl.ds(qoff, tb), sl], v_ref[pl.ds(koff, tb), sl], "nt"))

            def qblock(i, carry, diagonal):
                qoff = pl.multiple_of(i * tb, tb)
                out = []
                for hh in range(hp):
                    dk, dv, dc, qk_i, dp = carry[hh]
                    sl = slice(hh * HEAD, (hh + 1) * HEAD)
                    ahead = products(jnp.minimum(i + 1, nb - 1), hh)
                    k = k_ref[pl.ds(koff, tb), sl]
                    q = q_ref[pl.ds(qoff, tb), sl]
                    do = do_ref[pl.ds(qoff, tb), sl].astype(bf16)
                    s = qk_i * scale + (cq_ref[hh, pl.ds(qoff, tb), :] - ck_ref[hh, j])
                    if diagonal:
                        s = jnp.where(below, s, NEG)
                    p = jnp.exp(s - lse_ref[hh, pl.ds(qoff, tb), :])
                    dv = dv + _dot(p, do, "tn")
                    ds = p * (dp - delta_s[hh, pl.ds(qoff, tb), :])
                    dc = dc - jnp.sum(ds, axis=0, keepdims=True)
                    dsb = (ds * scale).astype(bf16)
                    dk = dk + _dot(dsb, q, "tn")
                    dq_acc[pl.ds(qoff, tb), sl] += _dot(dsb, k, "nn")
                    out.append((dk, dv, dc, *ahead))
                return tuple(out)

            init = tuple((jnp.zeros((tb, HEAD), f32), jnp.zeros((tb, HEAD), f32), jnp.zeros((1, tb), f32),
                          *products(j, hh)) for hh in range(hp))
            carry = qblock(j, init, True)
            carry = lax.fori_loop(j + 1, nb, lambda i, c: qblock(i, c, False), carry)
            for hh in range(hp):
                dk, dv, dc = carry[hh][:3]
                sl = slice(hh * HEAD, (hh + 1) * HEAD)
                dk_acc[pl.ds(koff, tb), sl] = dk
                dv_out[pl.ds(koff, tb), sl] = dv.astype(bf16)
                dc_out[hh, j] = dc
            return 0

        lax.fori_loop(0, nb, kblock, 0)

        for raw_ref, g_ref, acc_ref, d_out, dg_out in ((qraw_ref, qg_ref, dq_acc, dq_out, dqg_out),
                                                       (kraw_ref, kg_ref, dk_acc, dk_out, dkg_out)):
            part = jnp.zeros((1, HEAD), f32)
            for hh in range(hp):
                sl = slice(hh * HEAD, (hh + 1) * HEAD)
                xv = raw_ref[:, sl]
                r = lax.rsqrt(jnp.mean(xv * xv, axis=-1, keepdims=True) + EPS)
                xh = xv * r
                dn = acc_ref[:, sl]
                dxh = dn * g_ref[...]
                d_out[:, sl] = (r * (dxh - xh * jnp.mean(dxh * xh, axis=-1, keepdims=True))).astype(bf16)
                part = part + jnp.sum(dn * xh, axis=0, keepdims=True)

            @pl.when(h == 0)
            def _():
                dg_out[...] = part

            @pl.when(h > 0)
            def _():
                dg_out[...] += part

    heads = lambda off: BS((T, wide), lambda h: (0, off + h))
    col = BS((hp, T, 1), lambda h: (h, 0, 0))
    row = BS((hp, nb, 1, tb), lambda h: (h, 0, 0, 0))
    vec = BS((1, HEAD), lambda h: (0, 0))
    return pl.pallas_call(
        body, name=name, grid=(H // hp,),
        in_specs=[heads(0), heads(0), heads(0), heads(0), heads(0), col, col, row, heads(0), heads(H // hp), vec, vec],
        out_specs=[heads(0), heads(0), heads(0), row, vec, vec],
        out_shape=[S((T, A), bf16)] * 3 + [S((H, nb, 1, tb), f32), S((1, HEAD), f32), S((1, HEAD), f32)],
        scratch_shapes=[pltpu.VMEM((T, wide), f32), pltpu.VMEM((T, wide), f32), pltpu.VMEM((hp, T, 1), f32)],
        compiler_params=_params(),
    )(qn, kn, vb, o, dmix, lse, c_col, c_row, P, P, qg, kg)


def _gmlp_fwd(name, P, mix, gain, ws, b, col_u, col_v, col_y, Wd):
    T = P.shape[0]
    G = Wd // HEAD
    tr = _tile(T, 512, HEAD)

    def body(u_ref, v_ref, gain_ref, ws_ref, b_ref, mix_ref, y_ref):
        tril = lax.broadcasted_iota(jnp.int32, (HEAD, HEAD), 0) >= lax.broadcasted_iota(jnp.int32, (HEAD, HEAD), 1)
        wm = jnp.where(tril, ws_ref[...], 0.0).astype(bf16)
        for n in range(tr // HEAD):
            rows = slice(n * HEAD, (n + 1) * HEAD)
            u = jax.nn.gelu(u_ref[rows, :])
            a = jax.nn.gelu(v_ref[rows, :])
            r = lax.rsqrt(jnp.mean(a * a, axis=-1, keepdims=True) + EPS)
            vn = a * r * gain_ref[...]
            mixed = _dot(wm, vn, "nn") + b_ref[...]
            y_ref[rows, :] = (u * mixed).astype(bf16)

    return pl.pallas_call(
        body, name=name, grid=(G, T // tr),
        in_specs=[BS((tr, HEAD), lambda g, i: (i, col_u + g)), BS((tr, HEAD), lambda g, i: (i, col_v + g)),
                  BS((None, 1, HEAD), lambda g, i: (g, 0, 0)), BS((None, HEAD, HEAD), lambda g, i: (g, 0, 0)),
                  BS((None, HEAD, 1), lambda g, i: (g, 0, 0)), ANY],
        out_specs=BS((tr, HEAD), lambda g, i: (i, col_y + g)), out_shape=S(mix.shape, bf16),
        input_output_aliases={5: 0},
    )(P, P, gain, ws, b, mix)


def _gmlp_bwd(name, P, dmix, gain, ws, b, col_u, col_v, col_dy, Wd):
    T = P.shape[0]
    G = Wd // HEAD
    tr = _tile(T, 512, HEAD)

    def body(u_ref, v_ref, dy_ref, gain_ref, ws_ref, b_ref, du_ref, dv_ref, dws_ref, db_ref, dgain_ref):
        i = pl.program_id(1)
        tril = lax.broadcasted_iota(jnp.int32, (HEAD, HEAD), 0) >= lax.broadcasted_iota(jnp.int32, (HEAD, HEAD), 1)
        wm = jnp.where(tril, ws_ref[...], 0.0).astype(bf16)
        gain_v = gain_ref[...]
        dw = jnp.zeros((HEAD, HEAD), f32)
        db = jnp.zeros((HEAD, 1), f32)
        dgain = jnp.zeros((1, HEAD), f32)
        for n in range(tr // HEAD):
            rows = slice(n * HEAD, (n + 1) * HEAD)
            u, du_dx = _gelu_and_grad(u_ref[rows, :])
            a, da_dx = _gelu_and_grad(v_ref[rows, :])
            dy = dy_ref[rows, :]
            r = lax.rsqrt(jnp.mean(a * a, axis=-1, keepdims=True) + EPS)
            ah = a * r
            vnb = (ah * gain_v).astype(bf16)
            mixed = _dot(wm, vnb, "nn") + b_ref[...]
            dm = dy * u
            dmb = dm.astype(bf16)
            du_ref[rows, :] = (dy * mixed * du_dx).astype(bf16)
            db = db + jnp.sum(dm, axis=1, keepdims=True)
            dw = dw + _dot(dmb, vnb, "nt")
            dvn = _dot(wm, dmb, "tn")
            dgain = dgain + jnp.sum(dvn * ah, axis=0, keepdims=True)
            dah = dvn * gain_v
            da = r * (dah - ah * jnp.mean(dah * ah, axis=-1, keepdims=True))
            dv_ref[rows, :] = (da * da_dx).astype(bf16)
        dw = jnp.where(tril, dw, 0.0)

        @pl.when(i == 0)
        def _():
            dws_ref[...] = dw
            db_ref[...] = db
            dgain_ref[...] = dgain

        @pl.when(i > 0)
        def _():
            dws_ref[...] += dw
            db_ref[...] += db
            dgain_ref[...] += dgain

    out = BS((tr, HEAD), lambda g, i: (i, g))
    return pl.pallas_call(
        body, name=name, grid=(G, T // tr),
        in_specs=[BS((tr, HEAD), lambda g, i: (i, col_u + g)), BS((tr, HEAD), lambda g, i: (i, col_v + g)),
                  BS((tr, HEAD), lambda g, i: (i, col_dy + g)),
                  BS((None, 1, HEAD), lambda g, i: (g, 0, 0)), BS((None, HEAD, HEAD), lambda g, i: (g, 0, 0)),
                  BS((None, HEAD, 1), lambda g, i: (g, 0, 0))],
        out_specs=[out, out, BS((None, HEAD, HEAD), lambda g, i: (g, 0, 0)), BS((None, HEAD, 1), lambda g, i: (g, 0, 0)),
                   BS((None, 1, HEAD), lambda g, i: (g, 0, 0))],
        out_shape=[S((T, Wd), bf16), S((T, Wd), bf16), S((G, HEAD, HEAD), f32), S((G, HEAD, 1), f32),
                   S((G, 1, HEAD), f32)],
    )(P, P, dmix, gain, ws, b)


def _trailing_window(xv, w, row):
    k = 1
    while k < w:
        xv = xv + jnp.where(row >= k, pltpu.roll(xv, k, 0), 0.0)
        k *= 2
    return xv


def _leading_window(xv, w, row, T):
    k = 1
    while k < w:
        xv = xv + jnp.where(row + k < T, pltpu.roll(xv, T - k, 0), 0.0)
        k *= 2
    return xv


def _pool_fwd(name, P, mix, pw, ps, col_x, col_y, Wd):
    T = P.shape[0]
    Gp = Wd // HEAD

    def body(x_ref, pw_ref, ps_ref, mix_ref, y_ref):
        row = lax.broadcasted_iota(jnp.int32, (T, HEAD), 0)
        for g in range(Gp):
            w = POOL_WINDOWS[g]
            sl = slice(g * HEAD, (g + 1) * HEAD)
            xv = x_ref[:, sl]
            cnt = jnp.minimum(row + 1, w).astype(f32)
            d = _trailing_window(xv, w, row) / cnt - xv
            y_ref[:, sl] = (_dot(d, pw_ref[g], "nn") * ps_ref[:, sl]).astype(bf16)

    return pl.pallas_call(
        body, name=name, grid=(1,),
        in_specs=[BS((T, Wd), lambda i: (0, col_x)), BS((Gp, HEAD, HEAD), lambda i: (0, 0, 0)), BS((1, Wd), lambda i: (0, 0)),
                  ANY],
        out_specs=BS((T, Wd), lambda i: (0, col_y)), out_shape=S(mix.shape, bf16), input_output_aliases={3: 0},
        compiler_params=_params(),
    )(P, pw, ps, mix)


def _pool_bwd(name, P, dmix, pw, ps, col_x, col_dy, Wd):
    T = P.shape[0]
    Gp = Wd // HEAD

    def body(x_ref, dy_ref, pw_ref, ps_ref, dx_ref, dpw_ref, dps_ref):
        row = lax.broadcasted_iota(jnp.int32, (T, HEAD), 0)
        for g in range(Gp):
            w = POOL_WINDOWS[g]
            sl = slice(g * HEAD, (g + 1) * HEAD)
            xv = x_ref[:, sl]
            cnt = jnp.minimum(row + 1, w).astype(f32)
            d = (_trailing_window(xv, w, row) / cnt - xv).astype(bf16)
            pwb = pw_ref[g].astype(bf16)
            z = _dot(d, pwb, "nn")
            dy = dy_ref[:, sl]
            dps_ref[:, sl] = jnp.sum(dy * z, axis=0, keepdims=True)
            dzb = (dy * ps_ref[:, sl]).astype(bf16)
            dpw_ref[g] = _dot(d, dzb, "tn")
            dd = _dot(dzb, pwb, "nt")
            dx_ref[:, sl] = (_leading_window(dd / cnt, w, row, T) - dd).astype(bf16)

    return pl.pallas_call(
        body, name=name, grid=(1,),
        in_specs=[BS((T, Wd), lambda i: (0, col_x)), BS((T, Wd), lambda i: (0, col_dy)),
                  BS((Gp, HEAD, HEAD), lambda i: (0, 0, 0)), BS((1, Wd), lambda i: (0, 0))],
        out_specs=[BS((T, Wd), lambda i: (0, 0)), BS((Gp, HEAD, HEAD), lambda i: (0, 0, 0)), BS((1, Wd), lambda i: (0, 0))],
        out_shape=[S((T, Wd), bf16), S((Gp, HEAD, HEAD), f32), S((1, Wd), f32)], compiler_params=_params(),
    )(P, dmix, pw, ps)


def _adamw(name, w, g, m, v):
    R, C = w.shape
    lanes = -(-C // 128) * 128
    tr = _tile(R, max(8, (512 * 1024) // lanes // 8 * 8), 8)
    c1 = 1.0 - ADAM_B1 ** ADAM_STEP
    c2 = 1.0 - ADAM_B2 ** ADAM_STEP

    def body(w_ref, g_ref, m_ref, v_ref, d_ref, nm_ref, nv_ref):
        gv = g_ref[...]
        nm = ADAM_B1 * m_ref[...] + (1.0 - ADAM_B1) * gv
        nv = ADAM_B2 * v_ref[...] + (1.0 - ADAM_B2) * (gv * gv)
        d_ref[...] = -ADAM_LR * ((nm / c1) / (jnp.sqrt(nv / c2) + ADAM_EPS) + ADAM_WD * w_ref[...])
        nm_ref[...] = nm
        nv_ref[...] = nv

    blk = BS((tr, C), lambda i: (i, 0))
    return pl.pallas_call(
        body, name=name, grid=(R // tr,), in_specs=[blk] * 4, out_specs=[blk] * 3, out_shape=[S((R, C), f32)] * 3,
    )(w, g, m, v)


def _adamw_layer(name, layer, w_all, m_all, v_all, g, prev):
    L, R, C = w_all.shape
    lanes = -(-C // 128) * 128
    tr = _tile(R, max(8, (512 * 1024) // lanes // 8 * 8), 8)
    c1 = 1.0 - ADAM_B1 ** ADAM_STEP
    c2 = 1.0 - ADAM_B2 ** ADAM_STEP
    n_prev = 0 if prev is None else 4

    def body(w_ref, m_ref, v_ref, g_ref, *rest):
        go_ref, d_ref, nm_ref, nv_ref = rest[n_prev:]
        gv = g_ref[...]
        nm = ADAM_B1 * m_ref[...] + (1.0 - ADAM_B1) * gv
        nv = ADAM_B2 * v_ref[...] + (1.0 - ADAM_B2) * (gv * gv)
        d_ref[...] = -ADAM_LR * ((nm / c1) / (jnp.sqrt(nv / c2) + ADAM_EPS) + ADAM_WD * w_ref[...])
        nm_ref[...] = nm
        nv_ref[...] = nv
        go_ref[...] = gv

    slab = BS((None, tr, C), lambda r: (layer, r, 0))
    return pl.pallas_call(
        body, name=name, grid=(R // tr,),
        in_specs=[slab, slab, slab, BS((tr, C), lambda r: (r, 0))] + [ANY] * n_prev,
        out_specs=[slab] * 4, out_shape=[S((L, R, C), f32)] * 4,
        input_output_aliases={4 + k: k for k in range(n_prev)},
    )(w_all, m_all, v_all, g, *(prev or ()))


def _chip_of(k):
    return k // 2, k % 2


def _remote(src, dst, send_sems, recv_sems, idx, dev):
    return pltpu.make_async_remote_copy(src_ref=src, dst_ref=dst, send_sem=send_sems.at[idx], recv_sem=recv_sems.at[idx],
                                        device_id=dev, device_id_type=MESH)


def _plan_gather_near(n):
    def plan(refs, ss, rs, base):
        ins, lands = refs[:n], refs[n:]
        x, y, c, j0 = _my_place()
        sib = (x, y, 1 - c)
        sends, recvs = [], []
        for a in range(n):
            half = ins[a].shape[0] // 2
            lo = c * half
            sends.append(_remote(ins[a], lands[a].at[j0], ss, rs, base + 3 * a + 2, sib))
            recvs.append(_remote(lands[a].at[j0], lands[a].at[j0], ss, rs, base + 3 * a + 2, sib))
            for r in (1, 2):
                k = j0 ^ r
                dev = (*_chip_of(k), c)
                sends.append(_remote(ins[a].at[pl.ds(lo, half)], lands[a].at[j0, pl.ds(lo, half)], ss, rs,
                                     base + 3 * a + r - 1, dev))
                landed = lands[a].at[k, pl.ds(lo, half)]
                recvs.append(_remote(landed, landed, ss, rs, base + 3 * a + r - 1, dev))
        return sends, recvs
    return plan, 3 * n


def _plan_gather_relay(n):
    def plan(refs, ss, rs, base):
        x, y, c, j0 = _my_place()
        sib = (x, y, 1 - c)
        sends, recvs = [], []
        for a in range(n):
            half = refs[a].shape[1] // 2
            quarter = half // 2
            lo = c * half
            far = j0 ^ 3
            for r, to, off in ((1, 2, 0), (2, 1, quarter)):
                dev = (*_chip_of(j0 ^ to), c)
                piece = refs[a].at[j0 ^ r, pl.ds(lo + off, quarter)]
                sends.append(_remote(piece, piece, ss, rs, base + 4 * a + to - 1, dev))
                lands_here = refs[a].at[far, pl.ds(lo + off, quarter)]
                recvs.append(_remote(lands_here, lands_here, ss, rs, base + 4 * a + to - 1, dev))
                mine = refs[a].at[j0 ^ r, pl.ds(lo, half)]
                theirs = refs[a].at[j0 ^ r, pl.ds((1 - c) * half, half)]
                sends.append(_remote(mine, mine, ss, rs, base + 4 * a + 1 + r, sib))
                recvs.append(_remote(theirs, theirs, ss, rs, base + 4 * a + 1 + r, sib))
        return sends, recvs
    return plan, 4 * n


def _plan_gather_far(n):
    def plan(refs, ss, rs, base):
        x, y, c, j0 = _my_place()
        sib = (x, y, 1 - c)
        sends, recvs = [], []
        for a in range(n):
            half = refs[a].shape[1] // 2
            mine = refs[a].at[j0 ^ 3, pl.ds(c * half, half)]
            theirs = refs[a].at[j0 ^ 3, pl.ds((1 - c) * half, half)]
            sends.append(_remote(mine, mine, ss, rs, base + a, sib))
            recvs.append(_remote(theirs, theirs, ss, rs, base + a, sib))
        return sends, recvs
    return plan, n


def _plan_sibling_halves(n):
    def plan(refs, ss, rs, base):
        ins, lands = refs[:n], refs[n:]
        x, y, c, _ = _my_place()
        sib = (x, y, 1 - c)
        sends, recvs = [], []
        for a in range(n):
            half = ins[a].shape[1] // 2
            sends.append(_remote(ins[a].at[:, pl.ds((1 - c) * half, half), :], lands[a], ss, rs, base + a, sib))
            recvs.append(_remote(lands[a], lands[a], ss, rs, base + a, sib))
        return sends, recvs
    return plan, n


def _plan_chip_scatter(n):
    def plan(refs, ss, rs, base):
        ins, lands = refs[:n], refs[n:]
        x, y, c, j0 = _my_place()
        sends, recvs = [], []
        for a in range(n):
            for r in (1, 2, 3):
                k = j0 ^ r
                dev = (*_chip_of(k), c)
                sends.append(_remote(ins[a].at[k], lands[a].at[j0], ss, rs, base + 3 * a + r - 1, dev))
                recvs.append(_remote(lands[a].at[k], lands[a].at[k], ss, rs, base + 3 * a + r - 1, dev))
        return sends, recvs
    return plan, 3 * n


def _plan_sibling_join(n):
    def plan(refs, ss, rs, base):
        x, y, c, _ = _my_place()
        sib = (x, y, 1 - c)
        sends, recvs = [], []
        for a in range(n):
            half = refs[a].shape[0] // 2
            mine = refs[a].at[pl.ds(c * half, half)]
            theirs = refs[a].at[pl.ds((1 - c) * half, half)]
            sends.append(_remote(mine, mine, ss, rs, base + a, sib))
            recvs.append(_remote(theirs, theirs, ss, rs, base + a, sib))
        return sends, recvs
    return plan, n


_HBM = pl.BlockSpec(memory_space=pltpu.HBM)
_SEM = pl.BlockSpec(memory_space=pltpu.SEMAPHORE)
_EFFECT = pltpu.SideEffectType.DATAFLOW_SIDE_EFFECTING


def _exchange_start(name, plan, bufs, after):
    plan_fn, n_sems = plan
    n = len(bufs)

    def body(*refs):
        ss, rs, token = refs[n + len(after)], refs[n + len(after) + 1], refs[-1]
        sends, _ = plan_fn(refs[:n], ss, rs, 0)
        for cp in sends:
            cp.start()
        token[...] = jnp.zeros_like(token)

    res = pl.pallas_call(
        body, name=name,
        out_shape=(pltpu.SemaphoreType.DMA((n_sems,)), pltpu.SemaphoreType.DMA((n_sems,)),
                   *[pltpu.HBM(b.shape, b.dtype) for b in bufs], S((8, 128), f32)),
        in_specs=[_HBM] * n + [ANY] * len(after),
        out_specs=(_SEM, _SEM, *[_HBM] * n, pl.BlockSpec(memory_space=pltpu.VMEM)),
        input_output_aliases={k: 2 + k for k in range(n)},
        compiler_params=pltpu.CompilerParams(has_side_effects=_EFFECT),
    )(*[pltpu.with_memory_space_constraint(b, pltpu.HBM) for b in bufs], *after)
    return res[0], res[1], list(res[2:2 + n]), res[-1]


def _exchange_wait(name, plan, send_sems, recv_sems, bufs, after):
    plan_fn, _ = plan
    n = len(bufs)

    def body(*refs):
        ss, rs, token = refs[n], refs[n + 1], refs[-1]
        sends, recvs = plan_fn(refs[:n], ss, rs, 0)
        for cp in recvs:
            cp.wait_recv()
        for cp in sends:
            cp.wait_send()
        token[...] = jnp.zeros_like(token)

    res = pl.pallas_call(
        body, name=name,
        out_shape=(*[pltpu.HBM(b.shape, b.dtype) for b in bufs], S((8, 128), f32)),
        in_specs=[_HBM] * n + [_SEM, _SEM] + [ANY] * len(after),
        out_specs=(*[_HBM] * n, pl.BlockSpec(memory_space=pltpu.VMEM)),
        input_output_aliases={k: k for k in range(n)},
        compiler_params=pltpu.CompilerParams(has_side_effects=_EFFECT),
    )(*bufs, send_sems, recv_sems, *after)
    return list(res[:n]), res[-1]


class _Order:
    def __init__(self, first):
        self.marker = first
        self.token = None

    def _after(self):
        return [self.marker] + ([] if self.token is None else [self.token])

    def start(self, name, plan, bufs):
        ss, rs, thru, self.token = _exchange_start(name, plan, bufs, self._after())
        return name, plan, ss, rs, thru

    def wait(self, handle):
        name, plan, ss, rs, thru = handle
        out, self.token = _exchange_wait(name + "_wait", plan, ss, rs, thru, self._after())
        return out

    def follows(self, small):
        return small if self.token is None else small + self.token[0, 0]

    def done(self, result):
        self.marker = result[(slice(0, 1),) * result.ndim].reshape(1, 1)


def _all_reduce_small(name, g8):
    _, R, L = g8.shape

    def body(g_ref, out_ref, land, red, send1, recv1, send2, recv2):
        x, y, c, _ = _my_place()
        me = 4 * x + 2 * y + c
        peers = []
        for r in range(1, N_DEV):
            q = me ^ r
            peers.append((q, (q // 4, (q // 2) % 2, q % 2)))
        first = []
        for r, (q, dev) in enumerate(peers):
            cp = pltpu.make_async_remote_copy(src_ref=g_ref.at[q], dst_ref=land.at[me], send_sem=send1.at[r],
                                              recv_sem=recv1.at[r], device_id=dev, device_id_type=MESH)
            cp.start()
            first.append(cp)
        land[me] = g_ref[me]
        for r, (q, dev) in enumerate(peers):
            pltpu.make_async_remote_copy(src_ref=land.at[q], dst_ref=land.at[q], send_sem=send1.at[r],
                                         recv_sem=recv1.at[r], device_id=dev, device_id_type=MESH).wait_recv()
        acc = land[0]
        for d in range(1, N_DEV):
            acc = acc + land[d]
        red[...] = acc
        out_ref[me] = acc
        second = []
        for r, (q, dev) in enumerate(peers):
            cp = pltpu.make_async_remote_copy(src_ref=red, dst_ref=out_ref.at[me], send_sem=send2.at[r],
                                              recv_sem=recv2.at[r], device_id=dev, device_id_type=MESH)
            cp.start()
            second.append(cp)
        for r, (q, dev) in enumerate(peers):
            pltpu.make_async_remote_copy(src_ref=out_ref.at[q], dst_ref=out_ref.at[q], send_sem=send2.at[r],
                                         recv_sem=recv2.at[r], device_id=dev, device_id_type=MESH).wait_recv()
        for cp in first + second:
            cp.wait_send()

    vm = pl.BlockSpec(memory_space=pltpu.VMEM)
    return pl.pallas_call(
        body, name=name, in_specs=[vm], out_specs=vm, out_shape=S(g8.shape, f32),
        scratch_shapes=[pltpu.VMEM((N_DEV, R, L), f32), pltpu.VMEM((R, L), f32)]
        + [pltpu.SemaphoreType.DMA((N_DEV - 1,))] * 4,
        compiler_params=_params(),
    )(g8)


def _pair_sum(name, g4, sib):
    _, rows, cols = g4.shape
    half = rows // 2
    lanes = -(-cols // 128) * 128
    tr = _tile(half, max(16, (512 * 1024) // lanes // 16 * 16), 16)
    nb = half // tr

    def body(g_ref, s_ref, pb_ref, own_ref):
        j = pl.program_id(1)
        t = g_ref[...] + s_ref[...]
        pb_ref[...] = t.astype(bf16)

        @pl.when(j == _my_place()[3])
        def _():
            own_ref[...] = t

    return pl.pallas_call(
        body, name=name, grid=(nb, N_CHIPS),
        in_specs=[BS((None, tr, cols), lambda i, j: (j, lax.axis_index("c") * nb + i, 0)),
                  BS((None, tr, cols), lambda i, j: (j, i, 0))],
        out_specs=[BS((None, tr, cols), lambda i, j: (j, i, 0)), BS((tr, cols), lambda i, j: (i, 0))],
        out_shape=[S((N_CHIPS, half, cols), bf16), S((half, cols), f32)],
    )(g4, sib)


def _chip_sum(name, own, got):
    half, cols = own.shape
    lanes = -(-cols // 128) * 128
    tr = _tile(half, max(16, (512 * 1024) // lanes // 16 * 16), 16)
    nb = half // tr

    def body(own_ref, *rest):
        got_refs, o_ref = rest[:N_CHIPS], rest[N_CHIPS]
        j0 = _my_place()[3]
        acc = None
        for k in range(N_CHIPS):
            t = jnp.where(j0 == k, own_ref[...], got_refs[k][...].astype(f32))
            acc = t if acc is None else acc + t
        o_ref[...] = acc

    def slot(k):
        return BS((None, tr, cols), lambda i: (jnp.where(_my_place()[3] == k, (k + 1) % N_CHIPS, k), i, 0))

    return pl.pallas_call(
        body, name=name, grid=(nb,),
        in_specs=[BS((tr, cols), lambda i: (i, 0))] + [slot(k) for k in range(N_CHIPS)],
        out_specs=BS((tr, cols), lambda i: (lax.axis_index("c") * nb + i, 0)),
        out_shape=S((2 * half, cols), f32),
    )(own, got, got, got, got)


def _rows_of(size):
    return -(-size // 1024) * 8


def _pack_rows(arrs, n_rows):
    parts = []
    for a in arrs:
        rows = _rows_of(a.size)
        if a.size % 128 == 0:
            part = a.astype(f32).reshape(-1, 128)
            part = jnp.pad(part, ((0, rows - part.shape[0]), (0, 0)))
        else:
            part = jnp.pad(a.reshape(-1).astype(f32), (0, rows * 128 - a.size)).reshape(rows, 128)
        parts.append(part)
    used = sum(p.shape[0] for p in parts)
    return jnp.concatenate(parts + [jnp.zeros((n_rows - used, 128), f32)], axis=0)


def _unpack_rows(packed, shapes):
    out, row = [], 0
    for shp in shapes:
        size = 1
        for d in shp:
            size *= d
        rows = packed[row:row + _rows_of(size)]
        out.append(rows[:size // 128].reshape(shp) if size % 128 == 0 else rows.reshape(-1)[:size].reshape(shp))
        row += _rows_of(size)
    return out


def kernel(x, p, norm_mix, w_in, q_norm, k_norm, forget_bias, gmlp_v_norm, gmlp_w_s, gmlp_b_s, pool_w, pool_scale, w_out, norm_ffn, w_ffn_gate, w_ffn_up, w_ffn_down, norm_ple, w_ple_gate, w_ple_proj, loss_target, m_norm_mix, m_w_in, m_q_norm, m_k_norm, m_forget_bias, m_gmlp_v_norm, m_gmlp_w_s, m_gmlp_b_s, m_pool_w, m_pool_scale, m_w_out, m_norm_ffn, m_w_ffn_gate, m_w_ffn_up, m_w_ffn_down, m_norm_ple, m_w_ple_gate, m_w_ple_proj, v_norm_mix, v_w_in, v_q_norm, v_k_norm, v_forget_bias, v_gmlp_v_norm, v_gmlp_w_s, v_gmlp_b_s, v_pool_w, v_pool_scale, v_w_out, v_norm_ffn, v_w_ffn_gate, v_w_ffn_up, v_w_ffn_down, v_norm_ple, v_w_ple_gate, v_w_ple_proj):
    W = dict(norm_mix=norm_mix, w_in=w_in, q_norm=q_norm, k_norm=k_norm, forget_bias=forget_bias,
             gmlp_v_norm=gmlp_v_norm, gmlp_w_s=gmlp_w_s, gmlp_b_s=gmlp_b_s, pool_w=pool_w, pool_scale=pool_scale,
             w_out=w_out, norm_ffn=norm_ffn, w_ffn_gate=w_ffn_gate, w_ffn_up=w_ffn_up, w_ffn_down=w_ffn_down,
             norm_ple=norm_ple, w_ple_gate=w_ple_gate, w_ple_proj=w_ple_proj)
    M = dict(norm_mix=m_norm_mix, w_in=m_w_in, q_norm=m_q_norm, k_norm=m_k_norm, forget_bias=m_forget_bias,
             gmlp_v_norm=m_gmlp_v_norm, gmlp_w_s=m_gmlp_w_s, gmlp_b_s=m_gmlp_b_s, pool_w=m_pool_w,
             pool_scale=m_pool_scale, w_out=m_w_out, norm_ffn=m_norm_ffn, w_ffn_gate=m_w_ffn_gate,
             w_ffn_up=m_w_ffn_up, w_ffn_down=m_w_ffn_down, norm_ple=m_norm_ple, w_ple_gate=m_w_ple_gate,
             w_ple_proj=m_w_ple_proj)
    V = dict(norm_mix=v_norm_mix, w_in=v_w_in, q_norm=v_q_norm, k_norm=v_k_norm, forget_bias=v_forget_bias,
             gmlp_v_norm=v_gmlp_v_norm, gmlp_w_s=v_gmlp_w_s, gmlp_b_s=v_gmlp_b_s, pool_w=v_pool_w,
             pool_scale=v_pool_scale, w_out=v_w_out, norm_ffn=v_norm_ffn, w_ffn_gate=v_w_ffn_gate,
             w_ffn_up=v_w_ffn_up, w_ffn_down=v_w_ffn_down, norm_ple=v_norm_ple, w_ple_gate=v_w_ple_gate,
             w_ple_proj=v_w_ple_proj)

    L = w_in.shape[0]
    _, T, D = x.shape
    A, Wd = D // 2, D // 4
    H = A // HEAD
    G = gmlp_w_s.shape[1]
    Gp = pool_w.shape[1]
    DP4 = w_in.shape[2]
    DP = N_CHIPS * DP4
    NM = 3 * A + 3 * Wd
    FS = w_ffn_gate.shape[2]
    FF = N_CHIPS * FS
    DS = D // N_CHIPS
    PL = p.shape[-1]
    assert Wd // G == HEAD and Wd // Gp == HEAD and DP == NM + H and H <= HEAD
    assert all(w & (w - 1) == 0 for w in POOL_WINDOWS[:Gp])
    tb = _tile(T, 256, HEAD)
    nb = T // tb
    tm = _tile(T, 512, 16)
    tmw = _tile(T, 1024, 16)
    tn = _tile(NM, 512, 128)
    tnd = _tile(D, 512, 128)
    tkd = _tile(D, 1024, 128)
    tnw = _tile(D, 1024, 128)
    col_gu, col_gv, col_xp = 3 * A // HEAD, (3 * A + Wd) // HEAD, (3 * A + 2 * Wd) // Wd
    col_dg, col_dp = A // HEAD, (A + Wd) // Wd

    order = _Order(x[0, :1, :1])

    def gather_near(i, names, tag=""):
        shards = [_cast_layer(f"cast_{n}_{i}", W[n], i) for n in names]
        lands = [lax.empty((N_CHIPS,) + s.shape, bf16) for s in shards]
        return names, order.start(f"ag_near_{i}{tag}", _plan_gather_near(len(names)), shards + lands)

    def gather_relay(i, near, tag=""):
        names, handle = near
        return names, order.start(f"ag_relay_{i}{tag}", _plan_gather_relay(len(names)), order.wait(handle)[len(names):])

    def gather_far(i, relayed, tag=""):
        names, handle = relayed
        return names, order.start(f"ag_far_{i}{tag}", _plan_gather_far(len(names)), order.wait(handle))

    win_count = [0]

    def gathered(far):
        names, handle = far
        g = dict(zip(names, order.wait(handle)))
        out = {}
        if "w_in" in g:
            out["w_main"], out["w_f"] = _win_assemble(f"w_in_cols_{win_count[0]}", g["w_in"], A, H)
            win_count[0] += 1
        if "w_out" in g:
            out["w_out"] = g["w_out"].reshape(D, D)
        if "w_ffn_gate" in g:
            out.update(w_gate=g["w_ffn_gate"], w_up=g["w_ffn_up"], w_down=g["w_ffn_down"].reshape(FF, D),
                       w_pg=g["w_ple_gate"].reshape(D, D), w_pp=g["w_ple_proj"])
        return out

    Wf = [None] * L
    relayed = gather_relay(0, gather_near(0, BIG[:2], "a"), "a")
    near_rest = gather_near(0, BIG[2:], "c")
    Wf[0] = gathered(gather_far(0, relayed, "a"))
    near = relayed = None

    h = x.reshape(T, D)
    pb16 = p.reshape(L, T, PL).astype(bf16)
    saved = []

    for i in range(L):
        w = Wf[i]
        sv = dict(h0=h)
        xn1 = _rms_fwd(f"rms1_{i}", h, order.follows(norm_mix[i]))
        (P,) = _matmul(f"proj_{i}", "nn", (T // tmw, NM // tn),
                       [((xn1, BS((tmw, D), lambda i, j: (i, 0))), (w["w_main"], BS((D, tn), lambda i, j: (0, j))))], [],
                       [((T, NM), f32, BS((tmw, tn), lambda i, j: (i, j)))])
        (Pf,) = _matmul(f"projf_{i}", "nn", (T // tm, 1),
                        [((xn1, BS((tm, D), lambda i, j: (i, 0))), (w["w_f"], BS((D, HEAD), lambda i, j: (0, 0))))], [],
                        [((T, HEAD), f32, BS((tm, HEAD), lambda i, j: (i, 0)))])
        fb = jnp.pad(forget_bias[i], (0, HEAD - H)).reshape(1, HEAD)
        cc, ct = _fgate_fwd(f"fgate_{i}", Pf, fb)
        c_col = ct[:H].reshape(H, T, 1)
        c_row = ct[:H].reshape(H, nb, 1, tb)
        qg, kg = q_norm[i].reshape(1, HEAD), k_norm[i].reshape(1, HEAD)
        qn, kn, vb = _qk_norm(f"qknorm_{i}", P, qg, kg, A)
        mix, o32, lse = _attn_fwd(f"attn_{i}", qn, kn, vb, c_col, c_row, tb, D)
        if i == 0:
            order.done(o32)
            relayed_rest = gather_relay(0, near_rest, "c")
            near = gather_near(1, BIG) if L > 1 else None
        gain = gmlp_v_norm[i].reshape(G, 1, HEAD)
        bs = gmlp_b_s[i].reshape(G, HEAD, 1)
        mix = _gmlp_fwd(f"gmlp_{i}", P, mix, order.follows(gain), gmlp_w_s[i], bs, col_gu, col_gv, col_dg, Wd)
        ps = pool_scale[i].reshape(1, Wd)
        mix = _pool_fwd(f"pool_{i}", P, mix, pool_w[i], ps, col_xp, col_dp, Wd)
        (h1,) = _matmul(f"out_{i}", "nn", (T // tmw, D // tnd),
                        [((mix, BS((tmw, D), lambda i, j: (i, 0))), (w["w_out"], BS((D, tnd), lambda i, j: (0, j))))],
                        [(h, BS((tmw, tnd), lambda i, j: (i, j)))],
                        [((T, D), f32, BS((tmw, tnd), lambda i, j: (i, j)))],
                        epilogue=lambda accs, ex: (accs[0] + ex[0],))
        xn2 = _rms_fwd(f"rms2_{i}", h1, norm_ffn[i])
        order.done(xn2)
        if i == 0:
            w.update(gathered(gather_far(0, relayed_rest, "c")))
        elif i + 1 < L:
            relayed = gather_relay(i + 1, near)
            near = gather_near(i + 2, BIG) if i + 2 < L else None

        def ffn_epi(accs, ex):
            g_, u_ = accs
            return g_, u_, g_ * jax.nn.sigmoid(g_) * u_

        ffo = BS((tm, FS), lambda j, i: (i, j))
        Gt, Ut, act = _matmul(f"ffn1_{i}", "nn", (N_CHIPS, T // tm),
                              [((xn2, BS((tm, D), lambda j, i: (i, 0))), (w["w_gate"], BS((None, D, FS), lambda j, i: (j, 0, 0)))),
                               ((xn2, BS((tm, D), lambda j, i: (i, 0))), (w["w_up"], BS((None, D, FS), lambda j, i: (j, 0, 0))))],
                              [], [((T, FF), bf16, ffo)] * 3, epilogue=ffn_epi, after=order.token)
        (h2,) = _matmul(f"ffn2_{i}", "nn", (T // tmw, D // tnd),
                        [((act, BS((tmw, FF), lambda i, j: (i, 0))), (w["w_down"], BS((FF, tnd), lambda i, j: (0, j))))],
                        [(h1, BS((tmw, tnd), lambda i, j: (i, j)))],
                        [((T, D), f32, BS((tmw, tnd), lambda i, j: (i, j)))],
                        epilogue=lambda accs, ex: (accs[0] + ex[0],))
        if i == 0 and L > 1:
            order.done(h2)
            relayed = gather_relay(1, near)
            near = gather_near(2, BIG) if L > 2 else None
        xn3 = _rms_fwd(f"rms3_{i}", h2, order.follows(norm_ple[i]))

        def ple_epi(accs, ex):
            gate = jax.nn.sigmoid(accs[0])
            return ex[0] + accs[1] * gate, gate, accs[1]

        dso = BS((tmw, DS), lambda i, j: (i, j))
        h3, gate, e = _matmul(f"ple_{i}", "nn", (T // tmw, N_CHIPS),
                              [((xn3, BS((tmw, D), lambda i, j: (i, 0))), (w["w_pg"], BS((D, DS), lambda i, j: (0, j)))),
                               ((pb16[i], BS((tmw, PL), lambda i, j: (i, 0))), (w["w_pp"], BS((None, PL, DS), lambda i, j: (j, 0, 0))))],
                              [(h2, dso)], [((T, D), f32, dso), ((T, D), bf16, dso), ((T, D), bf16, dso)], epilogue=ple_epi)
        sv.update(xn1=xn1, P=P, Pf=Pf, fb=fb, c_col=c_col, c_row=c_row, qn=qn, kn=kn, vb=vb, o32=o32, lse=lse,
                  mix=mix, h1=h1, xn2=xn2, Gt=Gt, Ut=Ut, act=act, h2=h2, xn3=xn3, gate=gate, e=e)
        saved.append(sv)
        h = h3
        order.done(h3)
        if i + 1 < L:
            Wf[i + 1] = gathered(gather_far(i + 1, relayed))

    dh, loss_tile = _loss_grad("loss", h, loss_target.reshape(T, D))

    small_g = {n: [None] * L for n in SMALL}
    big_out = {}

    def stage_a(u):
        n_u = len(u["names"])
        lands = [lax.empty((N_CHIPS, g.shape[1] // 2, g.shape[2]), f32) for g in u["grads"]]
        u["a"] = order.start(f"rs_a_{u['tag']}", _plan_sibling_halves(n_u), u["grads"] + lands)

    def stage_pair(u):
        n_u = len(u["names"])
        out = order.wait(u["a"])
        pairs = [_pair_sum(f"rs_pair_{u['tag']}_{a}", out[a], out[n_u + a]) for a in range(n_u)]
        u["pb"], u["own"] = [t[0] for t in pairs], [t[1] for t in pairs]

    def stage_b(u):
        lands = [lax.empty(t.shape, bf16) for t in u["pb"]]
        u["b"] = order.start(f"rs_b_{u['tag']}", _plan_chip_scatter(len(u["names"])), u["pb"] + lands)

    def stage_sum(u):
        n_u = len(u["names"])
        out = order.wait(u["b"])
        u["sum"] = [_chip_sum(f"rs_sum_{u['tag']}_{a}", u["own"][a], out[n_u + a]) for a in range(n_u)]

    def stage_c(u):
        u["c"] = order.start(f"rs_c_{u['tag']}", _plan_sibling_join(len(u["names"])), u["sum"])

    def stage_adamw(u):
        for n, r in zip(u["names"], order.wait(u["c"])):
            big_out[n] = _adamw_layer(f"adamw_{n}_{u['layer']}", u["layer"], W[n], M[n], V[n],
                                      r.reshape(W[n].shape[1:]), big_out.get(n))
    dh1 = prev_f = prev_m = None
    for i in reversed(range(L)):
        w, sv = Wf[i], saved[i]
        if dh1 is not None:
            dh, _, dg = _rms_bwd(f"rms1_bw_{i + 1}", dxn1, saved[i + 1]["h0"], order.follows(norm_mix[i + 1]), dh1)
            small_g["norm_mix"][i + 1] = dg.reshape(D)
        de, dz = _ple_bwd_elem(f"ple_bw_{i}", dh, sv["gate"], sv["e"])
        (d_wpp,) = _matmul(f"d_wpp_{i}", "tn", (N_CHIPS, 1),
                           [((pb16[i], BS((T, PL), lambda i, j: (0, 0))), (de, BS((T, DS), lambda i, j: (0, i))))], [],
                           [((N_CHIPS, PL, DS), f32, BS((None, PL, DS), lambda i, j: (i, 0, 0)))])
        (d_wpg,) = _matmul(f"d_wpg_{i}", "tn", (D // tkd, D // tnd),
                           [((sv["xn3"], BS((T, tkd), lambda i, j: (0, i))), (dz, BS((T, tnd), lambda i, j: (0, j))))], [],
                           [((D, D), f32, BS((tkd, tnd), lambda i, j: (i, j)))])
        order.done(dz)
        if prev_m is not None:
            stage_pair(prev_m)
        (dxn3,) = _matmul(f"d_xn3_{i}", "nt", (T // tmw, D // tnd),
                          [((dz, BS((tmw, D), lambda i, j: (i, 0))), (w["w_pg"], BS((tnd, D), lambda i, j: (j, 0))))], [],
                          [((T, D), f32, BS((tmw, tnd), lambda i, j: (i, j)))], after=order.token)
        dh2, dh2b, dg = _rms_bwd(f"rms3_bw_{i}", dxn3, sv["h2"], norm_ple[i], dh)
        small_g["norm_ple"][i] = dg.reshape(D)

        def dffn_epi(accs, ex):
            da = accs[0]
            g_, u_ = ex[0].astype(f32), ex[1].astype(f32)
            sg = jax.nn.sigmoid(g_)
            return da * u_ * (sg * (1.0 + g_ * (1.0 - sg))), da * (g_ * sg)

        ffo = BS((tm, FS), lambda j, i: (i, j))
        dG, dU = _matmul(f"d_act_{i}", "nt", (N_CHIPS, T // tm),
                         [((dh2b, BS((tm, D), lambda j, i: (i, 0))), (w["w_down"], BS((FS, D), lambda j, i: (j, 0))))],
                         [(sv["Gt"], ffo), (sv["Ut"], ffo)], [((T, FF), bf16, ffo)] * 2, epilogue=dffn_epi)
        (d_wd,) = _matmul(f"d_wd_{i}", "tn", (N_CHIPS, D // tnw),
                          [((sv["act"], BS((T, FS), lambda i, j: (0, i))), (dh2b, BS((T, tnw), lambda i, j: (0, j))))], [],
                          [((FF, D), f32, BS((FS, tnw), lambda i, j: (i, j)))])
        gu_out = BS((None, tnd, FS), lambda j, i: (j, i, 0))
        d_wg, d_wu = _matmul(f"d_wgu_{i}", "tn", (N_CHIPS, D // tnd),
                             [((sv["xn2"], BS((T, tnd), lambda j, i: (0, i))), (dG, BS((T, FS), lambda j, i: (0, j)))),
                              ((sv["xn2"], BS((T, tnd), lambda j, i: (0, i))), (dU, BS((T, FS), lambda j, i: (0, j))))], [],
                             [((N_CHIPS, D, FS), f32, gu_out)] * 2, epilogue=lambda accs, ex: (accs[0], accs[1]))
        order.done(dG)
        unit_f = dict(tag=f"{i}f", layer=i, names=["w_ffn_gate", "w_ffn_up", "w_ffn_down", "w_ple_gate", "w_ple_proj"],
                      grads=[d_wg, d_wu, d_wd.reshape(N_CHIPS, FS, D), d_wpg.reshape(N_CHIPS, DS, D), d_wpp])
        stage_a(unit_f)
        if prev_f is not None:
            stage_sum(prev_f)
            stage_c(prev_f)
        if prev_m is not None:
            stage_b(prev_m)
        tm2 = _tile(T, 256, 16)
        (dxn2,) = _matmul(f"d_xn2_{i}", "nt", (D // tnd, T // tm2),
                          [((dG, BS((tm2, FF), lambda j, i: (i, 0))), (w["w_gate"], BS((N_CHIPS, tnd, FS), lambda j, i: (0, j, 0)))),
                           ((dU, BS((tm2, FF), lambda j, i: (i, 0))), (w["w_up"], BS((N_CHIPS, tnd, FS), lambda j, i: (0, j, 0))))], [],
                          [((T, D), f32, BS((tm2, tnd), lambda j, i: (i, j)))], after=order.token)
        dh1, dh1b, dg = _rms_bwd(f"rms2_bw_{i}", dxn2, sv["h1"], norm_ffn[i], dh2)
        small_g["norm_ffn"][i] = dg.reshape(D)
        (dmix,) = _matmul(f"d_mix_{i}", "nt", (T // tmw, D // tnd),
                          [((dh1b, BS((tmw, D), lambda i, j: (i, 0))), (w["w_out"], BS((tnd, D), lambda i, j: (j, 0))))], [],
                          [((T, D), f32, BS((tmw, tnd), lambda i, j: (i, j)))])
        (d_wout,) = _matmul(f"d_wout_{i}", "tn", (D // tkd, D // tnd),
                            [((sv["mix"], BS((T, tkd), lambda i, j: (0, i))), (dh1b, BS((T, tnd), lambda i, j: (0, j))))], [],
                            [((D, D), f32, BS((tkd, tnd), lambda i, j: (i, j)))])
        order.done(dmix)
        stage_pair(unit_f)
        stage_b(unit_f)
        if prev_f is not None:
            stage_adamw(prev_f)
        qg, kg = q_norm[i].reshape(1, HEAD), k_norm[i].reshape(1, HEAD)
        dq, dk, dv, dc_row, dqg, dkg = _attn_bwd(f"attn_bw_{i}", sv["qn"], sv["kn"], sv["vb"], sv["o32"], dmix,
                                                 sv["lse"], sv["c_col"], sv["c_row"], sv["P"], order.follows(qg), kg, tb)
        small_g["q_norm"][i] = dqg.reshape(HEAD)
        small_g["k_norm"][i] = dkg.reshape(HEAD)
        dct = jnp.pad(dc_row.reshape(H, T), ((0, HEAD - H), (0, 0)))
        dPf, dfb = _fgate_bwd(f"fgate_bw_{i}", dct, sv["Pf"], sv["fb"])
        small_g["forget_bias"][i] = dfb[0, :H]
        gain = gmlp_v_norm[i].reshape(G, 1, HEAD)
        bs = gmlp_b_s[i].reshape(G, HEAD, 1)
        dgu, dgv, dws, dbs, dgain = _gmlp_bwd(f"gmlp_bw_{i}", sv["P"], dmix, gain, gmlp_w_s[i], bs, col_gu, col_gv,
                                              col_dg, Wd)
        small_g["gmlp_w_s"][i] = dws
        small_g["gmlp_b_s"][i] = dbs.reshape(G, HEAD)
        small_g["gmlp_v_norm"][i] = dgain.reshape(G, HEAD)
        ps = pool_scale[i].reshape(1, Wd)
        dxp, dpw, dps = _pool_bwd(f"pool_bw_{i}", sv["P"], dmix, pool_w[i], ps, col_xp, col_dp, Wd)
        small_g["pool_w"][i] = dpw
        small_g["pool_scale"][i] = dps.reshape(Wd)
        dP = jnp.concatenate([dq, dk, dv, dgu, dgv, dxp], axis=1)
        (d_wmain,) = _matmul(f"d_wmain_{i}", "tn", (D // tkd, NM // tn),
                             [((sv["xn1"], BS((T, tkd), lambda i, j: (0, i))), (dP, BS((T, tn), lambda i, j: (0, j))))], [],
                             [((D, NM), f32, BS((tkd, tn), lambda i, j: (i, j)))])
        (d_wf,) = _matmul(f"d_wf_{i}", "tn", (D // tnd, 1),
                          [((sv["xn1"], BS((T, tnd), lambda i, j: (0, i))), (dPf, BS((T, HEAD), lambda i, j: (0, 0))))], [],
                          [((D, HEAD), f32, BS((tnd, HEAD), lambda i, j: (i, 0)))])
        d_win4 = _dwin_split(f"d_win_cols_{i}", d_wmain, d_wf, A, H)
        order.done(dP)
        unit_m = dict(tag=f"{i}m", layer=i, names=["w_in", "w_out"], grads=[d_win4, d_wout.reshape(N_CHIPS, DS, D)])
        stage_a(unit_m)
        if prev_m is not None:
            stage_sum(prev_m)
            stage_c(prev_m)
        (dxn1,) = _matmul(f"d_xn1_{i}", "nt", (D // tnd, T // tm),
                          [((dP, BS((tm, NM), lambda j, i: (i, 0))), (w["w_main"], BS((tnd, NM), lambda j, i: (j, 0)))),
                           ((dPf, BS((tm, HEAD), lambda j, i: (i, 0))), (w["w_f"], BS((tnd, HEAD), lambda j, i: (j, 0))))], [],
                          [((T, D), f32, BS((tm, tnd), lambda j, i: (i, j)))], after=order.token)
        order.done(dxn1)
        if prev_m is not None:
            stage_adamw(prev_m)
        prev_f, prev_m = unit_f, unit_m

    dh, _, dg = _rms_bwd("rms1_bw_0", dxn1, saved[0]["h0"], order.follows(norm_mix[0]), dh1)
    small_g["norm_mix"][0] = dg.reshape(D)
    order.done(dh)
    stage_pair(prev_m)
    stage_b(prev_m)
    stage_sum(prev_f)
    stage_c(prev_f)

    small_full = {n: jnp.stack(small_g[n]) for n in SMALL}
    small_shapes = [W[n].shape for n in SMALL]
    n_rows = sum(_rows_of(W[n].size) for n in SMALL) + _rows_of(1)
    rows8 = -(-n_rows // 64) * 8
    packed = order.follows(_pack_rows([small_full[n] for n in SMALL] + [loss_tile[0, :1]], N_DEV * rows8))
    summed = _all_reduce_small("allreduce_small", packed.reshape(N_DEV, rows8, 128)).reshape(-1, 128)
    order.done(summed)
    stage_adamw(prev_f)
    stage_sum(prev_m)
    stage_c(prev_m)
    stage_adamw(prev_m)
    *small_grads, loss_row = _unpack_rows(summed, small_shapes + [(1,)])
    grads = dict(zip(SMALL, small_grads))
    loss = loss_row[0]

    wp, mp, vp = (_pack_rows([t[n] for n in SMALL], N_DEV * rows8) for t in (W, M, V))
    delta, new_m, new_v = (dict(zip(SMALL, _unpack_rows(t, small_shapes)))
                           for t in _adamw("adamw_small", wp, summed, mp, vp))
    for n in BIG:
        grads[n], delta[n], new_m[n], new_v[n] = big_out[n]

    return (loss, dh.reshape(1, T, D), *[grads[n] for n in WEIGHTS], *[delta[n] for n in WEIGHTS],
            *[new_m[n] for n in WEIGHTS], *[new_v[n] for n in WEIGHTS])
```

```python
import jax
import jax.numpy as jnp
from jax import lax
from jax.experimental import pallas as pl
from jax.experimental.pallas import tpu as pltpu

f32, bf16 = jnp.float32, jnp.bfloat16
S = jax.ShapeDtypeStruct
BS = pl.BlockSpec
ANY = pl.BlockSpec(memory_space=pl.ANY)
MESH = pl.DeviceIdType.MESH

EPS = 1e-6
HEAD = 128
POOL_WINDOWS = (2, 4, 8, 16)
NEG = -1e30
N_CHIPS = 4
N_DEV = 8
VMEM_LIMIT = 56 * 1024 * 1024

ADAM_LR, ADAM_B1, ADAM_B2, ADAM_EPS, ADAM_WD, ADAM_STEP = 0.001, 0.9, 0.999, 1e-08, 0.01, 10

BIG = ("w_in", "w_out", "w_ffn_gate", "w_ffn_up", "w_ffn_down", "w_ple_gate", "w_ple_proj")
SMALL = ("norm_mix", "q_norm", "k_norm", "forget_bias", "gmlp_v_norm", "gmlp_w_s", "gmlp_b_s", "pool_w",
         "pool_scale", "norm_ffn", "norm_ple")
WEIGHTS = ("norm_mix", "w_in", "q_norm", "k_norm", "forget_bias", "gmlp_v_norm", "gmlp_w_s", "gmlp_b_s", "pool_w",
           "pool_scale", "w_out", "norm_ffn", "w_ffn_gate", "w_ffn_up", "w_ffn_down", "norm_ple", "w_ple_gate",
           "w_ple_proj")


def _tile(n, target, mult):
    best = None
    for t in range(mult, min(n, target) + 1, mult):
        if n % t == 0:
            best = t
    return best if best is not None else n


def _params(**kw):
    return pltpu.CompilerParams(vmem_limit_bytes=VMEM_LIMIT, **kw)


def _dot(a, b, kind):
    dims = {"nn": (((1,), (0,)), ((), ())), "nt": (((1,), (1,)), ((), ())), "tn": (((0,), (0,)), ((), ()))}[kind]
    return lax.dot_general(a.astype(bf16), b.astype(bf16), dims, preferred_element_type=f32)


def _heads_per_program(n_heads, want=2):
    while n_heads % want:
        want //= 2
    return want


def _my_place():
    x, y, c = lax.axis_index("x"), lax.axis_index("y"), lax.axis_index("c")
    return x, y, c, 2 * x + y


def _matmul(name, kind, grid, pairs, extras, outs, epilogue=None, after=None):
    n_p, n_e = len(pairs), len(extras)
    tokens = [] if after is None else [(after, BS((8, 128), lambda *_: (0, 0)))]

    def body(*refs):
        a_refs, b_refs = refs[:n_p], refs[n_p:2 * n_p]
        e_refs = refs[2 * n_p:2 * n_p + n_e]
        o_refs = refs[2 * n_p + n_e + len(tokens):]
        accs = []
        for a_ref, b_ref in zip(a_refs, b_refs):
            if len(b_ref.shape) == 3:
                w = b_ref.shape[2]
                acc = None
                for s in range(b_ref.shape[0]):
                    d = _dot(a_ref[:, s * w:(s + 1) * w], b_ref[s], kind)
                    acc = d if acc is None else acc + d
            else:
                acc = _dot(a_ref[...], b_ref[...], kind)
            accs.append(acc)
        if epilogue is None:
            res = accs[0]
            for t in accs[1:]:
                res = res + t
            res = (res,)
        else:
            res = epilogue(accs, [e[...] for e in e_refs])
        for o_ref, o in zip(o_refs, res):
            o_ref[...] = o.astype(o_ref.dtype)

    in_arrays = [p[0][0] for p in pairs] + [p[1][0] for p in pairs] + [e[0] for e in extras + tokens]
    in_specs = [p[0][1] for p in pairs] + [p[1][1] for p in pairs] + [e[1] for e in extras + tokens]
    res = pl.pallas_call(
        body, name=name, grid=grid, in_specs=in_specs,
        out_specs=[o[2] for o in outs], out_shape=[S(o[0], o[1]) for o in outs],
        compiler_params=_params(),
    )(*in_arrays)
    return res


def _rms_fwd(name, x, g):
    T, D = x.shape
    tr = _tile(T, 256, 8)

    def body(x_ref, g_ref, o_ref):
        xv = x_ref[...]
        r = lax.rsqrt(jnp.mean(xv * xv, axis=-1, keepdims=True) + EPS)
        o_ref[...] = (xv * r * g_ref[...]).astype(o_ref.dtype)

    return pl.pallas_call(
        body, name=name, grid=(T // tr,),
        in_specs=[BS((tr, D), lambda i: (i, 0)), BS((1, D), lambda i: (0, 0))],
        out_specs=BS((tr, D), lambda i: (i, 0)), out_shape=S((T, D), bf16),
    )(x, g.reshape(1, D))


def _rms_bwd(name, dxn, x, g, dres):
    T, D = x.shape
    tr = _tile(T, 256, 8)

    def body(dxn_ref, x_ref, g_ref, dres_ref, dx_ref, dxb_ref, dg_ref):
        i = pl.program_id(0)
        xv = x_ref[...]
        r = lax.rsqrt(jnp.mean(xv * xv, axis=-1, keepdims=True) + EPS)
        xh = xv * r
        dxn_v = dxn_ref[...]
        dxh = dxn_v * g_ref[...]
        dx = dres_ref[...] + r * (dxh - xh * jnp.mean(dxh * xh, axis=-1, keepdims=True))
        dx_ref[...] = dx
        dxb_ref[...] = dx.astype(bf16)
        part = jnp.sum(dxn_v * xh, axis=0, keepdims=True)

        @pl.when(i == 0)
        def _():
            dg_ref[...] = part

        @pl.when(i > 0)
        def _():
            dg_ref[...] += part

    row = BS((tr, D), lambda i: (i, 0))
    vec = BS((1, D), lambda i: (0, 0))
    return pl.pallas_call(
        body, name=name, grid=(T // tr,),
        in_specs=[row, row, vec, row], out_specs=[row, row, vec],
        out_shape=[S((T, D), f32), S((T, D), bf16), S((1, D), f32)],
    )(dxn, x, g.reshape(1, D), dres)


def _loss_grad(name, y, tgt):
    T, D = y.shape
    tr = _tile(T, 256, 8)

    def body(y_ref, t_ref, dy_ref, l_ref):
        i = pl.program_id(0)
        e = y_ref[...] - t_ref[...]
        dy_ref[...] = e * (1.0 / D)
        part = 0.5 * jnp.sum(jnp.mean(e * e, axis=-1, keepdims=True), axis=0, keepdims=True)

        @pl.when(i == 0)
        def _():
            l_ref[...] = jnp.zeros_like(l_ref)

        l_ref[...] += jnp.broadcast_to(part, l_ref.shape)

    row = BS((tr, D), lambda i: (i, 0))
    return pl.pallas_call(
        body, name=name, grid=(T // tr,), in_specs=[row, row],
        out_specs=[row, BS((8, 128), lambda i: (0, 0))],
        out_shape=[S((T, D), f32), S((8, 128), f32)],
    )(y, tgt)


def _ple_bwd_elem(name, dh, gate, e):
    T, D = dh.shape
    tr = _tile(T, 256, 16)

    def body(dh_ref, g_ref, e_ref, de_ref, dz_ref):
        d = dh_ref[...]
        g = g_ref[...].astype(f32)
        de_ref[...] = (d * g).astype(bf16)
        dz_ref[...] = (d * e_ref[...].astype(f32) * g * (1.0 - g)).astype(bf16)

    row = BS((tr, D), lambda i: (i, 0))
    return pl.pallas_call(
        body, name=name, grid=(T // tr,), in_specs=[row, row, row], out_specs=[row, row],
        out_shape=[S((T, D), bf16), S((T, D), bf16)],
    )(dh, gate, e)


def _cast_layer(name, w_all, layer):
    _, R, C = w_all.shape
    lanes = -(-C // 128) * 128
    tr = _tile(R, max(16, (1024 * 1024) // lanes // 16 * 16), 16)

    def body(w_ref, o_ref):
        o_ref[...] = w_ref[...].astype(bf16)

    return pl.pallas_call(
        body, name=name, grid=(R // tr,), in_specs=[BS((None, tr, C), lambda r: (layer, r, 0))],
        out_specs=BS((tr, C), lambda r: (r, 0)), out_shape=S((R, C), bf16),
    )(w_all)


def _win_assemble(name, g_win, A, H):
    _, D, DP4 = g_win.shape
    NM = N_CHIPS * DP4 - H
    tr = _tile(D, 256, 16)

    def body(g_ref, m_ref, f_ref):
        full = jnp.concatenate([g_ref[j] for j in range(N_CHIPS)], axis=1)
        m_ref[...] = jnp.concatenate([full[:, :3 * A], full[:, 3 * A + H:]], axis=1)
        f_ref[...] = jnp.concatenate([full[:, 3 * A:3 * A + H], jnp.zeros((tr, HEAD - H), bf16)], axis=1)

    return pl.pallas_call(
        body, name=name, grid=(D // tr,), in_specs=[BS((N_CHIPS, tr, DP4), lambda i: (0, i, 0))],
        out_specs=[BS((tr, NM), lambda i: (i, 0)), BS((tr, HEAD), lambda i: (i, 0))],
        out_shape=[S((D, NM), bf16), S((D, HEAD), bf16)],
    )(g_win)


def _dwin_split(name, d_wmain, d_wf, A, H):
    D, NM = d_wmain.shape
    DP4 = (NM + H) // N_CHIPS
    tr = _tile(D, 256, 8)

    def body(m_ref, f_ref, o_ref):
        m = m_ref[...]
        full = jnp.concatenate([m[:, :3 * A], f_ref[:, :H], m[:, 3 * A:]], axis=1)
        for j in range(N_CHIPS):
            o_ref[j] = full[:, j * DP4:(j + 1) * DP4]

    return pl.pallas_call(
        body, name=name, grid=(D // tr,),
        in_specs=[BS((tr, NM), lambda i: (i, 0)), BS((tr, HEAD), lambda i: (i, 0))],
        out_specs=BS((N_CHIPS, tr, DP4), lambda i: (0, i, 0)), out_shape=S((N_CHIPS, D, DP4), f32),
        compiler_params=_params(),
    )(d_wmain, d_wf)


def _gelu_and_grad(x):
    k0, k1 = 0.7978845608028654, 0.044715
    th = jnp.tanh(k0 * (x + k1 * x * x * x))
    val = 0.5 * x * (1.0 + th)
    grad = 0.5 * (1.0 + th) + 0.5 * x * (1.0 - th * th) * (k0 * (1.0 + 3.0 * k1 * x * x))
    return val, grad


def _fgate_fwd(name, pf, fb):
    T = pf.shape[0]

    def body(pf_ref, fb_ref, c_ref, ct_ref):
        xv = jax.nn.log_sigmoid(pf_ref[...] + fb_ref[...])
        row = lax.broadcasted_iota(jnp.int32, xv.shape, 0)
        s = 1
        while s < T:
            xv = xv + jnp.where(row >= s, pltpu.roll(xv, s, 0), 0.0)
            s *= 2
        c_ref[...] = xv
        ct_ref[...] = xv.T

    return pl.pallas_call(body, name=name, out_shape=[S((T, HEAD), f32), S((HEAD, T), f32)])(pf, fb)


def _fgate_bwd(name, dct, pf, fb):
    T = pf.shape[0]

    def body(dct_ref, pf_ref, fb_ref, dpf_ref, dfb_ref):
        xv = dct_ref[...].T
        row = lax.broadcasted_iota(jnp.int32, xv.shape, 0)
        s = 1
        while s < T:
            xv = xv + jnp.where(row + s < T, pltpu.roll(xv, T - s, 0), 0.0)
            s *= 2
        df = xv * jax.nn.sigmoid(-(pf_ref[...] + fb_ref[...]))
        dpf_ref[...] = df.astype(bf16)
        dfb_ref[...] = jnp.sum(df, axis=0, keepdims=True)

    return pl.pallas_call(body, name=name, out_shape=[S((T, HEAD), bf16), S((1, HEAD), f32)])(dct, pf, fb)


def _qk_norm(name, P, qg, kg, A):
    T = P.shape[0]
    tr = _tile(T, 256, 16)
    n_heads = A // HEAD

    def body(q_ref, k_ref, v_ref, qg_ref, kg_ref, qn_ref, kn_ref, vb_ref):
        for h in range(n_heads):
            sl = slice(h * HEAD, (h + 1) * HEAD)
            for src, g_ref, dst in ((q_ref, qg_ref, qn_ref), (k_ref, kg_ref, kn_ref)):
                xv = src[:, sl]
                r = lax.rsqrt(jnp.mean(xv * xv, axis=-1, keepdims=True) + EPS)
                dst[:, sl] = (xv * r * g_ref[...]).astype(bf16)
        vb_ref[...] = v_ref[...].astype(bf16)

    vec = BS((1, HEAD), lambda i: (0, 0))
    out = BS((tr, A), lambda i: (i, 0))
    return pl.pallas_call(
        body, name=name, grid=(T // tr,),
        in_specs=[BS((tr, A), lambda i: (i, 0)), BS((tr, A), lambda i: (i, 1)), BS((tr, A), lambda i: (i, 2)), vec, vec],
        out_specs=[out, out, out], out_shape=[S((T, A), bf16)] * 3,
    )(P, P, P, qg, kg)


def _attn_fwd(name, qn, kn, vb, c_col, c_row, tb, mix_width):
    T, A = qn.shape
    H = A // HEAD
    nb = T // tb
    scale = HEAD ** -0.5
    hp = _heads_per_program(H, 4)
    wide = hp * HEAD

    def body(q_ref, k_ref, v_ref, cq_ref, ck_ref, o_ref, o32_ref, lse_ref):
        i = pl.program_id(1)
        below = lax.broadcasted_iota(jnp.int32, (tb, tb), 0) >= lax.broadcasted_iota(jnp.int32, (tb, tb), 1)

        def block(j, carry, diagonal):
            koff = pl.multiple_of(j * tb, tb)
            out = []
            for hh in range(hp):
                m, l, acc = carry[hh]
                sl = slice(hh * HEAD, (hh + 1) * HEAD)
                k = k_ref[pl.ds(koff, tb), sl]
                v = v_ref[pl.ds(koff, tb), sl]
                s = _dot(q_ref[:, sl], k, "nt") * scale + (cq_ref[hh] - ck_ref[hh, j])
                if diagonal:
                    s = jnp.where(below, s, NEG)
                m_new = jnp.maximum(m, jnp.max(s, axis=-1, keepdims=True))
                alpha = jnp.exp(m - m_new)
                p = jnp.exp(s - m_new)
                l = l * alpha + jnp.sum(p, axis=-1, keepdims=True)
                acc = acc * alpha + _dot(p, v, "nn")
                out.append((m_new, l, acc))
            return tuple(out)

        init = tuple((jnp.full((tb, 1), NEG, f32), jnp.zeros((tb, 1), f32), jnp.zeros((tb, HEAD), f32))
                     for _ in range(hp))
        carry = lax.fori_loop(0, i, lambda j, c: block(j, c, False), init)
        carry = block(i, carry, True)
        for hh in range(hp):
            m, l, acc = carry[hh]
            sl = slice(hh * HEAD, (hh + 1) * HEAD)
            o = acc / l
            o_ref[:, sl] = o.astype(bf16)
            o32_ref[:, sl] = o
            lse_ref[hh] = m + jnp.log(l)

    return pl.pallas_call(
        body, name=name, grid=(H // hp, nb),
        in_specs=[BS((tb, wide), lambda h, i: (i, h)), BS((T, wide), lambda h, i: (0, h)),
                  BS((T, wide), lambda h, i: (0, h)), BS((hp, tb, 1), lambda h, i: (h, i, 0)),
                  BS((hp, nb, 1, tb), lambda h, i: (h, 0, 0, 0))],
        out_specs=[BS((tb, wide), lambda h, i: (i, h)), BS((tb, wide), lambda h, i: (i, h)),
                   BS((hp, tb, 1), lambda h, i: (h, i, 0))],
        out_shape=[S((T, mix_width), bf16), S((T, A), f32), S((H, T, 1), f32)],
    )(qn, kn, vb, c_col, c_row)


def _attn_bwd(name, qn, kn, vb, o, dmix, lse, c_col, c_row, P, qg, kg, tb):
    T, A = qn.shape
    H = A // HEAD
    nb = T // tb
    scale = HEAD ** -0.5
    hp = _heads_per_program(H)
    wide = hp * HEAD

    def body(q_ref, k_ref, v_ref, o_ref, do_ref, lse_ref, cq_ref, ck_ref, qraw_ref, kraw_ref, qg_ref, kg_ref,
             dq_out, dk_out, dv_out, dc_out, dqg_out, dkg_out, dq_acc, dk_acc, delta_s):
        h = pl.program_id(0)
        dq_acc[...] = jnp.zeros_like(dq_acc)
        below = lax.broadcasted_iota(jnp.int32, (tb, tb), 0) >= lax.broadcasted_iota(jnp.int32, (tb, tb), 1)
        for hh in range(hp):
            sl = slice(hh * HEAD, (hh + 1) * HEAD)
            delta_s[hh] = jnp.sum(do_ref[:, sl].astype(bf16).astype(f32) * o_ref[:, sl], axis=-1, keepdims=True)

        def kblock(j, _):
            koff = pl.multiple_of(j * tb, tb)

            def products(i, hh):
                sl = slice(hh * HEAD, (hh + 1) * HEAD)
                qoff = pl.multiple_of(i * tb, tb)
                return (_dot(q_ref[pl.ds(qoff, tb), sl], k_ref[pl.ds(koff, tb), sl], "nt"),
                        _dot(do_ref[pl.ds(qoff, tb), sl], v_ref[pl.ds(koff, tb), sl], "nt"))

            def qblock(i, carry, diagonal):
                qoff = pl.multiple_of(i * tb, tb)
                out = []
                for hh in range(hp):
                    dk, dv, dc, qk_i, dp = carry[hh]
                    sl = slice(hh * HEAD, (hh + 1) * HEAD)
                    ahead = products(jnp.minimum(i + 1, nb - 1), hh)
                    k = k_ref[pl.ds(koff, tb), sl]
                    q = q_ref[pl.ds(qoff, tb), sl]
                    do = do_ref[pl.ds(qoff, tb), sl].astype(bf16)
                    s = qk_i * scale + (cq_ref[hh, pl.ds(qoff, tb), :] - ck_ref[hh, j])
                    if diagonal:
                        s = jnp.where(below, s, NEG)
                    p = jnp.exp(s - lse_ref[hh, pl.ds(qoff, tb), :])
                    dv = dv + _dot(p, do, "tn")
                    ds = p * (dp - delta_s[hh, pl.ds(qoff, tb), :])
                    dc = dc - jnp.sum(ds, axis=0, keepdims=True)
                    dsb = (ds * scale).astype(bf16)
                    dk = dk + _dot(dsb, q, "tn")
                    dq_acc[pl.ds(qoff, tb), sl] += _dot(dsb, k, "nn")
                    out.append((dk, dv, dc, *ahead))
                return tuple(out)

            init = tuple((jnp.zeros((tb, HEAD), f32), jnp.zeros((tb, HEAD), f32), jnp.zeros((1, tb), f32),
                          *products(j, hh)) for hh in range(hp))
            carry = qblock(j, init, True)
            carry = lax.fori_loop(j + 1, nb, lambda i, c: qblock(i, c, False), carry)
            for hh in range(hp):
                dk, dv, dc = carry[hh][:3]
                sl = slice(hh * HEAD, (hh + 1) * HEAD)
                dk_acc[pl.ds(koff, tb), sl] = dk
                dv_out[pl.ds(koff, tb), sl] = dv.astype(bf16)
                dc_out[hh, j] = dc
            return 0

        lax.fori_loop(0, nb, kblock, 0)

        for raw_ref, g_ref, acc_ref, d_out, dg_out in ((qraw_ref, qg_ref, dq_acc, dq_out, dqg_out),
                                                       (kraw_ref, kg_ref, dk_acc, dk_out, dkg_out)):
            part = jnp.zeros((1, HEAD), f32)
            for hh in range(hp):
                sl = slice(hh * HEAD, (hh + 1) * HEAD)
                xv = raw_ref[:, sl]
                r = lax.rsqrt(jnp.mean(xv * xv, axis=-1, keepdims=True) + EPS)
                xh = xv * r
                dn = acc_ref[:, sl]
                dxh = dn * g_ref[...]
                d_out[:, sl] = (r * (dxh - xh * jnp.mean(dxh * xh, axis=-1, keepdims=True))).astype(bf16)
                part = part + jnp.sum(dn * xh, axis=0, keepdims=True)

            @pl.when(h == 0)
            def _():
                dg_out[...] = part

            @pl.when(h > 0)
            def _():
                dg_out[...] += part

    heads = lambda off: BS((T, wide), lambda h: (0, off + h))
    col = BS((hp, T, 1), lambda h: (h, 0, 0))
    row = BS((hp, nb, 1, tb), lambda h: (h, 0, 0, 0))
    vec = BS((1, HEAD), lambda h: (0, 0))
    return pl.pallas_call(
        body, name=name, grid=(H // hp,),
        in_specs=[heads(0), heads(0), heads(0), heads(0), heads(0), col, col, row, heads(0), heads(H // hp), vec, vec],
        out_specs=[heads(0), heads(0), heads(0), row, vec, vec],
        out_shape=[S((T, A), bf16)] * 3 + [S((H, nb, 1, tb), f32), S((1, HEAD), f32), S((1, HEAD), f32)],
        scratch_shapes=[pltpu.VMEM((T, wide), f32), pltpu.VMEM((T, wide), f32), pltpu.VMEM((hp, T, 1), f32)],
        compiler_params=_params(),
    )(qn, kn, vb, o, dmix, lse, c_col, c_row, P, P, qg, kg)


def _gmlp_fwd(name, P, mix, gain, ws, b, col_u, col_v, col_y, Wd):
    T = P.shape[0]
    G = Wd // HEAD
    tr = _tile(T, 512, HEAD)

    def body(u_ref, v_ref, gain_ref, ws_ref, b_ref, mix_ref, y_ref):
        tril = lax.broadcasted_iota(jnp.int32, (HEAD, HEAD), 0) >= lax.broadcasted_iota(jnp.int32, (HEAD, HEAD), 1)
        wm = jnp.where(tril, ws_ref[...], 0.0).astype(bf16)
        for n in range(tr // HEAD):
            rows = slice(n * HEAD, (n + 1) * HEAD)
            u = jax.nn.gelu(u_ref[rows, :])
            a = jax.nn.gelu(v_ref[rows, :])
            r = lax.rsqrt(jnp.mean(a * a, axis=-1, keepdims=True) + EPS)
            vn = a * r * gain_ref[...]
            mixed = _dot(wm, vn, "nn") + b_ref[...]
            y_ref[rows, :] = (u * mixed).astype(bf16)

    return pl.pallas_call(
        body, name=name, grid=(G, T // tr),
        in_specs=[BS((tr, HEAD), lambda g, i: (i, col_u + g)), BS((tr, HEAD), lambda g, i: (i, col_v + g)),
                  BS((None, 1, HEAD), lambda g, i: (g, 0, 0)), BS((None, HEAD, HEAD), lambda g, i: (g, 0, 0)),
                  BS((None, HEAD, 1), lambda g, i: (g, 0, 0)), ANY],
        out_specs=BS((tr, HEAD), lambda g, i: (i, col_y + g)), out_shape=S(mix.shape, bf16),
        input_output_aliases={5: 0},
    )(P, P, gain, ws, b, mix)


def _gmlp_bwd(name, P, dmix, gain, ws, b, col_u, col_v, col_dy, Wd):
    T = P.shape[0]
    G = Wd // HEAD
    tr = _tile(T, 512, HEAD)

    def body(u_ref, v_ref, dy_ref, gain_ref, ws_ref, b_ref, du_ref, dv_ref, dws_ref, db_ref, dgain_ref):
        i = pl.program_id(1)
        tril = lax.broadcasted_iota(jnp.int32, (HEAD, HEAD), 0) >= lax.broadcasted_iota(jnp.int32, (HEAD, HEAD), 1)
        wm = jnp.where(tril, ws_ref[...], 0.0).astype(bf16)
        gain_v = gain_ref[...]
        dw = jnp.zeros((HEAD, HEAD), f32)
        db = jnp.zeros((HEAD, 1), f32)
        dgain = jnp.zeros((1, HEAD), f32)
        for n in range(tr // HEAD):
            rows = slice(n * HEAD, (n + 1) * HEAD)
            u, du_dx = _gelu_and_grad(u_ref[rows, :])
            a, da_dx = _gelu_and_grad(v_ref[rows, :])
            dy = dy_ref[rows, :]
            r = lax.rsqrt(jnp.mean(a * a, axis=-1, keepdims=True) + EPS)
            ah = a * r
            vnb = (ah * gain_v).astype(bf16)
            mixed = _dot(wm, vnb, "nn") + b_ref[...]
            dm = dy * u
            dmb = dm.astype(bf16)
            du_ref[rows, :] = (dy * mixed * du_dx).astype(bf16)
            db = db + jnp.sum(dm, axis=1, keepdims=True)
            dw = dw + _dot(dmb, vnb, "nt")
            dvn = _dot(wm, dmb, "tn")
            dgain = dgain + jnp.sum(dvn * ah, axis=0, keepdims=True)
            dah = dvn * gain_v
            da = r * (dah - ah * jnp.mean(dah * ah, axis=-1, keepdims=True))
            dv_ref[rows, :] = (da * da_dx).astype(bf16)
        dw = jnp.where(tril, dw, 0.0)

        @pl.when(i == 0)
        def _():
            dws_ref[...] = dw
            db_ref[...] = db
            dgain_ref[...] = dgain

        @pl.when(i > 0)
        def _():
            dws_ref[...] += dw
            db_ref[...] += db
            dgain_ref[...] += dgain

    out = BS((tr, HEAD), lambda g, i: (i, g))
    return pl.pallas_call(
        body, name=name, grid=(G, T // tr),
        in_specs=[BS((tr, HEAD), lambda g, i: (i, col_u + g)), BS((tr, HEAD), lambda g, i: (i, col_v + g)),
                  BS((tr, HEAD), lambda g, i: (i, col_dy + g)),
                  BS((None, 1, HEAD), lambda g, i: (g, 0, 0)), BS((None, HEAD, HEAD), lambda g, i: (g, 0, 0)),
                  BS((None, HEAD, 1), lambda g, i: (g, 0, 0))],
        out_specs=[out, out, BS((None, HEAD, HEAD), lambda g, i: (g, 0, 0)), BS((None, HEAD, 1), lambda g, i: (g, 0, 0)),
                   BS((None, 1, HEAD), lambda g, i: (g, 0, 0))],
        out_shape=[S((T, Wd), bf16), S((T, Wd), bf16), S((G, HEAD, HEAD), f32), S((G, HEAD, 1), f32),
                   S((G, 1, HEAD), f32)],
    )(P, P, dmix, gain, ws, b)


def _trailing_window(xv, w, row):
    k = 1
    while k < w:
        xv = xv + jnp.where(row >= k, pltpu.roll(xv, k, 0), 0.0)
        k *= 2
    return xv


def _leading_window(xv, w, row, T):
    k = 1
    while k < w:
        xv = xv + jnp.where(row + k < T, pltpu.roll(xv, T - k, 0), 0.0)
        k *= 2
    return xv


def _pool_fwd(name, P, mix, pw, ps, col_x, col_y, Wd):
    T = P.shape[0]
    Gp = Wd // HEAD

    def body(x_ref, pw_ref, ps_ref, mix_ref, y_ref):
        row = lax.broadcasted_iota(jnp.int32, (T, HEAD), 0)
        for g in range(Gp):
            w = POOL_WINDOWS[g]
            sl = slice(g * HEAD, (g + 1) * HEAD)
            xv = x_ref[:, sl]
            cnt = jnp.minimum(row + 1, w).astype(f32)
            d = _trailing_window(xv, w, row) / cnt - xv
            y_ref[:, sl] = (_dot(d, pw_ref[g], "nn") * ps_ref[:, sl]).astype(bf16)

    return pl.pallas_call(
        body, name=name, grid=(1,),
        in_specs=[BS((T, Wd), lambda i: (0, col_x)), BS((Gp, HEAD, HEAD), lambda i: (0, 0, 0)), BS((1, Wd), lambda i: (0, 0)),
                  ANY],
        out_specs=BS((T, Wd), lambda i: (0, col_y)), out_shape=S(mix.shape, bf16), input_output_aliases={3: 0},
        compiler_params=_params(),
    )(P, pw, ps, mix)


def _pool_bwd(name, P, dmix, pw, ps, col_x, col_dy, Wd):
    T = P.shape[0]
    Gp = Wd // HEAD

    def body(x_ref, dy_ref, pw_ref, ps_ref, dx_ref, dpw_ref, dps_ref):
        row = lax.broadcasted_iota(jnp.int32, (T, HEAD), 0)
        for g in range(Gp):
            w = POOL_WINDOWS[g]
            sl = slice(g * HEAD, (g + 1) * HEAD)
            xv = x_ref[:, sl]
            cnt = jnp.minimum(row + 1, w).astype(f32)
            d = (_trailing_window(xv, w, row) / cnt - xv).astype(bf16)
            pwb = pw_ref[g].astype(bf16)
            z = _dot(d, pwb, "nn")
            dy = dy_ref[:, sl]
            dps_ref[:, sl] = jnp.sum(dy * z, axis=0, keepdims=True)
            dzb = (dy * ps_ref[:, sl]).astype(bf16)
            dpw_ref[g] = _dot(d, dzb, "tn")
            dd = _dot(dzb, pwb, "nt")
            dx_ref[:, sl] = (_leading_window(dd / cnt, w, row, T) - dd).astype(bf16)

    return pl.pallas_call(
        body, name=name, grid=(1,),
        in_specs=[BS((T, Wd), lambda i: (0, col_x)), BS((T, Wd), lambda i: (0, col_dy)),
                  BS((Gp, HEAD, HEAD), lambda i: (0, 0, 0)), BS((1, Wd), lambda i: (0, 0))],
        out_specs=[BS((T, Wd), lambda i: (0, 0)), BS((Gp, HEAD, HEAD), lambda i: (0, 0, 0)), BS((1, Wd), lambda i: (0, 0))],
        out_shape=[S((T, Wd), bf16), S((Gp, HEAD, HEAD), f32), S((1, Wd), f32)], compiler_params=_params(),
    )(P, dmix, pw, ps)


def _adamw(name, w, g, m, v):
    R, C = w.shape
    lanes = -(-C // 128) * 128
    tr = _tile(R, max(8, (512 * 1024) // lanes // 8 * 8), 8)
    c1 = 1.0 - ADAM_B1 ** ADAM_STEP
    c2 = 1.0 - ADAM_B2 ** ADAM_STEP

    def body(w_ref, g_ref, m_ref, v_ref, d_ref, nm_ref, nv_ref):
        gv = g_ref[...]
        nm = ADAM_B1 * m_ref[...] + (1.0 - ADAM_B1) * gv
        nv = ADAM_B2 * v_ref[...] + (1.0 - ADAM_B2) * (gv * gv)
        d_ref[...] = -ADAM_LR * ((nm / c1) / (jnp.sqrt(nv / c2) + ADAM_EPS) + ADAM_WD * w_ref[...])
        nm_ref[...] = nm
        nv_ref[...] = nv

    blk = BS((tr, C), lambda i: (i, 0))
    return pl.pallas_call(
        body, name=name, grid=(R // tr,), in_specs=[blk] * 4, out_specs=[blk] * 3, out_shape=[S((R, C), f32)] * 3,
    )(w, g, m, v)


def _adamw_layer(name, layer, w_all, m_all, v_all, g, prev):
    L, R, C = w_all.shape
    lanes = -(-C // 128) * 128
    tr = _tile(R, max(8, (512 * 1024) // lanes // 8 * 8), 8)
    c1 = 1.0 - ADAM_B1 ** ADAM_STEP
    c2 = 1.0 - ADAM_B2 ** ADAM_STEP
    n_prev = 0 if prev is None else 4

    def body(w_ref, m_ref, v_ref, g_ref, *rest):
        go_ref, d_ref, nm_ref, nv_ref = rest[n_prev:]
        gv = g_ref[...]
        nm = ADAM_B1 * m_ref[...] + (1.0 - ADAM_B1) * gv
        nv = ADAM_B2 * v_ref[...] + (1.0 - ADAM_B2) * (gv * gv)
        d_ref[...] = -ADAM_LR * ((nm / c1) / (jnp.sqrt(nv / c2) + ADAM_EPS) + ADAM_WD * w_ref[...])
        nm_ref[...] = nm
        nv_ref[...] = nv
        go_ref[...] = gv

    slab = BS((None, tr, C), lambda r: (layer, r, 0))
    return pl.pallas_call(
        body, name=name, grid=(R // tr,),
        in_specs=[slab, slab, slab, BS((tr, C), lambda r: (r, 0))] + [ANY] * n_prev,
        out_specs=[slab] * 4, out_shape=[S((L, R, C), f32)] * 4,
        input_output_aliases={4 + k: k for k in range(n_prev)},
    )(w_all, m_all, v_all, g, *(prev or ()))


def _chip_of(k):
    return k // 2, k % 2


def _remote(src, dst, send_sems, recv_sems, idx, dev):
    return pltpu.make_async_remote_copy(src_ref=src, dst_ref=dst, send_sem=send_sems.at[idx], recv_sem=recv_sems.at[idx],
                                        device_id=dev, device_id_type=MESH)


def _plan_gather_near(n):
    def plan(refs, ss, rs, base):
        ins, lands = refs[:n], refs[n:]
        x, y, c, j0 = _my_place()
        sib = (x, y, 1 - c)
        sends, recvs = [], []
        for a in range(n):
            half = ins[a].shape[0] // 2
            lo = c * half
            sends.append(_remote(ins[a], lands[a].at[j0], ss, rs, base + 3 * a + 2, sib))
            recvs.append(_remote(lands[a].at[j0], lands[a].at[j0], ss, rs, base + 3 * a + 2, sib))
            for r in (1, 2):
                k = j0 ^ r
                dev = (*_chip_of(k), c)
                sends.append(_remote(ins[a].at[pl.ds(lo, half)], lands[a].at[j0, pl.ds(lo, half)], ss, rs,
                                     base + 3 * a + r - 1, dev))
                landed = lands[a].at[k, pl.ds(lo, half)]
                recvs.append(_remote(landed, landed, ss, rs, base + 3 * a + r - 1, dev))
        return sends, recvs
    return plan, 3 * n


def _plan_gather_relay(n):
    def plan(refs, ss, rs, base):
        x, y, c, j0 = _my_place()
        sib = (x, y, 1 - c)
        sends, recvs = [], []
        for a in range(n):
            half = refs[a].shape[1] // 2
            quarter = half // 2
            lo = c * half
            far = j0 ^ 3
            for r, to, off in ((1, 2, 0), (2, 1, quarter)):
                dev = (*_chip_of(j0 ^ to), c)
                piece = refs[a].at[j0 ^ r, pl.ds(lo + off, quarter)]
                sends.append(_remote(piece, piece, ss, rs, base + 4 * a + to - 1, dev))
                lands_here = refs[a].at[far, pl.ds(lo + off, quarter)]
                recvs.append(_remote(lands_here, lands_here, ss, rs, base + 4 * a + to - 1, dev))
                mine = refs[a].at[j0 ^ r, pl.ds(lo, half)]
                theirs = refs[a].at[j0 ^ r, pl.ds((1 - c) * half, half)]
                sends.append(_remote(mine, mine, ss, rs, base + 4 * a + 1 + r, sib))
                recvs.append(_remote(theirs, theirs, ss, rs, base + 4 * a + 1 + r, sib))
        return sends, recvs
    return plan, 4 * n


def _plan_gather_far(n):
    def plan(refs, ss, rs, base):
        x, y, c, j0 = _my_place()
        sib = (x, y, 1 - c)
        sends, recvs = [], []
        for a in range(n):
            half = refs[a].shape[1] // 2
            mine = refs[a].at[j0 ^ 3, pl.ds(c * half, half)]
            theirs = refs[a].at[j0 ^ 3, pl.ds((1 - c) * half, half)]
            sends.append(_remote(mine, mine, ss, rs, base + a, sib))
            recvs.append(_remote(theirs, theirs, ss, rs, base + a, sib))
        return sends, recvs
    return plan, n


def _plan_sibling_halves(n):
    def plan(refs, ss, rs, base):
        ins, lands = refs[:n], refs[n:]
        x, y, c, _ = _my_place()
        sib = (x, y, 1 - c)
        sends, recvs = [], []
        for a in range(n):
            half = ins[a].shape[1] // 2
            sends.append(_remote(ins[a].at[:, pl.ds((1 - c) * half, half), :], lands[a], ss, rs, base + a, sib))
            recvs.append(_remote(lands[a], lands[a], ss, rs, base + a, sib))
        return sends, recvs
    return plan, n


def _plan_chip_scatter(n):
    def plan(refs, ss, rs, base):
        ins, lands = refs[:n], refs[n:]
        x, y, c, j0 = _my_place()
        sends, recvs = [], []
        for a in range(n):
            for r in (1, 2, 3):
                k = j0 ^ r
                dev = (*_chip_of(k), c)
                sends.append(_remote(ins[a].at[k], lands[a].at[j0], ss, rs, base + 3 * a + r - 1, dev))
                recvs.append(_remote(lands[a].at[k], lands[a].at[k], ss, rs, base + 3 * a + r - 1, dev))
        return sends, recvs
    return plan, 3 * n


def _plan_sibling_join(n):
    def plan(refs, ss, rs, base):
        x, y, c, _ = _my_place()
        sib = (x, y, 1 - c)
        sends, recvs = [], []
        for a in range(n):
            half = refs[a].shape[0] // 2
            mine = refs[a].at[pl.ds(c * half, half)]
            theirs = refs[a].at[pl.ds((1 - c) * half, half)]
            sends.append(_remote(mine, mine, ss, rs, base + a, sib))
            recvs.append(_remote(theirs, theirs, ss, rs, base + a, sib))
        return sends, recvs
    return plan, n


_HBM = pl.BlockSpec(memory_space=pltpu.HBM)
_SEM = pl.BlockSpec(memory_space=pltpu.SEMAPHORE)
_EFFECT = pltpu.SideEffectType.DATAFLOW_SIDE_EFFECTING


def _exchange_start(name, plan, bufs, after):
    plan_fn, n_sems = plan
    n = len(bufs)

    def body(*refs):
        ss, rs, token = refs[n + len(after)], refs[n + len(after) + 1], refs[-1]
        sends, _ = plan_fn(refs[:n], ss, rs, 0)
        for cp in sends:
            cp.start()
        token[...] = jnp.zeros_like(token)

    res = pl.pallas_call(
        body, name=name,
        out_shape=(pltpu.SemaphoreType.DMA((n_sems,)), pltpu.SemaphoreType.DMA((n_sems,)),
                   *[pltpu.HBM(b.shape, b.dtype) for b in bufs], S((8, 128), f32)),
        in_specs=[_HBM] * n + [ANY] * len(after),
        out_specs=(_SEM, _SEM, *[_HBM] * n, pl.BlockSpec(memory_space=pltpu.VMEM)),
        input_output_aliases={k: 2 + k for k in range(n)},
        compiler_params=pltpu.CompilerParams(has_side_effects=_EFFECT),
    )(*[pltpu.with_memory_space_constraint(b, pltpu.HBM) for b in bufs], *after)
    return res[0], res[1], list(res[2:2 + n]), res[-1]


def _exchange_wait(name, plan, send_sems, recv_sems, bufs, after):
    plan_fn, _ = plan
    n = len(bufs)

    def body(*refs):
        ss, rs, token = refs[n], refs[n + 1], refs[-1]
        sends, recvs = plan_fn(refs[:n], ss, rs, 0)
        for cp in recvs:
            cp.wait_recv()
        for cp in sends:
            cp.wait_send()
        token[...] = jnp.zeros_like(token)

    res = pl.pallas_call(
        body, name=name,
        out_shape=(*[pltpu.HBM(b.shape, b.dtype) for b in bufs], S((8, 128), f32)),
        in_specs=[_HBM] * n + [_SEM, _SEM] + [ANY] * len(after),
        out_specs=(*[_HBM] * n, pl.BlockSpec(memory_space=pltpu.VMEM)),
        input_output_aliases={k: k for k in range(n)},
        compiler_params=pltpu.CompilerParams(has_side_effects=_EFFECT),
    )(*bufs, send_sems, recv_sems, *after)
    return list(res[:n]), res[-1]


class _Order:
    def __init__(self, first):
        self.marker = first
        self.token = None

    def _after(self):
        return [self.marker] + ([] if self.token is None else [self.token])

    def start(self, name, plan, bufs):
        ss, rs, thru, self.token = _exchange_start(name, plan, bufs, self._after())
        return name, plan, ss, rs, thru

    def wait(self, handle):
        name, plan, ss, rs, thru = handle
        out, self.token = _exchange_wait(name + "_wait", plan, ss, rs, thru, self._after())
        return out

    def follows(self, small):
        return small if self.token is None else small + self.token[0, 0]

    def done(self, result):
        self.marker = result[(slice(0, 1),) * result.ndim].reshape(1, 1)


def _all_reduce_small(name, g8):
    _, R, L = g8.shape

    def body(g_ref, out_ref, land, red, send1, recv1, send2, recv2):
        x, y, c, _ = _my_place()
        me = 4 * x + 2 * y + c
        peers = []
        for r in range(1, N_DEV):
            q = me ^ r
            peers.append((q, (q // 4, (q // 2) % 2, q % 2)))
        first = []
        for r, (q, dev) in enumerate(peers):
            cp = pltpu.make_async_remote_copy(src_ref=g_ref.at[q], dst_ref=land.at[me], send_sem=send1.at[r],
                                              recv_sem=recv1.at[r], device_id=dev, device_id_type=MESH)
            cp.start()
            first.append(cp)
        land[me] = g_ref[me]
        for r, (q, dev) in enumerate(peers):
            pltpu.make_async_remote_copy(src_ref=land.at[q], dst_ref=land.at[q], send_sem=send1.at[r],
                                         recv_sem=recv1.at[r], device_id=dev, device_id_type=MESH).wait_recv()
        acc = land[0]
        for d in range(1, N_DEV):
            acc = acc + land[d]
        red[...] = acc
        out_ref[me] = acc
        second = []
        for r, (q, dev) in enumerate(peers):
            cp = pltpu.make_async_remote_copy(src_ref=red, dst_ref=out_ref.at[me], send_sem=send2.at[r],
                                              recv_sem=recv2.at[r], device_id=dev, device_id_type=MESH)
            cp.start()
            second.append(cp)
        for r, (q, dev) in enumerate(peers):
            pltpu.make_async_remote_copy(src_ref=out_ref.at[q], dst_ref=out_ref.at[q], send_sem=send2.at[r],
                                         recv_sem=recv2.at[r], device_id=dev, device_id_type=MESH).wait_recv()
        for cp in first + second:
            cp.wait_send()

    vm = pl.BlockSpec(memory_space=pltpu.VMEM)
    return pl.pallas_call(
        body, name=name, in_specs=[vm], out_specs=vm, out_shape=S(g8.shape, f32),
        scratch_shapes=[pltpu.VMEM((N_DEV, R, L), f32), pltpu.VMEM((R, L), f32)]
        + [pltpu.SemaphoreType.DMA((N_DEV - 1,))] * 4,
        compiler_params=_params(),
    )(g8)


def _pair_sum(name, g4, sib):
    _, rows, cols = g4.shape
    half = rows // 2
    lanes = -(-cols // 128) * 128
    tr = _tile(half, max(16, (512 * 1024) // lanes // 16 * 16), 16)
    nb = half // tr

    def body(g_ref, s_ref, pb_ref, own_ref):
        j = pl.program_id(1)
        t = g_ref[...] + s_ref[...]
        pb_ref[...] = t.astype(bf16)

        @pl.when(j == _my_place()[3])
        def _():
            own_ref[...] = t

    return pl.pallas_call(
        body, name=name, grid=(nb, N_CHIPS),
        in_specs=[BS((None, tr, cols), lambda i, j: (j, lax.axis_index("c") * nb + i, 0)),
                  BS((None, tr, cols), lambda i, j: (j, i, 0))],
        out_specs=[BS((None, tr, cols), lambda i, j: (j, i, 0)), BS((tr, cols), lambda i, j: (i, 0))],
        out_shape=[S((N_CHIPS, half, cols), bf16), S((half, cols), f32)],
    )(g4, sib)


def _chip_sum(name, own, got):
    half, cols = own.shape
    lanes = -(-cols // 128) * 128
    tr = _tile(half, max(16, (512 * 1024) // lanes // 16 * 16), 16)
    nb = half // tr

    def body(own_ref, *rest):
        got_refs, o_ref = rest[:N_CHIPS], rest[N_CHIPS]
        j0 = _my_place()[3]
        acc = None
        for k in range(N_CHIPS):
            t = jnp.where(j0 == k, own_ref[...], got_refs[k][...].astype(f32))
            acc = t if acc is None else acc + t
        o_ref[...] = acc

    def slot(k):
        return BS((None, tr, cols), lambda i: (jnp.where(_my_place()[3] == k, (k + 1) % N_CHIPS, k), i, 0))

    return pl.pallas_call(
        body, name=name, grid=(nb,),
        in_specs=[BS((tr, cols), lambda i: (i, 0))] + [slot(k) for k in range(N_CHIPS)],
        out_specs=BS((tr, cols), lambda i: (lax.axis_index("c") * nb + i, 0)),
        out_shape=S((2 * half, cols), f32),
    )(own, got, got, got, got)


def _rows_of(size):
    return -(-size // 1024) * 8


def _pack_rows(arrs, n_rows):
    parts = []
    for a in arrs:
        rows = _rows_of(a.size)
        if a.size % 128 == 0:
            part = a.astype(f32).reshape(-1, 128)
            part = jnp.pad(part, ((0, rows - part.shape[0]), (0, 0)))
        else:
            part = jnp.pad(a.reshape(-1).astype(f32), (0, rows * 128 - a.size)).reshape(rows, 128)
        parts.append(part)
    used = sum(p.shape[0] for p in parts)
    return jnp.concatenate(parts + [jnp.zeros((n_rows - used, 128), f32)], axis=0)


def _unpack_rows(packed, shapes):
    out, row = [], 0
    for shp in shapes:
        size = 1
        for d in shp:
            size *= d
        rows = packed[row:row + _rows_of(size)]
        out.append(rows[:size // 128].reshape(shp) if size % 128 == 0 else rows.reshape(-1)[:size].reshape(shp))
        row += _rows_of(size)
    return out


def kernel(x, p, norm_mix, w_in, q_norm, k_norm, forget_bias, gmlp_v_norm, gmlp_w_s, gmlp_b_s, pool_w, pool_scale, w_out, norm_ffn, w_ffn_gate, w_ffn_up, w_ffn_down, norm_ple, w_ple_gate, w_ple_proj, loss_target, m_norm_mix, m_w_in, m_q_norm, m_k_norm, m_forget_bias, m_gmlp_v_norm, m_gmlp_w_s, m_gmlp_b_s, m_pool_w, m_pool_scale, m_w_out, m_norm_ffn, m_w_ffn_gate, m_w_ffn_up, m_w_ffn_down, m_norm_ple, m_w_ple_gate, m_w_ple_proj, v_norm_mix, v_w_in, v_q_norm, v_k_norm, v_forget_bias, v_gmlp_v_norm, v_gmlp_w_s, v_gmlp_b_s, v_pool_w, v_pool_scale, v_w_out, v_norm_ffn, v_w_ffn_gate, v_w_ffn_up, v_w_ffn_down, v_norm_ple, v_w_ple_gate, v_w_ple_proj):
    W = dict(norm_mix=norm_mix, w_in=w_in, q_norm=q_norm, k_norm=k_norm, forget_bias=forget_bias,
             gmlp_v_norm=gmlp_v_norm, gmlp_w_s=gmlp_w_s, gmlp_b_s=gmlp_b_s, pool_w=pool_w, pool_scale=pool_scale,
             w_out=w_out, norm_ffn=norm_ffn, w_ffn_gate=w_ffn_gate, w_ffn_up=w_ffn_up, w_ffn_down=w_ffn_down,
             norm_ple=norm_ple, w_ple_gate=w_ple_gate, w_ple_proj=w_ple_proj)
    M = dict(norm_mix=m_norm_mix, w_in=m_w_in, q_norm=m_q_norm, k_norm=m_k_norm, forget_bias=m_forget_bias,
             gmlp_v_norm=m_gmlp_v_norm, gmlp_w_s=m_gmlp_w_s, gmlp_b_s=m_gmlp_b_s, pool_w=m_pool_w,
             pool_scale=m_pool_scale, w_out=m_w_out, norm_ffn=m_norm_ffn, w_ffn_gate=m_w_ffn_gate,
             w_ffn_up=m_w_ffn_up, w_ffn_down=m_w_ffn_down, norm_ple=m_norm_ple, w_ple_gate=m_w_ple_gate,
             w_ple_proj=m_w_ple_proj)
    V = dict(norm_mix=v_norm_mix, w_in=v_w_in, q_norm=v_q_norm, k_norm=v_k_norm, forget_bias=v_forget_bias,
             gmlp_v_norm=v_gmlp_v_norm, gmlp_w_s=v_gmlp_w_s, gmlp_b_s=v_gmlp_b_s, pool_w=v_pool_w,
             pool_scale=v_pool_scale, w_out=v_w_out, norm_ffn=v_norm_ffn, w_ffn_gate=v_w_ffn_gate,
             w_ffn_up=v_w_ffn_up, w_ffn_down=v_w_ffn_down, norm_ple=v_norm_ple, w_ple_gate=v_w_ple_gate,
             w_ple_proj=v_w_ple_proj)

    L = w_in.shape[0]
    _, T, D = x.shape
    A, Wd = D // 2, D // 4
    H = A // HEAD
    G = gmlp_w_s.shape[1]
    Gp = pool_w.shape[1]
    DP4 = w_in.shape[2]
    DP = N_CHIPS * DP4
    NM = 3 * A + 3 * Wd
    FS = w_ffn_gate.shape[2]
    FF = N_CHIPS * FS
    DS = D // N_CHIPS
    PL = p.shape[-1]
    assert Wd // G == HEAD and Wd // Gp == HEAD and DP == NM + H and H <= HEAD
    assert all(w & (w - 1) == 0 for w in POOL_WINDOWS[:Gp])
    tb = _tile(T, 256, HEAD)
    nb = T // tb
    tm = _tile(T, 512, 16)
    tmw = _tile(T, 1024, 16)
    tn = _tile(NM, 512, 128)
    tnd = _tile(D, 512, 128)
    tkd = _tile(D, 1024, 128)
    tnw = _tile(D, 1024, 128)
    col_gu, col_gv, col_xp = 3 * A // HEAD, (3 * A + Wd) // HEAD, (3 * A + 2 * Wd) // Wd
    col_dg, col_dp = A // HEAD, (A + Wd) // Wd

    order = _Order(x[0, :1, :1])

    def gather_near(i, names, tag=""):
        shards = [_cast_layer(f"cast_{n}_{i}", W[n], i) for n in names]
        lands = [lax.empty((N_CHIPS,) + s.shape, bf16) for s in shards]
        return names, order.start(f"ag_near_{i}{tag}", _plan_gather_near(len(names)), shards + lands)

    def gather_relay(i, near, tag=""):
        names, handle = near
        return names, order.start(f"ag_relay_{i}{tag}", _plan_gather_relay(len(names)), order.wait(handle)[len(names):])

    def gather_far(i, relayed, tag=""):
        names, handle = relayed
        return names, order.start(f"ag_far_{i}{tag}", _plan_gather_far(len(names)), order.wait(handle))

    win_count = [0]

    def gathered(far):
        names, handle = far
        g = dict(zip(names, order.wait(handle)))
        out = {}
        if "w_in" in g:
            out["w_main"], out["w_f"] = _win_assemble(f"w_in_cols_{win_count[0]}", g["w_in"], A, H)
            win_count[0] += 1
        if "w_out" in g:
            out["w_out"] = g["w_out"].reshape(D, D)
        if "w_ffn_gate" in g:
            out.update(w_gate=g["w_ffn_gate"], w_up=g["w_ffn_up"], w_down=g["w_ffn_down"].reshape(FF, D),
                       w_pg=g["w_ple_gate"].reshape(D, D), w_pp=g["w_ple_proj"])
        return out

    Wf = [None] * L
    relayed = gather_relay(0, gather_near(0, BIG[:2], "a"), "a")
    near_rest = gather_near(0, BIG[2:], "c")
    Wf[0] = gathered(gather_far(0, relayed, "a"))
    near = relayed = None

    h = x.reshape(T, D)
    pb16 = p.reshape(L, T, PL).astype(bf16)
    saved = []

    for i in range(L):
        w = Wf[i]
        sv = dict(h0=h)
        xn1 = _rms_fwd(f"rms1_{i}", h, order.follows(norm_mix[i]))
        (P,) = _matmul(f"proj_{i}", "nn", (T // tmw, NM // tn),
                       [((xn1, BS((tmw, D), lambda i, j: (i, 0))), (w["w_main"], BS((D, tn), lambda i, j: (0, j))))], [],
                       [((T, NM), f32, BS((tmw, tn), lambda i, j: (i, j)))])
        (Pf,) = _matmul(f"projf_{i}", "nn", (T // tm, 1),
                        [((xn1, BS((tm, D), lambda i, j: (i, 0))), (w["w_f"], BS((D, HEAD), lambda i, j: (0, 0))))], [],
                        [((T, HEAD), f32, BS((tm, HEAD), lambda i, j: (i, 0)))])
        fb = jnp.pad(forget_bias[i], (0, HEAD - H)).reshape(1, HEAD)
        cc, ct = _fgate_fwd(f"fgate_{i}", Pf, fb)
        c_col = ct[:H].reshape(H, T, 1)
        c_row = ct[:H].reshape(H, nb, 1, tb)
        qg, kg = q_norm[i].reshape(1, HEAD), k_norm[i].reshape(1, HEAD)
        qn, kn, vb = _qk_norm(f"qknorm_{i}", P, qg, kg, A)
        mix, o32, lse = _attn_fwd(f"attn_{i}", qn, kn, vb, c_col, c_row, tb, D)
        if i == 0:
            order.done(o32)
            relayed_rest = gather_relay(0, near_rest, "c")
            near = gather_near(1, BIG) if L > 1 else None
        gain = gmlp_v_norm[i].reshape(G, 1, HEAD)
        bs = gmlp_b_s[i].reshape(G, HEAD, 1)
        mix = _gmlp_fwd(f"gmlp_{i}", P, mix, order.follows(gain), gmlp_w_s[i], bs, col_gu, col_gv, col_dg, Wd)
        ps = pool_scale[i].reshape(1, Wd)
        mix = _pool_fwd(f"pool_{i}", P, mix, pool_w[i], ps, col_xp, col_dp, Wd)
        (h1,) = _matmul(f"out_{i}", "nn", (T // tmw, D // tnd),
                        [((mix, BS((tmw, D), lambda i, j: (i, 0))), (w["w_out"], BS((D, tnd), lambda i, j: (0, j))))],
                        [(h, BS((tmw, tnd), lambda i, j: (i, j)))],
                        [((T, D), f32, BS((tmw, tnd), lambda i, j: (i, j)))],
                        epilogue=lambda accs, ex: (accs[0] + ex[0],))
        xn2 = _rms_fwd(f"rms2_{i}", h1, norm_ffn[i])
        order.done(xn2)
        if i == 0:
            w.update(gathered(gather_far(0, relayed_rest, "c")))
        elif i + 1 < L:
            relayed = gather_relay(i + 1, near)
            near = gather_near(i + 2, BIG) if i + 2 < L else None

        def ffn_epi(accs, ex):
            g_, u_ = accs
            return g_, u_, g_ * jax.nn.sigmoid(g_) * u_

        ffo = BS((tm, FS), lambda j, i: (i, j))
        Gt, Ut, act = _matmul(f"ffn1_{i}", "nn", (N_CHIPS, T // tm),
                              [((xn2, BS((tm, D), lambda j, i: (i, 0))), (w["w_gate"], BS((None, D, FS), lambda j, i: (j, 0, 0)))),
                               ((xn2, BS((tm, D), lambda j, i: (i, 0))), (w["w_up"], BS((None, D, FS), lambda j, i: (j, 0, 0))))],
                              [], [((T, FF), bf16, ffo)] * 3, epilogue=ffn_epi, after=order.token)
        (h2,) = _matmul(f"ffn2_{i}", "nn", (T // tmw, D // tnd),
                        [((act, BS((tmw, FF), lambda i, j: (i, 0))), (w["w_down"], BS((FF, tnd), lambda i, j: (0, j))))],
                        [(h1, BS((tmw, tnd), lambda i, j: (i, j)))],
                        [((T, D), f32, BS((tmw, tnd), lambda i, j: (i, j)))],
                        epilogue=lambda accs, ex: (accs[0] + ex[0],))
        if i == 0 and L > 1:
            order.done(h2)
            relayed = gather_relay(1, near)
            near = gather_near(2, BIG) if L > 2 else None
        xn3 = _rms_fwd(f"rms3_{i}", h2, order.follows(norm_ple[i]))

        def ple_epi(accs, ex):
            gate = jax.nn.sigmoid(accs[0])
            return ex[0] + accs[1] * gate, gate, accs[1]

        dso = BS((tmw, DS), lambda i, j: (i, j))
        h3, gate, e = _matmul(f"ple_{i}", "nn", (T // tmw, N_CHIPS),
                              [((xn3, BS((tmw, D), lambda i, j: (i, 0))), (w["w_pg"], BS((D, DS), lambda i, j: (0, j)))),
                               ((pb16[i], BS((tmw, PL), lambda i, j: (i, 0))), (w["w_pp"], BS((None, PL, DS), lambda i, j: (j, 0, 0))))],
                              [(h2, dso)], [((T, D), f32, dso), ((T, D), bf16, dso), ((T, D), bf16, dso)], epilogue=ple_epi)
        sv.update(xn1=xn1, P=P, Pf=Pf, fb=fb, c_col=c_col, c_row=c_row, qn=qn, kn=kn, vb=vb, o32=o32, lse=lse,
                  mix=mix, h1=h1, xn2=xn2, Gt=Gt, Ut=Ut, act=act, h2=h2, xn3=xn3, gate=gate, e=e)
        saved.append(sv)
        h = h3
        order.done(h3)
        if i + 1 < L:
            Wf[i + 1] = gathered(gather_far(i + 1, relayed))

    dh, loss_tile = _loss_grad("loss", h, loss_target.reshape(T, D))

    small_g = {n: [None] * L for n in SMALL}
    big_out = {}

    def stage_a(u):
        n_u = len(u["names"])
        lands = [lax.empty((N_CHIPS, g.shape[1] // 2, g.shape[2]), f32) for g in u["grads"]]
        u["a"] = order.start(f"rs_a_{u['tag']}", _plan_sibling_halves(n_u), u["grads"] + lands)

    def stage_pair(u):
        n_u = len(u["names"])
        out = order.wait(u["a"])
        pairs = [_pair_sum(f"rs_pair_{u['tag']}_{a}", out[a], out[n_u + a]) for a in range(n_u)]
        u["pb"], u["own"] = [t[0] for t in pairs], [t[1] for t in pairs]

    def stage_b(u):
        lands = [lax.empty(t.shape, bf16) for t in u["pb"]]
        u["b"] = order.start(f"rs_b_{u['tag']}", _plan_chip_scatter(len(u["names"])), u["pb"] + lands)

    def stage_sum(u):
        n_u = len(u["names"])
        out = order.wait(u["b"])
        u["sum"] = [_chip_sum(f"rs_sum_{u['tag']}_{a}", u["own"][a], out[n_u + a]) for a in range(n_u)]

    def stage_c(u):
        u["c"] = order.start(f"rs_c_{u['tag']}", _plan_sibling_join(len(u["names"])), u["sum"])

    def stage_adamw(u):
        for n, r in zip(u["names"], order.wait(u["c"])):
            big_out[n] = _adamw_layer(f"adamw_{n}_{u['layer']}", u["layer"], W[n], M[n], V[n],
                                      r.reshape(W[n].shape[1:]), big_out.get(n))
    dh1 = prev_f = prev_m = None
    for i in reversed(range(L)):
        w, sv = Wf[i], saved[i]
        if dh1 is not None:
            dh, _, dg = _rms_bwd(f"rms1_bw_{i + 1}", dxn1, saved[i + 1]["h0"], order.follows(norm_mix[i + 1]), dh1)
            small_g["norm_mix"][i + 1] = dg.reshape(D)
        de, dz = _ple_bwd_elem(f"ple_bw_{i}", dh, sv["gate"], sv["e"])
        (d_wpp,) = _matmul(f"d_wpp_{i}", "tn", (N_CHIPS, 1),
                           [((pb16[i], BS((T, PL), lambda i, j: (0, 0))), (de, BS((T, DS), lambda i, j: (0, i))))], [],
                           [((N_CHIPS, PL, DS), f32, BS((None, PL, DS), lambda i, j: (i, 0, 0)))])
        (d_wpg,) = _matmul(f"d_wpg_{i}", "tn", (D // tkd, D // tnd),
                           [((sv["xn3"], BS((T, tkd), lambda i, j: (0, i))), (dz, BS((T, tnd), lambda i, j: (0, j))))], [],
                           [((D, D), f32, BS((tkd, tnd), lambda i, j: (i, j)))])
        order.done(dz)
        if prev_m is not None:
            stage_pair(prev_m)
        (dxn3,) = _matmul(f"d_xn3_{i}", "nt", (T // tmw, D // tnd),
                          [((dz, BS((tmw, D), lambda i, j: (i, 0))), (w["w_pg"], BS((tnd, D), lambda i, j: (j, 0))))], [],
                          [((T, D), f32, BS((tmw, tnd), lambda i, j: (i, j)))], after=order.token)
        dh2, dh2b, dg = _rms_bwd(f"rms3_bw_{i}", dxn3, sv["h2"], norm_ple[i], dh)
        small_g["norm_ple"][i] = dg.reshape(D)

        def dffn_epi(accs, ex):
            da = accs[0]
            g_, u_ = ex[0].astype(f32), ex[1].astype(f32)
            sg = jax.nn.sigmoid(g_)
            return da * u_ * (sg * (1.0 + g_ * (1.0 - sg))), da * (g_ * sg)

        ffo = BS((tm, FS), lambda j, i: (i, j))
        dG, dU = _matmul(f"d_act_{i}", "nt", (N_CHIPS, T // tm),
                         [((dh2b, BS((tm, D), lambda j, i: (i, 0))), (w["w_down"], BS((FS, D), lambda j, i: (j, 0))))],
                         [(sv["Gt"], ffo), (sv["Ut"], ffo)], [((T, FF), bf16, ffo)] * 2, epilogue=dffn_epi)
        (d_wd,) = _matmul(f"d_wd_{i}", "tn", (N_CHIPS, D // tnw),
                          [((sv["act"], BS((T, FS), lambda i, j: (0, i))), (dh2b, BS((T, tnw), lambda i, j: (0, j))))], [],
                          [((FF, D), f32, BS((FS, tnw), lambda i, j: (i, j)))])
        gu_out = BS((None, tnd, FS), lambda j, i: (j, i, 0))
        d_wg, d_wu = _matmul(f"d_wgu_{i}", "tn", (N_CHIPS, D // tnd),
                             [((sv["xn2"], BS((T, tnd), lambda j, i: (0, i))), (dG, BS((T, FS), lambda j, i: (0, j)))),
                              ((sv["xn2"], BS((T, tnd), lambda j, i: (0, i))), (dU, BS((T, FS), lambda j, i: (0, j))))], [],
                             [((N_CHIPS, D, FS), f32, gu_out)] * 2, epilogue=lambda accs, ex: (accs[0], accs[1]))
        order.done(dG)
        unit_f = dict(tag=f"{i}f", layer=i, names=["w_ffn_gate", "w_ffn_up", "w_ffn_down", "w_ple_gate", "w_ple_proj"],
                      grads=[d_wg, d_wu, d_wd.reshape(N_CHIPS, FS, D), d_wpg.reshape(N_CHIPS, DS, D), d_wpp])
        stage_a(unit_f)
        if prev_f is not None:
            stage_sum(prev_f)
            stage_c(prev_f)
        if prev_m is not None:
            stage_b(prev_m)
        tm2 = _tile(T, 512, 16)
        (dxn2,) = _matmul(f"d_xn2_{i}", "nt", (D // tnd, T // tm2),
                          [((dG, BS((tm2, FF), lambda j, i: (i, 0))), (w["w_gate"], BS((N_CHIPS, tnd, FS), lambda j, i: (0, j, 0)))),
                           ((dU, BS((tm2, FF), lambda j, i: (i, 0))), (w["w_up"], BS((N_CHIPS, tnd, FS), lambda j, i: (0, j, 0))))], [],
                          [((T, D), f32, BS((tm2, tnd), lambda j, i: (i, j)))], after=order.token)
        dh1, dh1b, dg = _rms_bwd(f"rms2_bw_{i}", dxn2, sv["h1"], norm_ffn[i], dh2)
        small_g["norm_ffn"][i] = dg.reshape(D)
        (dmix,) = _matmul(f"d_mix_{i}", "nt", (T // tmw, D // tnd),
                          [((dh1b, BS((tmw, D), lambda i, j: (i, 0))), (w["w_out"], BS((tnd, D), lambda i, j: (j, 0))))], [],
                          [((T, D), f32, BS((tmw, tnd), lambda i, j: (i, j)))])
        (d_wout,) = _matmul(f"d_wout_{i}", "tn", (D // tkd, D // tnd),
                            [((sv["mix"], BS((T, tkd), lambda i, j: (0, i))), (dh1b, BS((T, tnd), lambda i, j: (0, j))))], [],
                            [((D, D), f32, BS((tkd, tnd), lambda i, j: (i, j)))])
        order.done(dmix)
        stage_pair(unit_f)
        stage_b(unit_f)
        if prev_f is not None:
            stage_adamw(prev_f)
        qg, kg = q_norm[i].reshape(1, HEAD), k_norm[i].reshape(1, HEAD)
        dq, dk, dv, dc_row, dqg, dkg = _attn_bwd(f"attn_bw_{i}", sv["qn"], sv["kn"], sv["vb"], sv["o32"], dmix,
                                                 sv["lse"], sv["c_col"], sv["c_row"], sv["P"], order.follows(qg), kg, tb)
        small_g["q_norm"][i] = dqg.reshape(HEAD)
        small_g["k_norm"][i] = dkg.reshape(HEAD)
        dct = jnp.pad(dc_row.reshape(H, T), ((0, HEAD - H), (0, 0)))
        dPf, dfb = _fgate_bwd(f"fgate_bw_{i}", dct, sv["Pf"], sv["fb"])
        small_g["forget_bias"][i] = dfb[0, :H]
        gain = gmlp_v_norm[i].reshape(G, 1, HEAD)
        bs = gmlp_b_s[i].reshape(G, HEAD, 1)
        dgu, dgv, dws, dbs, dgain = _gmlp_bwd(f"gmlp_bw_{i}", sv["P"], dmix, gain, gmlp_w_s[i], bs, col_gu, col_gv,
                                              col_dg, Wd)
        small_g["gmlp_w_s"][i] = dws
        small_g["gmlp_b_s"][i] = dbs.reshape(G, HEAD)
        small_g["gmlp_v_norm"][i] = dgain.reshape(G, HEAD)
        ps = pool_scale[i].reshape(1, Wd)
        dxp, dpw, dps = _pool_bwd(f"pool_bw_{i}", sv["P"], dmix, pool_w[i], ps, col_xp, col_dp, Wd)
        small_g["pool_w"][i] = dpw
        small_g["pool_scale"][i] = dps.reshape(Wd)
        dP = jnp.concatenate([dq, dk, dv, dgu, dgv, dxp], axis=1)
        (d_wmain,) = _matmul(f"d_wmain_{i}", "tn", (D // tkd, NM // tn),
                             [((sv["xn1"], BS((T, tkd), lambda i, j: (0, i))), (dP, BS((T, tn), lambda i, j: (0, j))))], [],
                             [((D, NM), f32, BS((tkd, tn), lambda i, j: (i, j)))])
        (d_wf,) = _matmul(f"d_wf_{i}", "tn", (D // tnd, 1),
                          [((sv["xn1"], BS((T, tnd), lambda i, j: (0, i))), (dPf, BS((T, HEAD), lambda i, j: (0, 0))))], [],
                          [((D, HEAD), f32, BS((tnd, HEAD), lambda i, j: (i, 0)))])
        d_win4 = _dwin_split(f"d_win_cols_{i}", d_wmain, d_wf, A, H)
        order.done(dP)
        unit_m = dict(tag=f"{i}m", layer=i, names=["w_in", "w_out"], grads=[d_win4, d_wout.reshape(N_CHIPS, DS, D)])
        stage_a(unit_m)
        if prev_m is not None:
            stage_sum(prev_m)
            stage_c(prev_m)
        (dxn1,) = _matmul(f"d_xn1_{i}", "nt", (D // tnd, T // tm),
                          [((dP, BS((tm, NM), lambda j, i: (i, 0))), (w["w_main"], BS((tnd, NM), lambda j, i: (j, 0)))),
                           ((dPf, BS((tm, HEAD), lambda j, i: (i, 0))), (w["w_f"], BS((tnd, HEAD), lambda j, i: (j, 0))))], [],
                          [((T, D), f32, BS((tm, tnd), lambda j, i: (i, j)))], after=order.token)
        order.done(dxn1)
        if prev_m is not None:
            stage_adamw(prev_m)
        prev_f, prev_m = unit_f, unit_m

    dh, _, dg = _rms_bwd("rms1_bw_0", dxn1, saved[0]["h0"], order.follows(norm_mix[0]), dh1)
    small_g["norm_mix"][0] = dg.reshape(D)
    order.done(dh)

    small_full = {n: jnp.stack(small_g[n]) for n in SMALL}
    small_shapes = [W[n].shape for n in SMALL]
    n_rows = sum(_rows_of(W[n].size) for n in SMALL) + _rows_of(1)
    rows8 = -(-n_rows // 64) * 8
    packed = order.follows(_pack_rows([small_full[n] for n in SMALL] + [loss_tile[0, :1]], N_DEV * rows8))
    summed = _all_reduce_small("allreduce_small", packed.reshape(N_DEV, rows8, 128)).reshape(-1, 128)
    order.done(summed)

    stage_pair(prev_m)
    stage_b(prev_m)
    stage_sum(prev_f)
    stage_c(prev_f)
    stage_adamw(prev_f)
    stage_sum(prev_m)
    stage_c(prev_m)
    stage_adamw(prev_m)
    *small_grads, loss_row = _unpack_rows(summed, small_shapes + [(1,)])
    grads = dict(zip(SMALL, small_grads))
    loss = loss_row[0]

    wp, mp, vp = (_pack_rows([t[n] for n in SMALL], N_DEV * rows8) for t in (W, M, V))
    delta, new_m, new_v = (dict(zip(SMALL, _unpack_rows(t, small_shapes)))
                           for t in _adamw("adamw_small", wp, summed, mp, vp))
    for n in BIG:
        grads[n], delta[n], new_m[n], new_v[n] = big_out[n]

    return (loss, dh.reshape(1, T, D), *[grads[n] for n in WEIGHTS], *[delta[n] for n in WEIGHTS],
            *[new_m[n] for n in WEIGHTS], *[new_v[n] for n in WEIGHTS])
```

```python
import jax
import jax.numpy as jnp
from jax import lax
from jax.experimental import pallas as pl
from jax.experimental.pallas import tpu as pltpu

f32, bf16 = jnp.float32, jnp.bfloat16
S = jax.ShapeDtypeStruct
BS = pl.BlockSpec
ANY = pl.BlockSpec(memory_space=pl.ANY)
MESH = pl.DeviceIdType.MESH

EPS = 1e-6
HEAD = 128
POOL_WINDOWS = (2, 4, 8, 16)
NEG = -1e30
N_CHIPS = 4
N_DEV = 8
VMEM_LIMIT = 56 * 1024 * 1024

ADAM_LR, ADAM_B1, ADAM_B2, ADAM_EPS, ADAM_WD, ADAM_STEP = 0.001, 0.9, 0.999, 1e-08, 0.01, 10

BIG = ("w_in", "w_out", "w_ffn_gate", "w_ffn_up", "w_ffn_down", "w_ple_gate", "w_ple_proj")
SMALL = ("norm_mix", "q_norm", "k_norm", "forget_bias", "gmlp_v_norm", "gmlp_w_s", "gmlp_b_s", "pool_w",
         "pool_scale", "norm_ffn", "norm_ple")
WEIGHTS = ("norm_mix", "w_in", "q_norm", "k_norm", "forget_bias", "gmlp_v_norm", "gmlp_w_s", "gmlp_b_s", "pool_w",
           "pool_scale", "w_out", "norm_ffn", "w_ffn_gate", "w_ffn_up", "w_ffn_down", "norm_ple", "w_ple_gate",
           "w_ple_proj")


def _tile(n, target, mult):
    best = None
    for t in range(mult, min(n, target) + 1, mult):
        if n % t == 0:
            best = t
    return best if best is not None else n


def _params(**kw):
    return pltpu.CompilerParams(vmem_limit_bytes=VMEM_LIMIT, **kw)


def _dot(a, b, kind):
    dims = {"nn": (((1,), (0,)), ((), ())), "nt": (((1,), (1,)), ((), ())), "tn": (((0,), (0,)), ((), ()))}[kind]
    return lax.dot_general(a.astype(bf16), b.astype(bf16), dims, preferred_element_type=f32)


def _heads_per_program(n_heads, want=2):
    while n_heads % want:
        want //= 2
    return want


def _my_place():
    x, y, c = lax.axis_index("x"), lax.axis_index("y"), lax.axis_index("c")
    return x, y, c, 2 * x + y


def _matmul(name, kind, grid, pairs, extras, outs, epilogue=None, after=None):
    n_p, n_e = len(pairs), len(extras)
    tokens = [] if after is None else [(after, BS((8, 128), lambda *_: (0, 0)))]

    def body(*refs):
        a_refs, b_refs = refs[:n_p], refs[n_p:2 * n_p]
        e_refs = refs[2 * n_p:2 * n_p + n_e]
        o_refs = refs[2 * n_p + n_e + len(tokens):]
        accs = []
        for a_ref, b_ref in zip(a_refs, b_refs):
            if len(b_ref.shape) == 3:
                w = b_ref.shape[2]
                acc = None
                for s in range(b_ref.shape[0]):
                    d = _dot(a_ref[:, s * w:(s + 1) * w], b_ref[s], kind)
                    acc = d if acc is None else acc + d
            else:
                acc = _dot(a_ref[...], b_ref[...], kind)
            accs.append(acc)
        if epilogue is None:
            res = accs[0]
            for t in accs[1:]:
                res = res + t
            res = (res,)
        else:
            res = epilogue(accs, [e[...] for e in e_refs])
        for o_ref, o in zip(o_refs, res):
            o_ref[...] = o.astype(o_ref.dtype)

    in_arrays = [p[0][0] for p in pairs] + [p[1][0] for p in pairs] + [e[0] for e in extras + tokens]
    in_specs = [p[0][1] for p in pairs] + [p[1][1] for p in pairs] + [e[1] for e in extras + tokens]
    res = pl.pallas_call(
        body, name=name, grid=grid, in_specs=in_specs,
        out_specs=[o[2] for o in outs], out_shape=[S(o[0], o[1]) for o in outs],
        compiler_params=_params(),
    )(*in_arrays)
    return res


def _rms_fwd(name, x, g):
    T, D = x.shape
    tr = _tile(T, 256, 8)

    def body(x_ref, g_ref, o_ref):
        xv = x_ref[...]
        r = lax.rsqrt(jnp.mean(xv * xv, axis=-1, keepdims=True) + EPS)
        o_ref[...] = (xv * r * g_ref[...]).astype(o_ref.dtype)

    return pl.pallas_call(
        body, name=name, grid=(T // tr,),
        in_specs=[BS((tr, D), lambda i: (i, 0)), BS((1, D), lambda i: (0, 0))],
        out_specs=BS((tr, D), lambda i: (i, 0)), out_shape=S((T, D), bf16),
    )(x, g.reshape(1, D))


def _rms_bwd(name, dxn, x, g, dres):
    T, D = x.shape
    tr = _tile(T, 256, 8)

    def body(dxn_ref, x_ref, g_ref, dres_ref, dx_ref, dxb_ref, dg_ref):
        i = pl.program_id(0)
        xv = x_ref[...]
        r = lax.rsqrt(jnp.mean(xv * xv, axis=-1, keepdims=True) + EPS)
        xh = xv * r
        dxn_v = dxn_ref[...]
        dxh = dxn_v * g_ref[...]
        dx = dres_ref[...] + r * (dxh - xh * jnp.mean(dxh * xh, axis=-1, keepdims=True))
        dx_ref[...] = dx
        dxb_ref[...] = dx.astype(bf16)
        part = jnp.sum(dxn_v * xh, axis=0, keepdims=True)

        @pl.when(i == 0)
        def _():
            dg_ref[...] = part

        @pl.when(i > 0)
        def _():
            dg_ref[...] += part

    row = BS((tr, D), lambda i: (i, 0))
    vec = BS((1, D), lambda i: (0, 0))
    return pl.pallas_call(
        body, name=name, grid=(T // tr,),
        in_specs=[row, row, vec, row], out_specs=[row, row, vec],
        out_shape=[S((T, D), f32), S((T, D), bf16), S((1, D), f32)],
    )(dxn, x, g.reshape(1, D), dres)


def _loss_grad(name, y, tgt):
    T, D = y.shape
    tr = _tile(T, 256, 8)

    def body(y_ref, t_ref, dy_ref, l_ref):
        i = pl.program_id(0)
        e = y_ref[...] - t_ref[...]
        dy_ref[...] = e * (1.0 / D)
        part = 0.5 * jnp.sum(jnp.mean(e * e, axis=-1, keepdims=True), axis=0, keepdims=True)

        @pl.when(i == 0)
        def _():
            l_ref[...] = jnp.zeros_like(l_ref)

        l_ref[...] += jnp.broadcast_to(part, l_ref.shape)

    row = BS((tr, D), lambda i: (i, 0))
    return pl.pallas_call(
        body, name=name, grid=(T // tr,), in_specs=[row, row],
        out_specs=[row, BS((8, 128), lambda i: (0, 0))],
        out_shape=[S((T, D), f32), S((8, 128), f32)],
    )(y, tgt)


def _ple_bwd_elem(name, dh, gate, e):
    T, D = dh.shape
    tr = _tile(T, 256, 16)

    def body(dh_ref, g_ref, e_ref, de_ref, dz_ref):
        d = dh_ref[...]
        g = g_ref[...].astype(f32)
        de_ref[...] = (d * g).astype(bf16)
        dz_ref[...] = (d * e_ref[...].astype(f32) * g * (1.0 - g)).astype(bf16)

    row = BS((tr, D), lambda i: (i, 0))
    return pl.pallas_call(
        body, name=name, grid=(T // tr,), in_specs=[row, row, row], out_specs=[row, row],
        out_shape=[S((T, D), bf16), S((T, D), bf16)],
    )(dh, gate, e)


def _cast_layer(name, w_all, layer):
    _, R, C = w_all.shape
    lanes = -(-C // 128) * 128
    tr = _tile(R, max(16, (1024 * 1024) // lanes // 16 * 16), 16)

    def body(w_ref, o_ref):
        o_ref[...] = w_ref[...].astype(bf16)

    return pl.pallas_call(
        body, name=name, grid=(R // tr,), in_specs=[BS((None, tr, C), lambda r: (layer, r, 0))],
        out_specs=BS((tr, C), lambda r: (r, 0)), out_shape=S((R, C), bf16),
    )(w_all)


def _win_assemble(name, g_win, A, H):
    _, D, DP4 = g_win.shape
    NM = N_CHIPS * DP4 - H
    tr = _tile(D, 256, 16)

    def body(g_ref, m_ref, f_ref):
        full = jnp.concatenate([g_ref[j] for j in range(N_CHIPS)], axis=1)
        m_ref[...] = jnp.concatenate([full[:, :3 * A], full[:, 3 * A + H:]], axis=1)
        f_ref[...] = jnp.concatenate([full[:, 3 * A:3 * A + H], jnp.zeros((tr, HEAD - H), bf16)], axis=1)

    return pl.pallas_call(
        body, name=name, grid=(D // tr,), in_specs=[BS((N_CHIPS, tr, DP4), lambda i: (0, i, 0))],
        out_specs=[BS((tr, NM), lambda i: (i, 0)), BS((tr, HEAD), lambda i: (i, 0))],
        out_shape=[S((D, NM), bf16), S((D, HEAD), bf16)],
    )(g_win)


def _dwin_split(name, d_wmain, d_wf, A, H):
    D, NM = d_wmain.shape
    DP4 = (NM + H) // N_CHIPS
    tr = _tile(D, 256, 8)

    def body(m_ref, f_ref, o_ref):
        m = m_ref[...]
        full = jnp.concatenate([m[:, :3 * A], f_ref[:, :H], m[:, 3 * A:]], axis=1)
        for j in range(N_CHIPS):
            o_ref[j] = full[:, j * DP4:(j + 1) * DP4]

    return pl.pallas_call(
        body, name=name, grid=(D // tr,),
        in_specs=[BS((tr, NM), lambda i: (i, 0)), BS((tr, HEAD), lambda i: (i, 0))],
        out_specs=BS((N_CHIPS, tr, DP4), lambda i: (0, i, 0)), out_shape=S((N_CHIPS, D, DP4), f32),
        compiler_params=_params(),
    )(d_wmain, d_wf)


def _gelu_and_grad(x):
    k0, k1 = 0.7978845608028654, 0.044715
    th = jnp.tanh(k0 * (x + k1 * x * x * x))
    val = 0.5 * x * (1.0 + th)
    grad = 0.5 * (1.0 + th) + 0.5 * x * (1.0 - th * th) * (k0 * (1.0 + 3.0 * k1 * x * x))
    return val, grad


def _fgate_fwd(name, pf, fb):
    T = pf.shape[0]

    def body(pf_ref, fb_ref, c_ref, ct_ref):
        xv = jax.nn.log_sigmoid(pf_ref[...] + fb_ref[...])
        row = lax.broadcasted_iota(jnp.int32, xv.shape, 0)
        s = 1
        while s < T:
            xv = xv + jnp.where(row >= s, pltpu.roll(xv, s, 0), 0.0)
            s *= 2
        c_ref[...] = xv
        ct_ref[...] = xv.T

    return pl.pallas_call(body, name=name, out_shape=[S((T, HEAD), f32), S((HEAD, T), f32)])(pf, fb)


def _fgate_bwd(name, dct, pf, fb):
    T = pf.shape[0]

    def body(dct_ref, pf_ref, fb_ref, dpf_ref, dfb_ref):
        xv = dct_ref[...].T
        row = lax.broadcasted_iota(jnp.int32, xv.shape, 0)
        s = 1
        while s < T:
            xv = xv + jnp.where(row + s < T, pltpu.roll(xv, T - s, 0), 0.0)
            s *= 2
        df = xv * jax.nn.sigmoid(-(pf_ref[...] + fb_ref[...]))
        dpf_ref[...] = df.astype(bf16)
        dfb_ref[...] = jnp.sum(df, axis=0, keepdims=True)

    return pl.pallas_call(body, name=name, out_shape=[S((T, HEAD), bf16), S((1, HEAD), f32)])(dct, pf, fb)


def _qk_norm(name, P, qg, kg, A):
    T = P.shape[0]
    tr = _tile(T, 256, 16)
    n_heads = A // HEAD

    def body(q_ref, k_ref, v_ref, qg_ref, kg_ref, qn_ref, kn_ref, vb_ref):
        for h in range(n_heads):
            sl = slice(h * HEAD, (h + 1) * HEAD)
            for src, g_ref, dst in ((q_ref, qg_ref, qn_ref), (k_ref, kg_ref, kn_ref)):
                xv = src[:, sl]
                r = lax.rsqrt(jnp.mean(xv * xv, axis=-1, keepdims=True) + EPS)
                dst[:, sl] = (xv * r * g_ref[...]).astype(bf16)
        vb_ref[...] = v_ref[...].astype(bf16)

    vec = BS((1, HEAD), lambda i: (0, 0))
    out = BS((tr, A), lambda i: (i, 0))
    return pl.pallas_call(
        body, name=name, grid=(T // tr,),
        in_specs=[BS((tr, A), lambda i: (i, 0)), BS((tr, A), lambda i: (i, 1)), BS((tr, A), lambda i: (i, 2)), vec, vec],
        out_specs=[out, out, out], out_shape=[S((T, A), bf16)] * 3,
    )(P, P, P, qg, kg)


def _attn_fwd(name, qn, kn, vb, c_col, c_row, tb, mix_width):
    T, A = qn.shape
    H = A // HEAD
    nb = T // tb
    scale = HEAD ** -0.5
    hp = _heads_per_program(H, 4)
    wide = hp * HEAD

    def body(q_ref, k_ref, v_ref, cq_ref, ck_ref, o_ref, o32_ref, lse_ref):
        i = pl.program_id(1)
        below = lax.broadcasted_iota(jnp.int32, (tb, tb), 0) >= lax.broadcasted_iota(jnp.int32, (tb, tb), 1)

        def block(j, carry, diagonal):
            koff = pl.multiple_of(j * tb, tb)
            out = []
            for hh in range(hp):
                m, l, acc = carry[hh]
                sl = slice(hh * HEAD, (hh + 1) * HEAD)
                k = k_ref[pl.ds(koff, tb), sl]
                v = v_ref[pl.ds(koff, tb), sl]
                s = _dot(q_ref[:, sl], k, "nt") * scale + (cq_ref[hh] - ck_ref[hh, j])
                if diagonal:
                    s = jnp.where(below, s, NEG)
                m_new = jnp.maximum(m, jnp.max(s, axis=-1, keepdims=True))
                alpha = jnp.exp(m - m_new)
                p = jnp.exp(s - m_new)
                l = l * alpha + jnp.sum(p, axis=-1, keepdims=True)
                acc = acc * alpha + _dot(p, v, "nn")
                out.append((m_new, l, acc))
            return tuple(out)

        init = tuple((jnp.full((tb, 1), NEG, f32), jnp.zeros((tb, 1), f32), jnp.zeros((tb, HEAD), f32))
                     for _ in range(hp))
        carry = lax.fori_loop(0, i, lambda j, c: block(j, c, False), init)
        carry = block(i, carry, True)
        for hh in range(hp):
            m, l, acc = carry[hh]
            sl = slice(hh * HEAD, (hh + 1) * HEAD)
            o = acc / l
            o_ref[:, sl] = o.astype(bf16)
            o32_ref[:, sl] = o
            lse_ref[hh] = m + jnp.log(l)

    return pl.pallas_call(
        body, name=name, grid=(H // hp, nb),
        in_specs=[BS((tb, wide), lambda h, i: (i, h)), BS((T, wide), lambda h, i: (0, h)),
                  BS((T, wide), lambda h, i: (0, h)), BS((hp, tb, 1), lambda h, i: (h, i, 0)),
                  BS((hp, nb, 1, tb), lambda h, i: (h, 0, 0, 0))],
        out_specs=[BS((tb, wide), lambda h, i: (i, h)), BS((tb, wide), lambda h, i: (i, h)),
                   BS((hp, tb, 1), lambda h, i: (h, i, 0))],
        out_shape=[S((T, mix_width), bf16), S((T, A), f32), S((H, T, 1), f32)],
    )(qn, kn, vb, c_col, c_row)


def _attn_bwd(name, qn, kn, vb, o, dmix, lse, c_col, c_row, P, qg, kg, tb):
    T, A = qn.shape
    H = A // HEAD
    nb = T // tb
    scale = HEAD ** -0.5
    hp = _heads_per_program(H)
    wide = hp * HEAD

    def body(q_ref, k_ref, v_ref, o_ref, do_ref, lse_ref, cq_ref, ck_ref, qraw_ref, kraw_ref, qg_ref, kg_ref,
             dq_out, dk_out, dv_out, dc_out, dqg_out, dkg_out, dq_acc, dk_acc, delta_s):
        h = pl.program_id(0)
        dq_acc[...] = jnp.zeros_like(dq_acc)
        below = lax.broadcasted_iota(jnp.int32, (tb, tb), 0) >= lax.broadcasted_iota(jnp.int32, (tb, tb), 1)
        for hh in range(hp):
            sl = slice(hh * HEAD, (hh + 1) * HEAD)
            delta_s[hh] = jnp.sum(do_ref[:, sl].astype(bf16).astype(f32) * o_ref[:, sl], axis=-1, keepdims=True)

        def kblock(j, _):
            koff = pl.multiple_of(j * tb, tb)

            def products(i, hh):
                sl = slice(hh * HEAD, (hh + 1) * HEAD)
                qoff = pl.multiple_of(i * tb, tb)
                return (_dot(q_ref[pl.ds(qoff, tb), sl], k_ref[pl.ds(koff, tb), sl], "nt"),
                        _dot(do_ref[pl.ds(qoff, tb), sl], v_ref[pl.ds(koff, tb), sl], "nt"))

            def qblock(i, carry, diagonal):
                qoff = pl.multiple_of(i * tb, tb)
                out = []
                for hh in range(hp):
                    dk, dv, dc, qk_i, dp = carry[hh]
                    sl = slice(hh * HEAD, (hh + 1) * HEAD)
                    ahead = products(jnp.minimum(i + 1, nb - 1), hh)
                    k = k_ref[pl.ds(koff, tb), sl]
                    q = q_ref[pl.ds(qoff, tb), sl]
                    do = do_ref[pl.ds(qoff, tb), sl].astype(bf16)
                    s = qk_i * scale + (cq_ref[hh, pl.ds(qoff, tb), :] - ck_ref[hh, j])
                    if diagonal:
                        s = jnp.where(below, s, NEG)
                    p = jnp.exp(s - lse_ref[hh, pl.ds(qoff, tb), :])
                    dv = dv + _dot(p, do, "tn")
                    ds = p * (dp - delta_s[hh, pl.ds(qoff, tb), :])
                    dc = dc - jnp.sum(ds, axis=0, keepdims=True)
                    dsb = (ds * scale).astype(bf16)
                    dk = dk + _dot(dsb, q, "tn")
                    dq_acc[pl.ds(qoff, tb), sl] += _dot(dsb, k, "nn")
                    out.append((dk, dv, dc, *ahead))
                return tuple(out)

            init = tuple((jnp.zeros((tb, HEAD), f32), jnp.zeros((tb, HEAD), f32), jnp.zeros((1, tb), f32),
                          *products(j, hh)) for hh in range(hp))
            carry = qblock(j, init, True)
            carry = lax.fori_loop(j + 1, nb, lambda i, c: qblock(i, c, False), carry)
            for hh in range(hp):
                dk, dv, dc = carry[hh][:3]
                sl = slice(hh * HEAD, (hh + 1) * HEAD)
                dk_acc[pl.ds(koff, tb), sl] = dk
                dv_out[pl.ds(koff, tb), sl] = dv.astype(bf16)
                dc_out[hh, j] = dc
            return 0

        lax.fori_loop(0, nb, kblock, 0)

        for raw_ref, g_ref, acc_ref, d_out, dg_out in ((qraw_ref, qg_ref, dq_acc, dq_out, dqg_out),
                                                       (kraw_ref, kg_ref, dk_acc, dk_out, dkg_out)):
            part = jnp.zeros((1, HEAD), f32)
            for hh in range(hp):
                sl = slice(hh * HEAD, (hh + 1) * HEAD)
                xv = raw_ref[:, sl]
                r = lax.rsqrt(jnp.mean(xv * xv, axis=-1, keepdims=True) + EPS)
                xh = xv * r
                dn = acc_ref[:, sl]
                dxh = dn * g_ref[...]
                d_out[:, sl] = (r * (dxh - xh * jnp.mean(dxh * xh, axis=-1, keepdims=True))).astype(bf16)
                part = part + jnp.sum(dn * xh, axis=0, keepdims=True)

            @pl.when(h == 0)
            def _():
                dg_out[...] = part

            @pl.when(h > 0)
            def _():
                dg_out[...] += part

    heads = lambda off: BS((T, wide), lambda h: (0, off + h))
    col = BS((hp, T, 1), lambda h: (h, 0, 0))
    row = BS((hp, nb, 1, tb), lambda h: (h, 0, 0, 0))
    vec = BS((1, HEAD), lambda h: (0, 0))
    return pl.pallas_call(
        body, name=name, grid=(H // hp,),
        in_specs=[heads(0), heads(0), heads(0), heads(0), heads(0), col, col, row, heads(0), heads(H // hp), vec, vec],
        out_specs=[heads(0), heads(0), heads(0), row, vec, vec],
        out_shape=[S((T, A), bf16)] * 3 + [S((H, nb, 1, tb), f32), S((1, HEAD), f32), S((1, HEAD), f32)],
        scratch_shapes=[pltpu.VMEM((T, wide), f32), pltpu.VMEM((T, wide), f32), pltpu.VMEM((hp, T, 1), f32)],
        compiler_params=_params(),
    )(qn, kn, vb, o, dmix, lse, c_col, c_row, P, P, qg, kg)


def _gmlp_fwd(name, P, mix, gain, ws, b, col_u, col_v, col_y, Wd):
    T = P.shape[0]
    G = Wd // HEAD
    tr = _tile(T, 512, HEAD)

    def body(u_ref, v_ref, gain_ref, ws_ref, b_ref, mix_ref, y_ref):
        tril = lax.broadcasted_iota(jnp.int32, (HEAD, HEAD), 0) >= lax.broadcasted_iota(jnp.int32, (HEAD, HEAD), 1)
        wm = jnp.where(tril, ws_ref[...], 0.0).astype(bf16)
        for n in range(tr // HEAD):
            rows = slice(n * HEAD, (n + 1) * HEAD)
            u = jax.nn.gelu(u_ref[rows, :])
            a = jax.nn.gelu(v_ref[rows, :])
            r = lax.rsqrt(jnp.mean(a * a, axis=-1, keepdims=True) + EPS)
            vn = a * r * gain_ref[...]
            mixed = _dot(wm, vn, "nn") + b_ref[...]
            y_ref[rows, :] = (u * mixed).astype(bf16)

    return pl.pallas_call(
        body, name=name, grid=(G, T // tr),
        in_specs=[BS((tr, HEAD), lambda g, i: (i, col_u + g)), BS((tr, HEAD), lambda g, i: (i, col_v + g)),
                  BS((None, 1, HEAD), lambda g, i: (g, 0, 0)), BS((None, HEAD, HEAD), lambda g, i: (g, 0, 0)),
                  BS((None, HEAD, 1), lambda g, i: (g, 0, 0)), ANY],
        out_specs=BS((tr, HEAD), lambda g, i: (i, col_y + g)), out_shape=S(mix.shape, bf16),
        input_output_aliases={5: 0},
    )(P, P, gain, ws, b, mix)


def _gmlp_bwd(name, P, dmix, gain, ws, b, col_u, col_v, col_dy, Wd):
    T = P.shape[0]
    G = Wd // HEAD
    tr = _tile(T, 512, HEAD)

    def body(u_ref, v_ref, dy_ref, gain_ref, ws_ref, b_ref, du_ref, dv_ref, dws_ref, db_ref, dgain_ref):
        i = pl.program_id(1)
        tril = lax.broadcasted_iota(jnp.int32, (HEAD, HEAD), 0) >= lax.broadcasted_iota(jnp.int32, (HEAD, HEAD), 1)
        wm = jnp.where(tril, ws_ref[...], 0.0).astype(bf16)
        gain_v = gain_ref[...]
        dw = jnp.zeros((HEAD, HEAD), f32)
        db = jnp.zeros((HEAD, 1), f32)
        dgain = jnp.zeros((1, HEAD), f32)
        for n in range(tr // HEAD):
            rows = slice(n * HEAD, (n + 1) * HEAD)
            u, du_dx = _gelu_and_grad(u_ref[rows, :])
            a, da_dx = _gelu_and_grad(v_ref[rows, :])
            dy = dy_ref[rows, :]
            r = lax.rsqrt(jnp.mean(a * a, axis=-1, keepdims=True) + EPS)
            ah = a * r
            vnb = (ah * gain_v).astype(bf16)
            mixed = _dot(wm, vnb, "nn") + b_ref[...]
            dm = dy * u
            dmb = dm.astype(bf16)
            du_ref[rows, :] = (dy * mixed * du_dx).astype(bf16)
            db = db + jnp.sum(dm, axis=1, keepdims=True)
            dw = dw + _dot(dmb, vnb, "nt")
            dvn = _dot(wm, dmb, "tn")
            dgain = dgain + jnp.sum(dvn * ah, axis=0, keepdims=True)
            dah = dvn * gain_v
            da = r * (dah - ah * jnp.mean(dah * ah, axis=-1, keepdims=True))
            dv_ref[rows, :] = (da * da_dx).astype(bf16)
        dw = jnp.where(tril, dw, 0.0)

        @pl.when(i == 0)
        def _():
            dws_ref[...] = dw
            db_ref[...] = db
            dgain_ref[...] = dgain

        @pl.when(i > 0)
        def _():
            dws_ref[...] += dw
            db_ref[...] += db
            dgain_ref[...] += dgain

    out = BS((tr, HEAD), lambda g, i: (i, g))
    return pl.pallas_call(
        body, name=name, grid=(G, T // tr),
        in_specs=[BS((tr, HEAD), lambda g, i: (i, col_u + g)), BS((tr, HEAD), lambda g, i: (i, col_v + g)),
                  BS((tr, HEAD), lambda g, i: (i, col_dy + g)),
                  BS((None, 1, HEAD), lambda g, i: (g, 0, 0)), BS((None, HEAD, HEAD), lambda g, i: (g, 0, 0)),
                  BS((None, HEAD, 1), lambda g, i: (g, 0, 0))],
        out_specs=[out, out, BS((None, HEAD, HEAD), lambda g, i: (g, 0, 0)), BS((None, HEAD, 1), lambda g, i: (g, 0, 0)),
                   BS((None, 1, HEAD), lambda g, i: (g, 0, 0))],
        out_shape=[S((T, Wd), bf16), S((T, Wd), bf16), S((G, HEAD, HEAD), f32), S((G, HEAD, 1), f32),
                   S((G, 1, HEAD), f32)],
    )(P, P, dmix, gain, ws, b)


def _trailing_window(xv, w, row):
    k = 1
    while k < w:
        xv = xv + jnp.where(row >= k, pltpu.roll(xv, k, 0), 0.0)
        k *= 2
    return xv


def _leading_window(xv, w, row, T):
    k = 1
    while k < w:
        xv = xv + jnp.where(row + k < T, pltpu.roll(xv, T - k, 0), 0.0)
        k *= 2
    return xv


def _pool_fwd(name, P, mix, pw, ps, col_x, col_y, Wd):
    T = P.shape[0]
    Gp = Wd // HEAD

    def body(x_ref, pw_ref, ps_ref, mix_ref, y_ref):
        row = lax.broadcasted_iota(jnp.int32, (T, HEAD), 0)
        for g in range(Gp):
            w = POOL_WINDOWS[g]
            sl = slice(g * HEAD, (g + 1) * HEAD)
            xv = x_ref[:, sl]
            cnt = jnp.minimum(row + 1, w).astype(f32)
            d = _trailing_window(xv, w, row) / cnt - xv
            y_ref[:, sl] = (_dot(d, pw_ref[g], "nn") * ps_ref[:, sl]).astype(bf16)

    return pl.pallas_call(
        body, name=name, grid=(1,),
        in_specs=[BS((T, Wd), lambda i: (0, col_x)), BS((Gp, HEAD, HEAD), lambda i: (0, 0, 0)), BS((1, Wd), lambda i: (0, 0)),
                  ANY],
        out_specs=BS((T, Wd), lambda i: (0, col_y)), out_shape=S(mix.shape, bf16), input_output_aliases={3: 0},
        compiler_params=_params(),
    )(P, pw, ps, mix)


def _pool_bwd(name, P, dmix, pw, ps, col_x, col_dy, Wd):
    T = P.shape[0]
    Gp = Wd // HEAD

    def body(x_ref, dy_ref, pw_ref, ps_ref, dx_ref, dpw_ref, dps_ref):
        row = lax.broadcasted_iota(jnp.int32, (T, HEAD), 0)
        for g in range(Gp):
            w = POOL_WINDOWS[g]
            sl = slice(g * HEAD, (g + 1) * HEAD)
            xv = x_ref[:, sl]
            cnt = jnp.minimum(row + 1, w).astype(f32)
            d = (_trailing_window(xv, w, row) / cnt - xv).astype(bf16)
            pwb = pw_ref[g].astype(bf16)
            z = _dot(d, pwb, "nn")
            dy = dy_ref[:, sl]
            dps_ref[:, sl] = jnp.sum(dy * z, axis=0, keepdims=True)
            dzb = (dy * ps_ref[:, sl]).astype(bf16)
            dpw_ref[g] = _dot(d, dzb, "tn")
            dd = _dot(dzb, pwb, "nt")
            dx_ref[:, sl] = (_leading_window(dd / cnt, w, row, T) - dd).astype(bf16)

    return pl.pallas_call(
        body, name=name, grid=(1,),
        in_specs=[BS((T, Wd), lambda i: (0, col_x)), BS((T, Wd), lambda i: (0, col_dy)),
                  BS((Gp, HEAD, HEAD), lambda i: (0, 0, 0)), BS((1, Wd), lambda i: (0, 0))],
        out_specs=[BS((T, Wd), lambda i: (0, 0)), BS((Gp, HEAD, HEAD), lambda i: (0, 0, 0)), BS((1, Wd), lambda i: (0, 0))],
        out_shape=[S((T, Wd), bf16), S((Gp, HEAD, HEAD), f32), S((1, Wd), f32)], compiler_params=_params(),
    )(P, dmix, pw, ps)


def _adamw(name, w, g, m, v):
    R, C = w.shape
    lanes = -(-C // 128) * 128
    tr = _tile(R, max(8, (512 * 1024) // lanes // 8 * 8), 8)
    c1 = 1.0 - ADAM_B1 ** ADAM_STEP
    c2 = 1.0 - ADAM_B2 ** ADAM_STEP

    def body(w_ref, g_ref, m_ref, v_ref, d_ref, nm_ref, nv_ref):
        gv = g_ref[...]
        nm = ADAM_B1 * m_ref[...] + (1.0 - ADAM_B1) * gv
        nv = ADAM_B2 * v_ref[...] + (1.0 - ADAM_B2) * (gv * gv)
        d_ref[...] = -ADAM_LR * ((nm / c1) / (jnp.sqrt(nv / c2) + ADAM_EPS) + ADAM_WD * w_ref[...])
        nm_ref[...] = nm
        nv_ref[...] = nv

    blk = BS((tr, C), lambda i: (i, 0))
    return pl.pallas_call(
        body, name=name, grid=(R // tr,), in_specs=[blk] * 4, out_specs=[blk] * 3, out_shape=[S((R, C), f32)] * 3,
    )(w, g, m, v)


def _adamw_layer(name, layer, w_all, m_all, v_all, g, prev):
    L, R, C = w_all.shape
    lanes = -(-C // 128) * 128
    tr = _tile(R, max(8, (512 * 1024) // lanes // 8 * 8), 8)
    c1 = 1.0 - ADAM_B1 ** ADAM_STEP
    c2 = 1.0 - ADAM_B2 ** ADAM_STEP
    n_prev = 0 if prev is None else 4

    def body(w_ref, m_ref, v_ref, g_ref, *rest):
        go_ref, d_ref, nm_ref, nv_ref = rest[n_prev:]
        gv = g_ref[...]
        nm = ADAM_B1 * m_ref[...] + (1.0 - ADAM_B1) * gv
        nv = ADAM_B2 * v_ref[...] + (1.0 - ADAM_B2) * (gv * gv)
        d_ref[...] = -ADAM_LR * ((nm / c1) / (jnp.sqrt(nv / c2) + ADAM_EPS) + ADAM_WD * w_ref[...])
        nm_ref[...] = nm
        nv_ref[...] = nv
        go_ref[...] = gv

    slab = BS((None, tr, C), lambda r: (layer, r, 0))
    return pl.pallas_call(
        body, name=name, grid=(R // tr,),
        in_specs=[slab, slab, slab, BS((tr, C), lambda r: (r, 0))] + [ANY] * n_prev,
        out_specs=[slab] * 4, out_shape=[S((L, R, C), f32)] * 4,
        input_output_aliases={4 + k: k for k in range(n_prev)},
    )(w_all, m_all, v_all, g, *(prev or ()))


def _chip_of(k):
    return k // 2, k % 2


def _remote(src, dst, send_sems, recv_sems, idx, dev):
    return pltpu.make_async_remote_copy(src_ref=src, dst_ref=dst, send_sem=send_sems.at[idx], recv_sem=recv_sems.at[idx],
                                        device_id=dev, device_id_type=MESH)


def _plan_gather_near(n):
    def plan(refs, ss, rs, base):
        ins, lands = refs[:n], refs[n:]
        x, y, c, j0 = _my_place()
        sib = (x, y, 1 - c)
        sends, recvs = [], []
        for a in range(n):
            half = ins[a].shape[0] // 2
            lo = c * half
            sends.append(_remote(ins[a], lands[a].at[j0], ss, rs, base + 3 * a + 2, sib))
            recvs.append(_remote(lands[a].at[j0], lands[a].at[j0], ss, rs, base + 3 * a + 2, sib))
            for r in (1, 2):
                k = j0 ^ r
                dev = (*_chip_of(k), c)
                sends.append(_remote(ins[a].at[pl.ds(lo, half)], lands[a].at[j0, pl.ds(lo, half)], ss, rs,
                                     base + 3 * a + r - 1, dev))
                landed = lands[a].at[k, pl.ds(lo, half)]
                recvs.append(_remote(landed, landed, ss, rs, base + 3 * a + r - 1, dev))
        return sends, recvs
    return plan, 3 * n


def _plan_gather_relay(n):
    def plan(refs, ss, rs, base):
        x, y, c, j0 = _my_place()
        sib = (x, y, 1 - c)
        sends, recvs = [], []
        for a in range(n):
            half = refs[a].shape[1] // 2
            quarter = half // 2
            lo = c * half
            far = j0 ^ 3
            for r, to, off in ((1, 2, 0), (2, 1, quarter)):
                dev = (*_chip_of(j0 ^ to), c)
                piece = refs[a].at[j0 ^ r, pl.ds(lo + off, quarter)]
                sends.append(_remote(piece, piece, ss, rs, base + 4 * a + to - 1, dev))
                lands_here = refs[a].at[far, pl.ds(lo + off, quarter)]
                recvs.append(_remote(lands_here, lands_here, ss, rs, base + 4 * a + to - 1, dev))
                mine = refs[a].at[j0 ^ r, pl.ds(lo, half)]
                theirs = refs[a].at[j0 ^ r, pl.ds((1 - c) * half, half)]
                sends.append(_remote(mine, mine, ss, rs, base + 4 * a + 1 + r, sib))
                recvs.append(_remote(theirs, theirs, ss, rs, base + 4 * a + 1 + r, sib))
        return sends, recvs
    return plan, 4 * n


def _plan_gather_far(n):
    def plan(refs, ss, rs, base):
        x, y, c, j0 = _my_place()
        sib = (x, y, 1 - c)
        sends, recvs = [], []
        for a in range(n):
            half = refs[a].shape[1] // 2
            mine = refs[a].at[j0 ^ 3, pl.ds(c * half, half)]
            theirs = refs[a].at[j0 ^ 3, pl.ds((1 - c) * half, half)]
            sends.append(_remote(mine, mine, ss, rs, base + a, sib))
            recvs.append(_remote(theirs, theirs, ss, rs, base + a, sib))
        return sends, recvs
    return plan, n


def _plan_sibling_halves(n):
    def plan(refs, ss, rs, base):
        ins, lands = refs[:n], refs[n:]
        x, y, c, _ = _my_place()
        sib = (x, y, 1 - c)
        sends, recvs = [], []
        for a in range(n):
            half = ins[a].shape[1] // 2
            sends.append(_remote(ins[a].at[:, pl.ds((1 - c) * half, half), :], lands[a], ss, rs, base + a, sib))
            recvs.append(_remote(lands[a], lands[a], ss, rs, base + a, sib))
        return sends, recvs
    return plan, n


def _plan_chip_scatter(n):
    def plan(refs, ss, rs, base):
        ins, lands = refs[:n], refs[n:]
        x, y, c, j0 = _my_place()
        sends, recvs = [], []
        for a in range(n):
            for r in (1, 2, 3):
                k = j0 ^ r
                dev = (*_chip_of(k), c)
                sends.append(_remote(ins[a].at[k], lands[a].at[j0], ss, rs, base + 3 * a + r - 1, dev))
                recvs.append(_remote(lands[a].at[k], lands[a].at[k], ss, rs, base + 3 * a + r - 1, dev))
        return sends, recvs
    return plan, 3 * n


def _plan_sibling_join(n):
    def plan(refs, ss, rs, base):
        x, y, c, _ = _my_place()
        sib = (x, y, 1 - c)
        sends, recvs = [], []
        for a in range(n):
            half = refs[a].shape[0] // 2
            mine = refs[a].at[pl.ds(c * half, half)]
            theirs = refs[a].at[pl.ds((1 - c) * half, half)]
            sends.append(_remote(mine, mine, ss, rs, base + a, sib))
            recvs.append(_remote(theirs, theirs, ss, rs, base + a, sib))
        return sends, recvs
    return plan, n


_HBM = pl.BlockSpec(memory_space=pltpu.HBM)
_SEM = pl.BlockSpec(memory_space=pltpu.SEMAPHORE)
_EFFECT = pltpu.SideEffectType.DATAFLOW_SIDE_EFFECTING


def _exchange_start(name, plan, bufs, after):
    plan_fn, n_sems = plan
    n = len(bufs)

    def body(*refs):
        ss, rs, token = refs[n + len(after)], refs[n + len(after) + 1], refs[-1]
        sends, _ = plan_fn(refs[:n], ss, rs, 0)
        for cp in sends:
            cp.start()
        token[...] = jnp.zeros_like(token)

    res = pl.pallas_call(
        body, name=name,
        out_shape=(pltpu.SemaphoreType.DMA((n_sems,)), pltpu.SemaphoreType.DMA((n_sems,)),
                   *[pltpu.HBM(b.shape, b.dtype) for b in bufs], S((8, 128), f32)),
        in_specs=[_HBM] * n + [ANY] * len(after),
        out_specs=(_SEM, _SEM, *[_HBM] * n, pl.BlockSpec(memory_space=pltpu.VMEM)),
        input_output_aliases={k: 2 + k for k in range(n)},
        compiler_params=pltpu.CompilerParams(has_side_effects=_EFFECT),
    )(*[pltpu.with_memory_space_constraint(b, pltpu.HBM) for b in bufs], *after)
    return res[0], res[1], list(res[2:2 + n]), res[-1]


def _exchange_wait(name, plan, send_sems, recv_sems, bufs, after):
    plan_fn, _ = plan
    n = len(bufs)

    def body(*refs):
        ss, rs, token = refs[n], refs[n + 1], refs[-1]
        sends, recvs = plan_fn(refs[:n], ss, rs, 0)
        for cp in recvs:
            cp.wait_recv()
        for cp in sends:
            cp.wait_send()
        token[...] = jnp.zeros_like(token)

    res = pl.pallas_call(
        body, name=name,
        out_shape=(*[pltpu.HBM(b.shape, b.dtype) for b in bufs], S((8, 128), f32)),
        in_specs=[_HBM] * n + [_SEM, _SEM] + [ANY] * len(after),
        out_specs=(*[_HBM] * n, pl.BlockSpec(memory_space=pltpu.VMEM)),
        input_output_aliases={k: k for k in range(n)},
        compiler_params=pltpu.CompilerParams(has_side_effects=_EFFECT),
    )(*bufs, send_sems, recv_sems, *after)
    return list(res[:n]), res[-1]


class _Order:
    def __init__(self, first):
        self.marker = first
        self.token = None

    def _after(self):
        return [self.marker] + ([] if self.token is None else [self.token])

    def start(self, name, plan, bufs):
        ss, rs, thru, self.token = _exchange_start(name, plan, bufs, self._after())
        return name, plan, ss, rs, thru

    def wait(self, handle):
        name, plan, ss, rs, thru = handle
        out, self.token = _exchange_wait(name + "_wait", plan, ss, rs, thru, self._after())
        return out

    def follows(self, small):
        return small if self.token is None else small + self.token[0, 0]

    def done(self, result):
        self.marker = result[(slice(0, 1),) * result.ndim].reshape(1, 1)


def _all_reduce_small(name, g8):
    _, R, L = g8.shape

    def body(g_ref, out_ref, land, red, send1, recv1, send2, recv2):
        x, y, c, _ = _my_place()
        me = 4 * x + 2 * y + c
        peers = []
        for r in range(1, N_DEV):
            q = me ^ r
            peers.append((q, (q // 4, (q // 2) % 2, q % 2)))
        first = []
        for r, (q, dev) in enumerate(peers):
            cp = pltpu.make_async_remote_copy(src_ref=g_ref.at[q], dst_ref=land.at[me], send_sem=send1.at[r],
                                              recv_sem=recv1.at[r], device_id=dev, device_id_type=MESH)
            cp.start()
            first.append(cp)
        land[me] = g_ref[me]
        for r, (q, dev) in enumerate(peers):
            pltpu.make_async_remote_copy(src_ref=land.at[q], dst_ref=land.at[q], send_sem=send1.at[r],
                                         recv_sem=recv1.at[r], device_id=dev, device_id_type=MESH).wait_recv()
        acc = land[0]
        for d in range(1, N_DEV):
            acc = acc + land[d]
        red[...] = acc
        out_ref[me] = acc
        second = []
        for r, (q, dev) in enumerate(peers):
            cp = pltpu.make_async_remote_copy(src_ref=red, dst_ref=out_ref.at[me], send_sem=send2.at[r],
                                              recv_sem=recv2.at[r], device_id=dev, device_id_type=MESH)
            cp.start()
            second.append(cp)
        for r, (q, dev) in enumerate(peers):
            pltpu.make_async_remote_copy(src_ref=out_ref.at[q], dst_ref=out_ref.at[q], send_sem=send2.at[r],
                                         recv_sem=recv2.at[r], device_id=dev, device_id_type=MESH).wait_recv()
        for cp in first + second:
            cp.wait_send()

    vm = pl.BlockSpec(memory_space=pltpu.VMEM)
    return pl.pallas_call(
        body, name=name, in_specs=[vm], out_specs=vm, out_shape=S(g8.shape, f32),
        scratch_shapes=[pltpu.VMEM((N_DEV, R, L), f32), pltpu.VMEM((R, L), f32)]
        + [pltpu.SemaphoreType.DMA((N_DEV - 1,))] * 4,
        compiler_params=_params(),
    )(g8)


def _pair_sum(name, g4, sib):
    _, rows, cols = g4.shape
    half = rows // 2
    lanes = -(-cols // 128) * 128
    tr = _tile(half, max(16, (512 * 1024) // lanes // 16 * 16), 16)
    nb = half // tr

    def body(g_ref, s_ref, pb_ref, own_ref):
        j = pl.program_id(1)
        t = g_ref[...] + s_ref[...]
        pb_ref[...] = t.astype(bf16)

        @pl.when(j == _my_place()[3])
        def _():
            own_ref[...] = t

    return pl.pallas_call(
        body, name=name, grid=(nb, N_CHIPS),
        in_specs=[BS((None, tr, cols), lambda i, j: (j, lax.axis_index("c") * nb + i, 0)),
                  BS((None, tr, cols), lambda i, j: (j, i, 0))],
        out_specs=[BS((None, tr, cols), lambda i, j: (j, i, 0)), BS((tr, cols), lambda i, j: (i, 0))],
        out_shape=[S((N_CHIPS, half, cols), bf16), S((half, cols), f32)],
    )(g4, sib)


def _chip_sum(name, own, got):
    half, cols = own.shape
    lanes = -(-cols // 128) * 128
    tr = _tile(half, max(16, (512 * 1024) // lanes // 16 * 16), 16)
    nb = half // tr

    def body(own_ref, *rest):
        got_refs, o_ref = rest[:N_CHIPS], rest[N_CHIPS]
        j0 = _my_place()[3]
        acc = None
        for k in range(N_CHIPS):
            t = jnp.where(j0 == k, own_ref[...], got_refs[k][...].astype(f32))
            acc = t if acc is None else acc + t
        o_ref[...] = acc

    def slot(k):
        return BS((None, tr, cols), lambda i: (jnp.where(_my_place()[3] == k, (k + 1) % N_CHIPS, k), i, 0))

    return pl.pallas_call(
        body, name=name, grid=(nb,),
        in_specs=[BS((tr, cols), lambda i: (i, 0))] + [slot(k) for k in range(N_CHIPS)],
        out_specs=BS((tr, cols), lambda i: (lax.axis_index("c") * nb + i, 0)),
        out_shape=S((2 * half, cols), f32),
    )(own, got, got, got, got)


def _rows_of(size):
    return -(-size // 1024) * 8


def _pack_rows(arrs, n_rows):
    parts = []
    for a in arrs:
        rows = _rows_of(a.size)
        if a.size % 128 == 0:
            part = a.astype(f32).reshape(-1, 128)
            part = jnp.pad(part, ((0, rows - part.shape[0]), (0, 0)))
        else:
            part = jnp.pad(a.reshape(-1).astype(f32), (0, rows * 128 - a.size)).reshape(rows, 128)
        parts.append(part)
    used = sum(p.shape[0] for p in parts)
    return jnp.concatenate(parts + [jnp.zeros((n_rows - used, 128), f32)], axis=0)


def _unpack_rows(packed, shapes):
    out, row = [], 0
    for shp in shapes:
        size = 1
        for d in shp:
            size *= d
        rows = packed[row:row + _rows_of(size)]
        out.append(rows[:size // 128].reshape(shp) if size % 128 == 0 else rows.reshape(-1)[:size].reshape(shp))
        row += _rows_of(size)
    return out


def kernel(x, p, norm_mix, w_in, q_norm, k_norm, forget_bias, gmlp_v_norm, gmlp_w_s, gmlp_b_s, pool_w, pool_scale, w_out, norm_ffn, w_ffn_gate, w_ffn_up, w_ffn_down, norm_ple, w_ple_gate, w_ple_proj, loss_target, m_norm_mix, m_w_in, m_q_norm, m_k_norm, m_forget_bias, m_gmlp_v_norm, m_gmlp_w_s, m_gmlp_b_s, m_pool_w, m_pool_scale, m_w_out, m_norm_ffn, m_w_ffn_gate, m_w_ffn_up, m_w_ffn_down, m_norm_ple, m_w_ple_gate, m_w_ple_proj, v_norm_mix, v_w_in, v_q_norm, v_k_norm, v_forget_bias, v_gmlp_v_norm, v_gmlp_w_s, v_gmlp_b_s, v_pool_w, v_pool_scale, v_w_out, v_norm_ffn, v_w_ffn_gate, v_w_ffn_up, v_w_ffn_down, v_norm_ple, v_w_ple_gate, v_w_ple_proj):
    W = dict(norm_mix=norm_mix, w_in=w_in, q_norm=q_norm, k_norm=k_norm, forget_bias=forget_bias,
             gmlp_v_norm=gmlp_v_norm, gmlp_w_s=gmlp_w_s, gmlp_b_s=gmlp_b_s, pool_w=pool_w, pool_scale=pool_scale,
             w_out=w_out, norm_ffn=norm_ffn, w_ffn_gate=w_ffn_gate, w_ffn_up=w_ffn_up, w_ffn_down=w_ffn_down,
             norm_ple=norm_ple, w_ple_gate=w_ple_gate, w_ple_proj=w_ple_proj)
    M = dict(norm_mix=m_norm_mix, w_in=m_w_in, q_norm=m_q_norm, k_norm=m_k_norm, forget_bias=m_forget_bias,
             gmlp_v_norm=m_gmlp_v_norm, gmlp_w_s=m_gmlp_w_s, gmlp_b_s=m_gmlp_b_s, pool_w=m_pool_w,
             pool_scale=m_pool_scale, w_out=m_w_out, norm_ffn=m_norm_ffn, w_ffn_gate=m_w_ffn_gate,
             w_ffn_up=m_w_ffn_up, w_ffn_down=m_w_ffn_down, norm_ple=m_norm_ple, w_ple_gate=m_w_ple_gate,
             w_ple_proj=m_w_ple_proj)
    V = dict(norm_mix=v_norm_mix, w_in=v_w_in, q_norm=v_q_norm, k_norm=v_k_norm, forget_bias=v_forget_bias,
             gmlp_v_norm=v_gmlp_v_norm, gmlp_w_s=v_gmlp_w_s, gmlp_b_s=v_gmlp_b_s, pool_w=v_pool_w,
             pool_scale=v_pool_scale, w_out=v_w_out, norm_ffn=v_norm_ffn, w_ffn_gate=v_w_ffn_gate,
             w_ffn_up=v_w_ffn_up, w_ffn_down=v_w_ffn_down, norm_ple=v_norm_ple, w_ple_gate=v_w_ple_gate,
             w_ple_proj=v_w_ple_proj)

    L = w_in.shape[0]
    _, T, D = x.shape
    A, Wd = D // 2, D // 4
    H = A // HEAD
    G = gmlp_w_s.shape[1]
    Gp = pool_w.shape[1]
    DP4 = w_in.shape[2]
    DP = N_CHIPS * DP4
    NM = 3 * A + 3 * Wd
    FS = w_ffn_gate.shape[2]
    FF = N_CHIPS * FS
    DS = D // N_CHIPS
    PL = p.shape[-1]
    assert Wd // G == HEAD and Wd // Gp == HEAD and DP == NM + H and H <= HEAD
    assert all(w & (w - 1) == 0 for w in POOL_WINDOWS[:Gp])
    tb = _tile(T, 256, HEAD)
    nb = T // tb
    tm = _tile(T, 512, 16)
    tmw = _tile(T, 1024, 16)
    tn = _tile(NM, 512, 128)
    tnd = _tile(D, 512, 128)
    tkd = _tile(D, 1024, 128)
    tnw = _tile(D, 1024, 128)
    col_gu, col_gv, col_xp = 3 * A // HEAD, (3 * A + Wd) // HEAD, (3 * A + 2 * Wd) // Wd
    col_dg, col_dp = A // HEAD, (A + Wd) // Wd

    order = _Order(x[0, :1, :1])

    def gather_near(i, names, tag=""):
        shards = [_cast_layer(f"cast_{n}_{i}", W[n], i) for n in names]
        lands = [lax.empty((N_CHIPS,) + s.shape, bf16) for s in shards]
        return names, order.start(f"ag_near_{i}{tag}", _plan_gather_near(len(names)), shards + lands)

    def gather_relay(i, near, tag=""):
        names, handle = near
        return names, order.start(f"ag_relay_{i}{tag}", _plan_gather_relay(len(names)), order.wait(handle)[len(names):])

    def gather_far(i, relayed, tag=""):
        names, handle = relayed
        return names, order.start(f"ag_far_{i}{tag}", _plan_gather_far(len(names)), order.wait(handle))

    win_count = [0]

    def gathered(far):
        names, handle = far
        g = dict(zip(names, order.wait(handle)))
        out = {}
        if "w_in" in g:
            out["w_main"], out["w_f"] = _win_assemble(f"w_in_cols_{win_count[0]}", g["w_in"], A, H)
            win_count[0] += 1
        if "w_out" in g:
            out["w_out"] = g["w_out"].reshape(D, D)
        if "w_ffn_gate" in g:
            out.update(w_gate=g["w_ffn_gate"], w_up=g["w_ffn_up"], w_down=g["w_ffn_down"].reshape(FF, D),
                       w_pg=g["w_ple_gate"].reshape(D, D), w_pp=g["w_ple_proj"])
        return out

    Wf = [None] * L
    relayed = gather_relay(0, gather_near(0, BIG[:2], "a"), "a")
    near_rest = gather_near(0, BIG[2:], "c")
    Wf[0] = gathered(gather_far(0, relayed, "a"))
    near = relayed = None

    h = x.reshape(T, D)
    pb16 = p.reshape(L, T, PL).astype(bf16)
    saved = []

    for i in range(L):
        w = Wf[i]
        sv = dict(h0=h)
        xn1 = _rms_fwd(f"rms1_{i}", h, order.follows(norm_mix[i]))
        (P,) = _matmul(f"proj_{i}", "nn", (T // tmw, NM // tn),
                       [((xn1, BS((tmw, D), lambda i, j: (i, 0))), (w["w_main"], BS((D, tn), lambda i, j: (0, j))))], [],
                       [((T, NM), f32, BS((tmw, tn), lambda i, j: (i, j)))])
        (Pf,) = _matmul(f"projf_{i}", "nn", (T // tm, 1),
                        [((xn1, BS((tm, D), lambda i, j: (i, 0))), (w["w_f"], BS((D, HEAD), lambda i, j: (0, 0))))], [],
                        [((T, HEAD), f32, BS((tm, HEAD), lambda i, j: (i, 0)))])
        fb = jnp.pad(forget_bias[i], (0, HEAD - H)).reshape(1, HEAD)
        cc, ct = _fgate_fwd(f"fgate_{i}", Pf, fb)
        c_col = ct[:H].reshape(H, T, 1)
        c_row = ct[:H].reshape(H, nb, 1, tb)
        qg, kg = q_norm[i].reshape(1, HEAD), k_norm[i].reshape(1, HEAD)
        qn, kn, vb = _qk_norm(f"qknorm_{i}", P, qg, kg, A)
        mix, o32, lse = _attn_fwd(f"attn_{i}", qn, kn, vb, c_col, c_row, tb, D)
        if i == 0:
            order.done(o32)
            relayed_rest = gather_relay(0, near_rest, "c")
            near = gather_near(1, BIG) if L > 1 else None
        gain = gmlp_v_norm[i].reshape(G, 1, HEAD)
        bs = gmlp_b_s[i].reshape(G, HEAD, 1)
        mix = _gmlp_fwd(f"gmlp_{i}", P, mix, order.follows(gain), gmlp_w_s[i], bs, col_gu, col_gv, col_dg, Wd)
        ps = pool_scale[i].reshape(1, Wd)
        mix = _pool_fwd(f"pool_{i}", P, mix, pool_w[i], ps, col_xp, col_dp, Wd)
        (h1,) = _matmul(f"out_{i}", "nn", (T // tmw, D // tnd),
                        [((mix, BS((tmw, D), lambda i, j: (i, 0))), (w["w_out"], BS((D, tnd), lambda i, j: (0, j))))],
                        [(h, BS((tmw, tnd), lambda i, j: (i, j)))],
                        [((T, D), f32, BS((tmw, tnd), lambda i, j: (i, j)))],
                        epilogue=lambda accs, ex: (accs[0] + ex[0],))
        xn2 = _rms_fwd(f"rms2_{i}", h1, norm_ffn[i])
        order.done(xn2)
        if i == 0:
            w.update(gathered(gather_far(0, relayed_rest, "c")))
        elif i + 1 < L:
            relayed = gather_relay(i + 1, near)
            near = gather_near(i + 2, BIG) if i + 2 < L else None

        def ffn_epi(accs, ex):
            g_, u_ = accs
            return g_, u_, g_ * jax.nn.sigmoid(g_) * u_

        ffo = BS((tm, FS), lambda j, i: (i, j))
        Gt, Ut, act = _matmul(f"ffn1_{i}", "nn", (N_CHIPS, T // tm),
                              [((xn2, BS((tm, D), lambda j, i: (i, 0))), (w["w_gate"], BS((None, D, FS), lambda j, i: (j, 0, 0)))),
                               ((xn2, BS((tm, D), lambda j, i: (i, 0))), (w["w_up"], BS((None, D, FS), lambda j, i: (j, 0, 0))))],
                              [], [((T, FF), bf16, ffo)] * 3, epilogue=ffn_epi, after=order.token)
        (h2,) = _matmul(f"ffn2_{i}", "nn", (T // tmw, D // tnd),
                        [((act, BS((tmw, FF), lambda i, j: (i, 0))), (w["w_down"], BS((FF, tnd), lambda i, j: (0, j))))],
                        [(h1, BS((tmw, tnd), lambda i, j: (i, j)))],
                        [((T, D), f32, BS((tmw, tnd), lambda i, j: (i, j)))],
                        epilogue=lambda accs, ex: (accs[0] + ex[0],))
        far = None
        if i == 0 and L > 1:
            order.done(h2)
            relayed = gather_relay(1, near)
            near = gather_near(2, BIG) if L > 2 else None
        elif i + 1 < L:
            order.done(h2)
            far = gather_far(i + 1, relayed)
        xn3 = _rms_fwd(f"rms3_{i}", h2, order.follows(norm_ple[i]))

        def ple_epi(accs, ex):
            gate = jax.nn.sigmoid(accs[0])
            return ex[0] + accs[1] * gate, gate, accs[1]

        dso = BS((tmw, DS), lambda i, j: (i, j))
        h3, gate, e = _matmul(f"ple_{i}", "nn", (T // tmw, N_CHIPS),
                              [((xn3, BS((tmw, D), lambda i, j: (i, 0))), (w["w_pg"], BS((D, DS), lambda i, j: (0, j)))),
                               ((pb16[i], BS((tmw, PL), lambda i, j: (i, 0))), (w["w_pp"], BS((None, PL, DS), lambda i, j: (j, 0, 0))))],
                              [(h2, dso)], [((T, D), f32, dso), ((T, D), bf16, dso), ((T, D), bf16, dso)], epilogue=ple_epi)
        sv.update(xn1=xn1, P=P, Pf=Pf, fb=fb, c_col=c_col, c_row=c_row, qn=qn, kn=kn, vb=vb, o32=o32, lse=lse,
                  mix=mix, h1=h1, xn2=xn2, Gt=Gt, Ut=Ut, act=act, h2=h2, xn3=xn3, gate=gate, e=e)
        saved.append(sv)
        h = h3
        order.done(h3)
        if i + 1 < L:
            Wf[i + 1] = gathered(far if far is not None else gather_far(i + 1, relayed))

    dh, loss_tile = _loss_grad("loss", h, loss_target.reshape(T, D))

    small_g = {n: [None] * L for n in SMALL}
    big_out = {}

    def stage_a(u):
        n_u = len(u["names"])
        lands = [lax.empty((N_CHIPS, g.shape[1] // 2, g.shape[2]), f32) for g in u["grads"]]
        u["a"] = order.start(f"rs_a_{u['tag']}", _plan_sibling_halves(n_u), u["grads"] + lands)

    def stage_pair(u):
        n_u = len(u["names"])
        out = order.wait(u["a"])
        pairs = [_pair_sum(f"rs_pair_{u['tag']}_{a}", out[a], out[n_u + a]) for a in range(n_u)]
        u["pb"], u["own"] = [t[0] for t in pairs], [t[1] for t in pairs]

    def stage_b(u):
        lands = [lax.empty(t.shape, bf16) for t in u["pb"]]
        u["b"] = order.start(f"rs_b_{u['tag']}", _plan_chip_scatter(len(u["names"])), u["pb"] + lands)

    def stage_sum(u):
        n_u = len(u["names"])
        out = order.wait(u["b"])
        u["sum"] = [_chip_sum(f"rs_sum_{u['tag']}_{a}", u["own"][a], out[n_u + a]) for a in range(n_u)]

    def stage_c(u):
        u["c"] = order.start(f"rs_c_{u['tag']}", _plan_sibling_join(len(u["names"])), u["sum"])

    def stage_adamw(u):
        for n, r in zip(u["names"], order.wait(u["c"])):
            big_out[n] = _adamw_layer(f"adamw_{n}_{u['layer']}", u["layer"], W[n], M[n], V[n],
                                      r.reshape(W[n].shape[1:]), big_out.get(n))
    dh1 = prev_f = prev_m = None
    for i in reversed(range(L)):
        w, sv = Wf[i], saved[i]
        if dh1 is not None:
            dh, _, dg = _rms_bwd(f"rms1_bw_{i + 1}", dxn1, saved[i + 1]["h0"], order.follows(norm_mix[i + 1]), dh1)
            small_g["norm_mix"][i + 1] = dg.reshape(D)
        de, dz = _ple_bwd_elem(f"ple_bw_{i}", dh, sv["gate"], sv["e"])
        (d_wpp,) = _matmul(f"d_wpp_{i}", "tn", (N_CHIPS, 1),
                           [((pb16[i], BS((T, PL), lambda i, j: (0, 0))), (de, BS((T, DS), lambda i, j: (0, i))))], [],
                           [((N_CHIPS, PL, DS), f32, BS((None, PL, DS), lambda i, j: (i, 0, 0)))])
        (d_wpg,) = _matmul(f"d_wpg_{i}", "tn", (D // tkd, D // tnd),
                           [((sv["xn3"], BS((T, tkd), lambda i, j: (0, i))), (dz, BS((T, tnd), lambda i, j: (0, j))))], [],
                           [((D, D), f32, BS((tkd, tnd), lambda i, j: (i, j)))])
        order.done(dz)
        if prev_m is not None:
            stage_pair(prev_m)
        (dxn3,) = _matmul(f"d_xn3_{i}", "nt", (T // tmw, D // tnd),
                          [((dz, BS((tmw, D), lambda i, j: (i, 0))), (w["w_pg"], BS((tnd, D), lambda i, j: (j, 0))))], [],
                          [((T, D), f32, BS((tmw, tnd), lambda i, j: (i, j)))], after=order.token)
        dh2, dh2b, dg = _rms_bwd(f"rms3_bw_{i}", dxn3, sv["h2"], norm_ple[i], dh)
        small_g["norm_ple"][i] = dg.reshape(D)

        def dffn_epi(accs, ex):
            da = accs[0]
            g_, u_ = ex[0].astype(f32), ex[1].astype(f32)
            sg = jax.nn.sigmoid(g_)
            return da * u_ * (sg * (1.0 + g_ * (1.0 - sg))), da * (g_ * sg)

        ffo = BS((tm, FS), lambda j, i: (i, j))
        dG, dU = _matmul(f"d_act_{i}", "nt", (N_CHIPS, T // tm),
                         [((dh2b, BS((tm, D), lambda j, i: (i, 0))), (w["w_down"], BS((FS, D), lambda j, i: (j, 0))))],
                         [(sv["Gt"], ffo), (sv["Ut"], ffo)], [((T, FF), bf16, ffo)] * 2, epilogue=dffn_epi)
        (d_wd,) = _matmul(f"d_wd_{i}", "tn", (N_CHIPS, D // tnw),
                          [((sv["act"], BS((T, FS), lambda i, j: (0, i))), (dh2b, BS((T, tnw), lambda i, j: (0, j))))], [],
                          [((FF, D), f32, BS((FS, tnw), lambda i, j: (i, j)))])
        gu_out = BS((None, tnd, FS), lambda j, i: (j, i, 0))
        d_wg, d_wu = _matmul(f"d_wgu_{i}", "tn", (N_CHIPS, D // tnd),
                             [((sv["xn2"], BS((T, tnd), lambda j, i: (0, i))), (dG, BS((T, FS), lambda j, i: (0, j)))),
                              ((sv["xn2"], BS((T, tnd), lambda j, i: (0, i))), (dU, BS((T, FS), lambda j, i: (0, j))))], [],
                             [((N_CHIPS, D, FS), f32, gu_out)] * 2, epilogue=lambda accs, ex: (accs[0], accs[1]))
        order.done(dG)
        unit_f = dict(tag=f"{i}f", layer=i, names=["w_ffn_gate", "w_ffn_up", "w_ffn_down", "w_ple_gate", "w_ple_proj"],
                      grads=[d_wg, d_wu, d_wd.reshape(N_CHIPS, FS, D), d_wpg.reshape(N_CHIPS, DS, D), d_wpp])
        stage_a(unit_f)
        if prev_f is not None:
            stage_sum(prev_f)
            stage_c(prev_f)
        if prev_m is not None:
            stage_b(prev_m)
        tm2 = _tile(T, 512, 16)
        (dxn2,) = _matmul(f"d_xn2_{i}", "nt", (D // tnd, T // tm2),
                          [((dG, BS((tm2, FF), lambda j, i: (i, 0))), (w["w_gate"], BS((N_CHIPS, tnd, FS), lambda j, i: (0, j, 0)))),
                           ((dU, BS((tm2, FF), lambda j, i: (i, 0))), (w["w_up"], BS((N_CHIPS, tnd, FS), lambda j, i: (0, j, 0))))], [],
                          [((T, D), f32, BS((tm2, tnd), lambda j, i: (i, j)))], after=order.token)
        dh1, dh1b, dg = _rms_bwd(f"rms2_bw_{i}", dxn2, sv["h1"], norm_ffn[i], dh2)
        small_g["norm_ffn"][i] = dg.reshape(D)
        (dmix,) = _matmul(f"d_mix_{i}", "nt", (T // tmw, D // tnd),
                          [((dh1b, BS((tmw, D), lambda i, j: (i, 0))), (w["w_out"], BS((tnd, D), lambda i, j: (j, 0))))], [],
                          [((T, D), f32, BS((tmw, tnd), lambda i, j: (i, j)))])
        (d_wout,) = _matmul(f"d_wout_{i}", "tn", (D // tkd, D // tnd),
                            [((sv["mix"], BS((T, tkd), lambda i, j: (0, i))), (dh1b, BS((T, tnd), lambda i, j: (0, j))))], [],
                            [((D, D), f32, BS((tkd, tnd), lambda i, j: (i, j)))])
        order.done(dmix)
        stage_pair(unit_f)
        stage_b(unit_f)
        if prev_f is not None:
            stage_adamw(prev_f)
        qg, kg = q_norm[i].reshape(1, HEAD), k_norm[i].reshape(1, HEAD)
        dq, dk, dv, dc_row, dqg, dkg = _attn_bwd(f"attn_bw_{i}", sv["qn"], sv["kn"], sv["vb"], sv["o32"], dmix,
                                                 sv["lse"], sv["c_col"], sv["c_row"], sv["P"], order.follows(qg), kg, tb)
        small_g["q_norm"][i] = dqg.reshape(HEAD)
        small_g["k_norm"][i] = dkg.reshape(HEAD)
        dct = jnp.pad(dc_row.reshape(H, T), ((0, HEAD - H), (0, 0)))
        dPf, dfb = _fgate_bwd(f"fgate_bw_{i}", dct, sv["Pf"], sv["fb"])
        small_g["forget_bias"][i] = dfb[0, :H]
        gain = gmlp_v_norm[i].reshape(G, 1, HEAD)
        bs = gmlp_b_s[i].reshape(G, HEAD, 1)
        dgu, dgv, dws, dbs, dgain = _gmlp_bwd(f"gmlp_bw_{i}", sv["P"], dmix, gain, gmlp_w_s[i], bs, col_gu, col_gv,
                                              col_dg, Wd)
        small_g["gmlp_w_s"][i] = dws
        small_g["gmlp_b_s"][i] = dbs.reshape(G, HEAD)
        small_g["gmlp_v_norm"][i] = dgain.reshape(G, HEAD)
        ps = pool_scale[i].reshape(1, Wd)
        dxp, dpw, dps = _pool_bwd(f"pool_bw_{i}", sv["P"], dmix, pool_w[i], ps, col_xp, col_dp, Wd)
        small_g["pool_w"][i] = dpw
        small_g["pool_scale"][i] = dps.reshape(Wd)
        dP = jnp.concatenate([dq, dk, dv, dgu, dgv, dxp], axis=1)
        (d_wmain,) = _matmul(f"d_wmain_{i}", "tn", (D // tkd, NM // tn),
                             [((sv["xn1"], BS((T, tkd), lambda i, j: (0, i))), (dP, BS((T, tn), lambda i, j: (0, j))))], [],
                             [((D, NM), f32, BS((tkd, tn), lambda i, j: (i, j)))])
        (d_wf,) = _matmul(f"d_wf_{i}", "tn", (D // tnd, 1),
                          [((sv["xn1"], BS((T, tnd), lambda i, j: (0, i))), (dPf, BS((T, HEAD), lambda i, j: (0, 0))))], [],
                          [((D, HEAD), f32, BS((tnd, HEAD), lambda i, j: (i, 0)))])
        d_win4 = _dwin_split(f"d_win_cols_{i}", d_wmain, d_wf, A, H)
        order.done(dP)
        unit_m = dict(tag=f"{i}m", layer=i, names=["w_in", "w_out"], grads=[d_win4, d_wout.reshape(N_CHIPS, DS, D)])
        stage_a(unit_m)
        if prev_m is not None:
            stage_sum(prev_m)
            stage_c(prev_m)
        (dxn1,) = _matmul(f"d_xn1_{i}", "nt", (D // tnd, T // tm),
                          [((dP, BS((tm, NM), lambda j, i: (i, 0))), (w["w_main"], BS((tnd, NM), lambda j, i: (j, 0)))),
                           ((dPf, BS((tm, HEAD), lambda j, i: (i, 0))), (w["w_f"], BS((tnd, HEAD), lambda j, i: (j, 0))))], [],
                          [((T, D), f32, BS((tm, tnd), lambda j, i: (i, j)))], after=order.token)
        order.done(dxn1)
        if prev_m is not None:
            stage_adamw(prev_m)
        prev_f, prev_m = unit_f, unit_m

    dh, _, dg = _rms_bwd("rms1_bw_0", dxn1, saved[0]["h0"], order.follows(norm_mix[0]), dh1)
    small_g["norm_mix"][0] = dg.reshape(D)
    order.done(dh)

    small_full = {n: jnp.stack(small_g[n]) for n in SMALL}
    small_shapes = [W[n].shape for n in SMALL]
    n_rows = sum(_rows_of(W[n].size) for n in SMALL) + _rows_of(1)
    rows8 = -(-n_rows // 64) * 8
    packed = order.follows(_pack_rows([small_full[n] for n in SMALL] + [loss_tile[0, :1]], N_DEV * rows8))
    summed = _all_reduce_small("allreduce_small", packed.reshape(N_DEV, rows8, 128)).reshape(-1, 128)
    order.done(summed)

    stage_pair(prev_m)
    stage_b(prev_m)
    stage_sum(prev_f)
    stage_c(prev_f)
    stage_adamw(prev_f)
    stage_sum(prev_m)
    stage_c(prev_m)
    stage_adamw(prev_m)
    *small_grads, loss_row = _unpack_rows(summed, small_shapes + [(1,)])
    grads = dict(zip(SMALL, small_grads))
    loss = loss_row[0]

    wp, mp, vp = (_pack_rows([t[n] for n in SMALL], N_DEV * rows8) for t in (W, M, V))
    delta, new_m, new_v = (dict(zip(SMALL, _unpack_rows(t, small_shapes)))
                           for t in _adamw("adamw_small", wp, summed, mp, vp))
    for n in BIG:
        grads[n], delta[n], new_m[n], new_v[n] = big_out[n]

    return (loss, dh.reshape(1, T, D), *[grads[n] for n in WEIGHTS], *[delta[n] for n in WEIGHTS],
            *[new_m[n] for n in WEIGHTS], *[new_v[n] for n in WEIGHTS])
```

```python
import jax
import jax.numpy as jnp
from jax import lax
from jax.experimental import pallas as pl
from jax.experimental.pallas import tpu as pltpu

f32, bf16 = jnp.float32, jnp.bfloat16
S = jax.ShapeDtypeStruct
BS = pl.BlockSpec
ANY = pl.BlockSpec(memory_space=pl.ANY)
MESH = pl.DeviceIdType.MESH

EPS = 1e-6
HEAD = 128
POOL_WINDOWS = (2, 4, 8, 16)
NEG = -1e30
N_CHIPS = 4
N_DEV = 8
VMEM_LIMIT = 56 * 1024 * 1024

ADAM_LR, ADAM_B1, ADAM_B2, ADAM_EPS, ADAM_WD, ADAM_STEP = 0.001, 0.9, 0.999, 1e-08, 0.01, 10

BIG = ("w_in", "w_out", "w_ffn_gate", "w_ffn_up", "w_ffn_down", "w_ple_gate", "w_ple_proj")
SMALL = ("norm_mix", "q_norm", "k_norm", "forget_bias", "gmlp_v_norm", "gmlp_w_s", "gmlp_b_s", "pool_w",
         "pool_scale", "norm_ffn", "norm_ple")
WEIGHTS = ("norm_mix", "w_in", "q_norm", "k_norm", "forget_bias", "gmlp_v_norm", "gmlp_w_s", "gmlp_b_s", "pool_w",
           "pool_scale", "w_out", "norm_ffn", "w_ffn_gate", "w_ffn_up", "w_ffn_down", "norm_ple", "w_ple_gate",
           "w_ple_proj")


def _tile(n, target, mult):
    best = None
    for t in range(mult, min(n, target) + 1, mult):
        if n % t == 0:
            best = t
    return best if best is not None else n


def _params(**kw):
    return pltpu.CompilerParams(vmem_limit_bytes=VMEM_LIMIT, **kw)


def _dot(a, b, kind):
    dims = {"nn": (((1,), (0,)), ((), ())), "nt": (((1,), (1,)), ((), ())), "tn": (((0,), (0,)), ((), ()))}[kind]
    return lax.dot_general(a.astype(bf16), b.astype(bf16), dims, preferred_element_type=f32)


def _heads_per_program(n_heads, want=2):
    while n_heads % want:
        want //= 2
    return want


def _my_place():
    x, y, c = lax.axis_index("x"), lax.axis_index("y"), lax.axis_index("c")
    return x, y, c, 2 * x + y


def _matmul(name, kind, grid, pairs, extras, outs, epilogue=None, after=None):
    n_p, n_e = len(pairs), len(extras)
    tokens = [] if after is None else [(after, BS((8, 128), lambda *_: (0, 0)))]

    def body(*refs):
        a_refs, b_refs = refs[:n_p], refs[n_p:2 * n_p]
        e_refs = refs[2 * n_p:2 * n_p + n_e]
        o_refs = refs[2 * n_p + n_e + len(tokens):]
        accs = []
        for a_ref, b_ref in zip(a_refs, b_refs):
            if len(b_ref.shape) == 3:
                w = b_ref.shape[2]
                acc = None
                for s in range(b_ref.shape[0]):
                    d = _dot(a_ref[:, s * w:(s + 1) * w], b_ref[s], kind)
                    acc = d if acc is None else acc + d
            else:
                acc = _dot(a_ref[...], b_ref[...], kind)
            accs.append(acc)
        if epilogue is None:
            res = accs[0]
            for t in accs[1:]:
                res = res + t
            res = (res,)
        else:
            res = epilogue(accs, [e[...] for e in e_refs])
        for o_ref, o in zip(o_refs, res):
            o_ref[...] = o.astype(o_ref.dtype)

    in_arrays = [p[0][0] for p in pairs] + [p[1][0] for p in pairs] + [e[0] for e in extras + tokens]
    in_specs = [p[0][1] for p in pairs] + [p[1][1] for p in pairs] + [e[1] for e in extras + tokens]
    res = pl.pallas_call(
        body, name=name, grid=grid, in_specs=in_specs,
        out_specs=[o[2] for o in outs], out_shape=[S(o[0], o[1]) for o in outs],
        compiler_params=_params(),
    )(*in_arrays)
    return res


def _rms_fwd(name, x, g):
    T, D = x.shape
    tr = _tile(T, 256, 8)

    def body(x_ref, g_ref, o_ref):
        xv = x_ref[...]
        r = lax.rsqrt(jnp.mean(xv * xv, axis=-1, keepdims=True) + EPS)
        o_ref[...] = (xv * r * g_ref[...]).astype(o_ref.dtype)

    return pl.pallas_call(
        body, name=name, grid=(T // tr,),
        in_specs=[BS((tr, D), lambda i: (i, 0)), BS((1, D), lambda i: (0, 0))],
        out_specs=BS((tr, D), lambda i: (i, 0)), out_shape=S((T, D), bf16),
    )(x, g.reshape(1, D))


def _rms_bwd(name, dxn, x, g, dres):
    T, D = x.shape
    tr = _tile(T, 256, 8)

    def body(dxn_ref, x_ref, g_ref, dres_ref, dx_ref, dxb_ref, dg_ref):
        i = pl.program_id(0)
        xv = x_ref[...]
        r = lax.rsqrt(jnp.mean(xv * xv, axis=-1, keepdims=True) + EPS)
        xh = xv * r
        dxn_v = dxn_ref[...]
        dxh = dxn_v * g_ref[...]
        dx = dres_ref[...] + r * (dxh - xh * jnp.mean(dxh * xh, axis=-1, keepdims=True))
        dx_ref[...] = dx
        dxb_ref[...] = dx.astype(bf16)
        part = jnp.sum(dxn_v * xh, axis=0, keepdims=True)

        @pl.when(i == 0)
        def _():
            dg_ref[...] = part

        @pl.when(i > 0)
        def _():
            dg_ref[...] += part

    row = BS((tr, D), lambda i: (i, 0))
    vec = BS((1, D), lambda i: (0, 0))
    return pl.pallas_call(
        body, name=name, grid=(T // tr,),
        in_specs=[row, row, vec, row], out_specs=[row, row, vec],
        out_shape=[S((T, D), f32), S((T, D), bf16), S((1, D), f32)],
    )(dxn, x, g.reshape(1, D), dres)


def _loss_grad(name, y, tgt):
    T, D = y.shape
    tr = _tile(T, 256, 8)

    def body(y_ref, t_ref, dy_ref, l_ref):
        i = pl.program_id(0)
        e = y_ref[...] - t_ref[...]
        dy_ref[...] = e * (1.0 / D)
        part = 0.5 * jnp.sum(jnp.mean(e * e, axis=-1, keepdims=True), axis=0, keepdims=True)

        @pl.when(i == 0)
        def _():
            l_ref[...] = jnp.zeros_like(l_ref)

        l_ref[...] += jnp.broadcast_to(part, l_ref.shape)

    row = BS((tr, D), lambda i: (i, 0))
    return pl.pallas_call(
        body, name=name, grid=(T // tr,), in_specs=[row, row],
        out_specs=[row, BS((8, 128), lambda i: (0, 0))],
        out_shape=[S((T, D), f32), S((8, 128), f32)],
    )(y, tgt)


def _ple_bwd_elem(name, dh, gate, e):
    T, D = dh.shape
    tr = _tile(T, 256, 16)

    def body(dh_ref, g_ref, e_ref, de_ref, dz_ref):
        d = dh_ref[...]
        g = g_ref[...].astype(f32)
        de_ref[...] = (d * g).astype(bf16)
        dz_ref[...] = (d * e_ref[...].astype(f32) * g * (1.0 - g)).astype(bf16)

    row = BS((tr, D), lambda i: (i, 0))
    return pl.pallas_call(
        body, name=name, grid=(T // tr,), in_specs=[row, row, row], out_specs=[row, row],
        out_shape=[S((T, D), bf16), S((T, D), bf16)],
    )(dh, gate, e)


def _cast_layer(name, w_all, layer):
    _, R, C = w_all.shape
    lanes = -(-C // 128) * 128
    tr = _tile(R, max(16, (1024 * 1024) // lanes // 16 * 16), 16)

    def body(w_ref, o_ref):
        o_ref[...] = w_ref[...].astype(bf16)

    return pl.pallas_call(
        body, name=name, grid=(R // tr,), in_specs=[BS((None, tr, C), lambda r: (layer, r, 0))],
        out_specs=BS((tr, C), lambda r: (r, 0)), out_shape=S((R, C), bf16),
    )(w_all)


def _win_assemble(name, g_win, A, H):
    _, D, DP4 = g_win.shape
    NM = N_CHIPS * DP4 - H
    tr = _tile(D, 256, 16)

    def body(g_ref, m_ref, f_ref):
        full = jnp.concatenate([g_ref[j] for j in range(N_CHIPS)], axis=1)
        m_ref[...] = jnp.concatenate([full[:, :3 * A], full[:, 3 * A + H:]], axis=1)
        f_ref[...] = jnp.concatenate([full[:, 3 * A:3 * A + H], jnp.zeros((tr, HEAD - H), bf16)], axis=1)

    return pl.pallas_call(
        body, name=name, grid=(D // tr,), in_specs=[BS((N_CHIPS, tr, DP4), lambda i: (0, i, 0))],
        out_specs=[BS((tr, NM), lambda i: (i, 0)), BS((tr, HEAD), lambda i: (i, 0))],
        out_shape=[S((D, NM), bf16), S((D, HEAD), bf16)],
    )(g_win)


def _dwin_split(name, d_wmain, d_wf, A, H):
    D, NM = d_wmain.shape
    DP4 = (NM + H) // N_CHIPS
    tr = _tile(D, 256, 8)

    def body(m_ref, f_ref, o_ref):
        m = m_ref[...]
        full = jnp.concatenate([m[:, :3 * A], f_ref[:, :H], m[:, 3 * A:]], axis=1)
        for j in range(N_CHIPS):
            o_ref[j] = full[:, j * DP4:(j + 1) * DP4]

    return pl.pallas_call(
        body, name=name, grid=(D // tr,),
        in_specs=[BS((tr, NM), lambda i: (i, 0)), BS((tr, HEAD), lambda i: (i, 0))],
        out_specs=BS((N_CHIPS, tr, DP4), lambda i: (0, i, 0)), out_shape=S((N_CHIPS, D, DP4), f32),
        compiler_params=_params(),
    )(d_wmain, d_wf)


def _gelu_and_grad(x):
    k0, k1 = 0.7978845608028654, 0.044715
    th = jnp.tanh(k0 * (x + k1 * x * x * x))
    val = 0.5 * x * (1.0 + th)
    grad = 0.5 * (1.0 + th) + 0.5 * x * (1.0 - th * th) * (k0 * (1.0 + 3.0 * k1 * x * x))
    return val, grad


def _fgate_fwd(name, pf, fb):
    T = pf.shape[0]

    def body(pf_ref, fb_ref, c_ref, ct_ref):
        xv = jax.nn.log_sigmoid(pf_ref[...] + fb_ref[...])
        row = lax.broadcasted_iota(jnp.int32, xv.shape, 0)
        s = 1
        while s < T:
            xv = xv + jnp.where(row >= s, pltpu.roll(xv, s, 0), 0.0)
            s *= 2
        c_ref[...] = xv
        ct_ref[...] = xv.T

    return pl.pallas_call(body, name=name, out_shape=[S((T, HEAD), f32), S((HEAD, T), f32)])(pf, fb)


def _fgate_bwd(name, dct, pf, fb):
    T = pf.shape[0]

    def body(dct_ref, pf_ref, fb_ref, dpf_ref, dfb_ref):
        xv = dct_ref[...].T
        row = lax.broadcasted_iota(jnp.int32, xv.shape, 0)
        s = 1
        while s < T:
            xv = xv + jnp.where(row + s < T, pltpu.roll(xv, T - s, 0), 0.0)
            s *= 2
        df = xv * jax.nn.sigmoid(-(pf_ref[...] + fb_ref[...]))
        dpf_ref[...] = df.astype(bf16)
        dfb_ref[...] = jnp.sum(df, axis=0, keepdims=True)

    return pl.pallas_call(body, name=name, out_shape=[S((T, HEAD), bf16), S((1, HEAD), f32)])(dct, pf, fb)


def _qk_norm(name, P, qg, kg, A):
    T = P.shape[0]
    tr = _tile(T, 256, 16)
    n_heads = A // HEAD

    def body(q_ref, k_ref, v_ref, qg_ref, kg_ref, qn_ref, kn_ref, vb_ref):
        for h in range(n_heads):
            sl = slice(h * HEAD, (h + 1) * HEAD)
            for src, g_ref, dst in ((q_ref, qg_ref, qn_ref), (k_ref, kg_ref, kn_ref)):
                xv = src[:, sl]
                r = lax.rsqrt(jnp.mean(xv * xv, axis=-1, keepdims=True) + EPS)
                dst[:, sl] = (xv * r * g_ref[...]).astype(bf16)
        vb_ref[...] = v_ref[...].astype(bf16)

    vec = BS((1, HEAD), lambda i: (0, 0))
    out = BS((tr, A), lambda i: (i, 0))
    return pl.pallas_call(
        body, name=name, grid=(T // tr,),
        in_specs=[BS((tr, A), lambda i: (i, 0)), BS((tr, A), lambda i: (i, 1)), BS((tr, A), lambda i: (i, 2)), vec, vec],
        out_specs=[out, out, out], out_shape=[S((T, A), bf16)] * 3,
    )(P, P, P, qg, kg)


def _attn_fwd(name, qn, kn, vb, c_col, c_row, tb, mix_width):
    T, A = qn.shape
    H = A // HEAD
    nb = T // tb
    scale = HEAD ** -0.5
    hp = _heads_per_program(H, 4)
    wide = hp * HEAD

    def body(q_ref, k_ref, v_ref, cq_ref, ck_ref, o_ref, o32_ref, lse_ref):
        i = pl.program_id(1)
        below = lax.broadcasted_iota(jnp.int32, (tb, tb), 0) >= lax.broadcasted_iota(jnp.int32, (tb, tb), 1)

        def block(j, carry, diagonal):
            koff = pl.multiple_of(j * tb, tb)
            out = []
            for hh in range(hp):
                m, l, acc = carry[hh]
                sl = slice(hh * HEAD, (hh + 1) * HEAD)
                k = k_ref[pl.ds(koff, tb), sl]
                v = v_ref[pl.ds(koff, tb), sl]
                s = _dot(q_ref[:, sl], k, "nt") * scale + (cq_ref[hh] - ck_ref[hh, j])
                if diagonal:
                    s = jnp.where(below, s, NEG)
                m_new = jnp.maximum(m, jnp.max(s, axis=-1, keepdims=True))
                alpha = jnp.exp(m - m_new)
                p = jnp.exp(s - m_new)
                l = l * alpha + jnp.sum(p, axis=-1, keepdims=True)
                acc = acc * alpha + _dot(p, v, "nn")
                out.append((m_new, l, acc))
            return tuple(out)

        init = tuple((jnp.full((tb, 1), NEG, f32), jnp.zeros((tb, 1), f32), jnp.zeros((tb, HEAD), f32))
                     for _ in range(hp))
        carry = lax.fori_loop(0, i, lambda j, c: block(j, c, False), init)
        carry = block(i, carry, True)
        for hh in range(hp):
            m, l, acc = carry[hh]
            sl = slice(hh * HEAD, (hh + 1) * HEAD)
            o = acc / l
            o_ref[:, sl] = o.astype(bf16)
            o32_ref[:, sl] = o
            lse_ref[hh] = m + jnp.log(l)

    return pl.pallas_call(
        body, name=name, grid=(H // hp, nb),
        in_specs=[BS((tb, wide), lambda h, i: (i, h)), BS((T, wide), lambda h, i: (0, h)),
                  BS((T, wide), lambda h, i: (0, h)), BS((hp, tb, 1), lambda h, i: (h, i, 0)),
                  BS((hp, nb, 1, tb), lambda h, i: (h, 0, 0, 0))],
        out_specs=[BS((tb, wide), lambda h, i: (i, h)), BS((tb, wide), lambda h, i: (i, h)),
                   BS((hp, tb, 1), lambda h, i: (h, i, 0))],
        out_shape=[S((T, mix_width), bf16), S((T, A), f32), S((H, T, 1), f32)],
    )(qn, kn, vb, c_col, c_row)


def _attn_bwd(name, qn, kn, vb, o, dmix, lse, c_col, c_row, P, qg, kg, tb):
    T, A = qn.shape
    H = A // HEAD
    nb = T // tb
    scale = HEAD ** -0.5
    hp = _heads_per_program(H)
    wide = hp * HEAD

    def body(q_ref, k_ref, v_ref, o_ref, do_ref, lse_ref, cq_ref, ck_ref, qraw_ref, kraw_ref, qg_ref, kg_ref,
             dq_out, dk_out, dv_out, dc_out, dqg_out, dkg_out, dq_acc, dk_acc, delta_s):
        h = pl.program_id(0)
        dq_acc[...] = jnp.zeros_like(dq_acc)
        below = lax.broadcasted_iota(jnp.int32, (tb, tb), 0) >= lax.broadcasted_iota(jnp.int32, (tb, tb), 1)
        for hh in range(hp):
            sl = slice(hh * HEAD, (hh + 1) * HEAD)
            delta_s[hh] = jnp.sum(do_ref[:, sl].astype(bf16).astype(f32) * o_ref[:, sl], axis=-1, keepdims=True)

        def kblock(j, _):
            koff = pl.multiple_of(j * tb, tb)

            def products(i, hh):
                sl = slice(hh * HEAD, (hh + 1) * HEAD)
                qoff = pl.multiple_of(i * tb, tb)
                return (_dot(q_ref[pl.ds(qoff, tb), sl], k_ref[pl.ds(koff, tb), sl], "nt"),
                        _dot(do_ref[pl.ds(qoff, tb), sl], v_ref[pl.ds(koff, tb), sl], "nt"))

            def qblock(i, carry, diagonal):
                qoff = pl.multiple_of(i * tb, tb)
                out = []
                for hh in range(hp):
                    dk, dv, dc, qk_i, dp = carry[hh]
                    sl = slice(hh * HEAD, (hh + 1) * HEAD)
                    ahead = products(jnp.minimum(i + 1, nb - 1), hh)
                    k = k_ref[pl.ds(koff, tb), sl]
                    q = q_ref[pl.ds(qoff, tb), sl]
                    do = do_ref[pl.ds(qoff, tb), sl].astype(bf16)
                    s = qk_i * scale + (cq_ref[hh, pl.ds(qoff, tb), :] - ck_ref[hh, j])
                    if diagonal:
                        s = jnp.where(below, s, NEG)
                    p = jnp.exp(s - lse_ref[hh, pl.ds(qoff, tb), :])
                    dv = dv + _dot(p, do, "tn")
                    ds = p * (dp - delta_s[hh, pl.ds(qoff, tb), :])
                    dc = dc - jnp.sum(ds, axis=0, keepdims=True)
                    dsb = (ds * scale).astype(bf16)
                    dk = dk + _dot(dsb, q, "tn")
                    dq_acc[pl.ds(qoff, tb), sl] += _dot(dsb, k, "nn")
                    out.append((dk, dv, dc, *ahead))
                return tuple(out)

            init = tuple((jnp.zeros((tb, HEAD), f32), jnp.zeros((tb, HEAD), f32), jnp.zeros((1, tb), f32),
                          *products(j, hh)) for hh in range(hp))
            carry = qblock(j, init, True)
            carry = lax.fori_loop(j + 1, nb, lambda i, c: qblock(i, c, False), carry)
            for hh in range(hp):
                dk, dv, dc = carry[hh][:3]
                sl = slice(hh * HEAD, (hh + 1) * HEAD)
                dk_acc[pl.ds(koff, tb), sl] = dk
                dv_out[pl.ds(koff, tb), sl] = dv.astype(bf16)
                dc_out[hh, j] = dc
            return 0

        lax.fori_loop(0, nb, kblock, 0)

        for raw_ref, g_ref, acc_ref, d_out, dg_out in ((qraw_ref, qg_ref, dq_acc, dq_out, dqg_out),
                                                       (kraw_ref, kg_ref, dk_acc, dk_out, dkg_out)):
            part = jnp.zeros((1, HEAD), f32)
            for hh in range(hp):
                sl = slice(hh * HEAD, (hh + 1) * HEAD)
                xv = raw_ref[:, sl]
                r = lax.rsqrt(jnp.mean(xv * xv, axis=-1, keepdims=True) + EPS)
                xh = xv * r
                dn = acc_ref[:, sl]
                dxh = dn * g_ref[...]
                d_out[:, sl] = (r * (dxh - xh * jnp.mean(dxh * xh, axis=-1, keepdims=True))).astype(bf16)
                part = part + jnp.sum(dn * xh, axis=0, keepdims=True)

            @pl.when(h == 0)
            def _():
                dg_out[...] = part

            @pl.when(h > 0)
            def _():
                dg_out[...] += part

    heads = lambda off: BS((T, wide), lambda h: (0, off + h))
    col = BS((hp, T, 1), lambda h: (h, 0, 0))
    row = BS((hp, nb, 1, tb), lambda h: (h, 0, 0, 0))
    vec = BS((1, HEAD), lambda h: (0, 0))
    return pl.pallas_call(
        body, name=name, grid=(H // hp,),
        in_specs=[heads(0), heads(0), heads(0), heads(0), heads(0), col, col, row, heads(0), heads(H // hp), vec, vec],
        out_specs=[heads(0), heads(0), heads(0), row, vec, vec],
        out_shape=[S((T, A), bf16)] * 3 + [S((H, nb, 1, tb), f32), S((1, HEAD), f32), S((1, HEAD), f32)],
        scratch_shapes=[pltpu.VMEM((T, wide), f32), pltpu.VMEM((T, wide), f32), pltpu.VMEM((hp, T, 1), f32)],
        compiler_params=_params(),
    )(qn, kn, vb, o, dmix, lse, c_col, c_row, P, P, qg, kg)


def _gmlp_fwd(name, P, mix, gain, ws, b, col_u, col_v, col_y, Wd):
    T = P.shape[0]
    G = Wd // HEAD
    tr = _tile(T, 512, HEAD)

    def body(u_ref, v_ref, gain_ref, ws_ref, b_ref, mix_ref, y_ref):
        tril = lax.broadcasted_iota(jnp.int32, (HEAD, HEAD), 0) >= lax.broadcasted_iota(jnp.int32, (HEAD, HEAD), 1)
        wm = jnp.where(tril, ws_ref[...], 0.0).astype(bf16)
        for n in range(tr // HEAD):
            rows = slice(n * HEAD, (n + 1) * HEAD)
            u = jax.nn.gelu(u_ref[rows, :])
            a = jax.nn.gelu(v_ref[rows, :])
            r = lax.rsqrt(jnp.mean(a * a, axis=-1, keepdims=True) + EPS)
            vn = a * r * gain_ref[...]
            mixed = _dot(wm, vn, "nn") + b_ref[...]
            y_ref[rows, :] = (u * mixed).astype(bf16)

    return pl.pallas_call(
        body, name=name, grid=(G, T // tr),
        in_specs=[BS((tr, HEAD), lambda g, i: (i, col_u + g)), BS((tr, HEAD), lambda g, i: (i, col_v + g)),
                  BS((None, 1, HEAD), lambda g, i: (g, 0, 0)), BS((None, HEAD, HEAD), lambda g, i: (g, 0, 0)),
                  BS((None, HEAD, 1), lambda g, i: (g, 0, 0)), ANY],
        out_specs=BS((tr, HEAD), lambda g, i: (i, col_y + g)), out_shape=S(mix.shape, bf16),
        input_output_aliases={5: 0},
    )(P, P, gain, ws, b, mix)


def _gmlp_bwd(name, P, dmix, gain, ws, b, col_u, col_v, col_dy, Wd):
    T = P.shape[0]
    G = Wd // HEAD
    tr = _tile(T, 512, HEAD)

    def body(u_ref, v_ref, dy_ref, gain_ref, ws_ref, b_ref, du_ref, dv_ref, dws_ref, db_ref, dgain_ref):
        i = pl.program_id(1)
        tril = lax.broadcasted_iota(jnp.int32, (HEAD, HEAD), 0) >= lax.broadcasted_iota(jnp.int32, (HEAD, HEAD), 1)
        wm = jnp.where(tril, ws_ref[...], 0.0).astype(bf16)
        gain_v = gain_ref[...]
        dw = jnp.zeros((HEAD, HEAD), f32)
        db = jnp.zeros((HEAD, 1), f32)
        dgain = jnp.zeros((1, HEAD), f32)
        for n in range(tr // HEAD):
            rows = slice(n * HEAD, (n + 1) * HEAD)
            u, du_dx = _gelu_and_grad(u_ref[rows, :])
            a, da_dx = _gelu_and_grad(v_ref[rows, :])
            dy = dy_ref[rows, :]
            r = lax.rsqrt(jnp.mean(a * a, axis=-1, keepdims=True) + EPS)
            ah = a * r
            vnb = (ah * gain_v).astype(bf16)
            mixed = _dot(wm, vnb, "nn") + b_ref[...]
            dm = dy * u
            dmb = dm.astype(bf16)
            du_ref[rows, :] = (dy * mixed * du_dx).astype(bf16)
            db = db + jnp.sum(dm, axis=1, keepdims=True)
            dw = dw + _dot(dmb, vnb, "nt")
            dvn = _dot(wm, dmb, "tn")
            dgain = dgain + jnp.sum(dvn * ah, axis=0, keepdims=True)
            dah = dvn * gain_v
            da = r * (dah - ah * jnp.mean(dah * ah, axis=-1, keepdims=True))
            dv_ref[rows, :] = (da * da_dx).astype(bf16)
        dw = jnp.where(tril, dw, 0.0)

        @pl.when(i == 0)
        def _():
            dws_ref[...] = dw
            db_ref[...] = db
            dgain_ref[...] = dgain

        @pl.when(i > 0)
        def _():
            dws_ref[...] += dw
            db_ref[...] += db
            dgain_ref[...] += dgain

    out = BS((tr, HEAD), lambda g, i: (i, g))
    return pl.pallas_call(
        body, name=name, grid=(G, T // tr),
        in_specs=[BS((tr, HEAD), lambda g, i: (i, col_u + g)), BS((tr, HEAD), lambda g, i: (i, col_v + g)),
                  BS((tr, HEAD), lambda g, i: (i, col_dy + g)),
                  BS((None, 1, HEAD), lambda g, i: (g, 0, 0)), BS((None, HEAD, HEAD), lambda g, i: (g, 0, 0)),
                  BS((None, HEAD, 1), lambda g, i: (g, 0, 0))],
        out_specs=[out, out, BS((None, HEAD, HEAD), lambda g, i: (g, 0, 0)), BS((None, HEAD, 1), lambda g, i: (g, 0, 0)),
                   BS((None, 1, HEAD), lambda g, i: (g, 0, 0))],
        out_shape=[S((T, Wd), bf16), S((T, Wd), bf16), S((G, HEAD, HEAD), f32), S((G, HEAD, 1), f32),
                   S((G, 1, HEAD), f32)],
    )(P, P, dmix, gain, ws, b)


def _trailing_window(xv, w, row):
    k = 1
    while k < w:
        xv = xv + jnp.where(row >= k, pltpu.roll(xv, k, 0), 0.0)
        k *= 2
    return xv


def _leading_window(xv, w, row, T):
    k = 1
    while k < w:
        xv = xv + jnp.where(row + k < T, pltpu.roll(xv, T - k, 0), 0.0)
        k *= 2
    return xv


def _pool_fwd(name, P, mix, pw, ps, col_x, col_y, Wd):
    T = P.shape[0]
    Gp = Wd // HEAD

    def body(x_ref, pw_ref, ps_ref, mix_ref, y_ref):
        row = lax.broadcasted_iota(jnp.int32, (T, HEAD), 0)
        for g in range(Gp):
            w = POOL_WINDOWS[g]
            sl = slice(g * HEAD, (g + 1) * HEAD)
            xv = x_ref[:, sl]
            cnt = jnp.minimum(row + 1, w).astype(f32)
            d = _trailing_window(xv, w, row) / cnt - xv
            y_ref[:, sl] = (_dot(d, pw_ref[g], "nn") * ps_ref[:, sl]).astype(bf16)

    return pl.pallas_call(
        body, name=name, grid=(1,),
        in_specs=[BS((T, Wd), lambda i: (0, col_x)), BS((Gp, HEAD, HEAD), lambda i: (0, 0, 0)), BS((1, Wd), lambda i: (0, 0)),
                  ANY],
        out_specs=BS((T, Wd), lambda i: (0, col_y)), out_shape=S(mix.shape, bf16), input_output_aliases={3: 0},
        compiler_params=_params(),
    )(P, pw, ps, mix)


def _pool_bwd(name, P, dmix, pw, ps, col_x, col_dy, Wd):
    T = P.shape[0]
    Gp = Wd // HEAD

    def body(x_ref, dy_ref, pw_ref, ps_ref, dx_ref, dpw_ref, dps_ref):
        row = lax.broadcasted_iota(jnp.int32, (T, HEAD), 0)
        for g in range(Gp):
            w = POOL_WINDOWS[g]
            sl = slice(g * HEAD, (g + 1) * HEAD)
            xv = x_ref[:, sl]
            cnt = jnp.minimum(row + 1, w).astype(f32)
            d = (_trailing_window(xv, w, row) / cnt - xv).astype(bf16)
            pwb = pw_ref[g].astype(bf16)
            z = _dot(d, pwb, "nn")
            dy = dy_ref[:, sl]
            dps_ref[:, sl] = jnp.sum(dy * z, axis=0, keepdims=True)
            dzb = (dy * ps_ref[:, sl]).astype(bf16)
            dpw_ref[g] = _dot(d, dzb, "tn")
            dd = _dot(dzb, pwb, "nt")
            dx_ref[:, sl] = (_leading_window(dd / cnt, w, row, T) - dd).astype(bf16)

    return pl.pallas_call(
        body, name=name, grid=(1,),
        in_specs=[BS((T, Wd), lambda i: (0, col_x)), BS((T, Wd), lambda i: (0, col_dy)),
                  BS((Gp, HEAD, HEAD), lambda i: (0, 0, 0)), BS((1, Wd), lambda i: (0, 0))],
        out_specs=[BS((T, Wd), lambda i: (0, 0)), BS((Gp, HEAD, HEAD), lambda i: (0, 0, 0)), BS((1, Wd), lambda i: (0, 0))],
        out_shape=[S((T, Wd), bf16), S((Gp, HEAD, HEAD), f32), S((1, Wd), f32)], compiler_params=_params(),
    )(P, dmix, pw, ps)


def _adamw(name, w, g, m, v):
    R, C = w.shape
    lanes = -(-C // 128) * 128
    tr = _tile(R, max(8, (512 * 1024) // lanes // 8 * 8), 8)
    c1 = 1.0 - ADAM_B1 ** ADAM_STEP
    c2 = 1.0 - ADAM_B2 ** ADAM_STEP

    def body(w_ref, g_ref, m_ref, v_ref, d_ref, nm_ref, nv_ref):
        gv = g_ref[...]
        nm = ADAM_B1 * m_ref[...] + (1.0 - ADAM_B1) * gv
        nv = ADAM_B2 * v_ref[...] + (1.0 - ADAM_B2) * (gv * gv)
        d_ref[...] = -ADAM_LR * ((nm / c1) / (jnp.sqrt(nv / c2) + ADAM_EPS) + ADAM_WD * w_ref[...])
        nm_ref[...] = nm
        nv_ref[...] = nv

    blk = BS((tr, C), lambda i: (i, 0))
    return pl.pallas_call(
        body, name=name, grid=(R // tr,), in_specs=[blk] * 4, out_specs=[blk] * 3, out_shape=[S((R, C), f32)] * 3,
    )(w, g, m, v)


def _adamw_layer(name, layer, w_all, m_all, v_all, g, prev):
    L, R, C = w_all.shape
    lanes = -(-C // 128) * 128
    tr = _tile(R, max(8, (512 * 1024) // lanes // 8 * 8), 8)
    c1 = 1.0 - ADAM_B1 ** ADAM_STEP
    c2 = 1.0 - ADAM_B2 ** ADAM_STEP
    n_prev = 0 if prev is None else 4

    def body(w_ref, m_ref, v_ref, g_ref, *rest):
        go_ref, d_ref, nm_ref, nv_ref = rest[n_prev:]
        gv = g_ref[...]
        nm = ADAM_B1 * m_ref[...] + (1.0 - ADAM_B1) * gv
        nv = ADAM_B2 * v_ref[...] + (1.0 - ADAM_B2) * (gv * gv)
        d_ref[...] = -ADAM_LR * ((nm / c1) / (jnp.sqrt(nv / c2) + ADAM_EPS) + ADAM_WD * w_ref[...])
        nm_ref[...] = nm
        nv_ref[...] = nv
        go_ref[...] = gv

    slab = BS((None, tr, C), lambda r: (layer, r, 0))
    return pl.pallas_call(
        body, name=name, grid=(R // tr,),
        in_specs=[slab, slab, slab, BS((tr, C), lambda r: (r, 0))] + [ANY] * n_prev,
        out_specs=[slab] * 4, out_shape=[S((L, R, C), f32)] * 4,
        input_output_aliases={4 + k: k for k in range(n_prev)},
    )(w_all, m_all, v_all, g, *(prev or ()))


def _chip_of(k):
    return k // 2, k % 2


def _remote(src, dst, send_sems, recv_sems, idx, dev):
    return pltpu.make_async_remote_copy(src_ref=src, dst_ref=dst, send_sem=send_sems.at[idx], recv_sem=recv_sems.at[idx],
                                        device_id=dev, device_id_type=MESH)


def _plan_gather_near(n):
    def plan(refs, ss, rs, base):
        ins, lands = refs[:n], refs[n:]
        x, y, c, j0 = _my_place()
        sib = (x, y, 1 - c)
        sends, recvs = [], []
        for a in range(n):
            half = ins[a].shape[0] // 2
            lo = c * half
            sends.append(_remote(ins[a], lands[a].at[j0], ss, rs, base + 3 * a + 2, sib))
            recvs.append(_remote(lands[a].at[j0], lands[a].at[j0], ss, rs, base + 3 * a + 2, sib))
            for r in (1, 2):
                k = j0 ^ r
                dev = (*_chip_of(k), c)
                sends.append(_remote(ins[a].at[pl.ds(lo, half)], lands[a].at[j0, pl.ds(lo, half)], ss, rs,
                                     base + 3 * a + r - 1, dev))
                landed = lands[a].at[k, pl.ds(lo, half)]
                recvs.append(_remote(landed, landed, ss, rs, base + 3 * a + r - 1, dev))
        return sends, recvs
    return plan, 3 * n


def _plan_gather_relay(n):
    def plan(refs, ss, rs, base):
        x, y, c, j0 = _my_place()
        sib = (x, y, 1 - c)
        sends, recvs = [], []
        for a in range(n):
            half = refs[a].shape[1] // 2
            quarter = half // 2
            lo = c * half
            far = j0 ^ 3
            for r, to, off in ((1, 2, 0), (2, 1, quarter)):
                dev = (*_chip_of(j0 ^ to), c)
                piece = refs[a].at[j0 ^ r, pl.ds(lo + off, quarter)]
                sends.append(_remote(piece, piece, ss, rs, base + 4 * a + to - 1, dev))
                lands_here = refs[a].at[far, pl.ds(lo + off, quarter)]
                recvs.append(_remote(lands_here, lands_here, ss, rs, base + 4 * a + to - 1, dev))
                mine = refs[a].at[j0 ^ r, pl.ds(lo, half)]
                theirs = refs[a].at[j0 ^ r, pl.ds((1 - c) * half, half)]
                sends.append(_remote(mine, mine, ss, rs, base + 4 * a + 1 + r, sib))
                recvs.append(_remote(theirs, theirs, ss, rs, base + 4 * a + 1 + r, sib))
        return sends, recvs
    return plan, 4 * n


def _plan_gather_far(n):
    def plan(refs, ss, rs, base):
        x, y, c, j0 = _my_place()
        sib = (x, y, 1 - c)
        sends, recvs = [], []
        for a in range(n):
            half = refs[a].shape[1] // 2
            mine = refs[a].at[j0 ^ 3, pl.ds(c * half, half)]
            theirs = refs[a].at[j0 ^ 3, pl.ds((1 - c) * half, half)]
            sends.append(_remote(mine, mine, ss, rs, base + a, sib))
            recvs.append(_remote(theirs, theirs, ss, rs, base + a, sib))
        return sends, recvs
    return plan, n


def _plan_sibling_halves(n):
    def plan(refs, ss, rs, base):
        ins, lands = refs[:n], refs[n:]
        x, y, c, _ = _my_place()
        sib = (x, y, 1 - c)
        sends, recvs = [], []
        for a in range(n):
            half = ins[a].shape[1] // 2
            sends.append(_remote(ins[a].at[:, pl.ds((1 - c) * half, half), :], lands[a], ss, rs, base + a, sib))
            recvs.append(_remote(lands[a], lands[a], ss, rs, base + a, sib))
        return sends, recvs
    return plan, n


def _plan_chip_scatter(n):
    def plan(refs, ss, rs, base):
        ins, lands = refs[:n], refs[n:]
        x, y, c, j0 = _my_place()
        sends, recvs = [], []
        for a in range(n):
            for r in (1, 2, 3):
                k = j0 ^ r
                dev = (*_chip_of(k), c)
                sends.append(_remote(ins[a].at[k], lands[a].at[j0], ss, rs, base + 3 * a + r - 1, dev))
                recvs.append(_remote(lands[a].at[k], lands[a].at[k], ss, rs, base + 3 * a + r - 1, dev))
        return sends, recvs
    return plan, 3 * n


def _plan_sibling_join(n):
    def plan(refs, ss, rs, base):
        x, y, c, _ = _my_place()
        sib = (x, y, 1 - c)
        sends, recvs = [], []
        for a in range(n):
            half = refs[a].shape[0] // 2
            mine = refs[a].at[pl.ds(c * half, half)]
            theirs = refs[a].at[pl.ds((1 - c) * half, half)]
            sends.append(_remote(mine, mine, ss, rs, base + a, sib))
            recvs.append(_remote(theirs, theirs, ss, rs, base + a, sib))
        return sends, recvs
    return plan, n


_HBM = pl.BlockSpec(memory_space=pltpu.HBM)
_SEM = pl.BlockSpec(memory_space=pltpu.SEMAPHORE)
_EFFECT = pltpu.SideEffectType.DATAFLOW_SIDE_EFFECTING


def _exchange_start(name, plan, bufs, after):
    plan_fn, n_sems = plan
    n = len(bufs)

    def body(*refs):
        ss, rs, token = refs[n + len(after)], refs[n + len(after) + 1], refs[-1]
        sends, _ = plan_fn(refs[:n], ss, rs, 0)
        for cp in sends:
            cp.start()
        token[...] = jnp.zeros_like(token)

    res = pl.pallas_call(
        body, name=name,
        out_shape=(pltpu.SemaphoreType.DMA((n_sems,)), pltpu.SemaphoreType.DMA((n_sems,)),
                   *[pltpu.HBM(b.shape, b.dtype) for b in bufs], S((8, 128), f32)),
        in_specs=[_HBM] * n + [ANY] * len(after),
        out_specs=(_SEM, _SEM, *[_HBM] * n, pl.BlockSpec(memory_space=pltpu.VMEM)),
        input_output_aliases={k: 2 + k for k in range(n)},
        compiler_params=pltpu.CompilerParams(has_side_effects=_EFFECT),
    )(*[pltpu.with_memory_space_constraint(b, pltpu.HBM) for b in bufs], *after)
    return res[0], res[1], list(res[2:2 + n]), res[-1]


def _exchange_wait(name, plan, send_sems, recv_sems, bufs, after):
    plan_fn, _ = plan
    n = len(bufs)

    def body(*refs):
        ss, rs, token = refs[n], refs[n + 1], refs[-1]
        sends, recvs = plan_fn(refs[:n], ss, rs, 0)
        for cp in recvs:
            cp.wait_recv()
        for cp in sends:
            cp.wait_send()
        token[...] = jnp.zeros_like(token)

    res = pl.pallas_call(
        body, name=name,
        out_shape=(*[pltpu.HBM(b.shape, b.dtype) for b in bufs], S((8, 128), f32)),
        in_specs=[_HBM] * n + [_SEM, _SEM] + [ANY] * len(after),
        out_specs=(*[_HBM] * n, pl.BlockSpec(memory_space=pltpu.VMEM)),
        input_output_aliases={k: k for k in range(n)},
        compiler_params=pltpu.CompilerParams(has_side_effects=_EFFECT),
    )(*bufs, send_sems, recv_sems, *after)
    return list(res[:n]), res[-1]


class _Order:
    def __init__(self, first):
        self.marker = first
        self.token = None

    def _after(self):
        return [self.marker] + ([] if self.token is None else [self.token])

    def start(self, name, plan, bufs):
        ss, rs, thru, self.token = _exchange_start(name, plan, bufs, self._after())
        return name, plan, ss, rs, thru

    def wait(self, handle):
        name, plan, ss, rs, thru = handle
        out, self.token = _exchange_wait(name + "_wait", plan, ss, rs, thru, self._after())
        return out

    def follows(self, small):
        return small if self.token is None else small + self.token[0, 0]

    def done(self, result):
        self.marker = result[(slice(0, 1),) * result.ndim].reshape(1, 1)


def _all_reduce_small(name, g8):
    _, R, L = g8.shape

    def body(g_ref, out_ref, land, red, send1, recv1, send2, recv2):
        x, y, c, _ = _my_place()
        me = 4 * x + 2 * y + c
        peers = []
        for r in range(1, N_DEV):
            q = me ^ r
            peers.append((q, (q // 4, (q // 2) % 2, q % 2)))
        first = []
        for r, (q, dev) in enumerate(peers):
            cp = pltpu.make_async_remote_copy(src_ref=g_ref.at[q], dst_ref=land.at[me], send_sem=send1.at[r],
                                              recv_sem=recv1.at[r], device_id=dev, device_id_type=MESH)
            cp.start()
            first.append(cp)
        land[me] = g_ref[me]
        for r, (q, dev) in enumerate(peers):
            pltpu.make_async_remote_copy(src_ref=land.at[q], dst_ref=land.at[q], send_sem=send1.at[r],
                                         recv_sem=recv1.at[r], device_id=dev, device_id_type=MESH).wait_recv()
        acc = land[0]
        for d in range(1, N_DEV):
            acc = acc + land[d]
        red[...] = acc
        out_ref[me] = acc
        second = []
        for r, (q, dev) in enumerate(peers):
            cp = pltpu.make_async_remote_copy(src_ref=red, dst_ref=out_ref.at[me], send_sem=send2.at[r],
                                              recv_sem=recv2.at[r], device_id=dev, device_id_type=MESH)
            cp.start()
            second.append(cp)
        for r, (q, dev) in enumerate(peers):
            pltpu.make_async_remote_copy(src_ref=out_ref.at[q], dst_ref=out_ref.at[q], send_sem=send2.at[r],
                                         recv_sem=recv2.at[r], device_id=dev, device_id_type=MESH).wait_recv()
        for cp in first + second:
            cp.wait_send()

    vm = pl.BlockSpec(memory_space=pltpu.VMEM)
    return pl.pallas_call(
        body, name=name, in_specs=[vm], out_specs=vm, out_shape=S(g8.shape, f32),
        scratch_shapes=[pltpu.VMEM((N_DEV, R, L), f32), pltpu.VMEM((R, L), f32)]
        + [pltpu.SemaphoreType.DMA((N_DEV - 1,))] * 4,
        compiler_params=_params(),
    )(g8)


def _pair_sum(name, g4, sib):
    _, rows, cols = g4.shape
    half = rows // 2
    lanes = -(-cols // 128) * 128
    tr = _tile(half, max(16, (1024 * 1024) // lanes // 16 * 16), 16)
    nb = half // tr

    def body(g_ref, s_ref, pb_ref, own_ref):
        j = pl.program_id(1)
        t = g_ref[...] + s_ref[...]
        pb_ref[...] = t.astype(bf16)

        @pl.when(j == _my_place()[3])
        def _():
            own_ref[...] = t

    return pl.pallas_call(
        body, name=name, grid=(nb, N_CHIPS),
        in_specs=[BS((None, tr, cols), lambda i, j: (j, lax.axis_index("c") * nb + i, 0)),
                  BS((None, tr, cols), lambda i, j: (j, i, 0))],
        out_specs=[BS((None, tr, cols), lambda i, j: (j, i, 0)), BS((tr, cols), lambda i, j: (i, 0))],
        out_shape=[S((N_CHIPS, half, cols), bf16), S((half, cols), f32)],
    )(g4, sib)


def _chip_sum(name, own, got):
    half, cols = own.shape
    lanes = -(-cols // 128) * 128
    tr = _tile(half, max(16, (1024 * 1024) // lanes // 16 * 16), 16)
    nb = half // tr

    def body(own_ref, *rest):
        got_refs, o_ref = rest[:N_CHIPS], rest[N_CHIPS]
        j0 = _my_place()[3]
        acc = None
        for k in range(N_CHIPS):
            t = jnp.where(j0 == k, own_ref[...], got_refs[k][...].astype(f32))
            acc = t if acc is None else acc + t
        o_ref[...] = acc

    def slot(k):
        return BS((None, tr, cols), lambda i: (jnp.where(_my_place()[3] == k, (k + 1) % N_CHIPS, k), i, 0))

    return pl.pallas_call(
        body, name=name, grid=(nb,),
        in_specs=[BS((tr, cols), lambda i: (i, 0))] + [slot(k) for k in range(N_CHIPS)],
        out_specs=BS((tr, cols), lambda i: (lax.axis_index("c") * nb + i, 0)),
        out_shape=S((2 * half, cols), f32),
    )(own, got, got, got, got)


def _rows_of(size):
    return -(-size // 1024) * 8


def _pack_rows(arrs, n_rows):
    parts = []
    for a in arrs:
        rows = _rows_of(a.size)
        if a.size % 128 == 0:
            part = a.astype(f32).reshape(-1, 128)
            part = jnp.pad(part, ((0, rows - part.shape[0]), (0, 0)))
        else:
            part = jnp.pad(a.reshape(-1).astype(f32), (0, rows * 128 - a.size)).reshape(rows, 128)
        parts.append(part)
    used = sum(p.shape[0] for p in parts)
    return jnp.concatenate(parts + [jnp.zeros((n_rows - used, 128), f32)], axis=0)


def _unpack_rows(packed, shapes):
    out, row = [], 0
    for shp in shapes:
        size = 1
        for d in shp:
            size *= d
        rows = packed[row:row + _rows_of(size)]
        out.append(rows[:size // 128].reshape(shp) if size % 128 == 0 else rows.reshape(-1)[:size].reshape(shp))
        row += _rows_of(size)
    return out


def kernel(x, p, norm_mix, w_in, q_norm, k_norm, forget_bias, gmlp_v_norm, gmlp_w_s, gmlp_b_s, pool_w, pool_scale, w_out, norm_ffn, w_ffn_gate, w_ffn_up, w_ffn_down, norm_ple, w_ple_gate, w_ple_proj, loss_target, m_norm_mix, m_w_in, m_q_norm, m_k_norm, m_forget_bias, m_gmlp_v_norm, m_gmlp_w_s, m_gmlp_b_s, m_pool_w, m_pool_scale, m_w_out, m_norm_ffn, m_w_ffn_gate, m_w_ffn_up, m_w_ffn_down, m_norm_ple, m_w_ple_gate, m_w_ple_proj, v_norm_mix, v_w_in, v_q_norm, v_k_norm, v_forget_bias, v_gmlp_v_norm, v_gmlp_w_s, v_gmlp_b_s, v_pool_w, v_pool_scale, v_w_out, v_norm_ffn, v_w_ffn_gate, v_w_ffn_up, v_w_ffn_down, v_norm_ple, v_w_ple_gate, v_w_ple_proj):
    W = dict(norm_mix=norm_mix, w_in=w_in, q_norm=q_norm, k_norm=k_norm, forget_bias=forget_bias,
             gmlp_v_norm=gmlp_v_norm, gmlp_w_s=gmlp_w_s, gmlp_b_s=gmlp_b_s, pool_w=pool_w, pool_scale=pool_scale,
             w_out=w_out, norm_ffn=norm_ffn, w_ffn_gate=w_ffn_gate, w_ffn_up=w_ffn_up, w_ffn_down=w_ffn_down,
             norm_ple=norm_ple, w_ple_gate=w_ple_gate, w_ple_proj=w_ple_proj)
    M = dict(norm_mix=m_norm_mix, w_in=m_w_in, q_norm=m_q_norm, k_norm=m_k_norm, forget_bias=m_forget_bias,
             gmlp_v_norm=m_gmlp_v_norm, gmlp_w_s=m_gmlp_w_s, gmlp_b_s=m_gmlp_b_s, pool_w=m_pool_w,
             pool_scale=m_pool_scale, w_out=m_w_out, norm_ffn=m_norm_ffn, w_ffn_gate=m_w_ffn_gate,
             w_ffn_up=m_w_ffn_up, w_ffn_down=m_w_ffn_down, norm_ple=m_norm_ple, w_ple_gate=m_w_ple_gate,
             w_ple_proj=m_w_ple_proj)
    V = dict(norm_mix=v_norm_mix, w_in=v_w_in, q_norm=v_q_norm, k_norm=v_k_norm, forget_bias=v_forget_bias,
             gmlp_v_norm=v_gmlp_v_norm, gmlp_w_s=v_gmlp_w_s, gmlp_b_s=v_gmlp_b_s, pool_w=v_pool_w,
             pool_scale=v_pool_scale, w_out=v_w_out, norm_ffn=v_norm_ffn, w_ffn_gate=v_w_ffn_gate,
             w_ffn_up=v_w_ffn_up, w_ffn_down=v_w_ffn_down, norm_ple=v_norm_ple, w_ple_gate=v_w_ple_gate,
             w_ple_proj=v_w_ple_proj)

    L = w_in.shape[0]
    _, T, D = x.shape
    A, Wd = D // 2, D // 4
    H = A // HEAD
    G = gmlp_w_s.shape[1]
    Gp = pool_w.shape[1]
    DP4 = w_in.shape[2]
    DP = N_CHIPS * DP4
    NM = 3 * A + 3 * Wd
    FS = w_ffn_gate.shape[2]
    FF = N_CHIPS * FS
    DS = D // N_CHIPS
    PL = p.shape[-1]
    assert Wd // G == HEAD and Wd // Gp == HEAD and DP == NM + H and H <= HEAD
    assert all(w & (w - 1) == 0 for w in POOL_WINDOWS[:Gp])
    tb = _tile(T, 256, HEAD)
    nb = T // tb
    tm = _tile(T, 512, 16)
    tmw = _tile(T, 1024, 16)
    tn = _tile(NM, 512, 128)
    tnd = _tile(D, 512, 128)
    tkd = _tile(D, 1024, 128)
    tnw = _tile(D, 1024, 128)
    col_gu, col_gv, col_xp = 3 * A // HEAD, (3 * A + Wd) // HEAD, (3 * A + 2 * Wd) // Wd
    col_dg, col_dp = A // HEAD, (A + Wd) // Wd

    order = _Order(x[0, :1, :1])

    def gather_near(i, names, tag=""):
        shards = [_cast_layer(f"cast_{n}_{i}", W[n], i) for n in names]
        lands = [lax.empty((N_CHIPS,) + s.shape, bf16) for s in shards]
        return names, order.start(f"ag_near_{i}{tag}", _plan_gather_near(len(names)), shards + lands)

    def gather_relay(i, near, tag=""):
        names, handle = near
        return names, order.start(f"ag_relay_{i}{tag}", _plan_gather_relay(len(names)), order.wait(handle)[len(names):])

    def gather_far(i, relayed, tag=""):
        names, handle = relayed
        return names, order.start(f"ag_far_{i}{tag}", _plan_gather_far(len(names)), order.wait(handle))

    win_count = [0]

    def gathered(far):
        names, handle = far
        g = dict(zip(names, order.wait(handle)))
        out = {}
        if "w_in" in g:
            out["w_main"], out["w_f"] = _win_assemble(f"w_in_cols_{win_count[0]}", g["w_in"], A, H)
            win_count[0] += 1
        if "w_out" in g:
            out["w_out"] = g["w_out"].reshape(D, D)
        if "w_ffn_gate" in g:
            out.update(w_gate=g["w_ffn_gate"], w_up=g["w_ffn_up"], w_down=g["w_ffn_down"].reshape(FF, D),
                       w_pg=g["w_ple_gate"].reshape(D, D), w_pp=g["w_ple_proj"])
        return out

    Wf = [None] * L
    relayed = gather_relay(0, gather_near(0, BIG[:2], "a"), "a")
    near_rest = gather_near(0, BIG[2:], "c")
    Wf[0] = gathered(gather_far(0, relayed, "a"))
    near = relayed = None

    h = x.reshape(T, D)
    pb16 = p.reshape(L, T, PL).astype(bf16)
    saved = []

    for i in range(L):
        w = Wf[i]
        sv = dict(h0=h)
        xn1 = _rms_fwd(f"rms1_{i}", h, order.follows(norm_mix[i]))
        (P,) = _matmul(f"proj_{i}", "nn", (T // tmw, NM // tn),
                       [((xn1, BS((tmw, D), lambda i, j: (i, 0))), (w["w_main"], BS((D, tn), lambda i, j: (0, j))))], [],
                       [((T, NM), f32, BS((tmw, tn), lambda i, j: (i, j)))])
        (Pf,) = _matmul(f"projf_{i}", "nn", (T // tm, 1),
                        [((xn1, BS((tm, D), lambda i, j: (i, 0))), (w["w_f"], BS((D, HEAD), lambda i, j: (0, 0))))], [],
                        [((T, HEAD), f32, BS((tm, HEAD), lambda i, j: (i, 0)))])
        fb = jnp.pad(forget_bias[i], (0, HEAD - H)).reshape(1, HEAD)
        cc, ct = _fgate_fwd(f"fgate_{i}", Pf, fb)
        c_col = ct[:H].reshape(H, T, 1)
        c_row = ct[:H].reshape(H, nb, 1, tb)
        qg, kg = q_norm[i].reshape(1, HEAD), k_norm[i].reshape(1, HEAD)
        qn, kn, vb = _qk_norm(f"qknorm_{i}", P, qg, kg, A)
        mix, o32, lse = _attn_fwd(f"attn_{i}", qn, kn, vb, c_col, c_row, tb, D)
        if i == 0:
            order.done(o32)
            relayed_rest = gather_relay(0, near_rest, "c")
            near = gather_near(1, BIG) if L > 1 else None
        gain = gmlp_v_norm[i].reshape(G, 1, HEAD)
        bs = gmlp_b_s[i].reshape(G, HEAD, 1)
        mix = _gmlp_fwd(f"gmlp_{i}", P, mix, order.follows(gain), gmlp_w_s[i], bs, col_gu, col_gv, col_dg, Wd)
        ps = pool_scale[i].reshape(1, Wd)
        mix = _pool_fwd(f"pool_{i}", P, mix, pool_w[i], ps, col_xp, col_dp, Wd)
        (h1,) = _matmul(f"out_{i}", "nn", (T // tmw, D // tnd),
                        [((mix, BS((tmw, D), lambda i, j: (i, 0))), (w["w_out"], BS((D, tnd), lambda i, j: (0, j))))],
                        [(h, BS((tmw, tnd), lambda i, j: (i, j)))],
                        [((T, D), f32, BS((tmw, tnd), lambda i, j: (i, j)))],
                        epilogue=lambda accs, ex: (accs[0] + ex[0],))
        xn2 = _rms_fwd(f"rms2_{i}", h1, norm_ffn[i])
        order.done(xn2)
        if i == 0:
            w.update(gathered(gather_far(0, relayed_rest, "c")))
        elif i + 1 < L:
            relayed = gather_relay(i + 1, near)
            near = gather_near(i + 2, BIG) if i + 2 < L else None

        def ffn_epi(accs, ex):
            g_, u_ = accs
            return g_, u_, g_ * jax.nn.sigmoid(g_) * u_

        ffo = BS((tm, FS), lambda j, i: (i, j))
        Gt, Ut, act = _matmul(f"ffn1_{i}", "nn", (N_CHIPS, T // tm),
                              [((xn2, BS((tm, D), lambda j, i: (i, 0))), (w["w_gate"], BS((None, D, FS), lambda j, i: (j, 0, 0)))),
                               ((xn2, BS((tm, D), lambda j, i: (i, 0))), (w["w_up"], BS((None, D, FS), lambda j, i: (j, 0, 0))))],
                              [], [((T, FF), bf16, ffo)] * 3, epilogue=ffn_epi, after=order.token)
        (h2,) = _matmul(f"ffn2_{i}", "nn", (T // tmw, D // tnd),
                        [((act, BS((tmw, FF), lambda i, j: (i, 0))), (w["w_down"], BS((FF, tnd), lambda i, j: (0, j))))],
                        [(h1, BS((tmw, tnd), lambda i, j: (i, j)))],
                        [((T, D), f32, BS((tmw, tnd), lambda i, j: (i, j)))],
                        epilogue=lambda accs, ex: (accs[0] + ex[0],))
        far = None
        if i == 0 and L > 1:
            order.done(h2)
            relayed = gather_relay(1, near)
            near = gather_near(2, BIG) if L > 2 else None
        elif i + 1 < L:
            order.done(h2)
            far = gather_far(i + 1, relayed)
        xn3 = _rms_fwd(f"rms3_{i}", h2, order.follows(norm_ple[i]))

        def ple_epi(accs, ex):
            gate = jax.nn.sigmoid(accs[0])
            return ex[0] + accs[1] * gate, gate, accs[1]

        dso = BS((tmw, DS), lambda i, j: (i, j))
        h3, gate, e = _matmul(f"ple_{i}", "nn", (T // tmw, N_CHIPS),
                              [((xn3, BS((tmw, D), lambda i, j: (i, 0))), (w["w_pg"], BS((D, DS), lambda i, j: (0, j)))),
                               ((pb16[i], BS((tmw, PL), lambda i, j: (i, 0))), (w["w_pp"], BS((None, PL, DS), lambda i, j: (j, 0, 0))))],
                              [(h2, dso)], [((T, D), f32, dso), ((T, D), bf16, dso), ((T, D), bf16, dso)], epilogue=ple_epi)
        sv.update(xn1=xn1, P=P, Pf=Pf, fb=fb, c_col=c_col, c_row=c_row, qn=qn, kn=kn, vb=vb, o32=o32, lse=lse,
                  mix=mix, h1=h1, xn2=xn2, Gt=Gt, Ut=Ut, act=act, h2=h2, xn3=xn3, gate=gate, e=e)
        saved.append(sv)
        h = h3
        order.done(h3)
        if i + 1 < L:
            Wf[i + 1] = gathered(far if far is not None else gather_far(i + 1, relayed))

    dh, loss_tile = _loss_grad("loss", h, loss_target.reshape(T, D))

    small_g = {n: [None] * L for n in SMALL}
    big_out = {}

    def stage_a(u):
        n_u = len(u["names"])
        lands = [lax.empty((N_CHIPS, g.shape[1] // 2, g.shape[2]), f32) for g in u["grads"]]
        u["a"] = order.start(f"rs_a_{u['tag']}", _plan_sibling_halves(n_u), u["grads"] + lands)

    def stage_pair(u):
        n_u = len(u["names"])
        out = order.wait(u["a"])
        pairs = [_pair_sum(f"rs_pair_{u['tag']}_{a}", out[a], out[n_u + a]) for a in range(n_u)]
        u["pb"], u["own"] = [t[0] for t in pairs], [t[1] for t in pairs]

    def stage_b(u):
        lands = [lax.empty(t.shape, bf16) for t in u["pb"]]
        u["b"] = order.start(f"rs_b_{u['tag']}", _plan_chip_scatter(len(u["names"])), u["pb"] + lands)

    def stage_sum(u):
        n_u = len(u["names"])
        out = order.wait(u["b"])
        u["sum"] = [_chip_sum(f"rs_sum_{u['tag']}_{a}", u["own"][a], out[n_u + a]) for a in range(n_u)]

    def stage_c(u):
        u["c"] = order.start(f"rs_c_{u['tag']}", _plan_sibling_join(len(u["names"])), u["sum"])

    def stage_adamw(u):
        for n, r in zip(u["names"], order.wait(u["c"])):
            big_out[n] = _adamw_layer(f"adamw_{n}_{u['layer']}", u["layer"], W[n], M[n], V[n],
                                      r.reshape(W[n].shape[1:]), big_out.get(n))
    dh1 = prev_f = prev_m = None
    for i in reversed(range(L)):
        w, sv = Wf[i], saved[i]
        if dh1 is not None:
            dh, _, dg = _rms_bwd(f"rms1_bw_{i + 1}", dxn1, saved[i + 1]["h0"], order.follows(norm_mix[i + 1]), dh1)
            small_g["norm_mix"][i + 1] = dg.reshape(D)
        de, dz = _ple_bwd_elem(f"ple_bw_{i}", dh, sv["gate"], sv["e"])
        (d_wpp,) = _matmul(f"d_wpp_{i}", "tn", (N_CHIPS, 1),
                           [((pb16[i], BS((T, PL), lambda i, j: (0, 0))), (de, BS((T, DS), lambda i, j: (0, i))))], [],
                           [((N_CHIPS, PL, DS), f32, BS((None, PL, DS), lambda i, j: (i, 0, 0)))])
        (d_wpg,) = _matmul(f"d_wpg_{i}", "tn", (D // tkd, D // tnd),
                           [((sv["xn3"], BS((T, tkd), lambda i, j: (0, i))), (dz, BS((T, tnd), lambda i, j: (0, j))))], [],
                           [((D, D), f32, BS((tkd, tnd), lambda i, j: (i, j)))])
        order.done(dz)
        if prev_m is not None:
            stage_pair(prev_m)
        (dxn3,) = _matmul(f"d_xn3_{i}", "nt", (T // tmw, D // tnd),
                          [((dz, BS((tmw, D), lambda i, j: (i, 0))), (w["w_pg"], BS((tnd, D), lambda i, j: (j, 0))))], [],
                          [((T, D), f32, BS((tmw, tnd), lambda i, j: (i, j)))], after=order.token)
        dh2, dh2b, dg = _rms_bwd(f"rms3_bw_{i}", dxn3, sv["h2"], norm_ple[i], dh)
        small_g["norm_ple"][i] = dg.reshape(D)

        def dffn_epi(accs, ex):
            da = accs[0]
            g_, u_ = ex[0].astype(f32), ex[1].astype(f32)
            sg = jax.nn.sigmoid(g_)
            return da * u_ * (sg * (1.0 + g_ * (1.0 - sg))), da * (g_ * sg)

        ffo = BS((tm, FS), lambda j, i: (i, j))
        dG, dU = _matmul(f"d_act_{i}", "nt", (N_CHIPS, T // tm),
                         [((dh2b, BS((tm, D), lambda j, i: (i, 0))), (w["w_down"], BS((FS, D), lambda j, i: (j, 0))))],
                         [(sv["Gt"], ffo), (sv["Ut"], ffo)], [((T, FF), bf16, ffo)] * 2, epilogue=dffn_epi)
        (d_wd,) = _matmul(f"d_wd_{i}", "tn", (N_CHIPS, D // tnw),
                          [((sv["act"], BS((T, FS), lambda i, j: (0, i))), (dh2b, BS((T, tnw), lambda i, j: (0, j))))], [],
                          [((FF, D), f32, BS((FS, tnw), lambda i, j: (i, j)))])
        gu_out = BS((None, tnd, FS), lambda j, i: (j, i, 0))
        d_wg, d_wu = _matmul(f"d_wgu_{i}", "tn", (N_CHIPS, D // tnd),
                             [((sv["xn2"], BS((T, tnd), lambda j, i: (0, i))), (dG, BS((T, FS), lambda j, i: (0, j)))),
                              ((sv["xn2"], BS((T, tnd), lambda j, i: (0, i))), (dU, BS((T, FS), lambda j, i: (0, j))))], [],
                             [((N_CHIPS, D, FS), f32, gu_out)] * 2, epilogue=lambda accs, ex: (accs[0], accs[1]))
        order.done(dG)
        unit_f = dict(tag=f"{i}f", layer=i, names=["w_ffn_gate", "w_ffn_up", "w_ffn_down", "w_ple_gate", "w_ple_proj"],
                      grads=[d_wg, d_wu, d_wd.reshape(N_CHIPS, FS, D), d_wpg.reshape(N_CHIPS, DS, D), d_wpp])
        stage_a(unit_f)
        if prev_f is not None:
            stage_sum(prev_f)
            stage_c(prev_f)
        if prev_m is not None:
            stage_b(prev_m)
        tm2 = _tile(T, 512, 16)
        (dxn2,) = _matmul(f"d_xn2_{i}", "nt", (D // tnd, T // tm2),
                          [((dG, BS((tm2, FF), lambda j, i: (i, 0))), (w["w_gate"], BS((N_CHIPS, tnd, FS), lambda j, i: (0, j, 0)))),
                           ((dU, BS((tm2, FF), lambda j, i: (i, 0))), (w["w_up"], BS((N_CHIPS, tnd, FS), lambda j, i: (0, j, 0))))], [],
                          [((T, D), f32, BS((tm2, tnd), lambda j, i: (i, j)))], after=order.token)
        dh1, dh1b, dg = _rms_bwd(f"rms2_bw_{i}", dxn2, sv["h1"], norm_ffn[i], dh2)
        small_g["norm_ffn"][i] = dg.reshape(D)
        (dmix,) = _matmul(f"d_mix_{i}", "nt", (T // tmw, D // tnd),
                          [((dh1b, BS((tmw, D), lambda i, j: (i, 0))), (w["w_out"], BS((tnd, D), lambda i, j: (j, 0))))], [],
                          [((T, D), f32, BS((tmw, tnd), lambda i, j: (i, j)))])
        (d_wout,) = _matmul(f"d_wout_{i}", "tn", (D // tkd, D // tnd),
                            [((sv["mix"], BS((T, tkd), lambda i, j: (0, i))), (dh1b, BS((T, tnd), lambda i, j: (0, j))))], [],
                            [((D, D), f32, BS((tkd, tnd), lambda i, j: (i, j)))])
        order.done(dmix)
        stage_pair(unit_f)
        stage_b(unit_f)
        if prev_f is not None:
            stage_adamw(prev_f)
        qg, kg = q_norm[i].reshape(1, HEAD), k_norm[i].reshape(1, HEAD)
        dq, dk, dv, dc_row, dqg, dkg = _attn_bwd(f"attn_bw_{i}", sv["qn"], sv["kn"], sv["vb"], sv["o32"], dmix,
                                                 sv["lse"], sv["c_col"], sv["c_row"], sv["P"], order.follows(qg), kg, tb)
        small_g["q_norm"][i] = dqg.reshape(HEAD)
        small_g["k_norm"][i] = dkg.reshape(HEAD)
        dct = jnp.pad(dc_row.reshape(H, T), ((0, HEAD - H), (0, 0)))
        dPf, dfb = _fgate_bwd(f"fgate_bw_{i}", dct, sv["Pf"], sv["fb"])
        small_g["forget_bias"][i] = dfb[0, :H]
        gain = gmlp_v_norm[i].reshape(G, 1, HEAD)
        bs = gmlp_b_s[i].reshape(G, HEAD, 1)
        dgu, dgv, dws, dbs, dgain = _gmlp_bwd(f"gmlp_bw_{i}", sv["P"], dmix, gain, gmlp_w_s[i], bs, col_gu, col_gv,
                                              col_dg, Wd)
        small_g["gmlp_w_s"][i] = dws
        small_g["gmlp_b_s"][i] = dbs.reshape(G, HEAD)
        small_g["gmlp_v_norm"][i] = dgain.reshape(G, HEAD)
        ps = pool_scale[i].reshape(1, Wd)
        dxp, dpw, dps = _pool_bwd(f"pool_bw_{i}", sv["P"], dmix, pool_w[i], ps, col_xp, col_dp, Wd)
        small_g["pool_w"][i] = dpw
        small_g["pool_scale"][i] = dps.reshape(Wd)
        dP = jnp.concatenate([dq, dk, dv, dgu, dgv, dxp], axis=1)
        (d_wmain,) = _matmul(f"d_wmain_{i}", "tn", (D // tkd, NM // tn),
                             [((sv["xn1"], BS((T, tkd), lambda i, j: (0, i))), (dP, BS((T, tn), lambda i, j: (0, j))))], [],
                             [((D, NM), f32, BS((tkd, tn), lambda i, j: (i, j)))])
        (d_wf,) = _matmul(f"d_wf_{i}", "tn", (D // tnd, 1),
                          [((sv["xn1"], BS((T, tnd), lambda i, j: (0, i))), (dPf, BS((T, HEAD), lambda i, j: (0, 0))))], [],
                          [((D, HEAD), f32, BS((tnd, HEAD), lambda i, j: (i, 0)))])
        d_win4 = _dwin_split(f"d_win_cols_{i}", d_wmain, d_wf, A, H)
        order.done(dP)
        unit_m = dict(tag=f"{i}m", layer=i, names=["w_in", "w_out"], grads=[d_win4, d_wout.reshape(N_CHIPS, DS, D)])
        stage_a(unit_m)
        if prev_m is not None:
            stage_sum(prev_m)
            stage_c(prev_m)
        (dxn1,) = _matmul(f"d_xn1_{i}", "nt", (D // tnd, T // tm),
                          [((dP, BS((tm, NM), lambda j, i: (i, 0))), (w["w_main"], BS((tnd, NM), lambda j, i: (j, 0)))),
                           ((dPf, BS((tm, HEAD), lambda j, i: (i, 0))), (w["w_f"], BS((tnd, HEAD), lambda j, i: (j, 0))))], [],
                          [((T, D), f32, BS((tm, tnd), lambda j, i: (i, j)))], after=order.token)
        order.done(dxn1)
        if prev_m is not None:
            stage_adamw(prev_m)
        prev_f, prev_m = unit_f, unit_m

    dh, _, dg = _rms_bwd("rms1_bw_0", dxn1, saved[0]["h0"], order.follows(norm_mix[0]), dh1)
    small_g["norm_mix"][0] = dg.reshape(D)
    order.done(dh)

    small_full = {n: jnp.stack(small_g[n]) for n in SMALL}
    small_shapes = [W[n].shape for n in SMALL]
    n_rows = sum(_rows_of(W[n].size) for n in SMALL) + _rows_of(1)
    rows8 = -(-n_rows // 64) * 8
    packed = order.follows(_pack_rows([small_full[n] for n in SMALL] + [loss_tile[0, :1]], N_DEV * rows8))
    summed = _all_reduce_small("allreduce_small", packed.reshape(N_DEV, rows8, 128)).reshape(-1, 128)
    order.done(summed)

    stage_sum(prev_f)
    stage_c(prev_f)
    stage_pair(prev_m)
    stage_b(prev_m)
    stage_adamw(prev_f)
    stage_sum(prev_m)
    stage_c(prev_m)
    stage_adamw(prev_m)
    *small_grads, loss_row = _unpack_rows(summed, small_shapes + [(1,)])
    grads = dict(zip(SMALL, small_grads))
    loss = loss_row[0]

    wp, mp, vp = (_pack_rows([t[n] for n in SMALL], N_DEV * rows8) for t in (W, M, V))
    delta, new_m, new_v = (dict(zip(SMALL, _unpack_rows(t, small_shapes)))
                           for t in _adamw("adamw_small", wp, summed, mp, vp))
    for n in BIG:
        grads[n], delta[n], new_m[n], new_v[n] = big_out[n]

    return (loss, dh.reshape(1, T, D), *[grads[n] for n in WEIGHTS], *[delta[n] for n in WEIGHTS],
            *[new_m[n] for n in WEIGHTS], *[new_v[n] for n in WEIGHTS])
```

```python
import jax
import jax.numpy as jnp
from jax import lax
from jax.experimental import pallas as pl
from jax.experimental.pallas import tpu as pltpu

f32, bf16 = jnp.float32, jnp.bfloat16
S = jax.ShapeDtypeStruct
BS = pl.BlockSpec
ANY = pl.BlockSpec(memory_space=pl.ANY)
MESH = pl.DeviceIdType.MESH

EPS = 1e-6
HEAD = 128
POOL_WINDOWS = (2, 4, 8, 16)
NEG = -1e30
N_CHIPS = 4
N_DEV = 8
VMEM_LIMIT = 56 * 1024 * 1024

ADAM_LR, ADAM_B1, ADAM_B2, ADAM_EPS, ADAM_WD, ADAM_STEP = 0.001, 0.9, 0.999, 1e-08, 0.01, 10

BIG = ("w_in", "w_out", "w_ffn_gate", "w_ffn_up", "w_ffn_down", "w_ple_gate", "w_ple_proj")
SMALL = ("norm_mix", "q_norm", "k_norm", "forget_bias", "gmlp_v_norm", "gmlp_w_s", "gmlp_b_s", "pool_w",
         "pool_scale", "norm_ffn", "norm_ple")
WEIGHTS = ("norm_mix", "w_in", "q_norm", "k_norm", "forget_bias", "gmlp_v_norm", "gmlp_w_s", "gmlp_b_s", "pool_w",
           "pool_scale", "w_out", "norm_ffn", "w_ffn_gate", "w_ffn_up", "w_ffn_down", "norm_ple", "w_ple_gate",
           "w_ple_proj")


def _tile(n, target, mult):
    best = None
    for t in range(mult, min(n, target) + 1, mult):
        if n % t == 0:
            best = t
    return best if best is not None else n


def _params(**kw):
    return pltpu.CompilerParams(vmem_limit_bytes=VMEM_LIMIT, **kw)


def _dot(a, b, kind):
    dims = {"nn": (((1,), (0,)), ((), ())), "nt": (((1,), (1,)), ((), ())), "tn": (((0,), (0,)), ((), ()))}[kind]
    return lax.dot_general(a.astype(bf16), b.astype(bf16), dims, preferred_element_type=f32)


def _heads_per_program(n_heads, want=2):
    while n_heads % want:
        want //= 2
    return want


def _my_place():
    x, y, c = lax.axis_index("x"), lax.axis_index("y"), lax.axis_index("c")
    return x, y, c, 2 * x + y


def _matmul(name, kind, grid, pairs, extras, outs, epilogue=None, after=None):
    n_p, n_e = len(pairs), len(extras)
    tokens = [] if after is None else [(after, BS((8, 128), lambda *_: (0, 0)))]

    def body(*refs):
        a_refs, b_refs = refs[:n_p], refs[n_p:2 * n_p]
        e_refs = refs[2 * n_p:2 * n_p + n_e]
        o_refs = refs[2 * n_p + n_e + len(tokens):]
        accs = []
        for a_ref, b_ref in zip(a_refs, b_refs):
            if len(b_ref.shape) == 3:
                w = b_ref.shape[2]
                acc = None
                for s in range(b_ref.shape[0]):
                    d = _dot(a_ref[:, s * w:(s + 1) * w], b_ref[s], kind)
                    acc = d if acc is None else acc + d
            else:
                acc = _dot(a_ref[...], b_ref[...], kind)
            accs.append(acc)
        if epilogue is None:
            res = accs[0]
            for t in accs[1:]:
                res = res + t
            res = (res,)
        else:
            res = epilogue(accs, [e[...] for e in e_refs])
        for o_ref, o in zip(o_refs, res):
            o_ref[...] = o.astype(o_ref.dtype)

    in_arrays = [p[0][0] for p in pairs] + [p[1][0] for p in pairs] + [e[0] for e in extras + tokens]
    in_specs = [p[0][1] for p in pairs] + [p[1][1] for p in pairs] + [e[1] for e in extras + tokens]
    res = pl.pallas_call(
        body, name=name, grid=grid, in_specs=in_specs,
        out_specs=[o[2] for o in outs], out_shape=[S(o[0], o[1]) for o in outs],
        compiler_params=_params(),
    )(*in_arrays)
    return res


def _rms_fwd(name, x, g):
    T, D = x.shape
    tr = _tile(T, 512, 16)

    def body(x_ref, g_ref, o_ref):
        xv = x_ref[...]
        r = lax.rsqrt(jnp.mean(xv * xv, axis=-1, keepdims=True) + EPS)
        o_ref[...] = (xv * r * g_ref[...]).astype(o_ref.dtype)

    return pl.pallas_call(
        body, name=name, grid=(T // tr,),
        in_specs=[BS((tr, D), lambda i: (i, 0)), BS((1, D), lambda i: (0, 0))],
        out_specs=BS((tr, D), lambda i: (i, 0)), out_shape=S((T, D), bf16),
    )(x, g.reshape(1, D))


def _rms_bwd(name, dxn, x, g, dres):
    T, D = x.shape
    tr = _tile(T, 256, 8)

    def body(dxn_ref, x_ref, g_ref, dres_ref, dx_ref, dxb_ref, dg_ref):
        i = pl.program_id(0)
        xv = x_ref[...]
        r = lax.rsqrt(jnp.mean(xv * xv, axis=-1, keepdims=True) + EPS)
        xh = xv * r
        dxn_v = dxn_ref[...]
        dxh = dxn_v * g_ref[...]
        dx = dres_ref[...] + r * (dxh - xh * jnp.mean(dxh * xh, axis=-1, keepdims=True))
        dx_ref[...] = dx
        dxb_ref[...] = dx.astype(bf16)
        part = jnp.sum(dxn_v * xh, axis=0, keepdims=True)

        @pl.when(i == 0)
        def _():
            dg_ref[...] = part

        @pl.when(i > 0)
        def _():
            dg_ref[...] += part

    row = BS((tr, D), lambda i: (i, 0))
    vec = BS((1, D), lambda i: (0, 0))
    return pl.pallas_call(
        body, name=name, grid=(T // tr,),
        in_specs=[row, row, vec, row], out_specs=[row, row, vec],
        out_shape=[S((T, D), f32), S((T, D), bf16), S((1, D), f32)],
    )(dxn, x, g.reshape(1, D), dres)


def _loss_grad(name, y, tgt):
    T, D = y.shape
    tr = _tile(T, 256, 8)

    def body(y_ref, t_ref, dy_ref, l_ref):
        i = pl.program_id(0)
        e = y_ref[...] - t_ref[...]
        dy_ref[...] = e * (1.0 / D)
        part = 0.5 * jnp.sum(jnp.mean(e * e, axis=-1, keepdims=True), axis=0, keepdims=True)

        @pl.when(i == 0)
        def _():
            l_ref[...] = jnp.zeros_like(l_ref)

        l_ref[...] += jnp.broadcast_to(part, l_ref.shape)

    row = BS((tr, D), lambda i: (i, 0))
    return pl.pallas_call(
        body, name=name, grid=(T // tr,), in_specs=[row, row],
        out_specs=[row, BS((8, 128), lambda i: (0, 0))],
        out_shape=[S((T, D), f32), S((8, 128), f32)],
    )(y, tgt)


def _ple_bwd_elem(name, dh, gate, e):
    T, D = dh.shape
    tr = _tile(T, 512, 16)

    def body(dh_ref, g_ref, e_ref, de_ref, dz_ref):
        d = dh_ref[...]
        g = g_ref[...].astype(f32)
        de_ref[...] = (d * g).astype(bf16)
        dz_ref[...] = (d * e_ref[...].astype(f32) * g * (1.0 - g)).astype(bf16)

    row = BS((tr, D), lambda i: (i, 0))
    return pl.pallas_call(
        body, name=name, grid=(T // tr,), in_specs=[row, row, row], out_specs=[row, row],
        out_shape=[S((T, D), bf16), S((T, D), bf16)],
    )(dh, gate, e)


def _cast_layer(name, w_all, layer):
    _, R, C = w_all.shape
    lanes = -(-C // 128) * 128
    tr = _tile(R, max(16, (1024 * 1024) // lanes // 16 * 16), 16)

    def body(w_ref, o_ref):
        o_ref[...] = w_ref[...].astype(bf16)

    return pl.pallas_call(
        body, name=name, grid=(R // tr,), in_specs=[BS((None, tr, C), lambda r: (layer, r, 0))],
        out_specs=BS((tr, C), lambda r: (r, 0)), out_shape=S((R, C), bf16),
    )(w_all)


def _win_assemble(name, g_win, A, H):
    _, D, DP4 = g_win.shape
    NM = N_CHIPS * DP4 - H
    tr = _tile(D, 256, 16)

    def body(g_ref, m_ref, f_ref):
        full = jnp.concatenate([g_ref[j] for j in range(N_CHIPS)], axis=1)
        m_ref[...] = jnp.concatenate([full[:, :3 * A], full[:, 3 * A + H:]], axis=1)
        f_ref[...] = jnp.concatenate([full[:, 3 * A:3 * A + H], jnp.zeros((tr, HEAD - H), bf16)], axis=1)

    return pl.pallas_call(
        body, name=name, grid=(D // tr,), in_specs=[BS((N_CHIPS, tr, DP4), lambda i: (0, i, 0))],
        out_specs=[BS((tr, NM), lambda i: (i, 0)), BS((tr, HEAD), lambda i: (i, 0))],
        out_shape=[S((D, NM), bf16), S((D, HEAD), bf16)],
    )(g_win)


def _dwin_split(name, d_wmain, d_wf, A, H):
    D, NM = d_wmain.shape
    DP4 = (NM + H) // N_CHIPS
    tr = _tile(D, 256, 8)

    def body(m_ref, f_ref, o_ref):
        m = m_ref[...]
        full = jnp.concatenate([m[:, :3 * A], f_ref[:, :H], m[:, 3 * A:]], axis=1)
        for j in range(N_CHIPS):
            o_ref[j] = full[:, j * DP4:(j + 1) * DP4]

    return pl.pallas_call(
        body, name=name, grid=(D // tr,),
        in_specs=[BS((tr, NM), lambda i: (i, 0)), BS((tr, HEAD), lambda i: (i, 0))],
        out_specs=BS((N_CHIPS, tr, DP4), lambda i: (0, i, 0)), out_shape=S((N_CHIPS, D, DP4), f32),
        compiler_params=_params(),
    )(d_wmain, d_wf)


def _gelu_and_grad(x):
    k0, k1 = 0.7978845608028654, 0.044715
    th = jnp.tanh(k0 * (x + k1 * x * x * x))
    val = 0.5 * x * (1.0 + th)
    grad = 0.5 * (1.0 + th) + 0.5 * x * (1.0 - th * th) * (k0 * (1.0 + 3.0 * k1 * x * x))
    return val, grad


def _fgate_fwd(name, pf, fb):
    T = pf.shape[0]

    def body(pf_ref, fb_ref, c_ref, ct_ref):
        xv = jax.nn.log_sigmoid(pf_ref[...] + fb_ref[...])
        row = lax.broadcasted_iota(jnp.int32, xv.shape, 0)
        s = 1
        while s < T:
            xv = xv + jnp.where(row >= s, pltpu.roll(xv, s, 0), 0.0)
            s *= 2
        c_ref[...] = xv
        ct_ref[...] = xv.T

    return pl.pallas_call(body, name=name, out_shape=[S((T, HEAD), f32), S((HEAD, T), f32)])(pf, fb)


def _fgate_bwd(name, dct, pf, fb):
    T = pf.shape[0]

    def body(dct_ref, pf_ref, fb_ref, dpf_ref, dfb_ref):
        xv = dct_ref[...].T
        row = lax.broadcasted_iota(jnp.int32, xv.shape, 0)
        s = 1
        while s < T:
            xv = xv + jnp.where(row + s < T, pltpu.roll(xv, T - s, 0), 0.0)
            s *= 2
        df = xv * jax.nn.sigmoid(-(pf_ref[...] + fb_ref[...]))
        dpf_ref[...] = df.astype(bf16)
        dfb_ref[...] = jnp.sum(df, axis=0, keepdims=True)

    return pl.pallas_call(body, name=name, out_shape=[S((T, HEAD), bf16), S((1, HEAD), f32)])(dct, pf, fb)


def _qk_norm(name, P, qg, kg, A):
    T = P.shape[0]
    tr = _tile(T, 256, 16)
    n_heads = A // HEAD

    def body(q_ref, k_ref, v_ref, qg_ref, kg_ref, qn_ref, kn_ref, vb_ref):
        for h in range(n_heads):
            sl = slice(h * HEAD, (h + 1) * HEAD)
            for src, g_ref, dst in ((q_ref, qg_ref, qn_ref), (k_ref, kg_ref, kn_ref)):
                xv = src[:, sl]
                r = lax.rsqrt(jnp.mean(xv * xv, axis=-1, keepdims=True) + EPS)
                dst[:, sl] = (xv * r * g_ref[...]).astype(bf16)
        vb_ref[...] = v_ref[...].astype(bf16)

    vec = BS((1, HEAD), lambda i: (0, 0))
    out = BS((tr, A), lambda i: (i, 0))
    return pl.pallas_call(
        body, name=name, grid=(T // tr,),
        in_specs=[BS((tr, A), lambda i: (i, 0)), BS((tr, A), lambda i: (i, 1)), BS((tr, A), lambda i: (i, 2)), vec, vec],
        out_specs=[out, out, out], out_shape=[S((T, A), bf16)] * 3,
    )(P, P, P, qg, kg)


def _attn_fwd(name, qn, kn, vb, c_col, c_row, tb, mix_width):
    T, A = qn.shape
    H = A // HEAD
    nb = T // tb
    scale = HEAD ** -0.5
    hp = _heads_per_program(H, 4)
    wide = hp * HEAD

    def body(q_ref, k_ref, v_ref, cq_ref, ck_ref, o_ref, o32_ref, lse_ref):
        i = pl.program_id(1)
        below = lax.broadcasted_iota(jnp.int32, (tb, tb), 0) >= lax.broadcasted_iota(jnp.int32, (tb, tb), 1)

        def block(j, carry, diagonal):
            koff = pl.multiple_of(j * tb, tb)
            out = []
            for hh in range(hp):
                m, l, acc = carry[hh]
                sl = slice(hh * HEAD, (hh + 1) * HEAD)
                k = k_ref[pl.ds(koff, tb), sl]
                v = v_ref[pl.ds(koff, tb), sl]
                s = _dot(q_ref[:, sl], k, "nt") * scale + (cq_ref[hh] - ck_ref[hh, j])
                if diagonal:
                    s = jnp.where(below, s, NEG)
                m_new = jnp.maximum(m, jnp.max(s, axis=-1, keepdims=True))
                alpha = jnp.exp(m - m_new)
                p = jnp.exp(s - m_new)
                l = l * alpha + jnp.sum(p, axis=-1, keepdims=True)
                acc = acc * alpha + _dot(p, v, "nn")
                out.append((m_new, l, acc))
            return tuple(out)

        init = tuple((jnp.full((tb, 1), NEG, f32), jnp.zeros((tb, 1), f32), jnp.zeros((tb, HEAD), f32))
                     for _ in range(hp))
        carry = lax.fori_loop(0, i, lambda j, c: block(j, c, False), init)
        carry = block(i, carry, True)
        for hh in range(hp):
            m, l, acc = carry[hh]
            sl = slice(hh * HEAD, (hh + 1) * HEAD)
            o = acc / l
            o_ref[:, sl] = o.astype(bf16)
            o32_ref[:, sl] = o
            lse_ref[hh] = m + jnp.log(l)

    return pl.pallas_call(
        body, name=name, grid=(H // hp, nb),
        in_specs=[BS((tb, wide), lambda h, i: (i, h)), BS((T, wide), lambda h, i: (0, h)),
                  BS((T, wide), lambda h, i: (0, h)), BS((hp, tb, 1), lambda h, i: (h, i, 0)),
                  BS((hp, nb, 1, tb), lambda h, i: (h, 0, 0, 0))],
        out_specs=[BS((tb, wide), lambda h, i: (i, h)), BS((tb, wide), lambda h, i: (i, h)),
                   BS((hp, tb, 1), lambda h, i: (h, i, 0))],
        out_shape=[S((T, mix_width), bf16), S((T, A), f32), S((H, T, 1), f32)],
    )(qn, kn, vb, c_col, c_row)


def _attn_bwd(name, qn, kn, vb, o, dmix, lse, c_col, c_row, P, qg, kg, tb):
    T, A = qn.shape
    H = A // HEAD
    nb = T // tb
    scale = HEAD ** -0.5
    hp = _heads_per_program(H)
    wide = hp * HEAD

    def body(q_ref, k_ref, v_ref, o_ref, do_ref, lse_ref, cq_ref, ck_ref, qraw_ref, kraw_ref, qg_ref, kg_ref,
             dq_out, dk_out, dv_out, dc_out, dqg_out, dkg_out, dq_acc, dk_acc, delta_s):
        h = pl.program_id(0)
        dq_acc[...] = jnp.zeros_like(dq_acc)
        below = lax.broadcasted_iota(jnp.int32, (tb, tb), 0) >= lax.broadcasted_iota(jnp.int32, (tb, tb), 1)
        for hh in range(hp):
            sl = slice(hh * HEAD, (hh + 1) * HEAD)
            delta_s[hh] = jnp.sum(do_ref[:, sl].astype(bf16).astype(f32) * o_ref[:, sl], axis=-1, keepdims=True)

        def kblock(j, _):
            koff = pl.multiple_of(j * tb, tb)

            def products(i, hh):
                sl = slice(hh * HEAD, (hh + 1) * HEAD)
                qoff = pl.multiple_of(i * tb, tb)
                return (_dot(q_ref[pl.ds(qoff, tb), sl], k_ref[pl.ds(koff, tb), sl], "nt"),
                        _dot(do_ref[pl.ds(qoff, tb), sl], v_ref[pl.ds(koff, tb), sl], "nt"))

            def qblock(i, carry, diagonal):
                qoff = pl.multiple_of(i * tb, tb)
                out = []
                for hh in range(hp):
                    dk, dv, dc, qk_i, dp = carry[hh]
                    sl = slice(hh * HEAD, (hh + 1) * HEAD)
                    ahead = products(jnp.minimum(i + 1, nb - 1), hh)
                    k = k_ref[pl.ds(koff, tb), sl]
                    q = q_ref[pl.ds(qoff, tb), sl]
                    do = do_ref[pl.ds(qoff, tb), sl].astype(bf16)
                    s = qk_i * scale + (cq_ref[hh, pl.ds(qoff, tb), :] - ck_ref[hh, j])
                    if diagonal:
                        s = jnp.where(below, s, NEG)
                    p = jnp.exp(s - lse_ref[hh, pl.ds(qoff, tb), :])
                    dv = dv + _dot(p, do, "tn")
                    ds = p * (dp - delta_s[hh, pl.ds(qoff, tb), :])
                    dc = dc - jnp.sum(ds, axis=0, keepdims=True)
                    dsb = (ds * scale).astype(bf16)
                    dk = dk + _dot(dsb, q, "tn")
                    dq_acc[pl.ds(qoff, tb), sl] += _dot(dsb, k, "nn")
                    out.append((dk, dv, dc, *ahead))
                return tuple(out)

            init = tuple((jnp.zeros((tb, HEAD), f32), jnp.zeros((tb, HEAD), f32), jnp.zeros((1, tb), f32),
                          *products(j, hh)) for hh in range(hp))
            carry = qblock(j, init, True)
            carry = lax.fori_loop(j + 1, nb, lambda i, c: qblock(i, c, False), carry)
            for hh in range(hp):
                dk, dv, dc = carry[hh][:3]
                sl = slice(hh * HEAD, (hh + 1) * HEAD)
                dk_acc[pl.ds(koff, tb), sl] = dk
                dv_out[pl.ds(koff, tb), sl] = dv.astype(bf16)
                dc_out[hh, j] = dc
            return 0

        lax.fori_loop(0, nb, kblock, 0)

        for raw_ref, g_ref, acc_ref, d_out, dg_out in ((qraw_ref, qg_ref, dq_acc, dq_out, dqg_out),
                                                       (kraw_ref, kg_ref, dk_acc, dk_out, dkg_out)):
            part = jnp.zeros((1, HEAD), f32)
            for hh in range(hp):
                sl = slice(hh * HEAD, (hh + 1) * HEAD)
                xv = raw_ref[:, sl]
                r = lax.rsqrt(jnp.mean(xv * xv, axis=-1, keepdims=True) + EPS)
                xh = xv * r
                dn = acc_ref[:, sl]
                dxh = dn * g_ref[...]
                d_out[:, sl] = (r * (dxh - xh * jnp.mean(dxh * xh, axis=-1, keepdims=True))).astype(bf16)
                part = part + jnp.sum(dn * xh, axis=0, keepdims=True)

            @pl.when(h == 0)
            def _():
                dg_out[...] = part

            @pl.when(h > 0)
            def _():
                dg_out[...] += part

    heads = lambda off: BS((T, wide), lambda h: (0, off + h))
    col = BS((hp, T, 1), lambda h: (h, 0, 0))
    row = BS((hp, nb, 1, tb), lambda h: (h, 0, 0, 0))
    vec = BS((1, HEAD), lambda h: (0, 0))
    return pl.pallas_call(
        body, name=name, grid=(H // hp,),
        in_specs=[heads(0), heads(0), heads(0), heads(0), heads(0), col, col, row, heads(0), heads(H // hp), vec, vec],
        out_specs=[heads(0), heads(0), heads(0), row, vec, vec],
        out_shape=[S((T, A), bf16)] * 3 + [S((H, nb, 1, tb), f32), S((1, HEAD), f32), S((1, HEAD), f32)],
        scratch_shapes=[pltpu.VMEM((T, wide), f32), pltpu.VMEM((T, wide), f32), pltpu.VMEM((hp, T, 1), f32)],
        compiler_params=_params(),
    )(qn, kn, vb, o, dmix, lse, c_col, c_row, P, P, qg, kg)


def _gmlp_fwd(name, P, mix, gain, ws, b, col_u, col_v, col_y, Wd):
    T = P.shape[0]
    G = Wd // HEAD
    tr = _tile(T, 512, HEAD)

    def body(u_ref, v_ref, gain_ref, ws_ref, b_ref, mix_ref, y_ref):
        tril = lax.broadcasted_iota(jnp.int32, (HEAD, HEAD), 0) >= lax.broadcasted_iota(jnp.int32, (HEAD, HEAD), 1)
        wm = jnp.where(tril, ws_ref[...], 0.0).astype(bf16)
        for n in range(tr // HEAD):
            rows = slice(n * HEAD, (n + 1) * HEAD)
            u = jax.nn.gelu(u_ref[rows, :])
            a = jax.nn.gelu(v_ref[rows, :])
            r = lax.rsqrt(jnp.mean(a * a, axis=-1, keepdims=True) + EPS)
            vn = a * r * gain_ref[...]
            mixed = _dot(wm, vn, "nn") + b_ref[...]
            y_ref[rows, :] = (u * mixed).astype(bf16)

    return pl.pallas_call(
        body, name=name, grid=(G, T // tr),
        in_specs=[BS((tr, HEAD), lambda g, i: (i, col_u + g)), BS((tr, HEAD), lambda g, i: (i, col_v + g)),
                  BS((None, 1, HEAD), lambda g, i: (g, 0, 0)), BS((None, HEAD, HEAD), lambda g, i: (g, 0, 0)),
                  BS((None, HEAD, 1), lambda g, i: (g, 0, 0)), ANY],
        out_specs=BS((tr, HEAD), lambda g, i: (i, col_y + g)), out_shape=S(mix.shape, bf16),
        input_output_aliases={5: 0},
    )(P, P, gain, ws, b, mix)


def _gmlp_bwd(name, P, dmix, gain, ws, b, col_u, col_v, col_dy, Wd):
    T = P.shape[0]
    G = Wd // HEAD
    tr = _tile(T, 512, HEAD)

    def body(u_ref, v_ref, dy_ref, gain_ref, ws_ref, b_ref, du_ref, dv_ref, dws_ref, db_ref, dgain_ref):
        i = pl.program_id(1)
        tril = lax.broadcasted_iota(jnp.int32, (HEAD, HEAD), 0) >= lax.broadcasted_iota(jnp.int32, (HEAD, HEAD), 1)
        wm = jnp.where(tril, ws_ref[...], 0.0).astype(bf16)
        gain_v = gain_ref[...]
        dw = jnp.zeros((HEAD, HEAD), f32)
        db = jnp.zeros((HEAD, 1), f32)
        dgain = jnp.zeros((1, HEAD), f32)
        for n in range(tr // HEAD):
            rows = slice(n * HEAD, (n + 1) * HEAD)
            u, du_dx = _gelu_and_grad(u_ref[rows, :])
            a, da_dx = _gelu_and_grad(v_ref[rows, :])
            dy = dy_ref[rows, :]
            r = lax.rsqrt(jnp.mean(a * a, axis=-1, keepdims=True) + EPS)
            ah = a * r
            vnb = (ah * gain_v).astype(bf16)
            mixed = _dot(wm, vnb, "nn") + b_ref[...]
            dm = dy * u
            dmb = dm.astype(bf16)
            du_ref[rows, :] = (dy * mixed * du_dx).astype(bf16)
            db = db + jnp.sum(dm, axis=1, keepdims=True)
            dw = dw + _dot(dmb, vnb, "nt")
            dvn = _dot(wm, dmb, "tn")
            dgain = dgain + jnp.sum(dvn * ah, axis=0, keepdims=True)
            dah = dvn * gain_v
            da = r * (dah - ah * jnp.mean(dah * ah, axis=-1, keepdims=True))
            dv_ref[rows, :] = (da * da_dx).astype(bf16)
        dw = jnp.where(tril, dw, 0.0)

        @pl.when(i == 0)
        def _():
            dws_ref[...] = dw
            db_ref[...] = db
            dgain_ref[...] = dgain

        @pl.when(i > 0)
        def _():
            dws_ref[...] += dw
            db_ref[...] += db
            dgain_ref[...] += dgain

    out = BS((tr, HEAD), lambda g, i: (i, g))
    return pl.pallas_call(
        body, name=name, grid=(G, T // tr),
        in_specs=[BS((tr, HEAD), lambda g, i: (i, col_u + g)), BS((tr, HEAD), lambda g, i: (i, col_v + g)),
                  BS((tr, HEAD), lambda g, i: (i, col_dy + g)),
                  BS((None, 1, HEAD), lambda g, i: (g, 0, 0)), BS((None, HEAD, HEAD), lambda g, i: (g, 0, 0)),
                  BS((None, HEAD, 1), lambda g, i: (g, 0, 0))],
        out_specs=[out, out, BS((None, HEAD, HEAD), lambda g, i: (g, 0, 0)), BS((None, HEAD, 1), lambda g, i: (g, 0, 0)),
                   BS((None, 1, HEAD), lambda g, i: (g, 0, 0))],
        out_shape=[S((T, Wd), bf16), S((T, Wd), bf16), S((G, HEAD, HEAD), f32), S((G, HEAD, 1), f32),
                   S((G, 1, HEAD), f32)],
    )(P, P, dmix, gain, ws, b)


def _trailing_window(xv, w, row):
    k = 1
    while k < w:
        xv = xv + jnp.where(row >= k, pltpu.roll(xv, k, 0), 0.0)
        k *= 2
    return xv


def _leading_window(xv, w, row, T):
    k = 1
    while k < w:
        xv = xv + jnp.where(row + k < T, pltpu.roll(xv, T - k, 0), 0.0)
        k *= 2
    return xv


def _pool_fwd(name, P, mix, pw, ps, col_x, col_y, Wd):
    T = P.shape[0]
    Gp = Wd // HEAD

    def body(x_ref, pw_ref, ps_ref, mix_ref, y_ref):
        row = lax.broadcasted_iota(jnp.int32, (T, HEAD), 0)
        for g in range(Gp):
            w = POOL_WINDOWS[g]
            sl = slice(g * HEAD, (g + 1) * HEAD)
            xv = x_ref[:, sl]
            cnt = jnp.minimum(row + 1, w).astype(f32)
            d = _trailing_window(xv, w, row) / cnt - xv
            y_ref[:, sl] = (_dot(d, pw_ref[g], "nn") * ps_ref[:, sl]).astype(bf16)

    return pl.pallas_call(
        body, name=name, grid=(1,),
        in_specs=[BS((T, Wd), lambda i: (0, col_x)), BS((Gp, HEAD, HEAD), lambda i: (0, 0, 0)), BS((1, Wd), lambda i: (0, 0)),
                  ANY],
        out_specs=BS((T, Wd), lambda i: (0, col_y)), out_shape=S(mix.shape, bf16), input_output_aliases={3: 0},
        compiler_params=_params(),
    )(P, pw, ps, mix)


def _pool_bwd(name, P, dmix, pw, ps, col_x, col_dy, Wd):
    T = P.shape[0]
    Gp = Wd // HEAD

    def body(x_ref, dy_ref, pw_ref, ps_ref, dx_ref, dpw_ref, dps_ref):
        row = lax.broadcasted_iota(jnp.int32, (T, HEAD), 0)
        for g in range(Gp):
            w = POOL_WINDOWS[g]
            sl = slice(g * HEAD, (g + 1) * HEAD)
            xv = x_ref[:, sl]
            cnt = jnp.minimum(row + 1, w).astype(f32)
            d = (_trailing_window(xv, w, row) / cnt - xv).astype(bf16)
            pwb = pw_ref[g].astype(bf16)
            z = _dot(d, pwb, "nn")
            dy = dy_ref[:, sl]
            dps_ref[:, sl] = jnp.sum(dy * z, axis=0, keepdims=True)
            dzb = (dy * ps_ref[:, sl]).astype(bf16)
            dpw_ref[g] = _dot(d, dzb, "tn")
            dd = _dot(dzb, pwb, "nt")
            dx_ref[:, sl] = (_leading_window(dd / cnt, w, row, T) - dd).astype(bf16)

    return pl.pallas_call(
        body, name=name, grid=(1,),
        in_specs=[BS((T, Wd), lambda i: (0, col_x)), BS((T, Wd), lambda i: (0, col_dy)),
                  BS((Gp, HEAD, HEAD), lambda i: (0, 0, 0)), BS((1, Wd), lambda i: (0, 0))],
        out_specs=[BS((T, Wd), lambda i: (0, 0)), BS((Gp, HEAD, HEAD), lambda i: (0, 0, 0)), BS((1, Wd), lambda i: (0, 0))],
        out_shape=[S((T, Wd), bf16), S((Gp, HEAD, HEAD), f32), S((1, Wd), f32)], compiler_params=_params(),
    )(P, dmix, pw, ps)


def _adamw(name, w, g, m, v):
    R, C = w.shape
    lanes = -(-C // 128) * 128
    tr = _tile(R, max(8, (512 * 1024) // lanes // 8 * 8), 8)
    c1 = 1.0 - ADAM_B1 ** ADAM_STEP
    c2 = 1.0 - ADAM_B2 ** ADAM_STEP

    def body(w_ref, g_ref, m_ref, v_ref, d_ref, nm_ref, nv_ref):
        gv = g_ref[...]
        nm = ADAM_B1 * m_ref[...] + (1.0 - ADAM_B1) * gv
        nv = ADAM_B2 * v_ref[...] + (1.0 - ADAM_B2) * (gv * gv)
        d_ref[...] = -ADAM_LR * ((nm / c1) / (jnp.sqrt(nv / c2) + ADAM_EPS) + ADAM_WD * w_ref[...])
        nm_ref[...] = nm
        nv_ref[...] = nv

    blk = BS((tr, C), lambda i: (i, 0))
    return pl.pallas_call(
        body, name=name, grid=(R // tr,), in_specs=[blk] * 4, out_specs=[blk] * 3, out_shape=[S((R, C), f32)] * 3,
    )(w, g, m, v)


def _adamw_layer(name, layer, w_all, m_all, v_all, g, prev):
    L, R, C = w_all.shape
    lanes = -(-C // 128) * 128
    tr = _tile(R, max(8, (768 * 1024) // lanes // 8 * 8), 8)
    c1 = 1.0 - ADAM_B1 ** ADAM_STEP
    c2 = 1.0 - ADAM_B2 ** ADAM_STEP
    n_prev = 0 if prev is None else 4

    def body(w_ref, m_ref, v_ref, g_ref, *rest):
        go_ref, d_ref, nm_ref, nv_ref = rest[n_prev:]
        gv = g_ref[...]
        nm = ADAM_B1 * m_ref[...] + (1.0 - ADAM_B1) * gv
        nv = ADAM_B2 * v_ref[...] + (1.0 - ADAM_B2) * (gv * gv)
        d_ref[...] = -ADAM_LR * ((nm / c1) / (jnp.sqrt(nv / c2) + ADAM_EPS) + ADAM_WD * w_ref[...])
        nm_ref[...] = nm
        nv_ref[...] = nv
        go_ref[...] = gv

    slab = BS((None, tr, C), lambda r: (layer, r, 0))
    return pl.pallas_call(
        body, name=name, grid=(R // tr,),
        in_specs=[slab, slab, slab, BS((tr, C), lambda r: (r, 0))] + [ANY] * n_prev,
        out_specs=[slab] * 4, out_shape=[S((L, R, C), f32)] * 4,
        input_output_aliases={4 + k: k for k in range(n_prev)},
    )(w_all, m_all, v_all, g, *(prev or ()))


def _chip_of(k):
    return k // 2, k % 2


def _remote(src, dst, send_sems, recv_sems, idx, dev):
    return pltpu.make_async_remote_copy(src_ref=src, dst_ref=dst, send_sem=send_sems.at[idx], recv_sem=recv_sems.at[idx],
                                        device_id=dev, device_id_type=MESH)


def _plan_gather_near(n):
    def plan(refs, ss, rs, base):
        ins, lands = refs[:n], refs[n:]
        x, y, c, j0 = _my_place()
        sib = (x, y, 1 - c)
        sends, recvs = [], []
        for a in range(n):
            half = ins[a].shape[0] // 2
            lo = c * half
            sends.append(_remote(ins[a], lands[a].at[j0], ss, rs, base + 3 * a + 2, sib))
            recvs.append(_remote(lands[a].at[j0], lands[a].at[j0], ss, rs, base + 3 * a + 2, sib))
            for r in (1, 2):
                k = j0 ^ r
                dev = (*_chip_of(k), c)
                sends.append(_remote(ins[a].at[pl.ds(lo, half)], lands[a].at[j0, pl.ds(lo, half)], ss, rs,
                                     base + 3 * a + r - 1, dev))
                landed = lands[a].at[k, pl.ds(lo, half)]
                recvs.append(_remote(landed, landed, ss, rs, base + 3 * a + r - 1, dev))
        return sends, recvs
    return plan, 3 * n


def _plan_gather_relay(n):
    def plan(refs, ss, rs, base):
        x, y, c, j0 = _my_place()
        sib = (x, y, 1 - c)
        sends, recvs = [], []
        for a in range(n):
            half = refs[a].shape[1] // 2
            quarter = half // 2
            lo = c * half
            far = j0 ^ 3
            for r, to, off in ((1, 2, 0), (2, 1, quarter)):
                dev = (*_chip_of(j0 ^ to), c)
                piece = refs[a].at[j0 ^ r, pl.ds(lo + off, quarter)]
                sends.append(_remote(piece, piece, ss, rs, base + 4 * a + to - 1, dev))
                lands_here = refs[a].at[far, pl.ds(lo + off, quarter)]
                recvs.append(_remote(lands_here, lands_here, ss, rs, base + 4 * a + to - 1, dev))
                mine = refs[a].at[j0 ^ r, pl.ds(lo, half)]
                theirs = refs[a].at[j0 ^ r, pl.ds((1 - c) * half, half)]
                sends.append(_remote(mine, mine, ss, rs, base + 4 * a + 1 + r, sib))
                recvs.append(_remote(theirs, theirs, ss, rs, base + 4 * a + 1 + r, sib))
        return sends, recvs
    return plan, 4 * n


def _plan_gather_far(n):
    def plan(refs, ss, rs, base):
        x, y, c, j0 = _my_place()
        sib = (x, y, 1 - c)
        sends, recvs = [], []
        for a in range(n):
            half = refs[a].shape[1] // 2
            mine = refs[a].at[j0 ^ 3, pl.ds(c * half, half)]
            theirs = refs[a].at[j0 ^ 3, pl.ds((1 - c) * half, half)]
            sends.append(_remote(mine, mine, ss, rs, base + a, sib))
            recvs.append(_remote(theirs, theirs, ss, rs, base + a, sib))
        return sends, recvs
    return plan, n


def _plan_sibling_halves(n):
    def plan(refs, ss, rs, base):
        ins, lands = refs[:n], refs[n:]
        x, y, c, _ = _my_place()
        sib = (x, y, 1 - c)
        sends, recvs = [], []
        for a in range(n):
            half = ins[a].shape[1] // 2
            sends.append(_remote(ins[a].at[:, pl.ds((1 - c) * half, half), :], lands[a], ss, rs, base + a, sib))
            recvs.append(_remote(lands[a], lands[a], ss, rs, base + a, sib))
        return sends, recvs
    return plan, n


def _plan_chip_scatter(n):
    def plan(refs, ss, rs, base):
        ins, lands = refs[:n], refs[n:]
        x, y, c, j0 = _my_place()
        sends, recvs = [], []
        for a in range(n):
            for r in (1, 2, 3):
                k = j0 ^ r
                dev = (*_chip_of(k), c)
                sends.append(_remote(ins[a].at[k], lands[a].at[j0], ss, rs, base + 3 * a + r - 1, dev))
                recvs.append(_remote(lands[a].at[k], lands[a].at[k], ss, rs, base + 3 * a + r - 1, dev))
        return sends, recvs
    return plan, 3 * n


def _plan_sibling_join(n):
    def plan(refs, ss, rs, base):
        x, y, c, _ = _my_place()
        sib = (x, y, 1 - c)
        sends, recvs = [], []
        for a in range(n):
            half = refs[a].shape[0] // 2
            mine = refs[a].at[pl.ds(c * half, half)]
            theirs = refs[a].at[pl.ds((1 - c) * half, half)]
            sends.append(_remote(mine, mine, ss, rs, base + a, sib))
            recvs.append(_remote(theirs, theirs, ss, rs, base + a, sib))
        return sends, recvs
    return plan, n


_HBM = pl.BlockSpec(memory_space=pltpu.HBM)
_SEM = pl.BlockSpec(memory_space=pltpu.SEMAPHORE)
_EFFECT = pltpu.SideEffectType.DATAFLOW_SIDE_EFFECTING


def _exchange_start(name, plan, bufs, after):
    plan_fn, n_sems = plan
    n = len(bufs)

    def body(*refs):
        ss, rs, token = refs[n + len(after)], refs[n + len(after) + 1], refs[-1]
        sends, _ = plan_fn(refs[:n], ss, rs, 0)
        for cp in sends:
            cp.start()
        token[...] = jnp.zeros_like(token)

    res = pl.pallas_call(
        body, name=name,
        out_shape=(pltpu.SemaphoreType.DMA((n_sems,)), pltpu.SemaphoreType.DMA((n_sems,)),
                   *[pltpu.HBM(b.shape, b.dtype) for b in bufs], S((8, 128), f32)),
        in_specs=[_HBM] * n + [ANY] * len(after),
        out_specs=(_SEM, _SEM, *[_HBM] * n, pl.BlockSpec(memory_space=pltpu.VMEM)),
        input_output_aliases={k: 2 + k for k in range(n)},
        compiler_params=pltpu.CompilerParams(has_side_effects=_EFFECT),
    )(*[pltpu.with_memory_space_constraint(b, pltpu.HBM) for b in bufs], *after)
    return res[0], res[1], list(res[2:2 + n]), res[-1]


def _exchange_wait(name, plan, send_sems, recv_sems, bufs, after):
    plan_fn, _ = plan
    n = len(bufs)

    def body(*refs):
        ss, rs, token = refs[n], refs[n + 1], refs[-1]
        sends, recvs = plan_fn(refs[:n], ss, rs, 0)
        for cp in recvs:
            cp.wait_recv()
        for cp in sends:
            cp.wait_send()
        token[...] = jnp.zeros_like(token)

    res = pl.pallas_call(
        body, name=name,
        out_shape=(*[pltpu.HBM(b.shape, b.dtype) for b in bufs], S((8, 128), f32)),
        in_specs=[_HBM] * n + [_SEM, _SEM] + [ANY] * len(after),
        out_specs=(*[_HBM] * n, pl.BlockSpec(memory_space=pltpu.VMEM)),
        input_output_aliases={k: k for k in range(n)},
        compiler_params=pltpu.CompilerParams(has_side_effects=_EFFECT),
    )(*bufs, send_sems, recv_sems, *after)
    return list(res[:n]), res[-1]


class _Order:
    def __init__(self, first):
        self.marker = first
        self.token = None

    def _after(self):
        return [self.marker] + ([] if self.token is None else [self.token])

    def start(self, name, plan, bufs):
        ss, rs, thru, self.token = _exchange_start(name, plan, bufs, self._after())
        return name, plan, ss, rs, thru

    def wait(self, handle):
        name, plan, ss, rs, thru = handle
        out, self.token = _exchange_wait(name + "_wait", plan, ss, rs, thru, self._after())
        return out

    def follows(self, small):
        return small if self.token is None else small + self.token[0, 0]

    def done(self, result):
        self.marker = result[(slice(0, 1),) * result.ndim].reshape(1, 1)


def _all_reduce_small(name, g8):
    _, R, L = g8.shape

    def body(g_ref, out_ref, land, red, send1, recv1, send2, recv2):
        x, y, c, _ = _my_place()
        me = 4 * x + 2 * y + c
        peers = []
        for r in range(1, N_DEV):
            q = me ^ r
            peers.append((q, (q // 4, (q // 2) % 2, q % 2)))
        first = []
        for r, (q, dev) in enumerate(peers):
            cp = pltpu.make_async_remote_copy(src_ref=g_ref.at[q], dst_ref=land.at[me], send_sem=send1.at[r],
                                              recv_sem=recv1.at[r], device_id=dev, device_id_type=MESH)
            cp.start()
            first.append(cp)
        land[me] = g_ref[me]
        for r, (q, dev) in enumerate(peers):
            pltpu.make_async_remote_copy(src_ref=land.at[q], dst_ref=land.at[q], send_sem=send1.at[r],
                                         recv_sem=recv1.at[r], device_id=dev, device_id_type=MESH).wait_recv()
        acc = land[0]
        for d in range(1, N_DEV):
            acc = acc + land[d]
        red[...] = acc
        out_ref[me] = acc
        second = []
        for r, (q, dev) in enumerate(peers):
            cp = pltpu.make_async_remote_copy(src_ref=red, dst_ref=out_ref.at[me], send_sem=send2.at[r],
                                              recv_sem=recv2.at[r], device_id=dev, device_id_type=MESH)
            cp.start()
            second.append(cp)
        for r, (q, dev) in enumerate(peers):
            pltpu.make_async_remote_copy(src_ref=out_ref.at[q], dst_ref=out_ref.at[q], send_sem=send2.at[r],
                                         recv_sem=recv2.at[r], device_id=dev, device_id_type=MESH).wait_recv()
        for cp in first + second:
            cp.wait_send()

    vm = pl.BlockSpec(memory_space=pltpu.VMEM)
    return pl.pallas_call(
        body, name=name, in_specs=[vm], out_specs=vm, out_shape=S(g8.shape, f32),
        scratch_shapes=[pltpu.VMEM((N_DEV, R, L), f32), pltpu.VMEM((R, L), f32)]
        + [pltpu.SemaphoreType.DMA((N_DEV - 1,))] * 4,
        compiler_params=_params(),
    )(g8)


def _pair_sum(name, g4, sib):
    _, rows, cols = g4.shape
    half = rows // 2
    lanes = -(-cols // 128) * 128
    tr = _tile(half, max(16, (1024 * 1024) // lanes // 16 * 16), 16)
    nb = half // tr

    def body(g_ref, s_ref, pb_ref, own_ref):
        j = pl.program_id(1)
        t = g_ref[...] + s_ref[...]
        pb_ref[...] = t.astype(bf16)

        @pl.when(j == _my_place()[3])
        def _():
            own_ref[...] = t

    return pl.pallas_call(
        body, name=name, grid=(nb, N_CHIPS),
        in_specs=[BS((None, tr, cols), lambda i, j: (j, lax.axis_index("c") * nb + i, 0)),
                  BS((None, tr, cols), lambda i, j: (j, i, 0))],
        out_specs=[BS((None, tr, cols), lambda i, j: (j, i, 0)), BS((tr, cols), lambda i, j: (i, 0))],
        out_shape=[S((N_CHIPS, half, cols), bf16), S((half, cols), f32)],
    )(g4, sib)


def _chip_sum(name, own, got):
    half, cols = own.shape
    lanes = -(-cols // 128) * 128
    tr = _tile(half, max(16, (1024 * 1024) // lanes // 16 * 16), 16)
    nb = half // tr

    def body(own_ref, *rest):
        got_refs, o_ref = rest[:N_CHIPS], rest[N_CHIPS]
        j0 = _my_place()[3]
        acc = None
        for k in range(N_CHIPS):
            t = jnp.where(j0 == k, own_ref[...], got_refs[k][...].astype(f32))
            acc = t if acc is None else acc + t
        o_ref[...] = acc

    def slot(k):
        return BS((None, tr, cols), lambda i: (jnp.where(_my_place()[3] == k, (k + 1) % N_CHIPS, k), i, 0))

    return pl.pallas_call(
        body, name=name, grid=(nb,),
        in_specs=[BS((tr, cols), lambda i: (i, 0))] + [slot(k) for k in range(N_CHIPS)],
        out_specs=BS((tr, cols), lambda i: (lax.axis_index("c") * nb + i, 0)),
        out_shape=S((2 * half, cols), f32),
    )(own, got, got, got, got)


def _rows_of(size):
    return -(-size // 1024) * 8


def _pack_rows(arrs, n_rows):
    parts = []
    for a in arrs:
        rows = _rows_of(a.size)
        if a.size % 128 == 0:
            part = a.astype(f32).reshape(-1, 128)
            part = jnp.pad(part, ((0, rows - part.shape[0]), (0, 0)))
        else:
            part = jnp.pad(a.reshape(-1).astype(f32), (0, rows * 128 - a.size)).reshape(rows, 128)
        parts.append(part)
    used = sum(p.shape[0] for p in parts)
    return jnp.concatenate(parts + [jnp.zeros((n_rows - used, 128), f32)], axis=0)


def _unpack_rows(packed, shapes):
    out, row = [], 0
    for shp in shapes:
        size = 1
        for d in shp:
            size *= d
        rows = packed[row:row + _rows_of(size)]
        out.append(rows[:size // 128].reshape(shp) if size % 128 == 0 else rows.reshape(-1)[:size].reshape(shp))
        row += _rows_of(size)
    return out


def kernel(x, p, norm_mix, w_in, q_norm, k_norm, forget_bias, gmlp_v_norm, gmlp_w_s, gmlp_b_s, pool_w, pool_scale, w_out, norm_ffn, w_ffn_gate, w_ffn_up, w_ffn_down, norm_ple, w_ple_gate, w_ple_proj, loss_target, m_norm_mix, m_w_in, m_q_norm, m_k_norm, m_forget_bias, m_gmlp_v_norm, m_gmlp_w_s, m_gmlp_b_s, m_pool_w, m_pool_scale, m_w_out, m_norm_ffn, m_w_ffn_gate, m_w_ffn_up, m_w_ffn_down, m_norm_ple, m_w_ple_gate, m_w_ple_proj, v_norm_mix, v_w_in, v_q_norm, v_k_norm, v_forget_bias, v_gmlp_v_norm, v_gmlp_w_s, v_gmlp_b_s, v_pool_w, v_pool_scale, v_w_out, v_norm_ffn, v_w_ffn_gate, v_w_ffn_up, v_w_ffn_down, v_norm_ple, v_w_ple_gate, v_w_ple_proj):
    W = dict(norm_mix=norm_mix, w_in=w_in, q_norm=q_norm, k_norm=k_norm, forget_bias=forget_bias,
             gmlp_v_norm=gmlp_v_norm, gmlp_w_s=gmlp_w_s, gmlp_b_s=gmlp_b_s, pool_w=pool_w, pool_scale=pool_scale,
             w_out=w_out, norm_ffn=norm_ffn, w_ffn_gate=w_ffn_gate, w_ffn_up=w_ffn_up, w_ffn_down=w_ffn_down,
             norm_ple=norm_ple, w_ple_gate=w_ple_gate, w_ple_proj=w_ple_proj)
    M = dict(norm_mix=m_norm_mix, w_in=m_w_in, q_norm=m_q_norm, k_norm=m_k_norm, forget_bias=m_forget_bias,
             gmlp_v_norm=m_gmlp_v_norm, gmlp_w_s=m_gmlp_w_s, gmlp_b_s=m_gmlp_b_s, pool_w=m_pool_w,
             pool_scale=m_pool_scale, w_out=m_w_out, norm_ffn=m_norm_ffn, w_ffn_gate=m_w_ffn_gate,
             w_ffn_up=m_w_ffn_up, w_ffn_down=m_w_ffn_down, norm_ple=m_norm_ple, w_ple_gate=m_w_ple_gate,
             w_ple_proj=m_w_ple_proj)
    V = dict(norm_mix=v_norm_mix, w_in=v_w_in, q_norm=v_q_norm, k_norm=v_k_norm, forget_bias=v_forget_bias,
             gmlp_v_norm=v_gmlp_v_norm, gmlp_w_s=v_gmlp_w_s, gmlp_b_s=v_gmlp_b_s, pool_w=v_pool_w,
             pool_scale=v_pool_scale, w_out=v_w_out, norm_ffn=v_norm_ffn, w_ffn_gate=v_w_ffn_gate,
             w_ffn_up=v_w_ffn_up, w_ffn_down=v_w_ffn_down, norm_ple=v_norm_ple, w_ple_gate=v_w_ple_gate,
             w_ple_proj=v_w_ple_proj)

    L = w_in.shape[0]
    _, T, D = x.shape
    A, Wd = D // 2, D // 4
    H = A // HEAD
    G = gmlp_w_s.shape[1]
    Gp = pool_w.shape[1]
    DP4 = w_in.shape[2]
    DP = N_CHIPS * DP4
    NM = 3 * A + 3 * Wd
    FS = w_ffn_gate.shape[2]
    FF = N_CHIPS * FS
    DS = D // N_CHIPS
    PL = p.shape[-1]
    assert Wd // G == HEAD and Wd // Gp == HEAD and DP == NM + H and H <= HEAD
    assert all(w & (w - 1) == 0 for w in POOL_WINDOWS[:Gp])
    tb = _tile(T, 256, HEAD)
    nb = T // tb
    tm = _tile(T, 512, 16)
    tmw = _tile(T, 1024, 16)
    tn = _tile(NM, 512, 128)
    tnd = _tile(D, 512, 128)
    tkd = _tile(D, 1024, 128)
    tnw = _tile(D, 1024, 128)
    col_gu, col_gv, col_xp = 3 * A // HEAD, (3 * A + Wd) // HEAD, (3 * A + 2 * Wd) // Wd
    col_dg, col_dp = A // HEAD, (A + Wd) // Wd

    order = _Order(x[0, :1, :1])

    def gather_near(i, names, tag=""):
        shards = [_cast_layer(f"cast_{n}_{i}", W[n], i) for n in names]
        lands = [lax.empty((N_CHIPS,) + s.shape, bf16) for s in shards]
        return names, order.start(f"ag_near_{i}{tag}", _plan_gather_near(len(names)), shards + lands)

    def gather_relay(i, near, tag=""):
        names, handle = near
        return names, order.start(f"ag_relay_{i}{tag}", _plan_gather_relay(len(names)), order.wait(handle)[len(names):])

    def gather_far(i, relayed, tag=""):
        names, handle = relayed
        return names, order.start(f"ag_far_{i}{tag}", _plan_gather_far(len(names)), order.wait(handle))

    win_count = [0]

    def gathered(far):
        names, handle = far
        g = dict(zip(names, order.wait(handle)))
        out = {}
        if "w_in" in g:
            out["w_main"], out["w_f"] = _win_assemble(f"w_in_cols_{win_count[0]}", g["w_in"], A, H)
            win_count[0] += 1
        if "w_out" in g:
            out["w_out"] = g["w_out"].reshape(D, D)
        if "w_ffn_gate" in g:
            out.update(w_gate=g["w_ffn_gate"], w_up=g["w_ffn_up"], w_down=g["w_ffn_down"].reshape(FF, D),
                       w_pg=g["w_ple_gate"].reshape(D, D), w_pp=g["w_ple_proj"])
        return out

    Wf = [None] * L
    relayed = gather_relay(0, gather_near(0, BIG[:2], "a"), "a")
    near_rest = gather_near(0, BIG[2:], "c")
    Wf[0] = gathered(gather_far(0, relayed, "a"))
    near = relayed = None

    h = x.reshape(T, D)
    pb16 = p.reshape(L, T, PL).astype(bf16)
    saved = []

    for i in range(L):
        w = Wf[i]
        sv = dict(h0=h)
        xn1 = _rms_fwd(f"rms1_{i}", h, order.follows(norm_mix[i]))
        (P,) = _matmul(f"proj_{i}", "nn", (T // tmw, NM // tn),
                       [((xn1, BS((tmw, D), lambda i, j: (i, 0))), (w["w_main"], BS((D, tn), lambda i, j: (0, j))))], [],
                       [((T, NM), f32, BS((tmw, tn), lambda i, j: (i, j)))])
        (Pf,) = _matmul(f"projf_{i}", "nn", (T // tm, 1),
                        [((xn1, BS((tm, D), lambda i, j: (i, 0))), (w["w_f"], BS((D, HEAD), lambda i, j: (0, 0))))], [],
                        [((T, HEAD), f32, BS((tm, HEAD), lambda i, j: (i, 0)))])
        fb = jnp.pad(forget_bias[i], (0, HEAD - H)).reshape(1, HEAD)
        cc, ct = _fgate_fwd(f"fgate_{i}", Pf, fb)
        c_col = ct[:H].reshape(H, T, 1)
        c_row = ct[:H].reshape(H, nb, 1, tb)
        qg, kg = q_norm[i].reshape(1, HEAD), k_norm[i].reshape(1, HEAD)
        qn, kn, vb = _qk_norm(f"qknorm_{i}", P, qg, kg, A)
        mix, o32, lse = _attn_fwd(f"attn_{i}", qn, kn, vb, c_col, c_row, tb, D)
        if i == 0:
            order.done(o32)
            relayed_rest = gather_relay(0, near_rest, "c")
            near = gather_near(1, BIG) if L > 1 else None
        gain = gmlp_v_norm[i].reshape(G, 1, HEAD)
        bs = gmlp_b_s[i].reshape(G, HEAD, 1)
        mix = _gmlp_fwd(f"gmlp_{i}", P, mix, order.follows(gain), gmlp_w_s[i], bs, col_gu, col_gv, col_dg, Wd)
        ps = pool_scale[i].reshape(1, Wd)
        mix = _pool_fwd(f"pool_{i}", P, mix, pool_w[i], ps, col_xp, col_dp, Wd)
        (h1,) = _matmul(f"out_{i}", "nn", (T // tmw, D // tnd),
                        [((mix, BS((tmw, D), lambda i, j: (i, 0))), (w["w_out"], BS((D, tnd), lambda i, j: (0, j))))],
                        [(h, BS((tmw, tnd), lambda i, j: (i, j)))],
                        [((T, D), f32, BS((tmw, tnd), lambda i, j: (i, j)))],
                        epilogue=lambda accs, ex: (accs[0] + ex[0],))
        xn2 = _rms_fwd(f"rms2_{i}", h1, norm_ffn[i])
        order.done(xn2)
        if i == 0:
            w.update(gathered(gather_far(0, relayed_rest, "c")))
        elif i + 1 < L:
            relayed = gather_relay(i + 1, near)
            near = gather_near(i + 2, BIG) if i + 2 < L else None

        def ffn_epi(accs, ex):
            g_, u_ = accs
            return g_, u_, g_ * jax.nn.sigmoid(g_) * u_

        ffo = BS((tm, FS), lambda j, i: (i, j))
        Gt, Ut, act = _matmul(f"ffn1_{i}", "nn", (N_CHIPS, T // tm),
                              [((xn2, BS((tm, D), lambda j, i: (i, 0))), (w["w_gate"], BS((None, D, FS), lambda j, i: (j, 0, 0)))),
                               ((xn2, BS((tm, D), lambda j, i: (i, 0))), (w["w_up"], BS((None, D, FS), lambda j, i: (j, 0, 0))))],
                              [], [((T, FF), bf16, ffo)] * 3, epilogue=ffn_epi, after=order.token)
        (h2,) = _matmul(f"ffn2_{i}", "nn", (T // tmw, D // tnd),
                        [((act, BS((tmw, FF), lambda i, j: (i, 0))), (w["w_down"], BS((FF, tnd), lambda i, j: (0, j))))],
                        [(h1, BS((tmw, tnd), lambda i, j: (i, j)))],
                        [((T, D), f32, BS((tmw, tnd), lambda i, j: (i, j)))],
                        epilogue=lambda accs, ex: (accs[0] + ex[0],))
        far = None
        if i == 0 and L > 1:
            order.done(h2)
            relayed = gather_relay(1, near)
            near = gather_near(2, BIG) if L > 2 else None
        elif i + 1 < L:
            order.done(h2)
            far = gather_far(i + 1, relayed)
        xn3 = _rms_fwd(f"rms3_{i}", h2, order.follows(norm_ple[i]))

        def ple_epi(accs, ex):
            gate = jax.nn.sigmoid(accs[0])
            return ex[0] + accs[1] * gate, gate, accs[1]

        dso = BS((tmw, DS), lambda i, j: (i, j))
        h3, gate, e = _matmul(f"ple_{i}", "nn", (T // tmw, N_CHIPS),
                              [((xn3, BS((tmw, D), lambda i, j: (i, 0))), (w["w_pg"], BS((D, DS), lambda i, j: (0, j)))),
                               ((pb16[i], BS((tmw, PL), lambda i, j: (i, 0))), (w["w_pp"], BS((None, PL, DS), lambda i, j: (j, 0, 0))))],
                              [(h2, dso)], [((T, D), f32, dso), ((T, D), bf16, dso), ((T, D), bf16, dso)], epilogue=ple_epi)
        sv.update(xn1=xn1, P=P, Pf=Pf, fb=fb, c_col=c_col, c_row=c_row, qn=qn, kn=kn, vb=vb, o32=o32, lse=lse,
                  mix=mix, h1=h1, xn2=xn2, Gt=Gt, Ut=Ut, act=act, h2=h2, xn3=xn3, gate=gate, e=e)
        saved.append(sv)
        h = h3
        order.done(h3)
        if i + 1 < L:
            Wf[i + 1] = gathered(far if far is not None else gather_far(i + 1, relayed))

    dh, loss_tile = _loss_grad("loss", h, loss_target.reshape(T, D))

    small_g = {n: [None] * L for n in SMALL}
    big_out = {}

    def stage_a(u):
        n_u = len(u["names"])
        lands = [lax.empty((N_CHIPS, g.shape[1] // 2, g.shape[2]), f32) for g in u["grads"]]
        u["a"] = order.start(f"rs_a_{u['tag']}", _plan_sibling_halves(n_u), u["grads"] + lands)

    def stage_pair(u):
        n_u = len(u["names"])
        out = order.wait(u["a"])
        pairs = [_pair_sum(f"rs_pair_{u['tag']}_{a}", out[a], out[n_u + a]) for a in range(n_u)]
        u["pb"], u["own"] = [t[0] for t in pairs], [t[1] for t in pairs]

    def stage_b(u):
        lands = [lax.empty(t.shape, bf16) for t in u["pb"]]
        u["b"] = order.start(f"rs_b_{u['tag']}", _plan_chip_scatter(len(u["names"])), u["pb"] + lands)

    def stage_sum(u):
        n_u = len(u["names"])
        out = order.wait(u["b"])
        u["sum"] = [_chip_sum(f"rs_sum_{u['tag']}_{a}", u["own"][a], out[n_u + a]) for a in range(n_u)]

    def stage_c(u):
        u["c"] = order.start(f"rs_c_{u['tag']}", _plan_sibling_join(len(u["names"])), u["sum"])

    def stage_adamw(u):
        for n, r in zip(u["names"], order.wait(u["c"])):
            big_out[n] = _adamw_layer(f"adamw_{n}_{u['layer']}", u["layer"], W[n], M[n], V[n],
                                      r.reshape(W[n].shape[1:]), big_out.get(n))
    dh1 = prev_f = prev_m = None
    for i in reversed(range(L)):
        w, sv = Wf[i], saved[i]
        if dh1 is not None:
            dh, _, dg = _rms_bwd(f"rms1_bw_{i + 1}", dxn1, saved[i + 1]["h0"], order.follows(norm_mix[i + 1]), dh1)
            small_g["norm_mix"][i + 1] = dg.reshape(D)
        de, dz = _ple_bwd_elem(f"ple_bw_{i}", dh, sv["gate"], sv["e"])
        (d_wpp,) = _matmul(f"d_wpp_{i}", "tn", (N_CHIPS, 1),
                           [((pb16[i], BS((T, PL), lambda i, j: (0, 0))), (de, BS((T, DS), lambda i, j: (0, i))))], [],
                           [((N_CHIPS, PL, DS), f32, BS((None, PL, DS), lambda i, j: (i, 0, 0)))])
        (d_wpg,) = _matmul(f"d_wpg_{i}", "tn", (D // tkd, D // tnd),
                           [((sv["xn3"], BS((T, tkd), lambda i, j: (0, i))), (dz, BS((T, tnd), lambda i, j: (0, j))))], [],
                           [((D, D), f32, BS((tkd, tnd), lambda i, j: (i, j)))])
        order.done(dz)
        if prev_m is not None:
            stage_pair(prev_m)
        (dxn3,) = _matmul(f"d_xn3_{i}", "nt", (T // tmw, D // tnd),
                          [((dz, BS((tmw, D), lambda i, j: (i, 0))), (w["w_pg"], BS((tnd, D), lambda i, j: (j, 0))))], [],
                          [((T, D), f32, BS((tmw, tnd), lambda i, j: (i, j)))], after=order.token)
        dh2, dh2b, dg = _rms_bwd(f"rms3_bw_{i}", dxn3, sv["h2"], norm_ple[i], dh)
        small_g["norm_ple"][i] = dg.reshape(D)

        def dffn_epi(accs, ex):
            da = accs[0]
            g_, u_ = ex[0].astype(f32), ex[1].astype(f32)
            sg = jax.nn.sigmoid(g_)
            return da * u_ * (sg * (1.0 + g_ * (1.0 - sg))), da * (g_ * sg)

        ffo = BS((tm, FS), lambda j, i: (i, j))
        dG, dU = _matmul(f"d_act_{i}", "nt", (N_CHIPS, T // tm),
                         [((dh2b, BS((tm, D), lambda j, i: (i, 0))), (w["w_down"], BS((FS, D), lambda j, i: (j, 0))))],
                         [(sv["Gt"], ffo), (sv["Ut"], ffo)], [((T, FF), bf16, ffo)] * 2, epilogue=dffn_epi)
        (d_wd,) = _matmul(f"d_wd_{i}", "tn", (N_CHIPS, D // tnw),
                          [((sv["act"], BS((T, FS), lambda i, j: (0, i))), (dh2b, BS((T, tnw), lambda i, j: (0, j))))], [],
                          [((FF, D), f32, BS((FS, tnw), lambda i, j: (i, j)))])
        gu_out = BS((None, tnd, FS), lambda j, i: (j, i, 0))
        d_wg, d_wu = _matmul(f"d_wgu_{i}", "tn", (N_CHIPS, D // tnd),
                             [((sv["xn2"], BS((T, tnd), lambda j, i: (0, i))), (dG, BS((T, FS), lambda j, i: (0, j)))),
                              ((sv["xn2"], BS((T, tnd), lambda j, i: (0, i))), (dU, BS((T, FS), lambda j, i: (0, j))))], [],
                             [((N_CHIPS, D, FS), f32, gu_out)] * 2, epilogue=lambda accs, ex: (accs[0], accs[1]))
        order.done(dG)
        unit_f = dict(tag=f"{i}f", layer=i, names=["w_ffn_gate", "w_ffn_up", "w_ffn_down", "w_ple_gate", "w_ple_proj"],
                      grads=[d_wg, d_wu, d_wd.reshape(N_CHIPS, FS, D), d_wpg.reshape(N_CHIPS, DS, D), d_wpp])
        stage_a(unit_f)
        if prev_f is not None:
            stage_sum(prev_f)
            stage_c(prev_f)
        if prev_m is not None:
            stage_b(prev_m)
        tm2 = _tile(T, 512, 16)
        (dxn2,) = _matmul(f"d_xn2_{i}", "nt", (D // tnd, T // tm2),
                          [((dG, BS((tm2, FF), lambda j, i: (i, 0))), (w["w_gate"], BS((N_CHIPS, tnd, FS), lambda j, i: (0, j, 0)))),
                           ((dU, BS((tm2, FF), lambda j, i: (i, 0))), (w["w_up"], BS((N_CHIPS, tnd, FS), lambda j, i: (0, j, 0))))], [],
                          [((T, D), f32, BS((tm2, tnd), lambda j, i: (i, j)))], after=order.token)
        dh1, dh1b, dg = _rms_bwd(f"rms2_bw_{i}", dxn2, sv["h1"], norm_ffn[i], dh2)
        small_g["norm_ffn"][i] = dg.reshape(D)
        (dmix,) = _matmul(f"d_mix_{i}", "nt", (T // tmw, D // tnd),
                          [((dh1b, BS((tmw, D), lambda i, j: (i, 0))), (w["w_out"], BS((tnd, D), lambda i, j: (j, 0))))], [],
                          [((T, D), f32, BS((tmw, tnd), lambda i, j: (i, j)))])
        (d_wout,) = _matmul(f"d_wout_{i}", "tn", (D // tkd, D // tnd),
                            [((sv["mix"], BS((T, tkd), lambda i, j: (0, i))), (dh1b, BS((T, tnd), lambda i, j: (0, j))))], [],
                            [((D, D), f32, BS((tkd, tnd), lambda i, j: (i, j)))])
        order.done(dmix)
        stage_pair(unit_f)
        stage_b(unit_f)
        if prev_f is not None:
            stage_adamw(prev_f)
        qg, kg = q_norm[i].reshape(1, HEAD), k_norm[i].reshape(1, HEAD)
        dq, dk, dv, dc_row, dqg, dkg = _attn_bwd(f"attn_bw_{i}", sv["qn"], sv["kn"], sv["vb"], sv["o32"], dmix,
                                                 sv["lse"], sv["c_col"], sv["c_row"], sv["P"], order.follows(qg), kg, tb)
        small_g["q_norm"][i] = dqg.reshape(HEAD)
        small_g["k_norm"][i] = dkg.reshape(HEAD)
        dct = jnp.pad(dc_row.reshape(H, T), ((0, HEAD - H), (0, 0)))
        dPf, dfb = _fgate_bwd(f"fgate_bw_{i}", dct, sv["Pf"], sv["fb"])
        small_g["forget_bias"][i] = dfb[0, :H]
        gain = gmlp_v_norm[i].reshape(G, 1, HEAD)
        bs = gmlp_b_s[i].reshape(G, HEAD, 1)
        dgu, dgv, dws, dbs, dgain = _gmlp_bwd(f"gmlp_bw_{i}", sv["P"], dmix, gain, gmlp_w_s[i], bs, col_gu, col_gv,
                                              col_dg, Wd)
        small_g["gmlp_w_s"][i] = dws
        small_g["gmlp_b_s"][i] = dbs.reshape(G, HEAD)
        small_g["gmlp_v_norm"][i] = dgain.reshape(G, HEAD)
        ps = pool_scale[i].reshape(1, Wd)
        dxp, dpw, dps = _pool_bwd(f"pool_bw_{i}", sv["P"], dmix, pool_w[i], ps, col_xp, col_dp, Wd)
        small_g["pool_w"][i] = dpw
        small_g["pool_scale"][i] = dps.reshape(Wd)
        dP = jnp.concatenate([dq, dk, dv, dgu, dgv, dxp], axis=1)
        (d_wmain,) = _matmul(f"d_wmain_{i}", "tn", (D // tkd, NM // tn),
                             [((sv["xn1"], BS((T, tkd), lambda i, j: (0, i))), (dP, BS((T, tn), lambda i, j: (0, j))))], [],
                             [((D, NM), f32, BS((tkd, tn), lambda i, j: (i, j)))])
        (d_wf,) = _matmul(f"d_wf_{i}", "tn", (D // tnd, 1),
                          [((sv["xn1"], BS((T, tnd), lambda i, j: (0, i))), (dPf, BS((T, HEAD), lambda i, j: (0, 0))))], [],
                          [((D, HEAD), f32, BS((tnd, HEAD), lambda i, j: (i, 0)))])
        d_win4 = _dwin_split(f"d_win_cols_{i}", d_wmain, d_wf, A, H)
        order.done(dP)
        unit_m = dict(tag=f"{i}m", layer=i, names=["w_in", "w_out"], grads=[d_win4, d_wout.reshape(N_CHIPS, DS, D)])
        stage_a(unit_m)
        if prev_m is not None:
            stage_sum(prev_m)
            stage_c(prev_m)
        (dxn1,) = _matmul(f"d_xn1_{i}", "nt", (D // tnd, T // tm),
                          [((dP, BS((tm, NM), lambda j, i: (i, 0))), (w["w_main"], BS((tnd, NM), lambda j, i: (j, 0)))),
                           ((dPf, BS((tm, HEAD), lambda j, i: (i, 0))), (w["w_f"], BS((tnd, HEAD), lambda j, i: (j, 0))))], [],
                          [((T, D), f32, BS((tm, tnd), lambda j, i: (i, j)))], after=order.token)
        order.done(dxn1)
        if prev_m is not None:
            stage_adamw(prev_m)
        prev_f, prev_m = unit_f, unit_m

    dh, _, dg = _rms_bwd("rms1_bw_0", dxn1, saved[0]["h0"], order.follows(norm_mix[0]), dh1)
    small_g["norm_mix"][0] = dg.reshape(D)
    order.done(dh)

    small_full = {n: jnp.stack(small_g[n]) for n in SMALL}
    small_shapes = [W[n].shape for n in SMALL]
    n_rows = sum(_rows_of(W[n].size) for n in SMALL) + _rows_of(1)
    rows8 = -(-n_rows // 64) * 8
    packed = order.follows(_pack_rows([small_full[n] for n in SMALL] + [loss_tile[0, :1]], N_DEV * rows8))
    summed = _all_reduce_small("allreduce_small", packed.reshape(N_DEV, rows8, 128)).reshape(-1, 128)
    order.done(summed)

    stage_sum(prev_f)
    stage_c(prev_f)
    stage_pair(prev_m)
    stage_b(prev_m)
    stage_adamw(prev_f)
    stage_sum(prev_m)
    stage_c(prev_m)
    stage_adamw(prev_m)
    *small_grads, loss_row = _unpack_rows(summed, small_shapes + [(1,)])
    grads = dict(zip(SMALL, small_grads))
    loss = loss_row[0]

    wp, mp, vp = (_pack_rows([t[n] for n in SMALL], N_DEV * rows8) for t in (W, M, V))
    delta, new_m, new_v = (dict(zip(SMALL, _unpack_rows(t, small_shapes)))
                           for t in _adamw("adamw_small", wp, summed, mp, vp))
    for n in BIG:
        grads[n], delta[n], new_m[n], new_v[n] = big_out[n]

    return (loss, dh.reshape(1, T, D), *[grads[n] for n in WEIGHTS], *[delta[n] for n in WEIGHTS],
            *[new_m[n] for n in WEIGHTS], *[new_v[n] for n in WEIGHTS])
```

```python
import jax
import jax.numpy as jnp
from jax import lax
from jax.experimental import pallas as pl
from jax.experimental.pallas import tpu as pltpu

f32, bf16 = jnp.float32, jnp.bfloat16
S = jax.ShapeDtypeStruct
BS = pl.BlockSpec
ANY = pl.BlockSpec(memory_space=pl.ANY)
MESH = pl.DeviceIdType.MESH

EPS = 1e-6
HEAD = 128
POOL_WINDOWS = (2, 4, 8, 16)
NEG = -1e30
N_CHIPS = 4
N_DEV = 8
VMEM_LIMIT = 56 * 1024 * 1024

ADAM_LR, ADAM_B1, ADAM_B2, ADAM_EPS, ADAM_WD, ADAM_STEP = 0.001, 0.9, 0.999, 1e-08, 0.01, 10

BIG = ("w_in", "w_out", "w_ffn_gate", "w_ffn_up", "w_ffn_down", "w_ple_gate", "w_ple_proj")
SMALL = ("norm_mix", "q_norm", "k_norm", "forget_bias", "gmlp_v_norm", "gmlp_w_s", "gmlp_b_s", "pool_w",
         "pool_scale", "norm_ffn", "norm_ple")
WEIGHTS = ("norm_mix", "w_in", "q_norm", "k_norm", "forget_bias", "gmlp_v_norm", "gmlp_w_s", "gmlp_b_s", "pool_w",
           "pool_scale", "w_out", "norm_ffn", "w_ffn_gate", "w_ffn_up", "w_ffn_down", "norm_ple", "w_ple_gate",
           "w_ple_proj")


def _tile(n, target, mult):
    best = None
    for t in range(mult, min(n, target) + 1, mult):
        if n % t == 0:
            best = t
    return best if best is not None else n


def _params(**kw):
    return pltpu.CompilerParams(vmem_limit_bytes=VMEM_LIMIT, **kw)


def _dot(a, b, kind):
    dims = {"nn": (((1,), (0,)), ((), ())), "nt": (((1,), (1,)), ((), ())), "tn": (((0,), (0,)), ((), ()))}[kind]
    return lax.dot_general(a.astype(bf16), b.astype(bf16), dims, preferred_element_type=f32)


def _heads_per_program(n_heads, want=2):
    while n_heads % want:
        want //= 2
    return want


def _my_place():
    x, y, c = lax.axis_index("x"), lax.axis_index("y"), lax.axis_index("c")
    return x, y, c, 2 * x + y


def _matmul(name, kind, grid, pairs, extras, outs, epilogue=None, after=None):
    n_p, n_e = len(pairs), len(extras)
    tokens = [] if after is None else [(after, BS((8, 128), lambda *_: (0, 0)))]

    def body(*refs):
        a_refs, b_refs = refs[:n_p], refs[n_p:2 * n_p]
        e_refs = refs[2 * n_p:2 * n_p + n_e]
        o_refs = refs[2 * n_p + n_e + len(tokens):]
        accs = []
        for a_ref, b_ref in zip(a_refs, b_refs):
            if len(b_ref.shape) == 3:
                w = b_ref.shape[2]
                acc = None
                for s in range(b_ref.shape[0]):
                    d = _dot(a_ref[:, s * w:(s + 1) * w], b_ref[s], kind)
                    acc = d if acc is None else acc + d
            else:
                acc = _dot(a_ref[...], b_ref[...], kind)
            accs.append(acc)
        if epilogue is None:
            res = accs[0]
            for t in accs[1:]:
                res = res + t
            res = (res,)
        else:
            res = epilogue(accs, [e[...] for e in e_refs])
        for o_ref, o in zip(o_refs, res):
            o_ref[...] = o.astype(o_ref.dtype)

    in_arrays = [p[0][0] for p in pairs] + [p[1][0] for p in pairs] + [e[0] for e in extras + tokens]
    in_specs = [p[0][1] for p in pairs] + [p[1][1] for p in pairs] + [e[1] for e in extras + tokens]
    res = pl.pallas_call(
        body, name=name, grid=grid, in_specs=in_specs,
        out_specs=[o[2] for o in outs], out_shape=[S(o[0], o[1]) for o in outs],
        compiler_params=_params(),
    )(*in_arrays)
    return res


def _rms_fwd(name, x, g):
    T, D = x.shape
    tr = _tile(T, 256, 8)

    def body(x_ref, g_ref, o_ref):
        xv = x_ref[...]
        r = lax.rsqrt(jnp.mean(xv * xv, axis=-1, keepdims=True) + EPS)
        o_ref[...] = (xv * r * g_ref[...]).astype(o_ref.dtype)

    return pl.pallas_call(
        body, name=name, grid=(T // tr,),
        in_specs=[BS((tr, D), lambda i: (i, 0)), BS((1, D), lambda i: (0, 0))],
        out_specs=BS((tr, D), lambda i: (i, 0)), out_shape=S((T, D), bf16),
    )(x, g.reshape(1, D))


def _rms_bwd(name, dxn, x, g, dres):
    T, D = x.shape
    tr = _tile(T, 256, 8)

    def body(dxn_ref, x_ref, g_ref, dres_ref, dx_ref, dxb_ref, dg_ref):
        i = pl.program_id(0)
        xv = x_ref[...]
        r = lax.rsqrt(jnp.mean(xv * xv, axis=-1, keepdims=True) + EPS)
        xh = xv * r
        dxn_v = dxn_ref[...]
        dxh = dxn_v * g_ref[...]
        dx = dres_ref[...] + r * (dxh - xh * jnp.mean(dxh * xh, axis=-1, keepdims=True))
        dx_ref[...] = dx
        dxb_ref[...] = dx.astype(bf16)
        part = jnp.sum(dxn_v * xh, axis=0, keepdims=True)

        @pl.when(i == 0)
        def _():
            dg_ref[...] = part

        @pl.when(i > 0)
        def _():
            dg_ref[...] += part

    row = BS((tr, D), lambda i: (i, 0))
    vec = BS((1, D), lambda i: (0, 0))
    return pl.pallas_call(
        body, name=name, grid=(T // tr,),
        in_specs=[row, row, vec, row], out_specs=[row, row, vec],
        out_shape=[S((T, D), f32), S((T, D), bf16), S((1, D), f32)],
    )(dxn, x, g.reshape(1, D), dres)


def _loss_grad(name, y, tgt):
    T, D = y.shape
    tr = _tile(T, 256, 8)

    def body(y_ref, t_ref, dy_ref, l_ref):
        i = pl.program_id(0)
        e = y_ref[...] - t_ref[...]
        dy_ref[...] = e * (1.0 / D)
        part = 0.5 * jnp.sum(jnp.mean(e * e, axis=-1, keepdims=True), axis=0, keepdims=True)

        @pl.when(i == 0)
        def _():
            l_ref[...] = jnp.zeros_like(l_ref)

        l_ref[...] += jnp.broadcast_to(part, l_ref.shape)

    row = BS((tr, D), lambda i: (i, 0))
    return pl.pallas_call(
        body, name=name, grid=(T // tr,), in_specs=[row, row],
        out_specs=[row, BS((8, 128), lambda i: (0, 0))],
        out_shape=[S((T, D), f32), S((8, 128), f32)],
    )(y, tgt)


def _ple_bwd_elem(name, dh, gate, e):
    T, D = dh.shape
    tr = _tile(T, 256, 16)

    def body(dh_ref, g_ref, e_ref, de_ref, dz_ref):
        d = dh_ref[...]
        g = g_ref[...].astype(f32)
        de_ref[...] = (d * g).astype(bf16)
        dz_ref[...] = (d * e_ref[...].astype(f32) * g * (1.0 - g)).astype(bf16)

    row = BS((tr, D), lambda i: (i, 0))
    return pl.pallas_call(
        body, name=name, grid=(T // tr,), in_specs=[row, row, row], out_specs=[row, row],
        out_shape=[S((T, D), bf16), S((T, D), bf16)],
    )(dh, gate, e)


def _cast_layer(name, w_all, layer):
    _, R, C = w_all.shape
    lanes = -(-C // 128) * 128
    tr = _tile(R, max(16, (1024 * 1024) // lanes // 16 * 16), 16)

    def body(w_ref, o_ref):
        o_ref[...] = w_ref[...].astype(bf16)

    return pl.pallas_call(
        body, name=name, grid=(R // tr,), in_specs=[BS((None, tr, C), lambda r: (layer, r, 0))],
        out_specs=BS((tr, C), lambda r: (r, 0)), out_shape=S((R, C), bf16),
    )(w_all)


def _win_assemble(name, g_win, A, H):
    _, D, DP4 = g_win.shape
    NM = N_CHIPS * DP4 - H
    tr = _tile(D, 256, 16)

    def body(g_ref, m_ref, f_ref):
        full = jnp.concatenate([g_ref[j] for j in range(N_CHIPS)], axis=1)
        m_ref[...] = jnp.concatenate([full[:, :3 * A], full[:, 3 * A + H:]], axis=1)
        f_ref[...] = jnp.concatenate([full[:, 3 * A:3 * A + H], jnp.zeros((tr, HEAD - H), bf16)], axis=1)

    return pl.pallas_call(
        body, name=name, grid=(D // tr,), in_specs=[BS((N_CHIPS, tr, DP4), lambda i: (0, i, 0))],
        out_specs=[BS((tr, NM), lambda i: (i, 0)), BS((tr, HEAD), lambda i: (i, 0))],
        out_shape=[S((D, NM), bf16), S((D, HEAD), bf16)],
    )(g_win)


def _dwin_split(name, d_wmain, d_wf, A, H):
    D, NM = d_wmain.shape
    DP4 = (NM + H) // N_CHIPS
    tr = _tile(D, 256, 8)

    def body(m_ref, f_ref, o_ref):
        m = m_ref[...]
        full = jnp.concatenate([m[:, :3 * A], f_ref[:, :H], m[:, 3 * A:]], axis=1)
        for j in range(N_CHIPS):
            o_ref[j] = full[:, j * DP4:(j + 1) * DP4]

    return pl.pallas_call(
        body, name=name, grid=(D // tr,),
        in_specs=[BS((tr, NM), lambda i: (i, 0)), BS((tr, HEAD), lambda i: (i, 0))],
        out_specs=BS((N_CHIPS, tr, DP4), lambda i: (0, i, 0)), out_shape=S((N_CHIPS, D, DP4), f32),
        compiler_params=_params(),
    )(d_wmain, d_wf)


def _gelu_and_grad(x):
    k0, k1 = 0.7978845608028654, 0.044715
    th = jnp.tanh(k0 * (x + k1 * x * x * x))
    val = 0.5 * x * (1.0 + th)
    grad = 0.5 * (1.0 + th) + 0.5 * x * (1.0 - th * th) * (k0 * (1.0 + 3.0 * k1 * x * x))
    return val, grad


def _fgate_fwd(name, pf, fb):
    T = pf.shape[0]

    def body(pf_ref, fb_ref, c_ref, ct_ref):
        xv = jax.nn.log_sigmoid(pf_ref[...] + fb_ref[...])
        row = lax.broadcasted_iota(jnp.int32, xv.shape, 0)
        s = 1
        while s < T:
            xv = xv + jnp.where(row >= s, pltpu.roll(xv, s, 0), 0.0)
            s *= 2
        c_ref[...] = xv
        ct_ref[...] = xv.T

    return pl.pallas_call(body, name=name, out_shape=[S((T, HEAD), f32), S((HEAD, T), f32)])(pf, fb)


def _fgate_bwd(name, dct, pf, fb):
    T = pf.shape[0]

    def body(dct_ref, pf_ref, fb_ref, dpf_ref, dfb_ref):
        xv = dct_ref[...].T
        row = lax.broadcasted_iota(jnp.int32, xv.shape, 0)
        s = 1
        while s < T:
            xv = xv + jnp.where(row + s < T, pltpu.roll(xv, T - s, 0), 0.0)
            s *= 2
        df = xv * jax.nn.sigmoid(-(pf_ref[...] + fb_ref[...]))
        dpf_ref[...] = df.astype(bf16)
        dfb_ref[...] = jnp.sum(df, axis=0, keepdims=True)

    return pl.pallas_call(body, name=name, out_shape=[S((T, HEAD), bf16), S((1, HEAD), f32)])(dct, pf, fb)


def _qk_norm(name, P, qg, kg, A):
    T = P.shape[0]
    tr = _tile(T, 256, 16)
    n_heads = A // HEAD

    def body(q_ref, k_ref, v_ref, qg_ref, kg_ref, qn_ref, kn_ref, vb_ref):
        for h in range(n_heads):
            sl = slice(h * HEAD, (h + 1) * HEAD)
            for src, g_ref, dst in ((q_ref, qg_ref, qn_ref), (k_ref, kg_ref, kn_ref)):
                xv = src[:, sl]
                r = lax.rsqrt(jnp.mean(xv * xv, axis=-1, keepdims=True) + EPS)
                dst[:, sl] = (xv * r * g_ref[...]).astype(bf16)
        vb_ref[...] = v_ref[...].astype(bf16)

    vec = BS((1, HEAD), lambda i: (0, 0))
    out = BS((tr, A), lambda i: (i, 0))
    return pl.pallas_call(
        body, name=name, grid=(T // tr,),
        in_specs=[BS((tr, A), lambda i: (i, 0)), BS((tr, A), lambda i: (i, 1)), BS((tr, A), lambda i: (i, 2)), vec, vec],
        out_specs=[out, out, out], out_shape=[S((T, A), bf16)] * 3,
    )(P, P, P, qg, kg)


def _attn_fwd(name, qn, kn, vb, c_col, c_row, tb, mix_width):
    T, A = qn.shape
    H = A // HEAD
    nb = T // tb
    scale = HEAD ** -0.5
    hp = _heads_per_program(H, 4)
    wide = hp * HEAD

    def body(q_ref, k_ref, v_ref, cq_ref, ck_ref, o_ref, o32_ref, lse_ref):
        i = pl.program_id(1)
        below = lax.broadcasted_iota(jnp.int32, (tb, tb), 0) >= lax.broadcasted_iota(jnp.int32, (tb, tb), 1)

        def block(j, carry, diagonal):
            koff = pl.multiple_of(j * tb, tb)
            out = []
            for hh in range(hp):
                m, l, acc = carry[hh]
                sl = slice(hh * HEAD, (hh + 1) * HEAD)
                k = k_ref[pl.ds(koff, tb), sl]
                v = v_ref[pl.ds(koff, tb), sl]
                s = _dot(q_ref[:, sl], k, "nt") * scale + (cq_ref[hh] - ck_ref[hh, j])
                if diagonal:
                    s = jnp.where(below, s, NEG)
                m_new = jnp.maximum(m, jnp.max(s, axis=-1, keepdims=True))
                alpha = jnp.exp(m - m_new)
                p = jnp.exp(s - m_new)
                l = l * alpha + jnp.sum(p, axis=-1, keepdims=True)
                acc = acc * alpha + _dot(p, v, "nn")
                out.append((m_new, l, acc))
            return tuple(out)

        init = tuple((jnp.full((tb, 1), NEG, f32), jnp.zeros((tb, 1), f32), jnp.zeros((tb, HEAD), f32))
                     for _ in range(hp))
        carry = lax.fori_loop(0, i, lambda j, c: block(j, c, False), init)
        carry = block(i, carry, True)
        for hh in range(hp):
            m, l, acc = carry[hh]
            sl = slice(hh * HEAD, (hh + 1) * HEAD)
            o = acc / l
            o_ref[:, sl] = o.astype(bf16)
            o32_ref[:, sl] = o
            lse_ref[hh] = m + jnp.log(l)

    return pl.pallas_call(
        body, name=name, grid=(H // hp, nb),
        in_specs=[BS((tb, wide), lambda h, i: (i, h)), BS((T, wide), lambda h, i: (0, h)),
                  BS((T, wide), lambda h, i: (0, h)), BS((hp, tb, 1), lambda h, i: (h, i, 0)),
                  BS((hp, nb, 1, tb), lambda h, i: (h, 0, 0, 0))],
        out_specs=[BS((tb, wide), lambda h, i: (i, h)), BS((tb, wide), lambda h, i: (i, h)),
                   BS((hp, tb, 1), lambda h, i: (h, i, 0))],
        out_shape=[S((T, mix_width), bf16), S((T, A), f32), S((H, T, 1), f32)],
    )(qn, kn, vb, c_col, c_row)


def _attn_bwd(name, qn, kn, vb, o, dmix, lse, c_col, c_row, P, qg, kg, tb):
    T, A = qn.shape
    H = A // HEAD
    nb = T // tb
    scale = HEAD ** -0.5
    hp = _heads_per_program(H)
    wide = hp * HEAD

    def body(q_ref, k_ref, v_ref, o_ref, do_ref, lse_ref, cq_ref, ck_ref, qraw_ref, kraw_ref, qg_ref, kg_ref,
             dq_out, dk_out, dv_out, dc_out, dqg_out, dkg_out, dq_acc, dk_acc, delta_s):
        h = pl.program_id(0)
        dq_acc[...] = jnp.zeros_like(dq_acc)
        below = lax.broadcasted_iota(jnp.int32, (tb, tb), 0) >= lax.broadcasted_iota(jnp.int32, (tb, tb), 1)
        for hh in range(hp):
            sl = slice(hh * HEAD, (hh + 1) * HEAD)
            delta_s[hh] = jnp.sum(do_ref[:, sl].astype(bf16).astype(f32) * o_ref[:, sl], axis=-1, keepdims=True)

        def kblock(j, _):
            koff = pl.multiple_of(j * tb, tb)

            def products(i, hh):
                sl = slice(hh * HEAD, (hh + 1) * HEAD)
                qoff = pl.multiple_of(i * tb, tb)
                return (_dot(q_ref[pl.ds(qoff, tb), sl], k_ref[pl.ds(koff, tb), sl], "nt"),
                        _dot(do_ref[pl.ds(qoff, tb), sl], v_ref[pl.ds(koff, tb), sl], "nt"))

            def qblock(i, carry, diagonal):
                qoff = pl.multiple_of(i * tb, tb)
                out = []
                for hh in range(hp):
                    dk, dv, dc, qk_i, dp = carry[hh]
                    sl = slice(hh * HEAD, (hh + 1) * HEAD)
                    ahead = products(jnp.minimum(i + 1, nb - 1), hh)
                    k = k_ref[pl.ds(koff, tb), sl]
                    q = q_ref[pl.ds(qoff, tb), sl]
                    do = do_ref[pl.ds(qoff, tb), sl].astype(bf16)
                    s = qk_i * scale + (cq_ref[hh, pl.ds(qoff, tb), :] - ck_ref[hh, j])
                    if diagonal:
                        s = jnp.where(below, s, NEG)
                    p = jnp.exp(s - lse_ref[hh, pl.ds(qoff, tb), :])
                    dv = dv + _dot(p, do, "tn")
                    ds = p * (dp - delta_s[hh, pl.ds(qoff, tb), :])
                    dc = dc - jnp.sum(ds, axis=0, keepdims=True)
                    dsb = (ds * scale).astype(bf16)
                    dk = dk + _dot(dsb, q, "tn")
                    dq_acc[pl.ds(qoff, tb), sl] += _dot(dsb, k, "nn")
                    out.append((dk, dv, dc, *ahead))
                return tuple(out)

            init = tuple((jnp.zeros((tb, HEAD), f32), jnp.zeros((tb, HEAD), f32), jnp.zeros((1, tb), f32),
                          *products(j, hh)) for hh in range(hp))
            carry = qblock(j, init, True)
            carry = lax.fori_loop(j + 1, nb, lambda i, c: qblock(i, c, False), carry)
            for hh in range(hp):
                dk, dv, dc = carry[hh][:3]
                sl = slice(hh * HEAD, (hh + 1) * HEAD)
                dk_acc[pl.ds(koff, tb), sl] = dk
                dv_out[pl.ds(koff, tb), sl] = dv.astype(bf16)
                dc_out[hh, j] = dc
            return 0

        lax.fori_loop(0, nb, kblock, 0)

        for raw_ref, g_ref, acc_ref, d_out, dg_out in ((qraw_ref, qg_ref, dq_acc, dq_out, dqg_out),
                                                       (kraw_ref, kg_ref, dk_acc, dk_out, dkg_out)):
            part = jnp.zeros((1, HEAD), f32)
            for hh in range(hp):
                sl = slice(hh * HEAD, (hh + 1) * HEAD)
                xv = raw_ref[:, sl]
                r = lax.rsqrt(jnp.mean(xv * xv, axis=-1, keepdims=True) + EPS)
                xh = xv * r
                dn = acc_ref[:, sl]
                dxh = dn * g_ref[...]
                d_out[:, sl] = (r * (dxh - xh * jnp.mean(dxh * xh, axis=-1, keepdims=True))).astype(bf16)
                part = part + jnp.sum(dn * xh, axis=0, keepdims=True)

            @pl.when(h == 0)
            def _():
                dg_out[...] = part

            @pl.when(h > 0)
            def _():
                dg_out[...] += part

    heads = lambda off: BS((T, wide), lambda h: (0, off + h))
    col = BS((hp, T, 1), lambda h: (h, 0, 0))
    row = BS((hp, nb, 1, tb), lambda h: (h, 0, 0, 0))
    vec = BS((1, HEAD), lambda h: (0, 0))
    return pl.pallas_call(
        body, name=name, grid=(H // hp,),
        in_specs=[heads(0), heads(0), heads(0), heads(0), heads(0), col, col, row, heads(0), heads(H // hp), vec, vec],
        out_specs=[heads(0), heads(0), heads(0), row, vec, vec],
        out_shape=[S((T, A), bf16)] * 3 + [S((H, nb, 1, tb), f32), S((1, HEAD), f32), S((1, HEAD), f32)],
        scratch_shapes=[pltpu.VMEM((T, wide), f32), pltpu.VMEM((T, wide), f32), pltpu.VMEM((hp, T, 1), f32)],
        compiler_params=_params(),
    )(qn, kn, vb, o, dmix, lse, c_col, c_row, P, P, qg, kg)


def _gmlp_fwd(name, P, mix, gain, ws, b, col_u, col_v, col_y, Wd):
    T = P.shape[0]
    G = Wd // HEAD
    tr = _tile(T, 512, HEAD)

    def body(u_ref, v_ref, gain_ref, ws_ref, b_ref, mix_ref, y_ref):
        tril = lax.broadcasted_iota(jnp.int32, (HEAD, HEAD), 0) >= lax.broadcasted_iota(jnp.int32, (HEAD, HEAD), 1)
        wm = jnp.where(tril, ws_ref[...], 0.0).astype(bf16)
        for n in range(tr // HEAD):
            rows = slice(n * HEAD, (n + 1) * HEAD)
            u = jax.nn.gelu(u_ref[rows, :])
            a = jax.nn.gelu(v_ref[rows, :])
            r = lax.rsqrt(jnp.mean(a * a, axis=-1, keepdims=True) + EPS)
            vn = a * r * gain_ref[...]
            mixed = _dot(wm, vn, "nn") + b_ref[...]
            y_ref[rows, :] = (u * mixed).astype(bf16)

    return pl.pallas_call(
        body, name=name, grid=(G, T // tr),
        in_specs=[BS((tr, HEAD), lambda g, i: (i, col_u + g)), BS((tr, HEAD), lambda g, i: (i, col_v + g)),
                  BS((None, 1, HEAD), lambda g, i: (g, 0, 0)), BS((None, HEAD, HEAD), lambda g, i: (g, 0, 0)),
                  BS((None, HEAD, 1), lambda g, i: (g, 0, 0)), ANY],
        out_specs=BS((tr, HEAD), lambda g, i: (i, col_y + g)), out_shape=S(mix.shape, bf16),
        input_output_aliases={5: 0},
    )(P, P, gain, ws, b, mix)


def _gmlp_bwd(name, P, dmix, gain, ws, b, col_u, col_v, col_dy, Wd):
    T = P.shape[0]
    G = Wd // HEAD
    tr = _tile(T, 512, HEAD)

    def body(u_ref, v_ref, dy_ref, gain_ref, ws_ref, b_ref, du_ref, dv_ref, dws_ref, db_ref, dgain_ref):
        i = pl.program_id(1)
        tril = lax.broadcasted_iota(jnp.int32, (HEAD, HEAD), 0) >= lax.broadcasted_iota(jnp.int32, (HEAD, HEAD), 1)
        wm = jnp.where(tril, ws_ref[...], 0.0).astype(bf16)
        gain_v = gain_ref[...]
        dw = jnp.zeros((HEAD, HEAD), f32)
        db = jnp.zeros((HEAD, 1), f32)
        dgain = jnp.zeros((1, HEAD), f32)
        for n in range(tr // HEAD):
            rows = slice(n * HEAD, (n + 1) * HEAD)
            u, du_dx = _gelu_and_grad(u_ref[rows, :])
            a, da_dx = _gelu_and_grad(v_ref[rows, :])
            dy = dy_ref[rows, :]
            r = lax.rsqrt(jnp.mean(a * a, axis=-1, keepdims=True) + EPS)
            ah = a * r
            vnb = (ah * gain_v).astype(bf16)
            mixed = _dot(wm, vnb, "nn") + b_ref[...]
            dm = dy * u
            dmb = dm.astype(bf16)
            du_ref[rows, :] = (dy * mixed * du_dx).astype(bf16)
            db = db + jnp.sum(dm, axis=1, keepdims=True)
            dw = dw + _dot(dmb, vnb, "nt")
            dvn = _dot(wm, dmb, "tn")
            dgain = dgain + jnp.sum(dvn * ah, axis=0, keepdims=True)
            dah = dvn * gain_v
            da = r * (dah - ah * jnp.mean(dah * ah, axis=-1, keepdims=True))
            dv_ref[rows, :] = (da * da_dx).astype(bf16)
        dw = jnp.where(tril, dw, 0.0)

        @pl.when(i == 0)
        def _():
            dws_ref[...] = dw
            db_ref[...] = db
            dgain_ref[...] = dgain

        @pl.when(i > 0)
        def _():
            dws_ref[...] += dw
            db_ref[...] += db
            dgain_ref[...] += dgain

    out = BS((tr, HEAD), lambda g, i: (i, g))
    return pl.pallas_call(
        body, name=name, grid=(G, T // tr),
        in_specs=[BS((tr, HEAD), lambda g, i: (i, col_u + g)), BS((tr, HEAD), lambda g, i: (i, col_v + g)),
                  BS((tr, HEAD), lambda g, i: (i, col_dy + g)),
                  BS((None, 1, HEAD), lambda g, i: (g, 0, 0)), BS((None, HEAD, HEAD), lambda g, i: (g, 0, 0)),
                  BS((None, HEAD, 1), lambda g, i: (g, 0, 0))],
        out_specs=[out, out, BS((None, HEAD, HEAD), lambda g, i: (g, 0, 0)), BS((None, HEAD, 1), lambda g, i: (g, 0, 0)),
                   BS((None, 1, HEAD), lambda g, i: (g, 0, 0))],
        out_shape=[S((T, Wd), bf16), S((T, Wd), bf16), S((G, HEAD, HEAD), f32), S((G, HEAD, 1), f32),
                   S((G, 1, HEAD), f32)],
    )(P, P, dmix, gain, ws, b)


def _trailing_window(xv, w, row):
    k = 1
    while k < w:
        xv = xv + jnp.where(row >= k, pltpu.roll(xv, k, 0), 0.0)
        k *= 2
    return xv


def _leading_window(xv, w, row, T):
    k = 1
    while k < w:
        xv = xv + jnp.where(row + k < T, pltpu.roll(xv, T - k, 0), 0.0)
        k *= 2
    return xv


def _pool_fwd(name, P, mix, pw, ps, col_x, col_y, Wd):
    T = P.shape[0]
    Gp = Wd // HEAD

    def body(x_ref, pw_ref, ps_ref, mix_ref, y_ref):
        row = lax.broadcasted_iota(jnp.int32, (T, HEAD), 0)
        for g in range(Gp):
            w = POOL_WINDOWS[g]
            sl = slice(g * HEAD, (g + 1) * HEAD)
            xv = x_ref[:, sl]
            cnt = jnp.minimum(row + 1, w).astype(f32)
            d = _trailing_window(xv, w, row) / cnt - xv
            y_ref[:, sl] = (_dot(d, pw_ref[g], "nn") * ps_ref[:, sl]).astype(bf16)

    return pl.pallas_call(
        body, name=name, grid=(1,),
        in_specs=[BS((T, Wd), lambda i: (0, col_x)), BS((Gp, HEAD, HEAD), lambda i: (0, 0, 0)), BS((1, Wd), lambda i: (0, 0)),
                  ANY],
        out_specs=BS((T, Wd), lambda i: (0, col_y)), out_shape=S(mix.shape, bf16), input_output_aliases={3: 0},
        compiler_params=_params(),
    )(P, pw, ps, mix)


def _pool_bwd(name, P, dmix, pw, ps, col_x, col_dy, Wd):
    T = P.shape[0]
    Gp = Wd // HEAD

    def body(x_ref, dy_ref, pw_ref, ps_ref, dx_ref, dpw_ref, dps_ref):
        row = lax.broadcasted_iota(jnp.int32, (T, HEAD), 0)
        for g in range(Gp):
            w = POOL_WINDOWS[g]
            sl = slice(g * HEAD, (g + 1) * HEAD)
            xv = x_ref[:, sl]
            cnt = jnp.minimum(row + 1, w).astype(f32)
            d = (_trailing_window(xv, w, row) / cnt - xv).astype(bf16)
            pwb = pw_ref[g].astype(bf16)
            z = _dot(d, pwb, "nn")
            dy = dy_ref[:, sl]
            dps_ref[:, sl] = jnp.sum(dy * z, axis=0, keepdims=True)
            dzb = (dy * ps_ref[:, sl]).astype(bf16)
            dpw_ref[g] = _dot(d, dzb, "tn")
            dd = _dot(dzb, pwb, "nt")
            dx_ref[:, sl] = (_leading_window(dd / cnt, w, row, T) - dd).astype(bf16)

    return pl.pallas_call(
        body, name=name, grid=(1,),
        in_specs=[BS((T, Wd), lambda i: (0, col_x)), BS((T, Wd), lambda i: (0, col_dy)),
                  BS((Gp, HEAD, HEAD), lambda i: (0, 0, 0)), BS((1, Wd), lambda i: (0, 0))],
        out_specs=[BS((T, Wd), lambda i: (0, 0)), BS((Gp, HEAD, HEAD), lambda i: (0, 0, 0)), BS((1, Wd), lambda i: (0, 0))],
        out_shape=[S((T, Wd), bf16), S((Gp, HEAD, HEAD), f32), S((1, Wd), f32)], compiler_params=_params(),
    )(P, dmix, pw, ps)


def _adamw(name, w, g, m, v):
    R, C = w.shape
    lanes = -(-C // 128) * 128
    tr = _tile(R, max(8, (512 * 1024) // lanes // 8 * 8), 8)
    c1 = 1.0 - ADAM_B1 ** ADAM_STEP
    c2 = 1.0 - ADAM_B2 ** ADAM_STEP

    def body(w_ref, g_ref, m_ref, v_ref, d_ref, nm_ref, nv_ref):
        gv = g_ref[...]
        nm = ADAM_B1 * m_ref[...] + (1.0 - ADAM_B1) * gv
        nv = ADAM_B2 * v_ref[...] + (1.0 - ADAM_B2) * (gv * gv)
        d_ref[...] = -ADAM_LR * ((nm / c1) / (jnp.sqrt(nv / c2) + ADAM_EPS) + ADAM_WD * w_ref[...])
        nm_ref[...] = nm
        nv_ref[...] = nv

    blk = BS((tr, C), lambda i: (i, 0))
    return pl.pallas_call(
        body, name=name, grid=(R // tr,), in_specs=[blk] * 4, out_specs=[blk] * 3, out_shape=[S((R, C), f32)] * 3,
    )(w, g, m, v)


def _adamw_layer(name, layer, w_all, m_all, v_all, g, prev):
    L, R, C = w_all.shape
    lanes = -(-C // 128) * 128
    tr = _tile(R, max(8, (512 * 1024) // lanes // 8 * 8), 8)
    c1 = 1.0 - ADAM_B1 ** ADAM_STEP
    c2 = 1.0 - ADAM_B2 ** ADAM_STEP
    n_prev = 0 if prev is None else 4

    def body(w_ref, m_ref, v_ref, g_ref, *rest):
        go_ref, d_ref, nm_ref, nv_ref = rest[n_prev:]
        gv = g_ref[...]
        nm = ADAM_B1 * m_ref[...] + (1.0 - ADAM_B1) * gv
        nv = ADAM_B2 * v_ref[...] + (1.0 - ADAM_B2) * (gv * gv)
        d_ref[...] = -ADAM_LR * ((nm / c1) / (jnp.sqrt(nv / c2) + ADAM_EPS) + ADAM_WD * w_ref[...])
        nm_ref[...] = nm
        nv_ref[...] = nv
        go_ref[...] = gv

    slab = BS((None, tr, C), lambda r: (layer, r, 0))
    return pl.pallas_call(
        body, name=name, grid=(R // tr,),
        in_specs=[slab, slab, slab, BS((tr, C), lambda r: (r, 0))] + [ANY] * n_prev,
        out_specs=[slab] * 4, out_shape=[S((L, R, C), f32)] * 4,
        input_output_aliases={4 + k: k for k in range(n_prev)},
    )(w_all, m_all, v_all, g, *(prev or ()))


def _chip_of(k):
    return k // 2, k % 2


def _remote(src, dst, send_sems, recv_sems, idx, dev):
    return pltpu.make_async_remote_copy(src_ref=src, dst_ref=dst, send_sem=send_sems.at[idx], recv_sem=recv_sems.at[idx],
                                        device_id=dev, device_id_type=MESH)


def _plan_gather_near(n):
    def plan(refs, ss, rs, base):
        ins, lands = refs[:n], refs[n:]
        x, y, c, j0 = _my_place()
        sib = (x, y, 1 - c)
        sends, recvs = [], []
        for a in range(n):
            half = ins[a].shape[0] // 2
            lo = c * half
            sends.append(_remote(ins[a], lands[a].at[j0], ss, rs, base + 3 * a + 2, sib))
            recvs.append(_remote(lands[a].at[j0], lands[a].at[j0], ss, rs, base + 3 * a + 2, sib))
            for r in (1, 2):
                k = j0 ^ r
                dev = (*_chip_of(k), c)
                sends.append(_remote(ins[a].at[pl.ds(lo, half)], lands[a].at[j0, pl.ds(lo, half)], ss, rs,
                                     base + 3 * a + r - 1, dev))
                landed = lands[a].at[k, pl.ds(lo, half)]
                recvs.append(_remote(landed, landed, ss, rs, base + 3 * a + r - 1, dev))
        return sends, recvs
    return plan, 3 * n


def _plan_gather_relay(n):
    def plan(refs, ss, rs, base):
        x, y, c, j0 = _my_place()
        sib = (x, y, 1 - c)
        sends, recvs = [], []
        for a in range(n):
            half = refs[a].shape[1] // 2
            quarter = half // 2
            lo = c * half
            far = j0 ^ 3
            for r, to, off in ((1, 2, 0), (2, 1, quarter)):
                dev = (*_chip_of(j0 ^ to), c)
                piece = refs[a].at[j0 ^ r, pl.ds(lo + off, quarter)]
                sends.append(_remote(piece, piece, ss, rs, base + 4 * a + to - 1, dev))
                lands_here = refs[a].at[far, pl.ds(lo + off, quarter)]
                recvs.append(_remote(lands_here, lands_here, ss, rs, base + 4 * a + to - 1, dev))
                mine = refs[a].at[j0 ^ r, pl.ds(lo, half)]
                theirs = refs[a].at[j0 ^ r, pl.ds((1 - c) * half, half)]
                sends.append(_remote(mine, mine, ss, rs, base + 4 * a + 1 + r, sib))
                recvs.append(_remote(theirs, theirs, ss, rs, base + 4 * a + 1 + r, sib))
        return sends, recvs
    return plan, 4 * n


def _plan_gather_far(n):
    def plan(refs, ss, rs, base):
        x, y, c, j0 = _my_place()
        sib = (x, y, 1 - c)
        sends, recvs = [], []
        for a in range(n):
            half = refs[a].shape[1] // 2
            mine = refs[a].at[j0 ^ 3, pl.ds(c * half, half)]
            theirs = refs[a].at[j0 ^ 3, pl.ds((1 - c) * half, half)]
            sends.append(_remote(mine, mine, ss, rs, base + a, sib))
            recvs.append(_remote(theirs, theirs, ss, rs, base + a, sib))
        return sends, recvs
    return plan, n


def _plan_sibling_halves(n):
    def plan(refs, ss, rs, base):
        ins, lands = refs[:n], refs[n:]
        x, y, c, _ = _my_place()
        sib = (x, y, 1 - c)
        sends, recvs = [], []
        for a in range(n):
            half = ins[a].shape[1] // 2
            sends.append(_remote(ins[a].at[:, pl.ds((1 - c) * half, half), :], lands[a], ss, rs, base + a, sib))
            recvs.append(_remote(lands[a], lands[a], ss, rs, base + a, sib))
        return sends, recvs
    return plan, n


def _plan_chip_scatter(n):
    def plan(refs, ss, rs, base):
        ins, lands = refs[:n], refs[n:]
        x, y, c, j0 = _my_place()
        sends, recvs = [], []
        for a in range(n):
            for r in (1, 2, 3):
                k = j0 ^ r
                dev = (*_chip_of(k), c)
                sends.append(_remote(ins[a].at[k], lands[a].at[j0], ss, rs, base + 3 * a + r - 1, dev))
                recvs.append(_remote(lands[a].at[k], lands[a].at[k], ss, rs, base + 3 * a + r - 1, dev))
        return sends, recvs
    return plan, 3 * n


def _plan_sibling_join(n):
    def plan(refs, ss, rs, base):
        x, y, c, _ = _my_place()
        sib = (x, y, 1 - c)
        sends, recvs = [], []
        for a in range(n):
            half = refs[a].shape[0] // 2
            mine = refs[a].at[pl.ds(c * half, half)]
            theirs = refs[a].at[pl.ds((1 - c) * half, half)]
            sends.append(_remote(mine, mine, ss, rs, base + a, sib))
            recvs.append(_remote(theirs, theirs, ss, rs, base + a, sib))
        return sends, recvs
    return plan, n


_HBM = pl.BlockSpec(memory_space=pltpu.HBM)
_SEM = pl.BlockSpec(memory_space=pltpu.SEMAPHORE)
_EFFECT = pltpu.SideEffectType.DATAFLOW_SIDE_EFFECTING


def _exchange_start(name, plan, bufs, after):
    plan_fn, n_sems = plan
    n = len(bufs)

    def body(*refs):
        ss, rs, token = refs[n + len(after)], refs[n + len(after) + 1], refs[-1]
        sends, _ = plan_fn(refs[:n], ss, rs, 0)
        for cp in sends:
            cp.start()
        token[...] = jnp.zeros_like(token)

    res = pl.pallas_call(
        body, name=name,
        out_shape=(pltpu.SemaphoreType.DMA((n_sems,)), pltpu.SemaphoreType.DMA((n_sems,)),
                   *[pltpu.HBM(b.shape, b.dtype) for b in bufs], S((8, 128), f32)),
        in_specs=[_HBM] * n + [ANY] * len(after),
        out_specs=(_SEM, _SEM, *[_HBM] * n, pl.BlockSpec(memory_space=pltpu.VMEM)),
        input_output_aliases={k: 2 + k for k in range(n)},
        compiler_params=pltpu.CompilerParams(has_side_effects=_EFFECT),
    )(*[pltpu.with_memory_space_constraint(b, pltpu.HBM) for b in bufs], *after)
    return res[0], res[1], list(res[2:2 + n]), res[-1]


def _exchange_wait(name, plan, send_sems, recv_sems, bufs, after):
    plan_fn, _ = plan
    n = len(bufs)

    def body(*refs):
        ss, rs, token = refs[n], refs[n + 1], refs[-1]
        sends, recvs = plan_fn(refs[:n], ss, rs, 0)
        for cp in recvs:
            cp.wait_recv()
        for cp in sends:
            cp.wait_send()
        token[...] = jnp.zeros_like(token)

    res = pl.pallas_call(
        body, name=name,
        out_shape=(*[pltpu.HBM(b.shape, b.dtype) for b in bufs], S((8, 128), f32)),
        in_specs=[_HBM] * n + [_SEM, _SEM] + [ANY] * len(after),
        out_specs=(*[_HBM] * n, pl.BlockSpec(memory_space=pltpu.VMEM)),
        input_output_aliases={k: k for k in range(n)},
        compiler_params=pltpu.CompilerParams(has_side_effects=_EFFECT),
    )(*bufs, send_sems, recv_sems, *after)
    return list(res[:n]), res[-1]


class _Order:
    def __init__(self, first):
        self.marker = first
        self.token = None

    def _after(self):
        return [self.marker] + ([] if self.token is None else [self.token])

    def start(self, name, plan, bufs):
        ss, rs, thru, self.token = _exchange_start(name, plan, bufs, self._after())
        return name, plan, ss, rs, thru

    def wait(self, handle):
        name, plan, ss, rs, thru = handle
        out, self.token = _exchange_wait(name + "_wait", plan, ss, rs, thru, self._after())
        return out

    def follows(self, small):
        return small if self.token is None else small + self.token[0, 0]

    def done(self, result):
        self.marker = result[(slice(0, 1),) * result.ndim].reshape(1, 1)


def _all_reduce_small(name, g8):
    _, R, L = g8.shape

    def body(g_ref, out_ref, land, red, send1, recv1, send2, recv2):
        x, y, c, _ = _my_place()
        me = 4 * x + 2 * y + c
        peers = []
        for r in range(1, N_DEV):
            q = me ^ r
            peers.append((q, (q // 4, (q // 2) % 2, q % 2)))
        first = []
        for r, (q, dev) in enumerate(peers):
            cp = pltpu.make_async_remote_copy(src_ref=g_ref.at[q], dst_ref=land.at[me], send_sem=send1.at[r],
                                              recv_sem=recv1.at[r], device_id=dev, device_id_type=MESH)
            cp.start()
            first.append(cp)
        land[me] = g_ref[me]
        for r, (q, dev) in enumerate(peers):
            pltpu.make_async_remote_copy(src_ref=land.at[q], dst_ref=land.at[q], send_sem=send1.at[r],
                                         recv_sem=recv1.at[r], device_id=dev, device_id_type=MESH).wait_recv()
        acc = land[0]
        for d in range(1, N_DEV):
            acc = acc + land[d]
        red[...] = acc
        out_ref[me] = acc
        second = []
        for r, (q, dev) in enumerate(peers):
            cp = pltpu.make_async_remote_copy(src_ref=red, dst_ref=out_ref.at[me], send_sem=send2.at[r],
                                              recv_sem=recv2.at[r], device_id=dev, device_id_type=MESH)
            cp.start()
            second.append(cp)
        for r, (q, dev) in enumerate(peers):
            pltpu.make_async_remote_copy(src_ref=out_ref.at[q], dst_ref=out_ref.at[q], send_sem=send2.at[r],
                                         recv_sem=recv2.at[r], device_id=dev, device_id_type=MESH).wait_recv()
        for cp in first + second:
            cp.wait_send()

    vm = pl.BlockSpec(memory_space=pltpu.VMEM)
    return pl.pallas_call(
        body, name=name, in_specs=[vm], out_specs=vm, out_shape=S(g8.shape, f32),
        scratch_shapes=[pltpu.VMEM((N_DEV, R, L), f32), pltpu.VMEM((R, L), f32)]
        + [pltpu.SemaphoreType.DMA((N_DEV - 1,))] * 4,
        compiler_params=_params(),
    )(g8)


def _pair_sum(name, g4, sib):
    _, rows, cols = g4.shape
    half = rows // 2
    lanes = -(-cols // 128) * 128
    tr = _tile(half, max(16, (1024 * 1024) // lanes // 16 * 16), 16)
    nb = half // tr

    def body(g_ref, s_ref, pb_ref, own_ref):
        j = pl.program_id(1)
        t = g_ref[...] + s_ref[...]
        pb_ref[...] = t.astype(bf16)

        @pl.when(j == _my_place()[3])
        def _():
            own_ref[...] = t

    return pl.pallas_call(
        body, name=name, grid=(nb, N_CHIPS),
        in_specs=[BS((None, tr, cols), lambda i, j: (j, lax.axis_index("c") * nb + i, 0)),
                  BS((None, tr, cols), lambda i, j: (j, i, 0))],
        out_specs=[BS((None, tr, cols), lambda i, j: (j, i, 0)), BS((tr, cols), lambda i, j: (i, 0))],
        out_shape=[S((N_CHIPS, half, cols), bf16), S((half, cols), f32)],
    )(g4, sib)


def _chip_sum(name, own, got):
    half, cols = own.shape
    lanes = -(-cols // 128) * 128
    tr = _tile(half, max(16, (1024 * 1024) // lanes // 16 * 16), 16)
    nb = half // tr

    def body(own_ref, *rest):
        got_refs, o_ref = rest[:N_CHIPS], rest[N_CHIPS]
        j0 = _my_place()[3]
        acc = None
        for k in range(N_CHIPS):
            t = jnp.where(j0 == k, own_ref[...], got_refs[k][...].astype(f32))
            acc = t if acc is None else acc + t
        o_ref[...] = acc

    def slot(k):
        return BS((None, tr, cols), lambda i: (jnp.where(_my_place()[3] == k, (k + 1) % N_CHIPS, k), i, 0))

    return pl.pallas_call(
        body, name=name, grid=(nb,),
        in_specs=[BS((tr, cols), lambda i: (i, 0))] + [slot(k) for k in range(N_CHIPS)],
        out_specs=BS((tr, cols), lambda i: (lax.axis_index("c") * nb + i, 0)),
        out_shape=S((2 * half, cols), f32),
    )(own, got, got, got, got)


def _rows_of(size):
    return -(-size // 1024) * 8


def _pack_rows(arrs, n_rows):
    parts = []
    for a in arrs:
        rows = _rows_of(a.size)
        if a.size % 128 == 0:
            part = a.astype(f32).reshape(-1, 128)
            part = jnp.pad(part, ((0, rows - part.shape[0]), (0, 0)))
        else:
            part = jnp.pad(a.reshape(-1).astype(f32), (0, rows * 128 - a.size)).reshape(rows, 128)
        parts.append(part)
    used = sum(p.shape[0] for p in parts)
    return jnp.concatenate(parts + [jnp.zeros((n_rows - used, 128), f32)], axis=0)


def _unpack_rows(packed, shapes):
    out, row = [], 0
    for shp in shapes:
        size = 1
        for d in shp:
            size *= d
        rows = packed[row:row + _rows_of(size)]
        out.append(rows[:size // 128].reshape(shp) if size % 128 == 0 else rows.reshape(-1)[:size].reshape(shp))
        row += _rows_of(size)
    return out


def kernel(x, p, norm_mix, w_in, q_norm, k_norm, forget_bias, gmlp_v_norm, gmlp_w_s, gmlp_b_s, pool_w, pool_scale, w_out, norm_ffn, w_ffn_gate, w_ffn_up, w_ffn_down, norm_ple, w_ple_gate, w_ple_proj, loss_target, m_norm_mix, m_w_in, m_q_norm, m_k_norm, m_forget_bias, m_gmlp_v_norm, m_gmlp_w_s, m_gmlp_b_s, m_pool_w, m_pool_scale, m_w_out, m_norm_ffn, m_w_ffn_gate, m_w_ffn_up, m_w_ffn_down, m_norm_ple, m_w_ple_gate, m_w_ple_proj, v_norm_mix, v_w_in, v_q_norm, v_k_norm, v_forget_bias, v_gmlp_v_norm, v_gmlp_w_s, v_gmlp_b_s, v_pool_w, v_pool_scale, v_w_out, v_norm_ffn, v_w_ffn_gate, v_w_ffn_up, v_w_ffn_down, v_norm_ple, v_w_ple_gate, v_w_ple_proj):
    W = dict(norm_mix=norm_mix, w_in=w_in, q_norm=q_norm, k_norm=k_norm, forget_bias=forget_bias,
             gmlp_v_norm=gmlp_v_norm, gmlp_w_s=gmlp_w_s, gmlp_b_s=gmlp_b_s, pool_w=pool_w, pool_scale=pool_scale,
             w_out=w_out, norm_ffn=norm_ffn, w_ffn_gate=w_ffn_gate, w_ffn_up=w_ffn_up, w_ffn_down=w_ffn_down,
             norm_ple=norm_ple, w_ple_gate=w_ple_gate, w_ple_proj=w_ple_proj)
    M = dict(norm_mix=m_norm_mix, w_in=m_w_in, q_norm=m_q_norm, k_norm=m_k_norm, forget_bias=m_forget_bias,
             gmlp_v_norm=m_gmlp_v_norm, gmlp_w_s=m_gmlp_w_s, gmlp_b_s=m_gmlp_b_s, pool_w=m_pool_w,
             pool_scale=m_pool_scale, w_out=m_w_out, norm_ffn=m_norm_ffn, w_ffn_gate=m_w_ffn_gate,
             w_ffn_up=m_w_ffn_up, w_ffn_down=m_w_ffn_down, norm_ple=m_norm_ple, w_ple_gate=m_w_ple_gate,
             w_ple_proj=m_w_ple_proj)
    V = dict(norm_mix=v_norm_mix, w_in=v_w_in, q_norm=v_q_norm, k_norm=v_k_norm, forget_bias=v_forget_bias,
             gmlp_v_norm=v_gmlp_v_norm, gmlp_w_s=v_gmlp_w_s, gmlp_b_s=v_gmlp_b_s, pool_w=v_pool_w,
             pool_scale=v_pool_scale, w_out=v_w_out, norm_ffn=v_norm_ffn, w_ffn_gate=v_w_ffn_gate,
             w_ffn_up=v_w_ffn_up, w_ffn_down=v_w_ffn_down, norm_ple=v_norm_ple, w_ple_gate=v_w_ple_gate,
             w_ple_proj=v_w_ple_proj)

    L = w_in.shape[0]
    _, T, D = x.shape
    A, Wd = D // 2, D // 4
    H = A // HEAD
    G = gmlp_w_s.shape[1]
    Gp = pool_w.shape[1]
    DP4 = w_in.shape[2]
    DP = N_CHIPS * DP4
    NM = 3 * A + 3 * Wd
    FS = w_ffn_gate.shape[2]
    FF = N_CHIPS * FS
    DS = D // N_CHIPS
    PL = p.shape[-1]
    assert Wd // G == HEAD and Wd // Gp == HEAD and DP == NM + H and H <= HEAD
    assert all(w & (w - 1) == 0 for w in POOL_WINDOWS[:Gp])
    tb = _tile(T, 256, HEAD)
    nb = T // tb
    tm = _tile(T, 512, 16)
    tmw = _tile(T, 1024, 16)
    tn = _tile(NM, 512, 128)
    tnd = _tile(D, 512, 128)
    tkd = _tile(D, 1024, 128)
    tnw = _tile(D, 1024, 128)
    col_gu, col_gv, col_xp = 3 * A // HEAD, (3 * A + Wd) // HEAD, (3 * A + 2 * Wd) // Wd
    col_dg, col_dp = A // HEAD, (A + Wd) // Wd

    order = _Order(x[0, :1, :1])

    def gather_near(i, names, tag=""):
        shards = [W[n][i].astype(bf16) if (i == 0 and n == "w_in") else _cast_layer(f"cast_{n}_{i}", W[n], i)
                  for n in names]
        lands = [lax.empty((N_CHIPS,) + s.shape, bf16) for s in shards]
        return names, order.start(f"ag_near_{i}{tag}", _plan_gather_near(len(names)), shards + lands)

    def gather_relay(i, near, tag=""):
        names, handle = near
        return names, order.start(f"ag_relay_{i}{tag}", _plan_gather_relay(len(names)), order.wait(handle)[len(names):])

    def gather_far(i, relayed, tag=""):
        names, handle = relayed
        return names, order.start(f"ag_far_{i}{tag}", _plan_gather_far(len(names)), order.wait(handle))

    win_count = [0]

    def gathered(far):
        names, handle = far
        g = dict(zip(names, order.wait(handle)))
        out = {}
        if "w_in" in g:
            out["w_main"], out["w_f"] = _win_assemble(f"w_in_cols_{win_count[0]}", g["w_in"], A, H)
            win_count[0] += 1
        if "w_out" in g:
            out["w_out"] = g["w_out"].reshape(D, D)
        if "w_ffn_gate" in g:
            out.update(w_gate=g["w_ffn_gate"], w_up=g["w_ffn_up"], w_down=g["w_ffn_down"].reshape(FF, D),
                       w_pg=g["w_ple_gate"].reshape(D, D), w_pp=g["w_ple_proj"])
        return out

    Wf = [None] * L
    relayed = gather_relay(0, gather_near(0, BIG[:2], "a"), "a")
    near_rest = gather_near(0, BIG[2:], "c")
    Wf[0] = gathered(gather_far(0, relayed, "a"))
    near = relayed = None

    h = x.reshape(T, D)
    pb16 = p.reshape(L, T, PL).astype(bf16)
    saved = []

    for i in range(L):
        w = Wf[i]
        sv = dict(h0=h)
        xn1 = _rms_fwd(f"rms1_{i}", h, order.follows(norm_mix[i]))
        (P,) = _matmul(f"proj_{i}", "nn", (T // tmw, NM // tn),
                       [((xn1, BS((tmw, D), lambda i, j: (i, 0))), (w["w_main"], BS((D, tn), lambda i, j: (0, j))))], [],
                       [((T, NM), f32, BS((tmw, tn), lambda i, j: (i, j)))])
        (Pf,) = _matmul(f"projf_{i}", "nn", (T // tm, 1),
                        [((xn1, BS((tm, D), lambda i, j: (i, 0))), (w["w_f"], BS((D, HEAD), lambda i, j: (0, 0))))], [],
                        [((T, HEAD), f32, BS((tm, HEAD), lambda i, j: (i, 0)))])
        fb = jnp.pad(forget_bias[i], (0, HEAD - H)).reshape(1, HEAD)
        cc, ct = _fgate_fwd(f"fgate_{i}", Pf, fb)
        c_col = ct[:H].reshape(H, T, 1)
        c_row = ct[:H].reshape(H, nb, 1, tb)
        qg, kg = q_norm[i].reshape(1, HEAD), k_norm[i].reshape(1, HEAD)
        qn, kn, vb = _qk_norm(f"qknorm_{i}", P, qg, kg, A)
        mix, o32, lse = _attn_fwd(f"attn_{i}", qn, kn, vb, c_col, c_row, tb, D)
        if i == 0:
            order.done(o32)
            relayed_rest = gather_relay(0, near_rest, "c")
            near = gather_near(1, BIG) if L > 1 else None
        gain = gmlp_v_norm[i].reshape(G, 1, HEAD)
        bs = gmlp_b_s[i].reshape(G, HEAD, 1)
        mix = _gmlp_fwd(f"gmlp_{i}", P, mix, order.follows(gain), gmlp_w_s[i], bs, col_gu, col_gv, col_dg, Wd)
        ps = pool_scale[i].reshape(1, Wd)
        mix = _pool_fwd(f"pool_{i}", P, mix, pool_w[i], ps, col_xp, col_dp, Wd)
        (h1,) = _matmul(f"out_{i}", "nn", (T // tmw, D // tnd),
                        [((mix, BS((tmw, D), lambda i, j: (i, 0))), (w["w_out"], BS((D, tnd), lambda i, j: (0, j))))],
                        [(h, BS((tmw, tnd), lambda i, j: (i, j)))],
                        [((T, D), f32, BS((tmw, tnd), lambda i, j: (i, j)))],
                        epilogue=lambda accs, ex: (accs[0] + ex[0],))
        xn2 = _rms_fwd(f"rms2_{i}", h1, norm_ffn[i])
        order.done(xn2)
        if i == 0:
            w.update(gathered(gather_far(0, relayed_rest, "c")))
        elif i + 1 < L:
            relayed = gather_relay(i + 1, near)
            near = gather_near(i + 2, BIG) if i + 2 < L else None

        def ffn_epi(accs, ex):
            g_, u_ = accs
            return g_, u_, g_ * jax.nn.sigmoid(g_) * u_

        ffo = BS((tm, FS), lambda j, i: (i, j))
        Gt, Ut, act = _matmul(f"ffn1_{i}", "nn", (N_CHIPS, T // tm),
                              [((xn2, BS((tm, D), lambda j, i: (i, 0))), (w["w_gate"], BS((None, D, FS), lambda j, i: (j, 0, 0)))),
                               ((xn2, BS((tm, D), lambda j, i: (i, 0))), (w["w_up"], BS((None, D, FS), lambda j, i: (j, 0, 0))))],
                              [], [((T, FF), bf16, ffo)] * 3, epilogue=ffn_epi, after=order.token)
        (h2,) = _matmul(f"ffn2_{i}", "nn", (T // tmw, D // tnd),
                        [((act, BS((tmw, FF), lambda i, j: (i, 0))), (w["w_down"], BS((FF, tnd), lambda i, j: (0, j))))],
                        [(h1, BS((tmw, tnd), lambda i, j: (i, j)))],
                        [((T, D), f32, BS((tmw, tnd), lambda i, j: (i, j)))],
                        epilogue=lambda accs, ex: (accs[0] + ex[0],))
        far = None
        if i == 0 and L > 1:
            order.done(h2)
            relayed = gather_relay(1, near)
            near = gather_near(2, BIG) if L > 2 else None
        elif i + 1 < L:
            order.done(h2)
            far = gather_far(i + 1, relayed)
        xn3 = _rms_fwd(f"rms3_{i}", h2, order.follows(norm_ple[i]))

        def ple_epi(accs, ex):
            gate = jax.nn.sigmoid(accs[0])
            return ex[0] + accs[1] * gate, gate, accs[1]

        dso = BS((tmw, DS), lambda i, j: (i, j))
        h3, gate, e = _matmul(f"ple_{i}", "nn", (T // tmw, N_CHIPS),
                              [((xn3, BS((tmw, D), lambda i, j: (i, 0))), (w["w_pg"], BS((D, DS), lambda i, j: (0, j)))),
                               ((pb16[i], BS((tmw, PL), lambda i, j: (i, 0))), (w["w_pp"], BS((None, PL, DS), lambda i, j: (j, 0, 0))))],
                              [(h2, dso)], [((T, D), f32, dso), ((T, D), bf16, dso), ((T, D), bf16, dso)], epilogue=ple_epi)
        sv.update(xn1=xn1, P=P, Pf=Pf, fb=fb, c_col=c_col, c_row=c_row, qn=qn, kn=kn, vb=vb, o32=o32, lse=lse,
                  mix=mix, h1=h1, xn2=xn2, Gt=Gt, Ut=Ut, act=act, h2=h2, xn3=xn3, gate=gate, e=e)
        saved.append(sv)
        h = h3
        order.done(h3)
        if i + 1 < L:
            Wf[i + 1] = gathered(far if far is not None else gather_far(i + 1, relayed))

    dh, loss_tile = _loss_grad("loss", h, loss_target.reshape(T, D))

    small_g = {n: [None] * L for n in SMALL}
    big_out = {}

    def stage_a(u):
        n_u = len(u["names"])
        lands = [lax.empty((N_CHIPS, g.shape[1] // 2, g.shape[2]), f32) for g in u["grads"]]
        u["a"] = order.start(f"rs_a_{u['tag']}", _plan_sibling_halves(n_u), u["grads"] + lands)

    def stage_pair(u):
        n_u = len(u["names"])
        out = order.wait(u["a"])
        pairs = [_pair_sum(f"rs_pair_{u['tag']}_{a}", out[a], out[n_u + a]) for a in range(n_u)]
        u["pb"], u["own"] = [t[0] for t in pairs], [t[1] for t in pairs]

    def stage_b(u):
        lands = [lax.empty(t.shape, bf16) for t in u["pb"]]
        u["b"] = order.start(f"rs_b_{u['tag']}", _plan_chip_scatter(len(u["names"])), u["pb"] + lands)

    def stage_sum(u):
        n_u = len(u["names"])
        out = order.wait(u["b"])
        u["sum"] = [_chip_sum(f"rs_sum_{u['tag']}_{a}", u["own"][a], out[n_u + a]) for a in range(n_u)]

    def stage_c(u):
        u["c"] = order.start(f"rs_c_{u['tag']}", _plan_sibling_join(len(u["names"])), u["sum"])

    def stage_adamw(u):
        for n, r in zip(u["names"], order.wait(u["c"])):
            big_out[n] = _adamw_layer(f"adamw_{n}_{u['layer']}", u["layer"], W[n], M[n], V[n],
                                      r.reshape(W[n].shape[1:]), big_out.get(n))
    dh1 = prev_f = prev_m = None
    for i in reversed(range(L)):
        w, sv = Wf[i], saved[i]
        if dh1 is not None:
            dh, _, dg = _rms_bwd(f"rms1_bw_{i + 1}", dxn1, saved[i + 1]["h0"], order.follows(norm_mix[i + 1]), dh1)
            small_g["norm_mix"][i + 1] = dg.reshape(D)
        de, dz = _ple_bwd_elem(f"ple_bw_{i}", dh, sv["gate"], sv["e"])
        (d_wpp,) = _matmul(f"d_wpp_{i}", "tn", (N_CHIPS, 1),
                           [((pb16[i], BS((T, PL), lambda i, j: (0, 0))), (de, BS((T, DS), lambda i, j: (0, i))))], [],
                           [((N_CHIPS, PL, DS), f32, BS((None, PL, DS), lambda i, j: (i, 0, 0)))])
        (d_wpg,) = _matmul(f"d_wpg_{i}", "tn", (D // tkd, D // tnd),
                           [((sv["xn3"], BS((T, tkd), lambda i, j: (0, i))), (dz, BS((T, tnd), lambda i, j: (0, j))))], [],
                           [((D, D), f32, BS((tkd, tnd), lambda i, j: (i, j)))])
        order.done(dz)
        if prev_m is not None:
            stage_pair(prev_m)
        (dxn3,) = _matmul(f"d_xn3_{i}", "nt", (T // tmw, D // tnd),
                          [((dz, BS((tmw, D), lambda i, j: (i, 0))), (w["w_pg"], BS((tnd, D), lambda i, j: (j, 0))))], [],
                          [((T, D), f32, BS((tmw, tnd), lambda i, j: (i, j)))], after=order.token)
        dh2, dh2b, dg = _rms_bwd(f"rms3_bw_{i}", dxn3, sv["h2"], norm_ple[i], dh)
        small_g["norm_ple"][i] = dg.reshape(D)

        def dffn_epi(accs, ex):
            da = accs[0]
            g_, u_ = ex[0].astype(f32), ex[1].astype(f32)
            sg = jax.nn.sigmoid(g_)
            return da * u_ * (sg * (1.0 + g_ * (1.0 - sg))), da * (g_ * sg)

        ffo = BS((tm, FS), lambda j, i: (i, j))
        dG, dU = _matmul(f"d_act_{i}", "nt", (N_CHIPS, T // tm),
                         [((dh2b, BS((tm, D), lambda j, i: (i, 0))), (w["w_down"], BS((FS, D), lambda j, i: (j, 0))))],
                         [(sv["Gt"], ffo), (sv["Ut"], ffo)], [((T, FF), bf16, ffo)] * 2, epilogue=dffn_epi)
        (d_wd,) = _matmul(f"d_wd_{i}", "tn", (N_CHIPS, D // tnw),
                          [((sv["act"], BS((T, FS), lambda i, j: (0, i))), (dh2b, BS((T, tnw), lambda i, j: (0, j))))], [],
                          [((FF, D), f32, BS((FS, tnw), lambda i, j: (i, j)))])
        gu_out = BS((None, tnd, FS), lambda j, i: (j, i, 0))
        d_wg, d_wu = _matmul(f"d_wgu_{i}", "tn", (N_CHIPS, D // tnd),
                             [((sv["xn2"], BS((T, tnd), lambda j, i: (0, i))), (dG, BS((T, FS), lambda j, i: (0, j)))),
                              ((sv["xn2"], BS((T, tnd), lambda j, i: (0, i))), (dU, BS((T, FS), lambda j, i: (0, j))))], [],
                             [((N_CHIPS, D, FS), f32, gu_out)] * 2, epilogue=lambda accs, ex: (accs[0], accs[1]))
        order.done(dG)
        unit_f = dict(tag=f"{i}f", layer=i, names=["w_ffn_gate", "w_ffn_up", "w_ffn_down", "w_ple_gate", "w_ple_proj"],
                      grads=[d_wg, d_wu, d_wd.reshape(N_CHIPS, FS, D), d_wpg.reshape(N_CHIPS, DS, D), d_wpp])
        stage_a(unit_f)
        if prev_f is not None:
            stage_sum(prev_f)
            stage_c(prev_f)
        if prev_m is not None:
            stage_b(prev_m)
        tm2 = _tile(T, 512, 16)
        (dxn2,) = _matmul(f"d_xn2_{i}", "nt", (D // tnd, T // tm2),
                          [((dG, BS((tm2, FF), lambda j, i: (i, 0))), (w["w_gate"], BS((N_CHIPS, tnd, FS), lambda j, i: (0, j, 0)))),
                           ((dU, BS((tm2, FF), lambda j, i: (i, 0))), (w["w_up"], BS((N_CHIPS, tnd, FS), lambda j, i: (0, j, 0))))], [],
                          [((T, D), f32, BS((tm2, tnd), lambda j, i: (i, j)))], after=order.token)
        dh1, dh1b, dg = _rms_bwd(f"rms2_bw_{i}", dxn2, sv["h1"], norm_ffn[i], dh2)
        small_g["norm_ffn"][i] = dg.reshape(D)
        (dmix,) = _matmul(f"d_mix_{i}", "nt", (T // tmw, D // tnd),
                          [((dh1b, BS((tmw, D), lambda i, j: (i, 0))), (w["w_out"], BS((tnd, D), lambda i, j: (j, 0))))], [],
                          [((T, D), f32, BS((tmw, tnd), lambda i, j: (i, j)))])
        (d_wout,) = _matmul(f"d_wout_{i}", "tn", (D // tkd, D // tnd),
                            [((sv["mix"], BS((T, tkd), lambda i, j: (0, i))), (dh1b, BS((T, tnd), lambda i, j: (0, j))))], [],
                            [((D, D), f32, BS((tkd, tnd), lambda i, j: (i, j)))])
        order.done(dmix)
        stage_pair(unit_f)
        stage_b(unit_f)
        if prev_f is not None:
            stage_adamw(prev_f)
        qg, kg = q_norm[i].reshape(1, HEAD), k_norm[i].reshape(1, HEAD)
        dq, dk, dv, dc_row, dqg, dkg = _attn_bwd(f"attn_bw_{i}", sv["qn"], sv["kn"], sv["vb"], sv["o32"], dmix,
                                                 sv["lse"], sv["c_col"], sv["c_row"], sv["P"], order.follows(qg), kg, tb)
        small_g["q_norm"][i] = dqg.reshape(HEAD)
        small_g["k_norm"][i] = dkg.reshape(HEAD)
        dct = jnp.pad(dc_row.reshape(H, T), ((0, HEAD - H), (0, 0)))
        dPf, dfb = _fgate_bwd(f"fgate_bw_{i}", dct, sv["Pf"], sv["fb"])
        small_g["forget_bias"][i] = dfb[0, :H]
        gain = gmlp_v_norm[i].reshape(G, 1, HEAD)
        bs = gmlp_b_s[i].reshape(G, HEAD, 1)
        dgu, dgv, dws, dbs, dgain = _gmlp_bwd(f"gmlp_bw_{i}", sv["P"], dmix, gain, gmlp_w_s[i], bs, col_gu, col_gv,
                                              col_dg, Wd)
        small_g["gmlp_w_s"][i] = dws
        small_g["gmlp_b_s"][i] = dbs.reshape(G, HEAD)
        small_g["gmlp_v_norm"][i] = dgain.reshape(G, HEAD)
        ps = pool_scale[i].reshape(1, Wd)
        dxp, dpw, dps = _pool_bwd(f"pool_bw_{i}", sv["P"], dmix, pool_w[i], ps, col_xp, col_dp, Wd)
        small_g["pool_w"][i] = dpw
        small_g["pool_scale"][i] = dps.reshape(Wd)
        dP = jnp.concatenate([dq, dk, dv, dgu, dgv, dxp], axis=1)
        (d_wmain,) = _matmul(f"d_wmain_{i}", "tn", (D // tkd, NM // tn),
                             [((sv["xn1"], BS((T, tkd), lambda i, j: (0, i))), (dP, BS((T, tn), lambda i, j: (0, j))))], [],
                             [((D, NM), f32, BS((tkd, tn), lambda i, j: (i, j)))])
        (d_wf,) = _matmul(f"d_wf_{i}", "tn", (D // tnd, 1),
                          [((sv["xn1"], BS((T, tnd), lambda i, j: (0, i))), (dPf, BS((T, HEAD), lambda i, j: (0, 0))))], [],
                          [((D, HEAD), f32, BS((tnd, HEAD), lambda i, j: (i, 0)))])
        d_win4 = _dwin_split(f"d_win_cols_{i}", d_wmain, d_wf, A, H)
        order.done(dP)
        unit_m = dict(tag=f"{i}m", layer=i, names=["w_in", "w_out"], grads=[d_win4, d_wout.reshape(N_CHIPS, DS, D)])
        stage_a(unit_m)
        if prev_m is not None:
            stage_sum(prev_m)
            stage_c(prev_m)
        (dxn1,) = _matmul(f"d_xn1_{i}", "nt", (D // tnd, T // tm),
                          [((dP, BS((tm, NM), lambda j, i: (i, 0))), (w["w_main"], BS((tnd, NM), lambda j, i: (j, 0)))),
                           ((dPf, BS((tm, HEAD), lambda j, i: (i, 0))), (w["w_f"], BS((tnd, HEAD), lambda j, i: (j, 0))))], [],
                          [((T, D), f32, BS((tm, tnd), lambda j, i: (i, j)))], after=order.token)
        order.done(dxn1)
        if prev_m is not None:
            stage_adamw(prev_m)
        prev_f, prev_m = unit_f, unit_m

    dh, _, dg = _rms_bwd("rms1_bw_0", dxn1, saved[0]["h0"], order.follows(norm_mix[0]), dh1)
    small_g["norm_mix"][0] = dg.reshape(D)
    order.done(dh)

    small_full = {n: jnp.stack(small_g[n]) for n in SMALL}
    small_shapes = [W[n].shape for n in SMALL]
    n_rows = sum(_rows_of(W[n].size) for n in SMALL) + _rows_of(1)
    rows8 = -(-n_rows // 64) * 8
    packed = order.follows(_pack_rows([small_full[n] for n in SMALL] + [loss_tile[0, :1]], N_DEV * rows8))
    summed = _all_reduce_small("allreduce_small", packed.reshape(N_DEV, rows8, 128)).reshape(-1, 128)
    order.done(summed)

    stage_sum(prev_f)
    stage_c(prev_f)
    stage_pair(prev_m)
    stage_b(prev_m)
    stage_adamw(prev_f)
    stage_sum(prev_m)
    stage_c(prev_m)
    stage_adamw(prev_m)
    *small_grads, loss_row = _unpack_rows(summed, small_shapes + [(1,)])
    grads = dict(zip(SMALL, small_grads))
    loss = loss_row[0]

    wp, mp, vp = (_pack_rows([t[n] for n in SMALL], N_DEV * rows8) for t in (W, M, V))
    delta, new_m, new_v = (dict(zip(SMALL, _unpack_rows(t, small_shapes)))
                           for t in _adamw("adamw_small", wp, summed, mp, vp))
    for n in BIG:
        grads[n], delta[n], new_m[n], new_v[n] = big_out[n]

    return (loss, dh.reshape(1, T, D), *[grads[n] for n in WEIGHTS], *[delta[n] for n in WEIGHTS],
            *[new_m[n] for n in WEIGHTS], *[new_v[n] for n in WEIGHTS])
```
